```python
import jax, jax.numpy as jnp
from jax import lax
import numpy as np

D_MODEL = 1024
BATCH = 4
SEQ = 8192
DEPTH = 1

CHUNK = 64
N_MEM = 256
HEAD_DIM = D_MODEL // 16
A_HEADS = 8
B_HEADS = 4
C_HEADS = 4
A_WIDTH = A_HEADS * HEAD_DIM
B_WIDTH = B_HEADS * HEAD_DIM
C_WIDTH = C_HEADS * HEAD_DIM
MIX_WIDTH = A_WIDTH + B_WIDTH + C_WIDTH
IN_COLS = 3 * A_WIDTH + 4 * B_WIDTH + C_WIDTH
LEFT_CHUNKS = 8
BAND_CHUNKS = LEFT_CHUNKS + 1
MAX_REL_DIST = 128
REL_TABLE = MAX_REL_DIST + CHUNK
ROPE_BASE = 10000.0
N_GROUPS = 4
EXPERTS_PER_GROUP = 8
N_EXPERTS = N_GROUPS * EXPERTS_PER_GROUP
TOP_K = 2
D_EXPERT = D_MODEL // 2
ROW_BLOCK = 256
EPS = 1e-6
NEG_INF = -1e30

kernel_name = "hybrid_chunk_attn_retention_memxattn_hiermoe"


def rms_norm(x, g):
    xf = x.astype(jnp.float32)
    y = xf * lax.rsqrt(jnp.mean(xf * xf, axis=-1, keepdims=True) + EPS)
    return (y * g.astype(jnp.float32)).astype(x.dtype)


def split_heads(t, n_heads):
    b, s, _ = t.shape
    return t.reshape(b, s, n_heads, -1).transpose(0, 2, 1, 3)


def merge_heads(t):
    b, h, s, d = t.shape
    return t.transpose(0, 2, 1, 3).reshape(b, s, h * d)


def chunked_relpos_attention(q, k, v, rel_bias):
    b, h, s, dh = q.shape
    nc = s // CHUNK
    qc = q.reshape(b, h, nc, CHUNK, dh)
    pad = ((0, 0), (0, 0), (LEFT_CHUNKS * CHUNK, 0), (0, 0))
    kp = jnp.pad(k, pad).reshape(b, h, nc + LEFT_CHUNKS, CHUNK, dh)
    vp = jnp.pad(v, pad).reshape(b, h, nc + LEFT_CHUNKS, CHUNK, dh)
    band_idx = jnp.arange(nc)[:, None] + jnp.arange(BAND_CHUNKS)[None, :]
    kb = kp[:, :, band_idx].reshape(b, h, nc, BAND_CHUNKS * CHUNK, dh)
    vb = vp[:, :, band_idx].reshape(b, h, nc, BAND_CHUNKS * CHUNK, dh)
    scores = jnp.einsum('bhcqd,bhckd->bhcqk', qc, kb).astype(jnp.float32) * (dh ** -0.5)
    qpos = LEFT_CHUNKS * CHUNK + jnp.arange(CHUNK)
    kpos = jnp.arange(BAND_CHUNKS * CHUNK)
    rel = jnp.clip(qpos[:, None] - kpos[None, :], -(CHUNK - 1), MAX_REL_DIST) + (CHUNK - 1)
    bias = rel_bias[:, rel].astype(jnp.float32)
    valid = jnp.repeat(band_idx >= LEFT_CHUNKS, CHUNK, axis=1)
    scores = scores + bias[None, :, None]
    scores = jnp.where(valid[None, None, :, None, :], scores, NEG_INF)
    probs = jax.nn.softmax(scores, axis=-1).astype(v.dtype)
    out = jnp.einsum('bhcqk,bhckd->bhcqd', probs, vb)
    return out.reshape(b, h, s, dh)


def rotary(t, positions):
    half = t.shape[-1] // 2
    inv = ROPE_BASE ** (-jnp.arange(half, dtype=jnp.float32) / half)
    ang = positions.astype(jnp.float32)[:, None, :, None] * inv
    cos, sin = jnp.cos(ang), jnp.sin(ang)
    t1, t2 = t[..., :half], t[..., half:]
    return jnp.concatenate([t1 * cos - t2 * sin, t1 * sin + t2 * cos], axis=-1)


def chunkwise_retention(q, k, v):
    b, h, s, dh = q.shape
    nc = s // CHUNK
    log_g = jnp.log(1.0 - jnp.exp2(-5.0 - jnp.arange(h, dtype=jnp.float32)))
    idx = jnp.arange(CHUNK, dtype=jnp.float32)
    diff = idx[:, None] - idx[None, :]
    decay = jnp.where(diff >= 0, jnp.exp(log_g[:, None, None] * jnp.maximum(diff, 0.0)), 0.0)
    qc = q.reshape(b, h, nc, CHUNK, dh)
    kc = k.reshape(b, h, nc, CHUNK, dh)
    vc = v.reshape(b, h, nc, CHUNK, dh)
    inner = jnp.einsum('bhcqd,bhckd->bhcqk', qc, kc) * decay[None, :, None]
    inner_out = jnp.einsum('bhcqk,bhckd->bhcqd', inner, vc)
    zeta = jnp.exp(log_g[:, None] * (CHUNK - 1 - idx))
    kv = jnp.einsum('bhckd,hk,bhcke->bhcde', kc, zeta, vc)
    chunk_decay = jnp.exp(log_g * CHUNK)[None, :, None, None]

    def step(state, kv_c):
        return chunk_decay * state + kv_c, state

    _, states_prev = lax.scan(step, jnp.zeros((b, h, dh, dh), jnp.float32), jnp.moveaxis(kv, 2, 0))
    states_prev = jnp.moveaxis(states_prev, 0, 2)
    xi = jnp.exp(log_g[:, None] * (idx + 1.0))
    cross = jnp.einsum('bhcqd,bhcde->bhcqe', qc, states_prev) * xi[None, :, None, :, None]
    return (inner_out + cross).reshape(b, h, s, dh)


def head_group_norm(o, g):
    mu = jnp.mean(o, axis=-1, keepdims=True)
    var = jnp.mean(jnp.square(o - mu), axis=-1, keepdims=True)
    y = (o - mu) * lax.rsqrt(var + EPS)
    return merge_heads(y) * g.astype(jnp.float32)


def memory_cross_attention(q, mem_n, w_mem_kv, qn_c, kn_c):
    kv = mem_n @ w_mem_kv
    k, v = jnp.split(kv, 2, axis=-1)
    qh = rms_norm(split_heads(q, C_HEADS), qn_c)
    kh = rms_norm(split_heads(k, C_HEADS), kn_c)
    vh = split_heads(v, C_HEADS)
    scores = jnp.einsum('bhsd,bhmd->bhsm', qh, kh).astype(jnp.float32) * (HEAD_DIM ** -0.5)
    probs = jax.nn.softmax(scores, axis=-1).astype(vh.dtype)
    return merge_heads(jnp.einsum('bhsm,bhmd->bhsd', probs, vh))


def hierarchical_moe(hf, w_router_group, b_router_group, w_router_expert, b_router_expert,
                     w_gate, w_up, w_down):
    t, d = hf.shape
    g_prob = jax.nn.softmax((hf @ w_router_group + b_router_group).astype(jnp.float32), axis=-1)
    g_sel = jnp.argmax(g_prob, axis=-1)
    p_group = jnp.take_along_axis(g_prob, g_sel[:, None], axis=-1)[:, 0]
    e_logits = (hf @ w_router_expert + b_router_expert).astype(jnp.float32)
    e_logits = e_logits.reshape(t, N_GROUPS, EXPERTS_PER_GROUP)
    e_in_group = jnp.take_along_axis(e_logits, g_sel[:, None, None], axis=1)[:, 0]
    e_prob = jax.nn.softmax(e_in_group, axis=-1)
    top_p, top_i = lax.top_k(e_prob, TOP_K)
    top_p = top_p / jnp.sum(top_p, axis=-1, keepdims=True)
    weights = p_group[:, None] * top_p
    expert_id = g_sel[:, None] * EXPERTS_PER_GROUP + top_i

    n_assign = t * TOP_K
    flat_e = expert_id.reshape(-1).astype(jnp.int32)
    flat_t = jnp.repeat(jnp.arange(t, dtype=jnp.int32), TOP_K)
    flat_w = weights.reshape(-1)
    order = jnp.argsort(flat_e)
    e_s, t_s, w_s = flat_e[order], flat_t[order], flat_w[order]
    counts = jnp.bincount(flat_e, length=N_EXPERTS)
    starts = jnp.cumsum(counts) - counts
    padded = (counts + ROW_BLOCK - 1) // ROW_BLOCK * ROW_BLOCK
    pends = jnp.cumsum(padded)
    pstarts = pends - padded
    dest = pstarts[e_s] + (jnp.arange(n_assign) - starts[e_s])
    n_blocks = -(-n_assign // ROW_BLOCK) + N_EXPERTS
    n_rows = n_blocks * ROW_BLOCK
    row_tok = jnp.zeros((n_rows,), jnp.int32).at[dest].set(t_s)
    row_w = jnp.zeros((n_rows,), hf.dtype).at[dest].set(w_s.astype(hf.dtype))
    block_e = jnp.minimum(jnp.searchsorted(pends, jnp.arange(n_blocks) * ROW_BLOCK, side='right'),
                          N_EXPERTS - 1)
    xs = hf[row_tok].reshape(n_blocks, ROW_BLOCK, d)

    def expert_block(args):
        xb, e = args
        return (jax.nn.silu(xb @ w_gate[e]) * (xb @ w_up[e])) @ w_down[e]

    ys = lax.map(expert_block, (xs, block_e)).reshape(n_rows, d)
    return jnp.zeros((t, d), hf.dtype).at[row_tok].add(ys * row_w[:, None])


def setup_inputs(seed: int = 0) -> dict:
    key = jax.random.key(seed)
    ks = jax.random.split(key, 24)
    f32 = jnp.float32
    nrm = lambda k, shape, scale: jax.random.normal(k, shape, f32) * scale
    gain = lambda k, shape: 1.0 + 0.02 * jax.random.normal(k, shape, f32)
    return {
        "x": nrm(ks[0], (BATCH, SEQ, D_MODEL), 1.0),
        "mem": nrm(ks[1], (BATCH, N_MEM, D_MODEL), 1.0),
        "positions": jnp.broadcast_to(jnp.arange(SEQ, dtype=jnp.int32), (BATCH, SEQ)),
        "mix_norm_g": gain(ks[2], (D_MODEL,)),
        "w_in": nrm(ks[3], (D_MODEL, IN_COLS), D_MODEL ** -0.5),
        "qn_a": gain(ks[4], (HEAD_DIM,)),
        "kn_a": gain(ks[5], (HEAD_DIM,)),
        "rel_bias": nrm(ks[6], (A_HEADS, REL_TABLE), 0.2),
        "ret_gn_g": gain(ks[7], (B_WIDTH,)),
        "mem_norm_g": gain(ks[8], (D_MODEL,)),
        "w_mem_kv": nrm(ks[9], (D_MODEL, 2 * C_WIDTH), D_MODEL ** -0.5),
        "qn_c": gain(ks[10], (HEAD_DIM,)),
        "kn_c": gain(ks[11], (HEAD_DIM,)),
        "w_out": nrm(ks[12], (MIX_WIDTH, D_MODEL), MIX_WIDTH ** -0.5),
        "ffn_norm_g": gain(ks[13], (D_MODEL,)),
        "w_router_group": nrm(ks[14], (D_MODEL, N_GROUPS), D_MODEL ** -0.5),
        "b_router_group": nrm(ks[15], (N_GROUPS,), 0.01),
        "w_router_expert": nrm(ks[16], (D_MODEL, N_EXPERTS), D_MODEL ** -0.5),
        "b_router_expert": nrm(ks[17], (N_EXPERTS,), 0.01),
        "w_gate": nrm(ks[18], (N_EXPERTS, D_MODEL, D_EXPERT), D_MODEL ** -0.5),
        "w_up": nrm(ks[19], (N_EXPERTS, D_MODEL, D_EXPERT), D_MODEL ** -0.5),
        "w_down": nrm(ks[20], (N_EXPERTS, D_EXPERT, D_MODEL), D_EXPERT ** -0.5),
    }


def reference(x, mem, positions, mix_norm_g, w_in, qn_a, kn_a, rel_bias, ret_gn_g, mem_norm_g,
              w_mem_kv, qn_c, kn_c, w_out, ffn_norm_g, w_router_group, b_router_group,
              w_router_expert, b_router_expert, w_gate, w_up, w_down):
    b, s, d = x.shape
    splits = [int(v) for v in np.cumsum([A_WIDTH, A_WIDTH, A_WIDTH, B_WIDTH, B_WIDTH, B_WIDTH, B_WIDTH])]
    mem_n = rms_norm(mem, mem_norm_g)
    h = x
    for _ in range(DEPTH):
        xn = rms_norm(h, mix_norm_g)
        proj = xn @ w_in
        qa, ka, va, qb, kb, vb, gb, qc = jnp.split(proj, splits, axis=-1)
        qah = rms_norm(split_heads(qa, A_HEADS), qn_a)
        kah = rms_norm(split_heads(ka, A_HEADS), kn_a)
        out_a = merge_heads(chunked_relpos_attention(qah, kah, split_heads(va, A_HEADS), rel_bias))
        qbh = rotary(split_heads(qb, B_HEADS).astype(jnp.float32), positions)
        kbh = rotary(split_heads(kb, B_HEADS).astype(jnp.float32), positions) * (HEAD_DIM ** -0.5)
        vbh = split_heads(vb, B_HEADS).astype(jnp.float32)
        ret = head_group_norm(chunkwise_retention(qbh, kbh, vbh), ret_gn_g)
        out_b = (jax.nn.silu(gb.astype(jnp.float32)) * ret).astype(h.dtype)
        out_c = memory_cross_attention(qc, mem_n, w_mem_kv, qn_c, kn_c)
        h = h + jnp.concatenate([out_a, out_b, out_c], axis=-1) @ w_out
        hn = rms_norm(h, ffn_norm_g).reshape(b * s, d)
        h = h + hierarchical_moe(hn, w_router_group, b_router_group, w_router_expert,
                                 b_router_expert, w_gate, w_up, w_down).reshape(b, s, d)
    return h
```

```python
import functools

import jax
import jax.numpy as jnp
from jax import lax
from jax.experimental import pallas as pl
from jax.experimental.pallas import tpu as pltpu

D_MODEL = 1024
CHUNK = 64
HEAD_DIM = 64
A_HEADS = 8
B_HEADS = 4
C_HEADS = 4
A_WIDTH = A_HEADS * HEAD_DIM
B_WIDTH = B_HEADS * HEAD_DIM
C_WIDTH = C_HEADS * HEAD_DIM
IN_COLS = 3 * A_WIDTH + 4 * B_WIDTH + C_WIDTH
LEFT_CHUNKS = 8
BAND_CHUNKS = LEFT_CHUNKS + 1
MAX_REL_DIST = 128
ROPE_BASE = 10000.0
N_GROUPS = 4
EXPERTS_PER_GROUP = 8
N_EXPERTS = N_GROUPS * EXPERTS_PER_GROUP
D_EXPERT = D_MODEL // 2
EPS = 1e-6
NEG_INF = -1e30

LANES = 128
Q_SUB = 4 * CHUNK
BAND = Q_SUB + LEFT_CHUNKS * CHUNK
LEFT_ROWS = LEFT_CHUNKS * CHUNK
RET_CHUNK = 256
ROW_BLOCK = 256
ROUTE_LANE0 = N_GROUPS
VMEM_LIMIT = 48 * 1024 * 1024

_F32 = jnp.float32
_BF16 = jnp.bfloat16


def _cparams(n_axes):
    return pltpu.CompilerParams(dimension_semantics=("arbitrary",) * n_axes,
                                vmem_limit_bytes=VMEM_LIMIT)


def _dot(a, b):
    return jnp.dot(a, b, preferred_element_type=_F32)


def _dot_nt(a, b):
    return lax.dot_general(a, b, (((1,), (1,)), ((), ())), preferred_element_type=_F32)


def _lane(shape):
    return lax.broadcasted_iota(jnp.int32, shape, len(shape) - 1)


def _pair_rms(t, gain):
    low = _lane(t.shape) < HEAD_DIM
    t2 = t * t
    ms0 = jnp.sum(jnp.where(low, t2, 0.0), axis=-1, keepdims=True) * (1.0 / HEAD_DIM)
    ms1 = jnp.sum(jnp.where(low, 0.0, t2), axis=-1, keepdims=True) * (1.0 / HEAD_DIM)
    r = jnp.where(low, lax.rsqrt(ms0 + EPS), lax.rsqrt(ms1 + EPS))
    return (t * r) * gain


def _rope_kernel(pos_ref, inv_ref, sign_ref, cos_ref, sin_ref):
    ang = pos_ref[...].astype(_F32) * inv_ref[...]
    cos_ref[...] = jnp.cos(ang)
    sin_ref[...] = jnp.sin(ang) * sign_ref[...]


def _rope_tables(positions):
    t = positions.size
    half = HEAD_DIM // 2
    inv = ROPE_BASE ** (-jnp.arange(half, dtype=_F32) / half)
    inv128 = jnp.tile(inv, LANES // half).reshape(1, LANES)
    sign = jnp.where((jnp.arange(LANES) % HEAD_DIM) < half, -1.0, 1.0).astype(_F32).reshape(1, LANES)
    pos = jnp.broadcast_to(positions.reshape(t, 1), (t, LANES))
    tm = min(t, 1024)
    row = pl.BlockSpec((tm, LANES), lambda i: (i, 0))
    one = pl.BlockSpec((1, LANES), lambda i: (0, 0))
    return pl.pallas_call(
        _rope_kernel,
        grid=(t // tm,),
        in_specs=[row, one, one],
        out_specs=[row, row],
        out_shape=[jax.ShapeDtypeStruct((t, LANES), _F32)] * 2,
        compiler_params=_cparams(1),
        name="rope_tables",
    )(pos, inv128, sign)


def _mem_kv_kernel(mem_ref, g_ref, w_ref, kn_ref, k_ref, v_ref):
    m = mem_ref[...]
    ms = jnp.mean(m * m, axis=-1, keepdims=True)
    mn = (m * lax.rsqrt(ms + EPS)) * g_ref[...]
    kv = _dot(mn.astype(_BF16), w_ref[...])
    for j in range(C_WIDTH // LANES):
        sl = slice(j * LANES, (j + 1) * LANES)
        k_ref[:, sl] = _pair_rms(kv[:, sl], kn_ref[...]).astype(_BF16)
        v_ref[:, sl] = kv[:, C_WIDTH + j * LANES:C_WIDTH + (j + 1) * LANES].astype(_BF16)


def _mem_kv(mem, mem_norm_g, w_mem_kv, kn_c):
    b, m, d = mem.shape
    kn = jnp.tile(kn_c, 2).reshape(1, LANES)
    out = pl.BlockSpec((None, m, C_WIDTH), lambda i: (i, 0, 0))
    return pl.pallas_call(
        _mem_kv_kernel,
        grid=(b,),
        in_specs=[pl.BlockSpec((None, m, d), lambda i: (i, 0, 0)),
                  pl.BlockSpec((1, d), lambda i: (0, 0)),
                  pl.BlockSpec((d, 2 * C_WIDTH), lambda i: (0, 0)),
                  pl.BlockSpec((1, LANES), lambda i: (0, 0))],
        out_specs=[out, out],
        out_shape=[jax.ShapeDtypeStruct((b, m, C_WIDTH), _BF16)] * 2,
        compiler_params=_cparams(1),
        name="mem_kv",
    )(mem, mem_norm_g.reshape(1, d), w_mem_kv.astype(_BF16), kn)


def _in_proj_kernel(x_ref, g_ref, w_ref, o_ref):
    x = x_ref[...]
    ms = jnp.mean(x * x, axis=-1, keepdims=True)
    xn = ((x * lax.rsqrt(ms + EPS)) * g_ref[...]).astype(_BF16)
    o_ref[...] = _dot(xn, w_ref[...])


def _in_proj(x2, g, w_in):
    t, d = x2.shape
    tm = min(t, 512)
    return pl.pallas_call(
        _in_proj_kernel,
        grid=(t // tm,),
        in_specs=[pl.BlockSpec((tm, d), lambda i: (i, 0)),
                  pl.BlockSpec((1, d), lambda i: (0, 0)),
                  pl.BlockSpec((d, IN_COLS), lambda i: (0, 0))],
        out_specs=pl.BlockSpec((tm, IN_COLS), lambda i: (i, 0)),
        out_shape=jax.ShapeDtypeStruct((t, IN_COLS), _F32),
        compiler_params=_cparams(1),
        name="in_proj",
    )(x2, g.reshape(1, d), w_in.astype(_BF16))


def _attn_kernel(q_ref, k_ref, v_ref, qn_ref, kn_ref, bias_ref, o_ref, kp_ref, vp_ref, *, q_rows):
    qs = pl.program_id(2)
    s = k_ref.shape[0]
    norm_rows = min(s, 512)

    @pl.when(qs == 0)
    def _():
        kp_ref[0:LEFT_ROWS, :] = jnp.zeros((LEFT_ROWS, LANES), _BF16)
        vp_ref[0:LEFT_ROWS, :] = jnp.zeros((LEFT_ROWS, LANES), _BF16)

        def fill(i, carry):
            r = pl.multiple_of(i * norm_rows, norm_rows)
            kp_ref[pl.ds(LEFT_ROWS + r, norm_rows), :] = _pair_rms(
                k_ref[pl.ds(r, norm_rows), :], kn_ref[...]).astype(_BF16)
            vp_ref[pl.ds(LEFT_ROWS + r, norm_rows), :] = v_ref[pl.ds(r, norm_rows), :].astype(_BF16)
            return carry

        lax.fori_loop(0, s // norm_rows, fill, 0)

    low = _lane((Q_SUB, LANES)) < HEAD_DIM
    col = _lane((Q_SUB, BAND))

    def sub(j, carry):
        r = pl.multiple_of(j * Q_SUB, Q_SUB)
        row0 = pl.multiple_of(qs * q_rows + r, Q_SUB)
        qn = _pair_rms(q_ref[pl.ds(r, Q_SUB), :], qn_ref[...])
        kb = kp_ref[pl.ds(row0, BAND), :]
        vb = vp_ref[pl.ds(row0, BAND), :]
        exists = col >= LEFT_ROWS - row0
        outs = []
        for h in range(2):
            qh = jnp.where(low if h == 0 else ~low, qn, 0.0).astype(_BF16)
            sc = _dot_nt(qh, kb) + bias_ref[h]
            sc = jnp.where(exists, sc, NEG_INF)
            m = jnp.max(sc, axis=-1, keepdims=True)
            p = jnp.exp(sc - m)
            probs = p / jnp.sum(p, axis=-1, keepdims=True)
            outs.append(_dot(probs.astype(_BF16), vb))
        o_ref[pl.ds(r, Q_SUB), :] = jnp.where(low, outs[0], outs[1]).astype(o_ref.dtype)
        return carry

    lax.fori_loop(0, q_rows // Q_SUB, sub, 0)


def _band_bias(rel_bias):
    qpos = LEFT_ROWS + jnp.arange(CHUNK)
    kpos = jnp.arange(BAND_CHUNKS * CHUNK)
    rel = jnp.clip(qpos[:, None] - kpos[None, :], -(CHUNK - 1), MAX_REL_DIST) + (CHUNK - 1)
    bias = rel_bias[:, rel].astype(_F32)
    full = jnp.full((rel_bias.shape[0], Q_SUB, BAND), NEG_INF, _F32)
    for a in range(Q_SUB // CHUNK):
        full = lax.dynamic_update_slice(full, bias, (0, a * CHUNK, a * CHUNK))
    return full


def _attention(proj3, qn_a, kn_a, rel_bias):
    b, s, _ = proj3.shape
    q_rows = min(s, 1024)
    qn = (jnp.tile(qn_a, 2) * (HEAD_DIM ** -0.5)).reshape(1, LANES)
    kn = jnp.tile(kn_a, 2).reshape(1, LANES)
    pairs = A_HEADS // 2
    kcol, vcol = A_WIDTH // LANES, 2 * A_WIDTH // LANES
    return pl.pallas_call(
        functools.partial(_attn_kernel, q_rows=q_rows),
        grid=(b, pairs, s // q_rows),
        in_specs=[pl.BlockSpec((None, q_rows, LANES), lambda i, p, j: (i, j, p)),
                  pl.BlockSpec((None, s, LANES), lambda i, p, j: (i, 0, kcol + p)),
                  pl.BlockSpec((None, s, LANES), lambda i, p, j: (i, 0, vcol + p)),
                  pl.BlockSpec((1, LANES), lambda i, p, j: (0, 0)),
                  pl.BlockSpec((1, LANES), lambda i, p, j: (0, 0)),
                  pl.BlockSpec((2, Q_SUB, BAND), lambda i, p, j: (p, 0, 0))],
        out_specs=pl.BlockSpec((None, q_rows, LANES), lambda i, p, j: (i, j, p)),
        out_shape=jax.ShapeDtypeStruct((b, s, A_WIDTH), _BF16),
        scratch_shapes=[pltpu.VMEM((s + LEFT_ROWS, LANES), _BF16),
                        pltpu.VMEM((s + LEFT_ROWS, LANES), _BF16)],
        compiler_params=_cparams(3),
        name="attn_a",
    )(proj3, proj3, proj3, qn, kn, _band_bias(rel_bias))


def _swap_halves(t):
    first = (_lane(t.shape) % HEAD_DIM) < (HEAD_DIM // 2)
    return jnp.where(first, pltpu.roll(t, LANES - HEAD_DIM // 2, 1), pltpu.roll(t, HEAD_DIM // 2, 1))


def _retention_kernel(q_ref, k_ref, v_ref, gate_ref, cos_ref, sin_ref, decay_ref, zeta_ref, xi_ref,
                      cd_ref, gn_ref, o_ref, state_ref, *, rows):
    @pl.when(pl.program_id(2) == 0)
    def _():
        state_ref[...] = jnp.zeros_like(state_ref)

    c = RET_CHUNK
    low = _lane((c, LANES)) < HEAD_DIM
    srow = lax.broadcasted_iota(jnp.int32, (LANES, LANES), 0) < HEAD_DIM
    scol = _lane((LANES, LANES)) < HEAD_DIM
    same_head = srow == scol

    for j in range(rows // c):
        sl = slice(j * c, (j + 1) * c)
        cos, sin = cos_ref[sl, :], sin_ref[sl, :]
        q = q_ref[sl, :]
        k = k_ref[sl, :]
        qr = q * cos + _swap_halves(q) * sin
        kr = (k * cos + _swap_halves(k) * sin) * (HEAD_DIM ** -0.5)
        vb = v_ref[sl, :].astype(_BF16)
        qb = qr.astype(_BF16)
        kb = kr.astype(_BF16)
        inner_out = []
        for h in range(2):
            qh = jnp.where(low if h == 0 else ~low, qr, 0.0).astype(_BF16)
            inner = _dot_nt(qh, kb) * decay_ref[h]
            inner_out.append(_dot(inner.astype(_BF16), vb))
        state = state_ref[...]
        cross = _dot(qb, state.astype(_BF16)) * xi_ref[...]
        o = jnp.where(low, inner_out[0], inner_out[1]) + cross
        kz = (kr * zeta_ref[...]).T.astype(_BF16)
        state_ref[...] = cd_ref[...] * state + jnp.where(same_head, _dot(kz, vb), 0.0)
        mu = jnp.where(low,
                       jnp.sum(jnp.where(low, o, 0.0), axis=-1, keepdims=True),
                       jnp.sum(jnp.where(low, 0.0, o), axis=-1, keepdims=True)) * (1.0 / HEAD_DIM)
        dlt = o - mu
        d2 = dlt * dlt
        var = jnp.where(low,
                        jnp.sum(jnp.where(low, d2, 0.0), axis=-1, keepdims=True),
                        jnp.sum(jnp.where(low, 0.0, d2), axis=-1, keepdims=True)) * (1.0 / HEAD_DIM)
        y = (dlt * lax.rsqrt(var + EPS)) * gn_ref[...]
        g = gate_ref[sl, :]
        o_ref[sl, :] = ((g * jax.nn.sigmoid(g)) * y).astype(o_ref.dtype)


def _retention_tables():
    c = RET_CHUNK
    log_g = jnp.log(1.0 - jnp.exp2(-5.0 - jnp.arange(B_HEADS, dtype=_F32)))
    idx = jnp.arange(c, dtype=_F32)
    diff = idx[:, None] - idx[None, :]
    decay = jnp.where(diff >= 0, jnp.exp(log_g[:, None, None] * jnp.maximum(diff, 0.0)), 0.0)
    zeta = jnp.exp(log_g[:, None] * (c - 1 - idx))
    xi = jnp.exp(log_g[:, None] * (idx + 1.0))
    cd = jnp.exp(log_g * c)

    def lanes(tab):
        return jnp.repeat(tab.reshape(B_HEADS // 2, 2, c), HEAD_DIM, axis=1).transpose(0, 2, 1)

    cdm = jnp.repeat(cd.reshape(B_HEADS // 2, 2), HEAD_DIM, axis=1)
    cdm = jnp.broadcast_to(cdm[:, :, None], (B_HEADS // 2, LANES, LANES))
    return decay, lanes(zeta), lanes(xi), cdm


def _retention(proj3, cos, sin, ret_gn_g):
    b, s, _ = proj3.shape
    rows = min(s, 1024)
    pairs = B_HEADS // 2
    base = 3 * A_WIDTH // LANES
    decay, zeta, xi, cdm = _retention_tables()
    cos3, sin3 = cos.reshape(b, s, LANES), sin.reshape(b, s, LANES)
    gn = ret_gn_g.reshape(pairs, 1, LANES)

    def col(off):
        return pl.BlockSpec((None, rows, LANES), lambda i, p, j: (i, j, base + off * pairs + p))

    tab = pl.BlockSpec((None, rows, LANES), lambda i, p, j: (i, j, 0))
    return pl.pallas_call(
        functools.partial(_retention_kernel, rows=rows),
        grid=(b, pairs, s // rows),
        in_specs=[col(0), col(1), col(2), col(3), tab, tab,
                  pl.BlockSpec((2, RET_CHUNK, RET_CHUNK), lambda i, p, j: (p, 0, 0)),
                  pl.BlockSpec((None, RET_CHUNK, LANES), lambda i, p, j: (p, 0, 0)),
                  pl.BlockSpec((None, RET_CHUNK, LANES), lambda i, p, j: (p, 0, 0)),
                  pl.BlockSpec((None, LANES, LANES), lambda i, p, j: (p, 0, 0)),
                  pl.BlockSpec((None, 1, LANES), lambda i, p, j: (p, 0, 0))],
        out_specs=pl.BlockSpec((None, rows, LANES), lambda i, p, j: (i, j, p)),
        out_shape=jax.ShapeDtypeStruct((b, s, B_WIDTH), _BF16),
        scratch_shapes=[pltpu.VMEM((LANES, LANES), _F32)],
        compiler_params=_cparams(3),
        name="retention_b",
    )(proj3, proj3, proj3, proj3, cos3, sin3, decay, zeta, xi, cdm, gn)


def _cross_kernel(q_ref, k_ref, v_ref, qn_ref, o_ref, *, rows):
    low = _lane((Q_SUB, LANES)) < HEAD_DIM

    def sub(j, carry):
        r = pl.multiple_of(j * Q_SUB, Q_SUB)
        for lb in range(C_WIDTH // LANES):
            sl = slice(lb * LANES, (lb + 1) * LANES)
            qn = _pair_rms(q_ref[pl.ds(r, Q_SUB), sl], qn_ref[...])
            kb = k_ref[:, sl]
            vb = v_ref[:, sl]
            outs = []
            for h in range(2):
                qh = jnp.where(low if h == 0 else ~low, qn, 0.0).astype(_BF16)
                sc = _dot_nt(qh, kb)
                m = jnp.max(sc, axis=-1, keepdims=True)
                p = jnp.exp(sc - m)
                probs = p / jnp.sum(p, axis=-1, keepdims=True)
                outs.append(_dot(probs.astype(_BF16), vb))
            o_ref[pl.ds(r, Q_SUB), sl] = jnp.where(low, outs[0], outs[1]).astype(o_ref.dtype)
        return carry

    lax.fori_loop(0, rows // Q_SUB, sub, 0)


def _cross_attention(proj3, kc, vc, qn_c):
    b, s, _ = proj3.shape
    m = kc.shape[1]
    rows = min(s, 1024)
    qn = (jnp.tile(qn_c, 2) * (HEAD_DIM ** -0.5)).reshape(1, LANES)
    qcol = (3 * A_WIDTH + 4 * B_WIDTH) // C_WIDTH
    kv = pl.BlockSpec((None, m, C_WIDTH), lambda i, j: (i, 0, 0))
    return pl.pallas_call(
        functools.partial(_cross_kernel, rows=rows),
        grid=(b, s // rows),
        in_specs=[pl.BlockSpec((None, rows, C_WIDTH), lambda i, j: (i, j, qcol)), kv, kv,
                  pl.BlockSpec((1, LANES), lambda i, j: (0, 0))],
        out_specs=pl.BlockSpec((None, rows, C_WIDTH), lambda i, j: (i, j, 0)),
        out_shape=jax.ShapeDtypeStruct((b, s, C_WIDTH), _BF16),
        compiler_params=_cparams(2),
        name="cross_c",
    )(proj3, kc, vc, qn)


def _out_router_kernel(x_ref, a_ref, b_ref, c_ref, wo_ref, g_ref, wr_ref, br_ref,
                       h_ref, hn_ref, info_ref, cnt_ref, carry_ref):
    @pl.when(pl.program_id(0) == 0)
    def _():
        carry_ref[...] = jnp.zeros_like(carry_ref)

    tm = x_ref.shape[0]
    h = x_ref[...]
    h = h + _dot(a_ref[...], wo_ref[0:A_WIDTH, :])
    h = h + _dot(b_ref[...], wo_ref[A_WIDTH:A_WIDTH + B_WIDTH, :])
    h = h + _dot(c_ref[...], wo_ref[A_WIDTH + B_WIDTH:, :])
    h_ref[...] = h
    ms = jnp.mean(h * h, axis=-1, keepdims=True)
    hn = (h * lax.rsqrt(ms + EPS)) * g_ref[...]
    hn_ref[...] = hn
    logits = _dot(hn.astype(_BF16), wr_ref[...]) + br_ref[...]

    lane = _lane((tm, LANES)).astype(_F32)
    big = float(LANES)

    def first_lane(mask):
        return jnp.min(jnp.where(mask, lane, big), axis=-1, keepdims=True)

    gmask = lane < N_GROUPS
    gl = jnp.where(gmask, logits, NEG_INF)
    ge = jnp.exp(gl - jnp.max(gl, axis=-1, keepdims=True))
    gp = ge / jnp.sum(ge, axis=-1, keepdims=True)
    p_group = jnp.max(gp, axis=-1, keepdims=True)
    g_sel = first_lane(gmask & (gp == p_group))
    lo = ROUTE_LANE0 + g_sel * EXPERTS_PER_GROUP
    emask = (lane >= lo) & (lane < lo + EXPERTS_PER_GROUP)
    el = jnp.where(emask, logits, NEG_INF)
    ee = jnp.exp(el - jnp.max(el, axis=-1, keepdims=True))
    ep = ee / jnp.sum(ee, axis=-1, keepdims=True)
    p1 = jnp.max(ep, axis=-1, keepdims=True)
    i1 = first_lane(emask & (ep == p1))
    ep2 = jnp.where(emask & (lane != i1), ep, -1.0)
    p2 = jnp.max(ep2, axis=-1, keepdims=True)
    i2 = first_lane(ep2 == p2)
    den = p1 + p2
    w1 = p_group * (p1 / den)
    w2 = p_group * (p2 / den)
    hit1 = lane == i1
    hit2 = lane == i2
    onehot = jnp.where(hit1 | hit2, 1.0, 0.0)
    r_i = lax.broadcasted_iota(jnp.int32, (tm, tm), 0)
    c_i = lax.broadcasted_iota(jnp.int32, (tm, tm), 1)
    strict = jnp.where(c_i < r_i, 1.0, 0.0).astype(_BF16)
    before = _dot(strict, onehot.astype(_BF16)) + carry_ref[...]
    r1 = jnp.sum(jnp.where(hit1, before, 0.0), axis=-1, keepdims=True)
    r2 = jnp.sum(jnp.where(hit2, before, 0.0), axis=-1, keepdims=True)
    carry_ref[...] = carry_ref[...] + jnp.sum(onehot, axis=0, keepdims=True)
    cnt_ref[...] = carry_ref[...]
    info = jnp.where(lane == 0, w1, 0.0)
    info = jnp.where(lane == 1, w2, info)
    info = jnp.where(lane == 2, i1 - ROUTE_LANE0, info)
    info = jnp.where(lane == 3, i2 - ROUTE_LANE0, info)
    info = jnp.where(lane == 4, r1, info)
    info = jnp.where(lane == 5, r2, info)
    info_ref[...] = info


def _out_router(x2, oa, ob, oc, w_out, ffn_g, w_rg, b_rg, w_re, b_re):
    t, d = x2.shape
    tm = min(t, 512)
    pad = LANES - N_GROUPS - N_EXPERTS
    wr = jnp.concatenate([w_rg, w_re, jnp.zeros((d, pad), _F32)], axis=1).astype(_BF16)
    br = jnp.concatenate([b_rg, b_re, jnp.zeros((pad,), _F32)]).reshape(1, LANES)

    def rows(w):
        return pl.BlockSpec((tm, w), lambda i: (i, 0))

    def whole(r, c):
        return pl.BlockSpec((r, c), lambda i: (0, 0))

    return pl.pallas_call(
        _out_router_kernel,
        grid=(t // tm,),
        in_specs=[rows(d), rows(A_WIDTH), rows(B_WIDTH), rows(C_WIDTH), whole(d, d), whole(1, d),
                  whole(d, LANES), whole(1, LANES)],
        out_specs=[rows(d), rows(d), rows(LANES), whole(1, LANES)],
        out_shape=[jax.ShapeDtypeStruct((t, d), _F32), jax.ShapeDtypeStruct((t, d), _F32),
                   jax.ShapeDtypeStruct((t, LANES), _F32), jax.ShapeDtypeStruct((1, LANES), _F32)],
        scratch_shapes=[pltpu.VMEM((1, LANES), _F32)],
        compiler_params=_cparams(1),
        name="out_router",
    )(x2, oa, ob, oc, w_out.astype(_BF16), ffn_g.reshape(1, d), wr, br)


DISPATCH_TOKENS = 512
COMBINE_TOKENS = 256


def _row_copy(src, s_row, dst, d_row, sem):
    return pltpu.make_async_copy(src.at[pl.ds(s_row, 1)], dst.at[pl.ds(d_row, 1)], sem)


def _dispatch_kernel(dest_ref, hn_ref, xs_in_ref, xs_ref, sem):
    del xs_in_ref
    n = dest_ref.shape[0] // 2
    base = pl.program_id(0) * n

    def issue(t, carry):
        _row_copy(hn_ref, base + t, xs_ref, dest_ref[2 * t], sem).start()
        _row_copy(hn_ref, base + t, xs_ref, dest_ref[2 * t + 1], sem).start()
        return carry

    lax.fori_loop(0, n, issue, 0)

    def drain(t, carry):
        _row_copy(hn_ref, 0, xs_ref, 0, sem).wait()
        _row_copy(hn_ref, 0, xs_ref, 0, sem).wait()
        return carry

    lax.fori_loop(0, n, drain, 0)


def _dispatch(hn, dest, n_rows):
    t, d = hn.shape
    n = min(t, DISPATCH_TOKENS)
    xs0 = jnp.zeros((n_rows, d), hn.dtype)
    return pl.pallas_call(
        _dispatch_kernel,
        grid=(t // n,),
        in_specs=[pl.BlockSpec((2 * n,), lambda i: (i,), memory_space=pltpu.SMEM),
                  pl.BlockSpec(memory_space=pl.ANY),
                  pl.BlockSpec(memory_space=pl.ANY)],
        out_specs=pl.BlockSpec(memory_space=pl.ANY),
        out_shape=jax.ShapeDtypeStruct((n_rows, d), hn.dtype),
        scratch_shapes=[pltpu.SemaphoreType.DMA],
        input_output_aliases={2: 0},
        compiler_params=_cparams(1),
        name="moe_dispatch",
    )(dest, hn, xs0)


def _expert_kernel(be_ref, used_ref, x_ref, wg_ref, wu_ref, wd_ref, y_ref):
    @pl.when(pl.program_id(0) < used_ref[0])
    def _():
        x = x_ref[...].astype(_BF16)
        gate = _dot(x, wg_ref[...])
        up = _dot(x, wu_ref[...])
        act = (gate * jax.nn.sigmoid(gate)) * up
        y_ref[...] = _dot(act.astype(_BF16), wd_ref[...])

    @pl.when(pl.program_id(0) >= used_ref[0])
    def _():
        y_ref[...] = jnp.zeros_like(y_ref)


def _experts(xs, block_e, n_used, w_gate, w_up, w_down):
    n_rows, d = xs.shape
    n_blocks = n_rows // ROW_BLOCK

    def xmap(i, be, used):
        return (jnp.minimum(i, used[0] - 1), 0)

    def wmap(i, be, used):
        return (be[jnp.minimum(i, used[0] - 1)], 0, 0)

    return pl.pallas_call(
        _expert_kernel,
        grid_spec=pltpu.PrefetchScalarGridSpec(
            num_scalar_prefetch=2,
            grid=(n_blocks,),
            in_specs=[pl.BlockSpec((ROW_BLOCK, d), xmap),
                      pl.BlockSpec((None, d, D_EXPERT), wmap),
                      pl.BlockSpec((None, d, D_EXPERT), wmap),
                      pl.BlockSpec((None, D_EXPERT, d), wmap)],
            out_specs=pl.BlockSpec((ROW_BLOCK, d), lambda i, be, used: (i, 0))),
        out_shape=jax.ShapeDtypeStruct((n_rows, d), _F32),
        compiler_params=_cparams(1),
        name="moe_experts",
    )(block_e, n_used, xs, w_gate.astype(_BF16), w_up.astype(_BF16), w_down.astype(_BF16))


def _combine_kernel(dest_ref, h_ref, info_ref, ys_ref, o_ref, buf_ref, sem):
    n = h_ref.shape[0]

    def issue(t, carry):
        _row_copy(ys_ref, dest_ref[2 * t], buf_ref.at[0], t, sem).start()
        _row_copy(ys_ref, dest_ref[2 * t + 1], buf_ref.at[1], t, sem).start()
        return carry

    lax.fori_loop(0, n, issue, 0)

    def drain(t, carry):
        _row_copy(ys_ref, 0, buf_ref.at[0], 0, sem).wait()
        _row_copy(ys_ref, 0, buf_ref.at[1], 0, sem).wait()
        return carry

    lax.fori_loop(0, n, drain, 0)
    info = info_ref[...]
    moe = info[:, 0:1] * buf_ref[0] + info[:, 1:2] * buf_ref[1]
    o_ref[...] = h_ref[...] + moe


def _combine(h, info, ys, dest):
    t, d = h.shape
    n = min(t, COMBINE_TOKENS)
    return pl.pallas_call(
        _combine_kernel,
        grid=(t // n,),
        in_specs=[pl.BlockSpec((2 * n,), lambda i: (i,), memory_space=pltpu.SMEM),
                  pl.BlockSpec((n, d), lambda i: (i, 0)),
                  pl.BlockSpec((n, LANES), lambda i: (i, 0)),
                  pl.BlockSpec(memory_space=pl.ANY)],
        out_specs=pl.BlockSpec((n, d), lambda i: (i, 0)),
        out_shape=jax.ShapeDtypeStruct((t, d), _F32),
        scratch_shapes=[pltpu.VMEM((2, n, d), _F32), pltpu.SemaphoreType.DMA],
        compiler_params=_cparams(1),
        name="moe_combine",
    )(dest, h, info, ys)


def _moe_layout(info, counts, t):
    counts = counts[0, ROUTE_LANE0:ROUTE_LANE0 + N_EXPERTS].astype(jnp.int32)
    padded = (counts + ROW_BLOCK - 1) // ROW_BLOCK * ROW_BLOCK
    pends = jnp.cumsum(padded)
    pstarts = pends - padded
    eid = info[:, 2:4].astype(jnp.int32)
    rank = info[:, 4:6].astype(jnp.int32)
    dest = (pstarts[eid] + rank).reshape(-1)
    n_blocks = -(-2 * t // ROW_BLOCK) + N_EXPERTS
    block_e = jnp.minimum(
        jnp.searchsorted(pends, jnp.arange(n_blocks, dtype=jnp.int32) * ROW_BLOCK, side='right'),
        N_EXPERTS - 1).astype(jnp.int32)
    n_used = (pends[-1:] // ROW_BLOCK).astype(jnp.int32)
    return dest, block_e, n_used, n_blocks * ROW_BLOCK


def kernel(x, mem, positions, mix_norm_g, w_in, qn_a, kn_a, rel_bias, ret_gn_g, mem_norm_g, w_mem_kv,
           qn_c, kn_c, w_out, ffn_norm_g, w_router_group, b_router_group, w_router_expert,
           b_router_expert, w_gate, w_up, w_down):
    b, s, d = x.shape
    t = b * s
    x2 = x.reshape(t, d)
    cos, sin = _rope_tables(positions)
    kc, vc = _mem_kv(mem, mem_norm_g, w_mem_kv, kn_c)
    proj3 = _in_proj(x2, mix_norm_g, w_in).reshape(b, s, IN_COLS)
    out_a = _attention(proj3, qn_a, kn_a, rel_bias)
    out_b = _retention(proj3, cos, sin, ret_gn_g)
    out_c = _cross_attention(proj3, kc, vc, qn_c)
    h, hn, info, counts = _out_router(
        x2, out_a.reshape(t, A_WIDTH), out_b.reshape(t, B_WIDTH), out_c.reshape(t, C_WIDTH),
        w_out, ffn_norm_g, w_router_group, b_router_group, w_router_expert, b_router_expert)
    dest, block_e, n_used, n_rows = _moe_layout(info, counts, t)
    xs = _dispatch(hn, dest, n_rows)
    ys = _experts(xs, block_e, n_used, w_gate, w_up, w_down)
    return _combine(h, info, ys, dest).reshape(b, s, d)
```

```python
import functools

import jax
import jax.numpy as jnp
from jax import lax
from jax.experimental import pallas as pl
from jax.experimental.pallas import tpu as pltpu

D_MODEL = 1024
CHUNK = 64
HEAD_DIM = 64
A_HEADS = 8
B_HEADS = 4
C_HEADS = 4
A_WIDTH = A_HEADS * HEAD_DIM
B_WIDTH = B_HEADS * HEAD_DIM
C_WIDTH = C_HEADS * HEAD_DIM
IN_COLS = 3 * A_WIDTH + 4 * B_WIDTH + C_WIDTH
LEFT_CHUNKS = 8
BAND_CHUNKS = LEFT_CHUNKS + 1
MAX_REL_DIST = 128
ROPE_BASE = 10000.0
N_GROUPS = 4
EXPERTS_PER_GROUP = 8
N_EXPERTS = N_GROUPS * EXPERTS_PER_GROUP
D_EXPERT = D_MODEL // 2
EPS = 1e-6
NEG_INF = -1e30

LANES = 128
Q_SUB = 4 * CHUNK
BAND = Q_SUB + LEFT_CHUNKS * CHUNK
LEFT_ROWS = LEFT_CHUNKS * CHUNK
RET_CHUNK = 256
ROW_BLOCK = 256
ROUTE_LANE0 = N_GROUPS
VMEM_LIMIT = 48 * 1024 * 1024

_F32 = jnp.float32
_BF16 = jnp.bfloat16


def _cparams(n_axes):
    return pltpu.CompilerParams(dimension_semantics=("arbitrary",) * n_axes,
                                vmem_limit_bytes=VMEM_LIMIT)


def _dot(a, b):
    return jnp.dot(a, b, preferred_element_type=_F32)


def _dot_nt(a, b):
    return lax.dot_general(a, b, (((1,), (1,)), ((), ())), preferred_element_type=_F32)


def _lane(shape):
    return lax.broadcasted_iota(jnp.int32, shape, len(shape) - 1)


def _pair_rms(t, gain):
    low = _lane(t.shape) < HEAD_DIM
    t2 = t * t
    ms0 = jnp.sum(jnp.where(low, t2, 0.0), axis=-1, keepdims=True) * (1.0 / HEAD_DIM)
    ms1 = jnp.sum(jnp.where(low, 0.0, t2), axis=-1, keepdims=True) * (1.0 / HEAD_DIM)
    r = jnp.where(low, lax.rsqrt(ms0 + EPS), lax.rsqrt(ms1 + EPS))
    return (t * r) * gain


def _rope_kernel(pos_ref, inv_ref, sign_ref, cos_ref, sin_ref):
    ang = pos_ref[...].astype(_F32) * inv_ref[...]
    cos_ref[...] = jnp.cos(ang)
    sin_ref[...] = jnp.sin(ang) * sign_ref[...]


def _rope_tables(positions):
    t = positions.size
    half = HEAD_DIM // 2
    inv = ROPE_BASE ** (-jnp.arange(half, dtype=_F32) / half)
    inv128 = jnp.tile(inv, LANES // half).reshape(1, LANES)
    sign = jnp.where((jnp.arange(LANES) % HEAD_DIM) < half, -1.0, 1.0).astype(_F32).reshape(1, LANES)
    pos = jnp.broadcast_to(positions.reshape(t, 1), (t, LANES))
    tm = min(t, 1024)
    row = pl.BlockSpec((tm, LANES), lambda i: (i, 0))
    one = pl.BlockSpec((1, LANES), lambda i: (0, 0))
    return pl.pallas_call(
        _rope_kernel,
        grid=(t // tm,),
        in_specs=[row, one, one],
        out_specs=[row, row],
        out_shape=[jax.ShapeDtypeStruct((t, LANES), _F32)] * 2,
        compiler_params=_cparams(1),
        name="rope_tables",
    )(pos, inv128, sign)


def _mem_kv_kernel(mem_ref, g_ref, w_ref, kn_ref, k_ref, v_ref):
    m = mem_ref[...]
    ms = jnp.mean(m * m, axis=-1, keepdims=True)
    mn = (m * lax.rsqrt(ms + EPS)) * g_ref[...]
    kv = _dot(mn.astype(_BF16), w_ref[...])
    for j in range(C_WIDTH // LANES):
        sl = slice(j * LANES, (j + 1) * LANES)
        k_ref[:, sl] = _pair_rms(kv[:, sl], kn_ref[...]).astype(_BF16)
        v_ref[:, sl] = kv[:, C_WIDTH + j * LANES:C_WIDTH + (j + 1) * LANES].astype(_BF16)


def _mem_kv(mem, mem_norm_g, w_mem_kv, kn_c):
    b, m, d = mem.shape
    kn = jnp.tile(kn_c, 2).reshape(1, LANES)
    out = pl.BlockSpec((None, m, C_WIDTH), lambda i: (i, 0, 0))
    return pl.pallas_call(
        _mem_kv_kernel,
        grid=(b,),
        in_specs=[pl.BlockSpec((None, m, d), lambda i: (i, 0, 0)),
                  pl.BlockSpec((1, d), lambda i: (0, 0)),
                  pl.BlockSpec((d, 2 * C_WIDTH), lambda i: (0, 0)),
                  pl.BlockSpec((1, LANES), lambda i: (0, 0))],
        out_specs=[out, out],
        out_shape=[jax.ShapeDtypeStruct((b, m, C_WIDTH), _BF16)] * 2,
        compiler_params=_cparams(1),
        name="mem_kv",
    )(mem, mem_norm_g.reshape(1, d), w_mem_kv.astype(_BF16), kn)


def _in_proj_kernel(x_ref, g_ref, w_ref, o_ref):
    x = x_ref[...]
    ms = jnp.mean(x * x, axis=-1, keepdims=True)
    xn = ((x * lax.rsqrt(ms + EPS)) * g_ref[...]).astype(_BF16)
    o_ref[...] = _dot(xn, w_ref[...])


def _in_proj(x2, g, w_in):
    t, d = x2.shape
    tm = min(t, 512)
    return pl.pallas_call(
        _in_proj_kernel,
        grid=(t // tm,),
        in_specs=[pl.BlockSpec((tm, d), lambda i: (i, 0)),
                  pl.BlockSpec((1, d), lambda i: (0, 0)),
                  pl.BlockSpec((d, IN_COLS), lambda i: (0, 0))],
        out_specs=pl.BlockSpec((tm, IN_COLS), lambda i: (i, 0)),
        out_shape=jax.ShapeDtypeStruct((t, IN_COLS), _F32),
        compiler_params=_cparams(1),
        name="in_proj",
    )(x2, g.reshape(1, d), w_in.astype(_BF16))


def _attn_kernel(q_ref, k_ref, v_ref, qn_ref, kn_ref, bias_ref, o_ref, kp_ref, vp_ref, *, q_rows):
    qs = pl.program_id(2)
    s = k_ref.shape[0]
    norm_rows = min(s, 512)

    @pl.when(qs == 0)
    def _():
        kp_ref[0:LEFT_ROWS, :] = jnp.zeros((LEFT_ROWS, LANES), _BF16)
        vp_ref[0:LEFT_ROWS, :] = jnp.zeros((LEFT_ROWS, LANES), _BF16)

        def fill(i, carry):
            r = pl.multiple_of(i * norm_rows, norm_rows)
            kp_ref[pl.ds(LEFT_ROWS + r, norm_rows), :] = _pair_rms(
                k_ref[pl.ds(r, norm_rows), :], kn_ref[...]).astype(_BF16)
            vp_ref[pl.ds(LEFT_ROWS + r, norm_rows), :] = v_ref[pl.ds(r, norm_rows), :].astype(_BF16)
            return carry

        lax.fori_loop(0, s // norm_rows, fill, 0)

    low = _lane((Q_SUB, LANES)) < HEAD_DIM
    col = _lane((Q_SUB, BAND))

    def sub(j, carry):
        r = pl.multiple_of(j * Q_SUB, Q_SUB)
        row0 = pl.multiple_of(qs * q_rows + r, Q_SUB)
        qn = _pair_rms(q_ref[pl.ds(r, Q_SUB), :], qn_ref[...])
        kb = kp_ref[pl.ds(row0, BAND), :]
        vb = vp_ref[pl.ds(row0, BAND), :]
        exists = col >= LEFT_ROWS - row0
        outs = []
        for h in range(2):
            qh = jnp.where(low if h == 0 else ~low, qn, 0.0).astype(_BF16)
            sc = _dot_nt(qh, kb) + bias_ref[h]
            sc = jnp.where(exists, sc, NEG_INF)
            m = jnp.max(sc, axis=-1, keepdims=True)
            p = jnp.exp(sc - m)
            probs = p / jnp.sum(p, axis=-1, keepdims=True)
            outs.append(_dot(probs.astype(_BF16), vb))
        o_ref[pl.ds(r, Q_SUB), :] = jnp.where(low, outs[0], outs[1]).astype(o_ref.dtype)
        return carry

    lax.fori_loop(0, q_rows // Q_SUB, sub, 0)


def _band_bias(rel_bias):
    qpos = LEFT_ROWS + jnp.arange(CHUNK)
    kpos = jnp.arange(BAND_CHUNKS * CHUNK)
    rel = jnp.clip(qpos[:, None] - kpos[None, :], -(CHUNK - 1), MAX_REL_DIST) + (CHUNK - 1)
    bias = rel_bias[:, rel].astype(_F32)
    full = jnp.full((rel_bias.shape[0], Q_SUB, BAND), NEG_INF, _F32)
    for a in range(Q_SUB // CHUNK):
        full = lax.dynamic_update_slice(full, bias, (0, a * CHUNK, a * CHUNK))
    return full


def _attention(proj3, qn_a, kn_a, rel_bias):
    b, s, _ = proj3.shape
    q_rows = min(s, 1024)
    qn = (jnp.tile(qn_a, 2) * (HEAD_DIM ** -0.5)).reshape(1, LANES)
    kn = jnp.tile(kn_a, 2).reshape(1, LANES)
    pairs = A_HEADS // 2
    kcol, vcol = A_WIDTH // LANES, 2 * A_WIDTH // LANES
    return pl.pallas_call(
        functools.partial(_attn_kernel, q_rows=q_rows),
        grid=(b, pairs, s // q_rows),
        in_specs=[pl.BlockSpec((None, q_rows, LANES), lambda i, p, j: (i, j, p)),
                  pl.BlockSpec((None, s, LANES), lambda i, p, j: (i, 0, kcol + p)),
                  pl.BlockSpec((None, s, LANES), lambda i, p, j: (i, 0, vcol + p)),
                  pl.BlockSpec((1, LANES), lambda i, p, j: (0, 0)),
                  pl.BlockSpec((1, LANES), lambda i, p, j: (0, 0)),
                  pl.BlockSpec((2, Q_SUB, BAND), lambda i, p, j: (p, 0, 0))],
        out_specs=pl.BlockSpec((None, q_rows, LANES), lambda i, p, j: (i, j, p)),
        out_shape=jax.ShapeDtypeStruct((b, s, A_WIDTH), _BF16),
        scratch_shapes=[pltpu.VMEM((s + LEFT_ROWS, LANES), _BF16),
                        pltpu.VMEM((s + LEFT_ROWS, LANES), _BF16)],
        compiler_params=_cparams(3),
        name="attn_a",
    )(proj3, proj3, proj3, qn, kn, _band_bias(rel_bias))


def _swap_halves(t):
    first = (_lane(t.shape) % HEAD_DIM) < (HEAD_DIM // 2)
    return jnp.where(first, pltpu.roll(t, LANES - HEAD_DIM // 2, 1), pltpu.roll(t, HEAD_DIM // 2, 1))


def _retention_kernel(q_ref, k_ref, v_ref, gate_ref, cos_ref, sin_ref, decay_ref, zeta_ref, xi_ref,
                      cd_ref, gn_ref, o_ref, state_ref, *, rows):
    @pl.when(pl.program_id(2) == 0)
    def _():
        state_ref[...] = jnp.zeros_like(state_ref)

    c = RET_CHUNK
    low = _lane((c, LANES)) < HEAD_DIM
    srow = lax.broadcasted_iota(jnp.int32, (LANES, LANES), 0) < HEAD_DIM
    scol = _lane((LANES, LANES)) < HEAD_DIM
    same_head = srow == scol

    for j in range(rows // c):
        sl = slice(j * c, (j + 1) * c)
        cos, sin = cos_ref[sl, :], sin_ref[sl, :]
        q = q_ref[sl, :]
        k = k_ref[sl, :]
        qr = q * cos + _swap_halves(q) * sin
        kr = (k * cos + _swap_halves(k) * sin) * (HEAD_DIM ** -0.5)
        vb = v_ref[sl, :].astype(_BF16)
        qb = qr.astype(_BF16)
        kb = kr.astype(_BF16)
        inner_out = []
        for h in range(2):
            qh = jnp.where(low if h == 0 else ~low, qr, 0.0).astype(_BF16)
            inner = _dot_nt(qh, kb) * decay_ref[h]
            inner_out.append(_dot(inner.astype(_BF16), vb))
        state = state_ref[...]
        cross = _dot(qb, state.astype(_BF16)) * xi_ref[...]
        o = jnp.where(low, inner_out[0], inner_out[1]) + cross
        kz = (kr * zeta_ref[...]).T.astype(_BF16)
        state_ref[...] = cd_ref[...] * state + jnp.where(same_head, _dot(kz, vb), 0.0)
        mu = jnp.where(low,
                       jnp.sum(jnp.where(low, o, 0.0), axis=-1, keepdims=True),
                       jnp.sum(jnp.where(low, 0.0, o), axis=-1, keepdims=True)) * (1.0 / HEAD_DIM)
        dlt = o - mu
        d2 = dlt * dlt
        var = jnp.where(low,
                        jnp.sum(jnp.where(low, d2, 0.0), axis=-1, keepdims=True),
                        jnp.sum(jnp.where(low, 0.0, d2), axis=-1, keepdims=True)) * (1.0 / HEAD_DIM)
        y = (dlt * lax.rsqrt(var + EPS)) * gn_ref[...]
        g = gate_ref[sl, :]
        o_ref[sl, :] = ((g * jax.nn.sigmoid(g)) * y).astype(o_ref.dtype)


def _retention_tables():
    c = RET_CHUNK
    log_g = jnp.log(1.0 - jnp.exp2(-5.0 - jnp.arange(B_HEADS, dtype=_F32)))
    idx = jnp.arange(c, dtype=_F32)
    diff = idx[:, None] - idx[None, :]
    decay = jnp.where(diff >= 0, jnp.exp(log_g[:, None, None] * jnp.maximum(diff, 0.0)), 0.0)
    zeta = jnp.exp(log_g[:, None] * (c - 1 - idx))
    xi = jnp.exp(log_g[:, None] * (idx + 1.0))
    cd = jnp.exp(log_g * c)

    def lanes(tab):
        return jnp.repeat(tab.reshape(B_HEADS // 2, 2, c), HEAD_DIM, axis=1).transpose(0, 2, 1)

    cdm = jnp.repeat(cd.reshape(B_HEADS // 2, 2), HEAD_DIM, axis=1)
    cdm = jnp.broadcast_to(cdm[:, :, None], (B_HEADS // 2, LANES, LANES))
    return decay, lanes(zeta), lanes(xi), cdm


def _retention(proj3, cos, sin, ret_gn_g):
    b, s, _ = proj3.shape
    rows = min(s, 1024)
    pairs = B_HEADS // 2
    base = 3 * A_WIDTH // LANES
    decay, zeta, xi, cdm = _retention_tables()
    cos3, sin3 = cos.reshape(b, s, LANES), sin.reshape(b, s, LANES)
    gn = ret_gn_g.reshape(pairs, 1, LANES)

    def col(off):
        return pl.BlockSpec((None, rows, LANES), lambda i, p, j: (i, j, base + off * pairs + p))

    tab = pl.BlockSpec((None, rows, LANES), lambda i, p, j: (i, j, 0))
    return pl.pallas_call(
        functools.partial(_retention_kernel, rows=rows),
        grid=(b, pairs, s // rows),
        in_specs=[col(0), col(1), col(2), col(3), tab, tab,
                  pl.BlockSpec((2, RET_CHUNK, RET_CHUNK), lambda i, p, j: (p, 0, 0)),
                  pl.BlockSpec((None, RET_CHUNK, LANES), lambda i, p, j: (p, 0, 0)),
                  pl.BlockSpec((None, RET_CHUNK, LANES), lambda i, p, j: (p, 0, 0)),
                  pl.BlockSpec((None, LANES, LANES), lambda i, p, j: (p, 0, 0)),
                  pl.BlockSpec((None, 1, LANES), lambda i, p, j: (p, 0, 0))],
        out_specs=pl.BlockSpec((None, rows, LANES), lambda i, p, j: (i, j, p)),
        out_shape=jax.ShapeDtypeStruct((b, s, B_WIDTH), _BF16),
        scratch_shapes=[pltpu.VMEM((LANES, LANES), _F32)],
        compiler_params=_cparams(3),
        name="retention_b",
    )(proj3, proj3, proj3, proj3, cos3, sin3, decay, zeta, xi, cdm, gn)


def _cross_kernel(q_ref, k_ref, v_ref, qn_ref, o_ref, *, rows):
    low = _lane((Q_SUB, LANES)) < HEAD_DIM

    def sub(j, carry):
        r = pl.multiple_of(j * Q_SUB, Q_SUB)
        for lb in range(C_WIDTH // LANES):
            sl = slice(lb * LANES, (lb + 1) * LANES)
            qn = _pair_rms(q_ref[pl.ds(r, Q_SUB), sl], qn_ref[...])
            kb = k_ref[:, sl]
            vb = v_ref[:, sl]
            outs = []
            for h in range(2):
                qh = jnp.where(low if h == 0 else ~low, qn, 0.0).astype(_BF16)
                sc = _dot_nt(qh, kb)
                m = jnp.max(sc, axis=-1, keepdims=True)
                p = jnp.exp(sc - m)
                probs = p / jnp.sum(p, axis=-1, keepdims=True)
                outs.append(_dot(probs.astype(_BF16), vb))
            o_ref[pl.ds(r, Q_SUB), sl] = jnp.where(low, outs[0], outs[1]).astype(o_ref.dtype)
        return carry

    lax.fori_loop(0, rows // Q_SUB, sub, 0)


def _cross_attention(proj3, kc, vc, qn_c):
    b, s, _ = proj3.shape
    m = kc.shape[1]
    rows = min(s, 1024)
    qn = (jnp.tile(qn_c, 2) * (HEAD_DIM ** -0.5)).reshape(1, LANES)
    qcol = (3 * A_WIDTH + 4 * B_WIDTH) // C_WIDTH
    kv = pl.BlockSpec((None, m, C_WIDTH), lambda i, j: (i, 0, 0))
    return pl.pallas_call(
        functools.partial(_cross_kernel, rows=rows),
        grid=(b, s // rows),
        in_specs=[pl.BlockSpec((None, rows, C_WIDTH), lambda i, j: (i, j, qcol)), kv, kv,
                  pl.BlockSpec((1, LANES), lambda i, j: (0, 0))],
        out_specs=pl.BlockSpec((None, rows, C_WIDTH), lambda i, j: (i, j, 0)),
        out_shape=jax.ShapeDtypeStruct((b, s, C_WIDTH), _BF16),
        compiler_params=_cparams(2),
        name="cross_c",
    )(proj3, kc, vc, qn)


def _out_router_kernel(x_ref, a_ref, b_ref, c_ref, wo_ref, g_ref, wr_ref, br_ref,
                       h_ref, hn_ref, info_ref, cnt_ref, carry_ref):
    @pl.when(pl.program_id(0) == 0)
    def _():
        carry_ref[...] = jnp.zeros_like(carry_ref)

    tm = x_ref.shape[0]
    h = x_ref[...]
    h = h + _dot(a_ref[...], wo_ref[0:A_WIDTH, :])
    h = h + _dot(b_ref[...], wo_ref[A_WIDTH:A_WIDTH + B_WIDTH, :])
    h = h + _dot(c_ref[...], wo_ref[A_WIDTH + B_WIDTH:, :])
    h_ref[...] = h
    ms = jnp.mean(h * h, axis=-1, keepdims=True)
    hn = (h * lax.rsqrt(ms + EPS)) * g_ref[...]
    hn_ref[...] = hn
    logits = _dot(hn.astype(_BF16), wr_ref[...]) + br_ref[...]

    lane = _lane((tm, LANES)).astype(_F32)
    big = float(LANES)

    def first_lane(mask):
        return jnp.min(jnp.where(mask, lane, big), axis=-1, keepdims=True)

    gmask = lane < N_GROUPS
    gl = jnp.where(gmask, logits, NEG_INF)
    ge = jnp.exp(gl - jnp.max(gl, axis=-1, keepdims=True))
    gp = ge / jnp.sum(ge, axis=-1, keepdims=True)
    p_group = jnp.max(gp, axis=-1, keepdims=True)
    g_sel = first_lane(gmask & (gp == p_group))
    lo = ROUTE_LANE0 + g_sel * EXPERTS_PER_GROUP
    emask = (lane >= lo) & (lane < lo + EXPERTS_PER_GROUP)
    el = jnp.where(emask, logits, NEG_INF)
    ee = jnp.exp(el - jnp.max(el, axis=-1, keepdims=True))
    ep = ee / jnp.sum(ee, axis=-1, keepdims=True)
    p1 = jnp.max(ep, axis=-1, keepdims=True)
    i1 = first_lane(emask & (ep == p1))
    ep2 = jnp.where(emask & (lane != i1), ep, -1.0)
    p2 = jnp.max(ep2, axis=-1, keepdims=True)
    i2 = first_lane(ep2 == p2)
    den = p1 + p2
    w1 = p_group * (p1 / den)
    w2 = p_group * (p2 / den)
    hit1 = lane == i1
    hit2 = lane == i2
    onehot = jnp.where(hit1 | hit2, 1.0, 0.0)
    r_i = lax.broadcasted_iota(jnp.int32, (tm, tm), 0)
    c_i = lax.broadcasted_iota(jnp.int32, (tm, tm), 1)
    strict = jnp.where(c_i < r_i, 1.0, 0.0).astype(_BF16)
    before = _dot(strict, onehot.astype(_BF16)) + carry_ref[...]
    r1 = jnp.sum(jnp.where(hit1, before, 0.0), axis=-1, keepdims=True)
    r2 = jnp.sum(jnp.where(hit2, before, 0.0), axis=-1, keepdims=True)
    carry_ref[...] = carry_ref[...] + jnp.sum(onehot, axis=0, keepdims=True)
    cnt_ref[...] = carry_ref[...]
    info = jnp.where(lane == 0, w1, 0.0)
    info = jnp.where(lane == 1, w2, info)
    info = jnp.where(lane == 2, i1 - ROUTE_LANE0, info)
    info = jnp.where(lane == 3, i2 - ROUTE_LANE0, info)
    info = jnp.where(lane == 4, r1, info)
    info = jnp.where(lane == 5, r2, info)
    info_ref[...] = info


def _out_router(x2, oa, ob, oc, w_out, ffn_g, w_rg, b_rg, w_re, b_re):
    t, d = x2.shape
    tm = min(t, 512)
    pad = LANES - N_GROUPS - N_EXPERTS
    wr = jnp.concatenate([w_rg, w_re, jnp.zeros((d, pad), _F32)], axis=1).astype(_BF16)
    br = jnp.concatenate([b_rg, b_re, jnp.zeros((pad,), _F32)]).reshape(1, LANES)

    def rows(w):
        return pl.BlockSpec((tm, w), lambda i: (i, 0))

    def whole(r, c):
        return pl.BlockSpec((r, c), lambda i: (0, 0))

    return pl.pallas_call(
        _out_router_kernel,
        grid=(t // tm,),
        in_specs=[rows(d), rows(A_WIDTH), rows(B_WIDTH), rows(C_WIDTH), whole(d, d), whole(1, d),
                  whole(d, LANES), whole(1, LANES)],
        out_specs=[rows(d), rows(d), rows(LANES), whole(1, LANES)],
        out_shape=[jax.ShapeDtypeStruct((t, d), _F32), jax.ShapeDtypeStruct((t, d), _F32),
                   jax.ShapeDtypeStruct((t, LANES), _F32), jax.ShapeDtypeStruct((1, LANES), _F32)],
        scratch_shapes=[pltpu.VMEM((1, LANES), _F32)],
        compiler_params=_cparams(1),
        name="out_router",
    )(x2, oa, ob, oc, w_out.astype(_BF16), ffn_g.reshape(1, d), wr, br)


DISPATCH_TOKENS = 512
COMBINE_TOKENS = 256


ROW_UNROLL = 8


def _row_copy(src, s_row, dst, d_row, sem):
    return pltpu.make_async_copy(src.at[pl.ds(s_row, 1)], dst.at[pl.ds(d_row, 1)], sem)


def _dispatch_kernel(pad_start_ref, pad_len_ref, used_ref, dest_ref, hn_ref, xs_ref, zero_ref, sem,
                     pad_sem):
    n = hn_ref.shape[0]

    @pl.when(pl.program_id(0) == 0)
    def _():
        zero_ref[...] = jnp.zeros_like(zero_ref)
        n_blocks = xs_ref.shape[0] // ROW_BLOCK

        def block_copy(blk):
            return pltpu.make_async_copy(zero_ref, xs_ref.at[pl.ds(blk * ROW_BLOCK, ROW_BLOCK)], pad_sem)

        def put_block(blk, carry):
            block_copy(blk).start()
            return carry

        def done_block(blk, carry):
            block_copy(blk).wait()
            return carry

        lax.fori_loop(used_ref[0], n_blocks, put_block, 0)
        lax.fori_loop(used_ref[0], n_blocks, done_block, 0)
        for e in range(N_EXPERTS):
            def put(r, carry, e=e):
                _row_copy(zero_ref, 0, xs_ref, pad_start_ref[e] + r, pad_sem).start()
                return carry

            def done(r, carry):
                _row_copy(zero_ref, 0, xs_ref, 0, pad_sem).wait()
                return carry

            lax.fori_loop(0, pad_len_ref[e], put, 0)
            lax.fori_loop(0, pad_len_ref[e], done, 0)

    def issue(i, carry):
        for u in range(ROW_UNROLL):
            t = i * ROW_UNROLL + u
            _row_copy(hn_ref, t, xs_ref, dest_ref[2 * t], sem).start(priority=0)
            _row_copy(hn_ref, t, xs_ref, dest_ref[2 * t + 1], sem).start(priority=1)
        return carry

    lax.fori_loop(0, n // ROW_UNROLL, issue, 0)
    for _ in range(2):
        pltpu.make_async_copy(hn_ref, xs_ref.at[pl.ds(0, n)], sem).wait()


def _dispatch(hn, dest, pad_start, pad_len, n_used, n_rows):
    t, d = hn.shape
    n = min(t, DISPATCH_TOKENS)
    return pl.pallas_call(
        _dispatch_kernel,
        grid_spec=pltpu.PrefetchScalarGridSpec(
            num_scalar_prefetch=3,
            grid=(t // n,),
            in_specs=[pl.BlockSpec((2 * n,), lambda i, *_: (i,), memory_space=pltpu.SMEM),
                      pl.BlockSpec((n, d), lambda i, *_: (i, 0))],
            out_specs=pl.BlockSpec(memory_space=pl.ANY),
            scratch_shapes=[pltpu.VMEM((ROW_BLOCK, d), hn.dtype), pltpu.SemaphoreType.DMA,
                            pltpu.SemaphoreType.DMA]),
        out_shape=jax.ShapeDtypeStruct((n_rows, d), hn.dtype),
        compiler_params=_cparams(1),
        name="moe_dispatch",
    )(pad_start, pad_len, n_used, dest, hn)


def _expert_kernel(be_ref, used_ref, x_ref, wg_ref, wu_ref, wd_ref, y_ref):
    @pl.when(pl.program_id(0) < used_ref[0])
    def _():
        x = x_ref[...].astype(_BF16)
        gate = _dot(x, wg_ref[...])
        up = _dot(x, wu_ref[...])
        act = (gate * jax.nn.sigmoid(gate)) * up
        y_ref[...] = _dot(act.astype(_BF16), wd_ref[...])

    @pl.when(pl.program_id(0) >= used_ref[0])
    def _():
        y_ref[...] = jnp.zeros_like(y_ref)


def _experts(xs, block_e, n_used, w_gate, w_up, w_down):
    n_rows, d = xs.shape
    n_blocks = n_rows // ROW_BLOCK

    def xmap(i, be, used):
        return (jnp.minimum(i, used[0] - 1), 0)

    def wmap(i, be, used):
        return (be[jnp.minimum(i, used[0] - 1)], 0, 0)

    return pl.pallas_call(
        _expert_kernel,
        grid_spec=pltpu.PrefetchScalarGridSpec(
            num_scalar_prefetch=2,
            grid=(n_blocks,),
            in_specs=[pl.BlockSpec((ROW_BLOCK, d), xmap),
                      pl.BlockSpec((None, d, D_EXPERT), wmap),
                      pl.BlockSpec((None, d, D_EXPERT), wmap),
                      pl.BlockSpec((None, D_EXPERT, d), wmap)],
            out_specs=pl.BlockSpec((ROW_BLOCK, d), lambda i, be, used: (i, 0))),
        out_shape=jax.ShapeDtypeStruct((n_rows, d), _F32),
        compiler_params=_cparams(1),
        name="moe_experts",
    )(block_e, n_used, xs, w_gate.astype(_BF16), w_up.astype(_BF16), w_down.astype(_BF16))


def _combine_kernel(dest_ref, next_ref, h_ref, info_ref, ys_ref, o_ref, buf_ref, sem):
    n = h_ref.shape[0]
    step = pl.program_id(0)
    slot = step % 2

    def gather(idx_ref, to_slot):
        def issue(i, carry):
            for u in range(ROW_UNROLL):
                t = i * ROW_UNROLL + u
                _row_copy(ys_ref, idx_ref[2 * t], buf_ref.at[to_slot, 0], t,
                          sem.at[to_slot]).start(priority=0)
                _row_copy(ys_ref, idx_ref[2 * t + 1], buf_ref.at[to_slot, 1], t,
                          sem.at[to_slot]).start(priority=1)
            return carry

        lax.fori_loop(0, n // ROW_UNROLL, issue, 0)

    @pl.when(step == 0)
    def _():
        gather(dest_ref, 0)

    @pl.when(step + 1 < pl.num_programs(0))
    def _():
        gather(next_ref, 1 - slot)

    for k in range(2):
        pltpu.make_async_copy(ys_ref.at[pl.ds(0, n)], buf_ref.at[slot, k], sem.at[slot]).wait()
    info = info_ref[...]
    moe = info[:, 0:1] * buf_ref[slot, 0] + info[:, 1:2] * buf_ref[slot, 1]
    o_ref[...] = h_ref[...] + moe


def _combine(h, info, ys, dest):
    t, d = h.shape
    n = min(t, COMBINE_TOKENS)
    steps = t // n
    return pl.pallas_call(
        _combine_kernel,
        grid=(steps,),
        in_specs=[pl.BlockSpec((2 * n,), lambda i: (i,), memory_space=pltpu.SMEM),
                  pl.BlockSpec((2 * n,), lambda i: (jnp.minimum(i + 1, steps - 1),),
                               memory_space=pltpu.SMEM),
                  pl.BlockSpec((n, d), lambda i: (i, 0)),
                  pl.BlockSpec((n, LANES), lambda i: (i, 0)),
                  pl.BlockSpec(memory_space=pl.ANY)],
        out_specs=pl.BlockSpec((n, d), lambda i: (i, 0)),
        out_shape=jax.ShapeDtypeStruct((t, d), _F32),
        scratch_shapes=[pltpu.VMEM((2, 2, n, d), _F32), pltpu.SemaphoreType.DMA((2,))],
        compiler_params=_cparams(1),
        name="moe_combine",
    )(dest, dest, h, info, ys)


def _moe_layout(info, counts, t):
    counts = counts[0, ROUTE_LANE0:ROUTE_LANE0 + N_EXPERTS].astype(jnp.int32)
    padded = (counts + ROW_BLOCK - 1) // ROW_BLOCK * ROW_BLOCK
    pends = jnp.cumsum(padded)
    pstarts = pends - padded
    eid = info[:, 2:4].astype(jnp.int32)
    rank = info[:, 4:6].astype(jnp.int32)
    experts = jnp.arange(N_EXPERTS, dtype=jnp.int32)
    start_of = jnp.sum(jnp.where(eid[:, :, None] == experts, pstarts, 0), axis=-1)
    dest = (start_of + rank).reshape(-1)
    n_blocks = -(-2 * t // ROW_BLOCK) + N_EXPERTS
    first_row = jnp.arange(n_blocks, dtype=jnp.int32) * ROW_BLOCK
    block_e = jnp.minimum(jnp.sum((pends[None, :] <= first_row[:, None]).astype(jnp.int32), axis=1),
                          N_EXPERTS - 1)
    n_used = (pends[-1:] // ROW_BLOCK).astype(jnp.int32)
    return dest, block_e, n_used, pstarts + counts, padded - counts, n_blocks * ROW_BLOCK


def kernel(x, mem, positions, mix_norm_g, w_in, qn_a, kn_a, rel_bias, ret_gn_g, mem_norm_g, w_mem_kv,
           qn_c, kn_c, w_out, ffn_norm_g, w_router_group, b_router_group, w_router_expert,
           b_router_expert, w_gate, w_up, w_down):
    b, s, d = x.shape
    t = b * s
    x2 = x.reshape(t, d)
    cos, sin = _rope_tables(positions)
    kc, vc = _mem_kv(mem, mem_norm_g, w_mem_kv, kn_c)
    proj3 = _in_proj(x2, mix_norm_g, w_in).reshape(b, s, IN_COLS)
    out_a = _attention(proj3, qn_a, kn_a, rel_bias)
    out_b = _retention(proj3, cos, sin, ret_gn_g)
    out_c = _cross_attention(proj3, kc, vc, qn_c)
    h, hn, info, counts = _out_router(
        x2, out_a.reshape(t, A_WIDTH), out_b.reshape(t, B_WIDTH), out_c.reshape(t, C_WIDTH),
        w_out, ffn_norm_g, w_router_group, b_router_group, w_router_expert, b_router_expert)
    dest, block_e, n_used, pad_start, pad_len, n_rows = _moe_layout(info, counts, t)
    xs = _dispatch(hn, dest, pad_start, pad_len, n_used, n_rows)
    ys = _experts(xs, block_e, n_used, w_gate, w_up, w_down)
    return _combine(h, info, ys, dest).reshape(b, s, d)
```

```python
import functools

import jax
import jax.numpy as jnp
from jax import lax
from jax.experimental import pallas as pl
from jax.experimental.pallas import tpu as pltpu

D_MODEL = 1024
CHUNK = 64
HEAD_DIM = 64
A_HEADS = 8
B_HEADS = 4
C_HEADS = 4
A_WIDTH = A_HEADS * HEAD_DIM
B_WIDTH = B_HEADS * HEAD_DIM
C_WIDTH = C_HEADS * HEAD_DIM
IN_COLS = 3 * A_WIDTH + 4 * B_WIDTH + C_WIDTH
LEFT_CHUNKS = 8
BAND_CHUNKS = LEFT_CHUNKS + 1
MAX_REL_DIST = 128
ROPE_BASE = 10000.0
N_GROUPS = 4
EXPERTS_PER_GROUP = 8
N_EXPERTS = N_GROUPS * EXPERTS_PER_GROUP
D_EXPERT = D_MODEL // 2
EPS = 1e-6
NEG_INF = -1e30

LANES = 128
SUBLANES = 8
assert D_MODEL == SUBLANES * LANES
Q_SUB = 4 * CHUNK
BAND = Q_SUB + LEFT_CHUNKS * CHUNK
LEFT_ROWS = LEFT_CHUNKS * CHUNK
RET_CHUNK = 256
ROW_BLOCK = 256
ROUTE_LANE0 = N_GROUPS
VMEM_LIMIT = 48 * 1024 * 1024

_F32 = jnp.float32
_BF16 = jnp.bfloat16


def _cparams(n_axes):
    return pltpu.CompilerParams(dimension_semantics=("arbitrary",) * n_axes,
                                vmem_limit_bytes=VMEM_LIMIT)


def _dot(a, b):
    return jnp.dot(a, b, preferred_element_type=_F32)


def _dot_nt(a, b):
    return lax.dot_general(a, b, (((1,), (1,)), ((), ())), preferred_element_type=_F32)


def _lane(shape):
    return lax.broadcasted_iota(jnp.int32, shape, len(shape) - 1)


def _pair_rms(t, gain):
    low = _lane(t.shape) < HEAD_DIM
    t2 = t * t
    ms0 = jnp.sum(jnp.where(low, t2, 0.0), axis=-1, keepdims=True) * (1.0 / HEAD_DIM)
    ms1 = jnp.sum(jnp.where(low, 0.0, t2), axis=-1, keepdims=True) * (1.0 / HEAD_DIM)
    r = jnp.where(low, lax.rsqrt(ms0 + EPS), lax.rsqrt(ms1 + EPS))
    return (t * r) * gain


def _rows_to_tiles(ref, val):
    n = val.shape[0]
    for s in range(SUBLANES):
        ref[pl.ds(s, n, stride=SUBLANES), :] = val[:, s * LANES:(s + 1) * LANES]


def _tile_block(ref, s, n):
    return ref[pl.ds(s, n, stride=SUBLANES), :]


def _rope_kernel(pos_ref, inv_ref, sign_ref, cos_ref, sin_ref):
    ang = pos_ref[...].astype(_F32) * inv_ref[...]
    cos_ref[...] = jnp.cos(ang)
    sin_ref[...] = jnp.sin(ang) * sign_ref[...]


def _rope_tables(positions):
    t = positions.size
    half = HEAD_DIM // 2
    inv = ROPE_BASE ** (-jnp.arange(half, dtype=_F32) / half)
    inv128 = jnp.tile(inv, LANES // half).reshape(1, LANES)
    sign = jnp.where((jnp.arange(LANES) % HEAD_DIM) < half, -1.0, 1.0).astype(_F32).reshape(1, LANES)
    pos = jnp.broadcast_to(positions.reshape(t, 1), (t, LANES))
    tm = min(t, 1024)
    row = pl.BlockSpec((tm, LANES), lambda i: (i, 0))
    one = pl.BlockSpec((1, LANES), lambda i: (0, 0))
    return pl.pallas_call(
        _rope_kernel,
        grid=(t // tm,),
        in_specs=[row, one, one],
        out_specs=[row, row],
        out_shape=[jax.ShapeDtypeStruct((t, LANES), _F32)] * 2,
        compiler_params=_cparams(1),
        name="rope_tables",
    )(pos, inv128, sign)


def _mem_kv_kernel(mem_ref, g_ref, w_ref, kn_ref, k_ref, v_ref):
    m = mem_ref[...]
    ms = jnp.mean(m * m, axis=-1, keepdims=True)
    mn = (m * lax.rsqrt(ms + EPS)) * g_ref[...]
    kv = _dot(mn.astype(_BF16), w_ref[...])
    for j in range(C_WIDTH // LANES):
        sl = slice(j * LANES, (j + 1) * LANES)
        k_ref[:, sl] = _pair_rms(kv[:, sl], kn_ref[...]).astype(_BF16)
        v_ref[:, sl] = kv[:, C_WIDTH + j * LANES:C_WIDTH + (j + 1) * LANES].astype(_BF16)


def _mem_kv(mem, mem_norm_g, w_mem_kv, kn_c):
    b, m, d = mem.shape
    kn = jnp.tile(kn_c, 2).reshape(1, LANES)
    out = pl.BlockSpec((None, m, C_WIDTH), lambda i: (i, 0, 0))
    return pl.pallas_call(
        _mem_kv_kernel,
        grid=(b,),
        in_specs=[pl.BlockSpec((None, m, d), lambda i: (i, 0, 0)),
                  pl.BlockSpec((1, d), lambda i: (0, 0)),
                  pl.BlockSpec((d, 2 * C_WIDTH), lambda i: (0, 0)),
                  pl.BlockSpec((1, LANES), lambda i: (0, 0))],
        out_specs=[out, out],
        out_shape=[jax.ShapeDtypeStruct((b, m, C_WIDTH), _BF16)] * 2,
        compiler_params=_cparams(1),
        name="mem_kv",
    )(mem, mem_norm_g.reshape(1, d), w_mem_kv.astype(_BF16), kn)


def _in_proj_kernel(x_ref, g_ref, w_ref, o_ref):
    x = x_ref[...]
    ms = jnp.mean(x * x, axis=-1, keepdims=True)
    xn = ((x * lax.rsqrt(ms + EPS)) * g_ref[...]).astype(_BF16)
    o_ref[...] = _dot(xn, w_ref[...])


def _in_proj(x2, g, w_in):
    t, d = x2.shape
    tm = min(t, 512)
    return pl.pallas_call(
        _in_proj_kernel,
        grid=(t // tm,),
        in_specs=[pl.BlockSpec((tm, d), lambda i: (i, 0)),
                  pl.BlockSpec((1, d), lambda i: (0, 0)),
                  pl.BlockSpec((d, IN_COLS), lambda i: (0, 0))],
        out_specs=pl.BlockSpec((tm, IN_COLS), lambda i: (i, 0)),
        out_shape=jax.ShapeDtypeStruct((t, IN_COLS), _F32),
        compiler_params=_cparams(1),
        name="in_proj",
    )(x2, g.reshape(1, d), w_in.astype(_BF16))


def _attn_kernel(q_ref, k_ref, v_ref, qn_ref, kn_ref, bias_ref, o_ref, kp_ref, vp_ref, *, q_rows):
    qs = pl.program_id(2)
    s = k_ref.shape[0]
    norm_rows = min(s, 512)

    @pl.when(qs == 0)
    def _():
        kp_ref[0:LEFT_ROWS, :] = jnp.zeros((LEFT_ROWS, LANES), _BF16)
        vp_ref[0:LEFT_ROWS, :] = jnp.zeros((LEFT_ROWS, LANES), _BF16)

        def fill(i, carry):
            r = pl.multiple_of(i * norm_rows, norm_rows)
            kp_ref[pl.ds(LEFT_ROWS + r, norm_rows), :] = _pair_rms(
                k_ref[pl.ds(r, norm_rows), :], kn_ref[...]).astype(_BF16)
            vp_ref[pl.ds(LEFT_ROWS + r, norm_rows), :] = v_ref[pl.ds(r, norm_rows), :].astype(_BF16)
            return carry

        lax.fori_loop(0, s // norm_rows, fill, 0)

    low = _lane((Q_SUB, LANES)) < HEAD_DIM
    col = _lane((Q_SUB, BAND))

    def sub(j, carry):
        r = pl.multiple_of(j * Q_SUB, Q_SUB)
        row0 = pl.multiple_of(qs * q_rows + r, Q_SUB)
        qn = _pair_rms(q_ref[pl.ds(r, Q_SUB), :], qn_ref[...])
        kb = kp_ref[pl.ds(row0, BAND), :]
        vb = vp_ref[pl.ds(row0, BAND), :]
        exists = col >= LEFT_ROWS - row0
        outs = []
        for h in range(2):
            qh = jnp.where(low if h == 0 else ~low, qn, 0.0).astype(_BF16)
            sc = _dot_nt(qh, kb) + bias_ref[h]
            sc = jnp.where(exists, sc, NEG_INF)
            m = jnp.max(sc, axis=-1, keepdims=True)
            p = jnp.exp(sc - m)
            probs = p / jnp.sum(p, axis=-1, keepdims=True)
            outs.append(_dot(probs.astype(_BF16), vb))
        o_ref[pl.ds(r, Q_SUB), :] = jnp.where(low, outs[0], outs[1]).astype(o_ref.dtype)
        return carry

    lax.fori_loop(0, q_rows // Q_SUB, sub, 0)


def _band_bias(rel_bias):
    h, table = rel_bias.shape
    n_diag = Q_SUB + BAND - 1
    flat_lo = BAND - 1 - LEFT_ROWS - (CHUNK - 1)
    flat_hi = n_diag - flat_lo - table
    diag = jnp.concatenate([jnp.broadcast_to(rel_bias[:, :1], (h, flat_lo)), rel_bias,
                            jnp.broadcast_to(rel_bias[:, -1:], (h, flat_hi))], axis=1).astype(_F32)
    rev = diag[:, ::-1]
    width = n_diag + 1
    skew = jnp.tile(rev, (1, Q_SUB + 1))[:, :Q_SUB * width].reshape(h, Q_SUB, width)
    bias = skew[:, ::-1, :BAND]
    r = lax.broadcasted_iota(jnp.int32, (Q_SUB, BAND), 0)
    c = lax.broadcasted_iota(jnp.int32, (Q_SUB, BAND), 1)
    off = c - (r // CHUNK) * CHUNK
    in_band = (off >= 0) & (off < BAND_CHUNKS * CHUNK)
    return jnp.where(in_band[None], bias, NEG_INF)


def _attention(proj3, qn_a, kn_a, rel_bias):
    b, s, _ = proj3.shape
    q_rows = min(s, 1024)
    qn = (jnp.tile(qn_a, 2) * (HEAD_DIM ** -0.5)).reshape(1, LANES)
    kn = jnp.tile(kn_a, 2).reshape(1, LANES)
    pairs = A_HEADS // 2
    kcol, vcol = A_WIDTH // LANES, 2 * A_WIDTH // LANES
    return pl.pallas_call(
        functools.partial(_attn_kernel, q_rows=q_rows),
        grid=(b, pairs, s // q_rows),
        in_specs=[pl.BlockSpec((None, q_rows, LANES), lambda i, p, j: (i, j, p)),
                  pl.BlockSpec((None, s, LANES), lambda i, p, j: (i, 0, kcol + p)),
                  pl.BlockSpec((None, s, LANES), lambda i, p, j: (i, 0, vcol + p)),
                  pl.BlockSpec((1, LANES), lambda i, p, j: (0, 0)),
                  pl.BlockSpec((1, LANES), lambda i, p, j: (0, 0)),
                  pl.BlockSpec((2, Q_SUB, BAND), lambda i, p, j: (p, 0, 0))],
        out_specs=pl.BlockSpec((None, q_rows, LANES), lambda i, p, j: (i, j, p)),
        out_shape=jax.ShapeDtypeStruct((b, s, A_WIDTH), _BF16),
        scratch_shapes=[pltpu.VMEM((s + LEFT_ROWS, LANES), _BF16),
                        pltpu.VMEM((s + LEFT_ROWS, LANES), _BF16)],
        compiler_params=_cparams(3),
        name="attn_a",
    )(proj3, proj3, proj3, qn, kn, _band_bias(rel_bias))


def _swap_halves(t):
    first = (_lane(t.shape) % HEAD_DIM) < (HEAD_DIM // 2)
    return jnp.where(first, pltpu.roll(t, LANES - HEAD_DIM // 2, 1), pltpu.roll(t, HEAD_DIM // 2, 1))


def _retention_kernel(q_ref, k_ref, v_ref, gate_ref, cos_ref, sin_ref, decay_ref, zeta_ref, xi_ref,
                      cd_ref, gn_ref, o_ref, state_ref, *, rows):
    @pl.when(pl.program_id(2) == 0)
    def _():
        state_ref[...] = jnp.zeros_like(state_ref)

    c = RET_CHUNK
    low = _lane((c, LANES)) < HEAD_DIM
    srow = lax.broadcasted_iota(jnp.int32, (LANES, LANES), 0) < HEAD_DIM
    scol = _lane((LANES, LANES)) < HEAD_DIM
    same_head = srow == scol

    for j in range(rows // c):
        sl = slice(j * c, (j + 1) * c)
        cos, sin = cos_ref[sl, :], sin_ref[sl, :]
        q = q_ref[sl, :]
        k = k_ref[sl, :]
        qr = q * cos + _swap_halves(q) * sin
        kr = (k * cos + _swap_halves(k) * sin) * (HEAD_DIM ** -0.5)
        vb = v_ref[sl, :].astype(_BF16)
        qb = qr.astype(_BF16)
        kb = kr.astype(_BF16)
        inner_out = []
        for h in range(2):
            qh = jnp.where(low if h == 0 else ~low, qr, 0.0).astype(_BF16)
            inner = _dot_nt(qh, kb) * decay_ref[h]
            inner_out.append(_dot(inner.astype(_BF16), vb))
        state = state_ref[...]
        cross = _dot(qb, state.astype(_BF16)) * xi_ref[...]
        o = jnp.where(low, inner_out[0], inner_out[1]) + cross
        kz = (kr * zeta_ref[...]).T.astype(_BF16)
        state_ref[...] = cd_ref[...] * state + jnp.where(same_head, _dot(kz, vb), 0.0)
        mu = jnp.where(low,
                       jnp.sum(jnp.where(low, o, 0.0), axis=-1, keepdims=True),
                       jnp.sum(jnp.where(low, 0.0, o), axis=-1, keepdims=True)) * (1.0 / HEAD_DIM)
        dlt = o - mu
        d2 = dlt * dlt
        var = jnp.where(low,
                        jnp.sum(jnp.where(low, d2, 0.0), axis=-1, keepdims=True),
                        jnp.sum(jnp.where(low, 0.0, d2), axis=-1, keepdims=True)) * (1.0 / HEAD_DIM)
        y = (dlt * lax.rsqrt(var + EPS)) * gn_ref[...]
        g = gate_ref[sl, :]
        o_ref[sl, :] = ((g * jax.nn.sigmoid(g)) * y).astype(o_ref.dtype)


def _retention_tables():
    c = RET_CHUNK
    log_g = jnp.log(1.0 - jnp.exp2(-5.0 - jnp.arange(B_HEADS, dtype=_F32)))
    idx = jnp.arange(c, dtype=_F32)
    diff = idx[:, None] - idx[None, :]
    decay = jnp.where(diff >= 0, jnp.exp(log_g[:, None, None] * jnp.maximum(diff, 0.0)), 0.0)
    zeta = jnp.exp(log_g[:, None] * (c - 1 - idx))
    xi = jnp.exp(log_g[:, None] * (idx + 1.0))
    cd = jnp.exp(log_g * c)

    def lanes(tab):
        return jnp.repeat(tab.reshape(B_HEADS // 2, 2, c), HEAD_DIM, axis=1).transpose(0, 2, 1)

    cdm = jnp.repeat(cd.reshape(B_HEADS // 2, 2), HEAD_DIM, axis=1)
    cdm = jnp.broadcast_to(cdm[:, :, None], (B_HEADS // 2, LANES, LANES))
    return decay, lanes(zeta), lanes(xi), cdm


def _retention(proj3, cos, sin, ret_gn_g):
    b, s, _ = proj3.shape
    rows = min(s, 1024)
    pairs = B_HEADS // 2
    base = 3 * A_WIDTH // LANES
    decay, zeta, xi, cdm = _retention_tables()
    cos3, sin3 = cos.reshape(b, s, LANES), sin.reshape(b, s, LANES)
    gn = ret_gn_g.reshape(pairs, 1, LANES)

    def col(off):
        return pl.BlockSpec((None, rows, LANES), lambda i, p, j: (i, j, base + off * pairs + p))

    tab = pl.BlockSpec((None, rows, LANES), lambda i, p, j: (i, j, 0))
    return pl.pallas_call(
        functools.partial(_retention_kernel, rows=rows),
        grid=(b, pairs, s // rows),
        in_specs=[col(0), col(1), col(2), col(3), tab, tab,
                  pl.BlockSpec((2, RET_CHUNK, RET_CHUNK), lambda i, p, j: (p, 0, 0)),
                  pl.BlockSpec((None, RET_CHUNK, LANES), lambda i, p, j: (p, 0, 0)),
                  pl.BlockSpec((None, RET_CHUNK, LANES), lambda i, p, j: (p, 0, 0)),
                  pl.BlockSpec((None, LANES, LANES), lambda i, p, j: (p, 0, 0)),
                  pl.BlockSpec((None, 1, LANES), lambda i, p, j: (p, 0, 0))],
        out_specs=pl.BlockSpec((None, rows, LANES), lambda i, p, j: (i, j, p)),
        out_shape=jax.ShapeDtypeStruct((b, s, B_WIDTH), _BF16),
        scratch_shapes=[pltpu.VMEM((LANES, LANES), _F32)],
        compiler_params=_cparams(3),
        name="retention_b",
    )(proj3, proj3, proj3, proj3, cos3, sin3, decay, zeta, xi, cdm, gn)


def _cross_kernel(q_ref, k_ref, v_ref, qn_ref, o_ref, *, rows):
    low = _lane((Q_SUB, LANES)) < HEAD_DIM

    def sub(j, carry):
        r = pl.multiple_of(j * Q_SUB, Q_SUB)
        for lb in range(C_WIDTH // LANES):
            sl = slice(lb * LANES, (lb + 1) * LANES)
            qn = _pair_rms(q_ref[pl.ds(r, Q_SUB), sl], qn_ref[...])
            kb = k_ref[:, sl]
            vb = v_ref[:, sl]
            outs = []
            for h in range(2):
                qh = jnp.where(low if h == 0 else ~low, qn, 0.0).astype(_BF16)
                sc = _dot_nt(qh, kb)
                m = jnp.max(sc, axis=-1, keepdims=True)
                p = jnp.exp(sc - m)
                probs = p / jnp.sum(p, axis=-1, keepdims=True)
                outs.append(_dot(probs.astype(_BF16), vb))
            o_ref[pl.ds(r, Q_SUB), sl] = jnp.where(low, outs[0], outs[1]).astype(o_ref.dtype)
        return carry

    lax.fori_loop(0, rows // Q_SUB, sub, 0)


def _cross_attention(proj3, kc, vc, qn_c):
    b, s, _ = proj3.shape
    m = kc.shape[1]
    rows = min(s, 1024)
    qn = (jnp.tile(qn_c, 2) * (HEAD_DIM ** -0.5)).reshape(1, LANES)
    qcol = (3 * A_WIDTH + 4 * B_WIDTH) // C_WIDTH
    kv = pl.BlockSpec((None, m, C_WIDTH), lambda i, j: (i, 0, 0))
    return pl.pallas_call(
        functools.partial(_cross_kernel, rows=rows),
        grid=(b, s // rows),
        in_specs=[pl.BlockSpec((None, rows, C_WIDTH), lambda i, j: (i, j, qcol)), kv, kv,
                  pl.BlockSpec((1, LANES), lambda i, j: (0, 0))],
        out_specs=pl.BlockSpec((None, rows, C_WIDTH), lambda i, j: (i, j, 0)),
        out_shape=jax.ShapeDtypeStruct((b, s, C_WIDTH), _BF16),
        compiler_params=_cparams(2),
        name="cross_c",
    )(proj3, kc, vc, qn)


def _out_router_kernel(x_ref, a_ref, b_ref, c_ref, wo_ref, g_ref, wr_ref, br_ref,
                       h_ref, hn_ref, info_ref, cnt_ref, carry_ref):
    @pl.when(pl.program_id(0) == 0)
    def _():
        carry_ref[...] = jnp.zeros_like(carry_ref)

    tm = x_ref.shape[0]
    h = x_ref[...]
    h = h + _dot(a_ref[...], wo_ref[0:A_WIDTH, :])
    h = h + _dot(b_ref[...], wo_ref[A_WIDTH:A_WIDTH + B_WIDTH, :])
    h = h + _dot(c_ref[...], wo_ref[A_WIDTH + B_WIDTH:, :])
    h_ref[...] = h
    ms = jnp.mean(h * h, axis=-1, keepdims=True)
    hn = (h * lax.rsqrt(ms + EPS)) * g_ref[...]
    _rows_to_tiles(hn_ref, hn)
    logits = _dot(hn.astype(_BF16), wr_ref[...]) + br_ref[...]

    lane = _lane((tm, LANES)).astype(_F32)
    big = float(LANES)

    def first_lane(mask):
        return jnp.min(jnp.where(mask, lane, big), axis=-1, keepdims=True)

    gmask = lane < N_GROUPS
    gl = jnp.where(gmask, logits, NEG_INF)
    ge = jnp.exp(gl - jnp.max(gl, axis=-1, keepdims=True))
    gp = ge / jnp.sum(ge, axis=-1, keepdims=True)
    p_group = jnp.max(gp, axis=-1, keepdims=True)
    g_sel = first_lane(gmask & (gp == p_group))
    lo = ROUTE_LANE0 + g_sel * EXPERTS_PER_GROUP
    emask = (lane >= lo) & (lane < lo + EXPERTS_PER_GROUP)
    el = jnp.where(emask, logits, NEG_INF)
    ee = jnp.exp(el - jnp.max(el, axis=-1, keepdims=True))
    ep = ee / jnp.sum(ee, axis=-1, keepdims=True)
    p1 = jnp.max(ep, axis=-1, keepdims=True)
    i1 = first_lane(emask & (ep == p1))
    ep2 = jnp.where(emask & (lane != i1), ep, -1.0)
    p2 = jnp.max(ep2, axis=-1, keepdims=True)
    i2 = first_lane(ep2 == p2)
    den = p1 + p2
    w1 = p_group * (p1 / den)
    w2 = p_group * (p2 / den)
    hit1 = lane == i1
    hit2 = lane == i2
    onehot = jnp.where(hit1 | hit2, 1.0, 0.0)
    r_i = lax.broadcasted_iota(jnp.int32, (tm, tm), 0)
    c_i = lax.broadcasted_iota(jnp.int32, (tm, tm), 1)
    strict = jnp.where(c_i < r_i, 1.0, 0.0).astype(_BF16)
    before = _dot(strict, onehot.astype(_BF16)) + carry_ref[...]
    r1 = jnp.sum(jnp.where(hit1, before, 0.0), axis=-1, keepdims=True)
    r2 = jnp.sum(jnp.where(hit2, before, 0.0), axis=-1, keepdims=True)
    carry_ref[...] = carry_ref[...] + jnp.sum(onehot, axis=0, keepdims=True)
    cnt_ref[...] = carry_ref[...]
    info = jnp.where(lane == 0, w1, 0.0)
    info = jnp.where(lane == 1, w2, info)
    info = jnp.where(lane == 2, i1 - ROUTE_LANE0, info)
    info = jnp.where(lane == 3, i2 - ROUTE_LANE0, info)
    info = jnp.where(lane == 4, r1, info)
    info = jnp.where(lane == 5, r2, info)
    info_ref[...] = info


def _out_router(x2, oa, ob, oc, w_out, ffn_g, w_rg, b_rg, w_re, b_re):
    t, d = x2.shape
    tm = min(t, 512)
    pad = LANES - N_GROUPS - N_EXPERTS
    wr = jnp.concatenate([w_rg, w_re, jnp.zeros((d, pad), _F32)], axis=1).astype(_BF16)
    br = jnp.concatenate([b_rg, b_re, jnp.zeros((pad,), _F32)]).reshape(1, LANES)

    def rows(w):
        return pl.BlockSpec((tm, w), lambda i: (i, 0))

    def whole(r, c):
        return pl.BlockSpec((r, c), lambda i: (0, 0))

    return pl.pallas_call(
        _out_router_kernel,
        grid=(t // tm,),
        in_specs=[rows(d), rows(A_WIDTH), rows(B_WIDTH), rows(C_WIDTH), whole(d, d), whole(1, d),
                  whole(d, LANES), whole(1, LANES)],
        out_specs=[rows(d), pl.BlockSpec((tm * SUBLANES, LANES), lambda i: (i, 0)), rows(LANES),
                   whole(1, LANES)],
        out_shape=[jax.ShapeDtypeStruct((t, d), _F32), jax.ShapeDtypeStruct((t * SUBLANES, LANES), _F32),
                   jax.ShapeDtypeStruct((t, LANES), _F32), jax.ShapeDtypeStruct((1, LANES), _F32)],
        scratch_shapes=[pltpu.VMEM((1, LANES), _F32)],
        compiler_params=_cparams(1),
        name="out_router",
    )(x2, oa, ob, oc, w_out.astype(_BF16), ffn_g.reshape(1, d), wr, br)


DISPATCH_TOKENS = 512
COMBINE_TOKENS = 256


ROW_UNROLL = 8


def _tile_rows(row, count=1):
    start = row * SUBLANES
    if not isinstance(start, int):
        start = pl.multiple_of(start, SUBLANES)
    return pl.ds(start, count * SUBLANES)


def _row_copy(src, s_row, dst, d_row, sem):
    return pltpu.make_async_copy(src.at[_tile_rows(s_row)], dst.at[_tile_rows(d_row)], sem)


def _dispatch_kernel(pad_start_ref, pad_len_ref, used_ref, dest_ref, hn_ref, xs_ref, zero_ref, sem,
                     pad_sem):
    n = hn_ref.shape[0] // SUBLANES

    @pl.when(pl.program_id(0) == 0)
    def _():
        zero_ref[...] = jnp.zeros_like(zero_ref)
        n_blocks = xs_ref.shape[0] // (ROW_BLOCK * SUBLANES)

        def block_copy(blk):
            return pltpu.make_async_copy(zero_ref, xs_ref.at[_tile_rows(blk * ROW_BLOCK, ROW_BLOCK)], pad_sem)

        def put_block(blk, carry):
            block_copy(blk).start()
            return carry

        def done_block(blk, carry):
            block_copy(blk).wait()
            return carry

        lax.fori_loop(used_ref[0], n_blocks, put_block, 0)
        lax.fori_loop(used_ref[0], n_blocks, done_block, 0)
        for e in range(N_EXPERTS):
            def put(r, carry, e=e):
                _row_copy(zero_ref, 0, xs_ref, pad_start_ref[e] + r, pad_sem).start()
                return carry

            def done(r, carry):
                _row_copy(zero_ref, 0, xs_ref, 0, pad_sem).wait()
                return carry

            lax.fori_loop(0, pad_len_ref[e], put, 0)
            lax.fori_loop(0, pad_len_ref[e], done, 0)

    def issue(i, carry):
        for u in range(ROW_UNROLL):
            t = i * ROW_UNROLL + u
            _row_copy(hn_ref, t, xs_ref, dest_ref[2 * t], sem).start(priority=0)
            _row_copy(hn_ref, t, xs_ref, dest_ref[2 * t + 1], sem).start(priority=1)
        return carry

    lax.fori_loop(0, n // ROW_UNROLL, issue, 0)
    for _ in range(2):
        pltpu.make_async_copy(hn_ref, xs_ref.at[_tile_rows(0, n)], sem).wait()


def _dispatch(hn, dest, pad_start, pad_len, n_used, n_rows):
    t = hn.shape[0] // SUBLANES
    n = min(t, DISPATCH_TOKENS)
    return pl.pallas_call(
        _dispatch_kernel,
        grid_spec=pltpu.PrefetchScalarGridSpec(
            num_scalar_prefetch=3,
            grid=(t // n,),
            in_specs=[pl.BlockSpec((2 * n,), lambda i, *_: (i,), memory_space=pltpu.SMEM),
                      pl.BlockSpec((n * SUBLANES, LANES), lambda i, *_: (i, 0))],
            out_specs=pl.BlockSpec(memory_space=pl.ANY),
            scratch_shapes=[pltpu.VMEM((ROW_BLOCK * SUBLANES, LANES), hn.dtype), pltpu.SemaphoreType.DMA,
                            pltpu.SemaphoreType.DMA]),
        out_shape=jax.ShapeDtypeStruct((n_rows * SUBLANES, LANES), hn.dtype),
        compiler_params=_cparams(1),
        name="moe_dispatch",
    )(pad_start, pad_len, n_used, dest, hn)


def _expert_kernel(be_ref, used_ref, x_ref, wg_ref, wu_ref, wd_ref, y_ref, wg_bf, wu_bf, wd_bf):
    i = pl.program_id(0)
    live = i < used_ref[0]
    new_expert = (i == 0) | (be_ref[i] != be_ref[jnp.maximum(i - 1, 0)])

    @pl.when(live & new_expert)
    def _():
        wg_bf[...] = wg_ref[...].astype(_BF16)
        wu_bf[...] = wu_ref[...].astype(_BF16)
        wd_bf[...] = wd_ref[...].astype(_BF16)

    @pl.when(live)
    def _():
        x = jnp.concatenate([_tile_block(x_ref, s, ROW_BLOCK) for s in range(SUBLANES)],
                            axis=-1).astype(_BF16)
        gate = _dot(x, wg_bf[...])
        up = _dot(x, wu_bf[...])
        act = (gate * jax.nn.sigmoid(gate)) * up
        _rows_to_tiles(y_ref, _dot(act.astype(_BF16), wd_bf[...]))

    @pl.when(i >= used_ref[0])
    def _():
        y_ref[...] = jnp.zeros_like(y_ref)


def _experts(xs, block_e, n_used, w_gate, w_up, w_down):
    n_rows, d = xs.shape[0] // SUBLANES, D_MODEL
    n_blocks = n_rows // ROW_BLOCK
    tile_block = (ROW_BLOCK * SUBLANES, LANES)

    def xmap(i, be, used):
        return (jnp.minimum(i, used[0] - 1), 0)

    def wmap(i, be, used):
        return (be[jnp.minimum(i, used[0] - 1)], 0, 0)

    return pl.pallas_call(
        _expert_kernel,
        grid_spec=pltpu.PrefetchScalarGridSpec(
            num_scalar_prefetch=2,
            grid=(n_blocks,),
            in_specs=[pl.BlockSpec(tile_block, xmap),
                      pl.BlockSpec((None, d, D_EXPERT), wmap),
                      pl.BlockSpec((None, d, D_EXPERT), wmap),
                      pl.BlockSpec((None, D_EXPERT, d), wmap)],
            out_specs=pl.BlockSpec(tile_block, lambda i, be, used: (i, 0)),
            scratch_shapes=[pltpu.VMEM((d, D_EXPERT), _BF16), pltpu.VMEM((d, D_EXPERT), _BF16),
                            pltpu.VMEM((D_EXPERT, d), _BF16)]),
        out_shape=jax.ShapeDtypeStruct((n_rows * SUBLANES, LANES), _F32),
        compiler_params=_cparams(1),
        name="moe_experts",
    )(block_e, n_used, xs, w_gate, w_up, w_down)


def _combine_kernel(dest_ref, next_ref, h_ref, info_ref, ys_ref, o_ref, buf_ref, sem):
    n = h_ref.shape[0]
    step = pl.program_id(0)
    slot = step % 2

    def gather(idx_ref, to_slot):
        def issue(i, carry):
            for u in range(ROW_UNROLL):
                t = i * ROW_UNROLL + u
                _row_copy(ys_ref, idx_ref[2 * t], buf_ref.at[to_slot, 0], t,
                          sem.at[to_slot]).start(priority=0)
                _row_copy(ys_ref, idx_ref[2 * t + 1], buf_ref.at[to_slot, 1], t,
                          sem.at[to_slot]).start(priority=1)
            return carry

        lax.fori_loop(0, n // ROW_UNROLL, issue, 0)

    @pl.when(step == 0)
    def _():
        gather(dest_ref, 0)

    @pl.when(step + 1 < pl.num_programs(0))
    def _():
        gather(next_ref, 1 - slot)

    for k in range(2):
        pltpu.make_async_copy(ys_ref.at[_tile_rows(0, n)], buf_ref.at[slot, k], sem.at[slot]).wait()
    info = info_ref[...]
    w0 = info[:, 0:1]
    w1 = info[:, 1:2]
    for s in range(SUBLANES):
        sl = slice(s * LANES, (s + 1) * LANES)
        moe = w0 * _tile_block(buf_ref.at[slot, 0], s, n) + w1 * _tile_block(buf_ref.at[slot, 1], s, n)
        o_ref[:, sl] = h_ref[:, sl] + moe


def _combine(h, info, ys, dest):
    t, d = h.shape
    n = min(t, COMBINE_TOKENS)
    steps = t // n
    return pl.pallas_call(
        _combine_kernel,
        grid=(steps,),
        in_specs=[pl.BlockSpec((2 * n,), lambda i: (i,), memory_space=pltpu.SMEM),
                  pl.BlockSpec((2 * n,), lambda i: (jnp.minimum(i + 1, steps - 1),),
                               memory_space=pltpu.SMEM),
                  pl.BlockSpec((n, d), lambda i: (i, 0)),
                  pl.BlockSpec((n, LANES), lambda i: (i, 0)),
                  pl.BlockSpec(memory_space=pl.ANY)],
        out_specs=pl.BlockSpec((n, d), lambda i: (i, 0)),
        out_shape=jax.ShapeDtypeStruct((t, d), _F32),
        scratch_shapes=[pltpu.VMEM((2, 2, n * SUBLANES, LANES), _F32), pltpu.SemaphoreType.DMA((2,))],
        compiler_params=_cparams(1),
        name="moe_combine",
    )(dest, dest, h, info, ys)


def _moe_layout(info, counts, t):
    counts = counts[0, ROUTE_LANE0:ROUTE_LANE0 + N_EXPERTS].astype(jnp.int32)
    padded = (counts + ROW_BLOCK - 1) // ROW_BLOCK * ROW_BLOCK
    pends = jnp.cumsum(padded)
    pstarts = pends - padded
    eid = info[:, 2:4].astype(jnp.int32)
    rank = info[:, 4:6].astype(jnp.int32)
    experts = jnp.arange(N_EXPERTS, dtype=jnp.int32)
    start_of = jnp.sum(jnp.where(eid[:, :, None] == experts, pstarts, 0), axis=-1)
    dest = (start_of + rank).reshape(-1)
    n_blocks = -(-2 * t // ROW_BLOCK) + N_EXPERTS
    first_row = jnp.arange(n_blocks, dtype=jnp.int32) * ROW_BLOCK
    block_e = jnp.minimum(jnp.sum((pends[None, :] <= first_row[:, None]).astype(jnp.int32), axis=1),
                          N_EXPERTS - 1)
    n_used = (pends[-1:] // ROW_BLOCK).astype(jnp.int32)
    return dest, block_e, n_used, pstarts + counts, padded - counts, n_blocks * ROW_BLOCK


def kernel(x, mem, positions, mix_norm_g, w_in, qn_a, kn_a, rel_bias, ret_gn_g, mem_norm_g, w_mem_kv,
           qn_c, kn_c, w_out, ffn_norm_g, w_router_group, b_router_group, w_router_expert,
           b_router_expert, w_gate, w_up, w_down):
    b, s, d = x.shape
    t = b * s
    x2 = x.reshape(t, d)
    cos, sin = _rope_tables(positions)
    kc, vc = _mem_kv(mem, mem_norm_g, w_mem_kv, kn_c)
    proj3 = _in_proj(x2, mix_norm_g, w_in).reshape(b, s, IN_COLS)
    out_a = _attention(proj3, qn_a, kn_a, rel_bias)
    out_b = _retention(proj3, cos, sin, ret_gn_g)
    out_c = _cross_attention(proj3, kc, vc, qn_c)
    h, hn, info, counts = _out_router(
        x2, out_a.reshape(t, A_WIDTH), out_b.reshape(t, B_WIDTH), out_c.reshape(t, C_WIDTH),
        w_out, ffn_norm_g, w_router_group, b_router_group, w_router_expert, b_router_expert)
    dest, block_e, n_used, pad_start, pad_len, n_rows = _moe_layout(info, counts, t)
    xs = _dispatch(hn, dest, pad_start, pad_len, n_used, n_rows)
    ys = _experts(xs, block_e, n_used, w_gate, w_up, w_down)
    return _combine(h, info, ys, dest).reshape(b, s, d)
```

```python
import functools

import jax
import jax.numpy as jnp
from jax import lax
from jax.experimental import pallas as pl
from jax.experimental.pallas import tpu as pltpu

D_MODEL = 1024
CHUNK = 64
HEAD_DIM = 64
A_HEADS = 8
B_HEADS = 4
C_HEADS = 4
A_WIDTH = A_HEADS * HEAD_DIM
B_WIDTH = B_HEADS * HEAD_DIM
C_WIDTH = C_HEADS * HEAD_DIM
IN_COLS = 3 * A_WIDTH + 4 * B_WIDTH + C_WIDTH
LEFT_CHUNKS = 8
BAND_CHUNKS = LEFT_CHUNKS + 1
MAX_REL_DIST = 128
ROPE_BASE = 10000.0
N_GROUPS = 4
EXPERTS_PER_GROUP = 8
N_EXPERTS = N_GROUPS * EXPERTS_PER_GROUP
D_EXPERT = D_MODEL // 2
EPS = 1e-6
NEG_INF = -1e30

LANES = 128
SUBLANES = 8
assert D_MODEL == SUBLANES * LANES
Q_SUB = 4 * CHUNK
LEFT_ROWS = LEFT_CHUNKS * CHUNK
ATT_Q = 2 * CHUNK
ATT_K = ATT_Q + LEFT_ROWS
ATT_VARIANTS = LEFT_ROWS // ATT_Q + 1
RET_CHUNK = 256
ROW_BLOCK = 256
ROUTE_LANE0 = N_GROUPS
VMEM_LIMIT = 48 * 1024 * 1024

_F32 = jnp.float32
_BF16 = jnp.bfloat16


def _cparams(n_axes):
    return pltpu.CompilerParams(dimension_semantics=("arbitrary",) * n_axes,
                                vmem_limit_bytes=VMEM_LIMIT)


def _dot(a, b):
    return jnp.dot(a, b, preferred_element_type=_F32)


def _dot_nt(a, b):
    return lax.dot_general(a, b, (((1,), (1,)), ((), ())), preferred_element_type=_F32)


def _lane(shape):
    return lax.broadcasted_iota(jnp.int32, shape, len(shape) - 1)


def _pair_rms(t, gain):
    low = _lane(t.shape) < HEAD_DIM
    t2 = t * t
    ms0 = jnp.sum(jnp.where(low, t2, 0.0), axis=-1, keepdims=True) * (1.0 / HEAD_DIM)
    ms1 = jnp.sum(jnp.where(low, 0.0, t2), axis=-1, keepdims=True) * (1.0 / HEAD_DIM)
    r = jnp.where(low, lax.rsqrt(ms0 + EPS), lax.rsqrt(ms1 + EPS))
    return (t * r) * gain


def _rows_to_tiles(ref, val):
    n = val.shape[0]
    for s in range(SUBLANES):
        ref[pl.ds(s, n, stride=SUBLANES), :] = val[:, s * LANES:(s + 1) * LANES]


def _tile_block(ref, s, n):
    return ref[pl.ds(s, n, stride=SUBLANES), :]


def _rope_kernel(pos_ref, inv_ref, sign_ref, cos_ref, sin_ref):
    ang = pos_ref[...].astype(_F32) * inv_ref[...]
    cos_ref[...] = jnp.cos(ang)
    sin_ref[...] = jnp.sin(ang) * sign_ref[...]


def _rope_tables(positions):
    t = positions.size
    half = HEAD_DIM // 2
    inv = ROPE_BASE ** (-jnp.arange(half, dtype=_F32) / half)
    inv128 = jnp.tile(inv, LANES // half).reshape(1, LANES)
    sign = jnp.where((jnp.arange(LANES) % HEAD_DIM) < half, -1.0, 1.0).astype(_F32).reshape(1, LANES)
    pos = jnp.broadcast_to(positions.reshape(t, 1), (t, LANES))
    tm = min(t, 1024)
    row = pl.BlockSpec((tm, LANES), lambda i: (i, 0))
    one = pl.BlockSpec((1, LANES), lambda i: (0, 0))
    return pl.pallas_call(
        _rope_kernel,
        grid=(t // tm,),
        in_specs=[row, one, one],
        out_specs=[row, row],
        out_shape=[jax.ShapeDtypeStruct((t, LANES), _F32)] * 2,
        compiler_params=_cparams(1),
        name="rope_tables",
    )(pos, inv128, sign)


def _mem_kv_kernel(mem_ref, g_ref, w_ref, kn_ref, k_ref, v_ref):
    m = mem_ref[...]
    ms = jnp.mean(m * m, axis=-1, keepdims=True)
    mn = (m * lax.rsqrt(ms + EPS)) * g_ref[...]
    kv = _dot(mn.astype(_BF16), w_ref[...])
    for j in range(C_WIDTH // LANES):
        sl = slice(j * LANES, (j + 1) * LANES)
        k_ref[:, sl] = _pair_rms(kv[:, sl], kn_ref[...]).astype(_BF16)
        v_ref[:, sl] = kv[:, C_WIDTH + j * LANES:C_WIDTH + (j + 1) * LANES].astype(_BF16)


def _mem_kv(mem, mem_norm_g, w_mem_kv, kn_c):
    b, m, d = mem.shape
    kn = jnp.tile(kn_c, 2).reshape(1, LANES)
    out = pl.BlockSpec((None, m, C_WIDTH), lambda i: (i, 0, 0))
    return pl.pallas_call(
        _mem_kv_kernel,
        grid=(b,),
        in_specs=[pl.BlockSpec((None, m, d), lambda i: (i, 0, 0)),
                  pl.BlockSpec((1, d), lambda i: (0, 0)),
                  pl.BlockSpec((d, 2 * C_WIDTH), lambda i: (0, 0)),
                  pl.BlockSpec((1, LANES), lambda i: (0, 0))],
        out_specs=[out, out],
        out_shape=[jax.ShapeDtypeStruct((b, m, C_WIDTH), _BF16)] * 2,
        compiler_params=_cparams(1),
        name="mem_kv",
    )(mem, mem_norm_g.reshape(1, d), w_mem_kv.astype(_BF16), kn)


def _in_proj_kernel(x_ref, g_ref, w_ref, o_ref):
    x = x_ref[...]
    ms = jnp.mean(x * x, axis=-1, keepdims=True)
    xn = ((x * lax.rsqrt(ms + EPS)) * g_ref[...]).astype(_BF16)
    o_ref[...] = _dot(xn, w_ref[...])


def _in_proj(x2, g, w_in):
    t, d = x2.shape
    tm = min(t, 512)
    return pl.pallas_call(
        _in_proj_kernel,
        grid=(t // tm,),
        in_specs=[pl.BlockSpec((tm, d), lambda i: (i, 0)),
                  pl.BlockSpec((1, d), lambda i: (0, 0)),
                  pl.BlockSpec((d, IN_COLS), lambda i: (0, 0))],
        out_specs=pl.BlockSpec((tm, IN_COLS), lambda i: (i, 0)),
        out_shape=jax.ShapeDtypeStruct((t, IN_COLS), _F32),
        compiler_params=_cparams(1),
        name="in_proj",
    )(x2, g.reshape(1, d), w_in.astype(_BF16))


def _attn_kernel(q_ref, k_ref, v_ref, qn_ref, kn_ref, bias_ref, o_ref, kp_ref, vt_ref, st_ref, *, q_rows):
    qs = pl.program_id(2)
    s = k_ref.shape[0]
    fill_rows = min(s, 512)
    left_blocks = LEFT_ROWS // LANES

    @pl.when(qs == 0)
    def _():
        kp_ref[0:LEFT_ROWS, :] = jnp.zeros((LEFT_ROWS, LANES), _BF16)
        for blk in range(left_blocks):
            vt_ref[blk] = jnp.zeros((LANES, LANES), _BF16)

        def fill(i, carry):
            r = pl.multiple_of(i * fill_rows, fill_rows)
            kp_ref[pl.ds(LEFT_ROWS + r, fill_rows), :] = _pair_rms(
                k_ref[pl.ds(r, fill_rows), :], kn_ref[...]).astype(_BF16)
            vt = v_ref[pl.ds(r, fill_rows), :].T
            for j in range(fill_rows // LANES):
                vt_ref[left_blocks + i * (fill_rows // LANES) + j] = vt[:, j * LANES:(j + 1) * LANES].astype(_BF16)
            return carry

        lax.fori_loop(0, s // fill_rows, fill, 0)

    low = _lane((ATT_Q, LANES)) < HEAD_DIM
    tiles_per_step = q_rows // ATT_Q

    def scores(j):
        cp = qs * tiles_per_step + j
        qn = _pair_rms(q_ref[j * ATT_Q:(j + 1) * ATT_Q, :], qn_ref[...])
        q2 = jnp.concatenate([jnp.where(low, qn, 0.0), jnp.where(low, 0.0, qn)], axis=0).astype(_BF16)
        kb = kp_ref[pl.ds(pl.multiple_of(cp * ATT_Q, ATT_Q), ATT_K), :]
        st_ref[j % 2] = _dot_nt(kb, q2) + bias_ref[jnp.minimum(cp, ATT_VARIANTS - 1)]

    def finish(j):
        cp = qs * tiles_per_step + j
        st = st_ref[j % 2]
        m = jnp.max(st, axis=0, keepdims=True)
        p = jnp.exp(st - m)
        inv = 1.0 / jnp.sum(p, axis=0, keepdims=True)
        vt = jnp.concatenate([vt_ref[cp + kb_i] for kb_i in range(ATT_K // LANES)], axis=1)
        ot = _dot(vt, p.astype(_BF16))
        out_t = jnp.concatenate([ot[0:HEAD_DIM, 0:ATT_Q] * inv[:, 0:ATT_Q],
                                 ot[HEAD_DIM:, ATT_Q:] * inv[:, ATT_Q:]], axis=0)
        o_ref[j * ATT_Q:(j + 1) * ATT_Q, :] = out_t.T.astype(o_ref.dtype)

    scores(0)
    for j in range(tiles_per_step):
        if j + 1 < tiles_per_step:
            scores(j + 1)
        finish(j)


def _toeplitz_bias(rel_bias, q_len, k_len):
    h, table = rel_bias.shape
    n_diag = q_len + k_len - 1
    flat_lo = k_len - 1 - LEFT_ROWS - (CHUNK - 1)
    flat_hi = n_diag - flat_lo - table
    rev = jnp.concatenate([jnp.broadcast_to(rel_bias[:, -1:], (h, flat_hi)), rel_bias[:, ::-1],
                           jnp.broadcast_to(rel_bias[:, :1], (h, flat_lo))], axis=1).astype(_F32)
    flat = jnp.tile(rev, (1, q_len + 1))
    pitch = n_diag - 1
    skew = flat[:, q_len - 1:q_len - 1 + q_len * pitch].reshape(h, q_len, pitch)
    return skew[:, :, :k_len]


def _attn_bias(rel_bias):
    h = rel_bias.shape[0]
    bias = _toeplitz_bias(rel_bias, ATT_Q, ATT_K)
    q = lax.broadcasted_iota(jnp.int32, (ATT_Q, ATT_K), 0)
    k = lax.broadcasted_iota(jnp.int32, (ATT_Q, ATT_K), 1)
    off = k // CHUNK - q // CHUNK
    in_band = (off >= 0) & (off < BAND_CHUNKS)
    first_key = LEFT_ROWS - ATT_Q * jnp.arange(ATT_VARIANTS, dtype=jnp.int32)
    ok = in_band[None] & (k[None] >= first_key[:, None, None])
    full = jnp.where(ok[None], bias[:, None], NEG_INF)
    full = full.reshape(h // 2, 2, ATT_VARIANTS, ATT_Q, ATT_K)
    return full.transpose(0, 2, 4, 1, 3).reshape(h // 2, ATT_VARIANTS, ATT_K, 2 * ATT_Q)


def _attention(proj3, qn_a, kn_a, rel_bias):
    b, s, _ = proj3.shape
    q_rows = min(s, 1024)
    qn = (jnp.tile(qn_a, 2) * (HEAD_DIM ** -0.5)).reshape(1, LANES)
    kn = jnp.tile(kn_a, 2).reshape(1, LANES)
    pairs = A_HEADS // 2
    kcol, vcol = A_WIDTH // LANES, 2 * A_WIDTH // LANES
    return pl.pallas_call(
        functools.partial(_attn_kernel, q_rows=q_rows),
        grid=(b, pairs, s // q_rows),
        in_specs=[pl.BlockSpec((None, q_rows, LANES), lambda i, p, j: (i, j, p)),
                  pl.BlockSpec((None, s, LANES), lambda i, p, j: (i, 0, kcol + p)),
                  pl.BlockSpec((None, s, LANES), lambda i, p, j: (i, 0, vcol + p)),
                  pl.BlockSpec((1, LANES), lambda i, p, j: (0, 0)),
                  pl.BlockSpec((1, LANES), lambda i, p, j: (0, 0)),
                  pl.BlockSpec((None, ATT_VARIANTS, ATT_K, 2 * ATT_Q), lambda i, p, j: (p, 0, 0, 0))],
        out_specs=pl.BlockSpec((None, q_rows, LANES), lambda i, p, j: (i, j, p)),
        out_shape=jax.ShapeDtypeStruct((b, s, A_WIDTH), _BF16),
        scratch_shapes=[pltpu.VMEM((s + LEFT_ROWS, LANES), _BF16),
                        pltpu.VMEM(((s + LEFT_ROWS) // LANES, LANES, LANES), _BF16),
                        pltpu.VMEM((2, ATT_K, 2 * ATT_Q), _F32)],
        compiler_params=_cparams(3),
        name="attn_a",
    )(proj3, proj3, proj3, qn, kn, _attn_bias(rel_bias))


def _swap_halves(t):
    first = (_lane(t.shape) % HEAD_DIM) < (HEAD_DIM // 2)
    return jnp.where(first, pltpu.roll(t, LANES - HEAD_DIM // 2, 1), pltpu.roll(t, HEAD_DIM // 2, 1))


def _retention_kernel(q_ref, k_ref, v_ref, gate_ref, cos_ref, sin_ref, decay_ref, zeta_ref, xi_ref,
                      cd_ref, gn_ref, o_ref, state_ref, *, rows):
    @pl.when(pl.program_id(2) == 0)
    def _():
        state_ref[...] = jnp.zeros_like(state_ref)

    c = RET_CHUNK
    low = _lane((c, LANES)) < HEAD_DIM
    srow = lax.broadcasted_iota(jnp.int32, (LANES, LANES), 0) < HEAD_DIM
    scol = _lane((LANES, LANES)) < HEAD_DIM
    same_head = srow == scol

    for j in range(rows // c):
        sl = slice(j * c, (j + 1) * c)
        cos, sin = cos_ref[sl, :], sin_ref[sl, :]
        q = q_ref[sl, :]
        k = k_ref[sl, :]
        qr = q * cos + _swap_halves(q) * sin
        kr = (k * cos + _swap_halves(k) * sin) * (HEAD_DIM ** -0.5)
        vb = v_ref[sl, :].astype(_BF16)
        qb = qr.astype(_BF16)
        kb = kr.astype(_BF16)
        inner_out = []
        for h in range(2):
            qh = jnp.where(low if h == 0 else ~low, qr, 0.0).astype(_BF16)
            inner = _dot_nt(qh, kb) * decay_ref[h]
            inner_out.append(_dot(inner.astype(_BF16), vb))
        state = state_ref[...]
        cross = _dot(qb, state.astype(_BF16)) * xi_ref[...]
        o = jnp.where(low, inner_out[0], inner_out[1]) + cross
        kz = (kr * zeta_ref[...]).T.astype(_BF16)
        state_ref[...] = cd_ref[...] * state + jnp.where(same_head, _dot(kz, vb), 0.0)
        mu = jnp.where(low,
                       jnp.sum(jnp.where(low, o, 0.0), axis=-1, keepdims=True),
                       jnp.sum(jnp.where(low, 0.0, o), axis=-1, keepdims=True)) * (1.0 / HEAD_DIM)
        dlt = o - mu
        d2 = dlt * dlt
        var = jnp.where(low,
                        jnp.sum(jnp.where(low, d2, 0.0), axis=-1, keepdims=True),
                        jnp.sum(jnp.where(low, 0.0, d2), axis=-1, keepdims=True)) * (1.0 / HEAD_DIM)
        y = (dlt * lax.rsqrt(var + EPS)) * gn_ref[...]
        g = gate_ref[sl, :]
        o_ref[sl, :] = ((g * jax.nn.sigmoid(g)) * y).astype(o_ref.dtype)


def _retention_tables():
    c = RET_CHUNK
    log_g = jnp.log(1.0 - jnp.exp2(-5.0 - jnp.arange(B_HEADS, dtype=_F32)))
    idx = jnp.arange(c, dtype=_F32)
    diff = idx[:, None] - idx[None, :]
    decay = jnp.where(diff >= 0, jnp.exp(log_g[:, None, None] * jnp.maximum(diff, 0.0)), 0.0)
    zeta = jnp.exp(log_g[:, None] * (c - 1 - idx))
    xi = jnp.exp(log_g[:, None] * (idx + 1.0))
    cd = jnp.exp(log_g * c)

    def lanes(tab):
        return jnp.repeat(tab.reshape(B_HEADS // 2, 2, c), HEAD_DIM, axis=1).transpose(0, 2, 1)

    cdm = jnp.repeat(cd.reshape(B_HEADS // 2, 2), HEAD_DIM, axis=1)
    cdm = jnp.broadcast_to(cdm[:, :, None], (B_HEADS // 2, LANES, LANES))
    return decay, lanes(zeta), lanes(xi), cdm


def _retention(proj3, cos, sin, ret_gn_g):
    b, s, _ = proj3.shape
    rows = min(s, 1024)
    pairs = B_HEADS // 2
    base = 3 * A_WIDTH // LANES
    decay, zeta, xi, cdm = _retention_tables()
    cos3, sin3 = cos.reshape(b, s, LANES), sin.reshape(b, s, LANES)
    gn = ret_gn_g.reshape(pairs, 1, LANES)

    def col(off):
        return pl.BlockSpec((None, rows, LANES), lambda i, p, j: (i, j, base + off * pairs + p))

    tab = pl.BlockSpec((None, rows, LANES), lambda i, p, j: (i, j, 0))
    return pl.pallas_call(
        functools.partial(_retention_kernel, rows=rows),
        grid=(b, pairs, s // rows),
        in_specs=[col(0), col(1), col(2), col(3), tab, tab,
                  pl.BlockSpec((2, RET_CHUNK, RET_CHUNK), lambda i, p, j: (p, 0, 0)),
                  pl.BlockSpec((None, RET_CHUNK, LANES), lambda i, p, j: (p, 0, 0)),
                  pl.BlockSpec((None, RET_CHUNK, LANES), lambda i, p, j: (p, 0, 0)),
                  pl.BlockSpec((None, LANES, LANES), lambda i, p, j: (p, 0, 0)),
                  pl.BlockSpec((None, 1, LANES), lambda i, p, j: (p, 0, 0))],
        out_specs=pl.BlockSpec((None, rows, LANES), lambda i, p, j: (i, j, p)),
        out_shape=jax.ShapeDtypeStruct((b, s, B_WIDTH), _BF16),
        scratch_shapes=[pltpu.VMEM((LANES, LANES), _F32)],
        compiler_params=_cparams(3),
        name="retention_b",
    )(proj3, proj3, proj3, proj3, cos3, sin3, decay, zeta, xi, cdm, gn)


def _cross_kernel(q_ref, k_ref, v_ref, qn_ref, o_ref, *, rows):
    low = _lane((Q_SUB, LANES)) < HEAD_DIM

    def sub(j, carry):
        r = pl.multiple_of(j * Q_SUB, Q_SUB)
        for lb in range(C_WIDTH // LANES):
            sl = slice(lb * LANES, (lb + 1) * LANES)
            qn = _pair_rms(q_ref[pl.ds(r, Q_SUB), sl], qn_ref[...])
            kb = k_ref[:, sl]
            vb = v_ref[:, sl]
            outs = []
            for h in range(2):
                qh = jnp.where(low if h == 0 else ~low, qn, 0.0).astype(_BF16)
                sc = _dot_nt(qh, kb)
                m = jnp.max(sc, axis=-1, keepdims=True)
                p = jnp.exp(sc - m)
                probs = p / jnp.sum(p, axis=-1, keepdims=True)
                outs.append(_dot(probs.astype(_BF16), vb))
            o_ref[pl.ds(r, Q_SUB), sl] = jnp.where(low, outs[0], outs[1]).astype(o_ref.dtype)
        return carry

    lax.fori_loop(0, rows // Q_SUB, sub, 0)


def _cross_attention(proj3, kc, vc, qn_c):
    b, s, _ = proj3.shape
    m = kc.shape[1]
    rows = min(s, 1024)
    qn = (jnp.tile(qn_c, 2) * (HEAD_DIM ** -0.5)).reshape(1, LANES)
    qcol = (3 * A_WIDTH + 4 * B_WIDTH) // C_WIDTH
    kv = pl.BlockSpec((None, m, C_WIDTH), lambda i, j: (i, 0, 0))
    return pl.pallas_call(
        functools.partial(_cross_kernel, rows=rows),
        grid=(b, s // rows),
        in_specs=[pl.BlockSpec((None, rows, C_WIDTH), lambda i, j: (i, j, qcol)), kv, kv,
                  pl.BlockSpec((1, LANES), lambda i, j: (0, 0))],
        out_specs=pl.BlockSpec((None, rows, C_WIDTH), lambda i, j: (i, j, 0)),
        out_shape=jax.ShapeDtypeStruct((b, s, C_WIDTH), _BF16),
        compiler_params=_cparams(2),
        name="cross_c",
    )(proj3, kc, vc, qn)


def _out_router_kernel(x_ref, a_ref, b_ref, c_ref, wo_ref, g_ref, wr_ref, br_ref,
                       h_ref, hn_ref, info_ref, cnt_ref, carry_ref):
    @pl.when(pl.program_id(0) == 0)
    def _():
        carry_ref[...] = jnp.zeros_like(carry_ref)

    tm = x_ref.shape[0]
    h = x_ref[...]
    h = h + _dot(a_ref[...], wo_ref[0:A_WIDTH, :])
    h = h + _dot(b_ref[...], wo_ref[A_WIDTH:A_WIDTH + B_WIDTH, :])
    h = h + _dot(c_ref[...], wo_ref[A_WIDTH + B_WIDTH:, :])
    h_ref[...] = h
    ms = jnp.mean(h * h, axis=-1, keepdims=True)
    hn = (h * lax.rsqrt(ms + EPS)) * g_ref[...]
    _rows_to_tiles(hn_ref, hn)
    logits = _dot(hn.astype(_BF16), wr_ref[...]) + br_ref[...]

    lane = _lane((tm, LANES)).astype(_F32)
    big = float(LANES)

    def first_lane(mask):
        return jnp.min(jnp.where(mask, lane, big), axis=-1, keepdims=True)

    gmask = lane < N_GROUPS
    gl = jnp.where(gmask, logits, NEG_INF)
    ge = jnp.exp(gl - jnp.max(gl, axis=-1, keepdims=True))
    gp = ge / jnp.sum(ge, axis=-1, keepdims=True)
    p_group = jnp.max(gp, axis=-1, keepdims=True)
    g_sel = first_lane(gmask & (gp == p_group))
    lo = ROUTE_LANE0 + g_sel * EXPERTS_PER_GROUP
    emask = (lane >= lo) & (lane < lo + EXPERTS_PER_GROUP)
    el = jnp.where(emask, logits, NEG_INF)
    ee = jnp.exp(el - jnp.max(el, axis=-1, keepdims=True))
    ep = ee / jnp.sum(ee, axis=-1, keepdims=True)
    p1 = jnp.max(ep, axis=-1, keepdims=True)
    i1 = first_lane(emask & (ep == p1))
    ep2 = jnp.where(emask & (lane != i1), ep, -1.0)
    p2 = jnp.max(ep2, axis=-1, keepdims=True)
    i2 = first_lane(ep2 == p2)
    den = p1 + p2
    w1 = p_group * (p1 / den)
    w2 = p_group * (p2 / den)
    hit1 = lane == i1
    hit2 = lane == i2
    onehot = jnp.where(hit1 | hit2, 1.0, 0.0)
    r_i = lax.broadcasted_iota(jnp.int32, (tm, tm), 0)
    c_i = lax.broadcasted_iota(jnp.int32, (tm, tm), 1)
    strict = jnp.where(c_i < r_i, 1.0, 0.0).astype(_BF16)
    before = _dot(strict, onehot.astype(_BF16)) + carry_ref[...]
    r1 = jnp.sum(jnp.where(hit1, before, 0.0), axis=-1, keepdims=True)
    r2 = jnp.sum(jnp.where(hit2, before, 0.0), axis=-1, keepdims=True)
    carry_ref[...] = carry_ref[...] + jnp.sum(onehot, axis=0, keepdims=True)
    cnt_ref[...] = carry_ref[...]
    info = jnp.where(lane == 0, w1, 0.0)
    info = jnp.where(lane == 1, w2, info)
    info = jnp.where(lane == 2, i1 - ROUTE_LANE0, info)
    info = jnp.where(lane == 3, i2 - ROUTE_LANE0, info)
    info = jnp.where(lane == 4, r1, info)
    info = jnp.where(lane == 5, r2, info)
    info_ref[...] = info


def _out_router(x2, oa, ob, oc, w_out, ffn_g, w_rg, b_rg, w_re, b_re):
    t, d = x2.shape
    tm = min(t, 512)
    pad = LANES - N_GROUPS - N_EXPERTS
    wr = jnp.concatenate([w_rg, w_re, jnp.zeros((d, pad), _F32)], axis=1).astype(_BF16)
    br = jnp.concatenate([b_rg, b_re, jnp.zeros((pad,), _F32)]).reshape(1, LANES)

    def rows(w):
        return pl.BlockSpec((tm, w), lambda i: (i, 0))

    def whole(r, c):
        return pl.BlockSpec((r, c), lambda i: (0, 0))

    return pl.pallas_call(
        _out_router_kernel,
        grid=(t // tm,),
        in_specs=[rows(d), rows(A_WIDTH), rows(B_WIDTH), rows(C_WIDTH), whole(d, d), whole(1, d),
                  whole(d, LANES), whole(1, LANES)],
        out_specs=[rows(d), pl.BlockSpec((tm * SUBLANES, LANES), lambda i: (i, 0)), rows(LANES),
                   whole(1, LANES)],
        out_shape=[jax.ShapeDtypeStruct((t, d), _F32), jax.ShapeDtypeStruct((t * SUBLANES, LANES), _F32),
                   jax.ShapeDtypeStruct((t, LANES), _F32), jax.ShapeDtypeStruct((1, LANES), _F32)],
        scratch_shapes=[pltpu.VMEM((1, LANES), _F32)],
        compiler_params=_cparams(1),
        name="out_router",
    )(x2, oa, ob, oc, w_out.astype(_BF16), ffn_g.reshape(1, d), wr, br)


DISPATCH_TOKENS = 512
COMBINE_TOKENS = 256


ROW_UNROLL = 8


def _tile_rows(row, count=1):
    start = row * SUBLANES
    if not isinstance(start, int):
        start = pl.multiple_of(start, SUBLANES)
    return pl.ds(start, count * SUBLANES)


def _row_copy(src, s_row, dst, d_row, sem):
    return pltpu.make_async_copy(src.at[_tile_rows(s_row)], dst.at[_tile_rows(d_row)], sem)


def _dispatch_kernel(pad_start_ref, pad_len_ref, used_ref, dest_ref, hn_ref, xs_ref, zero_ref, sem,
                     pad_sem):
    n = hn_ref.shape[0] // SUBLANES

    @pl.when(pl.program_id(0) == 0)
    def _():
        zero_ref[...] = jnp.zeros_like(zero_ref)
        n_blocks = xs_ref.shape[0] // (ROW_BLOCK * SUBLANES)

        def block_copy(blk):
            return pltpu.make_async_copy(zero_ref, xs_ref.at[_tile_rows(blk * ROW_BLOCK, ROW_BLOCK)], pad_sem)

        def put_block(blk, carry):
            block_copy(blk).start()
            return carry

        def done_block(blk, carry):
            block_copy(blk).wait()
            return carry

        lax.fori_loop(used_ref[0], n_blocks, put_block, 0)
        lax.fori_loop(used_ref[0], n_blocks, done_block, 0)
        for e in range(N_EXPERTS):
            def put(r, carry, e=e):
                _row_copy(zero_ref, 0, xs_ref, pad_start_ref[e] + r, pad_sem).start()
                return carry

            def done(r, carry):
                _row_copy(zero_ref, 0, xs_ref, 0, pad_sem).wait()
                return carry

            lax.fori_loop(0, pad_len_ref[e], put, 0)
            lax.fori_loop(0, pad_len_ref[e], done, 0)

    def issue(i, carry):
        for u in range(ROW_UNROLL):
            t = i * ROW_UNROLL + u
            _row_copy(hn_ref, t, xs_ref, dest_ref[2 * t], sem).start(priority=0)
            _row_copy(hn_ref, t, xs_ref, dest_ref[2 * t + 1], sem).start(priority=1)
        return carry

    lax.fori_loop(0, n // ROW_UNROLL, issue, 0)
    for _ in range(2):
        pltpu.make_async_copy(hn_ref, xs_ref.at[_tile_rows(0, n)], sem).wait()


def _dispatch(hn, dest, pad_start, pad_len, n_used, n_rows):
    t = hn.shape[0] // SUBLANES
    n = min(t, DISPATCH_TOKENS)
    return pl.pallas_call(
        _dispatch_kernel,
        grid_spec=pltpu.PrefetchScalarGridSpec(
            num_scalar_prefetch=3,
            grid=(t // n,),
            in_specs=[pl.BlockSpec((2 * n,), lambda i, *_: (i,), memory_space=pltpu.SMEM),
                      pl.BlockSpec((n * SUBLANES, LANES), lambda i, *_: (i, 0))],
            out_specs=pl.BlockSpec(memory_space=pl.ANY),
            scratch_shapes=[pltpu.VMEM((ROW_BLOCK * SUBLANES, LANES), hn.dtype), pltpu.SemaphoreType.DMA,
                            pltpu.SemaphoreType.DMA]),
        out_shape=jax.ShapeDtypeStruct((n_rows * SUBLANES, LANES), hn.dtype),
        compiler_params=_cparams(1),
        name="moe_dispatch",
    )(pad_start, pad_len, n_used, dest, hn)


def _expert_kernel(be_ref, used_ref, x_ref, wg_ref, wu_ref, wd_ref, y_ref, wg_bf, wu_bf, wd_bf):
    i = pl.program_id(0)
    live = i < used_ref[0]
    new_expert = (i == 0) | (be_ref[i] != be_ref[jnp.maximum(i - 1, 0)])

    @pl.when(live & new_expert)
    def _():
        wg_bf[...] = wg_ref[...].astype(_BF16)
        wu_bf[...] = wu_ref[...].astype(_BF16)
        wd_bf[...] = wd_ref[...].astype(_BF16)

    @pl.when(live)
    def _():
        x = jnp.concatenate([_tile_block(x_ref, s, ROW_BLOCK) for s in range(SUBLANES)],
                            axis=-1).astype(_BF16)
        gate = _dot(x, wg_bf[...])
        up = _dot(x, wu_bf[...])
        act = (gate * jax.nn.sigmoid(gate)) * up
        _rows_to_tiles(y_ref, _dot(act.astype(_BF16), wd_bf[...]))

    @pl.when(i >= used_ref[0])
    def _():
        y_ref[...] = jnp.zeros_like(y_ref)


def _experts(xs, block_e, n_used, w_gate, w_up, w_down):
    n_rows, d = xs.shape[0] // SUBLANES, D_MODEL
    n_blocks = n_rows // ROW_BLOCK
    tile_block = (ROW_BLOCK * SUBLANES, LANES)

    def xmap(i, be, used):
        return (jnp.minimum(i, used[0] - 1), 0)

    def wmap(i, be, used):
        return (be[jnp.minimum(i, used[0] - 1)], 0, 0)

    return pl.pallas_call(
        _expert_kernel,
        grid_spec=pltpu.PrefetchScalarGridSpec(
            num_scalar_prefetch=2,
            grid=(n_blocks,),
            in_specs=[pl.BlockSpec(tile_block, xmap),
                      pl.BlockSpec((None, d, D_EXPERT), wmap),
                      pl.BlockSpec((None, d, D_EXPERT), wmap),
                      pl.BlockSpec((None, D_EXPERT, d), wmap)],
            out_specs=pl.BlockSpec(tile_block, lambda i, be, used: (i, 0)),
            scratch_shapes=[pltpu.VMEM((d, D_EXPERT), _BF16), pltpu.VMEM((d, D_EXPERT), _BF16),
                            pltpu.VMEM((D_EXPERT, d), _BF16)]),
        out_shape=jax.ShapeDtypeStruct((n_rows * SUBLANES, LANES), _F32),
        compiler_params=_cparams(1),
        name="moe_experts",
    )(block_e, n_used, xs, w_gate, w_up, w_down)


def _combine_kernel(dest_ref, next_ref, h_ref, info_ref, ys_ref, o_ref, buf_ref, sem):
    n = h_ref.shape[0]
    step = pl.program_id(0)
    slot = step % 2

    def gather(idx_ref, to_slot):
        def issue(i, carry):
            for u in range(ROW_UNROLL):
                t = i * ROW_UNROLL + u
                _row_copy(ys_ref, idx_ref[2 * t], buf_ref.at[to_slot, 0], t,
                          sem.at[to_slot]).start(priority=0)
                _row_copy(ys_ref, idx_ref[2 * t + 1], buf_ref.at[to_slot, 1], t,
                          sem.at[to_slot]).start(priority=1)
            return carry

        lax.fori_loop(0, n // ROW_UNROLL, issue, 0)

    @pl.when(step == 0)
    def _():
        gather(dest_ref, 0)

    @pl.when(step + 1 < pl.num_programs(0))
    def _():
        gather(next_ref, 1 - slot)

    for k in range(2):
        pltpu.make_async_copy(ys_ref.at[_tile_rows(0, n)], buf_ref.at[slot, k], sem.at[slot]).wait()
    info = info_ref[...]
    w0 = info[:, 0:1]
    w1 = info[:, 1:2]
    for s in range(SUBLANES):
        sl = slice(s * LANES, (s + 1) * LANES)
        moe = w0 * _tile_block(buf_ref.at[slot, 0], s, n) + w1 * _tile_block(buf_ref.at[slot, 1], s, n)
        o_ref[:, sl] = h_ref[:, sl] + moe


def _combine(h, info, ys, dest):
    t, d = h.shape
    n = min(t, COMBINE_TOKENS)
    steps = t // n
    return pl.pallas_call(
        _combine_kernel,
        grid=(steps,),
        in_specs=[pl.BlockSpec((2 * n,), lambda i: (i,), memory_space=pltpu.SMEM),
                  pl.BlockSpec((2 * n,), lambda i: (jnp.minimum(i + 1, steps - 1),),
                               memory_space=pltpu.SMEM),
                  pl.BlockSpec((n, d), lambda i: (i, 0)),
                  pl.BlockSpec((n, LANES), lambda i: (i, 0)),
                  pl.BlockSpec(memory_space=pl.ANY)],
        out_specs=pl.BlockSpec((n, d), lambda i: (i, 0)),
        out_shape=jax.ShapeDtypeStruct((t, d), _F32),
        scratch_shapes=[pltpu.VMEM((2, 2, n * SUBLANES, LANES), _F32), pltpu.SemaphoreType.DMA((2,))],
        compiler_params=_cparams(1),
        name="moe_combine",
    )(dest, dest, h, info, ys)


def _moe_layout(info, counts, t):
    counts = counts[0, ROUTE_LANE0:ROUTE_LANE0 + N_EXPERTS].astype(jnp.int32)
    padded = (counts + ROW_BLOCK - 1) // ROW_BLOCK * ROW_BLOCK
    pends = jnp.cumsum(padded)
    pstarts = pends - padded
    eid = info[:, 2:4].astype(jnp.int32)
    rank = info[:, 4:6].astype(jnp.int32)
    experts = jnp.arange(N_EXPERTS, dtype=jnp.int32)
    start_of = jnp.sum(jnp.where(eid[:, :, None] == experts, pstarts, 0), axis=-1)
    dest = (start_of + rank).reshape(-1)
    n_blocks = -(-2 * t // ROW_BLOCK) + N_EXPERTS
    first_row = jnp.arange(n_blocks, dtype=jnp.int32) * ROW_BLOCK
    block_e = jnp.minimum(jnp.sum((pends[None, :] <= first_row[:, None]).astype(jnp.int32), axis=1),
                          N_EXPERTS - 1)
    n_used = (pends[-1:] // ROW_BLOCK).astype(jnp.int32)
    return dest, block_e, n_used, pstarts + counts, padded - counts, n_blocks * ROW_BLOCK


def kernel(x, mem, positions, mix_norm_g, w_in, qn_a, kn_a, rel_bias, ret_gn_g, mem_norm_g, w_mem_kv,
           qn_c, kn_c, w_out, ffn_norm_g, w_router_group, b_router_group, w_router_expert,
           b_router_expert, w_gate, w_up, w_down):
    b, s, d = x.shape
    t = b * s
    x2 = x.reshape(t, d)
    cos, sin = _rope_tables(positions)
    kc, vc = _mem_kv(mem, mem_norm_g, w_mem_kv, kn_c)
    proj3 = _in_proj(x2, mix_norm_g, w_in).reshape(b, s, IN_COLS)
    out_a = _attention(proj3, qn_a, kn_a, rel_bias)
    out_b = _retention(proj3, cos, sin, ret_gn_g)
    out_c = _cross_attention(proj3, kc, vc, qn_c)
    h, hn, info, counts = _out_router(
        x2, out_a.reshape(t, A_WIDTH), out_b.reshape(t, B_WIDTH), out_c.reshape(t, C_WIDTH),
        w_out, ffn_norm_g, w_router_group, b_router_group, w_router_expert, b_router_expert)
    dest, block_e, n_used, pad_start, pad_len, n_rows = _moe_layout(info, counts, t)
    xs = _dispatch(hn, dest, pad_start, pad_len, n_used, n_rows)
    ys = _experts(xs, block_e, n_used, w_gate, w_up, w_down)
    return _combine(h, info, ys, dest).reshape(b, s, d)
```

```python
import functools

import jax
import jax.numpy as jnp
from jax import lax
from jax.experimental import pallas as pl
from jax.experimental.pallas import tpu as pltpu

D_MODEL = 1024
CHUNK = 64
HEAD_DIM = 64
A_HEADS = 8
B_HEADS = 4
C_HEADS = 4
A_WIDTH = A_HEADS * HEAD_DIM
B_WIDTH = B_HEADS * HEAD_DIM
C_WIDTH = C_HEADS * HEAD_DIM
IN_COLS = 3 * A_WIDTH + 4 * B_WIDTH + C_WIDTH
LEFT_CHUNKS = 8
BAND_CHUNKS = LEFT_CHUNKS + 1
MAX_REL_DIST = 128
ROPE_BASE = 10000.0
N_GROUPS = 4
EXPERTS_PER_GROUP = 8
N_EXPERTS = N_GROUPS * EXPERTS_PER_GROUP
D_EXPERT = D_MODEL // 2
EPS = 1e-6
NEG_INF = -1e30

LANES = 128
SUBLANES = 8
assert D_MODEL == SUBLANES * LANES
LEFT_ROWS = LEFT_CHUNKS * CHUNK
ATT_Q = 2 * CHUNK
ATT_K = ATT_Q + LEFT_ROWS
ATT_VARIANTS = LEFT_ROWS // ATT_Q + 1
RET_CHUNK = 256
ROW_BLOCK = 512
ROUTE_LANE0 = N_GROUPS
VMEM_LIMIT = 48 * 1024 * 1024

_F32 = jnp.float32
_BF16 = jnp.bfloat16


def _cparams(n_axes):
    return pltpu.CompilerParams(dimension_semantics=("arbitrary",) * n_axes,
                                vmem_limit_bytes=VMEM_LIMIT)


def _dot(a, b):
    return jnp.dot(a, b, preferred_element_type=_F32)


def _dot_nt(a, b):
    return lax.dot_general(a, b, (((1,), (1,)), ((), ())), preferred_element_type=_F32)


def _lane(shape):
    return lax.broadcasted_iota(jnp.int32, shape, len(shape) - 1)


def _pair_rms(t, gain):
    low = _lane(t.shape) < HEAD_DIM
    t2 = t * t
    ms0 = jnp.sum(jnp.where(low, t2, 0.0), axis=-1, keepdims=True) * (1.0 / HEAD_DIM)
    ms1 = jnp.sum(jnp.where(low, 0.0, t2), axis=-1, keepdims=True) * (1.0 / HEAD_DIM)
    r = jnp.where(low, lax.rsqrt(ms0 + EPS), lax.rsqrt(ms1 + EPS))
    return (t * r) * gain


def _rows_to_tiles(ref, val, row0=0):
    n = val.shape[0]
    for s in range(SUBLANES):
        ref[pl.ds(row0 * SUBLANES + s, n, stride=SUBLANES), :] = val[:, s * LANES:(s + 1) * LANES]


def _tile_block(ref, s, n, row0=0):
    return ref[pl.ds(row0 * SUBLANES + s, n, stride=SUBLANES), :]


def _tiles_to_rows(ref, n, row0=0):
    return jnp.concatenate([_tile_block(ref, s, n, row0) for s in range(SUBLANES)], axis=-1)


def _rope_kernel(pos_ref, inv_ref, sign_ref, cos_ref, sin_ref):
    ang = pos_ref[...].astype(_F32) * inv_ref[...]
    cos_ref[...] = jnp.cos(ang)
    sin_ref[...] = jnp.sin(ang) * sign_ref[...]


def _rope_tables(positions):
    t = positions.size
    half = HEAD_DIM // 2
    inv = ROPE_BASE ** (-jnp.arange(half, dtype=_F32) / half)
    inv128 = jnp.tile(inv, LANES // half).reshape(1, LANES)
    sign = jnp.where((jnp.arange(LANES) % HEAD_DIM) < half, -1.0, 1.0).astype(_F32).reshape(1, LANES)
    pos = jnp.broadcast_to(positions.reshape(t, 1), (t, LANES))
    tm = min(t, 1024)
    row = pl.BlockSpec((tm, LANES), lambda i: (i, 0))
    one = pl.BlockSpec((1, LANES), lambda i: (0, 0))
    return pl.pallas_call(
        _rope_kernel,
        grid=(t // tm,),
        in_specs=[row, one, one],
        out_specs=[row, row],
        out_shape=[jax.ShapeDtypeStruct((t, LANES), _F32)] * 2,
        compiler_params=_cparams(1),
        name="rope_tables",
    )(pos, inv128, sign)


def _mem_kv_kernel(mem_ref, g_ref, w_ref, kn_ref, k_ref, v_ref):
    m = mem_ref[...]
    ms = jnp.mean(m * m, axis=-1, keepdims=True)
    mn = (m * lax.rsqrt(ms + EPS)) * g_ref[...]
    kv = _dot(mn.astype(_BF16), w_ref[...])
    for j in range(C_WIDTH // LANES):
        sl = slice(j * LANES, (j + 1) * LANES)
        k_ref[:, sl] = _pair_rms(kv[:, sl], kn_ref[...]).astype(_BF16)
    v_ref[...] = kv[:, C_WIDTH:].T.astype(_BF16)


def _mem_kv(mem, mem_norm_g, w_mem_kv, kn_c):
    b, m, d = mem.shape
    kn = jnp.tile(kn_c, 2).reshape(1, LANES)
    return pl.pallas_call(
        _mem_kv_kernel,
        grid=(b,),
        in_specs=[pl.BlockSpec((None, m, d), lambda i: (i, 0, 0)),
                  pl.BlockSpec((1, d), lambda i: (0, 0)),
                  pl.BlockSpec((d, 2 * C_WIDTH), lambda i: (0, 0)),
                  pl.BlockSpec((1, LANES), lambda i: (0, 0))],
        out_specs=[pl.BlockSpec((None, m, C_WIDTH), lambda i: (i, 0, 0)),
                   pl.BlockSpec((None, C_WIDTH, m), lambda i: (i, 0, 0))],
        out_shape=[jax.ShapeDtypeStruct((b, m, C_WIDTH), _BF16),
                   jax.ShapeDtypeStruct((b, C_WIDTH, m), _BF16)],
        compiler_params=_cparams(1),
        name="mem_kv",
    )(mem, mem_norm_g.reshape(1, d), w_mem_kv.astype(_BF16), kn)


def _in_proj_kernel(x_ref, g_ref, w_ref, o_ref):
    x = x_ref[...]
    ms = jnp.mean(x * x, axis=-1, keepdims=True)
    xn = ((x * lax.rsqrt(ms + EPS)) * g_ref[...]).astype(_BF16)
    o_ref[...] = _dot(xn, w_ref[...])


def _in_proj(x2, g, w_in):
    t, d = x2.shape
    tm = min(t, 512)
    return pl.pallas_call(
        _in_proj_kernel,
        grid=(t // tm,),
        in_specs=[pl.BlockSpec((tm, d), lambda i: (i, 0)),
                  pl.BlockSpec((1, d), lambda i: (0, 0)),
                  pl.BlockSpec((d, IN_COLS), lambda i: (0, 0))],
        out_specs=pl.BlockSpec((tm, IN_COLS), lambda i: (i, 0)),
        out_shape=jax.ShapeDtypeStruct((t, IN_COLS), _F32),
        compiler_params=_cparams(1),
        name="in_proj",
    )(x2, g.reshape(1, d), w_in.astype(_BF16))


def _attn_kernel(q_ref, k_ref, v_ref, qn_ref, kn_ref, bias_ref, o_ref, kp_ref, vt_ref, st_ref, *, q_rows):
    qs = pl.program_id(2)
    s = k_ref.shape[0]
    fill_rows = min(s, 512)
    left_blocks = LEFT_ROWS // LANES

    @pl.when(qs == 0)
    def _():
        kp_ref[0:LEFT_ROWS, :] = jnp.zeros((LEFT_ROWS, LANES), _BF16)
        for blk in range(left_blocks):
            vt_ref[blk] = jnp.zeros((LANES, LANES), _BF16)

        def fill(i, carry):
            r = pl.multiple_of(i * fill_rows, fill_rows)
            kp_ref[pl.ds(LEFT_ROWS + r, fill_rows), :] = _pair_rms(
                k_ref[pl.ds(r, fill_rows), :], kn_ref[...]).astype(_BF16)
            vt = v_ref[pl.ds(r, fill_rows), :].T
            for j in range(fill_rows // LANES):
                vt_ref[left_blocks + i * (fill_rows // LANES) + j] = vt[:, j * LANES:(j + 1) * LANES].astype(_BF16)
            return carry

        lax.fori_loop(0, s // fill_rows, fill, 0)

    low = _lane((ATT_Q, LANES)) < HEAD_DIM
    tiles_per_step = q_rows // ATT_Q

    def scores(j):
        cp = qs * tiles_per_step + j
        qn = _pair_rms(q_ref[j * ATT_Q:(j + 1) * ATT_Q, :], qn_ref[...])
        q2 = jnp.concatenate([jnp.where(low, qn, 0.0), jnp.where(low, 0.0, qn)], axis=0).astype(_BF16)
        kb = kp_ref[pl.ds(pl.multiple_of(cp * ATT_Q, ATT_Q), ATT_K), :]
        st_ref[j % 2] = _dot_nt(kb, q2) + bias_ref[jnp.minimum(cp, ATT_VARIANTS - 1)]

    def finish(j):
        cp = qs * tiles_per_step + j
        st = st_ref[j % 2]
        m = jnp.max(st, axis=0, keepdims=True)
        p = jnp.exp(st - m)
        inv = 1.0 / jnp.sum(p, axis=0, keepdims=True)
        vt = jnp.concatenate([vt_ref[cp + kb_i] for kb_i in range(ATT_K // LANES)], axis=1)
        ot = _dot(vt, p.astype(_BF16))
        out_t = jnp.concatenate([ot[0:HEAD_DIM, 0:ATT_Q] * inv[:, 0:ATT_Q],
                                 ot[HEAD_DIM:, ATT_Q:] * inv[:, ATT_Q:]], axis=0)
        o_ref[j * ATT_Q:(j + 1) * ATT_Q, :] = out_t.T.astype(o_ref.dtype)

    scores(0)
    for j in range(tiles_per_step):
        if j + 1 < tiles_per_step:
            scores(j + 1)
        finish(j)


def _toeplitz_bias(rel_bias, q_len, k_len):
    h, table = rel_bias.shape
    n_diag = q_len + k_len - 1
    flat_lo = k_len - 1 - LEFT_ROWS - (CHUNK - 1)
    flat_hi = n_diag - flat_lo - table
    rev = jnp.concatenate([jnp.broadcast_to(rel_bias[:, -1:], (h, flat_hi)), rel_bias[:, ::-1],
                           jnp.broadcast_to(rel_bias[:, :1], (h, flat_lo))], axis=1).astype(_F32)
    flat = jnp.tile(rev, (1, q_len + 1))
    pitch = n_diag - 1
    skew = flat[:, q_len - 1:q_len - 1 + q_len * pitch].reshape(h, q_len, pitch)
    return skew[:, :, :k_len]


def _attn_bias(rel_bias):
    h = rel_bias.shape[0]
    bias = _toeplitz_bias(rel_bias, ATT_Q, ATT_K)
    q = lax.broadcasted_iota(jnp.int32, (ATT_Q, ATT_K), 0)
    k = lax.broadcasted_iota(jnp.int32, (ATT_Q, ATT_K), 1)
    off = k // CHUNK - q // CHUNK
    in_band = (off >= 0) & (off < BAND_CHUNKS)
    first_key = LEFT_ROWS - ATT_Q * jnp.arange(ATT_VARIANTS, dtype=jnp.int32)
    ok = in_band[None] & (k[None] >= first_key[:, None, None])
    full = jnp.where(ok[None], bias[:, None], NEG_INF)
    full = full.reshape(h // 2, 2, ATT_VARIANTS, ATT_Q, ATT_K)
    return full.transpose(0, 2, 4, 1, 3).reshape(h // 2, ATT_VARIANTS, ATT_K, 2 * ATT_Q)


def _attention(proj3, qn_a, kn_a, rel_bias):
    b, s, _ = proj3.shape
    q_rows = min(s, 1024)
    qn = (jnp.tile(qn_a, 2) * (HEAD_DIM ** -0.5)).reshape(1, LANES)
    kn = jnp.tile(kn_a, 2).reshape(1, LANES)
    pairs = A_HEADS // 2
    kcol, vcol = A_WIDTH // LANES, 2 * A_WIDTH // LANES
    return pl.pallas_call(
        functools.partial(_attn_kernel, q_rows=q_rows),
        grid=(b, pairs, s // q_rows),
        in_specs=[pl.BlockSpec((None, q_rows, LANES), lambda i, p, j: (i, j, p)),
                  pl.BlockSpec((None, s, LANES), lambda i, p, j: (i, 0, kcol + p)),
                  pl.BlockSpec((None, s, LANES), lambda i, p, j: (i, 0, vcol + p)),
                  pl.BlockSpec((1, LANES), lambda i, p, j: (0, 0)),
                  pl.BlockSpec((1, LANES), lambda i, p, j: (0, 0)),
                  pl.BlockSpec((None, ATT_VARIANTS, ATT_K, 2 * ATT_Q), lambda i, p, j: (p, 0, 0, 0))],
        out_specs=pl.BlockSpec((None, q_rows, LANES), lambda i, p, j: (i, j, p)),
        out_shape=jax.ShapeDtypeStruct((b, s, A_WIDTH), _BF16),
        scratch_shapes=[pltpu.VMEM((s + LEFT_ROWS, LANES), _BF16),
                        pltpu.VMEM(((s + LEFT_ROWS) // LANES, LANES, LANES), _BF16),
                        pltpu.VMEM((2, ATT_K, 2 * ATT_Q), _F32)],
        compiler_params=_cparams(3),
        name="attn_a",
    )(proj3, proj3, proj3, qn, kn, _attn_bias(rel_bias))


def _swap_halves(t):
    first = (_lane(t.shape) % HEAD_DIM) < (HEAD_DIM // 2)
    return jnp.where(first, pltpu.roll(t, LANES - HEAD_DIM // 2, 1), pltpu.roll(t, HEAD_DIM // 2, 1))


def _retention_kernel(q_ref, k_ref, v_ref, gate_ref, cos_ref, sin_ref, decay_ref, zeta_ref, xi_ref,
                      cd_ref, gn_ref, o_ref, state_ref, *, rows):
    @pl.when(pl.program_id(2) == 0)
    def _():
        state_ref[...] = jnp.zeros_like(state_ref)

    c = RET_CHUNK
    low = _lane((c, LANES)) < HEAD_DIM
    srow = lax.broadcasted_iota(jnp.int32, (LANES, LANES), 0) < HEAD_DIM
    scol = _lane((LANES, LANES)) < HEAD_DIM
    same_head = srow == scol

    for j in range(rows // c):
        sl = slice(j * c, (j + 1) * c)
        cos, sin = cos_ref[sl, :], sin_ref[sl, :]
        q = q_ref[sl, :]
        k = k_ref[sl, :]
        qr = q * cos + _swap_halves(q) * sin
        kr = (k * cos + _swap_halves(k) * sin) * (HEAD_DIM ** -0.5)
        vb = v_ref[sl, :].astype(_BF16)
        qb = qr.astype(_BF16)
        kb = kr.astype(_BF16)
        inner_out = []
        for h in range(2):
            qh = jnp.where(low if h == 0 else ~low, qr, 0.0).astype(_BF16)
            inner = _dot_nt(qh, kb) * decay_ref[h]
            inner_out.append(_dot(inner.astype(_BF16), vb))
        state = state_ref[...]
        cross = _dot(qb, state.astype(_BF16)) * xi_ref[...]
        o = jnp.where(low, inner_out[0], inner_out[1]) + cross
        kz = (kr * zeta_ref[...]).T.astype(_BF16)
        state_ref[...] = cd_ref[...] * state + jnp.where(same_head, _dot(kz, vb), 0.0)
        mu = jnp.where(low,
                       jnp.sum(jnp.where(low, o, 0.0), axis=-1, keepdims=True),
                       jnp.sum(jnp.where(low, 0.0, o), axis=-1, keepdims=True)) * (1.0 / HEAD_DIM)
        dlt = o - mu
        d2 = dlt * dlt
        var = jnp.where(low,
                        jnp.sum(jnp.where(low, d2, 0.0), axis=-1, keepdims=True),
                        jnp.sum(jnp.where(low, 0.0, d2), axis=-1, keepdims=True)) * (1.0 / HEAD_DIM)
        y = (dlt * lax.rsqrt(var + EPS)) * gn_ref[...]
        g = gate_ref[sl, :]
        o_ref[sl, :] = ((g * jax.nn.sigmoid(g)) * y).astype(o_ref.dtype)


def _retention_tables():
    c = RET_CHUNK
    log_g = jnp.log(1.0 - jnp.exp2(-5.0 - jnp.arange(B_HEADS, dtype=_F32)))
    idx = jnp.arange(c, dtype=_F32)
    diff = idx[:, None] - idx[None, :]
    decay = jnp.where(diff >= 0, jnp.exp(log_g[:, None, None] * jnp.maximum(diff, 0.0)), 0.0)
    zeta = jnp.exp(log_g[:, None] * (c - 1 - idx))
    xi = jnp.exp(log_g[:, None] * (idx + 1.0))
    cd = jnp.exp(log_g * c)

    def lanes(tab):
        return jnp.repeat(tab.reshape(B_HEADS // 2, 2, c), HEAD_DIM, axis=1).transpose(0, 2, 1)

    cdm = jnp.repeat(cd.reshape(B_HEADS // 2, 2), HEAD_DIM, axis=1)
    cdm = jnp.broadcast_to(cdm[:, :, None], (B_HEADS // 2, LANES, LANES))
    return decay, lanes(zeta), lanes(xi), cdm


def _retention(proj3, cos, sin, ret_gn_g):
    b, s, _ = proj3.shape
    rows = min(s, 1024)
    pairs = B_HEADS // 2
    base = 3 * A_WIDTH // LANES
    decay, zeta, xi, cdm = _retention_tables()
    cos3, sin3 = cos.reshape(b, s, LANES), sin.reshape(b, s, LANES)
    gn = ret_gn_g.reshape(pairs, 1, LANES)

    def col(off):
        return pl.BlockSpec((None, rows, LANES), lambda i, p, j: (i, j, base + off * pairs + p))

    tab = pl.BlockSpec((None, rows, LANES), lambda i, p, j: (i, j, 0))
    return pl.pallas_call(
        functools.partial(_retention_kernel, rows=rows),
        grid=(b, pairs, s // rows),
        in_specs=[col(0), col(1), col(2), col(3), tab, tab,
                  pl.BlockSpec((2, RET_CHUNK, RET_CHUNK), lambda i, p, j: (p, 0, 0)),
                  pl.BlockSpec((None, RET_CHUNK, LANES), lambda i, p, j: (p, 0, 0)),
                  pl.BlockSpec((None, RET_CHUNK, LANES), lambda i, p, j: (p, 0, 0)),
                  pl.BlockSpec((None, LANES, LANES), lambda i, p, j: (p, 0, 0)),
                  pl.BlockSpec((None, 1, LANES), lambda i, p, j: (p, 0, 0))],
        out_specs=pl.BlockSpec((None, rows, LANES), lambda i, p, j: (i, j, p)),
        out_shape=jax.ShapeDtypeStruct((b, s, B_WIDTH), _BF16),
        scratch_shapes=[pltpu.VMEM((LANES, LANES), _F32)],
        compiler_params=_cparams(3),
        name="retention_b",
    )(proj3, proj3, proj3, proj3, cos3, sin3, decay, zeta, xi, cdm, gn)


def _cross_kernel(q_ref, k_ref, vt_ref, qn_ref, o_ref, st_ref, *, rows):
    low = _lane((ATT_Q, LANES)) < HEAD_DIM
    lane_blocks = C_WIDTH // LANES
    tiles = [(j, lb) for j in range(rows // ATT_Q) for lb in range(lane_blocks)]

    def scores(i):
        j, lb = tiles[i]
        sl = slice(lb * LANES, (lb + 1) * LANES)
        qn = _pair_rms(q_ref[j * ATT_Q:(j + 1) * ATT_Q, sl], qn_ref[...])
        q2 = jnp.concatenate([jnp.where(low, qn, 0.0), jnp.where(low, 0.0, qn)], axis=0).astype(_BF16)
        st_ref[i % 2] = _dot_nt(k_ref[:, sl], q2)

    def finish(i):
        j, lb = tiles[i]
        sl = slice(lb * LANES, (lb + 1) * LANES)
        st = st_ref[i % 2]
        p = jnp.exp(st - jnp.max(st, axis=0, keepdims=True))
        inv = 1.0 / jnp.sum(p, axis=0, keepdims=True)
        ot = _dot(vt_ref[sl, :], p.astype(_BF16))
        out_t = jnp.concatenate([ot[0:HEAD_DIM, 0:ATT_Q] * inv[:, 0:ATT_Q],
                                 ot[HEAD_DIM:, ATT_Q:] * inv[:, ATT_Q:]], axis=0)
        o_ref[j * ATT_Q:(j + 1) * ATT_Q, sl] = out_t.T.astype(o_ref.dtype)

    scores(0)
    for i in range(len(tiles)):
        if i + 1 < len(tiles):
            scores(i + 1)
        finish(i)


def _cross_attention(proj3, kc, vtc, qn_c):
    b, s, _ = proj3.shape
    m = kc.shape[1]
    rows = min(s, 512)
    qn = (jnp.tile(qn_c, 2) * (HEAD_DIM ** -0.5)).reshape(1, LANES)
    qcol = (3 * A_WIDTH + 4 * B_WIDTH) // C_WIDTH
    return pl.pallas_call(
        functools.partial(_cross_kernel, rows=rows),
        grid=(b, s // rows),
        in_specs=[pl.BlockSpec((None, rows, C_WIDTH), lambda i, j: (i, j, qcol)),
                  pl.BlockSpec((None, m, C_WIDTH), lambda i, j: (i, 0, 0)),
                  pl.BlockSpec((None, C_WIDTH, m), lambda i, j: (i, 0, 0)),
                  pl.BlockSpec((1, LANES), lambda i, j: (0, 0))],
        out_specs=pl.BlockSpec((None, rows, C_WIDTH), lambda i, j: (i, j, 0)),
        out_shape=jax.ShapeDtypeStruct((b, s, C_WIDTH), _BF16),
        scratch_shapes=[pltpu.VMEM((2, m, 2 * ATT_Q), _F32)],
        compiler_params=_cparams(2),
        name="cross_c",
    )(proj3, kc, vtc, qn)


def _out_router_kernel(x_ref, a_ref, b_ref, c_ref, wo_ref, g_ref, wr_ref, br_ref,
                       h_ref, hn_ref, info_ref, cnt_ref, carry_ref):
    @pl.when(pl.program_id(0) == 0)
    def _():
        carry_ref[...] = jnp.zeros_like(carry_ref)

    tm = x_ref.shape[0]
    h = x_ref[...]
    h = h + _dot(a_ref[...], wo_ref[0:A_WIDTH, :])
    h = h + _dot(b_ref[...], wo_ref[A_WIDTH:A_WIDTH + B_WIDTH, :])
    h = h + _dot(c_ref[...], wo_ref[A_WIDTH + B_WIDTH:, :])
    h_ref[...] = h
    ms = jnp.mean(h * h, axis=-1, keepdims=True)
    hn = (h * lax.rsqrt(ms + EPS)) * g_ref[...]
    _rows_to_tiles(hn_ref, hn)
    logits = _dot(hn.astype(_BF16), wr_ref[...]) + br_ref[...]

    lane = _lane((tm, LANES)).astype(_F32)
    big = float(LANES)

    def first_lane(mask):
        return jnp.min(jnp.where(mask, lane, big), axis=-1, keepdims=True)

    gmask = lane < N_GROUPS
    gl = jnp.where(gmask, logits, NEG_INF)
    ge = jnp.exp(gl - jnp.max(gl, axis=-1, keepdims=True))
    gp = ge / jnp.sum(ge, axis=-1, keepdims=True)
    p_group = jnp.max(gp, axis=-1, keepdims=True)
    g_sel = first_lane(gmask & (gp == p_group))
    lo = ROUTE_LANE0 + g_sel * EXPERTS_PER_GROUP
    emask = (lane >= lo) & (lane < lo + EXPERTS_PER_GROUP)
    el = jnp.where(emask, logits, NEG_INF)
    ee = jnp.exp(el - jnp.max(el, axis=-1, keepdims=True))
    ep = ee / jnp.sum(ee, axis=-1, keepdims=True)
    p1 = jnp.max(ep, axis=-1, keepdims=True)
    i1 = first_lane(emask & (ep == p1))
    ep2 = jnp.where(emask & (lane != i1), ep, -1.0)
    p2 = jnp.max(ep2, axis=-1, keepdims=True)
    i2 = first_lane(ep2 == p2)
    den = p1 + p2
    w1 = p_group * (p1 / den)
    w2 = p_group * (p2 / den)
    hit1 = lane == i1
    hit2 = lane == i2
    onehot = jnp.where(hit1 | hit2, 1.0, 0.0)
    r_i = lax.broadcasted_iota(jnp.int32, (tm, tm), 0)
    c_i = lax.broadcasted_iota(jnp.int32, (tm, tm), 1)
    strict = jnp.where(c_i < r_i, 1.0, 0.0).astype(_BF16)
    before = _dot(strict, onehot.astype(_BF16)) + carry_ref[...]
    r1 = jnp.sum(jnp.where(hit1, before, 0.0), axis=-1, keepdims=True)
    r2 = jnp.sum(jnp.where(hit2, before, 0.0), axis=-1, keepdims=True)
    carry_ref[...] = carry_ref[...] + jnp.sum(onehot, axis=0, keepdims=True)
    cnt_ref[...] = carry_ref[...]
    info = jnp.where(lane == 0, w1, 0.0)
    info = jnp.where(lane == 1, w2, info)
    info = jnp.where(lane == 2, i1 - ROUTE_LANE0, info)
    info = jnp.where(lane == 3, i2 - ROUTE_LANE0, info)
    info = jnp.where(lane == 4, r1, info)
    info = jnp.where(lane == 5, r2, info)
    info_ref[...] = info


def _out_router(x2, oa, ob, oc, w_out, ffn_g, w_rg, b_rg, w_re, b_re):
    t, d = x2.shape
    tm = min(t, 512)
    pad = LANES - N_GROUPS - N_EXPERTS
    wr = jnp.concatenate([w_rg, w_re, jnp.zeros((d, pad), _F32)], axis=1).astype(_BF16)
    br = jnp.concatenate([b_rg, b_re, jnp.zeros((pad,), _F32)]).reshape(1, LANES)

    def rows(w):
        return pl.BlockSpec((tm, w), lambda i: (i, 0))

    def whole(r, c):
        return pl.BlockSpec((r, c), lambda i: (0, 0))

    return pl.pallas_call(
        _out_router_kernel,
        grid=(t // tm,),
        in_specs=[rows(d), rows(A_WIDTH), rows(B_WIDTH), rows(C_WIDTH), whole(d, d), whole(1, d),
                  whole(d, LANES), whole(1, LANES)],
        out_specs=[rows(d), pl.BlockSpec((tm * SUBLANES, LANES), lambda i: (i, 0)), rows(LANES),
                   whole(1, LANES)],
        out_shape=[jax.ShapeDtypeStruct((t, d), _F32), jax.ShapeDtypeStruct((t * SUBLANES, LANES), _F32),
                   jax.ShapeDtypeStruct((t, LANES), _F32), jax.ShapeDtypeStruct((1, LANES), _F32)],
        scratch_shapes=[pltpu.VMEM((1, LANES), _F32)],
        compiler_params=_cparams(1),
        name="out_router",
    )(x2, oa, ob, oc, w_out.astype(_BF16), ffn_g.reshape(1, d), wr, br)


DISPATCH_TOKENS = 512
COMBINE_TOKENS = 256


ROW_UNROLL = 8


def _tile_rows(row, count=1):
    start = row * SUBLANES
    if not isinstance(start, int):
        start = pl.multiple_of(start, SUBLANES)
    return pl.ds(start, count * SUBLANES)


def _row_copy(src, s_row, dst, d_row, sem):
    return pltpu.make_async_copy(src.at[_tile_rows(s_row)], dst.at[_tile_rows(d_row)], sem)


def _dispatch_kernel(pad_start_ref, pad_len_ref, used_ref, dest_ref, hn_ref, xs_ref, zero_ref, sem,
                     pad_sem):
    n = hn_ref.shape[0] // SUBLANES

    @pl.when(pl.program_id(0) == 0)
    def _():
        zero_ref[...] = jnp.zeros_like(zero_ref)
        n_blocks = xs_ref.shape[0] // (ROW_BLOCK * SUBLANES)

        def block_copy(blk):
            return pltpu.make_async_copy(zero_ref, xs_ref.at[_tile_rows(blk * ROW_BLOCK, ROW_BLOCK)], pad_sem)

        def put_block(blk, carry):
            block_copy(blk).start()
            return carry

        def done_block(blk, carry):
            block_copy(blk).wait()
            return carry

        lax.fori_loop(used_ref[0], n_blocks, put_block, 0)
        lax.fori_loop(used_ref[0], n_blocks, done_block, 0)
        bits = [1 << k for k in reversed(range(ROW_BLOCK.bit_length() - 1))]

        def tail(e, wait):
            row = pad_start_ref[e]
            for bit in bits:
                on = (pad_len_ref[e] & bit) != 0
                copy = pltpu.make_async_copy(zero_ref.at[_tile_rows(0, bit)], xs_ref.at[_tile_rows(row, bit)],
                                             pad_sem)

                @pl.when(on)
                def _():
                    copy.wait() if wait else copy.start()

                row = row + jnp.where(on, bit, 0)

        def put_tail(e, carry):
            tail(e, False)
            return carry

        def done_tail(e, carry):
            tail(e, True)
            return carry

        lax.fori_loop(0, N_EXPERTS, put_tail, 0)
        lax.fori_loop(0, N_EXPERTS, done_tail, 0)

    def issue(i, carry):
        for u in range(ROW_UNROLL):
            t = i * ROW_UNROLL + u
            _row_copy(hn_ref, t, xs_ref, dest_ref[2 * t], sem).start(priority=0)
            _row_copy(hn_ref, t, xs_ref, dest_ref[2 * t + 1], sem).start(priority=1)
        return carry

    lax.fori_loop(0, n // ROW_UNROLL, issue, 0)
    for _ in range(2):
        pltpu.make_async_copy(hn_ref, xs_ref.at[_tile_rows(0, n)], sem).wait()


def _dispatch(hn, dest, pad_start, pad_len, n_used, n_rows):
    t = hn.shape[0] // SUBLANES
    n = min(t, DISPATCH_TOKENS)
    return pl.pallas_call(
        _dispatch_kernel,
        grid_spec=pltpu.PrefetchScalarGridSpec(
            num_scalar_prefetch=3,
            grid=(t // n,),
            in_specs=[pl.BlockSpec((2 * n,), lambda i, *_: (i,), memory_space=pltpu.SMEM),
                      pl.BlockSpec((n * SUBLANES, LANES), lambda i, *_: (i, 0))],
            out_specs=pl.BlockSpec(memory_space=pl.ANY),
            scratch_shapes=[pltpu.VMEM((ROW_BLOCK * SUBLANES, LANES), hn.dtype), pltpu.SemaphoreType.DMA,
                            pltpu.SemaphoreType.DMA]),
        out_shape=jax.ShapeDtypeStruct((n_rows * SUBLANES, LANES), hn.dtype),
        compiler_params=_cparams(1),
        name="moe_dispatch",
    )(pad_start, pad_len, n_used, dest, hn)


def _expert_kernel(be_ref, used_ref, x_ref, wg_ref, wu_ref, wd_ref, y_ref, wg_bf, wu_bf, wd_bf):
    i = pl.program_id(0)
    live = i < used_ref[0]
    new_expert = (i == 0) | (be_ref[i] != be_ref[jnp.maximum(i - 1, 0)])

    @pl.when(live & new_expert)
    def _():
        wg_bf[...] = wg_ref[...].astype(_BF16)
        wu_bf[...] = wu_ref[...].astype(_BF16)
        wd_bf[...] = wd_ref[...].astype(_BF16)

    @pl.when(live)
    def _():
        half = ROW_BLOCK // 2
        gate_up = []
        for hh in range(2):
            x = _tiles_to_rows(x_ref, half, hh * half).astype(_BF16)
            gate_up.append((_dot(x, wg_bf[...]), _dot(x, wu_bf[...])))
        for hh in range(2):
            gate, up = gate_up[hh]
            act = (gate * jax.nn.sigmoid(gate)) * up
            _rows_to_tiles(y_ref, _dot(act.astype(_BF16), wd_bf[...]), hh * half)

    @pl.when(i >= used_ref[0])
    def _():
        y_ref[...] = jnp.zeros_like(y_ref)


def _experts(xs, block_e, n_used, w_gate, w_up, w_down):
    n_rows, d = xs.shape[0] // SUBLANES, D_MODEL
    n_blocks = n_rows // ROW_BLOCK
    tile_block = (ROW_BLOCK * SUBLANES, LANES)

    def xmap(i, be, used):
        return (jnp.minimum(i, used[0] - 1), 0)

    def wmap(i, be, used):
        return (be[jnp.minimum(i, used[0] - 1)], 0, 0)

    return pl.pallas_call(
        _expert_kernel,
        grid_spec=pltpu.PrefetchScalarGridSpec(
            num_scalar_prefetch=2,
            grid=(n_blocks,),
            in_specs=[pl.BlockSpec(tile_block, xmap),
                      pl.BlockSpec((None, d, D_EXPERT), wmap),
                      pl.BlockSpec((None, d, D_EXPERT), wmap),
                      pl.BlockSpec((None, D_EXPERT, d), wmap)],
            out_specs=pl.BlockSpec(tile_block, lambda i, be, used: (i, 0)),
            scratch_shapes=[pltpu.VMEM((d, D_EXPERT), _BF16), pltpu.VMEM((d, D_EXPERT), _BF16),
                            pltpu.VMEM((D_EXPERT, d), _BF16)]),
        out_shape=jax.ShapeDtypeStruct((n_rows * SUBLANES, LANES), _F32),
        compiler_params=_cparams(1),
        name="moe_experts",
    )(block_e, n_used, xs, w_gate, w_up, w_down)


def _combine_kernel(dest_ref, next_ref, h_ref, info_ref, ys_ref, o_ref, buf_ref, sem):
    n = h_ref.shape[0]
    step = pl.program_id(0)
    slot = step % 2

    def gather(idx_ref, to_slot):
        def issue(i, carry):
            for u in range(ROW_UNROLL):
                t = i * ROW_UNROLL + u
                _row_copy(ys_ref, idx_ref[2 * t], buf_ref.at[to_slot, 0], t,
                          sem.at[to_slot]).start(priority=0)
                _row_copy(ys_ref, idx_ref[2 * t + 1], buf_ref.at[to_slot, 1], t,
                          sem.at[to_slot]).start(priority=1)
            return carry

        lax.fori_loop(0, n // ROW_UNROLL, issue, 0)

    @pl.when(step == 0)
    def _():
        gather(dest_ref, 0)

    @pl.when(step + 1 < pl.num_programs(0))
    def _():
        gather(next_ref, 1 - slot)

    for k in range(2):
        pltpu.make_async_copy(ys_ref.at[_tile_rows(0, n)], buf_ref.at[slot, k], sem.at[slot]).wait()
    info = info_ref[...]
    w0 = info[:, 0:1]
    w1 = info[:, 1:2]
    for s in range(SUBLANES):
        sl = slice(s * LANES, (s + 1) * LANES)
        moe = w0 * _tile_block(buf_ref.at[slot, 0], s, n) + w1 * _tile_block(buf_ref.at[slot, 1], s, n)
        o_ref[:, sl] = h_ref[:, sl] + moe


def _combine(h, info, ys, dest):
    t, d = h.shape
    n = min(t, COMBINE_TOKENS)
    steps = t // n
    return pl.pallas_call(
        _combine_kernel,
        grid=(steps,),
        in_specs=[pl.BlockSpec((2 * n,), lambda i: (i,), memory_space=pltpu.SMEM),
                  pl.BlockSpec((2 * n,), lambda i: (jnp.minimum(i + 1, steps - 1),),
                               memory_space=pltpu.SMEM),
                  pl.BlockSpec((n, d), lambda i: (i, 0)),
                  pl.BlockSpec((n, LANES), lambda i: (i, 0)),
                  pl.BlockSpec(memory_space=pl.ANY)],
        out_specs=pl.BlockSpec((n, d), lambda i: (i, 0)),
        out_shape=jax.ShapeDtypeStruct((t, d), _F32),
        scratch_shapes=[pltpu.VMEM((2, 2, n * SUBLANES, LANES), _F32), pltpu.SemaphoreType.DMA((2,))],
        compiler_params=_cparams(1),
        name="moe_combine",
    )(dest, dest, h, info, ys)


def _moe_layout(info, counts, t):
    counts = counts[0, ROUTE_LANE0:ROUTE_LANE0 + N_EXPERTS].astype(jnp.int32)
    padded = (counts + ROW_BLOCK - 1) // ROW_BLOCK * ROW_BLOCK
    pends = jnp.cumsum(padded)
    pstarts = pends - padded
    eid = info[:, 2:4].astype(jnp.int32)
    rank = info[:, 4:6].astype(jnp.int32)
    experts = jnp.arange(N_EXPERTS, dtype=jnp.int32)
    start_of = jnp.sum(jnp.where(eid[:, :, None] == experts, pstarts, 0), axis=-1)
    dest = (start_of + rank).reshape(-1)
    n_blocks = -(-2 * t // ROW_BLOCK) + N_EXPERTS
    first_row = jnp.arange(n_blocks, dtype=jnp.int32) * ROW_BLOCK
    block_e = jnp.minimum(jnp.sum((pends[None, :] <= first_row[:, None]).astype(jnp.int32), axis=1),
                          N_EXPERTS - 1)
    n_used = (pends[-1:] // ROW_BLOCK).astype(jnp.int32)
    return dest, block_e, n_used, pstarts + counts, padded - counts, n_blocks * ROW_BLOCK


def kernel(x, mem, positions, mix_norm_g, w_in, qn_a, kn_a, rel_bias, ret_gn_g, mem_norm_g, w_mem_kv,
           qn_c, kn_c, w_out, ffn_norm_g, w_router_group, b_router_group, w_router_expert,
           b_router_expert, w_gate, w_up, w_down):
    b, s, d = x.shape
    t = b * s
    x2 = x.reshape(t, d)
    cos, sin = _rope_tables(positions)
    kc, vc = _mem_kv(mem, mem_norm_g, w_mem_kv, kn_c)
    proj3 = _in_proj(x2, mix_norm_g, w_in).reshape(b, s, IN_COLS)
    out_a = _attention(proj3, qn_a, kn_a, rel_bias)
    out_b = _retention(proj3, cos, sin, ret_gn_g)
    out_c = _cross_attention(proj3, kc, vc, qn_c)
    h, hn, info, counts = _out_router(
        x2, out_a.reshape(t, A_WIDTH), out_b.reshape(t, B_WIDTH), out_c.reshape(t, C_WIDTH),
        w_out, ffn_norm_g, w_router_group, b_router_group, w_router_expert, b_router_expert)
    dest, block_e, n_used, pad_start, pad_len, n_rows = _moe_layout(info, counts, t)
    xs = _dispatch(hn, dest, pad_start, pad_len, n_used, n_rows)
    ys = _experts(xs, block_e, n_used, w_gate, w_up, w_down)
    return _combine(h, info, ys, dest).reshape(b, s, d)
```

```python
import functools

import jax
import jax.numpy as jnp
from jax import lax
from jax.experimental import pallas as pl
from jax.experimental.pallas import tpu as pltpu

D_MODEL = 1024
CHUNK = 64
HEAD_DIM = 64
A_HEADS = 8
B_HEADS = 4
C_HEADS = 4
A_WIDTH = A_HEADS * HEAD_DIM
B_WIDTH = B_HEADS * HEAD_DIM
C_WIDTH = C_HEADS * HEAD_DIM
IN_COLS = 3 * A_WIDTH + 4 * B_WIDTH + C_WIDTH
LEFT_CHUNKS = 8
BAND_CHUNKS = LEFT_CHUNKS + 1
MAX_REL_DIST = 128
ROPE_BASE = 10000.0
N_GROUPS = 4
EXPERTS_PER_GROUP = 8
N_EXPERTS = N_GROUPS * EXPERTS_PER_GROUP
D_EXPERT = D_MODEL // 2
EPS = 1e-6
NEG_INF = -1e30
LOG2E = 1.4426950408889634

LANES = 128
SUBLANES = 8
assert D_MODEL == SUBLANES * LANES
LEFT_ROWS = LEFT_CHUNKS * CHUNK
ATT_Q = 2 * CHUNK
ATT_K = ATT_Q + LEFT_ROWS
ATT_VARIANTS = LEFT_ROWS // ATT_Q + 1
RET_CHUNK = 256
ROW_BLOCK = 512
ROUTE_LANE0 = N_GROUPS
ROUTE_ROWS = 64
VMEM_LIMIT = 48 * 1024 * 1024

_F32 = jnp.float32
_BF16 = jnp.bfloat16


def _cparams(n_axes):
    return pltpu.CompilerParams(dimension_semantics=("arbitrary",) * n_axes,
                                vmem_limit_bytes=VMEM_LIMIT)


def _dot(a, b):
    return jnp.dot(a, b, preferred_element_type=_F32)


def _dot_nt(a, b):
    return lax.dot_general(a, b, (((1,), (1,)), ((), ())), preferred_element_type=_F32)


def _lane(shape):
    return lax.broadcasted_iota(jnp.int32, shape, len(shape) - 1)


def _pair_rms(t, gain):
    low = _lane(t.shape) < HEAD_DIM
    t2 = t * t
    ms0 = jnp.sum(jnp.where(low, t2, 0.0), axis=-1, keepdims=True) * (1.0 / HEAD_DIM)
    ms1 = jnp.sum(jnp.where(low, 0.0, t2), axis=-1, keepdims=True) * (1.0 / HEAD_DIM)
    r = jnp.where(low, lax.rsqrt(ms0 + EPS), lax.rsqrt(ms1 + EPS))
    return (t * r) * gain


def _rows_to_tiles(ref, val, row0=0):
    n = val.shape[0]
    for s in range(SUBLANES):
        ref[pl.ds(row0 * SUBLANES + s, n, stride=SUBLANES), :] = val[:, s * LANES:(s + 1) * LANES]


def _tile_block(ref, s, n, row0=0):
    return ref[pl.ds(row0 * SUBLANES + s, n, stride=SUBLANES), :]


def _tiles_to_rows(ref, n, row0=0):
    return jnp.concatenate([_tile_block(ref, s, n, row0) for s in range(SUBLANES)], axis=-1)


def _rope_kernel(pos_ref, inv_ref, cos_ref, sin_ref):
    ang = pos_ref[...].astype(_F32) * inv_ref[...]
    cos_ref[...] = jnp.cos(ang)
    sin_ref[...] = jnp.sin(ang)


def _rope_tables(positions):
    t = positions.size
    half = HEAD_DIM // 2
    per_row = LANES // half
    inv = ROPE_BASE ** (-jnp.arange(half, dtype=_F32) / half)
    inv128 = jnp.tile(inv, per_row).reshape(1, LANES)
    pos = jnp.repeat(positions.reshape(t // per_row, per_row), half, axis=1)
    rows = t // per_row
    tm = min(rows, 1024)
    row = pl.BlockSpec((tm, LANES), lambda i: (i, 0))
    one = pl.BlockSpec((1, LANES), lambda i: (0, 0))
    cos, sin = pl.pallas_call(
        _rope_kernel,
        grid=(rows // tm,),
        in_specs=[row, one],
        out_specs=[row, row],
        out_shape=[jax.ShapeDtypeStruct((rows, LANES), _F32)] * 2,
        compiler_params=_cparams(1),
        name="rope_tables",
    )(pos, inv128)
    sign = jnp.where((jnp.arange(LANES) % HEAD_DIM) < half, -1.0, 1.0).astype(_F32)
    cos = jnp.tile(cos.reshape(t, half), (1, per_row))
    sin = jnp.tile(sin.reshape(t, half), (1, per_row)) * sign
    return cos, sin


def _mem_kv_kernel(mem_ref, g_ref, w_ref, kn_ref, k_ref, v_ref):
    m = mem_ref[...]
    ms = jnp.mean(m * m, axis=-1, keepdims=True)
    mn = (m * lax.rsqrt(ms + EPS)) * g_ref[...]
    kv = _dot(mn.astype(_BF16), w_ref[...])
    for j in range(C_WIDTH // LANES):
        sl = slice(j * LANES, (j + 1) * LANES)
        k_ref[:, sl] = _pair_rms(kv[:, sl], kn_ref[...]).astype(_BF16)
    v_ref[...] = kv[:, C_WIDTH:].T.astype(_BF16)


def _mem_kv(mem, mem_norm_g, w_mem_kv, kn_c):
    b, m, d = mem.shape
    kn = jnp.tile(kn_c, 2).reshape(1, LANES)
    return pl.pallas_call(
        _mem_kv_kernel,
        grid=(b,),
        in_specs=[pl.BlockSpec((None, m, d), lambda i: (i, 0, 0)),
                  pl.BlockSpec((1, d), lambda i: (0, 0)),
                  pl.BlockSpec((d, 2 * C_WIDTH), lambda i: (0, 0)),
                  pl.BlockSpec((1, LANES), lambda i: (0, 0))],
        out_specs=[pl.BlockSpec((None, m, C_WIDTH), lambda i: (i, 0, 0)),
                   pl.BlockSpec((None, C_WIDTH, m), lambda i: (i, 0, 0))],
        out_shape=[jax.ShapeDtypeStruct((b, m, C_WIDTH), _BF16),
                   jax.ShapeDtypeStruct((b, C_WIDTH, m), _BF16)],
        compiler_params=_cparams(1),
        name="mem_kv",
    )(mem, mem_norm_g.reshape(1, d), w_mem_kv.astype(_BF16), kn)


def _in_proj_kernel(x_ref, g_ref, w_ref, o_ref):
    x = x_ref[...]
    ms = jnp.mean(x * x, axis=-1, keepdims=True)
    xn = ((x * lax.rsqrt(ms + EPS)) * g_ref[...]).astype(_BF16)
    o_ref[...] = _dot(xn, w_ref[...])


def _in_proj(x2, g, w_in):
    t, d = x2.shape
    tm = min(t, 512)
    return pl.pallas_call(
        _in_proj_kernel,
        grid=(t // tm,),
        in_specs=[pl.BlockSpec((tm, d), lambda i: (i, 0)),
                  pl.BlockSpec((1, d), lambda i: (0, 0)),
                  pl.BlockSpec((d, IN_COLS), lambda i: (0, 0))],
        out_specs=pl.BlockSpec((tm, IN_COLS), lambda i: (i, 0)),
        out_shape=jax.ShapeDtypeStruct((t, IN_COLS), _F32),
        compiler_params=_cparams(1),
        name="in_proj",
    )(x2, g.reshape(1, d), w_in.astype(_BF16))


def _attn_kernel(q_ref, k_ref, v_ref, qn_ref, kn_ref, bias_ref, o_ref, kp_ref, vt_ref, st_ref, *, q_rows):
    qs = pl.program_id(2)
    s = k_ref.shape[0]
    fill_rows = min(s, 512)
    left_blocks = LEFT_ROWS // LANES

    @pl.when(qs == 0)
    def _():
        kp_ref[0:LEFT_ROWS, :] = jnp.zeros((LEFT_ROWS, LANES), _BF16)
        for blk in range(left_blocks):
            vt_ref[blk] = jnp.zeros((LANES, LANES), _BF16)

        def fill(i, carry):
            r = pl.multiple_of(i * fill_rows, fill_rows)
            kp_ref[pl.ds(LEFT_ROWS + r, fill_rows), :] = _pair_rms(
                k_ref[pl.ds(r, fill_rows), :], kn_ref[...]).astype(_BF16)
            vt = v_ref[pl.ds(r, fill_rows), :].T
            for j in range(fill_rows // LANES):
                vt_ref[left_blocks + i * (fill_rows // LANES) + j] = vt[:, j * LANES:(j + 1) * LANES].astype(_BF16)
            return carry

        lax.fori_loop(0, s // fill_rows, fill, 0)

    low = _lane((ATT_Q, LANES)) < HEAD_DIM
    tiles_per_step = q_rows // ATT_Q

    def scores(j):
        cp = qs * tiles_per_step + j
        qn = _pair_rms(q_ref[j * ATT_Q:(j + 1) * ATT_Q, :], qn_ref[...])
        q2 = jnp.concatenate([jnp.where(low, qn, 0.0), jnp.where(low, 0.0, qn)], axis=0).astype(_BF16)
        kb = kp_ref[pl.ds(pl.multiple_of(cp * ATT_Q, ATT_Q), ATT_K), :]
        st_ref[j % 2] = _dot_nt(kb, q2) + bias_ref[jnp.minimum(cp, ATT_VARIANTS - 1)]

    def finish(j):
        cp = qs * tiles_per_step + j
        st = st_ref[j % 2]
        m = jnp.max(st, axis=0, keepdims=True)
        p = jnp.exp2(st - m)
        inv = 1.0 / jnp.sum(p, axis=0, keepdims=True)
        vt = jnp.concatenate([vt_ref[cp + kb_i] for kb_i in range(ATT_K // LANES)], axis=1)
        ot = _dot(vt, p.astype(_BF16))
        out_t = jnp.concatenate([ot[0:HEAD_DIM, 0:ATT_Q] * inv[:, 0:ATT_Q],
                                 ot[HEAD_DIM:, ATT_Q:] * inv[:, ATT_Q:]], axis=0)
        o_ref[j * ATT_Q:(j + 1) * ATT_Q, :] = out_t.T.astype(o_ref.dtype)

    scores(0)
    for j in range(tiles_per_step):
        if j + 1 < tiles_per_step:
            scores(j + 1)
        finish(j)


def _toeplitz_bias(rel_bias, q_len, k_len):
    h, table = rel_bias.shape
    n_diag = q_len + k_len - 1
    flat_lo = k_len - 1 - LEFT_ROWS - (CHUNK - 1)
    flat_hi = n_diag - flat_lo - table
    rev = jnp.concatenate([jnp.broadcast_to(rel_bias[:, -1:], (h, flat_hi)), rel_bias[:, ::-1],
                           jnp.broadcast_to(rel_bias[:, :1], (h, flat_lo))], axis=1).astype(_F32)
    flat = jnp.tile(rev, (1, q_len + 1))
    pitch = n_diag - 1
    skew = flat[:, q_len - 1:q_len - 1 + q_len * pitch].reshape(h, q_len, pitch)
    return skew[:, :, :k_len]


def _attn_bias(rel_bias):
    h = rel_bias.shape[0]
    bias = _toeplitz_bias(rel_bias, ATT_Q, ATT_K)
    q = lax.broadcasted_iota(jnp.int32, (ATT_Q, ATT_K), 0)
    k = lax.broadcasted_iota(jnp.int32, (ATT_Q, ATT_K), 1)
    off = k // CHUNK - q // CHUNK
    in_band = (off >= 0) & (off < BAND_CHUNKS)
    first_key = LEFT_ROWS - ATT_Q * jnp.arange(ATT_VARIANTS, dtype=jnp.int32)
    ok = in_band[None] & (k[None] >= first_key[:, None, None])
    full = jnp.where(ok[None], bias[:, None] * LOG2E, NEG_INF)
    full = full.reshape(h // 2, 2, ATT_VARIANTS, ATT_Q, ATT_K)
    return full.transpose(0, 2, 4, 1, 3).reshape(h // 2, ATT_VARIANTS, ATT_K, 2 * ATT_Q)


def _attention(proj3, qn_a, kn_a, rel_bias):
    b, s, _ = proj3.shape
    q_rows = min(s, 1024)
    qn = (jnp.tile(qn_a, 2) * (HEAD_DIM ** -0.5 * LOG2E)).reshape(1, LANES)
    kn = jnp.tile(kn_a, 2).reshape(1, LANES)
    pairs = A_HEADS // 2
    kcol, vcol = A_WIDTH // LANES, 2 * A_WIDTH // LANES
    return pl.pallas_call(
        functools.partial(_attn_kernel, q_rows=q_rows),
        grid=(b, pairs, s // q_rows),
        in_specs=[pl.BlockSpec((None, q_rows, LANES), lambda i, p, j: (i, j, p)),
                  pl.BlockSpec((None, s, LANES), lambda i, p, j: (i, 0, kcol + p)),
                  pl.BlockSpec((None, s, LANES), lambda i, p, j: (i, 0, vcol + p)),
                  pl.BlockSpec((1, LANES), lambda i, p, j: (0, 0)),
                  pl.BlockSpec((1, LANES), lambda i, p, j: (0, 0)),
                  pl.BlockSpec((None, ATT_VARIANTS, ATT_K, 2 * ATT_Q), lambda i, p, j: (p, 0, 0, 0))],
        out_specs=pl.BlockSpec((None, q_rows, LANES), lambda i, p, j: (i, j, p)),
        out_shape=jax.ShapeDtypeStruct((b, s, A_WIDTH), _BF16),
        scratch_shapes=[pltpu.VMEM((s + LEFT_ROWS, LANES), _BF16),
                        pltpu.VMEM(((s + LEFT_ROWS) // LANES, LANES, LANES), _BF16),
                        pltpu.VMEM((2, ATT_K, 2 * ATT_Q), _F32)],
        compiler_params=_cparams(3),
        name="attn_a",
    )(proj3, proj3, proj3, qn, kn, _attn_bias(rel_bias))


def _swap_halves(t):
    first = (_lane(t.shape) % HEAD_DIM) < (HEAD_DIM // 2)
    return jnp.where(first, pltpu.roll(t, LANES - HEAD_DIM // 2, 1), pltpu.roll(t, HEAD_DIM // 2, 1))


def _retention_kernel(q_ref, k_ref, v_ref, gate_ref, cos_ref, sin_ref, decay_ref, zeta_ref, xi_ref,
                      cd_ref, gn_ref, o_ref, state_ref, *, rows):
    @pl.when(pl.program_id(2) == 0)
    def _():
        state_ref[...] = jnp.zeros_like(state_ref)

    c = RET_CHUNK
    low = _lane((c, LANES)) < HEAD_DIM
    srow = lax.broadcasted_iota(jnp.int32, (LANES, LANES), 0) < HEAD_DIM
    scol = _lane((LANES, LANES)) < HEAD_DIM
    same_head = srow == scol

    for j in range(rows // c):
        sl = slice(j * c, (j + 1) * c)
        cos, sin = cos_ref[sl, :], sin_ref[sl, :]
        q = q_ref[sl, :]
        k = k_ref[sl, :]
        qr = q * cos + _swap_halves(q) * sin
        kr = (k * cos + _swap_halves(k) * sin) * (HEAD_DIM ** -0.5)
        vb = v_ref[sl, :].astype(_BF16)
        qb = qr.astype(_BF16)
        kb = kr.astype(_BF16)
        inner_out = []
        for h in range(2):
            qh = jnp.where(low if h == 0 else ~low, qr, 0.0).astype(_BF16)
            inner = _dot_nt(qh, kb) * decay_ref[h]
            inner_out.append(_dot(inner.astype(_BF16), vb))
        state = state_ref[...]
        cross = _dot(qb, state.astype(_BF16)) * xi_ref[...]
        o = jnp.where(low, inner_out[0], inner_out[1]) + cross
        kz = (kr * zeta_ref[...]).T.astype(_BF16)
        state_ref[...] = cd_ref[...] * state + jnp.where(same_head, _dot(kz, vb), 0.0)
        mu = jnp.where(low,
                       jnp.sum(jnp.where(low, o, 0.0), axis=-1, keepdims=True),
                       jnp.sum(jnp.where(low, 0.0, o), axis=-1, keepdims=True)) * (1.0 / HEAD_DIM)
        dlt = o - mu
        d2 = dlt * dlt
        var = jnp.where(low,
                        jnp.sum(jnp.where(low, d2, 0.0), axis=-1, keepdims=True),
                        jnp.sum(jnp.where(low, 0.0, d2), axis=-1, keepdims=True)) * (1.0 / HEAD_DIM)
        y = (dlt * lax.rsqrt(var + EPS)) * gn_ref[...]
        g = gate_ref[sl, :]
        o_ref[sl, :] = ((g * jax.nn.sigmoid(g)) * y).astype(o_ref.dtype)


def _retention_tables():
    c = RET_CHUNK
    log_g = jnp.log(1.0 - jnp.exp2(-5.0 - jnp.arange(B_HEADS, dtype=_F32)))
    idx = jnp.arange(c, dtype=_F32)
    diff = idx[:, None] - idx[None, :]
    decay = jnp.where(diff >= 0, jnp.exp(log_g[:, None, None] * jnp.maximum(diff, 0.0)), 0.0)
    zeta = jnp.exp(log_g[:, None] * (c - 1 - idx))
    xi = jnp.exp(log_g[:, None] * (idx + 1.0))
    cd = jnp.exp(log_g * c)

    def lanes(tab):
        return jnp.repeat(tab.reshape(B_HEADS // 2, 2, c), HEAD_DIM, axis=1).transpose(0, 2, 1)

    cdm = jnp.repeat(cd.reshape(B_HEADS // 2, 2), HEAD_DIM, axis=1)
    cdm = jnp.broadcast_to(cdm[:, :, None], (B_HEADS // 2, LANES, LANES))
    return decay, lanes(zeta), lanes(xi), cdm


def _retention(proj3, cos, sin, ret_gn_g):
    b, s, _ = proj3.shape
    rows = min(s, 1024)
    pairs = B_HEADS // 2
    base = 3 * A_WIDTH // LANES
    decay, zeta, xi, cdm = _retention_tables()
    cos3, sin3 = cos.reshape(b, s, LANES), sin.reshape(b, s, LANES)
    gn = ret_gn_g.reshape(pairs, 1, LANES)

    def col(off):
        return pl.BlockSpec((None, rows, LANES), lambda i, p, j: (i, j, base + off * pairs + p))

    tab = pl.BlockSpec((None, rows, LANES), lambda i, p, j: (i, j, 0))
    return pl.pallas_call(
        functools.partial(_retention_kernel, rows=rows),
        grid=(b, pairs, s // rows),
        in_specs=[col(0), col(1), col(2), col(3), tab, tab,
                  pl.BlockSpec((2, RET_CHUNK, RET_CHUNK), lambda i, p, j: (p, 0, 0)),
                  pl.BlockSpec((None, RET_CHUNK, LANES), lambda i, p, j: (p, 0, 0)),
                  pl.BlockSpec((None, RET_CHUNK, LANES), lambda i, p, j: (p, 0, 0)),
                  pl.BlockSpec((None, LANES, LANES), lambda i, p, j: (p, 0, 0)),
                  pl.BlockSpec((None, 1, LANES), lambda i, p, j: (p, 0, 0))],
        out_specs=pl.BlockSpec((None, rows, LANES), lambda i, p, j: (i, j, p)),
        out_shape=jax.ShapeDtypeStruct((b, s, B_WIDTH), _BF16),
        scratch_shapes=[pltpu.VMEM((LANES, LANES), _F32)],
        compiler_params=_cparams(3),
        name="retention_b",
    )(proj3, proj3, proj3, proj3, cos3, sin3, decay, zeta, xi, cdm, gn)


def _cross_kernel(q_ref, k_ref, vt_ref, qn_ref, o_ref, st_ref, *, rows):
    low = _lane((ATT_Q, LANES)) < HEAD_DIM
    lane_blocks = C_WIDTH // LANES
    tiles = [(j, lb) for j in range(rows // ATT_Q) for lb in range(lane_blocks)]

    def scores(i):
        j, lb = tiles[i]
        sl = slice(lb * LANES, (lb + 1) * LANES)
        qn = _pair_rms(q_ref[j * ATT_Q:(j + 1) * ATT_Q, sl], qn_ref[...])
        q2 = jnp.concatenate([jnp.where(low, qn, 0.0), jnp.where(low, 0.0, qn)], axis=0).astype(_BF16)
        st_ref[i % 2] = _dot_nt(k_ref[:, sl], q2)

    def finish(i):
        j, lb = tiles[i]
        sl = slice(lb * LANES, (lb + 1) * LANES)
        st = st_ref[i % 2]
        p = jnp.exp2(st - jnp.max(st, axis=0, keepdims=True))
        inv = 1.0 / jnp.sum(p, axis=0, keepdims=True)
        ot = _dot(vt_ref[sl, :], p.astype(_BF16))
        out_t = jnp.concatenate([ot[0:HEAD_DIM, 0:ATT_Q] * inv[:, 0:ATT_Q],
                                 ot[HEAD_DIM:, ATT_Q:] * inv[:, ATT_Q:]], axis=0)
        o_ref[j * ATT_Q:(j + 1) * ATT_Q, sl] = out_t.T.astype(o_ref.dtype)

    scores(0)
    for i in range(len(tiles)):
        if i + 1 < len(tiles):
            scores(i + 1)
        finish(i)


def _cross_attention(proj3, kc, vtc, qn_c):
    b, s, _ = proj3.shape
    m = kc.shape[1]
    rows = min(s, 512)
    qn = (jnp.tile(qn_c, 2) * (HEAD_DIM ** -0.5 * LOG2E)).reshape(1, LANES)
    qcol = (3 * A_WIDTH + 4 * B_WIDTH) // C_WIDTH
    return pl.pallas_call(
        functools.partial(_cross_kernel, rows=rows),
        grid=(b, s // rows),
        in_specs=[pl.BlockSpec((None, rows, C_WIDTH), lambda i, j: (i, j, qcol)),
                  pl.BlockSpec((None, m, C_WIDTH), lambda i, j: (i, 0, 0)),
                  pl.BlockSpec((None, C_WIDTH, m), lambda i, j: (i, 0, 0)),
                  pl.BlockSpec((1, LANES), lambda i, j: (0, 0))],
        out_specs=pl.BlockSpec((None, rows, C_WIDTH), lambda i, j: (i, j, 0)),
        out_shape=jax.ShapeDtypeStruct((b, s, C_WIDTH), _BF16),
        scratch_shapes=[pltpu.VMEM((2, m, 2 * ATT_Q), _F32)],
        compiler_params=_cparams(2),
        name="cross_c",
    )(proj3, kc, vtc, qn)


def _out_router_kernel(x_ref, a_ref, b_ref, c_ref, wo_ref, g_ref, wr_ref, br_ref,
                       h_ref, hn_ref, info_ref, rows_ref, cnt_ref, carry_ref):
    @pl.when(pl.program_id(0) == 0)
    def _():
        carry_ref[...] = jnp.zeros_like(carry_ref)

    tm = x_ref.shape[0]
    h = x_ref[...]
    h = h + _dot(a_ref[...], wo_ref[0:A_WIDTH, :])
    h = h + _dot(b_ref[...], wo_ref[A_WIDTH:A_WIDTH + B_WIDTH, :])
    h = h + _dot(c_ref[...], wo_ref[A_WIDTH + B_WIDTH:, :])
    h_ref[...] = h
    ms = jnp.mean(h * h, axis=-1, keepdims=True)
    hn = (h * lax.rsqrt(ms + EPS)) * g_ref[...]
    _rows_to_tiles(hn_ref, hn)
    logits = _dot_nt(wr_ref[...], hn.astype(_BF16))[0:ROUTE_ROWS, :] + br_ref[:, 0:1]
    row = lax.broadcasted_iota(jnp.int32, (ROUTE_ROWS, tm), 0).astype(_F32)
    big = float(ROUTE_ROWS)

    def first_row(mask):
        return jnp.min(jnp.where(mask, row, big), axis=0, keepdims=True)

    gmask = row < N_GROUPS
    gl = jnp.where(gmask, logits, NEG_INF)
    ge = jnp.exp(gl - jnp.max(gl, axis=0, keepdims=True))
    gp = ge / jnp.sum(ge, axis=0, keepdims=True)
    p_group = jnp.max(gp, axis=0, keepdims=True)
    g_sel = first_row(gmask & (gp == p_group))
    lo = ROUTE_LANE0 + g_sel * EXPERTS_PER_GROUP
    emask = (row >= lo) & (row < lo + EXPERTS_PER_GROUP)
    el = jnp.where(emask, logits, NEG_INF)
    ee = jnp.exp(el - jnp.max(el, axis=0, keepdims=True))
    ep = ee / jnp.sum(ee, axis=0, keepdims=True)
    p1 = jnp.max(ep, axis=0, keepdims=True)
    i1 = first_row(emask & (ep == p1))
    ep2 = jnp.where(emask & (row != i1), ep, -1.0)
    p2 = jnp.max(ep2, axis=0, keepdims=True)
    i2 = first_row(ep2 == p2)
    den = p1 + p2
    w1 = p_group * (p1 / den)
    w2 = p_group * (p2 / den)
    hit1 = row == i1
    hit2 = row == i2
    onehot = jnp.where(hit1 | hit2, 1.0, 0.0)
    r_i = lax.broadcasted_iota(jnp.int32, (tm, tm), 0)
    c_i = lax.broadcasted_iota(jnp.int32, (tm, tm), 1)
    earlier = jnp.where(r_i < c_i, 1.0, 0.0).astype(_BF16)
    before = _dot(onehot.astype(_BF16), earlier) + carry_ref[:, 0:1]
    r1 = jnp.sum(jnp.where(hit1, before, 0.0), axis=0, keepdims=True)
    r2 = jnp.sum(jnp.where(hit2, before, 0.0), axis=0, keepdims=True)
    carry_ref[...] = carry_ref[...] + jnp.sum(onehot, axis=1, keepdims=True)
    cnt_ref[...] = carry_ref[...]
    out_row = lax.broadcasted_iota(jnp.int32, (LANES, tm), 0)
    info = jnp.where(out_row == 0, w1, 0.0)
    info = jnp.where(out_row == 1, w2, info)
    info = jnp.where(out_row == 2, i1 - ROUTE_LANE0, info)
    info = jnp.where(out_row == 3, i2 - ROUTE_LANE0, info)
    info = jnp.where(out_row == 4, r1, info)
    info = jnp.where(out_row == 5, r2, info)
    rows_ref[...] = info[0:SUBLANES, :]
    info_ref[...] = info.T


def _out_router(x2, oa, ob, oc, w_out, ffn_g, w_rg, b_rg, w_re, b_re):
    t, d = x2.shape
    tm = min(t, 512)
    pad = LANES - N_GROUPS - N_EXPERTS
    wr = jnp.concatenate([w_rg, w_re, jnp.zeros((d, pad), _F32)], axis=1).T.astype(_BF16)
    br = jnp.concatenate([b_rg, b_re, jnp.zeros((ROUTE_ROWS - N_GROUPS - N_EXPERTS,), _F32)])
    br = jnp.broadcast_to(br[:, None], (ROUTE_ROWS, LANES))

    def rows(w):
        return pl.BlockSpec((tm, w), lambda i: (i, 0))

    def whole(r, c):
        return pl.BlockSpec((r, c), lambda i: (0, 0))

    return pl.pallas_call(
        _out_router_kernel,
        grid=(t // tm,),
        in_specs=[rows(d), rows(A_WIDTH), rows(B_WIDTH), rows(C_WIDTH), whole(d, d), whole(1, d),
                  whole(LANES, d), whole(ROUTE_ROWS, LANES)],
        out_specs=[rows(d), pl.BlockSpec((tm * SUBLANES, LANES), lambda i: (i, 0)), rows(LANES),
                   pl.BlockSpec((SUBLANES, tm), lambda i: (0, i)), whole(ROUTE_ROWS, LANES)],
        out_shape=[jax.ShapeDtypeStruct((t, d), _F32), jax.ShapeDtypeStruct((t * SUBLANES, LANES), _F32),
                   jax.ShapeDtypeStruct((t, LANES), _F32), jax.ShapeDtypeStruct((SUBLANES, t), _F32),
                   jax.ShapeDtypeStruct((ROUTE_ROWS, LANES), _F32)],
        scratch_shapes=[pltpu.VMEM((ROUTE_ROWS, LANES), _F32)],
        compiler_params=_cparams(1),
        name="out_router",
    )(x2, oa, ob, oc, w_out.astype(_BF16), ffn_g.reshape(1, d), wr, br)


DISPATCH_TOKENS = 512
COMBINE_TOKENS = 256


ROW_UNROLL = 8


def _tile_rows(row, count=1):
    start = row * SUBLANES
    if not isinstance(start, int):
        start = pl.multiple_of(start, SUBLANES)
    return pl.ds(start, count * SUBLANES)


def _row_copy(src, s_row, dst, d_row, sem):
    return pltpu.make_async_copy(src.at[_tile_rows(s_row)], dst.at[_tile_rows(d_row)], sem)


def _dispatch_kernel(pad_start_ref, pad_len_ref, used_ref, dest_ref, hn_ref, xs_ref, zero_ref, sem,
                     pad_sem):
    n = hn_ref.shape[0] // SUBLANES

    @pl.when(pl.program_id(0) == 0)
    def _():
        zero_ref[...] = jnp.zeros_like(zero_ref)
        n_blocks = xs_ref.shape[0] // (ROW_BLOCK * SUBLANES)

        def block_copy(blk):
            return pltpu.make_async_copy(zero_ref, xs_ref.at[_tile_rows(blk * ROW_BLOCK, ROW_BLOCK)], pad_sem)

        def put_block(blk, carry):
            block_copy(blk).start()
            return carry

        def done_block(blk, carry):
            block_copy(blk).wait()
            return carry

        lax.fori_loop(used_ref[0], n_blocks, put_block, 0)
        lax.fori_loop(used_ref[0], n_blocks, done_block, 0)
        bits = [1 << k for k in reversed(range(ROW_BLOCK.bit_length() - 1))]

        def tail(e, wait):
            row = pad_start_ref[e]
            for bit in bits:
                on = (pad_len_ref[e] & bit) != 0
                copy = pltpu.make_async_copy(zero_ref.at[_tile_rows(0, bit)], xs_ref.at[_tile_rows(row, bit)],
                                             pad_sem)

                @pl.when(on)
                def _():
                    copy.wait() if wait else copy.start()

                row = row + jnp.where(on, bit, 0)

        def put_tail(e, carry):
            tail(e, False)
            return carry

        def done_tail(e, carry):
            tail(e, True)
            return carry

        lax.fori_loop(0, N_EXPERTS, put_tail, 0)
        lax.fori_loop(0, N_EXPERTS, done_tail, 0)

    def issue(i, carry):
        for u in range(ROW_UNROLL):
            t = i * ROW_UNROLL + u
            _row_copy(hn_ref, t, xs_ref, dest_ref[2 * t], sem).start(priority=0)
            _row_copy(hn_ref, t, xs_ref, dest_ref[2 * t + 1], sem).start(priority=1)
        return carry

    lax.fori_loop(0, n // ROW_UNROLL, issue, 0)
    for _ in range(2):
        pltpu.make_async_copy(hn_ref, xs_ref.at[_tile_rows(0, n)], sem).wait()


def _dispatch(hn, dest, pad_start, pad_len, n_used, n_rows):
    t = hn.shape[0] // SUBLANES
    n = min(t, DISPATCH_TOKENS)
    return pl.pallas_call(
        _dispatch_kernel,
        grid_spec=pltpu.PrefetchScalarGridSpec(
            num_scalar_prefetch=3,
            grid=(t // n,),
            in_specs=[pl.BlockSpec((2 * n,), lambda i, *_: (i,), memory_space=pltpu.SMEM),
                      pl.BlockSpec((n * SUBLANES, LANES), lambda i, *_: (i, 0))],
            out_specs=pl.BlockSpec(memory_space=pl.ANY),
            scratch_shapes=[pltpu.VMEM((ROW_BLOCK * SUBLANES, LANES), hn.dtype), pltpu.SemaphoreType.DMA,
                            pltpu.SemaphoreType.DMA]),
        out_shape=jax.ShapeDtypeStruct((n_rows * SUBLANES, LANES), hn.dtype),
        compiler_params=_cparams(1),
        name="moe_dispatch",
    )(pad_start, pad_len, n_used, dest, hn)


def _expert_kernel(be_ref, used_ref, x_ref, wg_ref, wu_ref, wd_ref, y_ref, wg_bf, wu_bf, wd_bf):
    i = pl.program_id(0)
    live = i < used_ref[0]
    new_expert = (i == 0) | (be_ref[i] != be_ref[jnp.maximum(i - 1, 0)])

    @pl.when(live & new_expert)
    def _():
        wg_bf[...] = wg_ref[...].astype(_BF16)
        wu_bf[...] = wu_ref[...].astype(_BF16)
        wd_bf[...] = wd_ref[...].astype(_BF16)

    @pl.when(live)
    def _():
        half = ROW_BLOCK // 2
        gate_up = []
        for hh in range(2):
            x = _tiles_to_rows(x_ref, half, hh * half).astype(_BF16)
            gate_up.append((_dot(x, wg_bf[...]), _dot(x, wu_bf[...])))
        for hh in range(2):
            gate, up = gate_up[hh]
            act = (gate * jax.nn.sigmoid(gate)) * up
            _rows_to_tiles(y_ref, _dot(act.astype(_BF16), wd_bf[...]), hh * half)

    @pl.when(i >= used_ref[0])
    def _():
        y_ref[...] = jnp.zeros_like(y_ref)


def _experts(xs, block_e, n_used, w_gate, w_up, w_down):
    n_rows, d = xs.shape[0] // SUBLANES, D_MODEL
    n_blocks = n_rows // ROW_BLOCK
    tile_block = (ROW_BLOCK * SUBLANES, LANES)

    def xmap(i, be, used):
        return (jnp.minimum(i, used[0] - 1), 0)

    def wmap(i, be, used):
        return (be[jnp.minimum(i, used[0] - 1)], 0, 0)

    return pl.pallas_call(
        _expert_kernel,
        grid_spec=pltpu.PrefetchScalarGridSpec(
            num_scalar_prefetch=2,
            grid=(n_blocks,),
            in_specs=[pl.BlockSpec(tile_block, xmap),
                      pl.BlockSpec((None, d, D_EXPERT), wmap),
                      pl.BlockSpec((None, d, D_EXPERT), wmap),
                      pl.BlockSpec((None, D_EXPERT, d), wmap)],
            out_specs=pl.BlockSpec(tile_block, lambda i, be, used: (i, 0)),
            scratch_shapes=[pltpu.VMEM((d, D_EXPERT), _BF16), pltpu.VMEM((d, D_EXPERT), _BF16),
                            pltpu.VMEM((D_EXPERT, d), _BF16)]),
        out_shape=jax.ShapeDtypeStruct((n_rows * SUBLANES, LANES), _F32),
        compiler_params=_cparams(1),
        name="moe_experts",
    )(block_e, n_used, xs, w_gate, w_up, w_down)


def _combine_kernel(dest_ref, next_ref, h_ref, info_ref, ys_ref, o_ref, buf_ref, sem):
    n = h_ref.shape[0]
    step = pl.program_id(0)
    slot = step % 2

    def gather(idx_ref, to_slot):
        def issue(i, carry):
            for u in range(ROW_UNROLL):
                t = i * ROW_UNROLL + u
                _row_copy(ys_ref, idx_ref[2 * t], buf_ref.at[to_slot, 0], t,
                          sem.at[to_slot]).start(priority=0)
                _row_copy(ys_ref, idx_ref[2 * t + 1], buf_ref.at[to_slot, 1], t,
                          sem.at[to_slot]).start(priority=1)
            return carry

        lax.fori_loop(0, n // ROW_UNROLL, issue, 0)

    @pl.when(step == 0)
    def _():
        gather(dest_ref, 0)

    @pl.when(step + 1 < pl.num_programs(0))
    def _():
        gather(next_ref, 1 - slot)

    for k in range(2):
        pltpu.make_async_copy(ys_ref.at[_tile_rows(0, n)], buf_ref.at[slot, k], sem.at[slot]).wait()
    info = info_ref[...]
    w0 = info[:, 0:1]
    w1 = info[:, 1:2]
    for s in range(SUBLANES):
        sl = slice(s * LANES, (s + 1) * LANES)
        moe = w0 * _tile_block(buf_ref.at[slot, 0], s, n) + w1 * _tile_block(buf_ref.at[slot, 1], s, n)
        o_ref[:, sl] = h_ref[:, sl] + moe


def _combine(h, info, ys, dest):
    t, d = h.shape
    n = min(t, COMBINE_TOKENS)
    steps = t // n
    return pl.pallas_call(
        _combine_kernel,
        grid=(steps,),
        in_specs=[pl.BlockSpec((2 * n,), lambda i: (i,), memory_space=pltpu.SMEM),
                  pl.BlockSpec((2 * n,), lambda i: (jnp.minimum(i + 1, steps - 1),),
                               memory_space=pltpu.SMEM),
                  pl.BlockSpec((n, d), lambda i: (i, 0)),
                  pl.BlockSpec((n, LANES), lambda i: (i, 0)),
                  pl.BlockSpec(memory_space=pl.ANY)],
        out_specs=pl.BlockSpec((n, d), lambda i: (i, 0)),
        out_shape=jax.ShapeDtypeStruct((t, d), _F32),
        scratch_shapes=[pltpu.VMEM((2, 2, n * SUBLANES, LANES), _F32), pltpu.SemaphoreType.DMA((2,))],
        compiler_params=_cparams(1),
        name="moe_combine",
    )(dest, dest, h, info, ys)


def _moe_layout(route_rows, counts, t):
    counts = counts[ROUTE_LANE0:ROUTE_LANE0 + N_EXPERTS, 0].astype(jnp.int32)
    padded = (counts + ROW_BLOCK - 1) // ROW_BLOCK * ROW_BLOCK
    pends = jnp.cumsum(padded)
    pstarts = pends - padded
    eid = route_rows[2:4].astype(jnp.int32)
    rank = route_rows[4:6].astype(jnp.int32)
    experts = jnp.arange(N_EXPERTS, dtype=jnp.int32)
    start_of = jnp.sum(jnp.where(eid[:, :, None] == experts, pstarts, 0), axis=-1)
    dest = (start_of + rank).T.reshape(-1)
    n_blocks = -(-2 * t // ROW_BLOCK) + N_EXPERTS
    first_row = jnp.arange(n_blocks, dtype=jnp.int32) * ROW_BLOCK
    block_e = jnp.minimum(jnp.sum((pends[None, :] <= first_row[:, None]).astype(jnp.int32), axis=1),
                          N_EXPERTS - 1)
    n_used = (pends[-1:] // ROW_BLOCK).astype(jnp.int32)
    return dest, block_e, n_used, pstarts + counts, padded - counts, n_blocks * ROW_BLOCK


def kernel(x, mem, positions, mix_norm_g, w_in, qn_a, kn_a, rel_bias, ret_gn_g, mem_norm_g, w_mem_kv,
           qn_c, kn_c, w_out, ffn_norm_g, w_router_group, b_router_group, w_router_expert,
           b_router_expert, w_gate, w_up, w_down):
    b, s, d = x.shape
    t = b * s
    x2 = x.reshape(t, d)
    cos, sin = _rope_tables(positions)
    kc, vc = _mem_kv(mem, mem_norm_g, w_mem_kv, kn_c)
    proj3 = _in_proj(x2, mix_norm_g, w_in).reshape(b, s, IN_COLS)
    out_a = _attention(proj3, qn_a, kn_a, rel_bias)
    out_b = _retention(proj3, cos, sin, ret_gn_g)
    out_c = _cross_attention(proj3, kc, vc, qn_c)
    h, hn, info, route_rows, counts = _out_router(
        x2, out_a.reshape(t, A_WIDTH), out_b.reshape(t, B_WIDTH), out_c.reshape(t, C_WIDTH),
        w_out, ffn_norm_g, w_router_group, b_router_group, w_router_expert, b_router_expert)
    dest, block_e, n_used, pad_start, pad_len, n_rows = _moe_layout(route_rows, counts, t)
    xs = _dispatch(hn, dest, pad_start, pad_len, n_used, n_rows)
    ys = _experts(xs, block_e, n_used, w_gate, w_up, w_down)
    return _combine(h, info, ys, dest).reshape(b, s, d)
```

```python
import functools

import jax
import jax.numpy as jnp
from jax import lax
from jax.experimental import pallas as pl
from jax.experimental.pallas import tpu as pltpu

D_MODEL = 1024
CHUNK = 64
HEAD_DIM = 64
A_HEADS = 8
B_HEADS = 4
C_HEADS = 4
A_WIDTH = A_HEADS * HEAD_DIM
B_WIDTH = B_HEADS * HEAD_DIM
C_WIDTH = C_HEADS * HEAD_DIM
IN_COLS = 3 * A_WIDTH + 4 * B_WIDTH + C_WIDTH
LEFT_CHUNKS = 8
BAND_CHUNKS = LEFT_CHUNKS + 1
MAX_REL_DIST = 128
ROPE_BASE = 10000.0
N_GROUPS = 4
EXPERTS_PER_GROUP = 8
N_EXPERTS = N_GROUPS * EXPERTS_PER_GROUP
D_EXPERT = D_MODEL // 2
EPS = 1e-6
NEG_INF = -1e30
LOG2E = 1.4426950408889634

LANES = 128
SUBLANES = 8
assert D_MODEL == SUBLANES * LANES
LEFT_ROWS = LEFT_CHUNKS * CHUNK
ATT_Q = 2 * CHUNK
ATT_K = ATT_Q + LEFT_ROWS
ATT_VARIANTS = LEFT_ROWS // ATT_Q + 1
RET_CHUNK = 256
ROW_BLOCK = 512
ROUTE_LANE0 = N_GROUPS
ROUTE_ROWS = 64
VMEM_LIMIT = 48 * 1024 * 1024

_F32 = jnp.float32
_BF16 = jnp.bfloat16


def _cparams(n_axes):
    return pltpu.CompilerParams(dimension_semantics=("arbitrary",) * n_axes,
                                vmem_limit_bytes=VMEM_LIMIT)


def _dot(a, b):
    return jnp.dot(a, b, preferred_element_type=_F32)


def _dot_nt(a, b):
    return lax.dot_general(a, b, (((1,), (1,)), ((), ())), preferred_element_type=_F32)


def _lane(shape):
    return lax.broadcasted_iota(jnp.int32, shape, len(shape) - 1)


def _pair_rms(t, gain):
    low = _lane(t.shape) < HEAD_DIM
    t2 = t * t
    ms0 = jnp.sum(jnp.where(low, t2, 0.0), axis=-1, keepdims=True) * (1.0 / HEAD_DIM)
    ms1 = jnp.sum(jnp.where(low, 0.0, t2), axis=-1, keepdims=True) * (1.0 / HEAD_DIM)
    r = jnp.where(low, lax.rsqrt(ms0 + EPS), lax.rsqrt(ms1 + EPS))
    return (t * r) * gain


def _rows_to_tiles(ref, val, row0=0):
    n = val.shape[0]
    for s in range(SUBLANES):
        ref[pl.ds(row0 * SUBLANES + s, n, stride=SUBLANES), :] = val[:, s * LANES:(s + 1) * LANES]


def _tile_block(ref, s, n, row0=0):
    return ref[pl.ds(row0 * SUBLANES + s, n, stride=SUBLANES), :]


def _tiles_to_rows(ref, n, row0=0):
    return jnp.concatenate([_tile_block(ref, s, n, row0) for s in range(SUBLANES)], axis=-1)


ROPE_HALF = HEAD_DIM // 2
ROPE_PACK = LANES // ROPE_HALF


def _rope_kernel(pos_ref, inv_ref, cos_ref, sin_ref):
    ang = pos_ref[...].astype(_F32) * inv_ref[...]
    rows = ang.shape[0]
    lane = _lane(ang.shape)
    sign = jnp.where((lane % HEAD_DIM) < ROPE_HALF, -1.0, 1.0)
    for out_ref, val in ((cos_ref, jnp.cos(ang)), (sin_ref, jnp.sin(ang))):
        for j in range(ROPE_PACK):
            seg = jnp.where(lane // ROPE_HALF == j, val, 0.0)
            full = seg
            for k in range(1, ROPE_PACK):
                full = full + pltpu.roll(seg, k * ROPE_HALF, 1)
            if out_ref is sin_ref:
                full = full * sign
            out_ref[pl.ds(j, rows, stride=ROPE_PACK), :] = full


def _rope_tables(positions):
    t = positions.size
    inv = ROPE_BASE ** (-jnp.arange(ROPE_HALF, dtype=_F32) / ROPE_HALF)
    inv128 = jnp.tile(inv, ROPE_PACK).reshape(1, LANES)
    rows = t // ROPE_PACK
    pos = jnp.repeat(positions.reshape(rows, ROPE_PACK), ROPE_HALF, axis=1)
    tm = min(rows, 512)
    out = pl.BlockSpec((tm * ROPE_PACK, LANES), lambda i: (i, 0))
    return pl.pallas_call(
        _rope_kernel,
        grid=(rows // tm,),
        in_specs=[pl.BlockSpec((tm, LANES), lambda i: (i, 0)), pl.BlockSpec((1, LANES), lambda i: (0, 0))],
        out_specs=[out, out],
        out_shape=[jax.ShapeDtypeStruct((t, LANES), _F32)] * 2,
        compiler_params=_cparams(1),
        name="rope_tables",
    )(pos, inv128)


def _mem_kv_kernel(mem_ref, g_ref, w_ref, kn_ref, k_ref, v_ref):
    m = mem_ref[...]
    ms = jnp.mean(m * m, axis=-1, keepdims=True)
    mn = (m * lax.rsqrt(ms + EPS)) * g_ref[...]
    kv = _dot(mn.astype(_BF16), w_ref[...])
    for j in range(C_WIDTH // LANES):
        sl = slice(j * LANES, (j + 1) * LANES)
        k_ref[:, sl] = _pair_rms(kv[:, sl], kn_ref[...]).astype(_BF16)
    v_ref[...] = kv[:, C_WIDTH:].T.astype(_BF16)


def _mem_kv(mem, mem_norm_g, w_mem_kv, kn_c):
    b, m, d = mem.shape
    kn = jnp.tile(kn_c, 2).reshape(1, LANES)
    return pl.pallas_call(
        _mem_kv_kernel,
        grid=(b,),
        in_specs=[pl.BlockSpec((None, m, d), lambda i: (i, 0, 0)),
                  pl.BlockSpec((1, d), lambda i: (0, 0)),
                  pl.BlockSpec((d, 2 * C_WIDTH), lambda i: (0, 0)),
                  pl.BlockSpec((1, LANES), lambda i: (0, 0))],
        out_specs=[pl.BlockSpec((None, m, C_WIDTH), lambda i: (i, 0, 0)),
                   pl.BlockSpec((None, C_WIDTH, m), lambda i: (i, 0, 0))],
        out_shape=[jax.ShapeDtypeStruct((b, m, C_WIDTH), _BF16),
                   jax.ShapeDtypeStruct((b, C_WIDTH, m), _BF16)],
        compiler_params=_cparams(1),
        name="mem_kv",
    )(mem, mem_norm_g.reshape(1, d), w_mem_kv.astype(_BF16), kn)


def _in_proj_kernel(x_ref, g_ref, w_ref, o_ref):
    x = x_ref[...]
    ms = jnp.mean(x * x, axis=-1, keepdims=True)
    xn = ((x * lax.rsqrt(ms + EPS)) * g_ref[...]).astype(_BF16)
    o_ref[...] = _dot(xn, w_ref[...])


def _in_proj(x2, g, w_in):
    t, d = x2.shape
    tm = min(t, 512)
    return pl.pallas_call(
        _in_proj_kernel,
        grid=(t // tm,),
        in_specs=[pl.BlockSpec((tm, d), lambda i: (i, 0)),
                  pl.BlockSpec((1, d), lambda i: (0, 0)),
                  pl.BlockSpec((d, IN_COLS), lambda i: (0, 0))],
        out_specs=pl.BlockSpec((tm, IN_COLS), lambda i: (i, 0)),
        out_shape=jax.ShapeDtypeStruct((t, IN_COLS), _F32),
        compiler_params=_cparams(1),
        name="in_proj",
    )(x2, g.reshape(1, d), w_in.astype(_BF16))


def _attn_kernel(q_ref, k_ref, v_ref, qn_ref, kn_ref, bias_ref, o_ref, kp_ref, vt_ref, st_ref, *, q_rows):
    qs = pl.program_id(2)
    s = k_ref.shape[0]
    fill_rows = min(s, 512)
    left_blocks = LEFT_ROWS // LANES

    @pl.when(qs == 0)
    def _():
        kp_ref[0:LEFT_ROWS, :] = jnp.zeros((LEFT_ROWS, LANES), _BF16)
        for blk in range(left_blocks):
            vt_ref[blk] = jnp.zeros((LANES, LANES), _BF16)
        eye = jnp.where(lax.broadcasted_iota(jnp.int32, (LANES, LANES), 0) == _lane((LANES, LANES)),
                        1.0, 0.0).astype(_BF16)

        def fill(i, carry):
            r = pl.multiple_of(i * fill_rows, fill_rows)
            kp_ref[pl.ds(LEFT_ROWS + r, fill_rows), :] = _pair_rms(
                k_ref[pl.ds(r, fill_rows), :], kn_ref[...]).astype(_BF16)
            vt = _dot_nt(eye, v_ref[pl.ds(r, fill_rows), :].astype(_BF16))
            for j in range(fill_rows // LANES):
                vt_ref[left_blocks + i * (fill_rows // LANES) + j] = vt[:, j * LANES:(j + 1) * LANES].astype(_BF16)
            return carry

        lax.fori_loop(0, s // fill_rows, fill, 0)

    low = _lane((ATT_Q, LANES)) < HEAD_DIM
    tiles_per_step = q_rows // ATT_Q

    def scores(j):
        cp = qs * tiles_per_step + j
        qn = _pair_rms(q_ref[j * ATT_Q:(j + 1) * ATT_Q, :], qn_ref[...])
        q2 = jnp.concatenate([jnp.where(low, qn, 0.0), jnp.where(low, 0.0, qn)], axis=0).astype(_BF16)
        kb = kp_ref[pl.ds(pl.multiple_of(cp * ATT_Q, ATT_Q), ATT_K), :]
        st_ref[j % 2] = _dot_nt(kb, q2) + bias_ref[jnp.minimum(cp, ATT_VARIANTS - 1)]

    def finish(j):
        cp = qs * tiles_per_step + j
        st = st_ref[j % 2]
        m = jnp.max(st, axis=0, keepdims=True)
        p = jnp.exp2(st - m)
        inv = 1.0 / jnp.sum(p, axis=0, keepdims=True)
        vt = jnp.concatenate([vt_ref[cp + kb_i] for kb_i in range(ATT_K // LANES)], axis=1)
        ot = _dot(vt, p.astype(_BF16))
        out_t = jnp.concatenate([ot[0:HEAD_DIM, 0:ATT_Q] * inv[:, 0:ATT_Q],
                                 ot[HEAD_DIM:, ATT_Q:] * inv[:, ATT_Q:]], axis=0)
        o_ref[j * ATT_Q:(j + 1) * ATT_Q, :] = out_t.T.astype(o_ref.dtype)

    scores(0)
    for j in range(tiles_per_step):
        if j + 1 < tiles_per_step:
            scores(j + 1)
        finish(j)


def _toeplitz_bias(rel_bias, q_len, k_len):
    h, table = rel_bias.shape
    n_diag = q_len + k_len - 1
    flat_lo = k_len - 1 - LEFT_ROWS - (CHUNK - 1)
    flat_hi = n_diag - flat_lo - table
    rev = jnp.concatenate([jnp.broadcast_to(rel_bias[:, -1:], (h, flat_hi)), rel_bias[:, ::-1],
                           jnp.broadcast_to(rel_bias[:, :1], (h, flat_lo))], axis=1).astype(_F32)
    flat = jnp.tile(rev, (1, q_len + 1))
    pitch = n_diag - 1
    skew = flat[:, q_len - 1:q_len - 1 + q_len * pitch].reshape(h, q_len, pitch)
    return skew[:, :, :k_len]


def _attn_bias(rel_bias):
    h = rel_bias.shape[0]
    bias = _toeplitz_bias(rel_bias, ATT_Q, ATT_K)
    q = lax.broadcasted_iota(jnp.int32, (ATT_Q, ATT_K), 0)
    k = lax.broadcasted_iota(jnp.int32, (ATT_Q, ATT_K), 1)
    off = k // CHUNK - q // CHUNK
    in_band = (off >= 0) & (off < BAND_CHUNKS)
    first_key = LEFT_ROWS - ATT_Q * jnp.arange(ATT_VARIANTS, dtype=jnp.int32)
    ok = in_band[None] & (k[None] >= first_key[:, None, None])
    full = jnp.where(ok[None], bias[:, None] * LOG2E, NEG_INF)
    full = full.reshape(h // 2, 2, ATT_VARIANTS, ATT_Q, ATT_K)
    return full.transpose(0, 2, 4, 1, 3).reshape(h // 2, ATT_VARIANTS, ATT_K, 2 * ATT_Q)


def _attention(proj3, qn_a, kn_a, rel_bias):
    b, s, _ = proj3.shape
    q_rows = min(s, 1024)
    qn = (jnp.tile(qn_a, 2) * (HEAD_DIM ** -0.5 * LOG2E)).reshape(1, LANES)
    kn = jnp.tile(kn_a, 2).reshape(1, LANES)
    pairs = A_HEADS // 2
    kcol, vcol = A_WIDTH // LANES, 2 * A_WIDTH // LANES
    return pl.pallas_call(
        functools.partial(_attn_kernel, q_rows=q_rows),
        grid=(b, pairs, s // q_rows),
        in_specs=[pl.BlockSpec((None, q_rows, LANES), lambda i, p, j: (i, j, p)),
                  pl.BlockSpec((None, s, LANES), lambda i, p, j: (i, 0, kcol + p)),
                  pl.BlockSpec((None, s, LANES), lambda i, p, j: (i, 0, vcol + p)),
                  pl.BlockSpec((1, LANES), lambda i, p, j: (0, 0)),
                  pl.BlockSpec((1, LANES), lambda i, p, j: (0, 0)),
                  pl.BlockSpec((None, ATT_VARIANTS, ATT_K, 2 * ATT_Q), lambda i, p, j: (p, 0, 0, 0))],
        out_specs=pl.BlockSpec((None, q_rows, LANES), lambda i, p, j: (i, j, p)),
        out_shape=jax.ShapeDtypeStruct((b, s, A_WIDTH), _BF16),
        scratch_shapes=[pltpu.VMEM((s + LEFT_ROWS, LANES), _BF16),
                        pltpu.VMEM(((s + LEFT_ROWS) // LANES, LANES, LANES), _BF16),
                        pltpu.VMEM((2, ATT_K, 2 * ATT_Q), _F32)],
        compiler_params=_cparams(3),
        name="attn_a",
    )(proj3, proj3, proj3, qn, kn, _attn_bias(rel_bias))


def _swap_halves(t):
    first = (_lane(t.shape) % HEAD_DIM) < (HEAD_DIM // 2)
    return jnp.where(first, pltpu.roll(t, LANES - HEAD_DIM // 2, 1), pltpu.roll(t, HEAD_DIM // 2, 1))


def _retention_kernel(q_ref, k_ref, v_ref, gate_ref, cos_ref, sin_ref, decay_ref, zeta_ref, xi_ref,
                      cd_ref, gn_ref, o_ref, state_ref, *, rows):
    @pl.when(pl.program_id(2) == 0)
    def _():
        state_ref[...] = jnp.zeros_like(state_ref)

    c = RET_CHUNK
    low = _lane((c, LANES)) < HEAD_DIM
    eye = jnp.where(lax.broadcasted_iota(jnp.int32, (LANES, LANES), 0) == _lane((LANES, LANES)),
                    1.0, 0.0).astype(_BF16)
    srow = lax.broadcasted_iota(jnp.int32, (LANES, LANES), 0) < HEAD_DIM
    scol = _lane((LANES, LANES)) < HEAD_DIM
    same_head = srow == scol

    for j in range(rows // c):
        sl = slice(j * c, (j + 1) * c)
        cos, sin = cos_ref[sl, :], sin_ref[sl, :]
        q = q_ref[sl, :]
        k = k_ref[sl, :]
        qr = q * cos + _swap_halves(q) * sin
        kr = (k * cos + _swap_halves(k) * sin) * (HEAD_DIM ** -0.5)
        vb = v_ref[sl, :].astype(_BF16)
        qb = qr.astype(_BF16)
        kb = kr.astype(_BF16)
        inner_out = []
        for h in range(2):
            qh = jnp.where(low if h == 0 else ~low, qr, 0.0).astype(_BF16)
            inner = _dot_nt(qh, kb) * decay_ref[h]
            inner_out.append(_dot(inner.astype(_BF16), vb))
        state = state_ref[...]
        cross = _dot(qb, state.astype(_BF16)) * xi_ref[...]
        o = jnp.where(low, inner_out[0], inner_out[1]) + cross
        kz = _dot_nt(eye, (kr * zeta_ref[...]).astype(_BF16)).astype(_BF16)
        state_ref[...] = cd_ref[...] * state + jnp.where(same_head, _dot(kz, vb), 0.0)
        mu = jnp.where(low,
                       jnp.sum(jnp.where(low, o, 0.0), axis=-1, keepdims=True),
                       jnp.sum(jnp.where(low, 0.0, o), axis=-1, keepdims=True)) * (1.0 / HEAD_DIM)
        dlt = o - mu
        d2 = dlt * dlt
        var = jnp.where(low,
                        jnp.sum(jnp.where(low, d2, 0.0), axis=-1, keepdims=True),
                        jnp.sum(jnp.where(low, 0.0, d2), axis=-1, keepdims=True)) * (1.0 / HEAD_DIM)
        y = (dlt * lax.rsqrt(var + EPS)) * gn_ref[...]
        g = gate_ref[sl, :]
        o_ref[sl, :] = ((g * jax.nn.sigmoid(g)) * y).astype(o_ref.dtype)


def _retention_tables():
    c = RET_CHUNK
    log_g = jnp.log(1.0 - jnp.exp2(-5.0 - jnp.arange(B_HEADS, dtype=_F32)))
    idx = jnp.arange(c, dtype=_F32)
    diff = idx[:, None] - idx[None, :]
    decay = jnp.where(diff >= 0, jnp.exp(log_g[:, None, None] * jnp.maximum(diff, 0.0)), 0.0)
    zeta = jnp.exp(log_g[:, None] * (c - 1 - idx))
    xi = jnp.exp(log_g[:, None] * (idx + 1.0))
    cd = jnp.exp(log_g * c)

    def lanes(tab):
        return jnp.repeat(tab.reshape(B_HEADS // 2, 2, c), HEAD_DIM, axis=1).transpose(0, 2, 1)

    cdm = jnp.repeat(cd.reshape(B_HEADS // 2, 2), HEAD_DIM, axis=1)
    cdm = jnp.broadcast_to(cdm[:, :, None], (B_HEADS // 2, LANES, LANES))
    return decay, lanes(zeta), lanes(xi), cdm


def _retention(proj3, cos, sin, ret_gn_g):
    b, s, _ = proj3.shape
    rows = min(s, 2048)
    pairs = B_HEADS // 2
    base = 3 * A_WIDTH // LANES
    decay, zeta, xi, cdm = _retention_tables()
    cos3, sin3 = cos.reshape(b, s, LANES), sin.reshape(b, s, LANES)
    gn = ret_gn_g.reshape(pairs, 1, LANES)

    def col(off):
        return pl.BlockSpec((None, rows, LANES), lambda i, p, j: (i, j, base + off * pairs + p))

    tab = pl.BlockSpec((None, rows, LANES), lambda i, p, j: (i, j, 0))
    return pl.pallas_call(
        functools.partial(_retention_kernel, rows=rows),
        grid=(b, pairs, s // rows),
        in_specs=[col(0), col(1), col(2), col(3), tab, tab,
                  pl.BlockSpec((2, RET_CHUNK, RET_CHUNK), lambda i, p, j: (p, 0, 0)),
                  pl.BlockSpec((None, RET_CHUNK, LANES), lambda i, p, j: (p, 0, 0)),
                  pl.BlockSpec((None, RET_CHUNK, LANES), lambda i, p, j: (p, 0, 0)),
                  pl.BlockSpec((None, LANES, LANES), lambda i, p, j: (p, 0, 0)),
                  pl.BlockSpec((None, 1, LANES), lambda i, p, j: (p, 0, 0))],
        out_specs=pl.BlockSpec((None, rows, LANES), lambda i, p, j: (i, j, p)),
        out_shape=jax.ShapeDtypeStruct((b, s, B_WIDTH), _BF16),
        scratch_shapes=[pltpu.VMEM((LANES, LANES), _F32)],
        compiler_params=_cparams(3),
        name="retention_b",
    )(proj3, proj3, proj3, proj3, cos3, sin3, decay, zeta, xi, cdm, gn)


def _cross_kernel(q_ref, k_ref, vt_ref, qn_ref, o_ref, st_ref, *, rows):
    low = _lane((ATT_Q, LANES)) < HEAD_DIM
    lane_blocks = C_WIDTH // LANES
    tiles = [(j, lb) for j in range(rows // ATT_Q) for lb in range(lane_blocks)]

    def scores(i):
        j, lb = tiles[i]
        sl = slice(lb * LANES, (lb + 1) * LANES)
        qn = _pair_rms(q_ref[j * ATT_Q:(j + 1) * ATT_Q, sl], qn_ref[...])
        q2 = jnp.concatenate([jnp.where(low, qn, 0.0), jnp.where(low, 0.0, qn)], axis=0).astype(_BF16)
        st_ref[i % 2] = _dot_nt(k_ref[:, sl], q2)

    def finish(i):
        j, lb = tiles[i]
        sl = slice(lb * LANES, (lb + 1) * LANES)
        st = st_ref[i % 2]
        p = jnp.exp2(st - jnp.max(st, axis=0, keepdims=True))
        inv = 1.0 / jnp.sum(p, axis=0, keepdims=True)
        ot = _dot(vt_ref[sl, :], p.astype(_BF16))
        out_t = jnp.concatenate([ot[0:HEAD_DIM, 0:ATT_Q] * inv[:, 0:ATT_Q],
                                 ot[HEAD_DIM:, ATT_Q:] * inv[:, ATT_Q:]], axis=0)
        o_ref[j * ATT_Q:(j + 1) * ATT_Q, sl] = out_t.T.astype(o_ref.dtype)

    scores(0)
    for i in range(len(tiles)):
        if i + 1 < len(tiles):
            scores(i + 1)
        finish(i)


def _cross_attention(proj3, kc, vtc, qn_c):
    b, s, _ = proj3.shape
    m = kc.shape[1]
    rows = min(s, 1024)
    qn = (jnp.tile(qn_c, 2) * (HEAD_DIM ** -0.5 * LOG2E)).reshape(1, LANES)
    qcol = (3 * A_WIDTH + 4 * B_WIDTH) // C_WIDTH
    return pl.pallas_call(
        functools.partial(_cross_kernel, rows=rows),
        grid=(b, s // rows),
        in_specs=[pl.BlockSpec((None, rows, C_WIDTH), lambda i, j: (i, j, qcol)),
                  pl.BlockSpec((None, m, C_WIDTH), lambda i, j: (i, 0, 0)),
                  pl.BlockSpec((None, C_WIDTH, m), lambda i, j: (i, 0, 0)),
                  pl.BlockSpec((1, LANES), lambda i, j: (0, 0))],
        out_specs=pl.BlockSpec((None, rows, C_WIDTH), lambda i, j: (i, j, 0)),
        out_shape=jax.ShapeDtypeStruct((b, s, C_WIDTH), _BF16),
        scratch_shapes=[pltpu.VMEM((2, m, 2 * ATT_Q), _F32)],
        compiler_params=_cparams(2),
        name="cross_c",
    )(proj3, kc, vtc, qn)


def _out_router_kernel(x_ref, a_ref, b_ref, c_ref, wo_ref, g_ref, wr_ref, br_ref,
                       h_ref, hn_ref, info_ref, rows_ref, cnt_ref, carry_ref):
    @pl.when(pl.program_id(0) == 0)
    def _():
        carry_ref[...] = jnp.zeros_like(carry_ref)

    tm = x_ref.shape[0]
    h = x_ref[...]
    h = h + _dot(a_ref[...], wo_ref[0:A_WIDTH, :])
    h = h + _dot(b_ref[...], wo_ref[A_WIDTH:A_WIDTH + B_WIDTH, :])
    h = h + _dot(c_ref[...], wo_ref[A_WIDTH + B_WIDTH:, :])
    h_ref[...] = h
    ms = jnp.mean(h * h, axis=-1, keepdims=True)
    hn = (h * lax.rsqrt(ms + EPS)) * g_ref[...]
    _rows_to_tiles(hn_ref, hn)
    logits = _dot_nt(wr_ref[...], hn.astype(_BF16))[0:ROUTE_ROWS, :] + br_ref[:, 0:1]
    row = lax.broadcasted_iota(jnp.int32, (ROUTE_ROWS, tm), 0).astype(_F32)
    big = float(ROUTE_ROWS)

    def first_row(mask):
        return jnp.min(jnp.where(mask, row, big), axis=0, keepdims=True)

    gmask = row < N_GROUPS
    gl = jnp.where(gmask, logits, NEG_INF)
    ge = jnp.exp(gl - jnp.max(gl, axis=0, keepdims=True))
    gp = ge / jnp.sum(ge, axis=0, keepdims=True)
    p_group = jnp.max(gp, axis=0, keepdims=True)
    g_sel = first_row(gmask & (gp == p_group))
    lo = ROUTE_LANE0 + g_sel * EXPERTS_PER_GROUP
    emask = (row >= lo) & (row < lo + EXPERTS_PER_GROUP)
    el = jnp.where(emask, logits, NEG_INF)
    ee = jnp.exp(el - jnp.max(el, axis=0, keepdims=True))
    ep = ee / jnp.sum(ee, axis=0, keepdims=True)
    p1 = jnp.max(ep, axis=0, keepdims=True)
    i1 = first_row(emask & (ep == p1))
    ep2 = jnp.where(emask & (row != i1), ep, -1.0)
    p2 = jnp.max(ep2, axis=0, keepdims=True)
    i2 = first_row(ep2 == p2)
    den = p1 + p2
    w1 = p_group * (p1 / den)
    w2 = p_group * (p2 / den)
    hit1 = row == i1
    hit2 = row == i2
    onehot = jnp.where(hit1 | hit2, 1.0, 0.0)
    r_i = lax.broadcasted_iota(jnp.int32, (tm, tm), 0)
    c_i = lax.broadcasted_iota(jnp.int32, (tm, tm), 1)
    earlier = jnp.where(r_i < c_i, 1.0, 0.0).astype(_BF16)
    before = _dot(onehot.astype(_BF16), earlier) + carry_ref[:, 0:1]
    r1 = jnp.sum(jnp.where(hit1, before, 0.0), axis=0, keepdims=True)
    r2 = jnp.sum(jnp.where(hit2, before, 0.0), axis=0, keepdims=True)
    carry_ref[...] = carry_ref[...] + jnp.sum(onehot, axis=1, keepdims=True)
    cnt_ref[...] = carry_ref[...]
    out_row = lax.broadcasted_iota(jnp.int32, (LANES, tm), 0)
    info = jnp.where(out_row == 0, w1, 0.0)
    info = jnp.where(out_row == 1, w2, info)
    info = jnp.where(out_row == 2, i1 - ROUTE_LANE0, info)
    info = jnp.where(out_row == 3, i2 - ROUTE_LANE0, info)
    info = jnp.where(out_row == 4, r1, info)
    info = jnp.where(out_row == 5, r2, info)
    rows_ref[...] = info[0:SUBLANES, :]
    info_ref[...] = info.T


def _out_router(x2, oa, ob, oc, w_out, ffn_g, w_rg, b_rg, w_re, b_re):
    t, d = x2.shape
    tm = min(t, 512)
    pad = LANES - N_GROUPS - N_EXPERTS
    wr = jnp.concatenate([w_rg, w_re, jnp.zeros((d, pad), _F32)], axis=1).T.astype(_BF16)
    br = jnp.concatenate([b_rg, b_re, jnp.zeros((ROUTE_ROWS - N_GROUPS - N_EXPERTS,), _F32)])
    br = jnp.broadcast_to(br[:, None], (ROUTE_ROWS, LANES))

    def rows(w):
        return pl.BlockSpec((tm, w), lambda i: (i, 0))

    def whole(r, c):
        return pl.BlockSpec((r, c), lambda i: (0, 0))

    return pl.pallas_call(
        _out_router_kernel,
        grid=(t // tm,),
        in_specs=[rows(d), rows(A_WIDTH), rows(B_WIDTH), rows(C_WIDTH), whole(d, d), whole(1, d),
                  whole(LANES, d), whole(ROUTE_ROWS, LANES)],
        out_specs=[rows(d), pl.BlockSpec((tm * SUBLANES, LANES), lambda i: (i, 0)), rows(LANES),
                   pl.BlockSpec((SUBLANES, tm), lambda i: (0, i)), whole(ROUTE_ROWS, LANES)],
        out_shape=[jax.ShapeDtypeStruct((t, d), _F32), jax.ShapeDtypeStruct((t * SUBLANES, LANES), _F32),
                   jax.ShapeDtypeStruct((t, LANES), _F32), jax.ShapeDtypeStruct((SUBLANES, t), _F32),
                   jax.ShapeDtypeStruct((ROUTE_ROWS, LANES), _F32)],
        scratch_shapes=[pltpu.VMEM((ROUTE_ROWS, LANES), _F32)],
        compiler_params=_cparams(1),
        name="out_router",
    )(x2, oa, ob, oc, w_out.astype(_BF16), ffn_g.reshape(1, d), wr, br)


DISPATCH_TOKENS = 1024
COMBINE_TOKENS = 512


ROW_UNROLL = 8


def _tile_rows(row, count=1):
    start = row * SUBLANES
    if not isinstance(start, int):
        start = pl.multiple_of(start, SUBLANES)
    return pl.ds(start, count * SUBLANES)


def _row_copy(src, s_row, dst, d_row, sem):
    return pltpu.make_async_copy(src.at[_tile_rows(s_row)], dst.at[_tile_rows(d_row)], sem)


def _dispatch_kernel(pad_start_ref, pad_len_ref, used_ref, dest_ref, hn_ref, xs_ref, zero_ref, sem,
                     pad_sem):
    n = hn_ref.shape[0] // SUBLANES

    @pl.when(pl.program_id(0) == 0)
    def _():
        zero_ref[...] = jnp.zeros_like(zero_ref)
        n_blocks = xs_ref.shape[0] // (ROW_BLOCK * SUBLANES)

        def block_copy(blk):
            return pltpu.make_async_copy(zero_ref, xs_ref.at[_tile_rows(blk * ROW_BLOCK, ROW_BLOCK)], pad_sem)

        def put_block(blk, carry):
            block_copy(blk).start()
            return carry

        def done_block(blk, carry):
            block_copy(blk).wait()
            return carry

        lax.fori_loop(used_ref[0], n_blocks, put_block, 0)
        lax.fori_loop(used_ref[0], n_blocks, done_block, 0)
        bits = [1 << k for k in reversed(range(ROW_BLOCK.bit_length() - 1))]

        def tail(e, wait):
            row = pad_start_ref[e]
            for bit in bits:
                on = (pad_len_ref[e] & bit) != 0
                copy = pltpu.make_async_copy(zero_ref.at[_tile_rows(0, bit)], xs_ref.at[_tile_rows(row, bit)],
                                             pad_sem)

                @pl.when(on)
                def _():
                    copy.wait() if wait else copy.start()

                row = row + jnp.where(on, bit, 0)

        def put_tail(e, carry):
            tail(e, False)
            return carry

        def done_tail(e, carry):
            tail(e, True)
            return carry

        lax.fori_loop(0, N_EXPERTS, put_tail, 0)
        lax.fori_loop(0, N_EXPERTS, done_tail, 0)

    def issue(i, carry):
        for u in range(ROW_UNROLL):
            t = i * ROW_UNROLL + u
            _row_copy(hn_ref, t, xs_ref, dest_ref[2 * t], sem).start(priority=0)
            _row_copy(hn_ref, t, xs_ref, dest_ref[2 * t + 1], sem).start(priority=1)
        return carry

    lax.fori_loop(0, n // ROW_UNROLL, issue, 0)
    for _ in range(2):
        pltpu.make_async_copy(hn_ref, xs_ref.at[_tile_rows(0, n)], sem).wait()


def _dispatch(hn, dest, pad_start, pad_len, n_used, n_rows):
    t = hn.shape[0] // SUBLANES
    n = min(t, DISPATCH_TOKENS)
    return pl.pallas_call(
        _dispatch_kernel,
        grid_spec=pltpu.PrefetchScalarGridSpec(
            num_scalar_prefetch=3,
            grid=(t // n,),
            in_specs=[pl.BlockSpec((2 * n,), lambda i, *_: (i,), memory_space=pltpu.SMEM),
                      pl.BlockSpec((n * SUBLANES, LANES), lambda i, *_: (i, 0))],
            out_specs=pl.BlockSpec(memory_space=pl.ANY),
            scratch_shapes=[pltpu.VMEM((ROW_BLOCK * SUBLANES, LANES), hn.dtype), pltpu.SemaphoreType.DMA,
                            pltpu.SemaphoreType.DMA]),
        out_shape=jax.ShapeDtypeStruct((n_rows * SUBLANES, LANES), hn.dtype),
        compiler_params=_cparams(1),
        name="moe_dispatch",
    )(pad_start, pad_len, n_used, dest, hn)


def _expert_kernel(be_ref, used_ref, x_ref, wg_ref, wu_ref, wd_ref, y_ref, wg_bf, wu_bf, wd_bf):
    i = pl.program_id(0)
    live = i < used_ref[0]
    new_expert = (i == 0) | (be_ref[i] != be_ref[jnp.maximum(i - 1, 0)])

    @pl.when(live & new_expert)
    def _():
        wg_bf[...] = wg_ref[...].astype(_BF16)
        wu_bf[...] = wu_ref[...].astype(_BF16)
        wd_bf[...] = wd_ref[...].astype(_BF16)

    @pl.when(live)
    def _():
        half = ROW_BLOCK // 2
        gate_up = []
        for hh in range(2):
            x = _tiles_to_rows(x_ref, half, hh * half).astype(_BF16)
            gate_up.append((_dot(x, wg_bf[...]), _dot(x, wu_bf[...])))
        for hh in range(2):
            gate, up = gate_up[hh]
            act = (gate * jax.nn.sigmoid(gate)) * up
            _rows_to_tiles(y_ref, _dot(act.astype(_BF16), wd_bf[...]), hh * half)

    @pl.when(i >= used_ref[0])
    def _():
        y_ref[...] = jnp.zeros_like(y_ref)


def _experts(xs, block_e, n_used, w_gate, w_up, w_down):
    n_rows, d = xs.shape[0] // SUBLANES, D_MODEL
    n_blocks = n_rows // ROW_BLOCK
    tile_block = (ROW_BLOCK * SUBLANES, LANES)

    def xmap(i, be, used):
        return (jnp.minimum(i, used[0] - 1), 0)

    def wmap(i, be, used):
        return (be[jnp.minimum(i, used[0] - 1)], 0, 0)

    return pl.pallas_call(
        _expert_kernel,
        grid_spec=pltpu.PrefetchScalarGridSpec(
            num_scalar_prefetch=2,
            grid=(n_blocks,),
            in_specs=[pl.BlockSpec(tile_block, xmap),
                      pl.BlockSpec((None, d, D_EXPERT), wmap),
                      pl.BlockSpec((None, d, D_EXPERT), wmap),
                      pl.BlockSpec((None, D_EXPERT, d), wmap)],
            out_specs=pl.BlockSpec(tile_block, lambda i, be, used: (i, 0)),
            scratch_shapes=[pltpu.VMEM((d, D_EXPERT), _BF16), pltpu.VMEM((d, D_EXPERT), _BF16),
                            pltpu.VMEM((D_EXPERT, d), _BF16)]),
        out_shape=jax.ShapeDtypeStruct((n_rows * SUBLANES, LANES), _F32),
        compiler_params=_cparams(1),
        name="moe_experts",
    )(block_e, n_used, xs, w_gate, w_up, w_down)


def _combine_kernel(dest_ref, next_ref, h_ref, info_ref, ys_ref, o_ref, buf_ref, sem):
    n = h_ref.shape[0]
    step = pl.program_id(0)
    slot = step % 2

    def gather(idx_ref, to_slot):
        def issue(i, carry):
            for u in range(ROW_UNROLL):
                t = i * ROW_UNROLL + u
                _row_copy(ys_ref, idx_ref[2 * t], buf_ref.at[to_slot, 0], t,
                          sem.at[to_slot]).start(priority=0)
                _row_copy(ys_ref, idx_ref[2 * t + 1], buf_ref.at[to_slot, 1], t,
                          sem.at[to_slot]).start(priority=1)
            return carry

        lax.fori_loop(0, n // ROW_UNROLL, issue, 0)

    @pl.when(step == 0)
    def _():
        gather(dest_ref, 0)

    @pl.when(step + 1 < pl.num_programs(0))
    def _():
        gather(next_ref, 1 - slot)

    for k in range(2):
        pltpu.make_async_copy(ys_ref.at[_tile_rows(0, n)], buf_ref.at[slot, k], sem.at[slot]).wait()
    info = info_ref[...]
    w0 = info[:, 0:1]
    w1 = info[:, 1:2]
    for s in range(SUBLANES):
        sl = slice(s * LANES, (s + 1) * LANES)
        moe = w0 * _tile_block(buf_ref.at[slot, 0], s, n) + w1 * _tile_block(buf_ref.at[slot, 1], s, n)
        o_ref[:, sl] = h_ref[:, sl] + moe


def _combine(h, info, ys, dest):
    t, d = h.shape
    n = min(t, COMBINE_TOKENS)
    steps = t // n
    return pl.pallas_call(
        _combine_kernel,
        grid=(steps,),
        in_specs=[pl.BlockSpec((2 * n,), lambda i: (i,), memory_space=pltpu.SMEM),
                  pl.BlockSpec((2 * n,), lambda i: (jnp.minimum(i + 1, steps - 1),),
                               memory_space=pltpu.SMEM),
                  pl.BlockSpec((n, d), lambda i: (i, 0)),
                  pl.BlockSpec((n, LANES), lambda i: (i, 0)),
                  pl.BlockSpec(memory_space=pl.ANY)],
        out_specs=pl.BlockSpec((n, d), lambda i: (i, 0)),
        out_shape=jax.ShapeDtypeStruct((t, d), _F32),
        scratch_shapes=[pltpu.VMEM((2, 2, n * SUBLANES, LANES), _F32), pltpu.SemaphoreType.DMA((2,))],
        compiler_params=_cparams(1),
        name="moe_combine",
    )(dest, dest, h, info, ys)


def _moe_layout(route_rows, counts, t):
    counts = counts[ROUTE_LANE0:ROUTE_LANE0 + N_EXPERTS, 0].astype(jnp.int32)
    padded = (counts + ROW_BLOCK - 1) // ROW_BLOCK * ROW_BLOCK
    pends = jnp.cumsum(padded)
    pstarts = pends - padded
    eid = route_rows[2:4].astype(jnp.int32)
    rank = route_rows[4:6].astype(jnp.int32)
    experts = jnp.arange(N_EXPERTS, dtype=jnp.int32)
    start_of = jnp.sum(jnp.where(eid[:, :, None] == experts, pstarts, 0), axis=-1)
    dest = (start_of + rank).T.reshape(-1)
    n_blocks = -(-2 * t // ROW_BLOCK) + N_EXPERTS
    first_row = jnp.arange(n_blocks, dtype=jnp.int32) * ROW_BLOCK
    block_e = jnp.minimum(jnp.sum((pends[None, :] <= first_row[:, None]).astype(jnp.int32), axis=1),
                          N_EXPERTS - 1)
    n_used = (pends[-1:] // ROW_BLOCK).astype(jnp.int32)
    return dest, block_e, n_used, pstarts + counts, padded - counts, n_blocks * ROW_BLOCK


def kernel(x, mem, positions, mix_norm_g, w_in, qn_a, kn_a, rel_bias, ret_gn_g, mem_norm_g, w_mem_kv,
           qn_c, kn_c, w_out, ffn_norm_g, w_router_group, b_router_group, w_router_expert,
           b_router_expert, w_gate, w_up, w_down):
    b, s, d = x.shape
    t = b * s
    x2 = x.reshape(t, d)
    cos, sin = _rope_tables(positions)
    kc, vc = _mem_kv(mem, mem_norm_g, w_mem_kv, kn_c)
    proj3 = _in_proj(x2, mix_norm_g, w_in).reshape(b, s, IN_COLS)
    out_a = _attention(proj3, qn_a, kn_a, rel_bias)
    out_b = _retention(proj3, cos, sin, ret_gn_g)
    out_c = _cross_attention(proj3, kc, vc, qn_c)
    h, hn, info, route_rows, counts = _out_router(
        x2, out_a.reshape(t, A_WIDTH), out_b.reshape(t, B_WIDTH), out_c.reshape(t, C_WIDTH),
        w_out, ffn_norm_g, w_router_group, b_router_group, w_router_expert, b_router_expert)
    dest, block_e, n_used, pad_start, pad_len, n_rows = _moe_layout(route_rows, counts, t)
    xs = _dispatch(hn, dest, pad_start, pad_len, n_used, n_rows)
    ys = _experts(xs, block_e, n_used, w_gate, w_up, w_down)
    return _combine(h, info, ys, dest).reshape(b, s, d)
```

```python
import functools

import jax
import jax.numpy as jnp
from jax import lax
from jax.experimental import pallas as pl
from jax.experimental.pallas import tpu as pltpu

D_MODEL = 1024
CHUNK = 64
HEAD_DIM = 64
A_HEADS = 8
B_HEADS = 4
C_HEADS = 4
A_WIDTH = A_HEADS * HEAD_DIM
B_WIDTH = B_HEADS * HEAD_DIM
C_WIDTH = C_HEADS * HEAD_DIM
IN_COLS = 3 * A_WIDTH + 4 * B_WIDTH + C_WIDTH
LEFT_CHUNKS = 8
BAND_CHUNKS = LEFT_CHUNKS + 1
MAX_REL_DIST = 128
ROPE_BASE = 10000.0
N_GROUPS = 4
EXPERTS_PER_GROUP = 8
N_EXPERTS = N_GROUPS * EXPERTS_PER_GROUP
D_EXPERT = D_MODEL // 2
EPS = 1e-6
NEG_INF = -1e30
LOG2E = 1.4426950408889634

LANES = 128
SUBLANES = 8
assert D_MODEL == SUBLANES * LANES
PACK_ROWS = SUBLANES // 2
LEFT_ROWS = LEFT_CHUNKS * CHUNK
ATT_Q = 2 * CHUNK
ATT_K = ATT_Q + LEFT_ROWS
ATT_VARIANTS = LEFT_ROWS // ATT_Q + 1
ATT_AHEAD = 3
RET_CHUNK = 256
ROW_BLOCK = 512
EXPERT_SPLIT = 2
EXPERT_AHEAD = 2
ROUTE_LANE0 = N_GROUPS
ROUTE_ROWS = 64
VMEM_LIMIT = 48 * 1024 * 1024

_F32 = jnp.float32
_BF16 = jnp.bfloat16


def _cparams(n_axes):
    return pltpu.CompilerParams(dimension_semantics=("arbitrary",) * n_axes,
                                vmem_limit_bytes=VMEM_LIMIT)


def _dot(a, b):
    return jnp.dot(a, b, preferred_element_type=_F32)


def _dot_nt(a, b):
    return lax.dot_general(a, b, (((1,), (1,)), ((), ())), preferred_element_type=_F32)


def _lane(shape):
    return lax.broadcasted_iota(jnp.int32, shape, len(shape) - 1)


def _pair_rms(t, gain):
    low = _lane(t.shape) < HEAD_DIM
    t2 = t * t
    ms0 = jnp.sum(jnp.where(low, t2, 0.0), axis=-1, keepdims=True) * (1.0 / HEAD_DIM)
    ms1 = jnp.sum(jnp.where(low, 0.0, t2), axis=-1, keepdims=True) * (1.0 / HEAD_DIM)
    r = jnp.where(low, lax.rsqrt(ms0 + EPS), lax.rsqrt(ms1 + EPS))
    return (t * r) * gain


def _rows_to_tiles(ref, val, row0=0):
    n = val.shape[0]
    for s in range(SUBLANES):
        ref[pl.ds(row0 * SUBLANES + s, n, stride=SUBLANES), :] = val[:, s * LANES:(s + 1) * LANES]


def _tile_block(ref, s, n, row0=0):
    return ref[pl.ds(row0 * SUBLANES + s, n, stride=SUBLANES), :]


def _tiles_to_rows(ref, n, row0=0):
    return jnp.concatenate([_tile_block(ref, s, n, row0) for s in range(SUBLANES)], axis=-1)


def _pack_rows(ref, val):
    n = val.shape[0]
    for s in range(PACK_ROWS):
        lo = val[:, (2 * s) * LANES:(2 * s + 1) * LANES].astype(_BF16).astype(_F32)
        hi = val[:, (2 * s + 1) * LANES:(2 * s + 2) * LANES].astype(_BF16).astype(_F32)
        word = (lax.bitcast_convert_type(lo, jnp.uint32) >> 16) | (
            lax.bitcast_convert_type(hi, jnp.uint32) & jnp.uint32(0xFFFF0000))
        ref[pl.ds(s, n, stride=PACK_ROWS), :] = word


def _unpack_rows(ref, n, row0=0):
    parts = []
    for s in range(PACK_ROWS):
        word = ref[pl.ds(row0 * PACK_ROWS + s, n, stride=PACK_ROWS), :]
        parts.append(lax.bitcast_convert_type(word << 16, _F32))
        parts.append(lax.bitcast_convert_type(word & jnp.uint32(0xFFFF0000), _F32))
    return jnp.concatenate(parts, axis=-1).astype(_BF16)


ROPE_HALF = HEAD_DIM // 2
ROPE_PACK = LANES // ROPE_HALF


def _rope_kernel(pos_ref, inv_ref, cos_ref, sin_ref):
    ang = pos_ref[...].astype(_F32) * inv_ref[...]
    rows = ang.shape[0]
    lane = _lane(ang.shape)
    sign = jnp.where((lane % HEAD_DIM) < ROPE_HALF, -1.0, 1.0)
    for out_ref, val in ((cos_ref, jnp.cos(ang)), (sin_ref, jnp.sin(ang))):
        for j in range(ROPE_PACK):
            seg = jnp.where(lane // ROPE_HALF == j, val, 0.0)
            full = seg
            for k in range(1, ROPE_PACK):
                full = full + pltpu.roll(seg, k * ROPE_HALF, 1)
            if out_ref is sin_ref:
                full = full * sign
            out_ref[pl.ds(j, rows, stride=ROPE_PACK), :] = full


def _rope_tables(positions):
    t = positions.size
    inv = ROPE_BASE ** (-jnp.arange(ROPE_HALF, dtype=_F32) / ROPE_HALF)
    inv128 = jnp.tile(inv, ROPE_PACK).reshape(1, LANES)
    rows = t // ROPE_PACK
    pos = jnp.repeat(positions.reshape(rows, ROPE_PACK), ROPE_HALF, axis=1)
    tm = min(rows, 512)
    out = pl.BlockSpec((tm * ROPE_PACK, LANES), lambda i: (i, 0))
    return pl.pallas_call(
        _rope_kernel,
        grid=(rows // tm,),
        in_specs=[pl.BlockSpec((tm, LANES), lambda i: (i, 0)), pl.BlockSpec((1, LANES), lambda i: (0, 0))],
        out_specs=[out, out],
        out_shape=[jax.ShapeDtypeStruct((t, LANES), _F32)] * 2,
        compiler_params=_cparams(1),
        name="rope_tables",
    )(pos, inv128)


def _mem_kv_kernel(mem_ref, g_ref, w_ref, kn_ref, k_ref, v_ref):
    m = mem_ref[...]
    ms = jnp.mean(m * m, axis=-1, keepdims=True)
    mn = (m * lax.rsqrt(ms + EPS)) * g_ref[...]
    kv = _dot(mn.astype(_BF16), w_ref[...])
    for j in range(C_WIDTH // LANES):
        sl = slice(j * LANES, (j + 1) * LANES)
        k_ref[:, sl] = _pair_rms(kv[:, sl], kn_ref[...]).astype(_BF16)
    v_ref[...] = kv[:, C_WIDTH:].T.astype(_BF16)


def _mem_kv(mem, mem_norm_g, w_mem_kv, kn_c):
    b, m, d = mem.shape
    kn = jnp.tile(kn_c, 2).reshape(1, LANES)
    return pl.pallas_call(
        _mem_kv_kernel,
        grid=(b,),
        in_specs=[pl.BlockSpec((None, m, d), lambda i: (i, 0, 0)),
                  pl.BlockSpec((1, d), lambda i: (0, 0)),
                  pl.BlockSpec((d, 2 * C_WIDTH), lambda i: (0, 0)),
                  pl.BlockSpec((1, LANES), lambda i: (0, 0))],
        out_specs=[pl.BlockSpec((None, m, C_WIDTH), lambda i: (i, 0, 0)),
                   pl.BlockSpec((None, C_WIDTH, m), lambda i: (i, 0, 0))],
        out_shape=[jax.ShapeDtypeStruct((b, m, C_WIDTH), _BF16),
                   jax.ShapeDtypeStruct((b, C_WIDTH, m), _BF16)],
        compiler_params=_cparams(1),
        name="mem_kv",
    )(mem, mem_norm_g.reshape(1, d), w_mem_kv.astype(_BF16), kn)


def _in_proj_kernel(x_ref, g_ref, w_ref, o_ref):
    x = x_ref[...]
    ms = jnp.mean(x * x, axis=-1, keepdims=True)
    xn = ((x * lax.rsqrt(ms + EPS)) * g_ref[...]).astype(_BF16)
    o_ref[...] = _dot(xn, w_ref[...])


def _in_proj(x2, g, w_in):
    t, d = x2.shape
    tm = min(t, 512)
    return pl.pallas_call(
        _in_proj_kernel,
        grid=(t // tm,),
        in_specs=[pl.BlockSpec((tm, d), lambda i: (i, 0)),
                  pl.BlockSpec((1, d), lambda i: (0, 0)),
                  pl.BlockSpec((d, IN_COLS), lambda i: (0, 0))],
        out_specs=pl.BlockSpec((tm, IN_COLS), lambda i: (i, 0)),
        out_shape=jax.ShapeDtypeStruct((t, IN_COLS), _F32),
        compiler_params=_cparams(1),
        name="in_proj",
    )(x2, g.reshape(1, d), w_in.astype(_BF16))


def _attn_kernel(q_ref, k_ref, v_ref, qn_ref, kn_ref, bias_ref, o_ref, kp_ref, vt_ref, st_ref, *, q_rows):
    qs = pl.program_id(2)
    s = k_ref.shape[0]
    fill_rows = min(s, 512)
    left_blocks = LEFT_ROWS // LANES

    @pl.when(qs == 0)
    def _():
        kp_ref[0:LEFT_ROWS, :] = jnp.zeros((LEFT_ROWS, LANES), _BF16)
        for blk in range(left_blocks):
            vt_ref[blk] = jnp.zeros((LANES, LANES), _BF16)
        eye = jnp.where(lax.broadcasted_iota(jnp.int32, (LANES, LANES), 0) == _lane((LANES, LANES)),
                        1.0, 0.0).astype(_BF16)

        def fill(i, carry):
            r = pl.multiple_of(i * fill_rows, fill_rows)
            kp_ref[pl.ds(LEFT_ROWS + r, fill_rows), :] = _pair_rms(
                k_ref[pl.ds(r, fill_rows), :], kn_ref[...]).astype(_BF16)
            vt = _dot_nt(eye, v_ref[pl.ds(r, fill_rows), :].astype(_BF16))
            for j in range(fill_rows // LANES):
                vt_ref[left_blocks + i * (fill_rows // LANES) + j] = vt[:, j * LANES:(j + 1) * LANES].astype(_BF16)
            return carry

        lax.fori_loop(0, s // fill_rows, fill, 0)

    low = _lane((ATT_Q, LANES)) < HEAD_DIM
    tiles_per_step = q_rows // ATT_Q

    def scores(j):
        cp = qs * tiles_per_step + j
        qn = _pair_rms(q_ref[j * ATT_Q:(j + 1) * ATT_Q, :], qn_ref[...])
        q2 = jnp.concatenate([jnp.where(low, qn, 0.0), jnp.where(low, 0.0, qn)], axis=0).astype(_BF16)
        kb = kp_ref[pl.ds(pl.multiple_of(cp * ATT_Q, ATT_Q), ATT_K), :]
        st_ref[j % (ATT_AHEAD + 1)] = _dot_nt(kb, q2) + bias_ref[jnp.minimum(cp, ATT_VARIANTS - 1)]

    def finish(j):
        cp = qs * tiles_per_step + j
        st = st_ref[j % (ATT_AHEAD + 1)]
        m = jnp.max(st, axis=0, keepdims=True)
        p = jnp.exp2(st - m)
        inv = 1.0 / jnp.sum(p, axis=0, keepdims=True)
        vt = jnp.concatenate([vt_ref[cp + kb_i] for kb_i in range(ATT_K // LANES)], axis=1)
        ot = _dot(vt, p.astype(_BF16))
        out_t = jnp.concatenate([ot[0:HEAD_DIM, 0:ATT_Q] * inv[:, 0:ATT_Q],
                                 ot[HEAD_DIM:, ATT_Q:] * inv[:, ATT_Q:]], axis=0)
        o_ref[j * ATT_Q:(j + 1) * ATT_Q, :] = out_t.T.astype(o_ref.dtype)

    for j in range(min(ATT_AHEAD, tiles_per_step)):
        scores(j)
    for j in range(tiles_per_step):
        if j + ATT_AHEAD < tiles_per_step:
            scores(j + ATT_AHEAD)
        finish(j)


def _toeplitz_bias(rel_bias, q_len, k_len):
    h, table = rel_bias.shape
    n_diag = q_len + k_len - 1
    flat_lo = k_len - 1 - LEFT_ROWS - (CHUNK - 1)
    flat_hi = n_diag - flat_lo - table
    rev = jnp.concatenate([jnp.broadcast_to(rel_bias[:, -1:], (h, flat_hi)), rel_bias[:, ::-1],
                           jnp.broadcast_to(rel_bias[:, :1], (h, flat_lo))], axis=1).astype(_F32)
    flat = jnp.tile(rev, (1, q_len + 1))
    pitch = n_diag - 1
    skew = flat[:, q_len - 1:q_len - 1 + q_len * pitch].reshape(h, q_len, pitch)
    return skew[:, :, :k_len]


def _attn_bias(rel_bias):
    h = rel_bias.shape[0]
    bias = _toeplitz_bias(rel_bias, ATT_Q, ATT_K)
    q = lax.broadcasted_iota(jnp.int32, (ATT_Q, ATT_K), 0)
    k = lax.broadcasted_iota(jnp.int32, (ATT_Q, ATT_K), 1)
    off = k // CHUNK - q // CHUNK
    in_band = (off >= 0) & (off < BAND_CHUNKS)
    first_key = LEFT_ROWS - ATT_Q * jnp.arange(ATT_VARIANTS, dtype=jnp.int32)
    ok = in_band[None] & (k[None] >= first_key[:, None, None])
    full = jnp.where(ok[None], bias[:, None] * LOG2E, NEG_INF)
    full = full.reshape(h // 2, 2, ATT_VARIANTS, ATT_Q, ATT_K)
    return full.transpose(0, 2, 4, 1, 3).reshape(h // 2, ATT_VARIANTS, ATT_K, 2 * ATT_Q)


def _attention(proj3, qn_a, kn_a, rel_bias):
    b, s, _ = proj3.shape
    q_rows = min(s, 1024)
    qn = (jnp.tile(qn_a, 2) * (HEAD_DIM ** -0.5 * LOG2E)).reshape(1, LANES)
    kn = jnp.tile(kn_a, 2).reshape(1, LANES)
    pairs = A_HEADS // 2
    kcol, vcol = A_WIDTH // LANES, 2 * A_WIDTH // LANES
    return pl.pallas_call(
        functools.partial(_attn_kernel, q_rows=q_rows),
        grid=(b, pairs, s // q_rows),
        in_specs=[pl.BlockSpec((None, q_rows, LANES), lambda i, p, j: (i, j, p)),
                  pl.BlockSpec((None, s, LANES), lambda i, p, j: (i, 0, kcol + p)),
                  pl.BlockSpec((None, s, LANES), lambda i, p, j: (i, 0, vcol + p)),
                  pl.BlockSpec((1, LANES), lambda i, p, j: (0, 0)),
                  pl.BlockSpec((1, LANES), lambda i, p, j: (0, 0)),
                  pl.BlockSpec((None, ATT_VARIANTS, ATT_K, 2 * ATT_Q), lambda i, p, j: (p, 0, 0, 0))],
        out_specs=pl.BlockSpec((None, q_rows, LANES), lambda i, p, j: (i, j, p)),
        out_shape=jax.ShapeDtypeStruct((b, s, A_WIDTH), _BF16),
        scratch_shapes=[pltpu.VMEM((s + LEFT_ROWS, LANES), _BF16),
                        pltpu.VMEM(((s + LEFT_ROWS) // LANES, LANES, LANES), _BF16),
                        pltpu.VMEM((ATT_AHEAD + 1, ATT_K, 2 * ATT_Q), _F32)],
        compiler_params=_cparams(3),
        name="attn_a",
    )(proj3, proj3, proj3, qn, kn, _attn_bias(rel_bias))


def _swap_halves(t):
    first = (_lane(t.shape) % HEAD_DIM) < (HEAD_DIM // 2)
    return jnp.where(first, pltpu.roll(t, LANES - HEAD_DIM // 2, 1), pltpu.roll(t, HEAD_DIM // 2, 1))


def _retention_kernel(q_ref, k_ref, v_ref, gate_ref, cos_ref, sin_ref, decay_ref, zeta_ref, xi_ref,
                      cd_ref, gn_ref, o_ref, state_ref, *, rows):
    @pl.when(pl.program_id(2) == 0)
    def _():
        state_ref[...] = jnp.zeros_like(state_ref)

    c = RET_CHUNK
    low = _lane((c, LANES)) < HEAD_DIM
    eye = jnp.where(lax.broadcasted_iota(jnp.int32, (LANES, LANES), 0) == _lane((LANES, LANES)),
                    1.0, 0.0).astype(_BF16)
    srow = lax.broadcasted_iota(jnp.int32, (LANES, LANES), 0) < HEAD_DIM
    scol = _lane((LANES, LANES)) < HEAD_DIM
    same_head = srow == scol

    for j in range(rows // c):
        sl = slice(j * c, (j + 1) * c)
        cos, sin = cos_ref[sl, :], sin_ref[sl, :]
        q = q_ref[sl, :]
        k = k_ref[sl, :]
        qr = q * cos + _swap_halves(q) * sin
        kr = (k * cos + _swap_halves(k) * sin) * (HEAD_DIM ** -0.5)
        vb = v_ref[sl, :].astype(_BF16)
        qb = qr.astype(_BF16)
        kb = kr.astype(_BF16)
        inner_out = []
        for h in range(2):
            qh = jnp.where(low if h == 0 else ~low, qr, 0.0).astype(_BF16)
            inner = _dot_nt(qh, kb) * decay_ref[h]
            inner_out.append(_dot(inner.astype(_BF16), vb))
        state = state_ref[...]
        cross = _dot(qb, state.astype(_BF16)) * xi_ref[...]
        o = jnp.where(low, inner_out[0], inner_out[1]) + cross
        kz = _dot_nt(eye, (kr * zeta_ref[...]).astype(_BF16)).astype(_BF16)
        state_ref[...] = cd_ref[...] * state + jnp.where(same_head, _dot(kz, vb), 0.0)
        mu = jnp.where(low,
                       jnp.sum(jnp.where(low, o, 0.0), axis=-1, keepdims=True),
                       jnp.sum(jnp.where(low, 0.0, o), axis=-1, keepdims=True)) * (1.0 / HEAD_DIM)
        dlt = o - mu
        d2 = dlt * dlt
        var = jnp.where(low,
                        jnp.sum(jnp.where(low, d2, 0.0), axis=-1, keepdims=True),
                        jnp.sum(jnp.where(low, 0.0, d2), axis=-1, keepdims=True)) * (1.0 / HEAD_DIM)
        y = (dlt * lax.rsqrt(var + EPS)) * gn_ref[...]
        g = gate_ref[sl, :]
        o_ref[sl, :] = ((g * jax.nn.sigmoid(g)) * y).astype(o_ref.dtype)


def _retention_tables():
    c = RET_CHUNK
    log_g = jnp.log(1.0 - jnp.exp2(-5.0 - jnp.arange(B_HEADS, dtype=_F32)))
    idx = jnp.arange(c, dtype=_F32)
    diff = idx[:, None] - idx[None, :]
    decay = jnp.where(diff >= 0, jnp.exp(log_g[:, None, None] * jnp.maximum(diff, 0.0)), 0.0)
    zeta = jnp.exp(log_g[:, None] * (c - 1 - idx))
    xi = jnp.exp(log_g[:, None] * (idx + 1.0))
    cd = jnp.exp(log_g * c)

    def lanes(tab):
        return jnp.repeat(tab.reshape(B_HEADS // 2, 2, c), HEAD_DIM, axis=1).transpose(0, 2, 1)

    cdm = jnp.repeat(cd.reshape(B_HEADS // 2, 2), HEAD_DIM, axis=1)
    cdm = jnp.broadcast_to(cdm[:, :, None], (B_HEADS // 2, LANES, LANES))
    return decay, lanes(zeta), lanes(xi), cdm


def _retention(proj3, cos, sin, ret_gn_g):
    b, s, _ = proj3.shape
    rows = min(s, 2048)
    pairs = B_HEADS // 2
    base = 3 * A_WIDTH // LANES
    decay, zeta, xi, cdm = _retention_tables()
    cos3, sin3 = cos.reshape(b, s, LANES), sin.reshape(b, s, LANES)
    gn = ret_gn_g.reshape(pairs, 1, LANES)

    def col(off):
        return pl.BlockSpec((None, rows, LANES), lambda i, p, j: (i, j, base + off * pairs + p))

    tab = pl.BlockSpec((None, rows, LANES), lambda i, p, j: (i, j, 0))
    return pl.pallas_call(
        functools.partial(_retention_kernel, rows=rows),
        grid=(b, pairs, s // rows),
        in_specs=[col(0), col(1), col(2), col(3), tab, tab,
                  pl.BlockSpec((2, RET_CHUNK, RET_CHUNK), lambda i, p, j: (p, 0, 0)),
                  pl.BlockSpec((None, RET_CHUNK, LANES), lambda i, p, j: (p, 0, 0)),
                  pl.BlockSpec((None, RET_CHUNK, LANES), lambda i, p, j: (p, 0, 0)),
                  pl.BlockSpec((None, LANES, LANES), lambda i, p, j: (p, 0, 0)),
                  pl.BlockSpec((None, 1, LANES), lambda i, p, j: (p, 0, 0))],
        out_specs=pl.BlockSpec((None, rows, LANES), lambda i, p, j: (i, j, p)),
        out_shape=jax.ShapeDtypeStruct((b, s, B_WIDTH), _BF16),
        scratch_shapes=[pltpu.VMEM((LANES, LANES), _F32)],
        compiler_params=_cparams(3),
        name="retention_b",
    )(proj3, proj3, proj3, proj3, cos3, sin3, decay, zeta, xi, cdm, gn)


def _cross_kernel(q_ref, k_ref, vt_ref, qn_ref, o_ref, st_ref, *, rows):
    low = _lane((ATT_Q, LANES)) < HEAD_DIM
    lane_blocks = C_WIDTH // LANES
    tiles = [(j, lb) for j in range(rows // ATT_Q) for lb in range(lane_blocks)]

    def scores(i):
        j, lb = tiles[i]
        sl = slice(lb * LANES, (lb + 1) * LANES)
        qn = _pair_rms(q_ref[j * ATT_Q:(j + 1) * ATT_Q, sl], qn_ref[...])
        q2 = jnp.concatenate([jnp.where(low, qn, 0.0), jnp.where(low, 0.0, qn)], axis=0).astype(_BF16)
        st_ref[i % (ATT_AHEAD + 1)] = _dot_nt(k_ref[:, sl], q2)

    def finish(i):
        j, lb = tiles[i]
        sl = slice(lb * LANES, (lb + 1) * LANES)
        st = st_ref[i % (ATT_AHEAD + 1)]
        p = jnp.exp2(st - jnp.max(st, axis=0, keepdims=True))
        inv = 1.0 / jnp.sum(p, axis=0, keepdims=True)
        ot = _dot(vt_ref[sl, :], p.astype(_BF16))
        out_t = jnp.concatenate([ot[0:HEAD_DIM, 0:ATT_Q] * inv[:, 0:ATT_Q],
                                 ot[HEAD_DIM:, ATT_Q:] * inv[:, ATT_Q:]], axis=0)
        o_ref[j * ATT_Q:(j + 1) * ATT_Q, sl] = out_t.T.astype(o_ref.dtype)

    for i in range(min(ATT_AHEAD, len(tiles))):
        scores(i)
    for i in range(len(tiles)):
        if i + ATT_AHEAD < len(tiles):
            scores(i + ATT_AHEAD)
        finish(i)


def _cross_attention(proj3, kc, vtc, qn_c):
    b, s, _ = proj3.shape
    m = kc.shape[1]
    rows = min(s, 1024)
    qn = (jnp.tile(qn_c, 2) * (HEAD_DIM ** -0.5 * LOG2E)).reshape(1, LANES)
    qcol = (3 * A_WIDTH + 4 * B_WIDTH) // C_WIDTH
    return pl.pallas_call(
        functools.partial(_cross_kernel, rows=rows),
        grid=(b, s // rows),
        in_specs=[pl.BlockSpec((None, rows, C_WIDTH), lambda i, j: (i, j, qcol)),
                  pl.BlockSpec((None, m, C_WIDTH), lambda i, j: (i, 0, 0)),
                  pl.BlockSpec((None, C_WIDTH, m), lambda i, j: (i, 0, 0)),
                  pl.BlockSpec((1, LANES), lambda i, j: (0, 0))],
        out_specs=pl.BlockSpec((None, rows, C_WIDTH), lambda i, j: (i, j, 0)),
        out_shape=jax.ShapeDtypeStruct((b, s, C_WIDTH), _BF16),
        scratch_shapes=[pltpu.VMEM((ATT_AHEAD + 1, m, 2 * ATT_Q), _F32)],
        compiler_params=_cparams(2),
        name="cross_c",
    )(proj3, kc, vtc, qn)


def _out_router_kernel(x_ref, a_ref, b_ref, c_ref, wo_ref, g_ref, wr_ref, br_ref,
                       h_ref, hn_ref, info_ref, rows_ref, cnt_ref, carry_ref):
    @pl.when(pl.program_id(0) == 0)
    def _():
        carry_ref[...] = jnp.zeros_like(carry_ref)

    tm = x_ref.shape[0]
    h = x_ref[...]
    h = h + _dot(a_ref[...], wo_ref[0:A_WIDTH, :])
    h = h + _dot(b_ref[...], wo_ref[A_WIDTH:A_WIDTH + B_WIDTH, :])
    h = h + _dot(c_ref[...], wo_ref[A_WIDTH + B_WIDTH:, :])
    h_ref[...] = h
    ms = jnp.mean(h * h, axis=-1, keepdims=True)
    hn = (h * lax.rsqrt(ms + EPS)) * g_ref[...]
    _pack_rows(hn_ref, hn)
    logits = _dot_nt(wr_ref[...], hn.astype(_BF16))[0:ROUTE_ROWS, :] + br_ref[:, 0:1]
    row = lax.broadcasted_iota(jnp.int32, (ROUTE_ROWS, tm), 0).astype(_F32)
    big = float(ROUTE_ROWS)

    def first_row(mask):
        return jnp.min(jnp.where(mask, row, big), axis=0, keepdims=True)

    gmask = row < N_GROUPS
    gl = jnp.where(gmask, logits, NEG_INF)
    ge = jnp.exp(gl - jnp.max(gl, axis=0, keepdims=True))
    gp = ge / jnp.sum(ge, axis=0, keepdims=True)
    p_group = jnp.max(gp, axis=0, keepdims=True)
    g_sel = first_row(gmask & (gp == p_group))
    lo = ROUTE_LANE0 + g_sel * EXPERTS_PER_GROUP
    emask = (row >= lo) & (row < lo + EXPERTS_PER_GROUP)
    el = jnp.where(emask, logits, NEG_INF)
    ee = jnp.exp(el - jnp.max(el, axis=0, keepdims=True))
    ep = ee / jnp.sum(ee, axis=0, keepdims=True)
    p1 = jnp.max(ep, axis=0, keepdims=True)
    i1 = first_row(emask & (ep == p1))
    ep2 = jnp.where(emask & (row != i1), ep, -1.0)
    p2 = jnp.max(ep2, axis=0, keepdims=True)
    i2 = first_row(ep2 == p2)
    den = p1 + p2
    w1 = p_group * (p1 / den)
    w2 = p_group * (p2 / den)
    hit1 = row == i1
    hit2 = row == i2
    onehot = jnp.where(hit1 | hit2, 1.0, 0.0)
    r_i = lax.broadcasted_iota(jnp.int32, (tm, tm), 0)
    c_i = lax.broadcasted_iota(jnp.int32, (tm, tm), 1)
    earlier = jnp.where(r_i < c_i, 1.0, 0.0).astype(_BF16)
    before = _dot(onehot.astype(_BF16), earlier) + carry_ref[:, 0:1]
    r1 = jnp.sum(jnp.where(hit1, before, 0.0), axis=0, keepdims=True)
    r2 = jnp.sum(jnp.where(hit2, before, 0.0), axis=0, keepdims=True)
    carry_ref[...] = carry_ref[...] + jnp.sum(onehot, axis=1, keepdims=True)
    cnt_ref[...] = carry_ref[...]
    out_row = lax.broadcasted_iota(jnp.int32, (LANES, tm), 0)
    info = jnp.where(out_row == 0, w1, 0.0)
    info = jnp.where(out_row == 1, w2, info)
    info = jnp.where(out_row == 2, i1 - ROUTE_LANE0, info)
    info = jnp.where(out_row == 3, i2 - ROUTE_LANE0, info)
    info = jnp.where(out_row == 4, r1, info)
    info = jnp.where(out_row == 5, r2, info)
    rows_ref[...] = info[0:SUBLANES, :]
    info_ref[...] = info.T


def _out_router(x2, oa, ob, oc, w_out, ffn_g, w_rg, b_rg, w_re, b_re):
    t, d = x2.shape
    tm = min(t, 512)
    pad = LANES - N_GROUPS - N_EXPERTS
    wr = jnp.concatenate([w_rg, w_re, jnp.zeros((d, pad), _F32)], axis=1).T.astype(_BF16)
    br = jnp.concatenate([b_rg, b_re, jnp.zeros((ROUTE_ROWS - N_GROUPS - N_EXPERTS,), _F32)])
    br = jnp.broadcast_to(br[:, None], (ROUTE_ROWS, LANES))

    def rows(w):
        return pl.BlockSpec((tm, w), lambda i: (i, 0))

    def whole(r, c):
        return pl.BlockSpec((r, c), lambda i: (0, 0))

    return pl.pallas_call(
        _out_router_kernel,
        grid=(t // tm,),
        in_specs=[rows(d), rows(A_WIDTH), rows(B_WIDTH), rows(C_WIDTH), whole(d, d), whole(1, d),
                  whole(LANES, d), whole(ROUTE_ROWS, LANES)],
        out_specs=[rows(d), pl.BlockSpec((tm * PACK_ROWS, LANES), lambda i: (i, 0)), rows(LANES),
                   pl.BlockSpec((SUBLANES, tm), lambda i: (0, i)), whole(ROUTE_ROWS, LANES)],
        out_shape=[jax.ShapeDtypeStruct((t, d), _F32), jax.ShapeDtypeStruct((t * PACK_ROWS, LANES), jnp.uint32),
                   jax.ShapeDtypeStruct((t, LANES), _F32), jax.ShapeDtypeStruct((SUBLANES, t), _F32),
                   jax.ShapeDtypeStruct((ROUTE_ROWS, LANES), _F32)],
        scratch_shapes=[pltpu.VMEM((ROUTE_ROWS, LANES), _F32)],
        compiler_params=_cparams(1),
        name="out_router",
    )(x2, oa, ob, oc, w_out.astype(_BF16), ffn_g.reshape(1, d), wr, br)


DISPATCH_TOKENS = 1024
COMBINE_TOKENS = 512


ROW_UNROLL = 8


def _tile_rows(row, count=1, per=SUBLANES):
    start = row * per
    if not isinstance(start, int):
        start = pl.multiple_of(start, per)
    return pl.ds(start, count * per)


def _row_copy(src, s_row, dst, d_row, sem, per=SUBLANES):
    return pltpu.make_async_copy(src.at[_tile_rows(s_row, 1, per)], dst.at[_tile_rows(d_row, 1, per)], sem)


def _dispatch_kernel(pad_start_ref, pad_len_ref, used_ref, dest_ref, hn_ref, xs_ref, zero_ref, sem,
                     pad_sem):
    per = PACK_ROWS
    n = hn_ref.shape[0] // per

    @pl.when(pl.program_id(0) == 0)
    def _():
        zero_ref[...] = jnp.zeros_like(zero_ref)
        n_blocks = xs_ref.shape[0] // (ROW_BLOCK * per)

        def block_copy(blk):
            return pltpu.make_async_copy(zero_ref, xs_ref.at[_tile_rows(blk * ROW_BLOCK, ROW_BLOCK, per)],
                                         pad_sem)

        def put_block(blk, carry):
            block_copy(blk).start()
            return carry

        def done_block(blk, carry):
            block_copy(blk).wait()
            return carry

        lax.fori_loop(used_ref[0], n_blocks, put_block, 0)
        lax.fori_loop(used_ref[0], n_blocks, done_block, 0)
        bits = [1 << k for k in reversed(range(ROW_BLOCK.bit_length() - 1))]

        def tail(e, wait):
            row = pad_start_ref[e]
            for bit in bits:
                on = (pad_len_ref[e] & bit) != 0
                copy = pltpu.make_async_copy(zero_ref.at[_tile_rows(0, bit, per)],
                                             xs_ref.at[_tile_rows(row, bit, per)], pad_sem)

                @pl.when(on)
                def _():
                    copy.wait() if wait else copy.start()

                row = row + jnp.where(on, bit, 0)

        def put_tail(e, carry):
            tail(e, False)
            return carry

        def done_tail(e, carry):
            tail(e, True)
            return carry

        lax.fori_loop(0, N_EXPERTS, put_tail, 0)
        lax.fori_loop(0, N_EXPERTS, done_tail, 0)

    def issue(i, carry):
        for u in range(ROW_UNROLL):
            t = i * ROW_UNROLL + u
            _row_copy(hn_ref, t, xs_ref, dest_ref[2 * t], sem, per).start(priority=0)
            _row_copy(hn_ref, t, xs_ref, dest_ref[2 * t + 1], sem, per).start(priority=1)
        return carry

    lax.fori_loop(0, n // ROW_UNROLL, issue, 0)
    for _ in range(2):
        pltpu.make_async_copy(hn_ref, xs_ref.at[_tile_rows(0, n, per)], sem).wait()


def _dispatch(hn, dest, pad_start, pad_len, n_used, n_rows):
    t = hn.shape[0] // PACK_ROWS
    n = min(t, DISPATCH_TOKENS)
    return pl.pallas_call(
        _dispatch_kernel,
        grid_spec=pltpu.PrefetchScalarGridSpec(
            num_scalar_prefetch=3,
            grid=(t // n,),
            in_specs=[pl.BlockSpec((2 * n,), lambda i, *_: (i,), memory_space=pltpu.SMEM),
                      pl.BlockSpec((n * PACK_ROWS, LANES), lambda i, *_: (i, 0))],
            out_specs=pl.BlockSpec(memory_space=pl.ANY),
            scratch_shapes=[pltpu.VMEM((ROW_BLOCK * PACK_ROWS, LANES), hn.dtype), pltpu.SemaphoreType.DMA,
                            pltpu.SemaphoreType.DMA]),
        out_shape=jax.ShapeDtypeStruct((n_rows * PACK_ROWS, LANES), hn.dtype),
        compiler_params=_cparams(1),
        name="moe_dispatch",
    )(pad_start, pad_len, n_used, dest, hn)


def _expert_kernel(be_ref, used_ref, x_ref, wg_ref, wu_ref, wd_ref, y_ref, wg_bf, wu_bf, wd_bf):
    i = pl.program_id(0)
    live = i < used_ref[0]
    new_expert = (i == 0) | (be_ref[i] != be_ref[jnp.maximum(i - 1, 0)])

    @pl.when(live & new_expert)
    def _():
        wg_bf[...] = wg_ref[...].astype(_BF16)
        wu_bf[...] = wu_ref[...].astype(_BF16)
        wd_bf[...] = wd_ref[...].astype(_BF16)

    @pl.when(live)
    def _():
        sub = ROW_BLOCK // EXPERT_SPLIT
        gate_up = {}

        def first(k):
            x = _unpack_rows(x_ref, sub, k * sub)
            gate_up[k] = (_dot(x, wg_bf[...]), _dot(x, wu_bf[...]))

        def second(k):
            gate, up = gate_up.pop(k)
            act = (gate * jax.nn.sigmoid(gate)) * up
            _rows_to_tiles(y_ref, _dot(act.astype(_BF16), wd_bf[...]), k * sub)

        for k in range(min(EXPERT_AHEAD, EXPERT_SPLIT)):
            first(k)
        for k in range(EXPERT_SPLIT):
            if k + EXPERT_AHEAD < EXPERT_SPLIT:
                first(k + EXPERT_AHEAD)
            second(k)

    @pl.when(i >= used_ref[0])
    def _():
        y_ref[...] = jnp.zeros_like(y_ref)


def _experts(xs, block_e, n_used, w_gate, w_up, w_down):
    n_rows, d = xs.shape[0] // PACK_ROWS, D_MODEL
    n_blocks = n_rows // ROW_BLOCK
    tile_block = (ROW_BLOCK * SUBLANES, LANES)

    def xmap(i, be, used):
        return (jnp.minimum(i, used[0] - 1), 0)

    def wmap(i, be, used):
        return (be[jnp.minimum(i, used[0] - 1)], 0, 0)

    return pl.pallas_call(
        _expert_kernel,
        grid_spec=pltpu.PrefetchScalarGridSpec(
            num_scalar_prefetch=2,
            grid=(n_blocks,),
            in_specs=[pl.BlockSpec((ROW_BLOCK * PACK_ROWS, LANES), xmap),
                      pl.BlockSpec((None, d, D_EXPERT), wmap),
                      pl.BlockSpec((None, d, D_EXPERT), wmap),
                      pl.BlockSpec((None, D_EXPERT, d), wmap)],
            out_specs=pl.BlockSpec(tile_block, lambda i, be, used: (i, 0)),
            scratch_shapes=[pltpu.VMEM((d, D_EXPERT), _BF16), pltpu.VMEM((d, D_EXPERT), _BF16),
                            pltpu.VMEM((D_EXPERT, d), _BF16)]),
        out_shape=jax.ShapeDtypeStruct((n_rows * SUBLANES, LANES), _F32),
        compiler_params=_cparams(1),
        name="moe_experts",
    )(block_e, n_used, xs, w_gate, w_up, w_down)


def _combine_kernel(dest_ref, next_ref, h_ref, info_ref, ys_ref, o_ref, buf_ref, sem):
    n = h_ref.shape[0]
    step = pl.program_id(0)
    slot = step % 2

    def gather(idx_ref, to_slot):
        def issue(i, carry):
            for u in range(ROW_UNROLL):
                t = i * ROW_UNROLL + u
                _row_copy(ys_ref, idx_ref[2 * t], buf_ref.at[to_slot, 0], t,
                          sem.at[to_slot]).start(priority=0)
                _row_copy(ys_ref, idx_ref[2 * t + 1], buf_ref.at[to_slot, 1], t,
                          sem.at[to_slot]).start(priority=1)
            return carry

        lax.fori_loop(0, n // ROW_UNROLL, issue, 0)

    @pl.when(step == 0)
    def _():
        gather(dest_ref, 0)

    @pl.when(step + 1 < pl.num_programs(0))
    def _():
        gather(next_ref, 1 - slot)

    for k in range(2):
        pltpu.make_async_copy(ys_ref.at[_tile_rows(0, n)], buf_ref.at[slot, k], sem.at[slot]).wait()
    info = info_ref[...]
    w0 = info[:, 0:1]
    w1 = info[:, 1:2]
    for s in range(SUBLANES):
        sl = slice(s * LANES, (s + 1) * LANES)
        moe = w0 * _tile_block(buf_ref.at[slot, 0], s, n) + w1 * _tile_block(buf_ref.at[slot, 1], s, n)
        o_ref[:, sl] = h_ref[:, sl] + moe


def _combine(h, info, ys, dest):
    t, d = h.shape
    n = min(t, COMBINE_TOKENS)
    steps = t // n
    return pl.pallas_call(
        _combine_kernel,
        grid=(steps,),
        in_specs=[pl.BlockSpec((2 * n,), lambda i: (i,), memory_space=pltpu.SMEM),
                  pl.BlockSpec((2 * n,), lambda i: (jnp.minimum(i + 1, steps - 1),),
                               memory_space=pltpu.SMEM),
                  pl.BlockSpec((n, d), lambda i: (i, 0)),
                  pl.BlockSpec((n, LANES), lambda i: (i, 0)),
                  pl.BlockSpec(memory_space=pl.ANY)],
        out_specs=pl.BlockSpec((n, d), lambda i: (i, 0)),
        out_shape=jax.ShapeDtypeStruct((t, d), _F32),
        scratch_shapes=[pltpu.VMEM((2, 2, n * SUBLANES, LANES), _F32), pltpu.SemaphoreType.DMA((2,))],
        compiler_params=_cparams(1),
        name="moe_combine",
    )(dest, dest, h, info, ys)


def _moe_layout(route_rows, counts, t):
    counts = counts[ROUTE_LANE0:ROUTE_LANE0 + N_EXPERTS, 0].astype(jnp.int32)
    padded = (counts + ROW_BLOCK - 1) // ROW_BLOCK * ROW_BLOCK
    pends = jnp.cumsum(padded)
    pstarts = pends - padded
    eid = route_rows[2:4].astype(jnp.int32)
    rank = route_rows[4:6].astype(jnp.int32)
    experts = jnp.arange(N_EXPERTS, dtype=jnp.int32)
    start_of = jnp.sum(jnp.where(eid[:, :, None] == experts, pstarts, 0), axis=-1)
    dest = (start_of + rank).T.reshape(-1)
    n_blocks = -(-2 * t // ROW_BLOCK) + N_EXPERTS
    first_row = jnp.arange(n_blocks, dtype=jnp.int32) * ROW_BLOCK
    block_e = jnp.minimum(jnp.sum((pends[None, :] <= first_row[:, None]).astype(jnp.int32), axis=1),
                          N_EXPERTS - 1)
    n_used = (pends[-1:] // ROW_BLOCK).astype(jnp.int32)
    return dest, block_e, n_used, pstarts + counts, padded - counts, n_blocks * ROW_BLOCK


def kernel(x, mem, positions, mix_norm_g, w_in, qn_a, kn_a, rel_bias, ret_gn_g, mem_norm_g, w_mem_kv,
           qn_c, kn_c, w_out, ffn_norm_g, w_router_group, b_router_group, w_router_expert,
           b_router_expert, w_gate, w_up, w_down):
    b, s, d = x.shape
    t = b * s
    x2 = x.reshape(t, d)
    cos, sin = _rope_tables(positions)
    kc, vc = _mem_kv(mem, mem_norm_g, w_mem_kv, kn_c)
    proj3 = _in_proj(x2, mix_norm_g, w_in).reshape(b, s, IN_COLS)
    out_a = _attention(proj3, qn_a, kn_a, rel_bias)
    out_b = _retention(proj3, cos, sin, ret_gn_g)
    out_c = _cross_attention(proj3, kc, vc, qn_c)
    h, hn, info, route_rows, counts = _out_router(
        x2, out_a.reshape(t, A_WIDTH), out_b.reshape(t, B_WIDTH), out_c.reshape(t, C_WIDTH),
        w_out, ffn_norm_g, w_router_group, b_router_group, w_router_expert, b_router_expert)
    dest, block_e, n_used, pad_start, pad_len, n_rows = _moe_layout(route_rows, counts, t)
    xs = _dispatch(hn, dest, pad_start, pad_len, n_used, n_rows)
    ys = _experts(xs, block_e, n_used, w_gate, w_up, w_down)
    return _combine(h, info, ys, dest).reshape(b, s, d)
```

```python
import functools

import jax
import jax.numpy as jnp
from jax import lax
from jax.experimental import pallas as pl
from jax.experimental.pallas import tpu as pltpu

D_MODEL = 1024
CHUNK = 64
HEAD_DIM = 64
A_HEADS = 8
B_HEADS = 4
C_HEADS = 4
A_WIDTH = A_HEADS * HEAD_DIM
B_WIDTH = B_HEADS * HEAD_DIM
C_WIDTH = C_HEADS * HEAD_DIM
IN_COLS = 3 * A_WIDTH + 4 * B_WIDTH + C_WIDTH
LEFT_CHUNKS = 8
BAND_CHUNKS = LEFT_CHUNKS + 1
MAX_REL_DIST = 128
ROPE_BASE = 10000.0
N_GROUPS = 4
EXPERTS_PER_GROUP = 8
N_EXPERTS = N_GROUPS * EXPERTS_PER_GROUP
D_EXPERT = D_MODEL // 2
EPS = 1e-6
NEG_INF = -1e30
LOG2E = 1.4426950408889634

LANES = 128
SUBLANES = 8
assert D_MODEL == SUBLANES * LANES
PACK_ROWS = SUBLANES // 2
LEFT_ROWS = LEFT_CHUNKS * CHUNK
ATT_Q = 2 * CHUNK
ATT_K = ATT_Q + LEFT_ROWS
ATT_VARIANTS = LEFT_ROWS // ATT_Q + 1
ONES_ROWS = 16
ATT_AHEAD = 3
RET_CHUNK = 256
ROW_BLOCK = 512
EXPERT_SPLIT = 2
EXPERT_AHEAD = 2
ROUTE_LANE0 = N_GROUPS
ROUTE_ROWS = 64
VMEM_LIMIT = 48 * 1024 * 1024

_F32 = jnp.float32
_BF16 = jnp.bfloat16


def _cparams(n_axes):
    return pltpu.CompilerParams(dimension_semantics=("arbitrary",) * n_axes,
                                vmem_limit_bytes=VMEM_LIMIT)


def _dot(a, b):
    return jnp.dot(a, b, preferred_element_type=_F32)


def _dot_nt(a, b):
    return lax.dot_general(a, b, (((1,), (1,)), ((), ())), preferred_element_type=_F32)


def _lane(shape):
    return lax.broadcasted_iota(jnp.int32, shape, len(shape) - 1)


def _pair_rms(t, gain):
    low = _lane(t.shape) < HEAD_DIM
    t2 = t * t
    ms0 = jnp.sum(jnp.where(low, t2, 0.0), axis=-1, keepdims=True) * (1.0 / HEAD_DIM)
    ms1 = jnp.sum(jnp.where(low, 0.0, t2), axis=-1, keepdims=True) * (1.0 / HEAD_DIM)
    r = jnp.where(low, lax.rsqrt(ms0 + EPS), lax.rsqrt(ms1 + EPS))
    return (t * r) * gain


def _rows_to_tiles(ref, val, row0=0):
    n = val.shape[0]
    for s in range(SUBLANES):
        ref[pl.ds(row0 * SUBLANES + s, n, stride=SUBLANES), :] = val[:, s * LANES:(s + 1) * LANES]


def _tile_block(ref, s, n, row0=0):
    return ref[pl.ds(row0 * SUBLANES + s, n, stride=SUBLANES), :]


def _tiles_to_rows(ref, n, row0=0):
    return jnp.concatenate([_tile_block(ref, s, n, row0) for s in range(SUBLANES)], axis=-1)


def _pack_rows(ref, val):
    n = val.shape[0]
    for s in range(PACK_ROWS):
        lo = val[:, (2 * s) * LANES:(2 * s + 1) * LANES].astype(_BF16).astype(_F32)
        hi = val[:, (2 * s + 1) * LANES:(2 * s + 2) * LANES].astype(_BF16).astype(_F32)
        word = (lax.bitcast_convert_type(lo, jnp.uint32) >> 16) | (
            lax.bitcast_convert_type(hi, jnp.uint32) & jnp.uint32(0xFFFF0000))
        ref[pl.ds(s, n, stride=PACK_ROWS), :] = word


def _unpack_rows(ref, n, row0=0):
    parts = []
    for s in range(PACK_ROWS):
        word = ref[pl.ds(row0 * PACK_ROWS + s, n, stride=PACK_ROWS), :]
        parts.append(lax.bitcast_convert_type(word << 16, _F32))
        parts.append(lax.bitcast_convert_type(word & jnp.uint32(0xFFFF0000), _F32))
    return jnp.concatenate(parts, axis=-1).astype(_BF16)


ROPE_HALF = HEAD_DIM // 2
ROPE_PACK = LANES // ROPE_HALF


def _rope_kernel(pos_ref, inv_ref, cos_ref, sin_ref):
    ang = pos_ref[...].astype(_F32) * inv_ref[...]
    rows = ang.shape[0]
    lane = _lane(ang.shape)
    sign = jnp.where((lane % HEAD_DIM) < ROPE_HALF, -1.0, 1.0)
    for out_ref, val in ((cos_ref, jnp.cos(ang)), (sin_ref, jnp.sin(ang))):
        for j in range(ROPE_PACK):
            seg = jnp.where(lane // ROPE_HALF == j, val, 0.0)
            full = seg
            for k in range(1, ROPE_PACK):
                full = full + pltpu.roll(seg, k * ROPE_HALF, 1)
            if out_ref is sin_ref:
                full = full * sign
            out_ref[pl.ds(j, rows, stride=ROPE_PACK), :] = full


def _rope_tables(positions):
    t = positions.size
    inv = ROPE_BASE ** (-jnp.arange(ROPE_HALF, dtype=_F32) / ROPE_HALF)
    inv128 = jnp.tile(inv, ROPE_PACK).reshape(1, LANES)
    rows = t // ROPE_PACK
    pos = jnp.repeat(positions.reshape(rows, ROPE_PACK), ROPE_HALF, axis=1)
    tm = min(rows, 512)
    out = pl.BlockSpec((tm * ROPE_PACK, LANES), lambda i: (i, 0))
    return pl.pallas_call(
        _rope_kernel,
        grid=(rows // tm,),
        in_specs=[pl.BlockSpec((tm, LANES), lambda i: (i, 0)), pl.BlockSpec((1, LANES), lambda i: (0, 0))],
        out_specs=[out, out],
        out_shape=[jax.ShapeDtypeStruct((t, LANES), _F32)] * 2,
        compiler_params=_cparams(1),
        name="rope_tables",
    )(pos, inv128)


def _mem_kv_kernel(mem_ref, g_ref, w_ref, kn_ref, k_ref, v_ref):
    m = mem_ref[...]
    ms = jnp.mean(m * m, axis=-1, keepdims=True)
    mn = (m * lax.rsqrt(ms + EPS)) * g_ref[...]
    kv = _dot(mn.astype(_BF16), w_ref[...])
    for j in range(C_WIDTH // LANES):
        sl = slice(j * LANES, (j + 1) * LANES)
        k_ref[:, sl] = _pair_rms(kv[:, sl], kn_ref[...]).astype(_BF16)
    v_ref[...] = kv[:, C_WIDTH:].T.astype(_BF16)


def _mem_kv(mem, mem_norm_g, w_mem_kv, kn_c):
    b, m, d = mem.shape
    kn = jnp.tile(kn_c, 2).reshape(1, LANES)
    return pl.pallas_call(
        _mem_kv_kernel,
        grid=(b,),
        in_specs=[pl.BlockSpec((None, m, d), lambda i: (i, 0, 0)),
                  pl.BlockSpec((1, d), lambda i: (0, 0)),
                  pl.BlockSpec((d, 2 * C_WIDTH), lambda i: (0, 0)),
                  pl.BlockSpec((1, LANES), lambda i: (0, 0))],
        out_specs=[pl.BlockSpec((None, m, C_WIDTH), lambda i: (i, 0, 0)),
                   pl.BlockSpec((None, C_WIDTH, m), lambda i: (i, 0, 0))],
        out_shape=[jax.ShapeDtypeStruct((b, m, C_WIDTH), _BF16),
                   jax.ShapeDtypeStruct((b, C_WIDTH, m), _BF16)],
        compiler_params=_cparams(1),
        name="mem_kv",
    )(mem, mem_norm_g.reshape(1, d), w_mem_kv.astype(_BF16), kn)


def _in_proj_kernel(x_ref, g_ref, w_ref, o_ref):
    x = x_ref[...]
    ms = jnp.mean(x * x, axis=-1, keepdims=True)
    xn = ((x * lax.rsqrt(ms + EPS)) * g_ref[...]).astype(_BF16)
    o_ref[...] = _dot(xn, w_ref[...])


def _in_proj(x2, g, w_in):
    t, d = x2.shape
    tm = min(t, 512)
    return pl.pallas_call(
        _in_proj_kernel,
        grid=(t // tm,),
        in_specs=[pl.BlockSpec((tm, d), lambda i: (i, 0)),
                  pl.BlockSpec((1, d), lambda i: (0, 0)),
                  pl.BlockSpec((d, IN_COLS), lambda i: (0, 0))],
        out_specs=pl.BlockSpec((tm, IN_COLS), lambda i: (i, 0)),
        out_shape=jax.ShapeDtypeStruct((t, IN_COLS), _F32),
        compiler_params=_cparams(1),
        name="in_proj",
    )(x2, g.reshape(1, d), w_in.astype(_BF16))


def _attn_kernel(q_ref, k_ref, v_ref, qn_ref, kn_ref, bias_ref, o_ref, kp_ref, vt_ref, st_ref, var_ref, *,
                 q_rows):
    qs = pl.program_id(2)
    s = k_ref.shape[0]
    fill_rows = min(s, 512)
    left_blocks = LEFT_ROWS // LANES

    @pl.when(qs == 0)
    def _():
        kp_ref[0:LEFT_ROWS, :] = jnp.zeros((LEFT_ROWS, LANES), _BF16)
        for blk in range(left_blocks):
            vt_ref[blk] = jnp.zeros((LANES, LANES), _BF16)
        eye = jnp.where(lax.broadcasted_iota(jnp.int32, (LANES, LANES), 0) == _lane((LANES, LANES)),
                        1.0, 0.0).astype(_BF16)

        def fill(i, carry):
            r = pl.multiple_of(i * fill_rows, fill_rows)
            kp_ref[pl.ds(LEFT_ROWS + r, fill_rows), :] = _pair_rms(
                k_ref[pl.ds(r, fill_rows), :], kn_ref[...]).astype(_BF16)
            vt = _dot_nt(eye, v_ref[pl.ds(r, fill_rows), :].astype(_BF16))
            for j in range(fill_rows // LANES):
                vt_ref[left_blocks + i * (fill_rows // LANES) + j] = vt[:, j * LANES:(j + 1) * LANES].astype(_BF16)
            return carry

        lax.fori_loop(0, s // fill_rows, fill, 0)
        key = lax.broadcasted_iota(jnp.int32, (ATT_K, 2 * ATT_Q), 0)
        for v in range(ATT_VARIANTS):
            var_ref[v] = jnp.where(key >= LEFT_ROWS - ATT_Q * v, bias_ref[...], NEG_INF)

    low = _lane((ATT_Q, LANES)) < HEAD_DIM
    ones = jnp.ones((ONES_ROWS, ATT_K), _BF16)
    tiles_per_step = q_rows // ATT_Q

    def scores(j):
        cp = qs * tiles_per_step + j
        qn = _pair_rms(q_ref[j * ATT_Q:(j + 1) * ATT_Q, :], qn_ref[...])
        q2 = jnp.concatenate([jnp.where(low, qn, 0.0), jnp.where(low, 0.0, qn)], axis=0).astype(_BF16)
        kb = kp_ref[pl.ds(pl.multiple_of(cp * ATT_Q, ATT_Q), ATT_K), :]
        st_ref[j % (ATT_AHEAD + 1)] = _dot_nt(kb, q2) + var_ref[jnp.minimum(cp, ATT_VARIANTS - 1)]

    def finish(j):
        cp = qs * tiles_per_step + j
        st = st_ref[j % (ATT_AHEAD + 1)]
        m = jnp.max(st, axis=0, keepdims=True)
        p = jnp.exp2(st - m)
        vt = jnp.concatenate([vt_ref[cp + kb_i] for kb_i in range(ATT_K // LANES)] , axis=1)
        ot = _dot(jnp.concatenate([vt, ones], axis=0), p.astype(_BF16))
        inv = 1.0 / ot[LANES:LANES + 1, :]
        out_t = jnp.concatenate([ot[0:HEAD_DIM, 0:ATT_Q] * inv[:, 0:ATT_Q],
                                 ot[HEAD_DIM:LANES, ATT_Q:] * inv[:, ATT_Q:]], axis=0)
        o_ref[j * ATT_Q:(j + 1) * ATT_Q, :] = out_t.T.astype(o_ref.dtype)

    for j in range(min(ATT_AHEAD, tiles_per_step)):
        scores(j)
    for j in range(tiles_per_step):
        if j + ATT_AHEAD < tiles_per_step:
            scores(j + ATT_AHEAD)
        finish(j)


def _toeplitz_bias(rel_bias, q_len, k_len):
    h, table = rel_bias.shape
    n_diag = q_len + k_len - 1
    flat_lo = k_len - 1 - LEFT_ROWS - (CHUNK - 1)
    flat_hi = n_diag - flat_lo - table
    rev = jnp.concatenate([jnp.broadcast_to(rel_bias[:, -1:], (h, flat_hi)), rel_bias[:, ::-1],
                           jnp.broadcast_to(rel_bias[:, :1], (h, flat_lo))], axis=1).astype(_F32)
    flat = jnp.tile(rev, (1, q_len + 1))
    pitch = n_diag - 1
    skew = flat[:, q_len - 1:q_len - 1 + q_len * pitch].reshape(h, q_len, pitch)
    return skew[:, :, :k_len]


def _attn_bias(rel_bias):
    h = rel_bias.shape[0]
    bias = _toeplitz_bias(rel_bias, ATT_Q, ATT_K)
    q = lax.broadcasted_iota(jnp.int32, (ATT_Q, ATT_K), 0)
    k = lax.broadcasted_iota(jnp.int32, (ATT_Q, ATT_K), 1)
    off = k // CHUNK - q // CHUNK
    in_band = (off >= 0) & (off < BAND_CHUNKS)
    full = jnp.where(in_band[None], bias * LOG2E, NEG_INF)
    full = full.reshape(h // 2, 2, ATT_Q, ATT_K)
    return full.transpose(0, 3, 1, 2).reshape(h // 2, ATT_K, 2 * ATT_Q)


def _attention(proj3, qn_a, kn_a, rel_bias):
    b, s, _ = proj3.shape
    q_rows = min(s, 2048)
    qn = (jnp.tile(qn_a, 2) * (HEAD_DIM ** -0.5 * LOG2E)).reshape(1, LANES)
    kn = jnp.tile(kn_a, 2).reshape(1, LANES)
    pairs = A_HEADS // 2
    kcol, vcol = A_WIDTH // LANES, 2 * A_WIDTH // LANES
    return pl.pallas_call(
        functools.partial(_attn_kernel, q_rows=q_rows),
        grid=(b, pairs, s // q_rows),
        in_specs=[pl.BlockSpec((None, q_rows, LANES), lambda i, p, j: (i, j, p)),
                  pl.BlockSpec((None, s, LANES), lambda i, p, j: (i, 0, kcol + p)),
                  pl.BlockSpec((None, s, LANES), lambda i, p, j: (i, 0, vcol + p)),
                  pl.BlockSpec((1, LANES), lambda i, p, j: (0, 0)),
                  pl.BlockSpec((1, LANES), lambda i, p, j: (0, 0)),
                  pl.BlockSpec((None, ATT_K, 2 * ATT_Q), lambda i, p, j: (p, 0, 0))],
        out_specs=pl.BlockSpec((None, q_rows, LANES), lambda i, p, j: (i, j, p)),
        out_shape=jax.ShapeDtypeStruct((b, s, A_WIDTH), _BF16),
        scratch_shapes=[pltpu.VMEM((s + LEFT_ROWS, LANES), _BF16),
                        pltpu.VMEM(((s + LEFT_ROWS) // LANES, LANES, LANES), _BF16),
                        pltpu.VMEM((ATT_AHEAD + 1, ATT_K, 2 * ATT_Q), _F32),
                        pltpu.VMEM((ATT_VARIANTS, ATT_K, 2 * ATT_Q), _F32)],
        compiler_params=_cparams(3),
        name="attn_a",
    )(proj3, proj3, proj3, qn, kn, _attn_bias(rel_bias))


def _swap_halves(t):
    first = (_lane(t.shape) % HEAD_DIM) < (HEAD_DIM // 2)
    return jnp.where(first, pltpu.roll(t, LANES - HEAD_DIM // 2, 1), pltpu.roll(t, HEAD_DIM // 2, 1))


def _retention_kernel(q_ref, k_ref, v_ref, gate_ref, cos_ref, sin_ref, decay_ref, zeta_ref, xi_ref,
                      cd_ref, gn_ref, o_ref, state_ref, *, rows):
    @pl.when(pl.program_id(2) == 0)
    def _():
        state_ref[...] = jnp.zeros_like(state_ref)

    c = RET_CHUNK
    low = _lane((c, LANES)) < HEAD_DIM
    eye = jnp.where(lax.broadcasted_iota(jnp.int32, (LANES, LANES), 0) == _lane((LANES, LANES)),
                    1.0, 0.0).astype(_BF16)
    srow = lax.broadcasted_iota(jnp.int32, (LANES, LANES), 0) < HEAD_DIM
    scol = _lane((LANES, LANES)) < HEAD_DIM
    same_head = srow == scol

    for j in range(rows // c):
        sl = slice(j * c, (j + 1) * c)
        cos, sin = cos_ref[sl, :], sin_ref[sl, :]
        q = q_ref[sl, :]
        k = k_ref[sl, :]
        qr = q * cos + _swap_halves(q) * sin
        kr = (k * cos + _swap_halves(k) * sin) * (HEAD_DIM ** -0.5)
        vb = v_ref[sl, :].astype(_BF16)
        qb = qr.astype(_BF16)
        kb = kr.astype(_BF16)
        inner_out = []
        for h in range(2):
            qh = jnp.where(low if h == 0 else ~low, qr, 0.0).astype(_BF16)
            inner = _dot_nt(qh, kb) * decay_ref[h]
            inner_out.append(_dot(inner.astype(_BF16), vb))
        state = state_ref[...]
        cross = _dot(qb, state.astype(_BF16)) * xi_ref[...]
        o = jnp.where(low, inner_out[0], inner_out[1]) + cross
        kz = _dot_nt(eye, (kr * zeta_ref[...]).astype(_BF16)).astype(_BF16)
        state_ref[...] = cd_ref[...] * state + jnp.where(same_head, _dot(kz, vb), 0.0)
        mu = jnp.where(low,
                       jnp.sum(jnp.where(low, o, 0.0), axis=-1, keepdims=True),
                       jnp.sum(jnp.where(low, 0.0, o), axis=-1, keepdims=True)) * (1.0 / HEAD_DIM)
        dlt = o - mu
        d2 = dlt * dlt
        var = jnp.where(low,
                        jnp.sum(jnp.where(low, d2, 0.0), axis=-1, keepdims=True),
                        jnp.sum(jnp.where(low, 0.0, d2), axis=-1, keepdims=True)) * (1.0 / HEAD_DIM)
        y = (dlt * lax.rsqrt(var + EPS)) * gn_ref[...]
        g = gate_ref[sl, :]
        o_ref[sl, :] = ((g * jax.nn.sigmoid(g)) * y).astype(o_ref.dtype)


def _retention_tables():
    c = RET_CHUNK
    log_g = jnp.log(1.0 - jnp.exp2(-5.0 - jnp.arange(B_HEADS, dtype=_F32)))
    idx = jnp.arange(c, dtype=_F32)
    diff = idx[:, None] - idx[None, :]
    decay = jnp.where(diff >= 0, jnp.exp(log_g[:, None, None] * jnp.maximum(diff, 0.0)), 0.0)
    zeta = jnp.exp(log_g[:, None] * (c - 1 - idx))
    xi = jnp.exp(log_g[:, None] * (idx + 1.0))
    cd = jnp.exp(log_g * c)

    def lanes(tab):
        return jnp.repeat(tab.reshape(B_HEADS // 2, 2, c), HEAD_DIM, axis=1).transpose(0, 2, 1)

    cdm = jnp.repeat(cd.reshape(B_HEADS // 2, 2), HEAD_DIM, axis=1)
    cdm = jnp.broadcast_to(cdm[:, :, None], (B_HEADS // 2, LANES, LANES))
    return decay, lanes(zeta), lanes(xi), cdm


def _retention(proj3, cos, sin, ret_gn_g):
    b, s, _ = proj3.shape
    rows = min(s, 2048)
    pairs = B_HEADS // 2
    base = 3 * A_WIDTH // LANES
    decay, zeta, xi, cdm = _retention_tables()
    cos3, sin3 = cos.reshape(b, s, LANES), sin.reshape(b, s, LANES)
    gn = ret_gn_g.reshape(pairs, 1, LANES)

    def col(off):
        return pl.BlockSpec((None, rows, LANES), lambda i, p, j: (i, j, base + off * pairs + p))

    tab = pl.BlockSpec((None, rows, LANES), lambda i, p, j: (i, j, 0))
    return pl.pallas_call(
        functools.partial(_retention_kernel, rows=rows),
        grid=(b, pairs, s // rows),
        in_specs=[col(0), col(1), col(2), col(3), tab, tab,
                  pl.BlockSpec((2, RET_CHUNK, RET_CHUNK), lambda i, p, j: (p, 0, 0)),
                  pl.BlockSpec((None, RET_CHUNK, LANES), lambda i, p, j: (p, 0, 0)),
                  pl.BlockSpec((None, RET_CHUNK, LANES), lambda i, p, j: (p, 0, 0)),
                  pl.BlockSpec((None, LANES, LANES), lambda i, p, j: (p, 0, 0)),
                  pl.BlockSpec((None, 1, LANES), lambda i, p, j: (p, 0, 0))],
        out_specs=pl.BlockSpec((None, rows, LANES), lambda i, p, j: (i, j, p)),
        out_shape=jax.ShapeDtypeStruct((b, s, B_WIDTH), _BF16),
        scratch_shapes=[pltpu.VMEM((LANES, LANES), _F32)],
        compiler_params=_cparams(3),
        name="retention_b",
    )(proj3, proj3, proj3, proj3, cos3, sin3, decay, zeta, xi, cdm, gn)


def _cross_kernel(q_ref, k_ref, vt_ref, qn_ref, o_ref, st_ref, *, rows):
    low = _lane((ATT_Q, LANES)) < HEAD_DIM
    lane_blocks = C_WIDTH // LANES
    tiles = [(j, lb) for j in range(rows // ATT_Q) for lb in range(lane_blocks)]
    ones = jnp.ones((ONES_ROWS, vt_ref.shape[1]), _BF16)

    def scores(i):
        j, lb = tiles[i]
        sl = slice(lb * LANES, (lb + 1) * LANES)
        qn = _pair_rms(q_ref[j * ATT_Q:(j + 1) * ATT_Q, sl], qn_ref[...])
        q2 = jnp.concatenate([jnp.where(low, qn, 0.0), jnp.where(low, 0.0, qn)], axis=0).astype(_BF16)
        st_ref[i % (ATT_AHEAD + 1)] = _dot_nt(k_ref[:, sl], q2)

    def finish(i):
        j, lb = tiles[i]
        sl = slice(lb * LANES, (lb + 1) * LANES)
        st = st_ref[i % (ATT_AHEAD + 1)]
        p = jnp.exp2(st - jnp.max(st, axis=0, keepdims=True))
        ot = _dot(jnp.concatenate([vt_ref[sl, :], ones], axis=0), p.astype(_BF16))
        inv = 1.0 / ot[LANES:LANES + 1, :]
        out_t = jnp.concatenate([ot[0:HEAD_DIM, 0:ATT_Q] * inv[:, 0:ATT_Q],
                                 ot[HEAD_DIM:LANES, ATT_Q:] * inv[:, ATT_Q:]], axis=0)
        o_ref[j * ATT_Q:(j + 1) * ATT_Q, sl] = out_t.T.astype(o_ref.dtype)

    for i in range(min(ATT_AHEAD, len(tiles))):
        scores(i)
    for i in range(len(tiles)):
        if i + ATT_AHEAD < len(tiles):
            scores(i + ATT_AHEAD)
        finish(i)


def _cross_attention(proj3, kc, vtc, qn_c):
    b, s, _ = proj3.shape
    m = kc.shape[1]
    rows = min(s, 1024)
    qn = (jnp.tile(qn_c, 2) * (HEAD_DIM ** -0.5 * LOG2E)).reshape(1, LANES)
    qcol = (3 * A_WIDTH + 4 * B_WIDTH) // C_WIDTH
    return pl.pallas_call(
        functools.partial(_cross_kernel, rows=rows),
        grid=(b, s // rows),
        in_specs=[pl.BlockSpec((None, rows, C_WIDTH), lambda i, j: (i, j, qcol)),
                  pl.BlockSpec((None, m, C_WIDTH), lambda i, j: (i, 0, 0)),
                  pl.BlockSpec((None, C_WIDTH, m), lambda i, j: (i, 0, 0)),
                  pl.BlockSpec((1, LANES), lambda i, j: (0, 0))],
        out_specs=pl.BlockSpec((None, rows, C_WIDTH), lambda i, j: (i, j, 0)),
        out_shape=jax.ShapeDtypeStruct((b, s, C_WIDTH), _BF16),
        scratch_shapes=[pltpu.VMEM((ATT_AHEAD + 1, m, 2 * ATT_Q), _F32)],
        compiler_params=_cparams(2),
        name="cross_c",
    )(proj3, kc, vtc, qn)


def _out_router_kernel(x_ref, a_ref, b_ref, c_ref, wo_ref, g_ref, wr_ref, br_ref,
                       h_ref, hn_ref, info_ref, rows_ref, cnt_ref, carry_ref):
    @pl.when(pl.program_id(0) == 0)
    def _():
        carry_ref[...] = jnp.zeros_like(carry_ref)

    tm = x_ref.shape[0]
    h = x_ref[...]
    h = h + _dot(a_ref[...], wo_ref[0:A_WIDTH, :])
    h = h + _dot(b_ref[...], wo_ref[A_WIDTH:A_WIDTH + B_WIDTH, :])
    h = h + _dot(c_ref[...], wo_ref[A_WIDTH + B_WIDTH:, :])
    h_ref[...] = h
    ms = jnp.mean(h * h, axis=-1, keepdims=True)
    hn = (h * lax.rsqrt(ms + EPS)) * g_ref[...]
    _pack_rows(hn_ref, hn)
    logits = _dot_nt(wr_ref[...], hn.astype(_BF16))[0:ROUTE_ROWS, :] + br_ref[:, 0:1]
    row = lax.broadcasted_iota(jnp.int32, (ROUTE_ROWS, tm), 0).astype(_F32)
    big = float(ROUTE_ROWS)

    def first_row(mask):
        return jnp.min(jnp.where(mask, row, big), axis=0, keepdims=True)

    gmask = row < N_GROUPS
    gl = jnp.where(gmask, logits, NEG_INF)
    ge = jnp.exp(gl - jnp.max(gl, axis=0, keepdims=True))
    gp = ge / jnp.sum(ge, axis=0, keepdims=True)
    p_group = jnp.max(gp, axis=0, keepdims=True)
    g_sel = first_row(gmask & (gp == p_group))
    lo = ROUTE_LANE0 + g_sel * EXPERTS_PER_GROUP
    emask = (row >= lo) & (row < lo + EXPERTS_PER_GROUP)
    el = jnp.where(emask, logits, NEG_INF)
    ee = jnp.exp(el - jnp.max(el, axis=0, keepdims=True))
    ep = ee / jnp.sum(ee, axis=0, keepdims=True)
    p1 = jnp.max(ep, axis=0, keepdims=True)
    i1 = first_row(emask & (ep == p1))
    ep2 = jnp.where(emask & (row != i1), ep, -1.0)
    p2 = jnp.max(ep2, axis=0, keepdims=True)
    i2 = first_row(ep2 == p2)
    den = p1 + p2
    w1 = p_group * (p1 / den)
    w2 = p_group * (p2 / den)
    hit1 = row == i1
    hit2 = row == i2
    onehot = jnp.where(hit1 | hit2, 1.0, 0.0)
    r_i = lax.broadcasted_iota(jnp.int32, (tm, tm), 0)
    c_i = lax.broadcasted_iota(jnp.int32, (tm, tm), 1)
    earlier = jnp.where(r_i < c_i, 1.0, 0.0).astype(_BF16)
    before = _dot(onehot.astype(_BF16), earlier) + carry_ref[:, 0:1]
    r1 = jnp.sum(jnp.where(hit1, before, 0.0), axis=0, keepdims=True)
    r2 = jnp.sum(jnp.where(hit2, before, 0.0), axis=0, keepdims=True)
    carry_ref[...] = carry_ref[...] + jnp.sum(onehot, axis=1, keepdims=True)
    cnt_ref[...] = carry_ref[...]
    out_row = lax.broadcasted_iota(jnp.int32, (LANES, tm), 0)
    info = jnp.where(out_row == 0, w1, 0.0)
    info = jnp.where(out_row == 1, w2, info)
    info = jnp.where(out_row == 2, i1 - ROUTE_LANE0, info)
    info = jnp.where(out_row == 3, i2 - ROUTE_LANE0, info)
    info = jnp.where(out_row == 4, r1, info)
    info = jnp.where(out_row == 5, r2, info)
    rows_ref[...] = info[0:SUBLANES, :]
    info_ref[...] = info.T


def _out_router(x2, oa, ob, oc, w_out, ffn_g, w_rg, b_rg, w_re, b_re):
    t, d = x2.shape
    tm = min(t, 512)
    pad = LANES - N_GROUPS - N_EXPERTS
    wr = jnp.concatenate([w_rg, w_re, jnp.zeros((d, pad), _F32)], axis=1).T.astype(_BF16)
    br = jnp.concatenate([b_rg, b_re, jnp.zeros((ROUTE_ROWS - N_GROUPS - N_EXPERTS,), _F32)])
    br = jnp.broadcast_to(br[:, None], (ROUTE_ROWS, LANES))

    def rows(w):
        return pl.BlockSpec((tm, w), lambda i: (i, 0))

    def whole(r, c):
        return pl.BlockSpec((r, c), lambda i: (0, 0))

    return pl.pallas_call(
        _out_router_kernel,
        grid=(t // tm,),
        in_specs=[rows(d), rows(A_WIDTH), rows(B_WIDTH), rows(C_WIDTH), whole(d, d), whole(1, d),
                  whole(LANES, d), whole(ROUTE_ROWS, LANES)],
        out_specs=[rows(d), pl.BlockSpec((tm * PACK_ROWS, LANES), lambda i: (i, 0)), rows(LANES),
                   pl.BlockSpec((SUBLANES, tm), lambda i: (0, i)), whole(ROUTE_ROWS, LANES)],
        out_shape=[jax.ShapeDtypeStruct((t, d), _F32), jax.ShapeDtypeStruct((t * PACK_ROWS, LANES), jnp.uint32),
                   jax.ShapeDtypeStruct((t, LANES), _F32), jax.ShapeDtypeStruct((SUBLANES, t), _F32),
                   jax.ShapeDtypeStruct((ROUTE_ROWS, LANES), _F32)],
        scratch_shapes=[pltpu.VMEM((ROUTE_ROWS, LANES), _F32)],
        compiler_params=_cparams(1),
        name="out_router",
    )(x2, oa, ob, oc, w_out.astype(_BF16), ffn_g.reshape(1, d), wr, br)


DISPATCH_TOKENS = 1024
COMBINE_TOKENS = 512


ROW_UNROLL = 8


def _tile_rows(row, count=1, per=SUBLANES):
    start = row * per
    if not isinstance(start, int):
        start = pl.multiple_of(start, per)
    return pl.ds(start, count * per)


def _row_copy(src, s_row, dst, d_row, sem, per=SUBLANES):
    return pltpu.make_async_copy(src.at[_tile_rows(s_row, 1, per)], dst.at[_tile_rows(d_row, 1, per)], sem)


def _dispatch_kernel(pad_start_ref, pad_len_ref, used_ref, dest_ref, hn_ref, xs_ref, zero_ref, sem,
                     pad_sem):
    per = PACK_ROWS
    n = hn_ref.shape[0] // per

    @pl.when(pl.program_id(0) == 0)
    def _():
        zero_ref[...] = jnp.zeros_like(zero_ref)
        n_blocks = xs_ref.shape[0] // (ROW_BLOCK * per)

        def block_copy(blk):
            return pltpu.make_async_copy(zero_ref, xs_ref.at[_tile_rows(blk * ROW_BLOCK, ROW_BLOCK, per)],
                                         pad_sem)

        def put_block(blk, carry):
            block_copy(blk).start()
            return carry

        def done_block(blk, carry):
            block_copy(blk).wait()
            return carry

        lax.fori_loop(used_ref[0], n_blocks, put_block, 0)
        lax.fori_loop(used_ref[0], n_blocks, done_block, 0)
        bits = [1 << k for k in reversed(range(ROW_BLOCK.bit_length() - 1))]

        def tail(e, wait):
            row = pad_start_ref[e]
            for bit in bits:
                on = (pad_len_ref[e] & bit) != 0
                copy = pltpu.make_async_copy(zero_ref.at[_tile_rows(0, bit, per)],
                                             xs_ref.at[_tile_rows(row, bit, per)], pad_sem)

                @pl.when(on)
                def _():
                    copy.wait() if wait else copy.start()

                row = row + jnp.where(on, bit, 0)

        def put_tail(e, carry):
            tail(e, False)
            return carry

        def done_tail(e, carry):
            tail(e, True)
            return carry

        lax.fori_loop(0, N_EXPERTS, put_tail, 0)
        lax.fori_loop(0, N_EXPERTS, done_tail, 0)

    def issue(i, carry):
        for u in range(ROW_UNROLL):
            t = i * ROW_UNROLL + u
            _row_copy(hn_ref, t, xs_ref, dest_ref[2 * t], sem, per).start(priority=0)
            _row_copy(hn_ref, t, xs_ref, dest_ref[2 * t + 1], sem, per).start(priority=1)
        return carry

    lax.fori_loop(0, n // ROW_UNROLL, issue, 0)
    for _ in range(2):
        pltpu.make_async_copy(hn_ref, xs_ref.at[_tile_rows(0, n, per)], sem).wait()


def _dispatch(hn, dest, pad_start, pad_len, n_used, n_rows):
    t = hn.shape[0] // PACK_ROWS
    n = min(t, DISPATCH_TOKENS)
    return pl.pallas_call(
        _dispatch_kernel,
        grid_spec=pltpu.PrefetchScalarGridSpec(
            num_scalar_prefetch=3,
            grid=(t // n,),
            in_specs=[pl.BlockSpec((2 * n,), lambda i, *_: (i,), memory_space=pltpu.SMEM),
                      pl.BlockSpec((n * PACK_ROWS, LANES), lambda i, *_: (i, 0))],
            out_specs=pl.BlockSpec(memory_space=pl.ANY),
            scratch_shapes=[pltpu.VMEM((ROW_BLOCK * PACK_ROWS, LANES), hn.dtype), pltpu.SemaphoreType.DMA,
                            pltpu.SemaphoreType.DMA]),
        out_shape=jax.ShapeDtypeStruct((n_rows * PACK_ROWS, LANES), hn.dtype),
        compiler_params=_cparams(1),
        name="moe_dispatch",
    )(pad_start, pad_len, n_used, dest, hn)


def _expert_kernel(be_ref, used_ref, x_ref, wg_ref, wu_ref, wd_ref, y_ref, wg_bf, wu_bf, wd_bf):
    i = pl.program_id(0)
    live = i < used_ref[0]
    new_expert = (i == 0) | (be_ref[i] != be_ref[jnp.maximum(i - 1, 0)])

    @pl.when(live & new_expert)
    def _():
        wg_bf[...] = wg_ref[...].astype(_BF16)
        wu_bf[...] = wu_ref[...].astype(_BF16)
        wd_bf[...] = wd_ref[...].astype(_BF16)

    @pl.when(live)
    def _():
        sub = ROW_BLOCK // EXPERT_SPLIT
        gate_up = {}

        def first(k):
            x = _unpack_rows(x_ref, sub, k * sub)
            gate_up[k] = (_dot(x, wg_bf[...]), _dot(x, wu_bf[...]))

        def second(k):
            gate, up = gate_up.pop(k)
            act = (gate * jax.nn.sigmoid(gate)) * up
            _rows_to_tiles(y_ref, _dot(act.astype(_BF16), wd_bf[...]), k * sub)

        for k in range(min(EXPERT_AHEAD, EXPERT_SPLIT)):
            first(k)
        for k in range(EXPERT_SPLIT):
            if k + EXPERT_AHEAD < EXPERT_SPLIT:
                first(k + EXPERT_AHEAD)
            second(k)

    @pl.when(i >= used_ref[0])
    def _():
        y_ref[...] = jnp.zeros_like(y_ref)


def _experts(xs, block_e, n_used, w_gate, w_up, w_down):
    n_rows, d = xs.shape[0] // PACK_ROWS, D_MODEL
    n_blocks = n_rows // ROW_BLOCK
    tile_block = (ROW_BLOCK * SUBLANES, LANES)

    def xmap(i, be, used):
        return (jnp.minimum(i, used[0] - 1), 0)

    def wmap(i, be, used):
        return (be[jnp.minimum(i, used[0] - 1)], 0, 0)

    return pl.pallas_call(
        _expert_kernel,
        grid_spec=pltpu.PrefetchScalarGridSpec(
            num_scalar_prefetch=2,
            grid=(n_blocks,),
            in_specs=[pl.BlockSpec((ROW_BLOCK * PACK_ROWS, LANES), xmap),
                      pl.BlockSpec((None, d, D_EXPERT), wmap),
                      pl.BlockSpec((None, d, D_EXPERT), wmap),
                      pl.BlockSpec((None, D_EXPERT, d), wmap)],
            out_specs=pl.BlockSpec(tile_block, lambda i, be, used: (i, 0)),
            scratch_shapes=[pltpu.VMEM((d, D_EXPERT), _BF16), pltpu.VMEM((d, D_EXPERT), _BF16),
                            pltpu.VMEM((D_EXPERT, d), _BF16)]),
        out_shape=jax.ShapeDtypeStruct((n_rows * SUBLANES, LANES), _F32),
        compiler_params=_cparams(1),
        name="moe_experts",
    )(block_e, n_used, xs, w_gate, w_up, w_down)


def _combine_kernel(dest_ref, next_ref, h_ref, info_ref, ys_ref, o_ref, buf_ref, sem):
    n = h_ref.shape[0]
    step = pl.program_id(0)
    slot = step % 2

    def gather(idx_ref, to_slot):
        def issue(i, carry):
            for u in range(ROW_UNROLL):
                t = i * ROW_UNROLL + u
                _row_copy(ys_ref, idx_ref[2 * t], buf_ref.at[to_slot, 0], t,
                          sem.at[to_slot]).start(priority=0)
                _row_copy(ys_ref, idx_ref[2 * t + 1], buf_ref.at[to_slot, 1], t,
                          sem.at[to_slot]).start(priority=1)
            return carry

        lax.fori_loop(0, n // ROW_UNROLL, issue, 0)

    @pl.when(step == 0)
    def _():
        gather(dest_ref, 0)

    @pl.when(step + 1 < pl.num_programs(0))
    def _():
        gather(next_ref, 1 - slot)

    for k in range(2):
        pltpu.make_async_copy(ys_ref.at[_tile_rows(0, n)], buf_ref.at[slot, k], sem.at[slot]).wait()
    info = info_ref[...]
    w0 = info[:, 0:1]
    w1 = info[:, 1:2]
    for s in range(SUBLANES):
        sl = slice(s * LANES, (s + 1) * LANES)
        moe = w0 * _tile_block(buf_ref.at[slot, 0], s, n) + w1 * _tile_block(buf_ref.at[slot, 1], s, n)
        o_ref[:, sl] = h_ref[:, sl] + moe


def _combine(h, info, ys, dest):
    t, d = h.shape
    n = min(t, COMBINE_TOKENS)
    steps = t // n
    return pl.pallas_call(
        _combine_kernel,
        grid=(steps,),
        in_specs=[pl.BlockSpec((2 * n,), lambda i: (i,), memory_space=pltpu.SMEM),
                  pl.BlockSpec((2 * n,), lambda i: (jnp.minimum(i + 1, steps - 1),),
                               memory_space=pltpu.SMEM),
                  pl.BlockSpec((n, d), lambda i: (i, 0)),
                  pl.BlockSpec((n, LANES), lambda i: (i, 0)),
                  pl.BlockSpec(memory_space=pl.ANY)],
        out_specs=pl.BlockSpec((n, d), lambda i: (i, 0)),
        out_shape=jax.ShapeDtypeStruct((t, d), _F32),
        scratch_shapes=[pltpu.VMEM((2, 2, n * SUBLANES, LANES), _F32), pltpu.SemaphoreType.DMA((2,))],
        compiler_params=_cparams(1),
        name="moe_combine",
    )(dest, dest, h, info, ys)


def _moe_layout(route_rows, counts, t):
    counts = counts[ROUTE_LANE0:ROUTE_LANE0 + N_EXPERTS, 0].astype(jnp.int32)
    padded = (counts + ROW_BLOCK - 1) // ROW_BLOCK * ROW_BLOCK
    pends = jnp.cumsum(padded)
    pstarts = pends - padded
    eid = route_rows[2:4].astype(jnp.int32)
    rank = route_rows[4:6].astype(jnp.int32)
    experts = jnp.arange(N_EXPERTS, dtype=jnp.int32)
    start_of = jnp.sum(jnp.where(eid[:, :, None] == experts, pstarts, 0), axis=-1)
    dest = (start_of + rank).T.reshape(-1)
    n_blocks = -(-2 * t // ROW_BLOCK) + N_EXPERTS
    first_row = jnp.arange(n_blocks, dtype=jnp.int32) * ROW_BLOCK
    block_e = jnp.minimum(jnp.sum((pends[None, :] <= first_row[:, None]).astype(jnp.int32), axis=1),
                          N_EXPERTS - 1)
    n_used = (pends[-1:] // ROW_BLOCK).astype(jnp.int32)
    return dest, block_e, n_used, pstarts + counts, padded - counts, n_blocks * ROW_BLOCK


def kernel(x, mem, positions, mix_norm_g, w_in, qn_a, kn_a, rel_bias, ret_gn_g, mem_norm_g, w_mem_kv,
           qn_c, kn_c, w_out, ffn_norm_g, w_router_group, b_router_group, w_router_expert,
           b_router_expert, w_gate, w_up, w_down):
    b, s, d = x.shape
    t = b * s
    x2 = x.reshape(t, d)
    cos, sin = _rope_tables(positions)
    kc, vc = _mem_kv(mem, mem_norm_g, w_mem_kv, kn_c)
    proj3 = _in_proj(x2, mix_norm_g, w_in).reshape(b, s, IN_COLS)
    out_a = _attention(proj3, qn_a, kn_a, rel_bias)
    out_b = _retention(proj3, cos, sin, ret_gn_g)
    out_c = _cross_attention(proj3, kc, vc, qn_c)
    h, hn, info, route_rows, counts = _out_router(
        x2, out_a.reshape(t, A_WIDTH), out_b.reshape(t, B_WIDTH), out_c.reshape(t, C_WIDTH),
        w_out, ffn_norm_g, w_router_group, b_router_group, w_router_expert, b_router_expert)
    dest, block_e, n_used, pad_start, pad_len, n_rows = _moe_layout(route_rows, counts, t)
    xs = _dispatch(hn, dest, pad_start, pad_len, n_used, n_rows)
    ys = _experts(xs, block_e, n_used, w_gate, w_up, w_down)
    return _combine(h, info, ys, dest).reshape(b, s, d)
```

```python
import functools

import jax
import jax.numpy as jnp
from jax import lax
from jax.experimental import pallas as pl
from jax.experimental.pallas import tpu as pltpu

D_MODEL = 1024
CHUNK = 64
HEAD_DIM = 64
A_HEADS = 8
B_HEADS = 4
C_HEADS = 4
A_WIDTH = A_HEADS * HEAD_DIM
B_WIDTH = B_HEADS * HEAD_DIM
C_WIDTH = C_HEADS * HEAD_DIM
IN_COLS = 3 * A_WIDTH + 4 * B_WIDTH + C_WIDTH
LEFT_CHUNKS = 8
BAND_CHUNKS = LEFT_CHUNKS + 1
MAX_REL_DIST = 128
ROPE_BASE = 10000.0
N_GROUPS = 4
EXPERTS_PER_GROUP = 8
N_EXPERTS = N_GROUPS * EXPERTS_PER_GROUP
D_EXPERT = D_MODEL // 2
EPS = 1e-6
NEG_INF = -1e30
LOG2E = 1.4426950408889634

LANES = 128
SUBLANES = 8
assert D_MODEL == SUBLANES * LANES
PACK_ROWS = SUBLANES // 2
LEFT_ROWS = LEFT_CHUNKS * CHUNK
ATT_Q = 2 * CHUNK
ATT_K = ATT_Q + LEFT_ROWS
ATT_VARIANTS = LEFT_ROWS // ATT_Q + 1
ONES_ROWS = 16
ATT_AHEAD = 3
RET_CHUNK = 256
ROW_BLOCK = 512
EXPERT_SPLIT = 2
EXPERT_AHEAD = 2
ROUTE_LANE0 = N_GROUPS
ROUTE_ROWS = 64
VMEM_LIMIT = 48 * 1024 * 1024

_F32 = jnp.float32
_BF16 = jnp.bfloat16


def _cparams(n_axes):
    return pltpu.CompilerParams(dimension_semantics=("arbitrary",) * n_axes,
                                vmem_limit_bytes=VMEM_LIMIT)


def _dot(a, b):
    return jnp.dot(a, b, preferred_element_type=_F32)


def _dot_nt(a, b):
    return lax.dot_general(a, b, (((1,), (1,)), ((), ())), preferred_element_type=_F32)


def _lane(shape):
    return lax.broadcasted_iota(jnp.int32, shape, len(shape) - 1)


def _pair_rms(t, gain):
    low = _lane(t.shape) < HEAD_DIM
    t2 = t * t
    ms0 = jnp.sum(jnp.where(low, t2, 0.0), axis=-1, keepdims=True) * (1.0 / HEAD_DIM)
    ms1 = jnp.sum(jnp.where(low, 0.0, t2), axis=-1, keepdims=True) * (1.0 / HEAD_DIM)
    r = jnp.where(low, lax.rsqrt(ms0 + EPS), lax.rsqrt(ms1 + EPS))
    return (t * r) * gain


def _rows_to_tiles(ref, val, row0=0):
    n = val.shape[0]
    for s in range(SUBLANES):
        ref[pl.ds(row0 * SUBLANES + s, n, stride=SUBLANES), :] = val[:, s * LANES:(s + 1) * LANES]


def _tile_block(ref, s, n, row0=0):
    return ref[pl.ds(row0 * SUBLANES + s, n, stride=SUBLANES), :]


def _tiles_to_rows(ref, n, row0=0):
    return jnp.concatenate([_tile_block(ref, s, n, row0) for s in range(SUBLANES)], axis=-1)


def _pack_rows(ref, val, row0=0):
    n = val.shape[0]
    for s in range(PACK_ROWS):
        lo = val[:, (2 * s) * LANES:(2 * s + 1) * LANES].astype(_BF16).astype(_F32)
        hi = val[:, (2 * s + 1) * LANES:(2 * s + 2) * LANES].astype(_BF16).astype(_F32)
        word = (lax.bitcast_convert_type(lo, jnp.uint32) >> 16) | (
            lax.bitcast_convert_type(hi, jnp.uint32) & jnp.uint32(0xFFFF0000))
        ref[pl.ds(row0 * PACK_ROWS + s, n, stride=PACK_ROWS), :] = word


def _unpack_rows(ref, n, row0=0):
    parts = []
    for s in range(PACK_ROWS):
        word = ref[pl.ds(row0 * PACK_ROWS + s, n, stride=PACK_ROWS), :]
        parts.append(lax.bitcast_convert_type(word << 16, _F32))
        parts.append(lax.bitcast_convert_type(word & jnp.uint32(0xFFFF0000), _F32))
    return jnp.concatenate(parts, axis=-1).astype(_BF16)


ROPE_HALF = HEAD_DIM // 2
ROPE_PACK = LANES // ROPE_HALF


def _rope_kernel(pos_ref, inv_ref, cos_ref, sin_ref):
    ang = pos_ref[...].astype(_F32) * inv_ref[...]
    rows = ang.shape[0]
    lane = _lane(ang.shape)
    sign = jnp.where((lane % HEAD_DIM) < ROPE_HALF, -1.0, 1.0)
    for out_ref, val in ((cos_ref, jnp.cos(ang)), (sin_ref, jnp.sin(ang))):
        for j in range(ROPE_PACK):
            seg = jnp.where(lane // ROPE_HALF == j, val, 0.0)
            full = seg
            for k in range(1, ROPE_PACK):
                full = full + pltpu.roll(seg, k * ROPE_HALF, 1)
            if out_ref is sin_ref:
                full = full * sign
            out_ref[pl.ds(j, rows, stride=ROPE_PACK), :] = full


def _rope_tables(positions):
    t = positions.size
    inv = ROPE_BASE ** (-jnp.arange(ROPE_HALF, dtype=_F32) / ROPE_HALF)
    inv128 = jnp.tile(inv, ROPE_PACK).reshape(1, LANES)
    rows = t // ROPE_PACK
    pos = jnp.repeat(positions.reshape(rows, ROPE_PACK), ROPE_HALF, axis=1)
    tm = min(rows, 512)
    out = pl.BlockSpec((tm * ROPE_PACK, LANES), lambda i: (i, 0))
    return pl.pallas_call(
        _rope_kernel,
        grid=(rows // tm,),
        in_specs=[pl.BlockSpec((tm, LANES), lambda i: (i, 0)), pl.BlockSpec((1, LANES), lambda i: (0, 0))],
        out_specs=[out, out],
        out_shape=[jax.ShapeDtypeStruct((t, LANES), _F32)] * 2,
        compiler_params=_cparams(1),
        name="rope_tables",
    )(pos, inv128)


def _mem_kv_kernel(mem_ref, g_ref, w_ref, kn_ref, k_ref, v_ref):
    m = mem_ref[...]
    ms = jnp.mean(m * m, axis=-1, keepdims=True)
    mn = (m * lax.rsqrt(ms + EPS)) * g_ref[...]
    kv = _dot(mn.astype(_BF16), w_ref[...])
    for j in range(C_WIDTH // LANES):
        sl = slice(j * LANES, (j + 1) * LANES)
        k_ref[:, sl] = _pair_rms(kv[:, sl], kn_ref[...]).astype(_BF16)
    v_ref[...] = kv[:, C_WIDTH:].T.astype(_BF16)


def _mem_kv(mem, mem_norm_g, w_mem_kv, kn_c):
    b, m, d = mem.shape
    kn = jnp.tile(kn_c, 2).reshape(1, LANES)
    return pl.pallas_call(
        _mem_kv_kernel,
        grid=(b,),
        in_specs=[pl.BlockSpec((None, m, d), lambda i: (i, 0, 0)),
                  pl.BlockSpec((1, d), lambda i: (0, 0)),
                  pl.BlockSpec((d, 2 * C_WIDTH), lambda i: (0, 0)),
                  pl.BlockSpec((1, LANES), lambda i: (0, 0))],
        out_specs=[pl.BlockSpec((None, m, C_WIDTH), lambda i: (i, 0, 0)),
                   pl.BlockSpec((None, C_WIDTH, m), lambda i: (i, 0, 0))],
        out_shape=[jax.ShapeDtypeStruct((b, m, C_WIDTH), _BF16),
                   jax.ShapeDtypeStruct((b, C_WIDTH, m), _BF16)],
        compiler_params=_cparams(1),
        name="mem_kv",
    )(mem, mem_norm_g.reshape(1, d), w_mem_kv.astype(_BF16), kn)


def _in_proj_kernel(x_ref, g_ref, w_ref, o_ref):
    x = x_ref[...]
    ms = jnp.mean(x * x, axis=-1, keepdims=True)
    xn = ((x * lax.rsqrt(ms + EPS)) * g_ref[...]).astype(_BF16)
    o_ref[...] = _dot(xn, w_ref[...])


def _in_proj(x2, g, w_in):
    t, d = x2.shape
    tm = min(t, 512)
    return pl.pallas_call(
        _in_proj_kernel,
        grid=(t // tm,),
        in_specs=[pl.BlockSpec((tm, d), lambda i: (i, 0)),
                  pl.BlockSpec((1, d), lambda i: (0, 0)),
                  pl.BlockSpec((d, IN_COLS), lambda i: (0, 0))],
        out_specs=pl.BlockSpec((tm, IN_COLS), lambda i: (i, 0)),
        out_shape=jax.ShapeDtypeStruct((t, IN_COLS), _F32),
        compiler_params=_cparams(1),
        name="in_proj",
    )(x2, g.reshape(1, d), w_in.astype(_BF16))


def _attn_kernel(q_ref, k_ref, v_ref, qn_ref, kn_ref, bias_ref, o_ref, kp_ref, vt_ref, st_ref, var_ref, *,
                 q_rows):
    qs = pl.program_id(2)
    s = k_ref.shape[0]
    fill_rows = min(s, 512)
    left_blocks = LEFT_ROWS // LANES

    @pl.when(qs == 0)
    def _():
        kp_ref[0:LEFT_ROWS, :] = jnp.zeros((LEFT_ROWS, LANES), _BF16)
        for blk in range(left_blocks):
            vt_ref[blk] = jnp.zeros((LANES, LANES), _BF16)
        eye = jnp.where(lax.broadcasted_iota(jnp.int32, (LANES, LANES), 0) == _lane((LANES, LANES)),
                        1.0, 0.0).astype(_BF16)

        def fill(i, carry):
            r = pl.multiple_of(i * fill_rows, fill_rows)
            kp_ref[pl.ds(LEFT_ROWS + r, fill_rows), :] = _pair_rms(
                k_ref[pl.ds(r, fill_rows), :], kn_ref[...]).astype(_BF16)
            vt = _dot_nt(eye, v_ref[pl.ds(r, fill_rows), :].astype(_BF16))
            for j in range(fill_rows // LANES):
                vt_ref[left_blocks + i * (fill_rows // LANES) + j] = vt[:, j * LANES:(j + 1) * LANES].astype(_BF16)
            return carry

        lax.fori_loop(0, s // fill_rows, fill, 0)
        key = lax.broadcasted_iota(jnp.int32, (ATT_K, 2 * ATT_Q), 0)
        for v in range(ATT_VARIANTS):
            var_ref[v] = jnp.where(key >= LEFT_ROWS - ATT_Q * v, bias_ref[...], NEG_INF)

    low = _lane((ATT_Q, LANES)) < HEAD_DIM
    ones = jnp.ones((ONES_ROWS, ATT_K), _BF16)
    tiles_per_step = q_rows // ATT_Q

    def scores(j):
        cp = qs * tiles_per_step + j
        qn = _pair_rms(q_ref[j * ATT_Q:(j + 1) * ATT_Q, :], qn_ref[...])
        q2 = jnp.concatenate([jnp.where(low, qn, 0.0), jnp.where(low, 0.0, qn)], axis=0).astype(_BF16)
        kb = kp_ref[pl.ds(pl.multiple_of(cp * ATT_Q, ATT_Q), ATT_K), :]
        st_ref[j % (ATT_AHEAD + 1)] = _dot_nt(kb, q2) + var_ref[jnp.minimum(cp, ATT_VARIANTS - 1)]

    def finish(j):
        cp = qs * tiles_per_step + j
        st = st_ref[j % (ATT_AHEAD + 1)]
        m = jnp.max(st, axis=0, keepdims=True)
        p = jnp.exp2(st - m)
        vt = jnp.concatenate([vt_ref[cp + kb_i] for kb_i in range(ATT_K // LANES)] , axis=1)
        ot = _dot(jnp.concatenate([vt, ones], axis=0), p.astype(_BF16))
        inv = 1.0 / ot[LANES:LANES + 1, :]
        out_t = jnp.concatenate([ot[0:HEAD_DIM, 0:ATT_Q] * inv[:, 0:ATT_Q],
                                 ot[HEAD_DIM:LANES, ATT_Q:] * inv[:, ATT_Q:]], axis=0)
        o_ref[j * ATT_Q:(j + 1) * ATT_Q, :] = out_t.T.astype(o_ref.dtype)

    for j in range(min(ATT_AHEAD, tiles_per_step)):
        scores(j)
    for j in range(tiles_per_step):
        if j + ATT_AHEAD < tiles_per_step:
            scores(j + ATT_AHEAD)
        finish(j)


def _toeplitz_bias(rel_bias, q_len, k_len):
    h, table = rel_bias.shape
    n_diag = q_len + k_len - 1
    flat_lo = k_len - 1 - LEFT_ROWS - (CHUNK - 1)
    flat_hi = n_diag - flat_lo - table
    rev = jnp.concatenate([jnp.broadcast_to(rel_bias[:, -1:], (h, flat_hi)), rel_bias[:, ::-1],
                           jnp.broadcast_to(rel_bias[:, :1], (h, flat_lo))], axis=1).astype(_F32)
    flat = jnp.tile(rev, (1, q_len + 1))
    pitch = n_diag - 1
    skew = flat[:, q_len - 1:q_len - 1 + q_len * pitch].reshape(h, q_len, pitch)
    return skew[:, :, :k_len]


def _attn_bias(rel_bias):
    h = rel_bias.shape[0]
    bias = _toeplitz_bias(rel_bias, ATT_Q, ATT_K)
    q = lax.broadcasted_iota(jnp.int32, (ATT_Q, ATT_K), 0)
    k = lax.broadcasted_iota(jnp.int32, (ATT_Q, ATT_K), 1)
    off = k // CHUNK - q // CHUNK
    in_band = (off >= 0) & (off < BAND_CHUNKS)
    full = jnp.where(in_band[None], bias * LOG2E, NEG_INF)
    full = full.reshape(h // 2, 2, ATT_Q, ATT_K)
    return full.transpose(0, 3, 1, 2).reshape(h // 2, ATT_K, 2 * ATT_Q)


def _attention(proj3, qn_a, kn_a, rel_bias):
    b, s, _ = proj3.shape
    q_rows = min(s, 2048)
    qn = (jnp.tile(qn_a, 2) * (HEAD_DIM ** -0.5 * LOG2E)).reshape(1, LANES)
    kn = jnp.tile(kn_a, 2).reshape(1, LANES)
    pairs = A_HEADS // 2
    kcol, vcol = A_WIDTH // LANES, 2 * A_WIDTH // LANES
    return pl.pallas_call(
        functools.partial(_attn_kernel, q_rows=q_rows),
        grid=(b, pairs, s // q_rows),
        in_specs=[pl.BlockSpec((None, q_rows, LANES), lambda i, p, j: (i, j, p)),
                  pl.BlockSpec((None, s, LANES), lambda i, p, j: (i, 0, kcol + p)),
                  pl.BlockSpec((None, s, LANES), lambda i, p, j: (i, 0, vcol + p)),
                  pl.BlockSpec((1, LANES), lambda i, p, j: (0, 0)),
                  pl.BlockSpec((1, LANES), lambda i, p, j: (0, 0)),
                  pl.BlockSpec((None, ATT_K, 2 * ATT_Q), lambda i, p, j: (p, 0, 0))],
        out_specs=pl.BlockSpec((None, q_rows, LANES), lambda i, p, j: (i, j, p)),
        out_shape=jax.ShapeDtypeStruct((b, s, A_WIDTH), _BF16),
        scratch_shapes=[pltpu.VMEM((s + LEFT_ROWS, LANES), _BF16),
                        pltpu.VMEM(((s + LEFT_ROWS) // LANES, LANES, LANES), _BF16),
                        pltpu.VMEM((ATT_AHEAD + 1, ATT_K, 2 * ATT_Q), _F32),
                        pltpu.VMEM((ATT_VARIANTS, ATT_K, 2 * ATT_Q), _F32)],
        compiler_params=_cparams(3),
        name="attn_a",
    )(proj3, proj3, proj3, qn, kn, _attn_bias(rel_bias))


def _swap_halves(t):
    first = (_lane(t.shape) % HEAD_DIM) < (HEAD_DIM // 2)
    return jnp.where(first, pltpu.roll(t, LANES - HEAD_DIM // 2, 1), pltpu.roll(t, HEAD_DIM // 2, 1))


def _retention_kernel(q_ref, k_ref, v_ref, gate_ref, cos_ref, sin_ref, decay_ref, zeta_ref, xi_ref,
                      cd_ref, gn_ref, o_ref, state_ref, *, rows):
    @pl.when(pl.program_id(2) == 0)
    def _():
        state_ref[...] = jnp.zeros_like(state_ref)

    c = RET_CHUNK
    low = _lane((c, LANES)) < HEAD_DIM
    eye = jnp.where(lax.broadcasted_iota(jnp.int32, (LANES, LANES), 0) == _lane((LANES, LANES)),
                    1.0, 0.0).astype(_BF16)
    srow = lax.broadcasted_iota(jnp.int32, (LANES, LANES), 0) < HEAD_DIM
    scol = _lane((LANES, LANES)) < HEAD_DIM
    same_head = srow == scol

    for j in range(rows // c):
        sl = slice(j * c, (j + 1) * c)
        cos, sin = cos_ref[sl, :], sin_ref[sl, :]
        q = q_ref[sl, :]
        k = k_ref[sl, :]
        qr = q * cos + _swap_halves(q) * sin
        kr = (k * cos + _swap_halves(k) * sin) * (HEAD_DIM ** -0.5)
        vb = v_ref[sl, :].astype(_BF16)
        qb = qr.astype(_BF16)
        kb = kr.astype(_BF16)
        inner_out = []
        for h in range(2):
            qh = jnp.where(low if h == 0 else ~low, qr, 0.0).astype(_BF16)
            inner = _dot_nt(qh, kb) * decay_ref[h]
            inner_out.append(_dot(inner.astype(_BF16), vb))
        state = state_ref[...]
        cross = _dot(qb, state.astype(_BF16)) * xi_ref[...]
        o = jnp.where(low, inner_out[0], inner_out[1]) + cross
        kz = _dot_nt(eye, (kr * zeta_ref[...]).astype(_BF16)).astype(_BF16)
        state_ref[...] = cd_ref[...] * state + jnp.where(same_head, _dot(kz, vb), 0.0)
        mu = jnp.where(low,
                       jnp.sum(jnp.where(low, o, 0.0), axis=-1, keepdims=True),
                       jnp.sum(jnp.where(low, 0.0, o), axis=-1, keepdims=True)) * (1.0 / HEAD_DIM)
        dlt = o - mu
        d2 = dlt * dlt
        var = jnp.where(low,
                        jnp.sum(jnp.where(low, d2, 0.0), axis=-1, keepdims=True),
                        jnp.sum(jnp.where(low, 0.0, d2), axis=-1, keepdims=True)) * (1.0 / HEAD_DIM)
        y = (dlt * lax.rsqrt(var + EPS)) * gn_ref[...]
        g = gate_ref[sl, :]
        o_ref[sl, :] = ((g * jax.nn.sigmoid(g)) * y).astype(o_ref.dtype)


def _retention_tables():
    c = RET_CHUNK
    log_g = jnp.log(1.0 - jnp.exp2(-5.0 - jnp.arange(B_HEADS, dtype=_F32)))
    idx = jnp.arange(c, dtype=_F32)
    diff = idx[:, None] - idx[None, :]
    decay = jnp.where(diff >= 0, jnp.exp(log_g[:, None, None] * jnp.maximum(diff, 0.0)), 0.0)
    zeta = jnp.exp(log_g[:, None] * (c - 1 - idx))
    xi = jnp.exp(log_g[:, None] * (idx + 1.0))
    cd = jnp.exp(log_g * c)

    def lanes(tab):
        return jnp.repeat(tab.reshape(B_HEADS // 2, 2, c), HEAD_DIM, axis=1).transpose(0, 2, 1)

    cdm = jnp.repeat(cd.reshape(B_HEADS // 2, 2), HEAD_DIM, axis=1)
    cdm = jnp.broadcast_to(cdm[:, :, None], (B_HEADS // 2, LANES, LANES))
    return decay, lanes(zeta), lanes(xi), cdm


def _retention(proj3, cos, sin, ret_gn_g):
    b, s, _ = proj3.shape
    rows = min(s, 2048)
    pairs = B_HEADS // 2
    base = 3 * A_WIDTH // LANES
    decay, zeta, xi, cdm = _retention_tables()
    cos3, sin3 = cos.reshape(b, s, LANES), sin.reshape(b, s, LANES)
    gn = ret_gn_g.reshape(pairs, 1, LANES)

    def col(off):
        return pl.BlockSpec((None, rows, LANES), lambda i, p, j: (i, j, base + off * pairs + p))

    tab = pl.BlockSpec((None, rows, LANES), lambda i, p, j: (i, j, 0))
    return pl.pallas_call(
        functools.partial(_retention_kernel, rows=rows),
        grid=(b, pairs, s // rows),
        in_specs=[col(0), col(1), col(2), col(3), tab, tab,
                  pl.BlockSpec((2, RET_CHUNK, RET_CHUNK), lambda i, p, j: (p, 0, 0)),
                  pl.BlockSpec((None, RET_CHUNK, LANES), lambda i, p, j: (p, 0, 0)),
                  pl.BlockSpec((None, RET_CHUNK, LANES), lambda i, p, j: (p, 0, 0)),
                  pl.BlockSpec((None, LANES, LANES), lambda i, p, j: (p, 0, 0)),
                  pl.BlockSpec((None, 1, LANES), lambda i, p, j: (p, 0, 0))],
        out_specs=pl.BlockSpec((None, rows, LANES), lambda i, p, j: (i, j, p)),
        out_shape=jax.ShapeDtypeStruct((b, s, B_WIDTH), _BF16),
        scratch_shapes=[pltpu.VMEM((LANES, LANES), _F32)],
        compiler_params=_cparams(3),
        name="retention_b",
    )(proj3, proj3, proj3, proj3, cos3, sin3, decay, zeta, xi, cdm, gn)


def _cross_kernel(q_ref, k_ref, vt_ref, qn_ref, o_ref, st_ref, *, rows):
    low = _lane((ATT_Q, LANES)) < HEAD_DIM
    lane_blocks = C_WIDTH // LANES
    tiles = [(j, lb) for j in range(rows // ATT_Q) for lb in range(lane_blocks)]
    ones = jnp.ones((ONES_ROWS, vt_ref.shape[1]), _BF16)

    def scores(i):
        j, lb = tiles[i]
        sl = slice(lb * LANES, (lb + 1) * LANES)
        qn = _pair_rms(q_ref[j * ATT_Q:(j + 1) * ATT_Q, sl], qn_ref[...])
        q2 = jnp.concatenate([jnp.where(low, qn, 0.0), jnp.where(low, 0.0, qn)], axis=0).astype(_BF16)
        st_ref[i % (ATT_AHEAD + 1)] = _dot_nt(k_ref[:, sl], q2)

    def finish(i):
        j, lb = tiles[i]
        sl = slice(lb * LANES, (lb + 1) * LANES)
        st = st_ref[i % (ATT_AHEAD + 1)]
        p = jnp.exp2(st - jnp.max(st, axis=0, keepdims=True))
        ot = _dot(jnp.concatenate([vt_ref[sl, :], ones], axis=0), p.astype(_BF16))
        inv = 1.0 / ot[LANES:LANES + 1, :]
        out_t = jnp.concatenate([ot[0:HEAD_DIM, 0:ATT_Q] * inv[:, 0:ATT_Q],
                                 ot[HEAD_DIM:LANES, ATT_Q:] * inv[:, ATT_Q:]], axis=0)
        o_ref[j * ATT_Q:(j + 1) * ATT_Q, sl] = out_t.T.astype(o_ref.dtype)

    for i in range(min(ATT_AHEAD, len(tiles))):
        scores(i)
    for i in range(len(tiles)):
        if i + ATT_AHEAD < len(tiles):
            scores(i + ATT_AHEAD)
        finish(i)


def _cross_attention(proj3, kc, vtc, qn_c):
    b, s, _ = proj3.shape
    m = kc.shape[1]
    rows = min(s, 1024)
    qn = (jnp.tile(qn_c, 2) * (HEAD_DIM ** -0.5 * LOG2E)).reshape(1, LANES)
    qcol = (3 * A_WIDTH + 4 * B_WIDTH) // C_WIDTH
    return pl.pallas_call(
        functools.partial(_cross_kernel, rows=rows),
        grid=(b, s // rows),
        in_specs=[pl.BlockSpec((None, rows, C_WIDTH), lambda i, j: (i, j, qcol)),
                  pl.BlockSpec((None, m, C_WIDTH), lambda i, j: (i, 0, 0)),
                  pl.BlockSpec((None, C_WIDTH, m), lambda i, j: (i, 0, 0)),
                  pl.BlockSpec((1, LANES), lambda i, j: (0, 0))],
        out_specs=pl.BlockSpec((None, rows, C_WIDTH), lambda i, j: (i, j, 0)),
        out_shape=jax.ShapeDtypeStruct((b, s, C_WIDTH), _BF16),
        scratch_shapes=[pltpu.VMEM((ATT_AHEAD + 1, m, 2 * ATT_Q), _F32)],
        compiler_params=_cparams(2),
        name="cross_c",
    )(proj3, kc, vtc, qn)


def _out_router_kernel(x_ref, a_ref, b_ref, c_ref, wo_ref, g_ref, wr_ref, br_ref,
                       h_ref, hn_ref, info_ref, rows_ref, cnt_ref, carry_ref, hbuf_ref):
    i = pl.program_id(0)

    @pl.when(i == 0)
    def _():
        carry_ref[...] = jnp.zeros_like(carry_ref)
        hbuf_ref[1] = jnp.zeros(hbuf_ref.shape[1:], _F32)

    valid = jnp.where(i > 0, 1.0, 0.0)
    for cur in range(2):
        @pl.when(i % 2 == cur)
        def _(cur=cur):
            mix = jnp.concatenate([a_ref[...], b_ref[...], c_ref[...]], axis=-1)
            hbuf_ref[cur] = x_ref[...] + _dot(mix, wo_ref[...])
            carry = _route(hbuf_ref[1 - cur], carry_ref[:, 0:1], valid, g_ref, wr_ref, br_ref,
                           h_ref, hn_ref, info_ref, rows_ref)
            carry_ref[...] = jnp.broadcast_to(carry, carry_ref.shape)
            cnt_ref[...] = carry_ref[...]


def _route(h, carry, valid, g_ref, wr_ref, br_ref, h_ref, hn_ref, info_ref, rows_ref):
    tm = h.shape[0]
    h_ref[...] = h
    ms = jnp.mean(h * h, axis=-1, keepdims=True)
    hn = (h * lax.rsqrt(ms + EPS)) * g_ref[...]
    _pack_rows(hn_ref, hn)
    logits = _dot_nt(wr_ref[...], hn.astype(_BF16))[0:ROUTE_ROWS, :] + br_ref[:, 0:1]
    row = lax.broadcasted_iota(jnp.int32, (ROUTE_ROWS, tm), 0).astype(_F32)
    big = float(ROUTE_ROWS)

    def first_row(mask):
        return jnp.min(jnp.where(mask, row, big), axis=0, keepdims=True)

    gmask = row < N_GROUPS
    gl = jnp.where(gmask, logits, NEG_INF)
    ge = jnp.exp(gl - jnp.max(gl, axis=0, keepdims=True))
    gp = ge / jnp.sum(ge, axis=0, keepdims=True)
    p_group = jnp.max(gp, axis=0, keepdims=True)
    g_sel = first_row(gmask & (gp == p_group))
    lo = ROUTE_LANE0 + g_sel * EXPERTS_PER_GROUP
    emask = (row >= lo) & (row < lo + EXPERTS_PER_GROUP)
    el = jnp.where(emask, logits, NEG_INF)
    ee = jnp.exp(el - jnp.max(el, axis=0, keepdims=True))
    ep = ee / jnp.sum(ee, axis=0, keepdims=True)
    p1 = jnp.max(ep, axis=0, keepdims=True)
    i1 = first_row(emask & (ep == p1))
    ep2 = jnp.where(emask & (row != i1), ep, -1.0)
    p2 = jnp.max(ep2, axis=0, keepdims=True)
    i2 = first_row(ep2 == p2)
    den = p1 + p2
    w1 = p_group * (p1 / den)
    w2 = p_group * (p2 / den)
    hit1 = row == i1
    hit2 = row == i2
    onehot = jnp.where(hit1 | hit2, 1.0, 0.0)
    r_i = lax.broadcasted_iota(jnp.int32, (tm, tm), 0)
    c_i = lax.broadcasted_iota(jnp.int32, (tm, tm), 1)
    earlier = jnp.where(r_i < c_i, 1.0, 0.0).astype(_BF16)
    before = _dot(onehot.astype(_BF16), earlier) + carry
    r1 = jnp.sum(jnp.where(hit1, before, 0.0), axis=0, keepdims=True)
    r2 = jnp.sum(jnp.where(hit2, before, 0.0), axis=0, keepdims=True)
    out_row = lax.broadcasted_iota(jnp.int32, (LANES, tm), 0)
    info = jnp.where(out_row == 0, w1, 0.0)
    info = jnp.where(out_row == 1, w2, info)
    info = jnp.where(out_row == 2, i1 - ROUTE_LANE0, info)
    info = jnp.where(out_row == 3, i2 - ROUTE_LANE0, info)
    info = jnp.where(out_row == 4, r1, info)
    info = jnp.where(out_row == 5, r2, info)
    rows_ref[...] = info[0:SUBLANES, :]
    info_ref[...] = info.T
    return carry + valid * jnp.sum(onehot, axis=1, keepdims=True)


def _out_router(x2, oa, ob, oc, w_out, ffn_g, w_rg, b_rg, w_re, b_re):
    t, d = x2.shape
    tm = min(t, 512)
    pad = LANES - N_GROUPS - N_EXPERTS
    wr = jnp.concatenate([w_rg, w_re, jnp.zeros((d, pad), _F32)], axis=1).T.astype(_BF16)
    br = jnp.concatenate([b_rg, b_re, jnp.zeros((ROUTE_ROWS - N_GROUPS - N_EXPERTS,), _F32)])
    br = jnp.broadcast_to(br[:, None], (ROUTE_ROWS, LANES))

    n = t // tm

    def src(w):
        return pl.BlockSpec((tm, w), lambda i: (jnp.minimum(i, n - 1), 0))

    def dst(rows, w):
        return pl.BlockSpec((rows, w), lambda i: (jnp.maximum(i - 1, 0), 0))

    def whole(r, c):
        return pl.BlockSpec((r, c), lambda i: (0, 0))

    return pl.pallas_call(
        _out_router_kernel,
        grid=(n + 1,),
        in_specs=[src(d), src(A_WIDTH), src(B_WIDTH), src(C_WIDTH), whole(d, d), whole(1, d),
                  whole(LANES, d), whole(ROUTE_ROWS, LANES)],
        out_specs=[dst(tm, d), dst(tm * PACK_ROWS, LANES), dst(tm, LANES),
                   pl.BlockSpec((SUBLANES, tm), lambda i: (0, jnp.maximum(i - 1, 0))), whole(ROUTE_ROWS, LANES)],
        out_shape=[jax.ShapeDtypeStruct((t, d), _F32), jax.ShapeDtypeStruct((t * PACK_ROWS, LANES), jnp.uint32),
                   jax.ShapeDtypeStruct((t, LANES), _F32), jax.ShapeDtypeStruct((SUBLANES, t), _F32),
                   jax.ShapeDtypeStruct((ROUTE_ROWS, LANES), _F32)],
        scratch_shapes=[pltpu.VMEM((ROUTE_ROWS, LANES), _F32), pltpu.VMEM((2, tm, d), _F32)],
        compiler_params=_cparams(1),
        name="out_router",
    )(x2, oa, ob, oc, w_out.astype(_BF16), ffn_g.reshape(1, d), wr, br)


DISPATCH_TOKENS = 1024
COMBINE_TOKENS = 512


ROW_UNROLL = 8


def _tile_rows(row, count=1, per=SUBLANES):
    start = row * per
    if not isinstance(start, int):
        start = pl.multiple_of(start, per)
    return pl.ds(start, count * per)


def _row_copy(src, s_row, dst, d_row, sem, per=SUBLANES):
    return pltpu.make_async_copy(src.at[_tile_rows(s_row, 1, per)], dst.at[_tile_rows(d_row, 1, per)], sem)


def _dispatch_kernel(pad_start_ref, pad_len_ref, used_ref, dest_ref, hn_ref, xs_ref, zero_ref, sem,
                     pad_sem):
    per = PACK_ROWS
    n = hn_ref.shape[0] // per

    @pl.when(pl.program_id(0) == 0)
    def _():
        zero_ref[...] = jnp.zeros_like(zero_ref)
        n_blocks = xs_ref.shape[0] // (ROW_BLOCK * per)

        def block_copy(blk):
            return pltpu.make_async_copy(zero_ref, xs_ref.at[_tile_rows(blk * ROW_BLOCK, ROW_BLOCK, per)],
                                         pad_sem)

        def put_block(blk, carry):
            block_copy(blk).start()
            return carry

        def done_block(blk, carry):
            block_copy(blk).wait()
            return carry

        lax.fori_loop(used_ref[0], n_blocks, put_block, 0)
        lax.fori_loop(used_ref[0], n_blocks, done_block, 0)
        bits = [1 << k for k in reversed(range(ROW_BLOCK.bit_length() - 1))]

        def tail(e, wait):
            row = pad_start_ref[e]
            for bit in bits:
                on = (pad_len_ref[e] & bit) != 0
                copy = pltpu.make_async_copy(zero_ref.at[_tile_rows(0, bit, per)],
                                             xs_ref.at[_tile_rows(row, bit, per)], pad_sem)

                @pl.when(on)
                def _():
                    copy.wait() if wait else copy.start()

                row = row + jnp.where(on, bit, 0)

        def put_tail(e, carry):
            tail(e, False)
            return carry

        def done_tail(e, carry):
            tail(e, True)
            return carry

        lax.fori_loop(0, N_EXPERTS, put_tail, 0)
        lax.fori_loop(0, N_EXPERTS, done_tail, 0)

    def issue(i, carry):
        for u in range(ROW_UNROLL):
            t = i * ROW_UNROLL + u
            _row_copy(hn_ref, t, xs_ref, dest_ref[2 * t], sem, per).start(priority=0)
            _row_copy(hn_ref, t, xs_ref, dest_ref[2 * t + 1], sem, per).start(priority=1)
        return carry

    lax.fori_loop(0, n // ROW_UNROLL, issue, 0)
    for _ in range(2):
        pltpu.make_async_copy(hn_ref, xs_ref.at[_tile_rows(0, n, per)], sem).wait()


def _dispatch(hn, dest, pad_start, pad_len, n_used, n_rows):
    t = hn.shape[0] // PACK_ROWS
    n = min(t, DISPATCH_TOKENS)
    return pl.pallas_call(
        _dispatch_kernel,
        grid_spec=pltpu.PrefetchScalarGridSpec(
            num_scalar_prefetch=3,
            grid=(t // n,),
            in_specs=[pl.BlockSpec((2 * n,), lambda i, *_: (i,), memory_space=pltpu.SMEM),
                      pl.BlockSpec((n * PACK_ROWS, LANES), lambda i, *_: (i, 0))],
            out_specs=pl.BlockSpec(memory_space=pl.ANY),
            scratch_shapes=[pltpu.VMEM((ROW_BLOCK * PACK_ROWS, LANES), hn.dtype), pltpu.SemaphoreType.DMA,
                            pltpu.SemaphoreType.DMA]),
        out_shape=jax.ShapeDtypeStruct((n_rows * PACK_ROWS, LANES), hn.dtype),
        compiler_params=_cparams(1),
        name="moe_dispatch",
    )(pad_start, pad_len, n_used, dest, hn)


def _expert_kernel(be_ref, used_ref, x_ref, wg_ref, wu_ref, wd_ref, y_ref, wg_bf, wu_bf, wd_bf):
    i = pl.program_id(0)
    live = i < used_ref[0]
    new_expert = (i == 0) | (be_ref[i] != be_ref[jnp.maximum(i - 1, 0)])

    @pl.when(live & new_expert)
    def _():
        wg_bf[...] = wg_ref[...].astype(_BF16)
        wu_bf[...] = wu_ref[...].astype(_BF16)
        wd_bf[...] = wd_ref[...].astype(_BF16)

    @pl.when(live)
    def _():
        sub = ROW_BLOCK // EXPERT_SPLIT
        gate_up = {}

        def first(k):
            x = _unpack_rows(x_ref, sub, k * sub)
            gate_up[k] = (_dot(x, wg_bf[...]), _dot(x, wu_bf[...]))

        def second(k):
            gate, up = gate_up.pop(k)
            act = (gate * jax.nn.sigmoid(gate)) * up
            _rows_to_tiles(y_ref, _dot(act.astype(_BF16), wd_bf[...]), k * sub)

        for k in range(min(EXPERT_AHEAD, EXPERT_SPLIT)):
            first(k)
        for k in range(EXPERT_SPLIT):
            if k + EXPERT_AHEAD < EXPERT_SPLIT:
                first(k + EXPERT_AHEAD)
            second(k)

    @pl.when(i >= used_ref[0])
    def _():
        y_ref[...] = jnp.zeros_like(y_ref)


def _experts(xs, block_e, n_used, w_gate, w_up, w_down):
    n_rows, d = xs.shape[0] // PACK_ROWS, D_MODEL
    n_blocks = n_rows // ROW_BLOCK
    tile_block = (ROW_BLOCK * SUBLANES, LANES)

    def xmap(i, be, used):
        return (jnp.minimum(i, used[0] - 1), 0)

    def wmap(i, be, used):
        return (be[jnp.minimum(i, used[0] - 1)], 0, 0)

    return pl.pallas_call(
        _expert_kernel,
        grid_spec=pltpu.PrefetchScalarGridSpec(
            num_scalar_prefetch=2,
            grid=(n_blocks,),
            in_specs=[pl.BlockSpec((ROW_BLOCK * PACK_ROWS, LANES), xmap),
                      pl.BlockSpec((None, d, D_EXPERT), wmap),
                      pl.BlockSpec((None, d, D_EXPERT), wmap),
                      pl.BlockSpec((None, D_EXPERT, d), wmap)],
            out_specs=pl.BlockSpec(tile_block, lambda i, be, used: (i, 0)),
            scratch_shapes=[pltpu.VMEM((d, D_EXPERT), _BF16), pltpu.VMEM((d, D_EXPERT), _BF16),
                            pltpu.VMEM((D_EXPERT, d), _BF16)]),
        out_shape=jax.ShapeDtypeStruct((n_rows * SUBLANES, LANES), _F32),
        compiler_params=_cparams(1),
        name="moe_experts",
    )(block_e, n_used, xs, w_gate, w_up, w_down)


def _combine_kernel(dest_ref, next_ref, h_ref, info_ref, ys_ref, o_ref, buf_ref, sem):
    n = h_ref.shape[0]
    step = pl.program_id(0)
    slot = step % 2

    def gather(idx_ref, to_slot):
        def issue(i, carry):
            for u in range(ROW_UNROLL):
                t = i * ROW_UNROLL + u
                _row_copy(ys_ref, idx_ref[2 * t], buf_ref.at[to_slot, 0], t,
                          sem.at[to_slot]).start(priority=0)
                _row_copy(ys_ref, idx_ref[2 * t + 1], buf_ref.at[to_slot, 1], t,
                          sem.at[to_slot]).start(priority=1)
            return carry

        lax.fori_loop(0, n // ROW_UNROLL, issue, 0)

    @pl.when(step == 0)
    def _():
        gather(dest_ref, 0)

    @pl.when(step + 1 < pl.num_programs(0))
    def _():
        gather(next_ref, 1 - slot)

    for k in range(2):
        pltpu.make_async_copy(ys_ref.at[_tile_rows(0, n)], buf_ref.at[slot, k], sem.at[slot]).wait()
    info = info_ref[...]
    w0 = info[:, 0:1]
    w1 = info[:, 1:2]
    for s in range(SUBLANES):
        sl = slice(s * LANES, (s + 1) * LANES)
        moe = w0 * _tile_block(buf_ref.at[slot, 0], s, n) + w1 * _tile_block(buf_ref.at[slot, 1], s, n)
        o_ref[:, sl] = h_ref[:, sl] + moe


def _combine(h, info, ys, dest):
    t, d = h.shape
    n = min(t, COMBINE_TOKENS)
    steps = t // n
    return pl.pallas_call(
        _combine_kernel,
        grid=(steps,),
        in_specs=[pl.BlockSpec((2 * n,), lambda i: (i,), memory_space=pltpu.SMEM),
                  pl.BlockSpec((2 * n,), lambda i: (jnp.minimum(i + 1, steps - 1),),
                               memory_space=pltpu.SMEM),
                  pl.BlockSpec((n, d), lambda i: (i, 0)),
                  pl.BlockSpec((n, LANES), lambda i: (i, 0)),
                  pl.BlockSpec(memory_space=pl.ANY)],
        out_specs=pl.BlockSpec((n, d), lambda i: (i, 0)),
        out_shape=jax.ShapeDtypeStruct((t, d), _F32),
        scratch_shapes=[pltpu.VMEM((2, 2, n * SUBLANES, LANES), _F32), pltpu.SemaphoreType.DMA((2,))],
        compiler_params=_cparams(1),
        name="moe_combine",
    )(dest, dest, h, info, ys)


def _moe_layout(route_rows, counts, t):
    counts = counts[ROUTE_LANE0:ROUTE_LANE0 + N_EXPERTS, 0].astype(jnp.int32)
    padded = (counts + ROW_BLOCK - 1) // ROW_BLOCK * ROW_BLOCK
    pends = jnp.cumsum(padded)
    pstarts = pends - padded
    eid = route_rows[2:4].astype(jnp.int32)
    rank = route_rows[4:6].astype(jnp.int32)
    experts = jnp.arange(N_EXPERTS, dtype=jnp.int32)
    start_of = jnp.sum(jnp.where(eid[:, :, None] == experts, pstarts, 0), axis=-1)
    dest = (start_of + rank).T.reshape(-1)
    n_blocks = -(-2 * t // ROW_BLOCK) + N_EXPERTS
    first_row = jnp.arange(n_blocks, dtype=jnp.int32) * ROW_BLOCK
    block_e = jnp.minimum(jnp.sum((pends[None, :] <= first_row[:, None]).astype(jnp.int32), axis=1),
                          N_EXPERTS - 1)
    n_used = (pends[-1:] // ROW_BLOCK).astype(jnp.int32)
    return dest, block_e, n_used, pstarts + counts, padded - counts, n_blocks * ROW_BLOCK


def kernel(x, mem, positions, mix_norm_g, w_in, qn_a, kn_a, rel_bias, ret_gn_g, mem_norm_g, w_mem_kv,
           qn_c, kn_c, w_out, ffn_norm_g, w_router_group, b_router_group, w_router_expert,
           b_router_expert, w_gate, w_up, w_down):
    b, s, d = x.shape
    t = b * s
    x2 = x.reshape(t, d)
    cos, sin = _rope_tables(positions)
    kc, vc = _mem_kv(mem, mem_norm_g, w_mem_kv, kn_c)
    proj3 = _in_proj(x2, mix_norm_g, w_in).reshape(b, s, IN_COLS)
    out_a = _attention(proj3, qn_a, kn_a, rel_bias)
    out_b = _retention(proj3, cos, sin, ret_gn_g)
    out_c = _cross_attention(proj3, kc, vc, qn_c)
    h, hn, info, route_rows, counts = _out_router(
        x2, out_a.reshape(t, A_WIDTH), out_b.reshape(t, B_WIDTH), out_c.reshape(t, C_WIDTH),
        w_out, ffn_norm_g, w_router_group, b_router_group, w_router_expert, b_router_expert)
    dest, block_e, n_used, pad_start, pad_len, n_rows = _moe_layout(route_rows, counts, t)
    xs = _dispatch(hn, dest, pad_start, pad_len, n_used, n_rows)
    ys = _experts(xs, block_e, n_used, w_gate, w_up, w_down)
    return _combine(h, info, ys, dest).reshape(b, s, d)
```

```python
import functools

import jax
import jax.numpy as jnp
from jax import lax
from jax.experimental import pallas as pl
from jax.experimental.pallas import tpu as pltpu

D_MODEL = 1024
CHUNK = 64
HEAD_DIM = 64
A_HEADS = 8
B_HEADS = 4
C_HEADS = 4
A_WIDTH = A_HEADS * HEAD_DIM
B_WIDTH = B_HEADS * HEAD_DIM
C_WIDTH = C_HEADS * HEAD_DIM
IN_COLS = 3 * A_WIDTH + 4 * B_WIDTH + C_WIDTH
LEFT_CHUNKS = 8
BAND_CHUNKS = LEFT_CHUNKS + 1
MAX_REL_DIST = 128
ROPE_BASE = 10000.0
N_GROUPS = 4
EXPERTS_PER_GROUP = 8
N_EXPERTS = N_GROUPS * EXPERTS_PER_GROUP
D_EXPERT = D_MODEL // 2
EPS = 1e-6
NEG_INF = -1e30
LOG2E = 1.4426950408889634

LANES = 128
SUBLANES = 8
assert D_MODEL == SUBLANES * LANES
PACK_ROWS = SUBLANES // 2
LEFT_ROWS = LEFT_CHUNKS * CHUNK
ATT_Q = 2 * CHUNK
ATT_K = ATT_Q + LEFT_ROWS
ATT_VARIANTS = LEFT_ROWS // ATT_Q + 1
ONES_ROWS = 16
ATT_AHEAD = 3
RET_CHUNK = 256
ROW_BLOCK = 512
EXPERT_SPLIT = 2
EXPERT_AHEAD = 2
ROUTE_LANE0 = N_GROUPS
ROUTE_ROWS = 64
VMEM_LIMIT = 48 * 1024 * 1024

_F32 = jnp.float32
_BF16 = jnp.bfloat16


def _cparams(n_axes):
    return pltpu.CompilerParams(dimension_semantics=("arbitrary",) * n_axes,
                                vmem_limit_bytes=VMEM_LIMIT)


def _dot(a, b):
    return jnp.dot(a, b, preferred_element_type=_F32)


def _dot_nt(a, b):
    return lax.dot_general(a, b, (((1,), (1,)), ((), ())), preferred_element_type=_F32)


def _lane(shape):
    return lax.broadcasted_iota(jnp.int32, shape, len(shape) - 1)


def _pair_rms(t, gain):
    low = _lane(t.shape) < HEAD_DIM
    t2 = t * t
    ms0 = jnp.sum(jnp.where(low, t2, 0.0), axis=-1, keepdims=True) * (1.0 / HEAD_DIM)
    ms1 = jnp.sum(jnp.where(low, 0.0, t2), axis=-1, keepdims=True) * (1.0 / HEAD_DIM)
    r = jnp.where(low, lax.rsqrt(ms0 + EPS), lax.rsqrt(ms1 + EPS))
    return (t * r) * gain


def _rows_to_tiles(ref, val, row0=0):
    n = val.shape[0]
    for s in range(SUBLANES):
        ref[pl.ds(row0 * SUBLANES + s, n, stride=SUBLANES), :] = val[:, s * LANES:(s + 1) * LANES]


def _tile_block(ref, s, n, row0=0):
    return ref[pl.ds(row0 * SUBLANES + s, n, stride=SUBLANES), :]


def _tiles_to_rows(ref, n, row0=0):
    return jnp.concatenate([_tile_block(ref, s, n, row0) for s in range(SUBLANES)], axis=-1)


def _pack_rows(ref, val, row0=0):
    n = val.shape[0]
    for s in range(PACK_ROWS):
        lo = val[:, (2 * s) * LANES:(2 * s + 1) * LANES].astype(_BF16).astype(_F32)
        hi = val[:, (2 * s + 1) * LANES:(2 * s + 2) * LANES].astype(_BF16).astype(_F32)
        word = (lax.bitcast_convert_type(lo, jnp.uint32) >> 16) | (
            lax.bitcast_convert_type(hi, jnp.uint32) & jnp.uint32(0xFFFF0000))
        ref[pl.ds(row0 * PACK_ROWS + s, n, stride=PACK_ROWS), :] = word


def _unpack_rows(ref, n, row0=0):
    parts = []
    for s in range(PACK_ROWS):
        word = ref[pl.ds(row0 * PACK_ROWS + s, n, stride=PACK_ROWS), :]
        parts.append(lax.bitcast_convert_type(word << 16, _F32))
        parts.append(lax.bitcast_convert_type(word & jnp.uint32(0xFFFF0000), _F32))
    return jnp.concatenate(parts, axis=-1).astype(_BF16)


ROPE_HALF = HEAD_DIM // 2
ROPE_PACK = LANES // ROPE_HALF


def _rope_kernel(pos_ref, inv_ref, cos_ref, sin_ref):
    ang = pos_ref[...].astype(_F32) * inv_ref[...]
    rows = ang.shape[0]
    lane = _lane(ang.shape)
    sign = jnp.where((lane % HEAD_DIM) < ROPE_HALF, -1.0, 1.0)
    for out_ref, val in ((cos_ref, jnp.cos(ang)), (sin_ref, jnp.sin(ang))):
        for j in range(ROPE_PACK):
            seg = jnp.where(lane // ROPE_HALF == j, val, 0.0)
            full = seg
            for k in range(1, ROPE_PACK):
                full = full + pltpu.roll(seg, k * ROPE_HALF, 1)
            if out_ref is sin_ref:
                full = full * sign
            out_ref[pl.ds(j, rows, stride=ROPE_PACK), :] = full


def _rope_tables(positions):
    t = positions.size
    inv = ROPE_BASE ** (-jnp.arange(ROPE_HALF, dtype=_F32) / ROPE_HALF)
    inv128 = jnp.tile(inv, ROPE_PACK).reshape(1, LANES)
    rows = t // ROPE_PACK
    pos = jnp.repeat(positions.reshape(rows, ROPE_PACK), ROPE_HALF, axis=1)
    tm = min(rows, 512)
    out = pl.BlockSpec((tm * ROPE_PACK, LANES), lambda i: (i, 0))
    return pl.pallas_call(
        _rope_kernel,
        grid=(rows // tm,),
        in_specs=[pl.BlockSpec((tm, LANES), lambda i: (i, 0)), pl.BlockSpec((1, LANES), lambda i: (0, 0))],
        out_specs=[out, out],
        out_shape=[jax.ShapeDtypeStruct((t, LANES), _F32)] * 2,
        compiler_params=_cparams(1),
        name="rope_tables",
    )(pos, inv128)


def _mem_kv_kernel(mem_ref, g_ref, w_ref, kn_ref, k_ref, v_ref):
    m = mem_ref[...]
    ms = jnp.mean(m * m, axis=-1, keepdims=True)
    mn = (m * lax.rsqrt(ms + EPS)) * g_ref[...]
    kv = _dot(mn.astype(_BF16), w_ref[...])
    for j in range(C_WIDTH // LANES):
        sl = slice(j * LANES, (j + 1) * LANES)
        k_ref[:, sl] = _pair_rms(kv[:, sl], kn_ref[...]).astype(_BF16)
    v_ref[...] = kv[:, C_WIDTH:].T.astype(_BF16)


def _mem_kv(mem, mem_norm_g, w_mem_kv, kn_c):
    b, m, d = mem.shape
    kn = jnp.tile(kn_c, 2).reshape(1, LANES)
    return pl.pallas_call(
        _mem_kv_kernel,
        grid=(b,),
        in_specs=[pl.BlockSpec((None, m, d), lambda i: (i, 0, 0)),
                  pl.BlockSpec((1, d), lambda i: (0, 0)),
                  pl.BlockSpec((d, 2 * C_WIDTH), lambda i: (0, 0)),
                  pl.BlockSpec((1, LANES), lambda i: (0, 0))],
        out_specs=[pl.BlockSpec((None, m, C_WIDTH), lambda i: (i, 0, 0)),
                   pl.BlockSpec((None, C_WIDTH, m), lambda i: (i, 0, 0))],
        out_shape=[jax.ShapeDtypeStruct((b, m, C_WIDTH), _BF16),
                   jax.ShapeDtypeStruct((b, C_WIDTH, m), _BF16)],
        compiler_params=_cparams(1),
        name="mem_kv",
    )(mem, mem_norm_g.reshape(1, d), w_mem_kv.astype(_BF16), kn)


def _in_proj_kernel(x_ref, g_ref, wq_ref, wk_ref, wvt_ref, wr_ref, wc_ref, qn_ref, kn_ref, cn_ref,
                    qa_ref, ka_ref, vt_ref, ret_ref, qc_ref, xn_ref, acc_ref):
    x = x_ref[...]
    ms = jnp.mean(x * x, axis=-1, keepdims=True)
    xn_ref[...] = ((x * lax.rsqrt(ms + EPS)) * g_ref[...]).astype(_BF16)

    def normed(slot, out_ref, gain_ref):
        for blk in range(out_ref.shape[1] // LANES):
            sl = slice(blk * LANES, (blk + 1) * LANES)
            out_ref[:, sl] = _pair_rms(acc_ref[slot, :, sl], gain_ref[...]).astype(_BF16)

    acc_ref[0] = _dot(xn_ref[...], wq_ref[...])
    acc_ref[1] = _dot(xn_ref[...], wk_ref[...])
    normed(0, qa_ref, qn_ref)
    acc_ref[0] = _dot_nt(wvt_ref[...], xn_ref[...])
    normed(1, ka_ref, kn_ref)
    ret_ref[...] = _dot(xn_ref[...], wr_ref[...])
    for blk in range(vt_ref.shape[0]):
        vt_ref[blk] = acc_ref[0, :, blk * LANES:(blk + 1) * LANES].astype(_BF16)
    acc_ref[1, :, 0:C_WIDTH] = _dot(xn_ref[...], wc_ref[...])
    normed(1, qc_ref, cn_ref)


def _in_proj(x3, g, w_in, qn_a, kn_a, qn_c):
    b, s, d = x3.shape
    tm = min(s, 512)
    assert tm == A_WIDTH
    w = w_in.astype(_BF16)
    cuts = [0, A_WIDTH, 2 * A_WIDTH, 3 * A_WIDTH, 3 * A_WIDTH + 4 * B_WIDTH, IN_COLS]
    wq, wk, wv, wr, wc = (w[:, lo:hi] for lo, hi in zip(cuts[:-1], cuts[1:]))
    scale = HEAD_DIM ** -0.5 * LOG2E
    gains = [(jnp.tile(gn, 2) * sc).reshape(1, LANES) for gn, sc in ((qn_a, scale), (kn_a, 1.0), (qn_c, scale))]

    def whole(arr):
        return pl.BlockSpec(arr.shape, lambda i, j: (0,) * arr.ndim)

    def rows(width):
        return pl.BlockSpec((None, tm, width), lambda i, j: (i, j, 0))

    consts = [g.reshape(1, d), wq, wk, wv.T, wr, wc] + gains
    return pl.pallas_call(
        _in_proj_kernel,
        grid=(b, s // tm),
        in_specs=[rows(d)] + [whole(c) for c in consts],
        out_specs=[rows(A_WIDTH), rows(A_WIDTH),
                   pl.BlockSpec((None, tm // LANES, A_WIDTH, LANES), lambda i, j: (i, j, 0, 0)),
                   rows(4 * B_WIDTH), rows(C_WIDTH)],
        out_shape=[jax.ShapeDtypeStruct((b, s, A_WIDTH), _BF16), jax.ShapeDtypeStruct((b, s, A_WIDTH), _BF16),
                   jax.ShapeDtypeStruct((b, s // LANES, A_WIDTH, LANES), _BF16),
                   jax.ShapeDtypeStruct((b, s, 4 * B_WIDTH), _F32), jax.ShapeDtypeStruct((b, s, C_WIDTH), _BF16)],
        scratch_shapes=[pltpu.VMEM((tm, d), _BF16), pltpu.VMEM((2, tm, A_WIDTH), _F32)],
        compiler_params=_cparams(2),
        name="in_proj",
    )(x3, *consts)


def _attn_kernel(q_ref, k_ref, vt_ref, bias_ref, o_ref, kp_ref, st_ref, var_ref, *, q_rows):
    qs = pl.program_id(2)
    s = k_ref.shape[0]
    fill_rows = min(s, 1024)
    left_blocks = LEFT_ROWS // LANES

    @pl.when(qs == 0)
    def _():
        kp_ref[0:LEFT_ROWS, :] = jnp.zeros((LEFT_ROWS, LANES), _BF16)

        def fill(i, carry):
            r = pl.multiple_of(i * fill_rows, fill_rows)
            kp_ref[pl.ds(LEFT_ROWS + r, fill_rows), :] = k_ref[pl.ds(r, fill_rows), :]
            return carry

        lax.fori_loop(0, s // fill_rows, fill, 0)
        key = lax.broadcasted_iota(jnp.int32, (ATT_K, 2 * ATT_Q), 0)
        for v in range(ATT_VARIANTS):
            var_ref[v] = jnp.where(key >= LEFT_ROWS - ATT_Q * v, bias_ref[...], NEG_INF)

    low = _lane((ATT_Q, LANES)) < HEAD_DIM
    ones = jnp.ones((ONES_ROWS, ATT_K), _BF16)
    tiles_per_step = q_rows // ATT_Q

    def scores(j):
        cp = qs * tiles_per_step + j
        q = q_ref[j * ATT_Q:(j + 1) * ATT_Q, :]
        q2 = jnp.concatenate([jnp.where(low, q, jnp.zeros_like(q)), jnp.where(low, jnp.zeros_like(q), q)], axis=0)
        kb = kp_ref[pl.ds(pl.multiple_of(cp * ATT_Q, ATT_Q), ATT_K), :]
        st_ref[j % (ATT_AHEAD + 1)] = _dot_nt(kb, q2) + var_ref[jnp.minimum(cp, ATT_VARIANTS - 1)]

    def finish(j):
        cp = qs * tiles_per_step + j
        st = st_ref[j % (ATT_AHEAD + 1)]
        m = jnp.max(st, axis=0, keepdims=True)
        p = jnp.exp2(st - m)
        vt = jnp.concatenate([vt_ref[jnp.maximum(cp + kb_i - left_blocks, 0)] for kb_i in range(ATT_K // LANES)],
                             axis=1)
        ot = _dot(jnp.concatenate([vt, ones], axis=0), p.astype(_BF16))
        inv = 1.0 / ot[LANES:LANES + 1, :]
        out_t = jnp.concatenate([ot[0:HEAD_DIM, 0:ATT_Q] * inv[:, 0:ATT_Q],
                                 ot[HEAD_DIM:LANES, ATT_Q:] * inv[:, ATT_Q:]], axis=0)
        o_ref[j * ATT_Q:(j + 1) * ATT_Q, :] = out_t.T.astype(o_ref.dtype)

    for j in range(min(ATT_AHEAD, tiles_per_step)):
        scores(j)
    for j in range(tiles_per_step):
        if j + ATT_AHEAD < tiles_per_step:
            scores(j + ATT_AHEAD)
        finish(j)


def _toeplitz_bias(rel_bias, q_len, k_len):
    h, table = rel_bias.shape
    n_diag = q_len + k_len - 1
    flat_lo = k_len - 1 - LEFT_ROWS - (CHUNK - 1)
    flat_hi = n_diag - flat_lo - table
    rev = jnp.concatenate([jnp.broadcast_to(rel_bias[:, -1:], (h, flat_hi)), rel_bias[:, ::-1],
                           jnp.broadcast_to(rel_bias[:, :1], (h, flat_lo))], axis=1).astype(_F32)
    flat = jnp.tile(rev, (1, q_len + 1))
    pitch = n_diag - 1
    skew = flat[:, q_len - 1:q_len - 1 + q_len * pitch].reshape(h, q_len, pitch)
    return skew[:, :, :k_len]


def _attn_bias(rel_bias):
    h = rel_bias.shape[0]
    bias = _toeplitz_bias(rel_bias, ATT_Q, ATT_K)
    q = lax.broadcasted_iota(jnp.int32, (ATT_Q, ATT_K), 0)
    k = lax.broadcasted_iota(jnp.int32, (ATT_Q, ATT_K), 1)
    off = k // CHUNK - q // CHUNK
    in_band = (off >= 0) & (off < BAND_CHUNKS)
    full = jnp.where(in_band[None], bias * LOG2E, NEG_INF)
    full = full.reshape(h // 2, 2, ATT_Q, ATT_K)
    return full.transpose(0, 3, 1, 2).reshape(h // 2, ATT_K, 2 * ATT_Q)


def _attention(qa, ka, vta, rel_bias):
    b, s, _ = qa.shape
    q_rows = min(s, 2048)
    pairs = A_HEADS // 2
    return pl.pallas_call(
        functools.partial(_attn_kernel, q_rows=q_rows),
        grid=(b, pairs, s // q_rows),
        in_specs=[pl.BlockSpec((None, q_rows, LANES), lambda i, p, j: (i, j, p)),
                  pl.BlockSpec((None, s, LANES), lambda i, p, j: (i, 0, p)),
                  pl.BlockSpec((None, s // LANES, LANES, LANES), lambda i, p, j: (i, 0, p, 0)),
                  pl.BlockSpec((None, ATT_K, 2 * ATT_Q), lambda i, p, j: (p, 0, 0))],
        out_specs=pl.BlockSpec((None, q_rows, LANES), lambda i, p, j: (i, j, p)),
        out_shape=jax.ShapeDtypeStruct((b, s, A_WIDTH), _BF16),
        scratch_shapes=[pltpu.VMEM((s + LEFT_ROWS, LANES), _BF16),
                        pltpu.VMEM((ATT_AHEAD + 1, ATT_K, 2 * ATT_Q), _F32),
                        pltpu.VMEM((ATT_VARIANTS, ATT_K, 2 * ATT_Q), _F32)],
        compiler_params=_cparams(3),
        name="attn_a",
    )(qa, ka, vta, _attn_bias(rel_bias))


def _swap_halves(t):
    first = (_lane(t.shape) % HEAD_DIM) < (HEAD_DIM // 2)
    return jnp.where(first, pltpu.roll(t, LANES - HEAD_DIM // 2, 1), pltpu.roll(t, HEAD_DIM // 2, 1))


def _retention_kernel(q_ref, k_ref, v_ref, gate_ref, cos_ref, sin_ref, decay_ref, zeta_ref, xi_ref,
                      cd_ref, gn_ref, o_ref, state_ref, *, rows):
    @pl.when(pl.program_id(2) == 0)
    def _():
        state_ref[...] = jnp.zeros_like(state_ref)

    c = RET_CHUNK
    low = _lane((c, LANES)) < HEAD_DIM
    eye = jnp.where(lax.broadcasted_iota(jnp.int32, (LANES, LANES), 0) == _lane((LANES, LANES)),
                    1.0, 0.0).astype(_BF16)
    srow = lax.broadcasted_iota(jnp.int32, (LANES, LANES), 0) < HEAD_DIM
    scol = _lane((LANES, LANES)) < HEAD_DIM
    same_head = srow == scol

    for j in range(rows // c):
        sl = slice(j * c, (j + 1) * c)
        cos, sin = cos_ref[sl, :], sin_ref[sl, :]
        q = q_ref[sl, :]
        k = k_ref[sl, :]
        qr = q * cos + _swap_halves(q) * sin
        kr = (k * cos + _swap_halves(k) * sin) * (HEAD_DIM ** -0.5)
        vb = v_ref[sl, :].astype(_BF16)
        qb = qr.astype(_BF16)
        kb = kr.astype(_BF16)
        inner_out = []
        for h in range(2):
            qh = jnp.where(low if h == 0 else ~low, qr, 0.0).astype(_BF16)
            inner = _dot_nt(qh, kb) * decay_ref[h]
            inner_out.append(_dot(inner.astype(_BF16), vb))
        state = state_ref[...]
        cross = _dot(qb, state.astype(_BF16)) * xi_ref[...]
        o = jnp.where(low, inner_out[0], inner_out[1]) + cross
        kz = _dot_nt(eye, (kr * zeta_ref[...]).astype(_BF16)).astype(_BF16)
        state_ref[...] = cd_ref[...] * state + jnp.where(same_head, _dot(kz, vb), 0.0)
        mu = jnp.where(low,
                       jnp.sum(jnp.where(low, o, 0.0), axis=-1, keepdims=True),
                       jnp.sum(jnp.where(low, 0.0, o), axis=-1, keepdims=True)) * (1.0 / HEAD_DIM)
        dlt = o - mu
        d2 = dlt * dlt
        var = jnp.where(low,
                        jnp.sum(jnp.where(low, d2, 0.0), axis=-1, keepdims=True),
                        jnp.sum(jnp.where(low, 0.0, d2), axis=-1, keepdims=True)) * (1.0 / HEAD_DIM)
        y = (dlt * lax.rsqrt(var + EPS)) * gn_ref[...]
        g = gate_ref[sl, :]
        o_ref[sl, :] = ((g * jax.nn.sigmoid(g)) * y).astype(o_ref.dtype)


def _retention_tables():
    c = RET_CHUNK
    log_g = jnp.log(1.0 - jnp.exp2(-5.0 - jnp.arange(B_HEADS, dtype=_F32)))
    idx = jnp.arange(c, dtype=_F32)
    diff = idx[:, None] - idx[None, :]
    decay = jnp.where(diff >= 0, jnp.exp(log_g[:, None, None] * jnp.maximum(diff, 0.0)), 0.0)
    zeta = jnp.exp(log_g[:, None] * (c - 1 - idx))
    xi = jnp.exp(log_g[:, None] * (idx + 1.0))
    cd = jnp.exp(log_g * c)

    def lanes(tab):
        return jnp.repeat(tab.reshape(B_HEADS // 2, 2, c), HEAD_DIM, axis=1).transpose(0, 2, 1)

    cdm = jnp.repeat(cd.reshape(B_HEADS // 2, 2), HEAD_DIM, axis=1)
    cdm = jnp.broadcast_to(cdm[:, :, None], (B_HEADS // 2, LANES, LANES))
    return decay, lanes(zeta), lanes(xi), cdm


def _retention(proj3, cos, sin, ret_gn_g):
    b, s, _ = proj3.shape
    rows = min(s, 2048)
    pairs = B_HEADS // 2
    base = 0
    decay, zeta, xi, cdm = _retention_tables()
    cos3, sin3 = cos.reshape(b, s, LANES), sin.reshape(b, s, LANES)
    gn = ret_gn_g.reshape(pairs, 1, LANES)

    def col(off):
        return pl.BlockSpec((None, rows, LANES), lambda i, p, j: (i, j, base + off * pairs + p))

    tab = pl.BlockSpec((None, rows, LANES), lambda i, p, j: (i, j, 0))
    return pl.pallas_call(
        functools.partial(_retention_kernel, rows=rows),
        grid=(b, pairs, s // rows),
        in_specs=[col(0), col(1), col(2), col(3), tab, tab,
                  pl.BlockSpec((2, RET_CHUNK, RET_CHUNK), lambda i, p, j: (p, 0, 0)),
                  pl.BlockSpec((None, RET_CHUNK, LANES), lambda i, p, j: (p, 0, 0)),
                  pl.BlockSpec((None, RET_CHUNK, LANES), lambda i, p, j: (p, 0, 0)),
                  pl.BlockSpec((None, LANES, LANES), lambda i, p, j: (p, 0, 0)),
                  pl.BlockSpec((None, 1, LANES), lambda i, p, j: (p, 0, 0))],
        out_specs=pl.BlockSpec((None, rows, LANES), lambda i, p, j: (i, j, p)),
        out_shape=jax.ShapeDtypeStruct((b, s, B_WIDTH), _BF16),
        scratch_shapes=[pltpu.VMEM((LANES, LANES), _F32)],
        compiler_params=_cparams(3),
        name="retention_b",
    )(proj3, proj3, proj3, proj3, cos3, sin3, decay, zeta, xi, cdm, gn)


def _cross_kernel(q_ref, k_ref, vt_ref, o_ref, st_ref, *, rows):
    low = _lane((ATT_Q, LANES)) < HEAD_DIM
    lane_blocks = C_WIDTH // LANES
    tiles = [(j, lb) for j in range(rows // ATT_Q) for lb in range(lane_blocks)]
    ones = jnp.ones((ONES_ROWS, vt_ref.shape[1]), _BF16)

    def scores(i):
        j, lb = tiles[i]
        sl = slice(lb * LANES, (lb + 1) * LANES)
        q = q_ref[j * ATT_Q:(j + 1) * ATT_Q, sl]
        q2 = jnp.concatenate([jnp.where(low, q, jnp.zeros_like(q)), jnp.where(low, jnp.zeros_like(q), q)], axis=0)
        st_ref[i % (ATT_AHEAD + 1)] = _dot_nt(k_ref[:, sl], q2)

    def finish(i):
        j, lb = tiles[i]
        sl = slice(lb * LANES, (lb + 1) * LANES)
        st = st_ref[i % (ATT_AHEAD + 1)]
        p = jnp.exp2(st - jnp.max(st, axis=0, keepdims=True))
        ot = _dot(jnp.concatenate([vt_ref[sl, :], ones], axis=0), p.astype(_BF16))
        inv = 1.0 / ot[LANES:LANES + 1, :]
        out_t = jnp.concatenate([ot[0:HEAD_DIM, 0:ATT_Q] * inv[:, 0:ATT_Q],
                                 ot[HEAD_DIM:LANES, ATT_Q:] * inv[:, ATT_Q:]], axis=0)
        o_ref[j * ATT_Q:(j + 1) * ATT_Q, sl] = out_t.T.astype(o_ref.dtype)

    for i in range(min(ATT_AHEAD, len(tiles))):
        scores(i)
    for i in range(len(tiles)):
        if i + ATT_AHEAD < len(tiles):
            scores(i + ATT_AHEAD)
        finish(i)


def _cross_attention(qc, kc, vtc):
    b, s, _ = qc.shape
    m = kc.shape[1]
    rows = min(s, 1024)
    return pl.pallas_call(
        functools.partial(_cross_kernel, rows=rows),
        grid=(b, s // rows),
        in_specs=[pl.BlockSpec((None, rows, C_WIDTH), lambda i, j: (i, j, 0)),
                  pl.BlockSpec((None, m, C_WIDTH), lambda i, j: (i, 0, 0)),
                  pl.BlockSpec((None, C_WIDTH, m), lambda i, j: (i, 0, 0))],
        out_specs=pl.BlockSpec((None, rows, C_WIDTH), lambda i, j: (i, j, 0)),
        out_shape=jax.ShapeDtypeStruct((b, s, C_WIDTH), _BF16),
        scratch_shapes=[pltpu.VMEM((ATT_AHEAD + 1, m, 2 * ATT_Q), _F32)],
        compiler_params=_cparams(2),
        name="cross_c",
    )(qc, kc, vtc)


def _out_router_kernel(x_ref, a_ref, b_ref, c_ref, wo_ref, g_ref, wr_ref, br_ref,
                       h_ref, hn_ref, info_ref, rows_ref, cnt_ref, carry_ref):
    @pl.when(pl.program_id(0) == 0)
    def _():
        carry_ref[...] = jnp.zeros_like(carry_ref)

    tm = x_ref.shape[0]
    h = x_ref[...]
    h = h + _dot(a_ref[...], wo_ref[0:A_WIDTH, :])
    h = h + _dot(b_ref[...], wo_ref[A_WIDTH:A_WIDTH + B_WIDTH, :])
    h = h + _dot(c_ref[...], wo_ref[A_WIDTH + B_WIDTH:, :])
    h_ref[...] = h
    ms = jnp.mean(h * h, axis=-1, keepdims=True)
    hn = (h * lax.rsqrt(ms + EPS)) * g_ref[...]
    _pack_rows(hn_ref, hn)
    logits = _dot_nt(wr_ref[...], hn.astype(_BF16))[0:ROUTE_ROWS, :] + br_ref[:, 0:1]
    row = lax.broadcasted_iota(jnp.int32, (ROUTE_ROWS, tm), 0).astype(_F32)
    big = float(ROUTE_ROWS)

    def first_row(mask):
        return jnp.min(jnp.where(mask, row, big), axis=0, keepdims=True)

    gmask = row < N_GROUPS
    gl = jnp.where(gmask, logits, NEG_INF)
    ge = jnp.exp(gl - jnp.max(gl, axis=0, keepdims=True))
    gp = ge / jnp.sum(ge, axis=0, keepdims=True)
    p_group = jnp.max(gp, axis=0, keepdims=True)
    g_sel = first_row(gmask & (gp == p_group))
    lo = ROUTE_LANE0 + g_sel * EXPERTS_PER_GROUP
    emask = (row >= lo) & (row < lo + EXPERTS_PER_GROUP)
    el = jnp.where(emask, logits, NEG_INF)
    ee = jnp.exp(el - jnp.max(el, axis=0, keepdims=True))
    ep = ee / jnp.sum(ee, axis=0, keepdims=True)
    p1 = jnp.max(ep, axis=0, keepdims=True)
    i1 = first_row(emask & (ep == p1))
    ep2 = jnp.where(emask & (row != i1), ep, -1.0)
    p2 = jnp.max(ep2, axis=0, keepdims=True)
    i2 = first_row(ep2 == p2)
    den = p1 + p2
    w1 = p_group * (p1 / den)
    w2 = p_group * (p2 / den)
    hit1 = row == i1
    hit2 = row == i2
    onehot = jnp.where(hit1 | hit2, 1.0, 0.0)
    r_i = lax.broadcasted_iota(jnp.int32, (tm, tm), 0)
    c_i = lax.broadcasted_iota(jnp.int32, (tm, tm), 1)
    earlier = jnp.where(r_i < c_i, 1.0, 0.0).astype(_BF16)
    before = _dot(onehot.astype(_BF16), earlier) + carry_ref[:, 0:1]
    r1 = jnp.sum(jnp.where(hit1, before, 0.0), axis=0, keepdims=True)
    r2 = jnp.sum(jnp.where(hit2, before, 0.0), axis=0, keepdims=True)
    carry_ref[...] = carry_ref[...] + jnp.sum(onehot, axis=1, keepdims=True)
    cnt_ref[...] = carry_ref[...]
    out_row = lax.broadcasted_iota(jnp.int32, (LANES, tm), 0)
    info = jnp.where(out_row == 0, w1, 0.0)
    info = jnp.where(out_row == 1, w2, info)
    info = jnp.where(out_row == 2, i1 - ROUTE_LANE0, info)
    info = jnp.where(out_row == 3, i2 - ROUTE_LANE0, info)
    info = jnp.where(out_row == 4, r1, info)
    info = jnp.where(out_row == 5, r2, info)
    rows_ref[...] = info[0:SUBLANES, :]
    info_ref[...] = info.T


def _out_router(x2, oa, ob, oc, w_out, ffn_g, w_rg, b_rg, w_re, b_re):
    t, d = x2.shape
    tm = min(t, 512)
    pad = LANES - N_GROUPS - N_EXPERTS
    wr = jnp.concatenate([w_rg, w_re, jnp.zeros((d, pad), _F32)], axis=1).T.astype(_BF16)
    br = jnp.concatenate([b_rg, b_re, jnp.zeros((ROUTE_ROWS - N_GROUPS - N_EXPERTS,), _F32)])
    br = jnp.broadcast_to(br[:, None], (ROUTE_ROWS, LANES))

    def rows(w):
        return pl.BlockSpec((tm, w), lambda i: (i, 0))

    def whole(r, c):
        return pl.BlockSpec((r, c), lambda i: (0, 0))

    return pl.pallas_call(
        _out_router_kernel,
        grid=(t // tm,),
        in_specs=[rows(d), rows(A_WIDTH), rows(B_WIDTH), rows(C_WIDTH), whole(d, d), whole(1, d),
                  whole(LANES, d), whole(ROUTE_ROWS, LANES)],
        out_specs=[rows(d), pl.BlockSpec((tm * PACK_ROWS, LANES), lambda i: (i, 0)), rows(LANES),
                   pl.BlockSpec((SUBLANES, tm), lambda i: (0, i)), whole(ROUTE_ROWS, LANES)],
        out_shape=[jax.ShapeDtypeStruct((t, d), _F32), jax.ShapeDtypeStruct((t * PACK_ROWS, LANES), jnp.uint32),
                   jax.ShapeDtypeStruct((t, LANES), _F32), jax.ShapeDtypeStruct((SUBLANES, t), _F32),
                   jax.ShapeDtypeStruct((ROUTE_ROWS, LANES), _F32)],
        scratch_shapes=[pltpu.VMEM((ROUTE_ROWS, LANES), _F32)],
        compiler_params=_cparams(1),
        name="out_router",
    )(x2, oa, ob, oc, w_out.astype(_BF16), ffn_g.reshape(1, d), wr, br)


DISPATCH_TOKENS = 1024
COMBINE_TOKENS = 512


ROW_UNROLL = 8


def _tile_rows(row, count=1, per=SUBLANES):
    start = row * per
    if not isinstance(start, int):
        start = pl.multiple_of(start, per)
    return pl.ds(start, count * per)


def _row_copy(src, s_row, dst, d_row, sem, per=SUBLANES):
    return pltpu.make_async_copy(src.at[_tile_rows(s_row, 1, per)], dst.at[_tile_rows(d_row, 1, per)], sem)


def _dispatch_kernel(pad_start_ref, pad_len_ref, used_ref, dest_ref, hn_ref, xs_ref, zero_ref, sem,
                     pad_sem):
    per = PACK_ROWS
    n = hn_ref.shape[0] // per

    @pl.when(pl.program_id(0) == 0)
    def _():
        zero_ref[...] = jnp.zeros_like(zero_ref)
        n_blocks = xs_ref.shape[0] // (ROW_BLOCK * per)

        def block_copy(blk):
            return pltpu.make_async_copy(zero_ref, xs_ref.at[_tile_rows(blk * ROW_BLOCK, ROW_BLOCK, per)],
                                         pad_sem)

        def put_block(blk, carry):
            block_copy(blk).start()
            return carry

        def done_block(blk, carry):
            block_copy(blk).wait()
            return carry

        lax.fori_loop(used_ref[0], n_blocks, put_block, 0)
        lax.fori_loop(used_ref[0], n_blocks, done_block, 0)
        bits = [1 << k for k in reversed(range(ROW_BLOCK.bit_length() - 1))]

        def tail(e, wait):
            row = pad_start_ref[e]
            for bit in bits:
                on = (pad_len_ref[e] & bit) != 0
                copy = pltpu.make_async_copy(zero_ref.at[_tile_rows(0, bit, per)],
                                             xs_ref.at[_tile_rows(row, bit, per)], pad_sem)

                @pl.when(on)
                def _():
                    copy.wait() if wait else copy.start()

                row = row + jnp.where(on, bit, 0)

        def put_tail(e, carry):
            tail(e, False)
            return carry

        def done_tail(e, carry):
            tail(e, True)
            return carry

        lax.fori_loop(0, N_EXPERTS, put_tail, 0)
        lax.fori_loop(0, N_EXPERTS, done_tail, 0)

    def issue(i, carry):
        for u in range(ROW_UNROLL):
            t = i * ROW_UNROLL + u
            _row_copy(hn_ref, t, xs_ref, dest_ref[2 * t], sem, per).start(priority=0)
            _row_copy(hn_ref, t, xs_ref, dest_ref[2 * t + 1], sem, per).start(priority=1)
        return carry

    lax.fori_loop(0, n // ROW_UNROLL, issue, 0)
    for _ in range(2):
        pltpu.make_async_copy(hn_ref, xs_ref.at[_tile_rows(0, n, per)], sem).wait()


def _dispatch(hn, dest, pad_start, pad_len, n_used, n_rows):
    t = hn.shape[0] // PACK_ROWS
    n = min(t, DISPATCH_TOKENS)
    return pl.pallas_call(
        _dispatch_kernel,
        grid_spec=pltpu.PrefetchScalarGridSpec(
            num_scalar_prefetch=3,
            grid=(t // n,),
            in_specs=[pl.BlockSpec((2 * n,), lambda i, *_: (i,), memory_space=pltpu.SMEM),
                      pl.BlockSpec((n * PACK_ROWS, LANES), lambda i, *_: (i, 0))],
            out_specs=pl.BlockSpec(memory_space=pl.ANY),
            scratch_shapes=[pltpu.VMEM((ROW_BLOCK * PACK_ROWS, LANES), hn.dtype), pltpu.SemaphoreType.DMA,
                            pltpu.SemaphoreType.DMA]),
        out_shape=jax.ShapeDtypeStruct((n_rows * PACK_ROWS, LANES), hn.dtype),
        compiler_params=_cparams(1),
        name="moe_dispatch",
    )(pad_start, pad_len, n_used, dest, hn)


def _expert_kernel(be_ref, used_ref, x_ref, wg_ref, wu_ref, wd_ref, y_ref, wg_bf, wu_bf, wd_bf):
    i = pl.program_id(0)
    live = i < used_ref[0]
    new_expert = (i == 0) | (be_ref[i] != be_ref[jnp.maximum(i - 1, 0)])

    @pl.when(live & new_expert)
    def _():
        wg_bf[...] = wg_ref[...].astype(_BF16)
        wu_bf[...] = wu_ref[...].astype(_BF16)
        wd_bf[...] = wd_ref[...].astype(_BF16)

    @pl.when(live)
    def _():
        sub = ROW_BLOCK // EXPERT_SPLIT
        gate_up = {}

        def first(k):
            x = _unpack_rows(x_ref, sub, k * sub)
            gate_up[k] = (_dot(x, wg_bf[...]), _dot(x, wu_bf[...]))

        def second(k):
            gate, up = gate_up.pop(k)
            act = (gate * jax.nn.sigmoid(gate)) * up
            _rows_to_tiles(y_ref, _dot(act.astype(_BF16), wd_bf[...]), k * sub)

        for k in range(min(EXPERT_AHEAD, EXPERT_SPLIT)):
            first(k)
        for k in range(EXPERT_SPLIT):
            if k + EXPERT_AHEAD < EXPERT_SPLIT:
                first(k + EXPERT_AHEAD)
            second(k)

    @pl.when(i >= used_ref[0])
    def _():
        y_ref[...] = jnp.zeros_like(y_ref)


def _experts(xs, block_e, n_used, w_gate, w_up, w_down):
    n_rows, d = xs.shape[0] // PACK_ROWS, D_MODEL
    n_blocks = n_rows // ROW_BLOCK
    tile_block = (ROW_BLOCK * SUBLANES, LANES)

    def xmap(i, be, used):
        return (jnp.minimum(i, used[0] - 1), 0)

    def wmap(i, be, used):
        return (be[jnp.minimum(i, used[0] - 1)], 0, 0)

    return pl.pallas_call(
        _expert_kernel,
        grid_spec=pltpu.PrefetchScalarGridSpec(
            num_scalar_prefetch=2,
            grid=(n_blocks,),
            in_specs=[pl.BlockSpec((ROW_BLOCK * PACK_ROWS, LANES), xmap),
                      pl.BlockSpec((None, d, D_EXPERT), wmap),
                      pl.BlockSpec((None, d, D_EXPERT), wmap),
                      pl.BlockSpec((None, D_EXPERT, d), wmap)],
            out_specs=pl.BlockSpec(tile_block, lambda i, be, used: (i, 0)),
            scratch_shapes=[pltpu.VMEM((d, D_EXPERT), _BF16), pltpu.VMEM((d, D_EXPERT), _BF16),
                            pltpu.VMEM((D_EXPERT, d), _BF16)]),
        out_shape=jax.ShapeDtypeStruct((n_rows * SUBLANES, LANES), _F32),
        compiler_params=_cparams(1),
        name="moe_experts",
    )(block_e, n_used, xs, w_gate, w_up, w_down)


def _combine_kernel(dest_ref, next_ref, h_ref, info_ref, ys_ref, o_ref, buf_ref, sem):
    n = h_ref.shape[0]
    step = pl.program_id(0)
    slot = step % 2

    def gather(idx_ref, to_slot):
        def issue(i, carry):
            for u in range(ROW_UNROLL):
                t = i * ROW_UNROLL + u
                _row_copy(ys_ref, idx_ref[2 * t], buf_ref.at[to_slot, 0], t,
                          sem.at[to_slot]).start(priority=0)
                _row_copy(ys_ref, idx_ref[2 * t + 1], buf_ref.at[to_slot, 1], t,
                          sem.at[to_slot]).start(priority=1)
            return carry

        lax.fori_loop(0, n // ROW_UNROLL, issue, 0)

    @pl.when(step == 0)
    def _():
        gather(dest_ref, 0)

    @pl.when(step + 1 < pl.num_programs(0))
    def _():
        gather(next_ref, 1 - slot)

    for k in range(2):
        pltpu.make_async_copy(ys_ref.at[_tile_rows(0, n)], buf_ref.at[slot, k], sem.at[slot]).wait()
    info = info_ref[...]
    w0 = info[:, 0:1]
    w1 = info[:, 1:2]
    for s in range(SUBLANES):
        sl = slice(s * LANES, (s + 1) * LANES)
        moe = w0 * _tile_block(buf_ref.at[slot, 0], s, n) + w1 * _tile_block(buf_ref.at[slot, 1], s, n)
        o_ref[:, sl] = h_ref[:, sl] + moe


def _combine(h, info, ys, dest):
    t, d = h.shape
    n = min(t, COMBINE_TOKENS)
    steps = t // n
    return pl.pallas_call(
        _combine_kernel,
        grid=(steps,),
        in_specs=[pl.BlockSpec((2 * n,), lambda i: (i,), memory_space=pltpu.SMEM),
                  pl.BlockSpec((2 * n,), lambda i: (jnp.minimum(i + 1, steps - 1),),
                               memory_space=pltpu.SMEM),
                  pl.BlockSpec((n, d), lambda i: (i, 0)),
                  pl.BlockSpec((n, LANES), lambda i: (i, 0)),
                  pl.BlockSpec(memory_space=pl.ANY)],
        out_specs=pl.BlockSpec((n, d), lambda i: (i, 0)),
        out_shape=jax.ShapeDtypeStruct((t, d), _F32),
        scratch_shapes=[pltpu.VMEM((2, 2, n * SUBLANES, LANES), _F32), pltpu.SemaphoreType.DMA((2,))],
        compiler_params=_cparams(1),
        name="moe_combine",
    )(dest, dest, h, info, ys)


def _moe_layout(route_rows, counts, t):
    counts = counts[ROUTE_LANE0:ROUTE_LANE0 + N_EXPERTS, 0].astype(jnp.int32)
    padded = (counts + ROW_BLOCK - 1) // ROW_BLOCK * ROW_BLOCK
    pends = jnp.cumsum(padded)
    pstarts = pends - padded
    eid = route_rows[2:4].astype(jnp.int32)
    rank = route_rows[4:6].astype(jnp.int32)
    experts = jnp.arange(N_EXPERTS, dtype=jnp.int32)
    start_of = jnp.sum(jnp.where(eid[:, :, None] == experts, pstarts, 0), axis=-1)
    dest = (start_of + rank).T.reshape(-1)
    n_blocks = -(-2 * t // ROW_BLOCK) + N_EXPERTS
    first_row = jnp.arange(n_blocks, dtype=jnp.int32) * ROW_BLOCK
    block_e = jnp.minimum(jnp.sum((pends[None, :] <= first_row[:, None]).astype(jnp.int32), axis=1),
                          N_EXPERTS - 1)
    n_used = (pends[-1:] // ROW_BLOCK).astype(jnp.int32)
    return dest, block_e, n_used, pstarts + counts, padded - counts, n_blocks * ROW_BLOCK


def kernel(x, mem, positions, mix_norm_g, w_in, qn_a, kn_a, rel_bias, ret_gn_g, mem_norm_g, w_mem_kv,
           qn_c, kn_c, w_out, ffn_norm_g, w_router_group, b_router_group, w_router_expert,
           b_router_expert, w_gate, w_up, w_down):
    b, s, d = x.shape
    t = b * s
    x2 = x.reshape(t, d)
    cos, sin = _rope_tables(positions)
    kc, vc = _mem_kv(mem, mem_norm_g, w_mem_kv, kn_c)
    qa, ka, vta, proj_b, qc = _in_proj(x, mix_norm_g, w_in, qn_a, kn_a, qn_c)
    out_a = _attention(qa, ka, vta, rel_bias)
    out_b = _retention(proj_b, cos, sin, ret_gn_g)
    out_c = _cross_attention(qc, kc, vc)
    h, hn, info, route_rows, counts = _out_router(
        x2, out_a.reshape(t, A_WIDTH), out_b.reshape(t, B_WIDTH), out_c.reshape(t, C_WIDTH),
        w_out, ffn_norm_g, w_router_group, b_router_group, w_router_expert, b_router_expert)
    dest, block_e, n_used, pad_start, pad_len, n_rows = _moe_layout(route_rows, counts, t)
    xs = _dispatch(hn, dest, pad_start, pad_len, n_used, n_rows)
    ys = _experts(xs, block_e, n_used, w_gate, w_up, w_down)
    return _combine(h, info, ys, dest).reshape(b, s, d)
```

```python
import functools

import jax
import jax.numpy as jnp
from jax import lax
from jax.experimental import pallas as pl
from jax.experimental.pallas import tpu as pltpu

D_MODEL = 1024
CHUNK = 64
HEAD_DIM = 64
A_HEADS = 8
B_HEADS = 4
C_HEADS = 4
A_WIDTH = A_HEADS * HEAD_DIM
B_WIDTH = B_HEADS * HEAD_DIM
C_WIDTH = C_HEADS * HEAD_DIM
IN_COLS = 3 * A_WIDTH + 4 * B_WIDTH + C_WIDTH
LEFT_CHUNKS = 8
BAND_CHUNKS = LEFT_CHUNKS + 1
MAX_REL_DIST = 128
ROPE_BASE = 10000.0
N_GROUPS = 4
EXPERTS_PER_GROUP = 8
N_EXPERTS = N_GROUPS * EXPERTS_PER_GROUP
D_EXPERT = D_MODEL // 2
EPS = 1e-6
NEG_INF = -1e30
LOG2E = 1.4426950408889634

LANES = 128
SUBLANES = 8
assert D_MODEL == SUBLANES * LANES
PACK_ROWS = SUBLANES // 2
LEFT_ROWS = LEFT_CHUNKS * CHUNK
ATT_Q = 2 * CHUNK
ATT_K = ATT_Q + LEFT_ROWS
ATT_VARIANTS = LEFT_ROWS // ATT_Q + 1
ONES_ROWS = 16
ATT_AHEAD = 3
RET_CHUNK = 256
ROW_BLOCK = 512
EXPERT_SPLIT = 2
EXPERT_AHEAD = 2
ROUTE_LANE0 = N_GROUPS
ROUTE_ROWS = 64
VMEM_LIMIT = 48 * 1024 * 1024

_F32 = jnp.float32
_BF16 = jnp.bfloat16


def _cparams(n_axes):
    return pltpu.CompilerParams(dimension_semantics=("arbitrary",) * n_axes,
                                vmem_limit_bytes=VMEM_LIMIT)


def _dot(a, b):
    return jnp.dot(a, b, preferred_element_type=_F32)


def _dot_nt(a, b):
    return lax.dot_general(a, b, (((1,), (1,)), ((), ())), preferred_element_type=_F32)


def _lane(shape):
    return lax.broadcasted_iota(jnp.int32, shape, len(shape) - 1)


def _pair_rms(t, gain):
    low = _lane(t.shape) < HEAD_DIM
    t2 = t * t
    ms0 = jnp.sum(jnp.where(low, t2, 0.0), axis=-1, keepdims=True) * (1.0 / HEAD_DIM)
    ms1 = jnp.sum(jnp.where(low, 0.0, t2), axis=-1, keepdims=True) * (1.0 / HEAD_DIM)
    r = jnp.where(low, lax.rsqrt(ms0 + EPS), lax.rsqrt(ms1 + EPS))
    return (t * r) * gain


def _rows_to_tiles(ref, val, row0=0):
    n = val.shape[0]
    for s in range(SUBLANES):
        ref[pl.ds(row0 * SUBLANES + s, n, stride=SUBLANES), :] = val[:, s * LANES:(s + 1) * LANES]


def _tile_block(ref, s, n, row0=0):
    return ref[pl.ds(row0 * SUBLANES + s, n, stride=SUBLANES), :]


def _tiles_to_rows(ref, n, row0=0):
    return jnp.concatenate([_tile_block(ref, s, n, row0) for s in range(SUBLANES)], axis=-1)


def _pack_rows(ref, val, row0=0):
    n = val.shape[0]
    for s in range(PACK_ROWS):
        lo = val[:, (2 * s) * LANES:(2 * s + 1) * LANES].astype(_BF16).astype(_F32)
        hi = val[:, (2 * s + 1) * LANES:(2 * s + 2) * LANES].astype(_BF16).astype(_F32)
        word = (lax.bitcast_convert_type(lo, jnp.uint32) >> 16) | (
            lax.bitcast_convert_type(hi, jnp.uint32) & jnp.uint32(0xFFFF0000))
        ref[pl.ds(row0 * PACK_ROWS + s, n, stride=PACK_ROWS), :] = word


def _unpack_rows(ref, n, row0=0):
    parts = []
    for s in range(PACK_ROWS):
        word = ref[pl.ds(row0 * PACK_ROWS + s, n, stride=PACK_ROWS), :]
        parts.append(lax.bitcast_convert_type(word << 16, _F32))
        parts.append(lax.bitcast_convert_type(word & jnp.uint32(0xFFFF0000), _F32))
    return jnp.concatenate(parts, axis=-1).astype(_BF16)


ROPE_HALF = HEAD_DIM // 2
ROPE_PACK = LANES // ROPE_HALF


def _rope_kernel(pos_ref, inv_ref, cos_ref, sin_ref):
    ang = pos_ref[...].astype(_F32) * inv_ref[...]
    rows = ang.shape[0]
    lane = _lane(ang.shape)
    sign = jnp.where((lane % HEAD_DIM) < ROPE_HALF, -1.0, 1.0)
    for out_ref, val in ((cos_ref, jnp.cos(ang)), (sin_ref, jnp.sin(ang))):
        for j in range(ROPE_PACK):
            seg = jnp.where(lane // ROPE_HALF == j, val, 0.0)
            full = seg
            for k in range(1, ROPE_PACK):
                full = full + pltpu.roll(seg, k * ROPE_HALF, 1)
            if out_ref is sin_ref:
                full = full * sign
            out_ref[pl.ds(j, rows, stride=ROPE_PACK), :] = full


def _rope_tables(positions):
    t = positions.size
    inv = ROPE_BASE ** (-jnp.arange(ROPE_HALF, dtype=_F32) / ROPE_HALF)
    inv128 = jnp.tile(inv, ROPE_PACK).reshape(1, LANES)
    rows = t // ROPE_PACK
    pos = jnp.repeat(positions.reshape(rows, ROPE_PACK), ROPE_HALF, axis=1)
    tm = min(rows, 512)
    out = pl.BlockSpec((tm * ROPE_PACK, LANES), lambda i: (i, 0))
    return pl.pallas_call(
        _rope_kernel,
        grid=(rows // tm,),
        in_specs=[pl.BlockSpec((tm, LANES), lambda i: (i, 0)), pl.BlockSpec((1, LANES), lambda i: (0, 0))],
        out_specs=[out, out],
        out_shape=[jax.ShapeDtypeStruct((t, LANES), _F32)] * 2,
        compiler_params=_cparams(1),
        name="rope_tables",
    )(pos, inv128)


def _mem_kv_kernel(mem_ref, g_ref, w_ref, kn_ref, k_ref, v_ref):
    m = mem_ref[...]
    ms = jnp.mean(m * m, axis=-1, keepdims=True)
    mn = (m * lax.rsqrt(ms + EPS)) * g_ref[...]
    kv = _dot(mn.astype(_BF16), w_ref[...])
    for j in range(C_WIDTH // LANES):
        sl = slice(j * LANES, (j + 1) * LANES)
        k_ref[:, sl] = _pair_rms(kv[:, sl], kn_ref[...]).astype(_BF16)
    v_ref[...] = kv[:, C_WIDTH:].T.astype(_BF16)


def _mem_kv(mem, mem_norm_g, w_mem_kv, kn_c):
    b, m, d = mem.shape
    kn = jnp.tile(kn_c, 2).reshape(1, LANES)
    return pl.pallas_call(
        _mem_kv_kernel,
        grid=(b,),
        in_specs=[pl.BlockSpec((None, m, d), lambda i: (i, 0, 0)),
                  pl.BlockSpec((1, d), lambda i: (0, 0)),
                  pl.BlockSpec((d, 2 * C_WIDTH), lambda i: (0, 0)),
                  pl.BlockSpec((1, LANES), lambda i: (0, 0))],
        out_specs=[pl.BlockSpec((None, m, C_WIDTH), lambda i: (i, 0, 0)),
                   pl.BlockSpec((None, C_WIDTH, m), lambda i: (i, 0, 0))],
        out_shape=[jax.ShapeDtypeStruct((b, m, C_WIDTH), _BF16),
                   jax.ShapeDtypeStruct((b, C_WIDTH, m), _BF16)],
        compiler_params=_cparams(1),
        name="mem_kv",
    )(mem, mem_norm_g.reshape(1, d), w_mem_kv.astype(_BF16), kn)


def _in_proj_kernel(x_ref, g_ref, wq_ref, wk_ref, wvt_ref, wr_ref, wc_ref, qn_ref, kn_ref, cn_ref,
                    qa_ref, ka_ref, vt_ref, ret_ref, qc_ref, xn_ref, acc_ref):
    x = x_ref[...]
    ms = jnp.mean(x * x, axis=-1, keepdims=True)
    xn_ref[...] = ((x * lax.rsqrt(ms + EPS)) * g_ref[...]).astype(_BF16)

    def normed(slot, out_ref, gain_ref):
        for blk in range(out_ref.shape[1] // LANES):
            sl = slice(blk * LANES, (blk + 1) * LANES)
            out_ref[:, sl] = _pair_rms(acc_ref[slot, :, sl], gain_ref[...]).astype(_BF16)

    acc_ref[0] = _dot(xn_ref[...], wq_ref[...])
    acc_ref[1] = _dot(xn_ref[...], wk_ref[...])
    normed(0, qa_ref, qn_ref)
    acc_ref[0] = _dot_nt(wvt_ref[...], xn_ref[...])
    normed(1, ka_ref, kn_ref)
    ret_ref[...] = _dot(xn_ref[...], wr_ref[...])
    for blk in range(vt_ref.shape[0]):
        vt_ref[blk] = acc_ref[0, :, blk * LANES:(blk + 1) * LANES].astype(_BF16)
    acc_ref[1, :, 0:C_WIDTH] = _dot(xn_ref[...], wc_ref[...])
    normed(1, qc_ref, cn_ref)


def _in_proj(x3, g, w_in, qn_a, kn_a, qn_c):
    b, s, d = x3.shape
    tm = min(s, 512)
    assert tm == A_WIDTH
    w = w_in.astype(_BF16)
    cuts = [0, A_WIDTH, 2 * A_WIDTH, 3 * A_WIDTH, 3 * A_WIDTH + 4 * B_WIDTH, IN_COLS]
    wq, wk, wv, wr, wc = (w[:, lo:hi] for lo, hi in zip(cuts[:-1], cuts[1:]))
    scale = HEAD_DIM ** -0.5 * LOG2E
    gains = [(jnp.tile(gn, 2) * sc).reshape(1, LANES) for gn, sc in ((qn_a, scale), (kn_a, 1.0), (qn_c, scale))]

    def whole(arr):
        return pl.BlockSpec(arr.shape, lambda i, j: (0,) * arr.ndim)

    def rows(width):
        return pl.BlockSpec((None, tm, width), lambda i, j: (i, j, 0))

    consts = [g.reshape(1, d), wq, wk, wv.T, wr, wc] + gains
    return pl.pallas_call(
        _in_proj_kernel,
        grid=(b, s // tm),
        in_specs=[rows(d)] + [whole(c) for c in consts],
        out_specs=[rows(A_WIDTH), rows(A_WIDTH),
                   pl.BlockSpec((None, tm // LANES, A_WIDTH, LANES), lambda i, j: (i, j, 0, 0)),
                   rows(4 * B_WIDTH), rows(C_WIDTH)],
        out_shape=[jax.ShapeDtypeStruct((b, s, A_WIDTH), _BF16), jax.ShapeDtypeStruct((b, s, A_WIDTH), _BF16),
                   jax.ShapeDtypeStruct((b, s // LANES, A_WIDTH, LANES), _BF16),
                   jax.ShapeDtypeStruct((b, s, 4 * B_WIDTH), _F32), jax.ShapeDtypeStruct((b, s, C_WIDTH), _BF16)],
        scratch_shapes=[pltpu.VMEM((tm, d), _BF16), pltpu.VMEM((2, tm, A_WIDTH), _F32)],
        compiler_params=_cparams(2),
        name="in_proj",
    )(x3, *consts)


def _attn_kernel(q_ref, k_ref, vt_ref, bias_ref, o_ref, kp_ref, st_ref, var_ref, *, q_rows):
    qs = pl.program_id(2)
    s = k_ref.shape[0]
    fill_rows = min(s, 1024)
    left_blocks = LEFT_ROWS // LANES

    @pl.when(qs == 0)
    def _():
        kp_ref[0:LEFT_ROWS, :] = jnp.zeros((LEFT_ROWS, LANES), _BF16)

        def fill(i, carry):
            r = pl.multiple_of(i * fill_rows, fill_rows)
            kp_ref[pl.ds(LEFT_ROWS + r, fill_rows), :] = k_ref[pl.ds(r, fill_rows), :]
            return carry

        lax.fori_loop(0, s // fill_rows, fill, 0)
        key = lax.broadcasted_iota(jnp.int32, (ATT_K, 2 * ATT_Q), 0)
        for v in range(ATT_VARIANTS):
            var_ref[v] = jnp.where(key >= LEFT_ROWS - ATT_Q * v, bias_ref[...], NEG_INF)

    low = _lane((ATT_Q, LANES)) < HEAD_DIM
    ones = jnp.ones((ONES_ROWS, ATT_K), _BF16)
    tiles_per_step = q_rows // ATT_Q

    def scores(j):
        cp = qs * tiles_per_step + j
        q = q_ref[j * ATT_Q:(j + 1) * ATT_Q, :]
        q2 = jnp.concatenate([jnp.where(low, q, jnp.zeros_like(q)), jnp.where(low, jnp.zeros_like(q), q)], axis=0)
        kb = kp_ref[pl.ds(pl.multiple_of(cp * ATT_Q, ATT_Q), ATT_K), :]
        st_ref[j % (ATT_AHEAD + 1)] = _dot_nt(kb, q2) + var_ref[jnp.minimum(cp, ATT_VARIANTS - 1)]

    def finish(j):
        cp = qs * tiles_per_step + j
        st = st_ref[j % (ATT_AHEAD + 1)]
        m = jnp.max(st, axis=0, keepdims=True)
        p = jnp.exp2(st - m)
        vt = jnp.concatenate([vt_ref[jnp.maximum(cp + kb_i - left_blocks, 0)] for kb_i in range(ATT_K // LANES)],
                             axis=1)
        ot = _dot(jnp.concatenate([vt, ones], axis=0), p.astype(_BF16))
        inv = 1.0 / ot[LANES:LANES + 1, :]
        out_t = jnp.concatenate([ot[0:HEAD_DIM, 0:ATT_Q] * inv[:, 0:ATT_Q],
                                 ot[HEAD_DIM:LANES, ATT_Q:] * inv[:, ATT_Q:]], axis=0)
        o_ref[j * ATT_Q:(j + 1) * ATT_Q, :] = out_t.T.astype(o_ref.dtype)

    for j in range(min(ATT_AHEAD, tiles_per_step)):
        scores(j)
    for j in range(tiles_per_step):
        if j + ATT_AHEAD < tiles_per_step:
            scores(j + ATT_AHEAD)
        finish(j)


def _toeplitz_bias(rel_bias, q_len, k_len):
    h, table = rel_bias.shape
    n_diag = q_len + k_len - 1
    flat_lo = k_len - 1 - LEFT_ROWS - (CHUNK - 1)
    flat_hi = n_diag - flat_lo - table
    rev = jnp.concatenate([jnp.broadcast_to(rel_bias[:, -1:], (h, flat_hi)), rel_bias[:, ::-1],
                           jnp.broadcast_to(rel_bias[:, :1], (h, flat_lo))], axis=1).astype(_F32)
    flat = jnp.tile(rev, (1, q_len + 1))
    pitch = n_diag - 1
    skew = flat[:, q_len - 1:q_len - 1 + q_len * pitch].reshape(h, q_len, pitch)
    return skew[:, :, :k_len]


def _attn_bias(rel_bias):
    h = rel_bias.shape[0]
    bias = _toeplitz_bias(rel_bias, ATT_Q, ATT_K)
    q = lax.broadcasted_iota(jnp.int32, (ATT_Q, ATT_K), 0)
    k = lax.broadcasted_iota(jnp.int32, (ATT_Q, ATT_K), 1)
    off = k // CHUNK - q // CHUNK
    in_band = (off >= 0) & (off < BAND_CHUNKS)
    full = jnp.where(in_band[None], bias * LOG2E, NEG_INF)
    full = full.reshape(h // 2, 2, ATT_Q, ATT_K)
    return full.transpose(0, 3, 1, 2).reshape(h // 2, ATT_K, 2 * ATT_Q)


def _attention(qa, ka, vta, rel_bias):
    b, s, _ = qa.shape
    q_rows = min(s, 2048)
    pairs = A_HEADS // 2
    return pl.pallas_call(
        functools.partial(_attn_kernel, q_rows=q_rows),
        grid=(b, pairs, s // q_rows),
        in_specs=[pl.BlockSpec((None, q_rows, LANES), lambda i, p, j: (i, j, p)),
                  pl.BlockSpec((None, s, LANES), lambda i, p, j: (i, 0, p)),
                  pl.BlockSpec((None, s // LANES, LANES, LANES), lambda i, p, j: (i, 0, p, 0)),
                  pl.BlockSpec((None, ATT_K, 2 * ATT_Q), lambda i, p, j: (p, 0, 0))],
        out_specs=pl.BlockSpec((None, q_rows, LANES), lambda i, p, j: (i, j, p)),
        out_shape=jax.ShapeDtypeStruct((b, s, A_WIDTH), _BF16),
        scratch_shapes=[pltpu.VMEM((s + LEFT_ROWS, LANES), _BF16),
                        pltpu.VMEM((ATT_AHEAD + 1, ATT_K, 2 * ATT_Q), _F32),
                        pltpu.VMEM((ATT_VARIANTS, ATT_K, 2 * ATT_Q), _F32)],
        compiler_params=_cparams(3),
        name="attn_a",
    )(qa, ka, vta, _attn_bias(rel_bias))


def _swap_halves(t):
    first = (_lane(t.shape) % HEAD_DIM) < (HEAD_DIM // 2)
    return jnp.where(first, pltpu.roll(t, LANES - HEAD_DIM // 2, 1), pltpu.roll(t, HEAD_DIM // 2, 1))


def _retention_kernel(q_ref, k_ref, v_ref, gate_ref, cos_ref, sin_ref, decay_ref, zeta_ref, xi_ref,
                      cd_ref, gn_ref, o_ref, state_ref, *, rows):
    @pl.when(pl.program_id(2) == 0)
    def _():
        state_ref[...] = jnp.zeros_like(state_ref)

    c = RET_CHUNK
    low = _lane((c, LANES)) < HEAD_DIM
    eye = jnp.where(lax.broadcasted_iota(jnp.int32, (LANES, LANES), 0) == _lane((LANES, LANES)),
                    1.0, 0.0).astype(_BF16)
    srow = lax.broadcasted_iota(jnp.int32, (LANES, LANES), 0) < HEAD_DIM
    scol = _lane((LANES, LANES)) < HEAD_DIM
    same_head = srow == scol

    for j in range(rows // c):
        sl = slice(j * c, (j + 1) * c)
        cos, sin = cos_ref[sl, :], sin_ref[sl, :]
        q = q_ref[sl, :]
        k = k_ref[sl, :]
        qr = q * cos + _swap_halves(q) * sin
        kr = (k * cos + _swap_halves(k) * sin) * (HEAD_DIM ** -0.5)
        vb = v_ref[sl, :].astype(_BF16)
        qb = qr.astype(_BF16)
        kb = kr.astype(_BF16)
        inner_out = []
        for h in range(2):
            qh = jnp.where(low if h == 0 else ~low, qr, 0.0).astype(_BF16)
            inner = _dot_nt(qh, kb) * decay_ref[h]
            inner_out.append(_dot(inner.astype(_BF16), vb))
        state = state_ref[...]
        cross = _dot(qb, state.astype(_BF16)) * xi_ref[...]
        o = jnp.where(low, inner_out[0], inner_out[1]) + cross
        kz = _dot_nt(eye, (kr * zeta_ref[...]).astype(_BF16)).astype(_BF16)
        state_ref[...] = cd_ref[...] * state + jnp.where(same_head, _dot(kz, vb), 0.0)
        mu = jnp.where(low,
                       jnp.sum(jnp.where(low, o, 0.0), axis=-1, keepdims=True),
                       jnp.sum(jnp.where(low, 0.0, o), axis=-1, keepdims=True)) * (1.0 / HEAD_DIM)
        dlt = o - mu
        d2 = dlt * dlt
        var = jnp.where(low,
                        jnp.sum(jnp.where(low, d2, 0.0), axis=-1, keepdims=True),
                        jnp.sum(jnp.where(low, 0.0, d2), axis=-1, keepdims=True)) * (1.0 / HEAD_DIM)
        y = (dlt * lax.rsqrt(var + EPS)) * gn_ref[...]
        g = gate_ref[sl, :]
        o_ref[sl, :] = ((g * jax.nn.sigmoid(g)) * y).astype(o_ref.dtype)


def _retention_tables():
    c = RET_CHUNK
    log_g = jnp.log(1.0 - jnp.exp2(-5.0 - jnp.arange(B_HEADS, dtype=_F32)))
    idx = jnp.arange(c, dtype=_F32)
    diff = idx[:, None] - idx[None, :]
    decay = jnp.where(diff >= 0, jnp.exp(log_g[:, None, None] * jnp.maximum(diff, 0.0)), 0.0)
    zeta = jnp.exp(log_g[:, None] * (c - 1 - idx))
    xi = jnp.exp(log_g[:, None] * (idx + 1.0))
    cd = jnp.exp(log_g * c)

    def lanes(tab):
        return jnp.repeat(tab.reshape(B_HEADS // 2, 2, c), HEAD_DIM, axis=1).transpose(0, 2, 1)

    cdm = jnp.repeat(cd.reshape(B_HEADS // 2, 2), HEAD_DIM, axis=1)
    cdm = jnp.broadcast_to(cdm[:, :, None], (B_HEADS // 2, LANES, LANES))
    return decay, lanes(zeta), lanes(xi), cdm


def _retention(proj3, cos, sin, ret_gn_g):
    b, s, _ = proj3.shape
    rows = min(s, 2048)
    pairs = B_HEADS // 2
    base = 0
    decay, zeta, xi, cdm = _retention_tables()
    cos3, sin3 = cos.reshape(b, s, LANES), sin.reshape(b, s, LANES)
    gn = ret_gn_g.reshape(pairs, 1, LANES)

    def col(off):
        return pl.BlockSpec((None, rows, LANES), lambda i, p, j: (i, j, base + off * pairs + p))

    tab = pl.BlockSpec((None, rows, LANES), lambda i, p, j: (i, j, 0))
    return pl.pallas_call(
        functools.partial(_retention_kernel, rows=rows),
        grid=(b, pairs, s // rows),
        in_specs=[col(0), col(1), col(2), col(3), tab, tab,
                  pl.BlockSpec((2, RET_CHUNK, RET_CHUNK), lambda i, p, j: (p, 0, 0)),
                  pl.BlockSpec((None, RET_CHUNK, LANES), lambda i, p, j: (p, 0, 0)),
                  pl.BlockSpec((None, RET_CHUNK, LANES), lambda i, p, j: (p, 0, 0)),
                  pl.BlockSpec((None, LANES, LANES), lambda i, p, j: (p, 0, 0)),
                  pl.BlockSpec((None, 1, LANES), lambda i, p, j: (p, 0, 0))],
        out_specs=pl.BlockSpec((None, rows, LANES), lambda i, p, j: (i, j, p)),
        out_shape=jax.ShapeDtypeStruct((b, s, B_WIDTH), _BF16),
        scratch_shapes=[pltpu.VMEM((LANES, LANES), _F32)],
        compiler_params=_cparams(3),
        name="retention_b",
    )(proj3, proj3, proj3, proj3, cos3, sin3, decay, zeta, xi, cdm, gn)


def _cross_kernel(q_ref, k_ref, vt_ref, o_ref, st_ref, *, rows):
    low = _lane((ATT_Q, LANES)) < HEAD_DIM
    lane_blocks = C_WIDTH // LANES
    tiles = [(j, lb) for j in range(rows // ATT_Q) for lb in range(lane_blocks)]
    ones = jnp.ones((ONES_ROWS, vt_ref.shape[1]), _BF16)

    def scores(i):
        j, lb = tiles[i]
        sl = slice(lb * LANES, (lb + 1) * LANES)
        q = q_ref[j * ATT_Q:(j + 1) * ATT_Q, sl]
        q2 = jnp.concatenate([jnp.where(low, q, jnp.zeros_like(q)), jnp.where(low, jnp.zeros_like(q), q)], axis=0)
        st_ref[i % (ATT_AHEAD + 1)] = _dot_nt(k_ref[:, sl], q2)

    def finish(i):
        j, lb = tiles[i]
        sl = slice(lb * LANES, (lb + 1) * LANES)
        st = st_ref[i % (ATT_AHEAD + 1)]
        p = jnp.exp2(st - jnp.max(st, axis=0, keepdims=True))
        ot = _dot(jnp.concatenate([vt_ref[sl, :], ones], axis=0), p.astype(_BF16))
        inv = 1.0 / ot[LANES:LANES + 1, :]
        out_t = jnp.concatenate([ot[0:HEAD_DIM, 0:ATT_Q] * inv[:, 0:ATT_Q],
                                 ot[HEAD_DIM:LANES, ATT_Q:] * inv[:, ATT_Q:]], axis=0)
        o_ref[j * ATT_Q:(j + 1) * ATT_Q, sl] = out_t.T.astype(o_ref.dtype)

    for i in range(min(ATT_AHEAD, len(tiles))):
        scores(i)
    for i in range(len(tiles)):
        if i + ATT_AHEAD < len(tiles):
            scores(i + ATT_AHEAD)
        finish(i)


def _cross_attention(qc, kc, vtc):
    b, s, _ = qc.shape
    m = kc.shape[1]
    rows = min(s, 1024)
    return pl.pallas_call(
        functools.partial(_cross_kernel, rows=rows),
        grid=(b, s // rows),
        in_specs=[pl.BlockSpec((None, rows, C_WIDTH), lambda i, j: (i, j, 0)),
                  pl.BlockSpec((None, m, C_WIDTH), lambda i, j: (i, 0, 0)),
                  pl.BlockSpec((None, C_WIDTH, m), lambda i, j: (i, 0, 0))],
        out_specs=pl.BlockSpec((None, rows, C_WIDTH), lambda i, j: (i, j, 0)),
        out_shape=jax.ShapeDtypeStruct((b, s, C_WIDTH), _BF16),
        scratch_shapes=[pltpu.VMEM((ATT_AHEAD + 1, m, 2 * ATT_Q), _F32)],
        compiler_params=_cparams(2),
        name="cross_c",
    )(qc, kc, vtc)


def _out_router_kernel(x_ref, a_ref, b_ref, c_ref, wo_ref, g_ref, wr_ref, br_ref,
                       h_ref, hn_ref, info_ref, rows_ref, cnt_ref, carry_ref):
    @pl.when(pl.program_id(0) == 0)
    def _():
        carry_ref[...] = jnp.zeros_like(carry_ref)

    tm = x_ref.shape[0]
    h = x_ref[...]
    h = h + _dot(a_ref[...], wo_ref[0:A_WIDTH, :])
    h = h + _dot(b_ref[...], wo_ref[A_WIDTH:A_WIDTH + B_WIDTH, :])
    h = h + _dot(c_ref[...], wo_ref[A_WIDTH + B_WIDTH:, :])
    h_ref[...] = h
    ms = jnp.mean(h * h, axis=-1, keepdims=True)
    hn = (h * lax.rsqrt(ms + EPS)) * g_ref[...]
    _pack_rows(hn_ref, hn)
    logits = _dot_nt(wr_ref[...], hn.astype(_BF16))[0:ROUTE_ROWS, :] + br_ref[:, 0:1]
    row = lax.broadcasted_iota(jnp.int32, (ROUTE_ROWS, tm), 0).astype(_F32)
    big = float(ROUTE_ROWS)

    def first_row(mask):
        return jnp.min(jnp.where(mask, row, big), axis=0, keepdims=True)

    gmask = row < N_GROUPS
    gl = jnp.where(gmask, logits, NEG_INF)
    ge = jnp.exp(gl - jnp.max(gl, axis=0, keepdims=True))
    gp = ge / jnp.sum(ge, axis=0, keepdims=True)
    p_group = jnp.max(gp, axis=0, keepdims=True)
    g_sel = first_row(gmask & (gp == p_group))
    lo = ROUTE_LANE0 + g_sel * EXPERTS_PER_GROUP
    emask = (row >= lo) & (row < lo + EXPERTS_PER_GROUP)
    el = jnp.where(emask, logits, NEG_INF)
    ee = jnp.exp(el - jnp.max(el, axis=0, keepdims=True))
    ep = ee / jnp.sum(ee, axis=0, keepdims=True)
    p1 = jnp.max(ep, axis=0, keepdims=True)
    i1 = first_row(emask & (ep == p1))
    ep2 = jnp.where(emask & (row != i1), ep, -1.0)
    p2 = jnp.max(ep2, axis=0, keepdims=True)
    i2 = first_row(ep2 == p2)
    den = p1 + p2
    w1 = p_group * (p1 / den)
    w2 = p_group * (p2 / den)
    hit1 = row == i1
    hit2 = row == i2
    onehot = jnp.where(hit1 | hit2, 1.0, 0.0)
    r_i = lax.broadcasted_iota(jnp.int32, (tm, tm), 0)
    c_i = lax.broadcasted_iota(jnp.int32, (tm, tm), 1)
    earlier = jnp.where(r_i < c_i, 1.0, 0.0).astype(_BF16)
    before = _dot(onehot.astype(_BF16), earlier) + carry_ref[:, 0:1]
    r1 = jnp.sum(jnp.where(hit1, before, 0.0), axis=0, keepdims=True)
    r2 = jnp.sum(jnp.where(hit2, before, 0.0), axis=0, keepdims=True)
    carry_ref[...] = carry_ref[...] + jnp.sum(onehot, axis=1, keepdims=True)
    cnt_ref[...] = carry_ref[...]
    out_row = lax.broadcasted_iota(jnp.int32, (LANES, tm), 0)
    info = jnp.where(out_row == 0, w1, 0.0)
    info = jnp.where(out_row == 1, w2, info)
    info = jnp.where(out_row == 2, i1 - ROUTE_LANE0, info)
    info = jnp.where(out_row == 3, i2 - ROUTE_LANE0, info)
    info = jnp.where(out_row == 4, r1, info)
    info = jnp.where(out_row == 5, r2, info)
    rows_ref[...] = info[0:SUBLANES, :]
    info_ref[...] = info.T


def _out_router(x2, oa, ob, oc, w_out, ffn_g, w_rg, b_rg, w_re, b_re):
    t, d = x2.shape
    tm = min(t, 512)
    pad = LANES - N_GROUPS - N_EXPERTS
    wr = jnp.concatenate([w_rg, w_re, jnp.zeros((d, pad), _F32)], axis=1).T.astype(_BF16)
    br = jnp.concatenate([b_rg, b_re, jnp.zeros((ROUTE_ROWS - N_GROUPS - N_EXPERTS,), _F32)])
    br = jnp.broadcast_to(br[:, None], (ROUTE_ROWS, LANES))

    def rows(w):
        return pl.BlockSpec((tm, w), lambda i: (i, 0))

    def whole(r, c):
        return pl.BlockSpec((r, c), lambda i: (0, 0))

    return pl.pallas_call(
        _out_router_kernel,
        grid=(t // tm,),
        in_specs=[rows(d), rows(A_WIDTH), rows(B_WIDTH), rows(C_WIDTH), whole(d, d), whole(1, d),
                  whole(LANES, d), whole(ROUTE_ROWS, LANES)],
        out_specs=[rows(d), pl.BlockSpec((tm * PACK_ROWS, LANES), lambda i: (i, 0)), rows(LANES),
                   pl.BlockSpec((SUBLANES, tm), lambda i: (0, i)), whole(ROUTE_ROWS, LANES)],
        out_shape=[jax.ShapeDtypeStruct((t, d), _F32), jax.ShapeDtypeStruct((t * PACK_ROWS, LANES), jnp.uint32),
                   jax.ShapeDtypeStruct((t, LANES), _F32), jax.ShapeDtypeStruct((SUBLANES, t), _F32),
                   jax.ShapeDtypeStruct((ROUTE_ROWS, LANES), _F32)],
        scratch_shapes=[pltpu.VMEM((ROUTE_ROWS, LANES), _F32)],
        compiler_params=_cparams(1),
        name="out_router",
    )(x2, oa, ob, oc, w_out.astype(_BF16), ffn_g.reshape(1, d), wr, br)


DISPATCH_TOKENS = 1024
COMBINE_TOKENS = 512


ROW_UNROLL = 8


def _tile_rows(row, count=1, per=SUBLANES):
    start = row * per
    if not isinstance(start, int):
        start = pl.multiple_of(start, per)
    return pl.ds(start, count * per)


def _row_copy(src, s_row, dst, d_row, sem, per=SUBLANES):
    return pltpu.make_async_copy(src.at[_tile_rows(s_row, 1, per)], dst.at[_tile_rows(d_row, 1, per)], sem)


def _dispatch_kernel(pad_start_ref, pad_len_ref, used_ref, dest_ref, hn_ref, xs_ref, zero_ref, sem,
                     pad_sem):
    per = PACK_ROWS
    n = hn_ref.shape[0] // per

    @pl.when(pl.program_id(0) == 0)
    def _():
        zero_ref[...] = jnp.zeros_like(zero_ref)
        n_blocks = xs_ref.shape[0] // (ROW_BLOCK * per)

        def block_copy(blk):
            return pltpu.make_async_copy(zero_ref, xs_ref.at[_tile_rows(blk * ROW_BLOCK, ROW_BLOCK, per)],
                                         pad_sem)

        def put_block(blk, carry):
            block_copy(blk).start()
            return carry

        def done_block(blk, carry):
            block_copy(blk).wait()
            return carry

        lax.fori_loop(used_ref[0], n_blocks, put_block, 0)
        lax.fori_loop(used_ref[0], n_blocks, done_block, 0)
        bits = [1 << k for k in reversed(range(ROW_BLOCK.bit_length() - 1))]

        def tail(e, wait):
            row = pad_start_ref[e]
            for bit in bits:
                on = (pad_len_ref[e] & bit) != 0
                copy = pltpu.make_async_copy(zero_ref.at[_tile_rows(0, bit, per)],
                                             xs_ref.at[_tile_rows(row, bit, per)], pad_sem)

                @pl.when(on)
                def _():
                    copy.wait() if wait else copy.start()

                row = row + jnp.where(on, bit, 0)

        def put_tail(e, carry):
            tail(e, False)
            return carry

        def done_tail(e, carry):
            tail(e, True)
            return carry

        lax.fori_loop(0, N_EXPERTS, put_tail, 0)
        lax.fori_loop(0, N_EXPERTS, done_tail, 0)

    def issue(i, carry):
        for u in range(ROW_UNROLL):
            t = i * ROW_UNROLL + u
            _row_copy(hn_ref, t, xs_ref, dest_ref[2 * t], sem, per).start(priority=0)
            _row_copy(hn_ref, t, xs_ref, dest_ref[2 * t + 1], sem, per).start(priority=1)
        return carry

    lax.fori_loop(0, n // ROW_UNROLL, issue, 0)
    for _ in range(2):
        pltpu.make_async_copy(hn_ref, xs_ref.at[_tile_rows(0, n, per)], sem).wait()


def _dispatch(hn, dest, pad_start, pad_len, n_used, n_rows):
    t = hn.shape[0] // PACK_ROWS
    n = min(t, DISPATCH_TOKENS)
    return pl.pallas_call(
        _dispatch_kernel,
        grid_spec=pltpu.PrefetchScalarGridSpec(
            num_scalar_prefetch=3,
            grid=(t // n,),
            in_specs=[pl.BlockSpec((2 * n,), lambda i, *_: (i,), memory_space=pltpu.SMEM),
                      pl.BlockSpec((n * PACK_ROWS, LANES), lambda i, *_: (i, 0))],
            out_specs=pl.BlockSpec(memory_space=pl.ANY),
            scratch_shapes=[pltpu.VMEM((ROW_BLOCK * PACK_ROWS, LANES), hn.dtype), pltpu.SemaphoreType.DMA,
                            pltpu.SemaphoreType.DMA]),
        out_shape=jax.ShapeDtypeStruct((n_rows * PACK_ROWS, LANES), hn.dtype),
        compiler_params=_cparams(1),
        name="moe_dispatch",
    )(pad_start, pad_len, n_used, dest, hn)


def _expert_kernel(be_ref, run_ref, next_ref, used_ref, x_ref, wg_hbm, wu_hbm, wd_hbm, y_ref,
                   wg_f32, wu_f32, wd_f32, wg_bf, wu_bf, wd_bf, sem):
    i = pl.program_id(0)
    live = i < used_ref[0]
    new_expert = (i == 0) | (be_ref[i] != be_ref[jnp.maximum(i - 1, 0)])
    slot = run_ref[i] % 2

    def fetch(expert, to_slot):
        return [pltpu.make_async_copy(src.at[expert], dst.at[to_slot], sem.at[to_slot, k])
                for k, (src, dst) in enumerate(((wg_hbm, wg_f32), (wu_hbm, wu_f32), (wd_hbm, wd_f32)))]

    @pl.when(live & (i == 0))
    def _():
        for copy in fetch(be_ref[0], 0):
            copy.start()

    @pl.when(live & new_expert)
    def _():
        for copy in fetch(be_ref[i], slot):
            copy.wait()

        @pl.when(next_ref[i] >= 0)
        def _():
            for copy in fetch(next_ref[i], 1 - slot):
                copy.start()

        wg_bf[...] = wg_f32[slot].astype(_BF16)
        wu_bf[...] = wu_f32[slot].astype(_BF16)
        wd_bf[...] = wd_f32[slot].astype(_BF16)

    @pl.when(live)
    def _():
        sub = ROW_BLOCK // EXPERT_SPLIT
        gate_up = {}

        def first(k):
            x = _unpack_rows(x_ref, sub, k * sub)
            gate_up[k] = (_dot(x, wg_bf[...]), _dot(x, wu_bf[...]))

        def second(k):
            gate, up = gate_up.pop(k)
            act = (gate * jax.nn.sigmoid(gate)) * up
            _rows_to_tiles(y_ref, _dot(act.astype(_BF16), wd_bf[...]), k * sub)

        for k in range(min(EXPERT_AHEAD, EXPERT_SPLIT)):
            first(k)
        for k in range(EXPERT_SPLIT):
            if k + EXPERT_AHEAD < EXPERT_SPLIT:
                first(k + EXPERT_AHEAD)
            second(k)

    @pl.when(i >= used_ref[0])
    def _():
        y_ref[...] = jnp.zeros_like(y_ref)


def _experts(xs, blocks, w_gate, w_up, w_down):
    n_rows, d = xs.shape[0] // PACK_ROWS, D_MODEL
    n_blocks = n_rows // ROW_BLOCK
    tile_block = (ROW_BLOCK * SUBLANES, LANES)
    hbm = pl.BlockSpec(memory_space=pl.ANY)

    return pl.pallas_call(
        _expert_kernel,
        grid_spec=pltpu.PrefetchScalarGridSpec(
            num_scalar_prefetch=4,
            grid=(n_blocks,),
            in_specs=[pl.BlockSpec((ROW_BLOCK * PACK_ROWS, LANES),
                                   lambda i, be, run, nxt, used: (jnp.minimum(i, used[0] - 1), 0)),
                      hbm, hbm, hbm],
            out_specs=pl.BlockSpec(tile_block, lambda i, *_: (i, 0)),
            scratch_shapes=[pltpu.VMEM((2, d, D_EXPERT), _F32), pltpu.VMEM((2, d, D_EXPERT), _F32),
                            pltpu.VMEM((2, D_EXPERT, d), _F32),
                            pltpu.VMEM((d, D_EXPERT), _BF16), pltpu.VMEM((d, D_EXPERT), _BF16),
                            pltpu.VMEM((D_EXPERT, d), _BF16), pltpu.SemaphoreType.DMA((2, 3))]),
        out_shape=jax.ShapeDtypeStruct((n_rows * SUBLANES, LANES), _F32),
        compiler_params=_cparams(1),
        name="moe_experts",
    )(*blocks, xs, w_gate, w_up, w_down)


def _combine_kernel(dest_ref, next_ref, h_ref, info_ref, ys_ref, o_ref, buf_ref, sem):
    n = h_ref.shape[0]
    step = pl.program_id(0)
    slot = step % 2

    def gather(idx_ref, to_slot):
        def issue(i, carry):
            for u in range(ROW_UNROLL):
                t = i * ROW_UNROLL + u
                _row_copy(ys_ref, idx_ref[2 * t], buf_ref.at[to_slot, 0], t,
                          sem.at[to_slot]).start(priority=0)
                _row_copy(ys_ref, idx_ref[2 * t + 1], buf_ref.at[to_slot, 1], t,
                          sem.at[to_slot]).start(priority=1)
            return carry

        lax.fori_loop(0, n // ROW_UNROLL, issue, 0)

    @pl.when(step == 0)
    def _():
        gather(dest_ref, 0)

    @pl.when(step + 1 < pl.num_programs(0))
    def _():
        gather(next_ref, 1 - slot)

    for k in range(2):
        pltpu.make_async_copy(ys_ref.at[_tile_rows(0, n)], buf_ref.at[slot, k], sem.at[slot]).wait()
    info = info_ref[...]
    w0 = info[:, 0:1]
    w1 = info[:, 1:2]
    for s in range(SUBLANES):
        sl = slice(s * LANES, (s + 1) * LANES)
        moe = w0 * _tile_block(buf_ref.at[slot, 0], s, n) + w1 * _tile_block(buf_ref.at[slot, 1], s, n)
        o_ref[:, sl] = h_ref[:, sl] + moe


def _combine(h, info, ys, dest):
    t, d = h.shape
    n = min(t, COMBINE_TOKENS)
    steps = t // n
    return pl.pallas_call(
        _combine_kernel,
        grid=(steps,),
        in_specs=[pl.BlockSpec((2 * n,), lambda i: (i,), memory_space=pltpu.SMEM),
                  pl.BlockSpec((2 * n,), lambda i: (jnp.minimum(i + 1, steps - 1),),
                               memory_space=pltpu.SMEM),
                  pl.BlockSpec((n, d), lambda i: (i, 0)),
                  pl.BlockSpec((n, LANES), lambda i: (i, 0)),
                  pl.BlockSpec(memory_space=pl.ANY)],
        out_specs=pl.BlockSpec((n, d), lambda i: (i, 0)),
        out_shape=jax.ShapeDtypeStruct((t, d), _F32),
        scratch_shapes=[pltpu.VMEM((2, 2, n * SUBLANES, LANES), _F32), pltpu.SemaphoreType.DMA((2,))],
        compiler_params=_cparams(1),
        name="moe_combine",
    )(dest, dest, h, info, ys)


def _moe_layout(route_rows, counts, t):
    counts = counts[ROUTE_LANE0:ROUTE_LANE0 + N_EXPERTS, 0].astype(jnp.int32)
    padded = (counts + ROW_BLOCK - 1) // ROW_BLOCK * ROW_BLOCK
    pends = jnp.cumsum(padded)
    pstarts = pends - padded
    eid = route_rows[2:4].astype(jnp.int32)
    rank = route_rows[4:6].astype(jnp.int32)
    experts = jnp.arange(N_EXPERTS, dtype=jnp.int32)
    start_of = jnp.sum(jnp.where(eid[:, :, None] == experts, pstarts, 0), axis=-1)
    dest = (start_of + rank).T.reshape(-1)
    n_blocks = -(-2 * t // ROW_BLOCK) + N_EXPERTS
    first_row = jnp.arange(n_blocks, dtype=jnp.int32) * ROW_BLOCK
    block_e = jnp.minimum(jnp.sum((pends[None, :] <= first_row[:, None]).astype(jnp.int32), axis=1),
                          N_EXPERTS - 1)
    n_used = (pends[-1:] // ROW_BLOCK).astype(jnp.int32)
    changed = jnp.concatenate([jnp.zeros((1,), jnp.int32), (block_e[1:] != block_e[:-1]).astype(jnp.int32)])
    block_run = jnp.cumsum(changed)
    later = (counts[None, :] > 0) & (experts[None, :] > experts[:, None])
    next_expert = jnp.min(jnp.where(later, experts[None, :], N_EXPERTS), axis=1)
    next_expert = jnp.where(next_expert < N_EXPERTS, next_expert, -1)
    block_next = jnp.sum(jnp.where(block_e[:, None] == experts[None, :], next_expert[None, :], 0), axis=1)
    blocks = (block_e, block_run.astype(jnp.int32), block_next.astype(jnp.int32), n_used)
    return dest, blocks, pstarts + counts, padded - counts, n_blocks * ROW_BLOCK


def kernel(x, mem, positions, mix_norm_g, w_in, qn_a, kn_a, rel_bias, ret_gn_g, mem_norm_g, w_mem_kv,
           qn_c, kn_c, w_out, ffn_norm_g, w_router_group, b_router_group, w_router_expert,
           b_router_expert, w_gate, w_up, w_down):
    b, s, d = x.shape
    t = b * s
    x2 = x.reshape(t, d)
    cos, sin = _rope_tables(positions)
    kc, vc = _mem_kv(mem, mem_norm_g, w_mem_kv, kn_c)
    qa, ka, vta, proj_b, qc = _in_proj(x, mix_norm_g, w_in, qn_a, kn_a, qn_c)
    out_a = _attention(qa, ka, vta, rel_bias)
    out_b = _retention(proj_b, cos, sin, ret_gn_g)
    out_c = _cross_attention(qc, kc, vc)
    h, hn, info, route_rows, counts = _out_router(
        x2, out_a.reshape(t, A_WIDTH), out_b.reshape(t, B_WIDTH), out_c.reshape(t, C_WIDTH),
        w_out, ffn_norm_g, w_router_group, b_router_group, w_router_expert, b_router_expert)
    dest, blocks, pad_start, pad_len, n_rows = _moe_layout(route_rows, counts, t)
    xs = _dispatch(hn, dest, pad_start, pad_len, blocks[-1], n_rows)
    ys = _experts(xs, blocks, w_gate, w_up, w_down)
    return _combine(h, info, ys, dest).reshape(b, s, d)
```

```python
import functools

import jax
import jax.numpy as jnp
from jax import lax
from jax.experimental import pallas as pl
from jax.experimental.pallas import tpu as pltpu

D_MODEL = 1024
CHUNK = 64
HEAD_DIM = 64
A_HEADS = 8
B_HEADS = 4
C_HEADS = 4
A_WIDTH = A_HEADS * HEAD_DIM
B_WIDTH = B_HEADS * HEAD_DIM
C_WIDTH = C_HEADS * HEAD_DIM
IN_COLS = 3 * A_WIDTH + 4 * B_WIDTH + C_WIDTH
LEFT_CHUNKS = 8
BAND_CHUNKS = LEFT_CHUNKS + 1
MAX_REL_DIST = 128
ROPE_BASE = 10000.0
N_GROUPS = 4
EXPERTS_PER_GROUP = 8
N_EXPERTS = N_GROUPS * EXPERTS_PER_GROUP
D_EXPERT = D_MODEL // 2
EPS = 1e-6
NEG_INF = -1e30
LOG2E = 1.4426950408889634

LANES = 128
SUBLANES = 8
assert D_MODEL == SUBLANES * LANES
PACK_ROWS = SUBLANES // 2
LEFT_ROWS = LEFT_CHUNKS * CHUNK
ATT_Q = 2 * CHUNK
ATT_K = ATT_Q + LEFT_ROWS
ATT_VARIANTS = LEFT_ROWS // ATT_Q + 1
ONES_ROWS = 16
ATT_AHEAD = 3
RET_CHUNK = 256
ROW_BLOCK = 512
EXPERT_SPLIT = 2
EXPERT_AHEAD = 2
ROUTE_LANE0 = N_GROUPS
ROUTE_ROWS = 64
VMEM_LIMIT = 48 * 1024 * 1024

_F32 = jnp.float32
_BF16 = jnp.bfloat16


def _cparams(n_axes):
    return pltpu.CompilerParams(dimension_semantics=("arbitrary",) * n_axes,
                                vmem_limit_bytes=VMEM_LIMIT)


def _dot(a, b):
    return jnp.dot(a, b, preferred_element_type=_F32)


def _dot_nt(a, b):
    return lax.dot_general(a, b, (((1,), (1,)), ((), ())), preferred_element_type=_F32)


def _lane(shape):
    return lax.broadcasted_iota(jnp.int32, shape, len(shape) - 1)


def _pair_rms(t, gain):
    low = _lane(t.shape) < HEAD_DIM
    t2 = t * t
    ms0 = jnp.sum(jnp.where(low, t2, 0.0), axis=-1, keepdims=True) * (1.0 / HEAD_DIM)
    ms1 = jnp.sum(jnp.where(low, 0.0, t2), axis=-1, keepdims=True) * (1.0 / HEAD_DIM)
    r = jnp.where(low, lax.rsqrt(ms0 + EPS), lax.rsqrt(ms1 + EPS))
    return (t * r) * gain


def _rows_to_tiles(ref, val, row0=0):
    n = val.shape[0]
    for s in range(SUBLANES):
        ref[pl.ds(row0 * SUBLANES + s, n, stride=SUBLANES), :] = val[:, s * LANES:(s + 1) * LANES]


def _tile_block(ref, s, n, row0=0):
    return ref[pl.ds(row0 * SUBLANES + s, n, stride=SUBLANES), :]


def _tiles_to_rows(ref, n, row0=0):
    return jnp.concatenate([_tile_block(ref, s, n, row0) for s in range(SUBLANES)], axis=-1)


def _pack_rows(ref, val, row0=0):
    n = val.shape[0]
    for s in range(PACK_ROWS):
        lo = val[:, (2 * s) * LANES:(2 * s + 1) * LANES].astype(_BF16).astype(_F32)
        hi = val[:, (2 * s + 1) * LANES:(2 * s + 2) * LANES].astype(_BF16).astype(_F32)
        word = (lax.bitcast_convert_type(lo, jnp.uint32) >> 16) | (
            lax.bitcast_convert_type(hi, jnp.uint32) & jnp.uint32(0xFFFF0000))
        ref[pl.ds(row0 * PACK_ROWS + s, n, stride=PACK_ROWS), :] = word


def _unpack_rows(ref, n, row0=0):
    parts = []
    for s in range(PACK_ROWS):
        word = ref[pl.ds(row0 * PACK_ROWS + s, n, stride=PACK_ROWS), :]
        parts.append(lax.bitcast_convert_type(word << 16, _F32))
        parts.append(lax.bitcast_convert_type(word & jnp.uint32(0xFFFF0000), _F32))
    return jnp.concatenate(parts, axis=-1).astype(_BF16)


ROPE_HALF = HEAD_DIM // 2
ROPE_PACK = LANES // ROPE_HALF


def _rope_kernel(pos_ref, inv_ref, cos_ref, sin_ref):
    ang = pos_ref[...].astype(_F32) * inv_ref[...]
    rows = ang.shape[0]
    lane = _lane(ang.shape)
    sign = jnp.where((lane % HEAD_DIM) < ROPE_HALF, -1.0, 1.0)
    for out_ref, val in ((cos_ref, jnp.cos(ang)), (sin_ref, jnp.sin(ang))):
        for j in range(ROPE_PACK):
            seg = jnp.where(lane // ROPE_HALF == j, val, 0.0)
            full = seg
            for k in range(1, ROPE_PACK):
                full = full + pltpu.roll(seg, k * ROPE_HALF, 1)
            if out_ref is sin_ref:
                full = full * sign
            out_ref[pl.ds(j, rows, stride=ROPE_PACK), :] = full


def _rope_tables(positions):
    t = positions.size
    inv = ROPE_BASE ** (-jnp.arange(ROPE_HALF, dtype=_F32) / ROPE_HALF)
    inv128 = jnp.tile(inv, ROPE_PACK).reshape(1, LANES)
    rows = t // ROPE_PACK
    pos = jnp.repeat(positions.reshape(rows, ROPE_PACK), ROPE_HALF, axis=1)
    tm = min(rows, 512)
    out = pl.BlockSpec((tm * ROPE_PACK, LANES), lambda i: (i, 0))
    return pl.pallas_call(
        _rope_kernel,
        grid=(rows // tm,),
        in_specs=[pl.BlockSpec((tm, LANES), lambda i: (i, 0)), pl.BlockSpec((1, LANES), lambda i: (0, 0))],
        out_specs=[out, out],
        out_shape=[jax.ShapeDtypeStruct((t, LANES), _F32)] * 2,
        compiler_params=_cparams(1),
        name="rope_tables",
    )(pos, inv128)


def _mem_kv_kernel(mem_ref, g_ref, w_ref, kn_ref, k_ref, v_ref):
    m = mem_ref[...]
    ms = jnp.mean(m * m, axis=-1, keepdims=True)
    mn = (m * lax.rsqrt(ms + EPS)) * g_ref[...]
    kv = _dot(mn.astype(_BF16), w_ref[...])
    for j in range(C_WIDTH // LANES):
        sl = slice(j * LANES, (j + 1) * LANES)
        k_ref[:, sl] = _pair_rms(kv[:, sl], kn_ref[...]).astype(_BF16)
    v_ref[...] = kv[:, C_WIDTH:].T.astype(_BF16)


def _mem_kv(mem, mem_norm_g, w_mem_kv, kn_c):
    b, m, d = mem.shape
    kn = jnp.tile(kn_c, 2).reshape(1, LANES)
    return pl.pallas_call(
        _mem_kv_kernel,
        grid=(b,),
        in_specs=[pl.BlockSpec((None, m, d), lambda i: (i, 0, 0)),
                  pl.BlockSpec((1, d), lambda i: (0, 0)),
                  pl.BlockSpec((d, 2 * C_WIDTH), lambda i: (0, 0)),
                  pl.BlockSpec((1, LANES), lambda i: (0, 0))],
        out_specs=[pl.BlockSpec((None, m, C_WIDTH), lambda i: (i, 0, 0)),
                   pl.BlockSpec((None, C_WIDTH, m), lambda i: (i, 0, 0))],
        out_shape=[jax.ShapeDtypeStruct((b, m, C_WIDTH), _BF16),
                   jax.ShapeDtypeStruct((b, C_WIDTH, m), _BF16)],
        compiler_params=_cparams(1),
        name="mem_kv",
    )(mem, mem_norm_g.reshape(1, d), w_mem_kv.astype(_BF16), kn)


def _in_proj_kernel(x_ref, cos_ref, sin_ref, g_ref, wq_ref, wk_ref, wvt_ref, wr_ref, wc_ref, qn_ref, kn_ref,
                    cn_ref, zeta_ref, qa_ref, ka_ref, vt_ref, ret_ref, gate_ref, qc_ref, xn_ref, acc_ref,
                    accb_ref):
    x = x_ref[...]
    ms = jnp.mean(x * x, axis=-1, keepdims=True)
    xn_ref[...] = ((x * lax.rsqrt(ms + EPS)) * g_ref[...]).astype(_BF16)

    def normed(slot, out_ref, gain_ref):
        for blk in range(out_ref.shape[1] // LANES):
            sl = slice(blk * LANES, (blk + 1) * LANES)
            out_ref[:, sl] = _pair_rms(acc_ref[slot, :, sl], gain_ref[...]).astype(_BF16)

    acc_ref[0] = _dot(xn_ref[...], wq_ref[...])
    acc_ref[1] = _dot(xn_ref[...], wk_ref[...])
    normed(0, qa_ref, qn_ref)
    acc_ref[0] = _dot_nt(wvt_ref[...], xn_ref[...])
    normed(1, ka_ref, kn_ref)
    accb_ref[...] = _dot(xn_ref[...], wr_ref[...])
    for blk in range(vt_ref.shape[0]):
        vt_ref[blk] = acc_ref[0, :, blk * LANES:(blk + 1) * LANES].astype(_BF16)
    acc_ref[1, :, 0:C_WIDTH] = _dot(xn_ref[...], wc_ref[...])
    cos, sin = cos_ref[...], sin_ref[...]
    chunks = x_ref.shape[0] // RET_CHUNK
    for p in range(B_WIDTH // LANES):
        sl = slice(p * LANES, (p + 1) * LANES)
        q = accb_ref[:, sl]
        k = accb_ref[:, B_WIDTH + p * LANES:B_WIDTH + (p + 1) * LANES]
        kr = (k * cos + _swap_halves(k) * sin) * (HEAD_DIM ** -0.5)
        ret_ref[:, sl] = (q * cos + _swap_halves(q) * sin).astype(_BF16)
        ret_ref[:, B_WIDTH + p * LANES:B_WIDTH + (p + 1) * LANES] = kr.astype(_BF16)
        ret_ref[:, 2 * B_WIDTH + p * LANES:2 * B_WIDTH + (p + 1) * LANES] = (
            kr * jnp.concatenate([zeta_ref[p]] * chunks, axis=0)).astype(_BF16)
        ret_ref[:, 3 * B_WIDTH + p * LANES:3 * B_WIDTH + (p + 1) * LANES] = accb_ref[
            :, 2 * B_WIDTH + p * LANES:2 * B_WIDTH + (p + 1) * LANES].astype(_BF16)
        gate_ref[:, sl] = accb_ref[:, 3 * B_WIDTH + p * LANES:3 * B_WIDTH + (p + 1) * LANES]
    normed(1, qc_ref, cn_ref)


def _in_proj(x3, cos, sin, zeta, g, w_in, qn_a, kn_a, qn_c):
    b, s, d = x3.shape
    tm = min(s, 512)
    assert tm == A_WIDTH
    assert tm % RET_CHUNK == 0
    cos3, sin3 = cos.reshape(b, s, LANES), sin.reshape(b, s, LANES)
    w = w_in.astype(_BF16)
    cuts = [0, A_WIDTH, 2 * A_WIDTH, 3 * A_WIDTH, 3 * A_WIDTH + 4 * B_WIDTH, IN_COLS]
    wq, wk, wv, wr, wc = (w[:, lo:hi] for lo, hi in zip(cuts[:-1], cuts[1:]))
    scale = HEAD_DIM ** -0.5 * LOG2E
    gains = [(jnp.tile(gn, 2) * sc).reshape(1, LANES) for gn, sc in ((qn_a, scale), (kn_a, 1.0), (qn_c, scale))]

    def whole(arr):
        return pl.BlockSpec(arr.shape, lambda i, j: (0,) * arr.ndim)

    def rows(width):
        return pl.BlockSpec((None, tm, width), lambda i, j: (i, j, 0))

    consts = [g.reshape(1, d), wq, wk, wv.T, wr, wc] + gains + [zeta]
    return pl.pallas_call(
        _in_proj_kernel,
        grid=(b, s // tm),
        in_specs=[rows(d), rows(LANES), rows(LANES)] + [whole(c) for c in consts],
        out_specs=[rows(A_WIDTH), rows(A_WIDTH),
                   pl.BlockSpec((None, tm // LANES, A_WIDTH, LANES), lambda i, j: (i, j, 0, 0)),
                   rows(4 * B_WIDTH), rows(B_WIDTH), rows(C_WIDTH)],
        out_shape=[jax.ShapeDtypeStruct((b, s, A_WIDTH), _BF16), jax.ShapeDtypeStruct((b, s, A_WIDTH), _BF16),
                   jax.ShapeDtypeStruct((b, s // LANES, A_WIDTH, LANES), _BF16),
                   jax.ShapeDtypeStruct((b, s, 4 * B_WIDTH), _BF16), jax.ShapeDtypeStruct((b, s, B_WIDTH), _F32),
                   jax.ShapeDtypeStruct((b, s, C_WIDTH), _BF16)],
        scratch_shapes=[pltpu.VMEM((tm, d), _BF16), pltpu.VMEM((2, tm, A_WIDTH), _F32),
                        pltpu.VMEM((tm, 4 * B_WIDTH), _F32)],
        compiler_params=_cparams(2),
        name="in_proj",
    )(x3, cos3, sin3, *consts)


def _attn_kernel(q_ref, k_ref, vt_ref, bias_ref, o_ref, kp_ref, st_ref, var_ref, *, q_rows):
    qs = pl.program_id(2)
    s = k_ref.shape[0]
    fill_rows = min(s, 1024)
    left_blocks = LEFT_ROWS // LANES

    @pl.when(qs == 0)
    def _():
        kp_ref[0:LEFT_ROWS, :] = jnp.zeros((LEFT_ROWS, LANES), _BF16)

        def fill(i, carry):
            r = pl.multiple_of(i * fill_rows, fill_rows)
            kp_ref[pl.ds(LEFT_ROWS + r, fill_rows), :] = k_ref[pl.ds(r, fill_rows), :]
            return carry

        lax.fori_loop(0, s // fill_rows, fill, 0)
        key = lax.broadcasted_iota(jnp.int32, (ATT_K, 2 * ATT_Q), 0)
        for v in range(ATT_VARIANTS):
            var_ref[v] = jnp.where(key >= LEFT_ROWS - ATT_Q * v, bias_ref[...], NEG_INF)

    low = _lane((ATT_Q, LANES)) < HEAD_DIM
    ones = jnp.ones((ONES_ROWS, ATT_K), _BF16)
    tiles_per_step = q_rows // ATT_Q

    def scores(j):
        cp = qs * tiles_per_step + j
        q = q_ref[j * ATT_Q:(j + 1) * ATT_Q, :]
        q2 = jnp.concatenate([jnp.where(low, q, jnp.zeros_like(q)), jnp.where(low, jnp.zeros_like(q), q)], axis=0)
        kb = kp_ref[pl.ds(pl.multiple_of(cp * ATT_Q, ATT_Q), ATT_K), :]
        st_ref[j % (ATT_AHEAD + 1)] = _dot_nt(kb, q2) + var_ref[jnp.minimum(cp, ATT_VARIANTS - 1)]

    def finish(j):
        cp = qs * tiles_per_step + j
        st = st_ref[j % (ATT_AHEAD + 1)]
        m = jnp.max(st, axis=0, keepdims=True)
        p = jnp.exp2(st - m)
        vt = jnp.concatenate([vt_ref[jnp.maximum(cp + kb_i - left_blocks, 0)] for kb_i in range(ATT_K // LANES)],
                             axis=1)
        ot = _dot(jnp.concatenate([vt, ones], axis=0), p.astype(_BF16))
        inv = 1.0 / ot[LANES:LANES + 1, :]
        out_t = jnp.concatenate([ot[0:HEAD_DIM, 0:ATT_Q] * inv[:, 0:ATT_Q],
                                 ot[HEAD_DIM:LANES, ATT_Q:] * inv[:, ATT_Q:]], axis=0)
        o_ref[j * ATT_Q:(j + 1) * ATT_Q, :] = out_t.T.astype(o_ref.dtype)

    for j in range(min(ATT_AHEAD, tiles_per_step)):
        scores(j)
    for j in range(tiles_per_step):
        if j + ATT_AHEAD < tiles_per_step:
            scores(j + ATT_AHEAD)
        finish(j)


def _toeplitz_bias(rel_bias, q_len, k_len):
    h, table = rel_bias.shape
    n_diag = q_len + k_len - 1
    flat_lo = k_len - 1 - LEFT_ROWS - (CHUNK - 1)
    flat_hi = n_diag - flat_lo - table
    rev = jnp.concatenate([jnp.broadcast_to(rel_bias[:, -1:], (h, flat_hi)), rel_bias[:, ::-1],
                           jnp.broadcast_to(rel_bias[:, :1], (h, flat_lo))], axis=1).astype(_F32)
    flat = jnp.tile(rev, (1, q_len + 1))
    pitch = n_diag - 1
    skew = flat[:, q_len - 1:q_len - 1 + q_len * pitch].reshape(h, q_len, pitch)
    return skew[:, :, :k_len]


def _attn_bias(rel_bias):
    h = rel_bias.shape[0]
    bias = _toeplitz_bias(rel_bias, ATT_Q, ATT_K)
    q = lax.broadcasted_iota(jnp.int32, (ATT_Q, ATT_K), 0)
    k = lax.broadcasted_iota(jnp.int32, (ATT_Q, ATT_K), 1)
    off = k // CHUNK - q // CHUNK
    in_band = (off >= 0) & (off < BAND_CHUNKS)
    full = jnp.where(in_band[None], bias * LOG2E, NEG_INF)
    full = full.reshape(h // 2, 2, ATT_Q, ATT_K)
    return full.transpose(0, 3, 1, 2).reshape(h // 2, ATT_K, 2 * ATT_Q)


def _attention(qa, ka, vta, rel_bias):
    b, s, _ = qa.shape
    q_rows = min(s, 2048)
    pairs = A_HEADS // 2
    return pl.pallas_call(
        functools.partial(_attn_kernel, q_rows=q_rows),
        grid=(b, pairs, s // q_rows),
        in_specs=[pl.BlockSpec((None, q_rows, LANES), lambda i, p, j: (i, j, p)),
                  pl.BlockSpec((None, s, LANES), lambda i, p, j: (i, 0, p)),
                  pl.BlockSpec((None, s // LANES, LANES, LANES), lambda i, p, j: (i, 0, p, 0)),
                  pl.BlockSpec((None, ATT_K, 2 * ATT_Q), lambda i, p, j: (p, 0, 0))],
        out_specs=pl.BlockSpec((None, q_rows, LANES), lambda i, p, j: (i, j, p)),
        out_shape=jax.ShapeDtypeStruct((b, s, A_WIDTH), _BF16),
        scratch_shapes=[pltpu.VMEM((s + LEFT_ROWS, LANES), _BF16),
                        pltpu.VMEM((ATT_AHEAD + 1, ATT_K, 2 * ATT_Q), _F32),
                        pltpu.VMEM((ATT_VARIANTS, ATT_K, 2 * ATT_Q), _F32)],
        compiler_params=_cparams(3),
        name="attn_a",
    )(qa, ka, vta, _attn_bias(rel_bias))


def _swap_halves(t):
    first = (_lane(t.shape) % HEAD_DIM) < (HEAD_DIM // 2)
    return jnp.where(first, pltpu.roll(t, LANES - HEAD_DIM // 2, 1), pltpu.roll(t, HEAD_DIM // 2, 1))


def _retention_kernel(q_ref, k_ref, kz_ref, v_ref, gate_ref, decay_ref, xi_ref, cd_ref, gn_ref, o_ref,
                      state_ref, *, rows):
    @pl.when(pl.program_id(2) == 0)
    def _():
        state_ref[...] = jnp.zeros_like(state_ref)

    c = RET_CHUNK
    low = _lane((c, LANES)) < HEAD_DIM
    eye = jnp.where(lax.broadcasted_iota(jnp.int32, (LANES, LANES), 0) == _lane((LANES, LANES)),
                    1.0, 0.0).astype(_BF16)
    srow = lax.broadcasted_iota(jnp.int32, (LANES, LANES), 0) < HEAD_DIM
    scol = _lane((LANES, LANES)) < HEAD_DIM
    same_head = srow == scol

    for j in range(rows // c):
        sl = slice(j * c, (j + 1) * c)
        qb = q_ref[sl, :]
        kb = k_ref[sl, :]
        vb = v_ref[sl, :]
        inner_out = []
        for h in range(2):
            qh = jnp.where(low if h == 0 else ~low, qb, jnp.zeros_like(qb))
            inner = _dot_nt(qh, kb) * decay_ref[h]
            inner_out.append(_dot(inner.astype(_BF16), vb))
        state = state_ref[...]
        cross = _dot(qb, state.astype(_BF16)) * xi_ref[...]
        o = jnp.where(low, inner_out[0], inner_out[1]) + cross
        kz = _dot_nt(eye, kz_ref[sl, :]).astype(_BF16)
        state_ref[...] = cd_ref[...] * state + jnp.where(same_head, _dot(kz, vb), 0.0)
        mu = jnp.where(low,
                       jnp.sum(jnp.where(low, o, 0.0), axis=-1, keepdims=True),
                       jnp.sum(jnp.where(low, 0.0, o), axis=-1, keepdims=True)) * (1.0 / HEAD_DIM)
        dlt = o - mu
        d2 = dlt * dlt
        var = jnp.where(low,
                        jnp.sum(jnp.where(low, d2, 0.0), axis=-1, keepdims=True),
                        jnp.sum(jnp.where(low, 0.0, d2), axis=-1, keepdims=True)) * (1.0 / HEAD_DIM)
        y = (dlt * lax.rsqrt(var + EPS)) * gn_ref[...]
        g = gate_ref[sl, :]
        o_ref[sl, :] = ((g * jax.nn.sigmoid(g)) * y).astype(o_ref.dtype)


def _retention_tables():
    c = RET_CHUNK
    log_g = jnp.log(1.0 - jnp.exp2(-5.0 - jnp.arange(B_HEADS, dtype=_F32)))
    idx = jnp.arange(c, dtype=_F32)
    diff = idx[:, None] - idx[None, :]
    decay = jnp.where(diff >= 0, jnp.exp(log_g[:, None, None] * jnp.maximum(diff, 0.0)), 0.0)
    zeta = jnp.exp(log_g[:, None] * (c - 1 - idx))
    xi = jnp.exp(log_g[:, None] * (idx + 1.0))
    cd = jnp.exp(log_g * c)

    def lanes(tab):
        return jnp.repeat(tab.reshape(B_HEADS // 2, 2, c), HEAD_DIM, axis=1).transpose(0, 2, 1)

    cdm = jnp.repeat(cd.reshape(B_HEADS // 2, 2), HEAD_DIM, axis=1)
    cdm = jnp.broadcast_to(cdm[:, :, None], (B_HEADS // 2, LANES, LANES))
    return decay, lanes(zeta), lanes(xi), cdm


def _retention(qkzv, gate, tables, ret_gn_g):
    b, s, _ = qkzv.shape
    rows = min(s, 2048)
    pairs = B_HEADS // 2
    decay, _, xi, cdm = tables
    gn = ret_gn_g.reshape(pairs, 1, LANES)

    def col(off):
        return pl.BlockSpec((None, rows, LANES), lambda i, p, j: (i, j, off * pairs + p))

    return pl.pallas_call(
        functools.partial(_retention_kernel, rows=rows),
        grid=(b, pairs, s // rows),
        in_specs=[col(0), col(1), col(2), col(3), col(0),
                  pl.BlockSpec((2, RET_CHUNK, RET_CHUNK), lambda i, p, j: (p, 0, 0)),
                  pl.BlockSpec((None, RET_CHUNK, LANES), lambda i, p, j: (p, 0, 0)),
                  pl.BlockSpec((None, LANES, LANES), lambda i, p, j: (p, 0, 0)),
                  pl.BlockSpec((None, 1, LANES), lambda i, p, j: (p, 0, 0))],
        out_specs=pl.BlockSpec((None, rows, LANES), lambda i, p, j: (i, j, p)),
        out_shape=jax.ShapeDtypeStruct((b, s, B_WIDTH), _BF16),
        scratch_shapes=[pltpu.VMEM((LANES, LANES), _F32)],
        compiler_params=_cparams(3),
        name="retention_b",
    )(qkzv, qkzv, qkzv, qkzv, gate, decay, xi, cdm, gn)


def _cross_kernel(q_ref, k_ref, vt_ref, o_ref, st_ref, *, rows):
    low = _lane((ATT_Q, LANES)) < HEAD_DIM
    lane_blocks = C_WIDTH // LANES
    tiles = [(j, lb) for j in range(rows // ATT_Q) for lb in range(lane_blocks)]
    ones = jnp.ones((ONES_ROWS, vt_ref.shape[1]), _BF16)

    def scores(i):
        j, lb = tiles[i]
        sl = slice(lb * LANES, (lb + 1) * LANES)
        q = q_ref[j * ATT_Q:(j + 1) * ATT_Q, sl]
        q2 = jnp.concatenate([jnp.where(low, q, jnp.zeros_like(q)), jnp.where(low, jnp.zeros_like(q), q)], axis=0)
        st_ref[i % (ATT_AHEAD + 1)] = _dot_nt(k_ref[:, sl], q2)

    def finish(i):
        j, lb = tiles[i]
        sl = slice(lb * LANES, (lb + 1) * LANES)
        st = st_ref[i % (ATT_AHEAD + 1)]
        p = jnp.exp2(st - jnp.max(st, axis=0, keepdims=True))
        ot = _dot(jnp.concatenate([vt_ref[sl, :], ones], axis=0), p.astype(_BF16))
        inv = 1.0 / ot[LANES:LANES + 1, :]
        out_t = jnp.concatenate([ot[0:HEAD_DIM, 0:ATT_Q] * inv[:, 0:ATT_Q],
                                 ot[HEAD_DIM:LANES, ATT_Q:] * inv[:, ATT_Q:]], axis=0)
        o_ref[j * ATT_Q:(j + 1) * ATT_Q, sl] = out_t.T.astype(o_ref.dtype)

    for i in range(min(ATT_AHEAD, len(tiles))):
        scores(i)
    for i in range(len(tiles)):
        if i + ATT_AHEAD < len(tiles):
            scores(i + ATT_AHEAD)
        finish(i)


def _cross_attention(qc, kc, vtc):
    b, s, _ = qc.shape
    m = kc.shape[1]
    rows = min(s, 1024)
    return pl.pallas_call(
        functools.partial(_cross_kernel, rows=rows),
        grid=(b, s // rows),
        in_specs=[pl.BlockSpec((None, rows, C_WIDTH), lambda i, j: (i, j, 0)),
                  pl.BlockSpec((None, m, C_WIDTH), lambda i, j: (i, 0, 0)),
                  pl.BlockSpec((None, C_WIDTH, m), lambda i, j: (i, 0, 0))],
        out_specs=pl.BlockSpec((None, rows, C_WIDTH), lambda i, j: (i, j, 0)),
        out_shape=jax.ShapeDtypeStruct((b, s, C_WIDTH), _BF16),
        scratch_shapes=[pltpu.VMEM((ATT_AHEAD + 1, m, 2 * ATT_Q), _F32)],
        compiler_params=_cparams(2),
        name="cross_c",
    )(qc, kc, vtc)


def _out_router_kernel(x_ref, a_ref, b_ref, c_ref, wo_ref, g_ref, wr_ref, br_ref,
                       h_ref, hn_ref, info_ref, rows_ref, cnt_ref, carry_ref):
    @pl.when(pl.program_id(0) == 0)
    def _():
        carry_ref[...] = jnp.zeros_like(carry_ref)

    tm = x_ref.shape[0]
    h = x_ref[...]
    h = h + _dot(a_ref[...], wo_ref[0:A_WIDTH, :])
    h = h + _dot(b_ref[...], wo_ref[A_WIDTH:A_WIDTH + B_WIDTH, :])
    h = h + _dot(c_ref[...], wo_ref[A_WIDTH + B_WIDTH:, :])
    h_ref[...] = h
    ms = jnp.mean(h * h, axis=-1, keepdims=True)
    hn = (h * lax.rsqrt(ms + EPS)) * g_ref[...]
    _pack_rows(hn_ref, hn)
    logits = _dot_nt(wr_ref[...], hn.astype(_BF16))[0:ROUTE_ROWS, :] + br_ref[:, 0:1]
    row = lax.broadcasted_iota(jnp.int32, (ROUTE_ROWS, tm), 0).astype(_F32)
    big = float(ROUTE_ROWS)

    def first_row(mask):
        return jnp.min(jnp.where(mask, row, big), axis=0, keepdims=True)

    gmask = row < N_GROUPS
    gl = jnp.where(gmask, logits, NEG_INF)
    ge = jnp.exp(gl - jnp.max(gl, axis=0, keepdims=True))
    gp = ge / jnp.sum(ge, axis=0, keepdims=True)
    p_group = jnp.max(gp, axis=0, keepdims=True)
    g_sel = first_row(gmask & (gp == p_group))
    lo = ROUTE_LANE0 + g_sel * EXPERTS_PER_GROUP
    emask = (row >= lo) & (row < lo + EXPERTS_PER_GROUP)
    el = jnp.where(emask, logits, NEG_INF)
    ee = jnp.exp(el - jnp.max(el, axis=0, keepdims=True))
    ep = ee / jnp.sum(ee, axis=0, keepdims=True)
    p1 = jnp.max(ep, axis=0, keepdims=True)
    i1 = first_row(emask & (ep == p1))
    ep2 = jnp.where(emask & (row != i1), ep, -1.0)
    p2 = jnp.max(ep2, axis=0, keepdims=True)
    i2 = first_row(ep2 == p2)
    den = p1 + p2
    w1 = p_group * (p1 / den)
    w2 = p_group * (p2 / den)
    hit1 = row == i1
    hit2 = row == i2
    onehot = jnp.where(hit1 | hit2, 1.0, 0.0)
    r_i = lax.broadcasted_iota(jnp.int32, (tm, tm), 0)
    c_i = lax.broadcasted_iota(jnp.int32, (tm, tm), 1)
    earlier = jnp.where(r_i < c_i, 1.0, 0.0).astype(_BF16)
    before = _dot(onehot.astype(_BF16), earlier) + carry_ref[:, 0:1]
    r1 = jnp.sum(jnp.where(hit1, before, 0.0), axis=0, keepdims=True)
    r2 = jnp.sum(jnp.where(hit2, before, 0.0), axis=0, keepdims=True)
    carry_ref[...] = carry_ref[...] + jnp.sum(onehot, axis=1, keepdims=True)
    cnt_ref[...] = carry_ref[...]
    out_row = lax.broadcasted_iota(jnp.int32, (LANES, tm), 0)
    info = jnp.where(out_row == 0, w1, 0.0)
    info = jnp.where(out_row == 1, w2, info)
    info = jnp.where(out_row == 2, i1 - ROUTE_LANE0, info)
    info = jnp.where(out_row == 3, i2 - ROUTE_LANE0, info)
    info = jnp.where(out_row == 4, r1, info)
    info = jnp.where(out_row == 5, r2, info)
    rows_ref[...] = info[0:SUBLANES, :]
    info_ref[...] = info.T


def _out_router(x2, oa, ob, oc, w_out, ffn_g, w_rg, b_rg, w_re, b_re):
    t, d = x2.shape
    tm = min(t, 512)
    pad = LANES - N_GROUPS - N_EXPERTS
    wr = jnp.concatenate([w_rg, w_re, jnp.zeros((d, pad), _F32)], axis=1).T.astype(_BF16)
    br = jnp.concatenate([b_rg, b_re, jnp.zeros((ROUTE_ROWS - N_GROUPS - N_EXPERTS,), _F32)])
    br = jnp.broadcast_to(br[:, None], (ROUTE_ROWS, LANES))

    def rows(w):
        return pl.BlockSpec((tm, w), lambda i: (i, 0))

    def whole(r, c):
        return pl.BlockSpec((r, c), lambda i: (0, 0))

    return pl.pallas_call(
        _out_router_kernel,
        grid=(t // tm,),
        in_specs=[rows(d), rows(A_WIDTH), rows(B_WIDTH), rows(C_WIDTH), whole(d, d), whole(1, d),
                  whole(LANES, d), whole(ROUTE_ROWS, LANES)],
        out_specs=[rows(d), pl.BlockSpec((tm * PACK_ROWS, LANES), lambda i: (i, 0)), rows(LANES),
                   pl.BlockSpec((SUBLANES, tm), lambda i: (0, i)), whole(ROUTE_ROWS, LANES)],
        out_shape=[jax.ShapeDtypeStruct((t, d), _F32), jax.ShapeDtypeStruct((t * PACK_ROWS, LANES), jnp.uint32),
                   jax.ShapeDtypeStruct((t, LANES), _F32), jax.ShapeDtypeStruct((SUBLANES, t), _F32),
                   jax.ShapeDtypeStruct((ROUTE_ROWS, LANES), _F32)],
        scratch_shapes=[pltpu.VMEM((ROUTE_ROWS, LANES), _F32)],
        compiler_params=_cparams(1),
        name="out_router",
    )(x2, oa, ob, oc, w_out.astype(_BF16), ffn_g.reshape(1, d), wr, br)


DISPATCH_TOKENS = 2048
COMBINE_TOKENS = 512


ROW_UNROLL = 8


def _tile_rows(row, count=1, per=SUBLANES):
    start = row * per
    if not isinstance(start, int):
        start = pl.multiple_of(start, per)
    return pl.ds(start, count * per)


def _row_copy(src, s_row, dst, d_row, sem, per=SUBLANES):
    return pltpu.make_async_copy(src.at[_tile_rows(s_row, 1, per)], dst.at[_tile_rows(d_row, 1, per)], sem)


def _dispatch_kernel(pad_start_ref, pad_len_ref, used_ref, dest_ref, hn_ref, xs_ref, zero_ref, sem,
                     pad_sem):
    per = PACK_ROWS
    n = hn_ref.shape[0] // per

    @pl.when(pl.program_id(0) == 0)
    def _():
        zero_ref[...] = jnp.zeros_like(zero_ref)
        n_blocks = xs_ref.shape[0] // (ROW_BLOCK * per)

        def block_copy(blk):
            return pltpu.make_async_copy(zero_ref, xs_ref.at[_tile_rows(blk * ROW_BLOCK, ROW_BLOCK, per)],
                                         pad_sem)

        def put_block(blk, carry):
            block_copy(blk).start()
            return carry

        def done_block(blk, carry):
            block_copy(blk).wait()
            return carry

        lax.fori_loop(used_ref[0], n_blocks, put_block, 0)
        lax.fori_loop(used_ref[0], n_blocks, done_block, 0)
        bits = [1 << k for k in reversed(range(ROW_BLOCK.bit_length() - 1))]

        def tail(e, wait):
            row = pad_start_ref[e]
            for bit in bits:
                on = (pad_len_ref[e] & bit) != 0
                copy = pltpu.make_async_copy(zero_ref.at[_tile_rows(0, bit, per)],
                                             xs_ref.at[_tile_rows(row, bit, per)], pad_sem)

                @pl.when(on)
                def _():
                    copy.wait() if wait else copy.start()

                row = row + jnp.where(on, bit, 0)

        def put_tail(e, carry):
            tail(e, False)
            return carry

        def done_tail(e, carry):
            tail(e, True)
            return carry

        lax.fori_loop(0, N_EXPERTS, put_tail, 0)
        lax.fori_loop(0, N_EXPERTS, done_tail, 0)

    def issue(i, carry):
        for u in range(ROW_UNROLL):
            t = i * ROW_UNROLL + u
            _row_copy(hn_ref, t, xs_ref, dest_ref[2 * t], sem, per).start(priority=0)
            _row_copy(hn_ref, t, xs_ref, dest_ref[2 * t + 1], sem, per).start(priority=1)
        return carry

    lax.fori_loop(0, n // ROW_UNROLL, issue, 0)
    for _ in range(2):
        pltpu.make_async_copy(hn_ref, xs_ref.at[_tile_rows(0, n, per)], sem).wait()


def _dispatch(hn, dest, pad_start, pad_len, n_used, n_rows):
    t = hn.shape[0] // PACK_ROWS
    n = min(t, DISPATCH_TOKENS)
    return pl.pallas_call(
        _dispatch_kernel,
        grid_spec=pltpu.PrefetchScalarGridSpec(
            num_scalar_prefetch=3,
            grid=(t // n,),
            in_specs=[pl.BlockSpec((2 * n,), lambda i, *_: (i,), memory_space=pltpu.SMEM),
                      pl.BlockSpec((n * PACK_ROWS, LANES), lambda i, *_: (i, 0))],
            out_specs=pl.BlockSpec(memory_space=pl.ANY),
            scratch_shapes=[pltpu.VMEM((ROW_BLOCK * PACK_ROWS, LANES), hn.dtype), pltpu.SemaphoreType.DMA,
                            pltpu.SemaphoreType.DMA]),
        out_shape=jax.ShapeDtypeStruct((n_rows * PACK_ROWS, LANES), hn.dtype),
        compiler_params=_cparams(1),
        name="moe_dispatch",
    )(pad_start, pad_len, n_used, dest, hn)


def _expert_kernel(be_ref, run_ref, next_ref, used_ref, x_ref, wg_hbm, wu_hbm, wd_hbm, y_ref,
                   wg_f32, wu_f32, wd_f32, wg_bf, wu_bf, wd_bf, sem):
    i = pl.program_id(0)
    live = i < used_ref[0]
    new_expert = (i == 0) | (be_ref[i] != be_ref[jnp.maximum(i - 1, 0)])
    slot = run_ref[i] % 2

    def fetch(expert, to_slot):
        return [pltpu.make_async_copy(src.at[expert], dst.at[to_slot], sem.at[to_slot, k])
                for k, (src, dst) in enumerate(((wg_hbm, wg_f32), (wu_hbm, wu_f32), (wd_hbm, wd_f32)))]

    @pl.when(live & (i == 0))
    def _():
        for copy in fetch(be_ref[0], 0):
            copy.start()

    @pl.when(live & new_expert)
    def _():
        for copy in fetch(be_ref[i], slot):
            copy.wait()

        @pl.when(next_ref[i] >= 0)
        def _():
            for copy in fetch(next_ref[i], 1 - slot):
                copy.start()

        wg_bf[...] = wg_f32[slot].astype(_BF16)
        wu_bf[...] = wu_f32[slot].astype(_BF16)
        wd_bf[...] = wd_f32[slot].astype(_BF16)

    @pl.when(live)
    def _():
        sub = ROW_BLOCK // EXPERT_SPLIT
        gate_up = {}

        def first(k):
            x = _unpack_rows(x_ref, sub, k * sub)
            gate_up[k] = (_dot(x, wg_bf[...]), _dot(x, wu_bf[...]))

        def second(k):
            gate, up = gate_up.pop(k)
            act = (gate * jax.nn.sigmoid(gate)) * up
            _rows_to_tiles(y_ref, _dot(act.astype(_BF16), wd_bf[...]), k * sub)

        for k in range(min(EXPERT_AHEAD, EXPERT_SPLIT)):
            first(k)
        for k in range(EXPERT_SPLIT):
            if k + EXPERT_AHEAD < EXPERT_SPLIT:
                first(k + EXPERT_AHEAD)
            second(k)

    @pl.when(i >= used_ref[0])
    def _():
        y_ref[...] = jnp.zeros_like(y_ref)


def _experts(xs, blocks, w_gate, w_up, w_down):
    n_rows, d = xs.shape[0] // PACK_ROWS, D_MODEL
    n_blocks = n_rows // ROW_BLOCK
    tile_block = (ROW_BLOCK * SUBLANES, LANES)
    hbm = pl.BlockSpec(memory_space=pl.ANY)

    return pl.pallas_call(
        _expert_kernel,
        grid_spec=pltpu.PrefetchScalarGridSpec(
            num_scalar_prefetch=4,
            grid=(n_blocks,),
            in_specs=[pl.BlockSpec((ROW_BLOCK * PACK_ROWS, LANES),
                                   lambda i, be, run, nxt, used: (jnp.minimum(i, used[0] - 1), 0)),
                      hbm, hbm, hbm],
            out_specs=pl.BlockSpec(tile_block, lambda i, *_: (i, 0)),
            scratch_shapes=[pltpu.VMEM((2, d, D_EXPERT), _F32), pltpu.VMEM((2, d, D_EXPERT), _F32),
                            pltpu.VMEM((2, D_EXPERT, d), _F32),
                            pltpu.VMEM((d, D_EXPERT), _BF16), pltpu.VMEM((d, D_EXPERT), _BF16),
                            pltpu.VMEM((D_EXPERT, d), _BF16), pltpu.SemaphoreType.DMA((2, 3))]),
        out_shape=jax.ShapeDtypeStruct((n_rows * SUBLANES, LANES), _F32),
        compiler_params=_cparams(1),
        name="moe_experts",
    )(*blocks, xs, w_gate, w_up, w_down)


def _combine_kernel(dest_ref, next_ref, h_ref, info_ref, ys_ref, o_ref, buf_ref, sem):
    n = h_ref.shape[0]
    step = pl.program_id(0)
    slot = step % 2

    def gather(idx_ref, to_slot):
        def issue(i, carry):
            for u in range(ROW_UNROLL):
                t = i * ROW_UNROLL + u
                _row_copy(ys_ref, idx_ref[2 * t], buf_ref.at[to_slot, 0], t,
                          sem.at[to_slot]).start(priority=0)
                _row_copy(ys_ref, idx_ref[2 * t + 1], buf_ref.at[to_slot, 1], t,
                          sem.at[to_slot]).start(priority=1)
            return carry

        lax.fori_loop(0, n // ROW_UNROLL, issue, 0)

    @pl.when(step == 0)
    def _():
        gather(dest_ref, 0)

    @pl.when(step + 1 < pl.num_programs(0))
    def _():
        gather(next_ref, 1 - slot)

    for k in range(2):
        pltpu.make_async_copy(ys_ref.at[_tile_rows(0, n)], buf_ref.at[slot, k], sem.at[slot]).wait()
    info = info_ref[...]
    w0 = info[:, 0:1]
    w1 = info[:, 1:2]
    for s in range(SUBLANES):
        sl = slice(s * LANES, (s + 1) * LANES)
        moe = w0 * _tile_block(buf_ref.at[slot, 0], s, n) + w1 * _tile_block(buf_ref.at[slot, 1], s, n)
        o_ref[:, sl] = h_ref[:, sl] + moe


def _combine(h, info, ys, dest):
    t, d = h.shape
    n = min(t, COMBINE_TOKENS)
    steps = t // n
    return pl.pallas_call(
        _combine_kernel,
        grid=(steps,),
        in_specs=[pl.BlockSpec((2 * n,), lambda i: (i,), memory_space=pltpu.SMEM),
                  pl.BlockSpec((2 * n,), lambda i: (jnp.minimum(i + 1, steps - 1),),
                               memory_space=pltpu.SMEM),
                  pl.BlockSpec((n, d), lambda i: (i, 0)),
                  pl.BlockSpec((n, LANES), lambda i: (i, 0)),
                  pl.BlockSpec(memory_space=pl.ANY)],
        out_specs=pl.BlockSpec((n, d), lambda i: (i, 0)),
        out_shape=jax.ShapeDtypeStruct((t, d), _F32),
        scratch_shapes=[pltpu.VMEM((2, 2, n * SUBLANES, LANES), _F32), pltpu.SemaphoreType.DMA((2,))],
        compiler_params=_cparams(1),
        name="moe_combine",
    )(dest, dest, h, info, ys)


def _moe_layout(route_rows, counts, t):
    counts = counts[ROUTE_LANE0:ROUTE_LANE0 + N_EXPERTS, 0].astype(jnp.int32)
    padded = (counts + ROW_BLOCK - 1) // ROW_BLOCK * ROW_BLOCK
    pends = jnp.cumsum(padded)
    pstarts = pends - padded
    eid = route_rows[2:4].astype(jnp.int32)
    rank = route_rows[4:6].astype(jnp.int32)
    experts = jnp.arange(N_EXPERTS, dtype=jnp.int32)
    start_of = jnp.sum(jnp.where(eid[:, :, None] == experts, pstarts, 0), axis=-1)
    dest = (start_of + rank).T.reshape(-1)
    n_blocks = -(-2 * t // ROW_BLOCK) + N_EXPERTS
    first_row = jnp.arange(n_blocks, dtype=jnp.int32) * ROW_BLOCK
    block_e = jnp.minimum(jnp.sum((pends[None, :] <= first_row[:, None]).astype(jnp.int32), axis=1),
                          N_EXPERTS - 1)
    n_used = (pends[-1:] // ROW_BLOCK).astype(jnp.int32)
    changed = jnp.concatenate([jnp.zeros((1,), jnp.int32), (block_e[1:] != block_e[:-1]).astype(jnp.int32)])
    block_run = jnp.cumsum(changed)
    later = (counts[None, :] > 0) & (experts[None, :] > experts[:, None])
    next_expert = jnp.min(jnp.where(later, experts[None, :], N_EXPERTS), axis=1)
    next_expert = jnp.where(next_expert < N_EXPERTS, next_expert, -1)
    block_next = jnp.sum(jnp.where(block_e[:, None] == experts[None, :], next_expert[None, :], 0), axis=1)
    blocks = (block_e, block_run.astype(jnp.int32), block_next.astype(jnp.int32), n_used)
    return dest, blocks, pstarts + counts, padded - counts, n_blocks * ROW_BLOCK


def kernel(x, mem, positions, mix_norm_g, w_in, qn_a, kn_a, rel_bias, ret_gn_g, mem_norm_g, w_mem_kv,
           qn_c, kn_c, w_out, ffn_norm_g, w_router_group, b_router_group, w_router_expert,
           b_router_expert, w_gate, w_up, w_down):
    b, s, d = x.shape
    t = b * s
    x2 = x.reshape(t, d)
    cos, sin = _rope_tables(positions)
    kc, vc = _mem_kv(mem, mem_norm_g, w_mem_kv, kn_c)
    tables = _retention_tables()
    qa, ka, vta, qkzv, gate, qc = _in_proj(x, cos, sin, tables[1], mix_norm_g, w_in, qn_a, kn_a, qn_c)
    out_a = _attention(qa, ka, vta, rel_bias)
    out_b = _retention(qkzv, gate, tables, ret_gn_g)
    out_c = _cross_attention(qc, kc, vc)
    h, hn, info, route_rows, counts = _out_router(
        x2, out_a.reshape(t, A_WIDTH), out_b.reshape(t, B_WIDTH), out_c.reshape(t, C_WIDTH),
        w_out, ffn_norm_g, w_router_group, b_router_group, w_router_expert, b_router_expert)
    dest, blocks, pad_start, pad_len, n_rows = _moe_layout(route_rows, counts, t)
    xs = _dispatch(hn, dest, pad_start, pad_len, blocks[-1], n_rows)
    ys = _experts(xs, blocks, w_gate, w_up, w_down)
    return _combine(h, info, ys, dest).reshape(b, s, d)
```

```python
import functools

import jax
import jax.numpy as jnp
from jax import lax
from jax.experimental import pallas as pl
from jax.experimental.pallas import tpu as pltpu

D_MODEL = 1024
CHUNK = 64
HEAD_DIM = 64
A_HEADS = 8
B_HEADS = 4
C_HEADS = 4
A_WIDTH = A_HEADS * HEAD_DIM
B_WIDTH = B_HEADS * HEAD_DIM
C_WIDTH = C_HEADS * HEAD_DIM
IN_COLS = 3 * A_WIDTH + 4 * B_WIDTH + C_WIDTH
LEFT_CHUNKS = 8
BAND_CHUNKS = LEFT_CHUNKS + 1
MAX_REL_DIST = 128
ROPE_BASE = 10000.0
N_GROUPS = 4
EXPERTS_PER_GROUP = 8
N_EXPERTS = N_GROUPS * EXPERTS_PER_GROUP
D_EXPERT = D_MODEL // 2
EPS = 1e-6
NEG_INF = -1e30
LOG2E = 1.4426950408889634

LANES = 128
SUBLANES = 8
assert D_MODEL == SUBLANES * LANES
PACK_ROWS = SUBLANES // 2
LEFT_ROWS = LEFT_CHUNKS * CHUNK
ATT_Q = 2 * CHUNK
ATT_K = ATT_Q + LEFT_ROWS
ATT_VARIANTS = LEFT_ROWS // ATT_Q + 1
ONES_ROWS = 16
ATT_AHEAD = 3
RET_CHUNK = 256
ROW_BLOCK = 512
EXPERT_SPLIT = 2
EXPERT_AHEAD = 2
ROUTE_LANE0 = N_GROUPS
ROUTE_ROWS = 64
VMEM_LIMIT = 48 * 1024 * 1024

_F32 = jnp.float32
_BF16 = jnp.bfloat16


def _cparams(n_axes):
    return pltpu.CompilerParams(dimension_semantics=("arbitrary",) * n_axes,
                                vmem_limit_bytes=VMEM_LIMIT)


def _dot(a, b):
    return jnp.dot(a, b, preferred_element_type=_F32)


def _dot_nt(a, b):
    return lax.dot_general(a, b, (((1,), (1,)), ((), ())), preferred_element_type=_F32)


def _lane(shape):
    return lax.broadcasted_iota(jnp.int32, shape, len(shape) - 1)


def _pair_rms(t, gain):
    low = _lane(t.shape) < HEAD_DIM
    t2 = t * t
    ms0 = jnp.sum(jnp.where(low, t2, 0.0), axis=-1, keepdims=True) * (1.0 / HEAD_DIM)
    ms1 = jnp.sum(jnp.where(low, 0.0, t2), axis=-1, keepdims=True) * (1.0 / HEAD_DIM)
    r = jnp.where(low, lax.rsqrt(ms0 + EPS), lax.rsqrt(ms1 + EPS))
    return (t * r) * gain


def _rows_to_tiles(ref, val, row0=0):
    n = val.shape[0]
    for s in range(SUBLANES):
        ref[pl.ds(row0 * SUBLANES + s, n, stride=SUBLANES), :] = val[:, s * LANES:(s + 1) * LANES]


def _tile_block(ref, s, n, row0=0):
    return ref[pl.ds(row0 * SUBLANES + s, n, stride=SUBLANES), :]


def _tiles_to_rows(ref, n, row0=0):
    return jnp.concatenate([_tile_block(ref, s, n, row0) for s in range(SUBLANES)], axis=-1)


def _pack_rows(ref, val, row0=0):
    n = val.shape[0]
    for s in range(PACK_ROWS):
        lo = val[:, (2 * s) * LANES:(2 * s + 1) * LANES].astype(_BF16).astype(_F32)
        hi = val[:, (2 * s + 1) * LANES:(2 * s + 2) * LANES].astype(_BF16).astype(_F32)
        word = (lax.bitcast_convert_type(lo, jnp.uint32) >> 16) | (
            lax.bitcast_convert_type(hi, jnp.uint32) & jnp.uint32(0xFFFF0000))
        ref[pl.ds(row0 * PACK_ROWS + s, n, stride=PACK_ROWS), :] = word


def _unpack_rows(ref, n, row0=0):
    parts = []
    for s in range(PACK_ROWS):
        word = ref[pl.ds(row0 * PACK_ROWS + s, n, stride=PACK_ROWS), :]
        parts.append(lax.bitcast_convert_type(word << 16, _F32))
        parts.append(lax.bitcast_convert_type(word & jnp.uint32(0xFFFF0000), _F32))
    return jnp.concatenate(parts, axis=-1).astype(_BF16)


ROPE_HALF = HEAD_DIM // 2
ROPE_PACK = LANES // ROPE_HALF


def _rope_tables(pos_ref, inv_ref, cos_ref, sin_ref):
    ang = pos_ref[...].astype(_F32) * inv_ref[...]
    rows = ang.shape[0]
    lane = _lane(ang.shape)
    sign = jnp.where((lane % HEAD_DIM) < ROPE_HALF, -1.0, 1.0)
    for out_ref, val in ((cos_ref, jnp.cos(ang)), (sin_ref, jnp.sin(ang))):
        for j in range(ROPE_PACK):
            seg = jnp.where(lane // ROPE_HALF == j, val, 0.0)
            full = seg
            for k in range(1, ROPE_PACK):
                full = full + pltpu.roll(seg, k * ROPE_HALF, 1)
            if out_ref is sin_ref:
                full = full * sign
            out_ref[pl.ds(j, rows, stride=ROPE_PACK), :] = full


def _rope_inputs(positions):
    b, s = positions.shape
    inv = ROPE_BASE ** (-jnp.arange(ROPE_HALF, dtype=_F32) / ROPE_HALF)
    inv128 = jnp.tile(inv, ROPE_PACK).reshape(1, LANES)
    pos = jnp.repeat(positions.reshape(b, s // ROPE_PACK, ROPE_PACK), ROPE_HALF, axis=2)
    return pos, inv128


def _mem_kv_kernel(mem_ref, g_ref, w_ref, kn_ref, k_ref, v_ref):
    m = mem_ref[...]
    ms = jnp.mean(m * m, axis=-1, keepdims=True)
    mn = (m * lax.rsqrt(ms + EPS)) * g_ref[...]
    kv = _dot(mn.astype(_BF16), w_ref[...])
    for j in range(C_WIDTH // LANES):
        sl = slice(j * LANES, (j + 1) * LANES)
        k_ref[:, sl] = _pair_rms(kv[:, sl], kn_ref[...]).astype(_BF16)
    v_ref[...] = kv[:, C_WIDTH:].T.astype(_BF16)


def _mem_kv(mem, mem_norm_g, w_mem_kv, kn_c):
    b, m, d = mem.shape
    kn = jnp.tile(kn_c, 2).reshape(1, LANES)
    return pl.pallas_call(
        _mem_kv_kernel,
        grid=(b,),
        in_specs=[pl.BlockSpec((None, m, d), lambda i: (i, 0, 0)),
                  pl.BlockSpec((1, d), lambda i: (0, 0)),
                  pl.BlockSpec((d, 2 * C_WIDTH), lambda i: (0, 0)),
                  pl.BlockSpec((1, LANES), lambda i: (0, 0))],
        out_specs=[pl.BlockSpec((None, m, C_WIDTH), lambda i: (i, 0, 0)),
                   pl.BlockSpec((None, C_WIDTH, m), lambda i: (i, 0, 0))],
        out_shape=[jax.ShapeDtypeStruct((b, m, C_WIDTH), _BF16),
                   jax.ShapeDtypeStruct((b, C_WIDTH, m), _BF16)],
        compiler_params=_cparams(1),
        name="mem_kv",
    )(mem, mem_norm_g.reshape(1, d), w_mem_kv.astype(_BF16), kn)


def _in_proj_kernel(x_ref, pos_ref, inv_ref, g_ref, wq_ref, wk_ref, wvt_ref, wr_ref, wc_ref, qn_ref, kn_ref,
                    cn_ref, zeta_ref, qa_ref, ka_ref, vt_ref, ret_ref, gate_ref, qc_ref, xn_ref, acc_ref,
                    accb_ref, cos_ref, sin_ref):
    x = x_ref[...]
    ms = jnp.mean(x * x, axis=-1, keepdims=True)
    xn_ref[...] = ((x * lax.rsqrt(ms + EPS)) * g_ref[...]).astype(_BF16)
    _rope_tables(pos_ref, inv_ref, cos_ref, sin_ref)

    def normed(slot, out_ref, gain_ref):
        for blk in range(out_ref.shape[1] // LANES):
            sl = slice(blk * LANES, (blk + 1) * LANES)
            out_ref[:, sl] = _pair_rms(acc_ref[slot, :, sl], gain_ref[...]).astype(_BF16)

    acc_ref[0] = _dot(xn_ref[...], wq_ref[...])
    acc_ref[1] = _dot(xn_ref[...], wk_ref[...])
    normed(0, qa_ref, qn_ref)
    acc_ref[0] = _dot_nt(wvt_ref[...], xn_ref[...])
    normed(1, ka_ref, kn_ref)
    accb_ref[...] = _dot(xn_ref[...], wr_ref[...])
    for blk in range(vt_ref.shape[0]):
        vt_ref[blk] = acc_ref[0, :, blk * LANES:(blk + 1) * LANES].astype(_BF16)
    acc_ref[1, :, 0:C_WIDTH] = _dot(xn_ref[...], wc_ref[...])
    cos, sin = cos_ref[...], sin_ref[...]
    chunks = x_ref.shape[0] // RET_CHUNK
    for p in range(B_WIDTH // LANES):
        sl = slice(p * LANES, (p + 1) * LANES)
        q = accb_ref[:, sl]
        k = accb_ref[:, B_WIDTH + p * LANES:B_WIDTH + (p + 1) * LANES]
        kr = (k * cos + _swap_halves(k) * sin) * (HEAD_DIM ** -0.5)
        ret_ref[:, sl] = (q * cos + _swap_halves(q) * sin).astype(_BF16)
        ret_ref[:, B_WIDTH + p * LANES:B_WIDTH + (p + 1) * LANES] = kr.astype(_BF16)
        ret_ref[:, 2 * B_WIDTH + p * LANES:2 * B_WIDTH + (p + 1) * LANES] = (
            kr * jnp.concatenate([zeta_ref[p]] * chunks, axis=0)).astype(_BF16)
        ret_ref[:, 3 * B_WIDTH + p * LANES:3 * B_WIDTH + (p + 1) * LANES] = accb_ref[
            :, 2 * B_WIDTH + p * LANES:2 * B_WIDTH + (p + 1) * LANES].astype(_BF16)
        gate_ref[:, sl] = accb_ref[:, 3 * B_WIDTH + p * LANES:3 * B_WIDTH + (p + 1) * LANES]
    normed(1, qc_ref, cn_ref)


def _in_proj(x3, positions, zeta, g, w_in, qn_a, kn_a, qn_c):
    b, s, d = x3.shape
    tm = min(s, 512)
    assert tm == A_WIDTH
    assert tm % RET_CHUNK == 0
    pos, inv128 = _rope_inputs(positions)
    w = w_in.astype(_BF16)
    cuts = [0, A_WIDTH, 2 * A_WIDTH, 3 * A_WIDTH, 3 * A_WIDTH + 4 * B_WIDTH, IN_COLS]
    wq, wk, wv, wr, wc = (w[:, lo:hi] for lo, hi in zip(cuts[:-1], cuts[1:]))
    scale = HEAD_DIM ** -0.5 * LOG2E
    gains = [(jnp.tile(gn, 2) * sc).reshape(1, LANES) for gn, sc in ((qn_a, scale), (kn_a, 1.0), (qn_c, scale))]

    def whole(arr):
        return pl.BlockSpec(arr.shape, lambda i, j: (0,) * arr.ndim)

    def rows(width):
        return pl.BlockSpec((None, tm, width), lambda i, j: (i, j, 0))

    consts = [inv128, g.reshape(1, d), wq, wk, wv.T, wr, wc] + gains + [zeta]
    return pl.pallas_call(
        _in_proj_kernel,
        grid=(b, s // tm),
        in_specs=[rows(d), pl.BlockSpec((None, tm // ROPE_PACK, LANES), lambda i, j: (i, j, 0))]
        + [whole(c) for c in consts],
        out_specs=[rows(A_WIDTH), rows(A_WIDTH),
                   pl.BlockSpec((None, tm // LANES, A_WIDTH, LANES), lambda i, j: (i, j, 0, 0)),
                   rows(4 * B_WIDTH), rows(B_WIDTH), rows(C_WIDTH)],
        out_shape=[jax.ShapeDtypeStruct((b, s, A_WIDTH), _BF16), jax.ShapeDtypeStruct((b, s, A_WIDTH), _BF16),
                   jax.ShapeDtypeStruct((b, s // LANES, A_WIDTH, LANES), _BF16),
                   jax.ShapeDtypeStruct((b, s, 4 * B_WIDTH), _BF16), jax.ShapeDtypeStruct((b, s, B_WIDTH), _F32),
                   jax.ShapeDtypeStruct((b, s, C_WIDTH), _BF16)],
        scratch_shapes=[pltpu.VMEM((tm, d), _BF16), pltpu.VMEM((2, tm, A_WIDTH), _F32),
                        pltpu.VMEM((tm, 4 * B_WIDTH), _F32), pltpu.VMEM((tm, LANES), _F32),
                        pltpu.VMEM((tm, LANES), _F32)],
        compiler_params=_cparams(2),
        name="in_proj",
    )(x3, pos, *consts)


def _attn_kernel(q_ref, k_ref, vt_ref, bias_ref, o_ref, kp_ref, st_ref, var_ref, *, q_rows):
    qs = pl.program_id(2)
    s = k_ref.shape[0]
    fill_rows = min(s, 1024)
    left_blocks = LEFT_ROWS // LANES

    @pl.when(qs == 0)
    def _():
        kp_ref[0:LEFT_ROWS, :] = jnp.zeros((LEFT_ROWS, LANES), _BF16)

        def fill(i, carry):
            r = pl.multiple_of(i * fill_rows, fill_rows)
            kp_ref[pl.ds(LEFT_ROWS + r, fill_rows), :] = k_ref[pl.ds(r, fill_rows), :]
            return carry

        lax.fori_loop(0, s // fill_rows, fill, 0)
        key = lax.broadcasted_iota(jnp.int32, (ATT_K, 2 * ATT_Q), 0)
        for v in range(ATT_VARIANTS):
            var_ref[v] = jnp.where(key >= LEFT_ROWS - ATT_Q * v, bias_ref[...], NEG_INF)

    low = _lane((ATT_Q, LANES)) < HEAD_DIM
    ones = jnp.ones((ONES_ROWS, ATT_K), _BF16)
    tiles_per_step = q_rows // ATT_Q

    def scores(j):
        cp = qs * tiles_per_step + j
        q = q_ref[j * ATT_Q:(j + 1) * ATT_Q, :]
        q2 = jnp.concatenate([jnp.where(low, q, jnp.zeros_like(q)), jnp.where(low, jnp.zeros_like(q), q)], axis=0)
        kb = kp_ref[pl.ds(pl.multiple_of(cp * ATT_Q, ATT_Q), ATT_K), :]
        st_ref[j % (ATT_AHEAD + 1)] = _dot_nt(kb, q2) + var_ref[jnp.minimum(cp, ATT_VARIANTS - 1)]

    def finish(j):
        cp = qs * tiles_per_step + j
        st = st_ref[j % (ATT_AHEAD + 1)]
        m = jnp.max(st, axis=0, keepdims=True)
        p = jnp.exp2(st - m)
        vt = jnp.concatenate([vt_ref[jnp.maximum(cp + kb_i - left_blocks, 0)] for kb_i in range(ATT_K // LANES)],
                             axis=1)
        ot = _dot(jnp.concatenate([vt, ones], axis=0), p.astype(_BF16))
        inv = 1.0 / ot[LANES:LANES + 1, :]
        out_t = jnp.concatenate([ot[0:HEAD_DIM, 0:ATT_Q] * inv[:, 0:ATT_Q],
                                 ot[HEAD_DIM:LANES, ATT_Q:] * inv[:, ATT_Q:]], axis=0)
        o_ref[j * ATT_Q:(j + 1) * ATT_Q, :] = out_t.T.astype(o_ref.dtype)

    for j in range(min(ATT_AHEAD, tiles_per_step)):
        scores(j)
    for j in range(tiles_per_step):
        if j + ATT_AHEAD < tiles_per_step:
            scores(j + ATT_AHEAD)
        finish(j)


def _toeplitz_bias(rel_bias, q_len, k_len):
    h, table = rel_bias.shape
    n_diag = q_len + k_len - 1
    flat_lo = k_len - 1 - LEFT_ROWS - (CHUNK - 1)
    flat_hi = n_diag - flat_lo - table
    rev = jnp.concatenate([jnp.broadcast_to(rel_bias[:, -1:], (h, flat_hi)), rel_bias[:, ::-1],
                           jnp.broadcast_to(rel_bias[:, :1], (h, flat_lo))], axis=1).astype(_F32)
    flat = jnp.tile(rev, (1, q_len + 1))
    pitch = n_diag - 1
    skew = flat[:, q_len - 1:q_len - 1 + q_len * pitch].reshape(h, q_len, pitch)
    return skew[:, :, :k_len]


def _attn_bias(rel_bias):
    h = rel_bias.shape[0]
    bias = _toeplitz_bias(rel_bias, ATT_Q, ATT_K)
    q = lax.broadcasted_iota(jnp.int32, (ATT_Q, ATT_K), 0)
    k = lax.broadcasted_iota(jnp.int32, (ATT_Q, ATT_K), 1)
    off = k // CHUNK - q // CHUNK
    in_band = (off >= 0) & (off < BAND_CHUNKS)
    full = jnp.where(in_band[None], bias * LOG2E, NEG_INF)
    full = full.reshape(h // 2, 2, ATT_Q, ATT_K)
    return full.transpose(0, 3, 1, 2).reshape(h // 2, ATT_K, 2 * ATT_Q)


def _attention(qa, ka, vta, rel_bias):
    b, s, _ = qa.shape
    q_rows = min(s, 2048)
    pairs = A_HEADS // 2
    return pl.pallas_call(
        functools.partial(_attn_kernel, q_rows=q_rows),
        grid=(b, pairs, s // q_rows),
        in_specs=[pl.BlockSpec((None, q_rows, LANES), lambda i, p, j: (i, j, p)),
                  pl.BlockSpec((None, s, LANES), lambda i, p, j: (i, 0, p)),
                  pl.BlockSpec((None, s // LANES, LANES, LANES), lambda i, p, j: (i, 0, p, 0)),
                  pl.BlockSpec((None, ATT_K, 2 * ATT_Q), lambda i, p, j: (p, 0, 0))],
        out_specs=pl.BlockSpec((None, q_rows, LANES), lambda i, p, j: (i, j, p)),
        out_shape=jax.ShapeDtypeStruct((b, s, A_WIDTH), _BF16),
        scratch_shapes=[pltpu.VMEM((s + LEFT_ROWS, LANES), _BF16),
                        pltpu.VMEM((ATT_AHEAD + 1, ATT_K, 2 * ATT_Q), _F32),
                        pltpu.VMEM((ATT_VARIANTS, ATT_K, 2 * ATT_Q), _F32)],
        compiler_params=_cparams(3),
        name="attn_a",
    )(qa, ka, vta, _attn_bias(rel_bias))


def _swap_halves(t):
    first = (_lane(t.shape) % HEAD_DIM) < (HEAD_DIM // 2)
    return jnp.where(first, pltpu.roll(t, LANES - HEAD_DIM // 2, 1), pltpu.roll(t, HEAD_DIM // 2, 1))


def _retention_kernel(q_ref, k_ref, kz_ref, v_ref, gate_ref, decay_ref, xi_ref, cd_ref, gn_ref, o_ref,
                      state_ref, *, rows):
    @pl.when(pl.program_id(2) == 0)
    def _():
        state_ref[...] = jnp.zeros_like(state_ref)

    c = RET_CHUNK
    low = _lane((c, LANES)) < HEAD_DIM
    eye = jnp.where(lax.broadcasted_iota(jnp.int32, (LANES, LANES), 0) == _lane((LANES, LANES)),
                    1.0, 0.0).astype(_BF16)
    srow = lax.broadcasted_iota(jnp.int32, (LANES, LANES), 0) < HEAD_DIM
    scol = _lane((LANES, LANES)) < HEAD_DIM
    same_head = srow == scol

    for j in range(rows // c):
        sl = slice(j * c, (j + 1) * c)
        qb = q_ref[sl, :]
        kb = k_ref[sl, :]
        vb = v_ref[sl, :]
        inner_out = []
        for h in range(2):
            qh = jnp.where(low if h == 0 else ~low, qb, jnp.zeros_like(qb))
            inner = _dot_nt(qh, kb) * decay_ref[h]
            inner_out.append(_dot(inner.astype(_BF16), vb))
        state = state_ref[...]
        cross = _dot(qb, state.astype(_BF16)) * xi_ref[...]
        o = jnp.where(low, inner_out[0], inner_out[1]) + cross
        kz = _dot_nt(eye, kz_ref[sl, :]).astype(_BF16)
        state_ref[...] = cd_ref[...] * state + jnp.where(same_head, _dot(kz, vb), 0.0)
        mu = jnp.where(low,
                       jnp.sum(jnp.where(low, o, 0.0), axis=-1, keepdims=True),
                       jnp.sum(jnp.where(low, 0.0, o), axis=-1, keepdims=True)) * (1.0 / HEAD_DIM)
        dlt = o - mu
        d2 = dlt * dlt
        var = jnp.where(low,
                        jnp.sum(jnp.where(low, d2, 0.0), axis=-1, keepdims=True),
                        jnp.sum(jnp.where(low, 0.0, d2), axis=-1, keepdims=True)) * (1.0 / HEAD_DIM)
        y = (dlt * lax.rsqrt(var + EPS)) * gn_ref[...]
        g = gate_ref[sl, :]
        o_ref[sl, :] = ((g * jax.nn.sigmoid(g)) * y).astype(o_ref.dtype)


def _retention_tables():
    c = RET_CHUNK
    log_g = jnp.log(1.0 - jnp.exp2(-5.0 - jnp.arange(B_HEADS, dtype=_F32)))
    idx = jnp.arange(c, dtype=_F32)
    diff = idx[:, None] - idx[None, :]
    decay = jnp.where(diff >= 0, jnp.exp(log_g[:, None, None] * jnp.maximum(diff, 0.0)), 0.0)
    zeta = jnp.exp(log_g[:, None] * (c - 1 - idx))
    xi = jnp.exp(log_g[:, None] * (idx + 1.0))
    cd = jnp.exp(log_g * c)

    def lanes(tab):
        return jnp.repeat(tab.reshape(B_HEADS // 2, 2, c), HEAD_DIM, axis=1).transpose(0, 2, 1)

    cdm = jnp.repeat(cd.reshape(B_HEADS // 2, 2), HEAD_DIM, axis=1)
    cdm = jnp.broadcast_to(cdm[:, :, None], (B_HEADS // 2, LANES, LANES))
    return decay, lanes(zeta), lanes(xi), cdm


def _retention(qkzv, gate, tables, ret_gn_g):
    b, s, _ = qkzv.shape
    rows = min(s, 2048)
    pairs = B_HEADS // 2
    decay, _, xi, cdm = tables
    gn = ret_gn_g.reshape(pairs, 1, LANES)

    def col(off):
        return pl.BlockSpec((None, rows, LANES), lambda i, p, j: (i, j, off * pairs + p))

    return pl.pallas_call(
        functools.partial(_retention_kernel, rows=rows),
        grid=(b, pairs, s // rows),
        in_specs=[col(0), col(1), col(2), col(3), col(0),
                  pl.BlockSpec((2, RET_CHUNK, RET_CHUNK), lambda i, p, j: (p, 0, 0)),
                  pl.BlockSpec((None, RET_CHUNK, LANES), lambda i, p, j: (p, 0, 0)),
                  pl.BlockSpec((None, LANES, LANES), lambda i, p, j: (p, 0, 0)),
                  pl.BlockSpec((None, 1, LANES), lambda i, p, j: (p, 0, 0))],
        out_specs=pl.BlockSpec((None, rows, LANES), lambda i, p, j: (i, j, p)),
        out_shape=jax.ShapeDtypeStruct((b, s, B_WIDTH), _BF16),
        scratch_shapes=[pltpu.VMEM((LANES, LANES), _F32)],
        compiler_params=_cparams(3),
        name="retention_b",
    )(qkzv, qkzv, qkzv, qkzv, gate, decay, xi, cdm, gn)


def _cross_kernel(q_ref, k_ref, vt_ref, o_ref, st_ref, *, rows):
    low = _lane((ATT_Q, LANES)) < HEAD_DIM
    lane_blocks = C_WIDTH // LANES
    tiles = [(j, lb) for j in range(rows // ATT_Q) for lb in range(lane_blocks)]
    ones = jnp.ones((ONES_ROWS, vt_ref.shape[1]), _BF16)

    def scores(i):
        j, lb = tiles[i]
        sl = slice(lb * LANES, (lb + 1) * LANES)
        q = q_ref[j * ATT_Q:(j + 1) * ATT_Q, sl]
        q2 = jnp.concatenate([jnp.where(low, q, jnp.zeros_like(q)), jnp.where(low, jnp.zeros_like(q), q)], axis=0)
        st_ref[i % (ATT_AHEAD + 1)] = _dot_nt(k_ref[:, sl], q2)

    def finish(i):
        j, lb = tiles[i]
        sl = slice(lb * LANES, (lb + 1) * LANES)
        st = st_ref[i % (ATT_AHEAD + 1)]
        p = jnp.exp2(st - jnp.max(st, axis=0, keepdims=True))
        ot = _dot(jnp.concatenate([vt_ref[sl, :], ones], axis=0), p.astype(_BF16))
        inv = 1.0 / ot[LANES:LANES + 1, :]
        out_t = jnp.concatenate([ot[0:HEAD_DIM, 0:ATT_Q] * inv[:, 0:ATT_Q],
                                 ot[HEAD_DIM:LANES, ATT_Q:] * inv[:, ATT_Q:]], axis=0)
        o_ref[j * ATT_Q:(j + 1) * ATT_Q, sl] = out_t.T.astype(o_ref.dtype)

    for i in range(min(ATT_AHEAD, len(tiles))):
        scores(i)
    for i in range(len(tiles)):
        if i + ATT_AHEAD < len(tiles):
            scores(i + ATT_AHEAD)
        finish(i)


def _cross_attention(qc, kc, vtc):
    b, s, _ = qc.shape
    m = kc.shape[1]
    rows = min(s, 1024)
    return pl.pallas_call(
        functools.partial(_cross_kernel, rows=rows),
        grid=(b, s // rows),
        in_specs=[pl.BlockSpec((None, rows, C_WIDTH), lambda i, j: (i, j, 0)),
                  pl.BlockSpec((None, m, C_WIDTH), lambda i, j: (i, 0, 0)),
                  pl.BlockSpec((None, C_WIDTH, m), lambda i, j: (i, 0, 0))],
        out_specs=pl.BlockSpec((None, rows, C_WIDTH), lambda i, j: (i, j, 0)),
        out_shape=jax.ShapeDtypeStruct((b, s, C_WIDTH), _BF16),
        scratch_shapes=[pltpu.VMEM((ATT_AHEAD + 1, m, 2 * ATT_Q), _F32)],
        compiler_params=_cparams(2),
        name="cross_c",
    )(qc, kc, vtc)


def _out_router_kernel(x_ref, a_ref, b_ref, c_ref, wo_ref, g_ref, wr_ref, br_ref,
                       h_ref, hn_ref, info_ref, rows_ref, cnt_ref, carry_ref):
    @pl.when(pl.program_id(0) == 0)
    def _():
        carry_ref[...] = jnp.zeros_like(carry_ref)

    tm = x_ref.shape[0]
    h = x_ref[...]
    h = h + _dot(a_ref[...], wo_ref[0:A_WIDTH, :])
    h = h + _dot(b_ref[...], wo_ref[A_WIDTH:A_WIDTH + B_WIDTH, :])
    h = h + _dot(c_ref[...], wo_ref[A_WIDTH + B_WIDTH:, :])
    h_ref[...] = h
    ms = jnp.mean(h * h, axis=-1, keepdims=True)
    hn = (h * lax.rsqrt(ms + EPS)) * g_ref[...]
    _pack_rows(hn_ref, hn)
    logits = _dot_nt(wr_ref[...], hn.astype(_BF16))[0:ROUTE_ROWS, :] + br_ref[:, 0:1]
    row = lax.broadcasted_iota(jnp.int32, (ROUTE_ROWS, tm), 0).astype(_F32)
    big = float(ROUTE_ROWS)

    def first_row(mask):
        return jnp.min(jnp.where(mask, row, big), axis=0, keepdims=True)

    gmask = row < N_GROUPS
    gl = jnp.where(gmask, logits, NEG_INF)
    ge = jnp.exp(gl - jnp.max(gl, axis=0, keepdims=True))
    gp = ge / jnp.sum(ge, axis=0, keepdims=True)
    p_group = jnp.max(gp, axis=0, keepdims=True)
    g_sel = first_row(gmask & (gp == p_group))
    lo = ROUTE_LANE0 + g_sel * EXPERTS_PER_GROUP
    emask = (row >= lo) & (row < lo + EXPERTS_PER_GROUP)
    el = jnp.where(emask, logits, NEG_INF)
    ee = jnp.exp(el - jnp.max(el, axis=0, keepdims=True))
    ep = ee / jnp.sum(ee, axis=0, keepdims=True)
    p1 = jnp.max(ep, axis=0, keepdims=True)
    i1 = first_row(emask & (ep == p1))
    ep2 = jnp.where(emask & (row != i1), ep, -1.0)
    p2 = jnp.max(ep2, axis=0, keepdims=True)
    i2 = first_row(ep2 == p2)
    den = p1 + p2
    w1 = p_group * (p1 / den)
    w2 = p_group * (p2 / den)
    hit1 = row == i1
    hit2 = row == i2
    onehot = jnp.where(hit1 | hit2, 1.0, 0.0)
    r_i = lax.broadcasted_iota(jnp.int32, (tm, tm), 0)
    c_i = lax.broadcasted_iota(jnp.int32, (tm, tm), 1)
    earlier = jnp.where(r_i < c_i, 1.0, 0.0).astype(_BF16)
    before = _dot(onehot.astype(_BF16), earlier) + carry_ref[:, 0:1]
    r1 = jnp.sum(jnp.where(hit1, before, 0.0), axis=0, keepdims=True)
    r2 = jnp.sum(jnp.where(hit2, before, 0.0), axis=0, keepdims=True)
    carry_ref[...] = carry_ref[...] + jnp.sum(onehot, axis=1, keepdims=True)
    cnt_ref[...] = carry_ref[...]
    out_row = lax.broadcasted_iota(jnp.int32, (LANES, tm), 0)
    info = jnp.where(out_row == 0, w1, 0.0)
    info = jnp.where(out_row == 1, w2, info)
    info = jnp.where(out_row == 2, i1 - ROUTE_LANE0, info)
    info = jnp.where(out_row == 3, i2 - ROUTE_LANE0, info)
    info = jnp.where(out_row == 4, r1, info)
    info = jnp.where(out_row == 5, r2, info)
    rows_ref[...] = info[0:SUBLANES, :]
    info_ref[...] = info.T


def _out_router(x2, oa, ob, oc, w_out, ffn_g, w_rg, b_rg, w_re, b_re):
    t, d = x2.shape
    tm = min(t, 512)
    pad = LANES - N_GROUPS - N_EXPERTS
    wr = jnp.concatenate([w_rg, w_re, jnp.zeros((d, pad), _F32)], axis=1).T.astype(_BF16)
    br = jnp.concatenate([b_rg, b_re, jnp.zeros((ROUTE_ROWS - N_GROUPS - N_EXPERTS,), _F32)])
    br = jnp.broadcast_to(br[:, None], (ROUTE_ROWS, LANES))

    def rows(w):
        return pl.BlockSpec((tm, w), lambda i: (i, 0))

    def whole(r, c):
        return pl.BlockSpec((r, c), lambda i: (0, 0))

    return pl.pallas_call(
        _out_router_kernel,
        grid=(t // tm,),
        in_specs=[rows(d), rows(A_WIDTH), rows(B_WIDTH), rows(C_WIDTH), whole(d, d), whole(1, d),
                  whole(LANES, d), whole(ROUTE_ROWS, LANES)],
        out_specs=[rows(d), pl.BlockSpec((tm * PACK_ROWS, LANES), lambda i: (i, 0)), rows(LANES),
                   pl.BlockSpec((SUBLANES, tm), lambda i: (0, i)), whole(ROUTE_ROWS, LANES)],
        out_shape=[jax.ShapeDtypeStruct((t, d), _F32), jax.ShapeDtypeStruct((t * PACK_ROWS, LANES), jnp.uint32),
                   jax.ShapeDtypeStruct((t, LANES), _F32), jax.ShapeDtypeStruct((SUBLANES, t), _F32),
                   jax.ShapeDtypeStruct((ROUTE_ROWS, LANES), _F32)],
        scratch_shapes=[pltpu.VMEM((ROUTE_ROWS, LANES), _F32)],
        compiler_params=_cparams(1),
        name="out_router",
    )(x2, oa, ob, oc, w_out.astype(_BF16), ffn_g.reshape(1, d), wr, br)


DISPATCH_TOKENS = 2048
COMBINE_TOKENS = 512


ROW_UNROLL = 8


def _tile_rows(row, count=1, per=SUBLANES):
    start = row * per
    if not isinstance(start, int):
        start = pl.multiple_of(start, per)
    return pl.ds(start, count * per)


def _row_copy(src, s_row, dst, d_row, sem, per=SUBLANES):
    return pltpu.make_async_copy(src.at[_tile_rows(s_row, 1, per)], dst.at[_tile_rows(d_row, 1, per)], sem)


def _dispatch_kernel(pad_start_ref, pad_len_ref, used_ref, dest_ref, hn_ref, xs_ref, zero_ref, sem,
                     pad_sem):
    per = PACK_ROWS
    n = hn_ref.shape[0] // per

    @pl.when(pl.program_id(0) == 0)
    def _():
        zero_ref[...] = jnp.zeros_like(zero_ref)
        n_blocks = xs_ref.shape[0] // (ROW_BLOCK * per)

        def block_copy(blk):
            return pltpu.make_async_copy(zero_ref, xs_ref.at[_tile_rows(blk * ROW_BLOCK, ROW_BLOCK, per)],
                                         pad_sem)

        def put_block(blk, carry):
            block_copy(blk).start()
            return carry

        def done_block(blk, carry):
            block_copy(blk).wait()
            return carry

        lax.fori_loop(used_ref[0], n_blocks, put_block, 0)
        lax.fori_loop(used_ref[0], n_blocks, done_block, 0)
        bits = [1 << k for k in reversed(range(ROW_BLOCK.bit_length() - 1))]

        def tail(e, wait):
            row = pad_start_ref[e]
            for bit in bits:
                on = (pad_len_ref[e] & bit) != 0
                copy = pltpu.make_async_copy(zero_ref.at[_tile_rows(0, bit, per)],
                                             xs_ref.at[_tile_rows(row, bit, per)], pad_sem)

                @pl.when(on)
                def _():
                    copy.wait() if wait else copy.start()

                row = row + jnp.where(on, bit, 0)

        def put_tail(e, carry):
            tail(e, False)
            return carry

        def done_tail(e, carry):
            tail(e, True)
            return carry

        lax.fori_loop(0, N_EXPERTS, put_tail, 0)
        lax.fori_loop(0, N_EXPERTS, done_tail, 0)

    def issue(i, carry):
        for u in range(ROW_UNROLL):
            t = i * ROW_UNROLL + u
            _row_copy(hn_ref, t, xs_ref, dest_ref[2 * t], sem, per).start(priority=0)
            _row_copy(hn_ref, t, xs_ref, dest_ref[2 * t + 1], sem, per).start(priority=1)
        return carry

    lax.fori_loop(0, n // ROW_UNROLL, issue, 0)
    for _ in range(2):
        pltpu.make_async_copy(hn_ref, xs_ref.at[_tile_rows(0, n, per)], sem).wait()


def _dispatch(hn, dest, pad_start, pad_len, n_used, n_rows):
    t = hn.shape[0] // PACK_ROWS
    n = min(t, DISPATCH_TOKENS)
    return pl.pallas_call(
        _dispatch_kernel,
        grid_spec=pltpu.PrefetchScalarGridSpec(
            num_scalar_prefetch=3,
            grid=(t // n,),
            in_specs=[pl.BlockSpec((2 * n,), lambda i, *_: (i,), memory_space=pltpu.SMEM),
                      pl.BlockSpec((n * PACK_ROWS, LANES), lambda i, *_: (i, 0))],
            out_specs=pl.BlockSpec(memory_space=pl.ANY),
            scratch_shapes=[pltpu.VMEM((ROW_BLOCK * PACK_ROWS, LANES), hn.dtype), pltpu.SemaphoreType.DMA,
                            pltpu.SemaphoreType.DMA]),
        out_shape=jax.ShapeDtypeStruct((n_rows * PACK_ROWS, LANES), hn.dtype),
        compiler_params=_cparams(1),
        name="moe_dispatch",
    )(pad_start, pad_len, n_used, dest, hn)


def _expert_kernel(be_ref, run_ref, next_ref, used_ref, x_ref, wg_hbm, wu_hbm, wd_hbm, y_ref,
                   wg_f32, wu_f32, wd_f32, wg_bf, wu_bf, wd_bf, sem):
    i = pl.program_id(0)
    live = i < used_ref[0]
    new_expert = (i == 0) | (be_ref[i] != be_ref[jnp.maximum(i - 1, 0)])
    slot = run_ref[i] % 2

    def fetch(expert, to_slot):
        return [pltpu.make_async_copy(src.at[expert], dst.at[to_slot], sem.at[to_slot, k])
                for k, (src, dst) in enumerate(((wg_hbm, wg_f32), (wu_hbm, wu_f32), (wd_hbm, wd_f32)))]

    @pl.when(live & (i == 0))
    def _():
        for copy in fetch(be_ref[0], 0):
            copy.start()

    @pl.when(live & new_expert)
    def _():
        for copy in fetch(be_ref[i], slot):
            copy.wait()

        @pl.when(next_ref[i] >= 0)
        def _():
            for copy in fetch(next_ref[i], 1 - slot):
                copy.start()

        wg_bf[...] = wg_f32[slot].astype(_BF16)
        wu_bf[...] = wu_f32[slot].astype(_BF16)
        wd_bf[...] = wd_f32[slot].astype(_BF16)

    @pl.when(live)
    def _():
        sub = ROW_BLOCK // EXPERT_SPLIT
        gate_up = {}

        def first(k):
            x = _unpack_rows(x_ref, sub, k * sub)
            gate_up[k] = (_dot(x, wg_bf[...]), _dot(x, wu_bf[...]))

        def second(k):
            gate, up = gate_up.pop(k)
            act = (gate * jax.nn.sigmoid(gate)) * up
            _rows_to_tiles(y_ref, _dot(act.astype(_BF16), wd_bf[...]), k * sub)

        for k in range(min(EXPERT_AHEAD, EXPERT_SPLIT)):
            first(k)
        for k in range(EXPERT_SPLIT):
            if k + EXPERT_AHEAD < EXPERT_SPLIT:
                first(k + EXPERT_AHEAD)
            second(k)

    @pl.when(i >= used_ref[0])
    def _():
        y_ref[...] = jnp.zeros_like(y_ref)


def _experts(xs, blocks, w_gate, w_up, w_down):
    n_rows, d = xs.shape[0] // PACK_ROWS, D_MODEL
    n_blocks = n_rows // ROW_BLOCK
    tile_block = (ROW_BLOCK * SUBLANES, LANES)
    hbm = pl.BlockSpec(memory_space=pl.ANY)

    return pl.pallas_call(
        _expert_kernel,
        grid_spec=pltpu.PrefetchScalarGridSpec(
            num_scalar_prefetch=4,
            grid=(n_blocks,),
            in_specs=[pl.BlockSpec((ROW_BLOCK * PACK_ROWS, LANES),
                                   lambda i, be, run, nxt, used: (jnp.minimum(i, used[0] - 1), 0)),
                      hbm, hbm, hbm],
            out_specs=pl.BlockSpec(tile_block, lambda i, *_: (i, 0)),
            scratch_shapes=[pltpu.VMEM((2, d, D_EXPERT), _F32), pltpu.VMEM((2, d, D_EXPERT), _F32),
                            pltpu.VMEM((2, D_EXPERT, d), _F32),
                            pltpu.VMEM((d, D_EXPERT), _BF16), pltpu.VMEM((d, D_EXPERT), _BF16),
                            pltpu.VMEM((D_EXPERT, d), _BF16), pltpu.SemaphoreType.DMA((2, 3))]),
        out_shape=jax.ShapeDtypeStruct((n_rows * SUBLANES, LANES), _F32),
        compiler_params=_cparams(1),
        name="moe_experts",
    )(*blocks, xs, w_gate, w_up, w_down)


def _combine_kernel(dest_ref, next_ref, h_ref, info_ref, ys_ref, o_ref, buf_ref, sem):
    n = h_ref.shape[0]
    step = pl.program_id(0)
    slot = step % 2

    def gather(idx_ref, to_slot):
        def issue(i, carry):
            for u in range(ROW_UNROLL):
                t = i * ROW_UNROLL + u
                _row_copy(ys_ref, idx_ref[2 * t], buf_ref.at[to_slot, 0], t,
                          sem.at[to_slot]).start(priority=0)
                _row_copy(ys_ref, idx_ref[2 * t + 1], buf_ref.at[to_slot, 1], t,
                          sem.at[to_slot]).start(priority=1)
            return carry

        lax.fori_loop(0, n // ROW_UNROLL, issue, 0)

    @pl.when(step == 0)
    def _():
        gather(dest_ref, 0)

    @pl.when(step + 1 < pl.num_programs(0))
    def _():
        gather(next_ref, 1 - slot)

    for k in range(2):
        pltpu.make_async_copy(ys_ref.at[_tile_rows(0, n)], buf_ref.at[slot, k], sem.at[slot]).wait()
    info = info_ref[...]
    w0 = info[:, 0:1]
    w1 = info[:, 1:2]
    for s in range(SUBLANES):
        sl = slice(s * LANES, (s + 1) * LANES)
        moe = w0 * _tile_block(buf_ref.at[slot, 0], s, n) + w1 * _tile_block(buf_ref.at[slot, 1], s, n)
        o_ref[:, sl] = h_ref[:, sl] + moe


def _combine(h, info, ys, dest):
    t, d = h.shape
    n = min(t, COMBINE_TOKENS)
    steps = t // n
    return pl.pallas_call(
        _combine_kernel,
        grid=(steps,),
        in_specs=[pl.BlockSpec((2 * n,), lambda i: (i,), memory_space=pltpu.SMEM),
                  pl.BlockSpec((2 * n,), lambda i: (jnp.minimum(i + 1, steps - 1),),
                               memory_space=pltpu.SMEM),
                  pl.BlockSpec((n, d), lambda i: (i, 0)),
                  pl.BlockSpec((n, LANES), lambda i: (i, 0)),
                  pl.BlockSpec(memory_space=pl.ANY)],
        out_specs=pl.BlockSpec((n, d), lambda i: (i, 0)),
        out_shape=jax.ShapeDtypeStruct((t, d), _F32),
        scratch_shapes=[pltpu.VMEM((2, 2, n * SUBLANES, LANES), _F32), pltpu.SemaphoreType.DMA((2,))],
        compiler_params=_cparams(1),
        name="moe_combine",
    )(dest, dest, h, info, ys)


def _moe_layout(route_rows, counts, t):
    counts = counts[ROUTE_LANE0:ROUTE_LANE0 + N_EXPERTS, 0].astype(jnp.int32)
    padded = (counts + ROW_BLOCK - 1) // ROW_BLOCK * ROW_BLOCK
    pends = jnp.cumsum(padded)
    pstarts = pends - padded
    eid = route_rows[2:4].astype(jnp.int32)
    rank = route_rows[4:6].astype(jnp.int32)
    experts = jnp.arange(N_EXPERTS, dtype=jnp.int32)
    start_of = jnp.sum(jnp.where(eid[:, :, None] == experts, pstarts, 0), axis=-1)
    dest = (start_of + rank).T.reshape(-1)
    n_blocks = -(-2 * t // ROW_BLOCK) + N_EXPERTS
    first_row = jnp.arange(n_blocks, dtype=jnp.int32) * ROW_BLOCK
    block_e = jnp.minimum(jnp.sum((pends[None, :] <= first_row[:, None]).astype(jnp.int32), axis=1),
                          N_EXPERTS - 1)
    n_used = (pends[-1:] // ROW_BLOCK).astype(jnp.int32)
    changed = jnp.concatenate([jnp.zeros((1,), jnp.int32), (block_e[1:] != block_e[:-1]).astype(jnp.int32)])
    block_run = jnp.cumsum(changed)
    later = (counts[None, :] > 0) & (experts[None, :] > experts[:, None])
    next_expert = jnp.min(jnp.where(later, experts[None, :], N_EXPERTS), axis=1)
    next_expert = jnp.where(next_expert < N_EXPERTS, next_expert, -1)
    block_next = jnp.sum(jnp.where(block_e[:, None] == experts[None, :], next_expert[None, :], 0), axis=1)
    blocks = (block_e, block_run.astype(jnp.int32), block_next.astype(jnp.int32), n_used)
    return dest, blocks, pstarts + counts, padded - counts, n_blocks * ROW_BLOCK


def kernel(x, mem, positions, mix_norm_g, w_in, qn_a, kn_a, rel_bias, ret_gn_g, mem_norm_g, w_mem_kv,
           qn_c, kn_c, w_out, ffn_norm_g, w_router_group, b_router_group, w_router_expert,
           b_router_expert, w_gate, w_up, w_down):
    b, s, d = x.shape
    t = b * s
    x2 = x.reshape(t, d)
    kc, vc = _mem_kv(mem, mem_norm_g, w_mem_kv, kn_c)
    tables = _retention_tables()
    qa, ka, vta, qkzv, gate, qc = _in_proj(x, positions, tables[1], mix_norm_g, w_in, qn_a, kn_a, qn_c)
    out_a = _attention(qa, ka, vta, rel_bias)
    out_b = _retention(qkzv, gate, tables, ret_gn_g)
    out_c = _cross_attention(qc, kc, vc)
    h, hn, info, route_rows, counts = _out_router(
        x2, out_a.reshape(t, A_WIDTH), out_b.reshape(t, B_WIDTH), out_c.reshape(t, C_WIDTH),
        w_out, ffn_norm_g, w_router_group, b_router_group, w_router_expert, b_router_expert)
    dest, blocks, pad_start, pad_len, n_rows = _moe_layout(route_rows, counts, t)
    xs = _dispatch(hn, dest, pad_start, pad_len, blocks[-1], n_rows)
    ys = _experts(xs, blocks, w_gate, w_up, w_down)
    return _combine(h, info, ys, dest).reshape(b, s, d)
```

```python
import functools

import jax
import jax.numpy as jnp
from jax import lax
from jax.experimental import pallas as pl
from jax.experimental.pallas import tpu as pltpu

D_MODEL = 1024
CHUNK = 64
HEAD_DIM = 64
A_HEADS = 8
B_HEADS = 4
C_HEADS = 4
A_WIDTH = A_HEADS * HEAD_DIM
B_WIDTH = B_HEADS * HEAD_DIM
C_WIDTH = C_HEADS * HEAD_DIM
IN_COLS = 3 * A_WIDTH + 4 * B_WIDTH + C_WIDTH
LEFT_CHUNKS = 8
BAND_CHUNKS = LEFT_CHUNKS + 1
MAX_REL_DIST = 128
ROPE_BASE = 10000.0
N_GROUPS = 4
EXPERTS_PER_GROUP = 8
N_EXPERTS = N_GROUPS * EXPERTS_PER_GROUP
D_EXPERT = D_MODEL // 2
EPS = 1e-6
NEG_INF = -1e30
LOG2E = 1.4426950408889634

LANES = 128
SUBLANES = 8
assert D_MODEL == SUBLANES * LANES
PACK_ROWS = SUBLANES // 2
LEFT_ROWS = LEFT_CHUNKS * CHUNK
ATT_Q = 2 * CHUNK
ATT_K = ATT_Q + LEFT_ROWS
ATT_VARIANTS = LEFT_ROWS // ATT_Q + 1
ONES_ROWS = 16
ATT_AHEAD = 3
RET_CHUNK = 256
ROW_BLOCK = 512
EXPERT_SPLIT = 2
EXPERT_AHEAD = 2
ROUTE_LANE0 = N_GROUPS
ROUTE_ROWS = 64
VMEM_LIMIT = 48 * 1024 * 1024

_F32 = jnp.float32
_BF16 = jnp.bfloat16


def _cparams(n_axes):
    return pltpu.CompilerParams(dimension_semantics=("arbitrary",) * n_axes,
                                vmem_limit_bytes=VMEM_LIMIT)


def _dot(a, b):
    return jnp.dot(a, b, preferred_element_type=_F32)


def _dot_nt(a, b):
    return lax.dot_general(a, b, (((1,), (1,)), ((), ())), preferred_element_type=_F32)


def _lane(shape):
    return lax.broadcasted_iota(jnp.int32, shape, len(shape) - 1)


def _pair_rms(t, gain):
    low = _lane(t.shape) < HEAD_DIM
    t2 = t * t
    ms0 = jnp.sum(jnp.where(low, t2, 0.0), axis=-1, keepdims=True) * (1.0 / HEAD_DIM)
    ms1 = jnp.sum(jnp.where(low, 0.0, t2), axis=-1, keepdims=True) * (1.0 / HEAD_DIM)
    r = jnp.where(low, lax.rsqrt(ms0 + EPS), lax.rsqrt(ms1 + EPS))
    return (t * r) * gain


def _rows_to_tiles(ref, val, row0=0):
    n = val.shape[0]
    for s in range(SUBLANES):
        ref[pl.ds(row0 * SUBLANES + s, n, stride=SUBLANES), :] = val[:, s * LANES:(s + 1) * LANES]


def _tile_block(ref, s, n, row0=0):
    return ref[pl.ds(row0 * SUBLANES + s, n, stride=SUBLANES), :]


def _pack_rows(ref, val, row0=0):
    n = val.shape[0]
    for s in range(PACK_ROWS):
        lo = val[:, (2 * s) * LANES:(2 * s + 1) * LANES].astype(_BF16).astype(_F32)
        hi = val[:, (2 * s + 1) * LANES:(2 * s + 2) * LANES].astype(_BF16).astype(_F32)
        word = (lax.bitcast_convert_type(lo, jnp.uint32) >> 16) | (
            lax.bitcast_convert_type(hi, jnp.uint32) & jnp.uint32(0xFFFF0000))
        ref[pl.ds(row0 * PACK_ROWS + s, n, stride=PACK_ROWS), :] = word


def _unpack_rows(ref, n, row0=0):
    parts = []
    for s in range(PACK_ROWS):
        word = ref[pl.ds(row0 * PACK_ROWS + s, n, stride=PACK_ROWS), :]
        parts.append(lax.bitcast_convert_type(word << 16, _F32))
        parts.append(lax.bitcast_convert_type(word & jnp.uint32(0xFFFF0000), _F32))
    return jnp.concatenate(parts, axis=-1).astype(_BF16)


ROPE_HALF = HEAD_DIM // 2
ROPE_PACK = LANES // ROPE_HALF


def _rope_tables(pos_ref, inv_ref, cos_ref, sin_ref):
    ang = pos_ref[...].astype(_F32) * inv_ref[...]
    rows = ang.shape[0]
    lane = _lane(ang.shape)
    sign = jnp.where((lane % HEAD_DIM) < ROPE_HALF, -1.0, 1.0)
    for out_ref, val in ((cos_ref, jnp.cos(ang)), (sin_ref, jnp.sin(ang))):
        for j in range(ROPE_PACK):
            seg = jnp.where(lane // ROPE_HALF == j, val, 0.0)
            full = seg
            for k in range(1, ROPE_PACK):
                full = full + pltpu.roll(seg, k * ROPE_HALF, 1)
            if out_ref is sin_ref:
                full = full * sign
            out_ref[pl.ds(j, rows, stride=ROPE_PACK), :] = full


def _rope_inputs(positions):
    b, s = positions.shape
    inv = ROPE_BASE ** (-jnp.arange(ROPE_HALF, dtype=_F32) / ROPE_HALF)
    inv128 = jnp.tile(inv, ROPE_PACK).reshape(1, LANES)
    pos = jnp.repeat(positions.reshape(b, s // ROPE_PACK, ROPE_PACK), ROPE_HALF, axis=2)
    return pos, inv128


def _mem_kv_kernel(mem_ref, g_ref, w_ref, kn_ref, k_ref, v_ref):
    m = mem_ref[...]
    ms = jnp.mean(m * m, axis=-1, keepdims=True)
    mn = (m * lax.rsqrt(ms + EPS)) * g_ref[...]
    kv = _dot(mn.astype(_BF16), w_ref[...])
    for j in range(C_WIDTH // LANES):
        sl = slice(j * LANES, (j + 1) * LANES)
        k_ref[:, sl] = _pair_rms(kv[:, sl], kn_ref[...]).astype(_BF16)
    v_ref[...] = kv[:, C_WIDTH:].T.astype(_BF16)


def _mem_kv(mem, mem_norm_g, w_mem_kv, kn_c):
    b, m, d = mem.shape
    kn = jnp.tile(kn_c, 2).reshape(1, LANES)
    return pl.pallas_call(
        _mem_kv_kernel,
        grid=(b,),
        in_specs=[pl.BlockSpec((None, m, d), lambda i: (i, 0, 0)),
                  pl.BlockSpec((1, d), lambda i: (0, 0)),
                  pl.BlockSpec((d, 2 * C_WIDTH), lambda i: (0, 0)),
                  pl.BlockSpec((1, LANES), lambda i: (0, 0))],
        out_specs=[pl.BlockSpec((None, m, C_WIDTH), lambda i: (i, 0, 0)),
                   pl.BlockSpec((None, C_WIDTH, m), lambda i: (i, 0, 0))],
        out_shape=[jax.ShapeDtypeStruct((b, m, C_WIDTH), _BF16),
                   jax.ShapeDtypeStruct((b, C_WIDTH, m), _BF16)],
        compiler_params=_cparams(1),
        name="mem_kv",
    )(mem, mem_norm_g.reshape(1, d), w_mem_kv.astype(_BF16), kn)


def _in_proj_kernel(x_ref, pos_ref, inv_ref, g_ref, wq_ref, wk_ref, wvt_ref, wr_ref, wc_ref, qn_ref, kn_ref,
                    cn_ref, zeta_ref, qa_ref, ka_ref, vt_ref, ret_ref, gate_ref, qc_ref, xn_ref, acc_ref,
                    accb_ref, cos_ref, sin_ref):
    x = x_ref[...]
    ms = jnp.mean(x * x, axis=-1, keepdims=True)
    xn_ref[...] = ((x * lax.rsqrt(ms + EPS)) * g_ref[...]).astype(_BF16)

    def normed(slot, out_ref, gain_ref):
        for blk in range(out_ref.shape[1] // LANES):
            sl = slice(blk * LANES, (blk + 1) * LANES)
            out_ref[:, sl] = _pair_rms(acc_ref[slot, :, sl], gain_ref[...]).astype(_BF16)

    acc_ref[0] = _dot(xn_ref[...], wq_ref[...])
    acc_ref[1] = _dot(xn_ref[...], wk_ref[...])
    _rope_tables(pos_ref, inv_ref, cos_ref, sin_ref)
    normed(0, qa_ref, qn_ref)
    acc_ref[0] = _dot_nt(wvt_ref[...], xn_ref[...])
    normed(1, ka_ref, kn_ref)
    accb_ref[...] = _dot(xn_ref[...], wr_ref[...])
    for blk in range(vt_ref.shape[0]):
        vt_ref[blk] = acc_ref[0, :, blk * LANES:(blk + 1) * LANES].astype(_BF16)
    acc_ref[1, :, 0:C_WIDTH] = _dot(xn_ref[...], wc_ref[...])
    cos, sin = cos_ref[...], sin_ref[...]
    chunks = x_ref.shape[0] // RET_CHUNK
    for p in range(B_WIDTH // LANES):
        sl = slice(p * LANES, (p + 1) * LANES)
        q = accb_ref[:, sl]
        k = accb_ref[:, B_WIDTH + p * LANES:B_WIDTH + (p + 1) * LANES]
        kr = (k * cos + _swap_halves(k) * sin) * (HEAD_DIM ** -0.5)
        ret_ref[:, sl] = (q * cos + _swap_halves(q) * sin).astype(_BF16)
        ret_ref[:, B_WIDTH + p * LANES:B_WIDTH + (p + 1) * LANES] = kr.astype(_BF16)
        ret_ref[:, 2 * B_WIDTH + p * LANES:2 * B_WIDTH + (p + 1) * LANES] = (
            kr * jnp.concatenate([zeta_ref[p]] * chunks, axis=0)).astype(_BF16)
        ret_ref[:, 3 * B_WIDTH + p * LANES:3 * B_WIDTH + (p + 1) * LANES] = accb_ref[
            :, 2 * B_WIDTH + p * LANES:2 * B_WIDTH + (p + 1) * LANES].astype(_BF16)
        gate_ref[:, sl] = accb_ref[:, 3 * B_WIDTH + p * LANES:3 * B_WIDTH + (p + 1) * LANES]
    normed(1, qc_ref, cn_ref)


def _in_proj(x3, positions, zeta, g, w_in, qn_a, kn_a, qn_c):
    b, s, d = x3.shape
    tm = min(s, 512)
    assert tm == A_WIDTH
    assert tm % RET_CHUNK == 0
    pos, inv128 = _rope_inputs(positions)
    w = w_in.astype(_BF16)
    cuts = [0, A_WIDTH, 2 * A_WIDTH, 3 * A_WIDTH, 3 * A_WIDTH + 4 * B_WIDTH, IN_COLS]
    wq, wk, wv, wr, wc = (w[:, lo:hi] for lo, hi in zip(cuts[:-1], cuts[1:]))
    scale = HEAD_DIM ** -0.5 * LOG2E
    gains = [(jnp.tile(gn, 2) * sc).reshape(1, LANES) for gn, sc in ((qn_a, scale), (kn_a, 1.0), (qn_c, scale))]

    def whole(arr):
        return pl.BlockSpec(arr.shape, lambda i, j: (0,) * arr.ndim)

    def rows(width):
        return pl.BlockSpec((None, tm, width), lambda i, j: (i, j, 0))

    consts = [inv128, g.reshape(1, d), wq, wk, wv.T, wr, wc] + gains + [zeta]
    return pl.pallas_call(
        _in_proj_kernel,
        grid=(b, s // tm),
        in_specs=[rows(d), pl.BlockSpec((None, tm // ROPE_PACK, LANES), lambda i, j: (i, j, 0))]
        + [whole(c) for c in consts],
        out_specs=[rows(A_WIDTH), rows(A_WIDTH),
                   pl.BlockSpec((None, tm // LANES, A_WIDTH, LANES), lambda i, j: (i, j, 0, 0)),
                   rows(4 * B_WIDTH), rows(B_WIDTH), rows(C_WIDTH)],
        out_shape=[jax.ShapeDtypeStruct((b, s, A_WIDTH), _BF16), jax.ShapeDtypeStruct((b, s, A_WIDTH), _BF16),
                   jax.ShapeDtypeStruct((b, s // LANES, A_WIDTH, LANES), _BF16),
                   jax.ShapeDtypeStruct((b, s, 4 * B_WIDTH), _BF16), jax.ShapeDtypeStruct((b, s, B_WIDTH), _F32),
                   jax.ShapeDtypeStruct((b, s, C_WIDTH), _BF16)],
        scratch_shapes=[pltpu.VMEM((tm, d), _BF16), pltpu.VMEM((2, tm, A_WIDTH), _F32),
                        pltpu.VMEM((tm, 4 * B_WIDTH), _F32), pltpu.VMEM((tm, LANES), _F32),
                        pltpu.VMEM((tm, LANES), _F32)],
        compiler_params=_cparams(2),
        name="in_proj",
    )(x3, pos, *consts)


def _attn_kernel(q_ref, k_ref, vt_ref, bias_ref, o_ref, kp_ref, st_ref, var_ref, *, q_rows):
    qs = pl.program_id(2)
    s = k_ref.shape[0]
    fill_rows = min(s, 1024)
    left_blocks = LEFT_ROWS // LANES

    @pl.when(qs == 0)
    def _():
        kp_ref[0:LEFT_ROWS, :] = jnp.zeros((LEFT_ROWS, LANES), _BF16)

        def fill(i, carry):
            r = pl.multiple_of(i * fill_rows, fill_rows)
            kp_ref[pl.ds(LEFT_ROWS + r, fill_rows), :] = k_ref[pl.ds(r, fill_rows), :]
            return carry

        lax.fori_loop(0, s // fill_rows, fill, 0)
        key = lax.broadcasted_iota(jnp.int32, (ATT_K, 2 * ATT_Q), 0)
        for v in range(ATT_VARIANTS):
            var_ref[v] = jnp.where(key >= LEFT_ROWS - ATT_Q * v, bias_ref[...], NEG_INF)

    low = _lane((ATT_Q, LANES)) < HEAD_DIM
    ones = jnp.ones((ONES_ROWS, ATT_K), _BF16)
    tiles_per_step = q_rows // ATT_Q

    def scores(j):
        cp = qs * tiles_per_step + j
        q = q_ref[j * ATT_Q:(j + 1) * ATT_Q, :]
        q2 = jnp.concatenate([jnp.where(low, q, jnp.zeros_like(q)), jnp.where(low, jnp.zeros_like(q), q)], axis=0)
        kb = kp_ref[pl.ds(pl.multiple_of(cp * ATT_Q, ATT_Q), ATT_K), :]
        st_ref[j % (ATT_AHEAD + 1)] = _dot_nt(kb, q2) + var_ref[jnp.minimum(cp, ATT_VARIANTS - 1)]

    def finish(j):
        cp = qs * tiles_per_step + j
        st = st_ref[j % (ATT_AHEAD + 1)]
        m = jnp.max(st, axis=0, keepdims=True)
        p = jnp.exp2(st - m)
        vt = jnp.concatenate([vt_ref[jnp.maximum(cp + kb_i - left_blocks, 0)] for kb_i in range(ATT_K // LANES)],
                             axis=1)
        ot = _dot(jnp.concatenate([vt, ones], axis=0), p.astype(_BF16))
        inv = 1.0 / ot[LANES:LANES + 1, :]
        out_t = jnp.concatenate([ot[0:HEAD_DIM, 0:ATT_Q] * inv[:, 0:ATT_Q],
                                 ot[HEAD_DIM:LANES, ATT_Q:] * inv[:, ATT_Q:]], axis=0)
        o_ref[j * ATT_Q:(j + 1) * ATT_Q, :] = out_t.T.astype(o_ref.dtype)

    for j in range(min(ATT_AHEAD, tiles_per_step)):
        scores(j)
    for j in range(tiles_per_step):
        if j + ATT_AHEAD < tiles_per_step:
            scores(j + ATT_AHEAD)
        finish(j)


def _toeplitz_bias(rel_bias, q_len, k_len):
    h, table = rel_bias.shape
    n_diag = q_len + k_len - 1
    flat_lo = k_len - 1 - LEFT_ROWS - (CHUNK - 1)
    flat_hi = n_diag - flat_lo - table
    rev = jnp.concatenate([jnp.broadcast_to(rel_bias[:, -1:], (h, flat_hi)), rel_bias[:, ::-1],
                           jnp.broadcast_to(rel_bias[:, :1], (h, flat_lo))], axis=1).astype(_F32)
    flat = jnp.tile(rev, (1, q_len + 1))
    pitch = n_diag - 1
    skew = flat[:, q_len - 1:q_len - 1 + q_len * pitch].reshape(h, q_len, pitch)
    return skew[:, :, :k_len]


def _attn_bias(rel_bias):
    h = rel_bias.shape[0]
    bias = _toeplitz_bias(rel_bias, ATT_Q, ATT_K)
    q = lax.broadcasted_iota(jnp.int32, (ATT_Q, ATT_K), 0)
    k = lax.broadcasted_iota(jnp.int32, (ATT_Q, ATT_K), 1)
    off = k // CHUNK - q // CHUNK
    in_band = (off >= 0) & (off < BAND_CHUNKS)
    full = jnp.where(in_band[None], bias * LOG2E, NEG_INF)
    full = full.reshape(h // 2, 2, ATT_Q, ATT_K)
    return full.transpose(0, 3, 1, 2).reshape(h // 2, ATT_K, 2 * ATT_Q)


def _attention(qa, ka, vta, rel_bias):
    b, s, _ = qa.shape
    q_rows = min(s, 2048)
    pairs = A_HEADS // 2
    return pl.pallas_call(
        functools.partial(_attn_kernel, q_rows=q_rows),
        grid=(b, pairs, s // q_rows),
        in_specs=[pl.BlockSpec((None, q_rows, LANES), lambda i, p, j: (i, j, p)),
                  pl.BlockSpec((None, s, LANES), lambda i, p, j: (i, 0, p)),
                  pl.BlockSpec((None, s // LANES, LANES, LANES), lambda i, p, j: (i, 0, p, 0)),
                  pl.BlockSpec((None, ATT_K, 2 * ATT_Q), lambda i, p, j: (p, 0, 0))],
        out_specs=pl.BlockSpec((None, q_rows, LANES), lambda i, p, j: (i, j, p)),
        out_shape=jax.ShapeDtypeStruct((b, s, A_WIDTH), _BF16),
        scratch_shapes=[pltpu.VMEM((s + LEFT_ROWS, LANES), _BF16),
                        pltpu.VMEM((ATT_AHEAD + 1, ATT_K, 2 * ATT_Q), _F32),
                        pltpu.VMEM((ATT_VARIANTS, ATT_K, 2 * ATT_Q), _F32)],
        compiler_params=_cparams(3),
        name="attn_a",
    )(qa, ka, vta, _attn_bias(rel_bias))


def _swap_halves(t):
    first = (_lane(t.shape) % HEAD_DIM) < (HEAD_DIM // 2)
    return jnp.where(first, pltpu.roll(t, LANES - HEAD_DIM // 2, 1), pltpu.roll(t, HEAD_DIM // 2, 1))


def _retention_kernel(q_ref, k_ref, kz_ref, v_ref, gate_ref, decay_ref, xi_ref, cd_ref, gn_ref, o_ref,
                      state_ref, *, rows):
    @pl.when(pl.program_id(2) == 0)
    def _():
        state_ref[...] = jnp.zeros_like(state_ref)

    c = RET_CHUNK
    low = _lane((c, LANES)) < HEAD_DIM
    eye = jnp.where(lax.broadcasted_iota(jnp.int32, (LANES, LANES), 0) == _lane((LANES, LANES)),
                    1.0, 0.0).astype(_BF16)
    srow = lax.broadcasted_iota(jnp.int32, (LANES, LANES), 0) < HEAD_DIM
    scol = _lane((LANES, LANES)) < HEAD_DIM
    same_head = srow == scol

    for j in range(rows // c):
        sl = slice(j * c, (j + 1) * c)
        qb = q_ref[sl, :]
        kb = k_ref[sl, :]
        vb = v_ref[sl, :]
        inner_out = []
        for h in range(2):
            qh = jnp.where(low if h == 0 else ~low, qb, jnp.zeros_like(qb))
            inner = _dot_nt(qh, kb) * decay_ref[h]
            inner_out.append(_dot(inner.astype(_BF16), vb))
        state = state_ref[...]
        cross = _dot(qb, state.astype(_BF16)) * xi_ref[...]
        o = jnp.where(low, inner_out[0], inner_out[1]) + cross
        kz = _dot_nt(eye, kz_ref[sl, :]).astype(_BF16)
        state_ref[...] = cd_ref[...] * state + jnp.where(same_head, _dot(kz, vb), 0.0)
        mu = jnp.where(low,
                       jnp.sum(jnp.where(low, o, 0.0), axis=-1, keepdims=True),
                       jnp.sum(jnp.where(low, 0.0, o), axis=-1, keepdims=True)) * (1.0 / HEAD_DIM)
        dlt = o - mu
        d2 = dlt * dlt
        var = jnp.where(low,
                        jnp.sum(jnp.where(low, d2, 0.0), axis=-1, keepdims=True),
                        jnp.sum(jnp.where(low, 0.0, d2), axis=-1, keepdims=True)) * (1.0 / HEAD_DIM)
        y = (dlt * lax.rsqrt(var + EPS)) * gn_ref[...]
        g = gate_ref[sl, :]
        o_ref[sl, :] = ((g * jax.nn.sigmoid(g)) * y).astype(o_ref.dtype)


def _retention_tables():
    c = RET_CHUNK
    log_g = jnp.log(1.0 - jnp.exp2(-5.0 - jnp.arange(B_HEADS, dtype=_F32)))
    idx = jnp.arange(c, dtype=_F32)
    diff = idx[:, None] - idx[None, :]
    decay = jnp.where(diff >= 0, jnp.exp(log_g[:, None, None] * jnp.maximum(diff, 0.0)), 0.0)
    zeta = jnp.exp(log_g[:, None] * (c - 1 - idx))
    xi = jnp.exp(log_g[:, None] * (idx + 1.0))
    cd = jnp.exp(log_g * c)

    def lanes(tab):
        return jnp.repeat(tab.reshape(B_HEADS // 2, 2, c), HEAD_DIM, axis=1).transpose(0, 2, 1)

    cdm = jnp.repeat(cd.reshape(B_HEADS // 2, 2), HEAD_DIM, axis=1)
    cdm = jnp.broadcast_to(cdm[:, :, None], (B_HEADS // 2, LANES, LANES))
    return decay, lanes(zeta), lanes(xi), cdm


def _retention(qkzv, gate, tables, ret_gn_g):
    b, s, _ = qkzv.shape
    rows = min(s, 2048)
    pairs = B_HEADS // 2
    decay, _, xi, cdm = tables
    gn = ret_gn_g.reshape(pairs, 1, LANES)

    def col(off):
        return pl.BlockSpec((None, rows, LANES), lambda i, p, j: (i, j, off * pairs + p))

    return pl.pallas_call(
        functools.partial(_retention_kernel, rows=rows),
        grid=(b, pairs, s // rows),
        in_specs=[col(0), col(1), col(2), col(3), col(0),
                  pl.BlockSpec((2, RET_CHUNK, RET_CHUNK), lambda i, p, j: (p, 0, 0)),
                  pl.BlockSpec((None, RET_CHUNK, LANES), lambda i, p, j: (p, 0, 0)),
                  pl.BlockSpec((None, LANES, LANES), lambda i, p, j: (p, 0, 0)),
                  pl.BlockSpec((None, 1, LANES), lambda i, p, j: (p, 0, 0))],
        out_specs=pl.BlockSpec((None, rows, LANES), lambda i, p, j: (i, j, p)),
        out_shape=jax.ShapeDtypeStruct((b, s, B_WIDTH), _BF16),
        scratch_shapes=[pltpu.VMEM((LANES, LANES), _F32)],
        compiler_params=_cparams(3),
        name="retention_b",
    )(qkzv, qkzv, qkzv, qkzv, gate, decay, xi, cdm, gn)


def _cross_kernel(q_ref, k_ref, vt_ref, o_ref, st_ref, *, rows):
    low = _lane((ATT_Q, LANES)) < HEAD_DIM
    lane_blocks = C_WIDTH // LANES
    tiles = [(j, lb) for j in range(rows // ATT_Q) for lb in range(lane_blocks)]
    ones = jnp.ones((ONES_ROWS, vt_ref.shape[1]), _BF16)

    def scores(i):
        j, lb = tiles[i]
        sl = slice(lb * LANES, (lb + 1) * LANES)
        q = q_ref[j * ATT_Q:(j + 1) * ATT_Q, sl]
        q2 = jnp.concatenate([jnp.where(low, q, jnp.zeros_like(q)), jnp.where(low, jnp.zeros_like(q), q)], axis=0)
        st_ref[i % (ATT_AHEAD + 1)] = _dot_nt(k_ref[:, sl], q2)

    def finish(i):
        j, lb = tiles[i]
        sl = slice(lb * LANES, (lb + 1) * LANES)
        st = st_ref[i % (ATT_AHEAD + 1)]
        p = jnp.exp2(st - jnp.max(st, axis=0, keepdims=True))
        ot = _dot(jnp.concatenate([vt_ref[sl, :], ones], axis=0), p.astype(_BF16))
        inv = 1.0 / ot[LANES:LANES + 1, :]
        out_t = jnp.concatenate([ot[0:HEAD_DIM, 0:ATT_Q] * inv[:, 0:ATT_Q],
                                 ot[HEAD_DIM:LANES, ATT_Q:] * inv[:, ATT_Q:]], axis=0)
        o_ref[j * ATT_Q:(j + 1) * ATT_Q, sl] = out_t.T.astype(o_ref.dtype)

    for i in range(min(ATT_AHEAD, len(tiles))):
        scores(i)
    for i in range(len(tiles)):
        if i + ATT_AHEAD < len(tiles):
            scores(i + ATT_AHEAD)
        finish(i)


def _cross_attention(qc, kc, vtc):
    b, s, _ = qc.shape
    m = kc.shape[1]
    rows = min(s, 1024)
    return pl.pallas_call(
        functools.partial(_cross_kernel, rows=rows),
        grid=(b, s // rows),
        in_specs=[pl.BlockSpec((None, rows, C_WIDTH), lambda i, j: (i, j, 0)),
                  pl.BlockSpec((None, m, C_WIDTH), lambda i, j: (i, 0, 0)),
                  pl.BlockSpec((None, C_WIDTH, m), lambda i, j: (i, 0, 0))],
        out_specs=pl.BlockSpec((None, rows, C_WIDTH), lambda i, j: (i, j, 0)),
        out_shape=jax.ShapeDtypeStruct((b, s, C_WIDTH), _BF16),
        scratch_shapes=[pltpu.VMEM((ATT_AHEAD + 1, m, 2 * ATT_Q), _F32)],
        compiler_params=_cparams(2),
        name="cross_c",
    )(qc, kc, vtc)


def _out_router_kernel(x_ref, a_ref, b_ref, c_ref, wo_ref, g_ref, wr_ref, br_ref,
                       h_ref, hn_ref, info_ref, rows_ref, cnt_ref, carry_ref):
    @pl.when(pl.program_id(0) == 0)
    def _():
        carry_ref[...] = jnp.zeros_like(carry_ref)

    tm = x_ref.shape[0]
    h = x_ref[...]
    h = h + _dot(a_ref[...], wo_ref[0:A_WIDTH, :])
    h = h + _dot(b_ref[...], wo_ref[A_WIDTH:A_WIDTH + B_WIDTH, :])
    h = h + _dot(c_ref[...], wo_ref[A_WIDTH + B_WIDTH:, :])
    h_ref[...] = h
    ms = jnp.mean(h * h, axis=-1, keepdims=True)
    hn = (h * lax.rsqrt(ms + EPS)) * g_ref[...]
    _pack_rows(hn_ref, hn)
    logits = _dot_nt(wr_ref[...], hn.astype(_BF16))[0:ROUTE_ROWS, :] + br_ref[:, 0:1]
    row = lax.broadcasted_iota(jnp.int32, (ROUTE_ROWS, tm), 0).astype(_F32)
    big = float(ROUTE_ROWS)

    def first_row(mask):
        return jnp.min(jnp.where(mask, row, big), axis=0, keepdims=True)

    gmask = row < N_GROUPS
    gl = jnp.where(gmask, logits, NEG_INF)
    ge = jnp.exp(gl - jnp.max(gl, axis=0, keepdims=True))
    gp = ge / jnp.sum(ge, axis=0, keepdims=True)
    p_group = jnp.max(gp, axis=0, keepdims=True)
    g_sel = first_row(gmask & (gp == p_group))
    lo = ROUTE_LANE0 + g_sel * EXPERTS_PER_GROUP
    emask = (row >= lo) & (row < lo + EXPERTS_PER_GROUP)
    el = jnp.where(emask, logits, NEG_INF)
    ee = jnp.exp(el - jnp.max(el, axis=0, keepdims=True))
    ep = ee / jnp.sum(ee, axis=0, keepdims=True)
    p1 = jnp.max(ep, axis=0, keepdims=True)
    i1 = first_row(emask & (ep == p1))
    ep2 = jnp.where(emask & (row != i1), ep, -1.0)
    p2 = jnp.max(ep2, axis=0, keepdims=True)
    i2 = first_row(ep2 == p2)
    den = p1 + p2
    w1 = p_group * (p1 / den)
    w2 = p_group * (p2 / den)
    hit1 = row == i1
    hit2 = row == i2
    onehot = jnp.where(hit1 | hit2, 1.0, 0.0)
    r_i = lax.broadcasted_iota(jnp.int32, (tm, tm), 0)
    c_i = lax.broadcasted_iota(jnp.int32, (tm, tm), 1)
    earlier = jnp.where(r_i < c_i, 1.0, 0.0).astype(_BF16)
    before = _dot(onehot.astype(_BF16), earlier) + carry_ref[:, 0:1]
    r1 = jnp.sum(jnp.where(hit1, before, 0.0), axis=0, keepdims=True)
    r2 = jnp.sum(jnp.where(hit2, before, 0.0), axis=0, keepdims=True)
    carry_ref[...] = carry_ref[...] + jnp.sum(onehot, axis=1, keepdims=True)
    cnt_ref[...] = carry_ref[...]
    out_row = lax.broadcasted_iota(jnp.int32, (LANES, tm), 0)
    info = jnp.where(out_row == 0, w1, 0.0)
    info = jnp.where(out_row == 1, w2, info)
    info = jnp.where(out_row == 2, i1 - ROUTE_LANE0, info)
    info = jnp.where(out_row == 3, i2 - ROUTE_LANE0, info)
    info = jnp.where(out_row == 4, r1, info)
    info = jnp.where(out_row == 5, r2, info)
    rows_ref[...] = info[0:SUBLANES, :]
    info_ref[...] = info.T


def _out_router(x2, oa, ob, oc, w_out, ffn_g, w_rg, b_rg, w_re, b_re):
    t, d = x2.shape
    tm = min(t, 512)
    pad = LANES - N_GROUPS - N_EXPERTS
    wr = jnp.concatenate([w_rg, w_re, jnp.zeros((d, pad), _F32)], axis=1).T.astype(_BF16)
    br = jnp.concatenate([b_rg, b_re, jnp.zeros((ROUTE_ROWS - N_GROUPS - N_EXPERTS,), _F32)])
    br = jnp.broadcast_to(br[:, None], (ROUTE_ROWS, LANES))

    def rows(w):
        return pl.BlockSpec((tm, w), lambda i: (i, 0))

    def whole(r, c):
        return pl.BlockSpec((r, c), lambda i: (0, 0))

    return pl.pallas_call(
        _out_router_kernel,
        grid=(t // tm,),
        in_specs=[rows(d), rows(A_WIDTH), rows(B_WIDTH), rows(C_WIDTH), whole(d, d), whole(1, d),
                  whole(LANES, d), whole(ROUTE_ROWS, LANES)],
        out_specs=[rows(d), pl.BlockSpec((tm * PACK_ROWS, LANES), lambda i: (i, 0)), rows(LANES),
                   pl.BlockSpec((SUBLANES, tm), lambda i: (0, i)), whole(ROUTE_ROWS, LANES)],
        out_shape=[jax.ShapeDtypeStruct((t, d), _F32), jax.ShapeDtypeStruct((t * PACK_ROWS, LANES), jnp.uint32),
                   jax.ShapeDtypeStruct((t, LANES), _F32), jax.ShapeDtypeStruct((SUBLANES, t), _F32),
                   jax.ShapeDtypeStruct((ROUTE_ROWS, LANES), _F32)],
        scratch_shapes=[pltpu.VMEM((ROUTE_ROWS, LANES), _F32)],
        compiler_params=_cparams(1),
        name="out_router",
    )(x2, oa, ob, oc, w_out.astype(_BF16), ffn_g.reshape(1, d), wr, br)


DISPATCH_TOKENS = 2048
COMBINE_TOKENS = 512


ROW_UNROLL = 8


def _tile_rows(row, count=1, per=SUBLANES):
    start = row * per
    if not isinstance(start, int):
        start = pl.multiple_of(start, per)
    return pl.ds(start, count * per)


def _row_copy(src, s_row, dst, d_row, sem, per=SUBLANES):
    return pltpu.make_async_copy(src.at[_tile_rows(s_row, 1, per)], dst.at[_tile_rows(d_row, 1, per)], sem)


def _dispatch_kernel(pad_start_ref, pad_len_ref, used_ref, dest_ref, hn_ref, xs_ref, zero_ref, sem,
                     pad_sem):
    per = PACK_ROWS
    n = hn_ref.shape[0] // per

    @pl.when(pl.program_id(0) == 0)
    def _():
        zero_ref[...] = jnp.zeros_like(zero_ref)
        n_blocks = xs_ref.shape[0] // (ROW_BLOCK * per)

        def block_copy(blk):
            return pltpu.make_async_copy(zero_ref, xs_ref.at[_tile_rows(blk * ROW_BLOCK, ROW_BLOCK, per)],
                                         pad_sem)

        def put_block(blk, carry):
            block_copy(blk).start()
            return carry

        def done_block(blk, carry):
            block_copy(blk).wait()
            return carry

        lax.fori_loop(used_ref[0], n_blocks, put_block, 0)
        lax.fori_loop(used_ref[0], n_blocks, done_block, 0)
        bits = [1 << k for k in reversed(range(ROW_BLOCK.bit_length() - 1))]

        def tail(e, wait):
            row = pad_start_ref[e]
            for bit in bits:
                on = (pad_len_ref[e] & bit) != 0
                copy = pltpu.make_async_copy(zero_ref.at[_tile_rows(0, bit, per)],
                                             xs_ref.at[_tile_rows(row, bit, per)], pad_sem)

                @pl.when(on)
                def _():
                    copy.wait() if wait else copy.start()

                row = row + jnp.where(on, bit, 0)

        def put_tail(e, carry):
            tail(e, False)
            return carry

        def done_tail(e, carry):
            tail(e, True)
            return carry

        lax.fori_loop(0, N_EXPERTS, put_tail, 0)
        lax.fori_loop(0, N_EXPERTS, done_tail, 0)

    def issue(i, carry):
        for u in range(ROW_UNROLL):
            t = i * ROW_UNROLL + u
            _row_copy(hn_ref, t, xs_ref, dest_ref[2 * t], sem, per).start(priority=0)
            _row_copy(hn_ref, t, xs_ref, dest_ref[2 * t + 1], sem, per).start(priority=1)
        return carry

    lax.fori_loop(0, n // ROW_UNROLL, issue, 0)
    for _ in range(2):
        pltpu.make_async_copy(hn_ref, xs_ref.at[_tile_rows(0, n, per)], sem).wait()


def _dispatch(hn, dest, pad_start, pad_len, n_used, n_rows):
    t = hn.shape[0] // PACK_ROWS
    n = min(t, DISPATCH_TOKENS)
    return pl.pallas_call(
        _dispatch_kernel,
        grid_spec=pltpu.PrefetchScalarGridSpec(
            num_scalar_prefetch=3,
            grid=(t // n,),
            in_specs=[pl.BlockSpec((2 * n,), lambda i, *_: (i,), memory_space=pltpu.SMEM),
                      pl.BlockSpec((n * PACK_ROWS, LANES), lambda i, *_: (i, 0))],
            out_specs=pl.BlockSpec(memory_space=pl.ANY),
            scratch_shapes=[pltpu.VMEM((ROW_BLOCK * PACK_ROWS, LANES), hn.dtype), pltpu.SemaphoreType.DMA,
                            pltpu.SemaphoreType.DMA]),
        out_shape=jax.ShapeDtypeStruct((n_rows * PACK_ROWS, LANES), hn.dtype),
        compiler_params=_cparams(1),
        name="moe_dispatch",
    )(pad_start, pad_len, n_used, dest, hn)


def _expert_kernel(be_ref, run_ref, next_ref, used_ref, x_ref, wg_hbm, wu_hbm, wd_hbm, y_ref,
                   wg_f32, wu_f32, wd_f32, wg_bf, wu_bf, wd_bf, sem):
    i = pl.program_id(0)
    live = i < used_ref[0]
    new_expert = (i == 0) | (be_ref[i] != be_ref[jnp.maximum(i - 1, 0)])
    slot = run_ref[i] % 2

    def fetch(expert, to_slot):
        return [pltpu.make_async_copy(src.at[expert], dst.at[to_slot], sem.at[to_slot, k])
                for k, (src, dst) in enumerate(((wg_hbm, wg_f32), (wu_hbm, wu_f32), (wd_hbm, wd_f32)))]

    @pl.when(live & (i == 0))
    def _():
        for copy in fetch(be_ref[0], 0):
            copy.start()

    @pl.when(live & new_expert)
    def _():
        for copy in fetch(be_ref[i], slot):
            copy.wait()

        @pl.when(next_ref[i] >= 0)
        def _():
            for copy in fetch(next_ref[i], 1 - slot):
                copy.start()

        wg_bf[...] = wg_f32[slot].astype(_BF16)
        wu_bf[...] = wu_f32[slot].astype(_BF16)
        wd_bf[...] = wd_f32[slot].astype(_BF16)

    @pl.when(live)
    def _():
        sub = ROW_BLOCK // EXPERT_SPLIT
        gate_up = {}

        def first(k):
            x = _unpack_rows(x_ref, sub, k * sub)
            gate_up[k] = (_dot(x, wg_bf[...]), _dot(x, wu_bf[...]))

        def second(k):
            gate, up = gate_up.pop(k)
            act = (gate * jax.nn.sigmoid(gate)) * up
            _rows_to_tiles(y_ref, _dot(act.astype(_BF16), wd_bf[...]), k * sub)

        for k in range(min(EXPERT_AHEAD, EXPERT_SPLIT)):
            first(k)
        for k in range(EXPERT_SPLIT):
            if k + EXPERT_AHEAD < EXPERT_SPLIT:
                first(k + EXPERT_AHEAD)
            second(k)

    @pl.when(i >= used_ref[0])
    def _():
        y_ref[...] = jnp.zeros_like(y_ref)


def _experts(xs, blocks, w_gate, w_up, w_down):
    n_rows, d = xs.shape[0] // PACK_ROWS, D_MODEL
    n_blocks = n_rows // ROW_BLOCK
    tile_block = (ROW_BLOCK * SUBLANES, LANES)
    hbm = pl.BlockSpec(memory_space=pl.ANY)

    return pl.pallas_call(
        _expert_kernel,
        grid_spec=pltpu.PrefetchScalarGridSpec(
            num_scalar_prefetch=4,
            grid=(n_blocks,),
            in_specs=[pl.BlockSpec((ROW_BLOCK * PACK_ROWS, LANES),
                                   lambda i, be, run, nxt, used: (jnp.minimum(i, used[0] - 1), 0)),
                      hbm, hbm, hbm],
            out_specs=pl.BlockSpec(tile_block, lambda i, *_: (i, 0)),
            scratch_shapes=[pltpu.VMEM((2, d, D_EXPERT), _F32), pltpu.VMEM((2, d, D_EXPERT), _F32),
                            pltpu.VMEM((2, D_EXPERT, d), _F32),
                            pltpu.VMEM((d, D_EXPERT), _BF16), pltpu.VMEM((d, D_EXPERT), _BF16),
                            pltpu.VMEM((D_EXPERT, d), _BF16), pltpu.SemaphoreType.DMA((2, 3))]),
        out_shape=jax.ShapeDtypeStruct((n_rows * SUBLANES, LANES), _F32),
        compiler_params=_cparams(1),
        name="moe_experts",
    )(*blocks, xs, w_gate, w_up, w_down)


def _combine_kernel(dest_ref, next_ref, h_ref, info_ref, ys_ref, o_ref, buf_ref, sem):
    n = h_ref.shape[0]
    step = pl.program_id(0)
    slot = step % 2

    def gather(idx_ref, to_slot):
        def issue(i, carry):
            for u in range(ROW_UNROLL):
                t = i * ROW_UNROLL + u
                _row_copy(ys_ref, idx_ref[2 * t], buf_ref.at[to_slot, 0], t,
                          sem.at[to_slot]).start(priority=0)
                _row_copy(ys_ref, idx_ref[2 * t + 1], buf_ref.at[to_slot, 1], t,
                          sem.at[to_slot]).start(priority=1)
            return carry

        lax.fori_loop(0, n // ROW_UNROLL, issue, 0)

    @pl.when(step == 0)
    def _():
        gather(dest_ref, 0)

    @pl.when(step + 1 < pl.num_programs(0))
    def _():
        gather(next_ref, 1 - slot)

    for k in range(2):
        pltpu.make_async_copy(ys_ref.at[_tile_rows(0, n)], buf_ref.at[slot, k], sem.at[slot]).wait()
    info = info_ref[...]
    w0 = info[:, 0:1]
    w1 = info[:, 1:2]
    for s in range(SUBLANES):
        sl = slice(s * LANES, (s + 1) * LANES)
        moe = w0 * _tile_block(buf_ref.at[slot, 0], s, n) + w1 * _tile_block(buf_ref.at[slot, 1], s, n)
        o_ref[:, sl] = h_ref[:, sl] + moe


def _combine(h, info, ys, dest):
    t, d = h.shape
    n = min(t, COMBINE_TOKENS)
    steps = t // n
    return pl.pallas_call(
        _combine_kernel,
        grid=(steps,),
        in_specs=[pl.BlockSpec((2 * n,), lambda i: (i,), memory_space=pltpu.SMEM),
                  pl.BlockSpec((2 * n,), lambda i: (jnp.minimum(i + 1, steps - 1),),
                               memory_space=pltpu.SMEM),
                  pl.BlockSpec((n, d), lambda i: (i, 0)),
                  pl.BlockSpec((n, LANES), lambda i: (i, 0)),
                  pl.BlockSpec(memory_space=pl.ANY)],
        out_specs=pl.BlockSpec((n, d), lambda i: (i, 0)),
        out_shape=jax.ShapeDtypeStruct((t, d), _F32),
        scratch_shapes=[pltpu.VMEM((2, 2, n * SUBLANES, LANES), _F32), pltpu.SemaphoreType.DMA((2,))],
        compiler_params=_cparams(1),
        name="moe_combine",
    )(dest, dest, h, info, ys)


def _moe_layout(route_rows, counts, t):
    counts = counts[ROUTE_LANE0:ROUTE_LANE0 + N_EXPERTS, 0].astype(jnp.int32)
    padded = (counts + ROW_BLOCK - 1) // ROW_BLOCK * ROW_BLOCK
    pends = jnp.cumsum(padded)
    pstarts = pends - padded
    eid = route_rows[2:4].astype(jnp.int32)
    rank = route_rows[4:6].astype(jnp.int32)
    experts = jnp.arange(N_EXPERTS, dtype=jnp.int32)
    start_of = jnp.sum(jnp.where(eid[:, :, None] == experts, pstarts, 0), axis=-1)
    dest = (start_of + rank).T.reshape(-1)
    n_blocks = -(-2 * t // ROW_BLOCK) + N_EXPERTS
    first_row = jnp.arange(n_blocks, dtype=jnp.int32) * ROW_BLOCK
    block_e = jnp.minimum(jnp.sum((pends[None, :] <= first_row[:, None]).astype(jnp.int32), axis=1),
                          N_EXPERTS - 1)
    n_used = (pends[-1:] // ROW_BLOCK).astype(jnp.int32)
    changed = jnp.concatenate([jnp.zeros((1,), jnp.int32), (block_e[1:] != block_e[:-1]).astype(jnp.int32)])
    block_run = jnp.cumsum(changed)
    later = (counts[None, :] > 0) & (experts[None, :] > experts[:, None])
    next_expert = jnp.min(jnp.where(later, experts[None, :], N_EXPERTS), axis=1)
    next_expert = jnp.where(next_expert < N_EXPERTS, next_expert, -1)
    block_next = jnp.sum(jnp.where(block_e[:, None] == experts[None, :], next_expert[None, :], 0), axis=1)
    blocks = (block_e, block_run.astype(jnp.int32), block_next.astype(jnp.int32), n_used)
    return dest, blocks, pstarts + counts, padded - counts, n_blocks * ROW_BLOCK


def kernel(x, mem, positions, mix_norm_g, w_in, qn_a, kn_a, rel_bias, ret_gn_g, mem_norm_g, w_mem_kv,
           qn_c, kn_c, w_out, ffn_norm_g, w_router_group, b_router_group, w_router_expert,
           b_router_expert, w_gate, w_up, w_down):
    b, s, d = x.shape
    t = b * s
    x2 = x.reshape(t, d)
    kc, vc = _mem_kv(mem, mem_norm_g, w_mem_kv, kn_c)
    tables = _retention_tables()
    qa, ka, vta, qkzv, gate, qc = _in_proj(x, positions, tables[1], mix_norm_g, w_in, qn_a, kn_a, qn_c)
    out_a = _attention(qa, ka, vta, rel_bias)
    out_b = _retention(qkzv, gate, tables, ret_gn_g)
    out_c = _cross_attention(qc, kc, vc)
    h, hn, info, route_rows, counts = _out_router(
        x2, out_a.reshape(t, A_WIDTH), out_b.reshape(t, B_WIDTH), out_c.reshape(t, C_WIDTH),
        w_out, ffn_norm_g, w_router_group, b_router_group, w_router_expert, b_router_expert)
    dest, blocks, pad_start, pad_len, n_rows = _moe_layout(route_rows, counts, t)
    xs = _dispatch(hn, dest, pad_start, pad_len, blocks[-1], n_rows)
    ys = _experts(xs, blocks, w_gate, w_up, w_down)
    return _combine(h, info, ys, dest).reshape(b, s, d)
```

```python
import functools

import jax
import jax.numpy as jnp
from jax import lax
from jax.experimental import pallas as pl
from jax.experimental.pallas import tpu as pltpu

D_MODEL = 1024
CHUNK = 64
HEAD_DIM = 64
A_HEADS = 8
B_HEADS = 4
C_HEADS = 4
A_WIDTH = A_HEADS * HEAD_DIM
B_WIDTH = B_HEADS * HEAD_DIM
C_WIDTH = C_HEADS * HEAD_DIM
IN_COLS = 3 * A_WIDTH + 4 * B_WIDTH + C_WIDTH
LEFT_CHUNKS = 8
BAND_CHUNKS = LEFT_CHUNKS + 1
MAX_REL_DIST = 128
ROPE_BASE = 10000.0
N_GROUPS = 4
EXPERTS_PER_GROUP = 8
N_EXPERTS = N_GROUPS * EXPERTS_PER_GROUP
D_EXPERT = D_MODEL // 2
EPS = 1e-6
NEG_INF = -1e30
LOG2E = 1.4426950408889634

LANES = 128
SUBLANES = 8
assert D_MODEL == SUBLANES * LANES
PACK_ROWS = SUBLANES // 2
LEFT_ROWS = LEFT_CHUNKS * CHUNK
ATT_Q = 2 * CHUNK
ATT_K = ATT_Q + LEFT_ROWS
ATT_VARIANTS = LEFT_ROWS // ATT_Q + 1
ONES_ROWS = 16
ATT_AHEAD = 3
RET_CHUNK = 256
ROW_BLOCK = 512
EXPERT_SPLIT = 2
EXPERT_AHEAD = 2
ROUTE_LANE0 = N_GROUPS
ROUTE_ROWS = 64
VMEM_LIMIT = 48 * 1024 * 1024

_F32 = jnp.float32
_BF16 = jnp.bfloat16


def _cparams(n_axes):
    return pltpu.CompilerParams(dimension_semantics=("arbitrary",) * n_axes,
                                vmem_limit_bytes=VMEM_LIMIT)


def _dot(a, b):
    return jnp.dot(a, b, preferred_element_type=_F32)


def _dot_nt(a, b):
    return lax.dot_general(a, b, (((1,), (1,)), ((), ())), preferred_element_type=_F32)


def _lane(shape):
    return lax.broadcasted_iota(jnp.int32, shape, len(shape) - 1)


def _pair_rms(t, gain):
    low = _lane(t.shape) < HEAD_DIM
    t2 = t * t
    ms0 = jnp.sum(jnp.where(low, t2, 0.0), axis=-1, keepdims=True) * (1.0 / HEAD_DIM)
    ms1 = jnp.sum(jnp.where(low, 0.0, t2), axis=-1, keepdims=True) * (1.0 / HEAD_DIM)
    r = jnp.where(low, lax.rsqrt(ms0 + EPS), lax.rsqrt(ms1 + EPS))
    return (t * r) * gain


def _rows_to_tiles(ref, val, row0=0):
    n = val.shape[0]
    for s in range(SUBLANES):
        ref[pl.ds(row0 * SUBLANES + s, n, stride=SUBLANES), :] = val[:, s * LANES:(s + 1) * LANES]


def _tile_block(ref, s, n, row0=0):
    return ref[pl.ds(row0 * SUBLANES + s, n, stride=SUBLANES), :]


def _pack_rows(ref, val, row0=0):
    n = val.shape[0]
    for s in range(PACK_ROWS):
        lo = val[:, (2 * s) * LANES:(2 * s + 1) * LANES].astype(_BF16).astype(_F32)
        hi = val[:, (2 * s + 1) * LANES:(2 * s + 2) * LANES].astype(_BF16).astype(_F32)
        word = (lax.bitcast_convert_type(lo, jnp.uint32) >> 16) | (
            lax.bitcast_convert_type(hi, jnp.uint32) & jnp.uint32(0xFFFF0000))
        ref[pl.ds(row0 * PACK_ROWS + s, n, stride=PACK_ROWS), :] = word


def _unpack_rows(ref, n, row0=0):
    parts = []
    for s in range(PACK_ROWS):
        word = ref[pl.ds(row0 * PACK_ROWS + s, n, stride=PACK_ROWS), :]
        parts.append(lax.bitcast_convert_type(word << 16, _F32))
        parts.append(lax.bitcast_convert_type(word & jnp.uint32(0xFFFF0000), _F32))
    return jnp.concatenate(parts, axis=-1).astype(_BF16)


ROPE_HALF = HEAD_DIM // 2
ROPE_PACK = LANES // ROPE_HALF


def _rope_tables(pos_ref, inv_ref, cos_ref, sin_ref):
    ang = pos_ref[...].astype(_F32) * inv_ref[...]
    rows = ang.shape[0]
    lane = _lane(ang.shape)
    sign = jnp.where((lane % HEAD_DIM) < ROPE_HALF, -1.0, 1.0)
    for out_ref, val in ((cos_ref, jnp.cos(ang)), (sin_ref, jnp.sin(ang))):
        for j in range(ROPE_PACK):
            seg = jnp.where(lane // ROPE_HALF == j, val, 0.0)
            full = seg
            for k in range(1, ROPE_PACK):
                full = full + pltpu.roll(seg, k * ROPE_HALF, 1)
            if out_ref is sin_ref:
                full = full * sign
            out_ref[pl.ds(j, rows, stride=ROPE_PACK), :] = full


def _rope_inputs(positions):
    b, s = positions.shape
    inv = ROPE_BASE ** (-jnp.arange(ROPE_HALF, dtype=_F32) / ROPE_HALF)
    inv128 = jnp.tile(inv, ROPE_PACK).reshape(1, LANES)
    pos = jnp.repeat(positions.reshape(b, s // ROPE_PACK, ROPE_PACK), ROPE_HALF, axis=2)
    return pos, inv128


def _mem_kv_kernel(mem_ref, g_ref, w_ref, kn_ref, k_ref, v_ref):
    m = mem_ref[...]
    ms = jnp.mean(m * m, axis=-1, keepdims=True)
    mn = (m * lax.rsqrt(ms + EPS)) * g_ref[...]
    kv = _dot(mn.astype(_BF16), w_ref[...])
    for j in range(C_WIDTH // LANES):
        sl = slice(j * LANES, (j + 1) * LANES)
        k_ref[:, sl] = _pair_rms(kv[:, sl], kn_ref[...]).astype(_BF16)
    v_ref[...] = kv[:, C_WIDTH:].T.astype(_BF16)


def _mem_kv(mem, mem_norm_g, w_mem_kv, kn_c):
    b, m, d = mem.shape
    kn = jnp.tile(kn_c, 2).reshape(1, LANES)
    return pl.pallas_call(
        _mem_kv_kernel,
        grid=(b,),
        in_specs=[pl.BlockSpec((None, m, d), lambda i: (i, 0, 0)),
                  pl.BlockSpec((1, d), lambda i: (0, 0)),
                  pl.BlockSpec((d, 2 * C_WIDTH), lambda i: (0, 0)),
                  pl.BlockSpec((1, LANES), lambda i: (0, 0))],
        out_specs=[pl.BlockSpec((None, m, C_WIDTH), lambda i: (i, 0, 0)),
                   pl.BlockSpec((None, C_WIDTH, m), lambda i: (i, 0, 0))],
        out_shape=[jax.ShapeDtypeStruct((b, m, C_WIDTH), _BF16),
                   jax.ShapeDtypeStruct((b, C_WIDTH, m), _BF16)],
        compiler_params=_cparams(1),
        name="mem_kv",
    )(mem, mem_norm_g.reshape(1, d), w_mem_kv.astype(_BF16), kn)


def _in_proj_kernel(x_ref, pos_ref, inv_ref, g_ref, wq_ref, wk_ref, wvt_ref, wr_ref, wc_ref, qn_ref, kn_ref,
                    cn_ref, zeta_ref, qa_ref, ka_ref, vt_ref, ret_ref, gate_ref, qc_ref, xn_ref, acc_ref,
                    accb_ref, cos_ref, sin_ref):
    x = x_ref[...]
    ms = jnp.mean(x * x, axis=-1, keepdims=True)
    xn_ref[...] = ((x * lax.rsqrt(ms + EPS)) * g_ref[...]).astype(_BF16)

    def normed(slot, out_ref, gain_ref):
        for blk in range(out_ref.shape[1] // LANES):
            sl = slice(blk * LANES, (blk + 1) * LANES)
            out_ref[:, sl] = _pair_rms(acc_ref[slot, :, sl], gain_ref[...]).astype(_BF16)

    acc_ref[0] = _dot(xn_ref[...], wq_ref[...])
    acc_ref[1] = _dot(xn_ref[...], wk_ref[...])
    _rope_tables(pos_ref, inv_ref, cos_ref, sin_ref)
    normed(0, qa_ref, qn_ref)
    acc_ref[0] = _dot_nt(wvt_ref[...], xn_ref[...])
    normed(1, ka_ref, kn_ref)
    accb_ref[...] = _dot(xn_ref[...], wr_ref[...])
    for blk in range(vt_ref.shape[0]):
        vt_ref[blk] = acc_ref[0, :, blk * LANES:(blk + 1) * LANES].astype(_BF16)
    acc_ref[1, :, 0:C_WIDTH] = _dot(xn_ref[...], wc_ref[...])
    cos, sin = cos_ref[...], sin_ref[...]
    chunks = x_ref.shape[0] // RET_CHUNK
    for p in range(B_WIDTH // LANES):
        sl = slice(p * LANES, (p + 1) * LANES)
        q = accb_ref[:, sl]
        k = accb_ref[:, B_WIDTH + p * LANES:B_WIDTH + (p + 1) * LANES]
        kr = (k * cos + _swap_halves(k) * sin) * (HEAD_DIM ** -0.5)
        ret_ref[:, sl] = (q * cos + _swap_halves(q) * sin).astype(_BF16)
        ret_ref[:, B_WIDTH + p * LANES:B_WIDTH + (p + 1) * LANES] = kr.astype(_BF16)
        ret_ref[:, 2 * B_WIDTH + p * LANES:2 * B_WIDTH + (p + 1) * LANES] = (
            kr * jnp.concatenate([zeta_ref[p]] * chunks, axis=0)).astype(_BF16)
        ret_ref[:, 3 * B_WIDTH + p * LANES:3 * B_WIDTH + (p + 1) * LANES] = accb_ref[
            :, 2 * B_WIDTH + p * LANES:2 * B_WIDTH + (p + 1) * LANES].astype(_BF16)
        gate_ref[:, sl] = accb_ref[:, 3 * B_WIDTH + p * LANES:3 * B_WIDTH + (p + 1) * LANES]
    normed(1, qc_ref, cn_ref)


def _in_proj(x3, positions, zeta, g, w_in, qn_a, kn_a, qn_c):
    b, s, d = x3.shape
    tm = min(s, 512)
    assert tm == A_WIDTH
    assert tm % RET_CHUNK == 0
    pos, inv128 = _rope_inputs(positions)
    w = w_in.astype(_BF16)
    cuts = [0, A_WIDTH, 2 * A_WIDTH, 3 * A_WIDTH, 3 * A_WIDTH + 4 * B_WIDTH, IN_COLS]
    wq, wk, wv, wr, wc = (w[:, lo:hi] for lo, hi in zip(cuts[:-1], cuts[1:]))
    scale = HEAD_DIM ** -0.5 * LOG2E
    gains = [(jnp.tile(gn, 2) * sc).reshape(1, LANES) for gn, sc in ((qn_a, scale), (kn_a, 1.0), (qn_c, scale))]

    def whole(arr):
        return pl.BlockSpec(arr.shape, lambda i, j: (0,) * arr.ndim)

    def rows(width):
        return pl.BlockSpec((None, tm, width), lambda i, j: (i, j, 0))

    consts = [inv128, g.reshape(1, d), wq, wk, wv.T, wr, wc] + gains + [zeta]
    return pl.pallas_call(
        _in_proj_kernel,
        grid=(b, s // tm),
        in_specs=[rows(d), pl.BlockSpec((None, tm // ROPE_PACK, LANES), lambda i, j: (i, j, 0))]
        + [whole(c) for c in consts],
        out_specs=[rows(A_WIDTH), rows(A_WIDTH),
                   pl.BlockSpec((None, tm // LANES, A_WIDTH, LANES), lambda i, j: (i, j, 0, 0)),
                   rows(4 * B_WIDTH), rows(B_WIDTH), rows(C_WIDTH)],
        out_shape=[jax.ShapeDtypeStruct((b, s, A_WIDTH), _BF16), jax.ShapeDtypeStruct((b, s, A_WIDTH), _BF16),
                   jax.ShapeDtypeStruct((b, s // LANES, A_WIDTH, LANES), _BF16),
                   jax.ShapeDtypeStruct((b, s, 4 * B_WIDTH), _BF16), jax.ShapeDtypeStruct((b, s, B_WIDTH), _F32),
                   jax.ShapeDtypeStruct((b, s, C_WIDTH), _BF16)],
        scratch_shapes=[pltpu.VMEM((tm, d), _BF16), pltpu.VMEM((2, tm, A_WIDTH), _F32),
                        pltpu.VMEM((tm, 4 * B_WIDTH), _F32), pltpu.VMEM((tm, LANES), _F32),
                        pltpu.VMEM((tm, LANES), _F32)],
        compiler_params=_cparams(2),
        name="in_proj",
    )(x3, pos, *consts)


def _attn_kernel(q_ref, k_ref, vt_ref, bias_ref, o_ref, kp_ref, st_ref, var_ref, *, q_rows):
    qs = pl.program_id(2)
    s = k_ref.shape[0]
    fill_rows = min(s, 1024)
    left_blocks = LEFT_ROWS // LANES

    @pl.when(qs == 0)
    def _():
        kp_ref[0:LEFT_ROWS, :] = jnp.zeros((LEFT_ROWS, LANES), _BF16)

        def fill(i, carry):
            r = pl.multiple_of(i * fill_rows, fill_rows)
            kp_ref[pl.ds(LEFT_ROWS + r, fill_rows), :] = k_ref[pl.ds(r, fill_rows), :]
            return carry

        lax.fori_loop(0, s // fill_rows, fill, 0)
        key = lax.broadcasted_iota(jnp.int32, (ATT_K, 2 * ATT_Q), 0)
        for v in range(ATT_VARIANTS):
            var_ref[v] = jnp.where(key >= LEFT_ROWS - ATT_Q * v, bias_ref[...], NEG_INF)

    low = _lane((ATT_Q, LANES)) < HEAD_DIM
    ones = jnp.ones((ONES_ROWS, ATT_K), _BF16)
    tiles_per_step = q_rows // ATT_Q

    def scores(j):
        cp = qs * tiles_per_step + j
        q = q_ref[j * ATT_Q:(j + 1) * ATT_Q, :]
        q2 = jnp.concatenate([jnp.where(low, q, jnp.zeros_like(q)), jnp.where(low, jnp.zeros_like(q), q)], axis=0)
        kb = kp_ref[pl.ds(pl.multiple_of(cp * ATT_Q, ATT_Q), ATT_K), :]
        st_ref[j % (ATT_AHEAD + 1)] = _dot_nt(kb, q2) + var_ref[jnp.minimum(cp, ATT_VARIANTS - 1)]

    def finish(j):
        cp = qs * tiles_per_step + j
        st = st_ref[j % (ATT_AHEAD + 1)]
        m = jnp.max(st, axis=0, keepdims=True)
        p = jnp.exp2(st - m)
        vt = jnp.concatenate([vt_ref[jnp.maximum(cp + kb_i - left_blocks, 0)] for kb_i in range(ATT_K // LANES)],
                             axis=1)
        ot = _dot(jnp.concatenate([vt, ones], axis=0), p.astype(_BF16))
        inv = 1.0 / ot[LANES:LANES + 1, :]
        out_t = jnp.concatenate([ot[0:HEAD_DIM, 0:ATT_Q] * inv[:, 0:ATT_Q],
                                 ot[HEAD_DIM:LANES, ATT_Q:] * inv[:, ATT_Q:]], axis=0)
        o_ref[j * ATT_Q:(j + 1) * ATT_Q, :] = out_t.T.astype(o_ref.dtype)

    for j in range(min(ATT_AHEAD, tiles_per_step)):
        scores(j)
    for j in range(tiles_per_step):
        if j + ATT_AHEAD < tiles_per_step:
            scores(j + ATT_AHEAD)
        finish(j)


def _toeplitz_bias(rel_bias, q_len, k_len):
    h, table = rel_bias.shape
    n_diag = q_len + k_len - 1
    flat_lo = k_len - 1 - LEFT_ROWS - (CHUNK - 1)
    flat_hi = n_diag - flat_lo - table
    rev = jnp.concatenate([jnp.broadcast_to(rel_bias[:, -1:], (h, flat_hi)), rel_bias[:, ::-1],
                           jnp.broadcast_to(rel_bias[:, :1], (h, flat_lo))], axis=1).astype(_F32)
    flat = jnp.tile(rev, (1, q_len + 1))
    pitch = n_diag - 1
    skew = flat[:, q_len - 1:q_len - 1 + q_len * pitch].reshape(h, q_len, pitch)
    return skew[:, :, :k_len]


def _attn_bias(rel_bias):
    h = rel_bias.shape[0]
    bias = _toeplitz_bias(rel_bias, ATT_Q, ATT_K)
    q = lax.broadcasted_iota(jnp.int32, (ATT_Q, ATT_K), 0)
    k = lax.broadcasted_iota(jnp.int32, (ATT_Q, ATT_K), 1)
    off = k // CHUNK - q // CHUNK
    in_band = (off >= 0) & (off < BAND_CHUNKS)
    full = jnp.where(in_band[None], bias * LOG2E, NEG_INF)
    full = full.reshape(h // 2, 2, ATT_Q, ATT_K)
    return full.transpose(0, 3, 1, 2).reshape(h // 2, ATT_K, 2 * ATT_Q)


def _attention(qa, ka, vta, rel_bias):
    b, s, _ = qa.shape
    q_rows = min(s, 2048)
    pairs = A_HEADS // 2
    return pl.pallas_call(
        functools.partial(_attn_kernel, q_rows=q_rows),
        grid=(b, pairs, s // q_rows),
        in_specs=[pl.BlockSpec((None, q_rows, LANES), lambda i, p, j: (i, j, p)),
                  pl.BlockSpec((None, s, LANES), lambda i, p, j: (i, 0, p)),
                  pl.BlockSpec((None, s // LANES, LANES, LANES), lambda i, p, j: (i, 0, p, 0)),
                  pl.BlockSpec((None, ATT_K, 2 * ATT_Q), lambda i, p, j: (p, 0, 0))],
        out_specs=pl.BlockSpec((None, q_rows, LANES), lambda i, p, j: (i, j, p)),
        out_shape=jax.ShapeDtypeStruct((b, s, A_WIDTH), _BF16),
        scratch_shapes=[pltpu.VMEM((s + LEFT_ROWS, LANES), _BF16),
                        pltpu.VMEM((ATT_AHEAD + 1, ATT_K, 2 * ATT_Q), _F32),
                        pltpu.VMEM((ATT_VARIANTS, ATT_K, 2 * ATT_Q), _F32)],
        compiler_params=_cparams(3),
        name="attn_a",
    )(qa, ka, vta, _attn_bias(rel_bias))


def _swap_halves(t):
    first = (_lane(t.shape) % HEAD_DIM) < (HEAD_DIM // 2)
    return jnp.where(first, pltpu.roll(t, LANES - HEAD_DIM // 2, 1), pltpu.roll(t, HEAD_DIM // 2, 1))


def _retention_kernel(q_ref, k_ref, kz_ref, v_ref, gate_ref, decay_ref, xi_ref, cd_ref, gn_ref, o_ref,
                      state_ref, *, rows):
    @pl.when(pl.program_id(2) == 0)
    def _():
        state_ref[...] = jnp.zeros_like(state_ref)

    c = RET_CHUNK
    low = _lane((c, LANES)) < HEAD_DIM
    eye = jnp.where(lax.broadcasted_iota(jnp.int32, (LANES, LANES), 0) == _lane((LANES, LANES)),
                    1.0, 0.0).astype(_BF16)
    srow = lax.broadcasted_iota(jnp.int32, (LANES, LANES), 0) < HEAD_DIM
    scol = _lane((LANES, LANES)) < HEAD_DIM
    same_head = srow == scol

    for j in range(rows // c):
        sl = slice(j * c, (j + 1) * c)
        qb = q_ref[sl, :]
        kb = k_ref[sl, :]
        vb = v_ref[sl, :]
        inner_out = []
        for h in range(2):
            qh = jnp.where(low if h == 0 else ~low, qb, jnp.zeros_like(qb))
            inner = _dot_nt(qh, kb) * decay_ref[h]
            inner_out.append(_dot(inner.astype(_BF16), vb))
        state = state_ref[...]
        cross = _dot(qb, state.astype(_BF16)) * xi_ref[...]
        o = jnp.where(low, inner_out[0], inner_out[1]) + cross
        kz = _dot_nt(eye, kz_ref[sl, :]).astype(_BF16)
        state_ref[...] = cd_ref[...] * state + jnp.where(same_head, _dot(kz, vb), 0.0)
        mu = jnp.where(low,
                       jnp.sum(jnp.where(low, o, 0.0), axis=-1, keepdims=True),
                       jnp.sum(jnp.where(low, 0.0, o), axis=-1, keepdims=True)) * (1.0 / HEAD_DIM)
        dlt = o - mu
        d2 = dlt * dlt
        var = jnp.where(low,
                        jnp.sum(jnp.where(low, d2, 0.0), axis=-1, keepdims=True),
                        jnp.sum(jnp.where(low, 0.0, d2), axis=-1, keepdims=True)) * (1.0 / HEAD_DIM)
        y = (dlt * lax.rsqrt(var + EPS)) * gn_ref[...]
        g = gate_ref[sl, :]
        o_ref[sl, :] = ((g * jax.nn.sigmoid(g)) * y).astype(o_ref.dtype)


def _retention_tables():
    c = RET_CHUNK
    log_g = jnp.log(1.0 - jnp.exp2(-5.0 - jnp.arange(B_HEADS, dtype=_F32)))
    idx = jnp.arange(c, dtype=_F32)
    diff = idx[:, None] - idx[None, :]
    decay = jnp.where(diff >= 0, jnp.exp(log_g[:, None, None] * jnp.maximum(diff, 0.0)), 0.0)
    zeta = jnp.exp(log_g[:, None] * (c - 1 - idx))
    xi = jnp.exp(log_g[:, None] * (idx + 1.0))
    cd = jnp.exp(log_g * c)

    def lanes(tab):
        return jnp.repeat(tab.reshape(B_HEADS // 2, 2, c), HEAD_DIM, axis=1).transpose(0, 2, 1)

    cdm = jnp.repeat(cd.reshape(B_HEADS // 2, 2), HEAD_DIM, axis=1)
    cdm = jnp.broadcast_to(cdm[:, :, None], (B_HEADS // 2, LANES, LANES))
    return decay, lanes(zeta), lanes(xi), cdm


def _retention(qkzv, gate, tables, ret_gn_g):
    b, s, _ = qkzv.shape
    rows = min(s, 4096)
    pairs = B_HEADS // 2
    decay, _, xi, cdm = tables
    gn = ret_gn_g.reshape(pairs, 1, LANES)

    def col(off):
        return pl.BlockSpec((None, rows, LANES), lambda i, p, j: (i, j, off * pairs + p))

    return pl.pallas_call(
        functools.partial(_retention_kernel, rows=rows),
        grid=(b, pairs, s // rows),
        in_specs=[col(0), col(1), col(2), col(3), col(0),
                  pl.BlockSpec((2, RET_CHUNK, RET_CHUNK), lambda i, p, j: (p, 0, 0)),
                  pl.BlockSpec((None, RET_CHUNK, LANES), lambda i, p, j: (p, 0, 0)),
                  pl.BlockSpec((None, LANES, LANES), lambda i, p, j: (p, 0, 0)),
                  pl.BlockSpec((None, 1, LANES), lambda i, p, j: (p, 0, 0))],
        out_specs=pl.BlockSpec((None, rows, LANES), lambda i, p, j: (i, j, p)),
        out_shape=jax.ShapeDtypeStruct((b, s, B_WIDTH), _BF16),
        scratch_shapes=[pltpu.VMEM((LANES, LANES), _F32)],
        compiler_params=_cparams(3),
        name="retention_b",
    )(qkzv, qkzv, qkzv, qkzv, gate, decay, xi, cdm, gn)


def _cross_kernel(q_ref, k_ref, vt_ref, o_ref, st_ref, *, rows):
    low = _lane((ATT_Q, LANES)) < HEAD_DIM
    lane_blocks = C_WIDTH // LANES
    tiles = [(j, lb) for j in range(rows // ATT_Q) for lb in range(lane_blocks)]
    ones = jnp.ones((ONES_ROWS, vt_ref.shape[1]), _BF16)

    def scores(i):
        j, lb = tiles[i]
        sl = slice(lb * LANES, (lb + 1) * LANES)
        q = q_ref[j * ATT_Q:(j + 1) * ATT_Q, sl]
        q2 = jnp.concatenate([jnp.where(low, q, jnp.zeros_like(q)), jnp.where(low, jnp.zeros_like(q), q)], axis=0)
        st_ref[i % (ATT_AHEAD + 1)] = _dot_nt(k_ref[:, sl], q2)

    def finish(i):
        j, lb = tiles[i]
        sl = slice(lb * LANES, (lb + 1) * LANES)
        st = st_ref[i % (ATT_AHEAD + 1)]
        p = jnp.exp2(st - jnp.max(st, axis=0, keepdims=True))
        ot = _dot(jnp.concatenate([vt_ref[sl, :], ones], axis=0), p.astype(_BF16))
        inv = 1.0 / ot[LANES:LANES + 1, :]
        out_t = jnp.concatenate([ot[0:HEAD_DIM, 0:ATT_Q] * inv[:, 0:ATT_Q],
                                 ot[HEAD_DIM:LANES, ATT_Q:] * inv[:, ATT_Q:]], axis=0)
        o_ref[j * ATT_Q:(j + 1) * ATT_Q, sl] = out_t.T.astype(o_ref.dtype)

    for i in range(min(ATT_AHEAD, len(tiles))):
        scores(i)
    for i in range(len(tiles)):
        if i + ATT_AHEAD < len(tiles):
            scores(i + ATT_AHEAD)
        finish(i)


def _cross_attention(qc, kc, vtc):
    b, s, _ = qc.shape
    m = kc.shape[1]
    rows = min(s, 2048)
    return pl.pallas_call(
        functools.partial(_cross_kernel, rows=rows),
        grid=(b, s // rows),
        in_specs=[pl.BlockSpec((None, rows, C_WIDTH), lambda i, j: (i, j, 0)),
                  pl.BlockSpec((None, m, C_WIDTH), lambda i, j: (i, 0, 0)),
                  pl.BlockSpec((None, C_WIDTH, m), lambda i, j: (i, 0, 0))],
        out_specs=pl.BlockSpec((None, rows, C_WIDTH), lambda i, j: (i, j, 0)),
        out_shape=jax.ShapeDtypeStruct((b, s, C_WIDTH), _BF16),
        scratch_shapes=[pltpu.VMEM((ATT_AHEAD + 1, m, 2 * ATT_Q), _F32)],
        compiler_params=_cparams(2),
        name="cross_c",
    )(qc, kc, vtc)


def _out_router_kernel(x_ref, a_ref, b_ref, c_ref, wo_ref, g_ref, wr_ref, br_ref,
                       h_ref, hn_ref, info_ref, rows_ref, cnt_ref, carry_ref):
    @pl.when(pl.program_id(0) == 0)
    def _():
        carry_ref[...] = jnp.zeros_like(carry_ref)

    tm = x_ref.shape[0]
    h = x_ref[...]
    h = h + _dot(a_ref[...], wo_ref[0:A_WIDTH, :])
    h = h + _dot(b_ref[...], wo_ref[A_WIDTH:A_WIDTH + B_WIDTH, :])
    h = h + _dot(c_ref[...], wo_ref[A_WIDTH + B_WIDTH:, :])
    h_ref[...] = h
    ms = jnp.mean(h * h, axis=-1, keepdims=True)
    hn = (h * lax.rsqrt(ms + EPS)) * g_ref[...]
    _pack_rows(hn_ref, hn)
    logits = _dot_nt(wr_ref[...], hn.astype(_BF16))[0:ROUTE_ROWS, :] + br_ref[:, 0:1]
    row = lax.broadcasted_iota(jnp.int32, (ROUTE_ROWS, tm), 0).astype(_F32)
    big = float(ROUTE_ROWS)

    def first_row(mask):
        return jnp.min(jnp.where(mask, row, big), axis=0, keepdims=True)

    gmask = row < N_GROUPS
    gl = jnp.where(gmask, logits, NEG_INF)
    ge = jnp.exp(gl - jnp.max(gl, axis=0, keepdims=True))
    gp = ge / jnp.sum(ge, axis=0, keepdims=True)
    p_group = jnp.max(gp, axis=0, keepdims=True)
    g_sel = first_row(gmask & (gp == p_group))
    lo = ROUTE_LANE0 + g_sel * EXPERTS_PER_GROUP
    emask = (row >= lo) & (row < lo + EXPERTS_PER_GROUP)
    el = jnp.where(emask, logits, NEG_INF)
    ee = jnp.exp(el - jnp.max(el, axis=0, keepdims=True))
    ep = ee / jnp.sum(ee, axis=0, keepdims=True)
    p1 = jnp.max(ep, axis=0, keepdims=True)
    i1 = first_row(emask & (ep == p1))
    ep2 = jnp.where(emask & (row != i1), ep, -1.0)
    p2 = jnp.max(ep2, axis=0, keepdims=True)
    i2 = first_row(ep2 == p2)
    den = p1 + p2
    w1 = p_group * (p1 / den)
    w2 = p_group * (p2 / den)
    hit1 = row == i1
    hit2 = row == i2
    onehot = jnp.where(hit1 | hit2, 1.0, 0.0)
    r_i = lax.broadcasted_iota(jnp.int32, (tm, tm), 0)
    c_i = lax.broadcasted_iota(jnp.int32, (tm, tm), 1)
    earlier = jnp.where(r_i < c_i, 1.0, 0.0).astype(_BF16)
    before = _dot(onehot.astype(_BF16), earlier) + carry_ref[:, 0:1]
    r1 = jnp.sum(jnp.where(hit1, before, 0.0), axis=0, keepdims=True)
    r2 = jnp.sum(jnp.where(hit2, before, 0.0), axis=0, keepdims=True)
    carry_ref[...] = carry_ref[...] + jnp.sum(onehot, axis=1, keepdims=True)
    cnt_ref[...] = carry_ref[...]
    out_row = lax.broadcasted_iota(jnp.int32, (LANES, tm), 0)
    info = jnp.where(out_row == 0, w1, 0.0)
    info = jnp.where(out_row == 1, w2, info)
    info = jnp.where(out_row == 2, i1 - ROUTE_LANE0, info)
    info = jnp.where(out_row == 3, i2 - ROUTE_LANE0, info)
    info = jnp.where(out_row == 4, r1, info)
    info = jnp.where(out_row == 5, r2, info)
    rows_ref[...] = info[0:SUBLANES, :]
    info_ref[...] = info.T


def _out_router(x2, oa, ob, oc, w_out, ffn_g, w_rg, b_rg, w_re, b_re):
    t, d = x2.shape
    tm = min(t, 512)
    pad = LANES - N_GROUPS - N_EXPERTS
    wr = jnp.concatenate([w_rg, w_re, jnp.zeros((d, pad), _F32)], axis=1).T.astype(_BF16)
    br = jnp.concatenate([b_rg, b_re, jnp.zeros((ROUTE_ROWS - N_GROUPS - N_EXPERTS,), _F32)])
    br = jnp.broadcast_to(br[:, None], (ROUTE_ROWS, LANES))

    def rows(w):
        return pl.BlockSpec((tm, w), lambda i: (i, 0))

    def whole(r, c):
        return pl.BlockSpec((r, c), lambda i: (0, 0))

    return pl.pallas_call(
        _out_router_kernel,
        grid=(t // tm,),
        in_specs=[rows(d), rows(A_WIDTH), rows(B_WIDTH), rows(C_WIDTH), whole(d, d), whole(1, d),
                  whole(LANES, d), whole(ROUTE_ROWS, LANES)],
        out_specs=[rows(d), pl.BlockSpec((tm * PACK_ROWS, LANES), lambda i: (i, 0)), rows(LANES),
                   pl.BlockSpec((SUBLANES, tm), lambda i: (0, i)), whole(ROUTE_ROWS, LANES)],
        out_shape=[jax.ShapeDtypeStruct((t, d), _F32), jax.ShapeDtypeStruct((t * PACK_ROWS, LANES), jnp.uint32),
                   jax.ShapeDtypeStruct((t, LANES), _F32), jax.ShapeDtypeStruct((SUBLANES, t), _F32),
                   jax.ShapeDtypeStruct((ROUTE_ROWS, LANES), _F32)],
        scratch_shapes=[pltpu.VMEM((ROUTE_ROWS, LANES), _F32)],
        compiler_params=_cparams(1),
        name="out_router",
    )(x2, oa, ob, oc, w_out.astype(_BF16), ffn_g.reshape(1, d), wr, br)


DISPATCH_TOKENS = 2048
COMBINE_TOKENS = 1024


ROW_UNROLL = 8


def _tile_rows(row, count=1, per=SUBLANES):
    start = row * per
    if not isinstance(start, int):
        start = pl.multiple_of(start, per)
    return pl.ds(start, count * per)


def _row_copy(src, s_row, dst, d_row, sem, per=SUBLANES):
    return pltpu.make_async_copy(src.at[_tile_rows(s_row, 1, per)], dst.at[_tile_rows(d_row, 1, per)], sem)


def _dispatch_kernel(pad_start_ref, pad_len_ref, used_ref, dest_ref, hn_ref, xs_ref, zero_ref, sem,
                     pad_sem):
    per = PACK_ROWS
    n = hn_ref.shape[0] // per

    @pl.when(pl.program_id(0) == 0)
    def _():
        zero_ref[...] = jnp.zeros_like(zero_ref)
        n_blocks = xs_ref.shape[0] // (ROW_BLOCK * per)

        def block_copy(blk):
            return pltpu.make_async_copy(zero_ref, xs_ref.at[_tile_rows(blk * ROW_BLOCK, ROW_BLOCK, per)],
                                         pad_sem)

        def put_block(blk, carry):
            block_copy(blk).start()
            return carry

        def done_block(blk, carry):
            block_copy(blk).wait()
            return carry

        lax.fori_loop(used_ref[0], n_blocks, put_block, 0)
        lax.fori_loop(used_ref[0], n_blocks, done_block, 0)
        bits = [1 << k for k in reversed(range(ROW_BLOCK.bit_length() - 1))]

        def tail(e, wait):
            row = pad_start_ref[e]
            for bit in bits:
                on = (pad_len_ref[e] & bit) != 0
                copy = pltpu.make_async_copy(zero_ref.at[_tile_rows(0, bit, per)],
                                             xs_ref.at[_tile_rows(row, bit, per)], pad_sem)

                @pl.when(on)
                def _():
                    copy.wait() if wait else copy.start()

                row = row + jnp.where(on, bit, 0)

        def put_tail(e, carry):
            tail(e, False)
            return carry

        def done_tail(e, carry):
            tail(e, True)
            return carry

        lax.fori_loop(0, N_EXPERTS, put_tail, 0)
        lax.fori_loop(0, N_EXPERTS, done_tail, 0)

    def issue(i, carry):
        for u in range(ROW_UNROLL):
            t = i * ROW_UNROLL + u
            _row_copy(hn_ref, t, xs_ref, dest_ref[2 * t], sem, per).start(priority=0)
            _row_copy(hn_ref, t, xs_ref, dest_ref[2 * t + 1], sem, per).start(priority=1)
        return carry

    lax.fori_loop(0, n // ROW_UNROLL, issue, 0)
    for _ in range(2):
        pltpu.make_async_copy(hn_ref, xs_ref.at[_tile_rows(0, n, per)], sem).wait()


def _dispatch(hn, dest, pad_start, pad_len, n_used, n_rows):
    t = hn.shape[0] // PACK_ROWS
    n = min(t, DISPATCH_TOKENS)
    return pl.pallas_call(
        _dispatch_kernel,
        grid_spec=pltpu.PrefetchScalarGridSpec(
            num_scalar_prefetch=3,
            grid=(t // n,),
            in_specs=[pl.BlockSpec((2 * n,), lambda i, *_: (i,), memory_space=pltpu.SMEM),
                      pl.BlockSpec((n * PACK_ROWS, LANES), lambda i, *_: (i, 0))],
            out_specs=pl.BlockSpec(memory_space=pl.ANY),
            scratch_shapes=[pltpu.VMEM((ROW_BLOCK * PACK_ROWS, LANES), hn.dtype), pltpu.SemaphoreType.DMA,
                            pltpu.SemaphoreType.DMA]),
        out_shape=jax.ShapeDtypeStruct((n_rows * PACK_ROWS, LANES), hn.dtype),
        compiler_params=_cparams(1),
        name="moe_dispatch",
    )(pad_start, pad_len, n_used, dest, hn)


def _expert_kernel(be_ref, run_ref, next_ref, used_ref, x_ref, wg_hbm, wu_hbm, wd_hbm, y_ref,
                   wg_f32, wu_f32, wd_f32, wg_bf, wu_bf, wd_bf, sem):
    i = pl.program_id(0)
    live = i < used_ref[0]
    new_expert = (i == 0) | (be_ref[i] != be_ref[jnp.maximum(i - 1, 0)])
    slot = run_ref[i] % 2

    def fetch(expert, to_slot):
        return [pltpu.make_async_copy(src.at[expert], dst.at[to_slot], sem.at[to_slot, k])
                for k, (src, dst) in enumerate(((wg_hbm, wg_f32), (wu_hbm, wu_f32), (wd_hbm, wd_f32)))]

    @pl.when(live & (i == 0))
    def _():
        for copy in fetch(be_ref[0], 0):
            copy.start()

    @pl.when(live & new_expert)
    def _():
        for copy in fetch(be_ref[i], slot):
            copy.wait()

        @pl.when(next_ref[i] >= 0)
        def _():
            for copy in fetch(next_ref[i], 1 - slot):
                copy.start()

        wg_bf[...] = wg_f32[slot].astype(_BF16)
        wu_bf[...] = wu_f32[slot].astype(_BF16)
        wd_bf[...] = wd_f32[slot].astype(_BF16)

    @pl.when(live)
    def _():
        sub = ROW_BLOCK // EXPERT_SPLIT
        gate_up = {}

        def first(k):
            x = _unpack_rows(x_ref, sub, k * sub)
            gate_up[k] = (_dot(x, wg_bf[...]), _dot(x, wu_bf[...]))

        def second(k):
            gate, up = gate_up.pop(k)
            act = (gate * jax.nn.sigmoid(gate)) * up
            _rows_to_tiles(y_ref, _dot(act.astype(_BF16), wd_bf[...]), k * sub)

        for k in range(min(EXPERT_AHEAD, EXPERT_SPLIT)):
            first(k)
        for k in range(EXPERT_SPLIT):
            if k + EXPERT_AHEAD < EXPERT_SPLIT:
                first(k + EXPERT_AHEAD)
            second(k)

    @pl.when(i >= used_ref[0])
    def _():
        y_ref[...] = jnp.zeros_like(y_ref)


def _experts(xs, blocks, w_gate, w_up, w_down):
    n_rows, d = xs.shape[0] // PACK_ROWS, D_MODEL
    n_blocks = n_rows // ROW_BLOCK
    tile_block = (ROW_BLOCK * SUBLANES, LANES)
    hbm = pl.BlockSpec(memory_space=pl.ANY)

    return pl.pallas_call(
        _expert_kernel,
        grid_spec=pltpu.PrefetchScalarGridSpec(
            num_scalar_prefetch=4,
            grid=(n_blocks,),
            in_specs=[pl.BlockSpec((ROW_BLOCK * PACK_ROWS, LANES),
                                   lambda i, be, run, nxt, used: (jnp.minimum(i, used[0] - 1), 0)),
                      hbm, hbm, hbm],
            out_specs=pl.BlockSpec(tile_block, lambda i, *_: (i, 0)),
            scratch_shapes=[pltpu.VMEM((2, d, D_EXPERT), _F32), pltpu.VMEM((2, d, D_EXPERT), _F32),
                            pltpu.VMEM((2, D_EXPERT, d), _F32),
                            pltpu.VMEM((d, D_EXPERT), _BF16), pltpu.VMEM((d, D_EXPERT), _BF16),
                            pltpu.VMEM((D_EXPERT, d), _BF16), pltpu.SemaphoreType.DMA((2, 3))]),
        out_shape=jax.ShapeDtypeStruct((n_rows * SUBLANES, LANES), _F32),
        compiler_params=_cparams(1),
        name="moe_experts",
    )(*blocks, xs, w_gate, w_up, w_down)


def _combine_kernel(dest_ref, next_ref, h_ref, info_ref, ys_ref, o_ref, buf_ref, sem):
    n = h_ref.shape[0]
    step = pl.program_id(0)
    slot = step % 2

    def gather(idx_ref, to_slot):
        def issue(i, carry):
            for u in range(ROW_UNROLL):
                t = i * ROW_UNROLL + u
                _row_copy(ys_ref, idx_ref[2 * t], buf_ref.at[to_slot, 0], t,
                          sem.at[to_slot]).start(priority=0)
                _row_copy(ys_ref, idx_ref[2 * t + 1], buf_ref.at[to_slot, 1], t,
                          sem.at[to_slot]).start(priority=1)
            return carry

        lax.fori_loop(0, n // ROW_UNROLL, issue, 0)

    @pl.when(step == 0)
    def _():
        gather(dest_ref, 0)

    @pl.when(step + 1 < pl.num_programs(0))
    def _():
        gather(next_ref, 1 - slot)

    for k in range(2):
        pltpu.make_async_copy(ys_ref.at[_tile_rows(0, n)], buf_ref.at[slot, k], sem.at[slot]).wait()
    info = info_ref[...]
    w0 = info[:, 0:1]
    w1 = info[:, 1:2]
    for s in range(SUBLANES):
        sl = slice(s * LANES, (s + 1) * LANES)
        moe = w0 * _tile_block(buf_ref.at[slot, 0], s, n) + w1 * _tile_block(buf_ref.at[slot, 1], s, n)
        o_ref[:, sl] = h_ref[:, sl] + moe


def _combine(h, info, ys, dest):
    t, d = h.shape
    n = min(t, COMBINE_TOKENS)
    steps = t // n
    return pl.pallas_call(
        _combine_kernel,
        grid=(steps,),
        in_specs=[pl.BlockSpec((2 * n,), lambda i: (i,), memory_space=pltpu.SMEM),
                  pl.BlockSpec((2 * n,), lambda i: (jnp.minimum(i + 1, steps - 1),),
                               memory_space=pltpu.SMEM),
                  pl.BlockSpec((n, d), lambda i: (i, 0)),
                  pl.BlockSpec((n, LANES), lambda i: (i, 0)),
                  pl.BlockSpec(memory_space=pl.ANY)],
        out_specs=pl.BlockSpec((n, d), lambda i: (i, 0)),
        out_shape=jax.ShapeDtypeStruct((t, d), _F32),
        scratch_shapes=[pltpu.VMEM((2, 2, n * SUBLANES, LANES), _F32), pltpu.SemaphoreType.DMA((2,))],
        compiler_params=_cparams(1),
        name="moe_combine",
    )(dest, dest, h, info, ys)


def _moe_layout(route_rows, counts, t):
    counts = counts[ROUTE_LANE0:ROUTE_LANE0 + N_EXPERTS, 0].astype(jnp.int32)
    padded = (counts + ROW_BLOCK - 1) // ROW_BLOCK * ROW_BLOCK
    pends = jnp.cumsum(padded)
    pstarts = pends - padded
    eid = route_rows[2:4].astype(jnp.int32)
    rank = route_rows[4:6].astype(jnp.int32)
    experts = jnp.arange(N_EXPERTS, dtype=jnp.int32)
    start_of = jnp.sum(jnp.where(eid[:, :, None] == experts, pstarts, 0), axis=-1)
    dest = (start_of + rank).T.reshape(-1)
    n_blocks = -(-2 * t // ROW_BLOCK) + N_EXPERTS
    first_row = jnp.arange(n_blocks, dtype=jnp.int32) * ROW_BLOCK
    block_e = jnp.minimum(jnp.sum((pends[None, :] <= first_row[:, None]).astype(jnp.int32), axis=1),
                          N_EXPERTS - 1)
    n_used = (pends[-1:] // ROW_BLOCK).astype(jnp.int32)
    changed = jnp.concatenate([jnp.zeros((1,), jnp.int32), (block_e[1:] != block_e[:-1]).astype(jnp.int32)])
    block_run = jnp.cumsum(changed)
    later = (counts[None, :] > 0) & (experts[None, :] > experts[:, None])
    next_expert = jnp.min(jnp.where(later, experts[None, :], N_EXPERTS), axis=1)
    next_expert = jnp.where(next_expert < N_EXPERTS, next_expert, -1)
    block_next = jnp.sum(jnp.where(block_e[:, None] == experts[None, :], next_expert[None, :], 0), axis=1)
    blocks = (block_e, block_run.astype(jnp.int32), block_next.astype(jnp.int32), n_used)
    return dest, blocks, pstarts + counts, padded - counts, n_blocks * ROW_BLOCK


def kernel(x, mem, positions, mix_norm_g, w_in, qn_a, kn_a, rel_bias, ret_gn_g, mem_norm_g, w_mem_kv,
           qn_c, kn_c, w_out, ffn_norm_g, w_router_group, b_router_group, w_router_expert,
           b_router_expert, w_gate, w_up, w_down):
    b, s, d = x.shape
    t = b * s
    x2 = x.reshape(t, d)
    kc, vc = _mem_kv(mem, mem_norm_g, w_mem_kv, kn_c)
    tables = _retention_tables()
    qa, ka, vta, qkzv, gate, qc = _in_proj(x, positions, tables[1], mix_norm_g, w_in, qn_a, kn_a, qn_c)
    out_a = _attention(qa, ka, vta, rel_bias)
    out_b = _retention(qkzv, gate, tables, ret_gn_g)
    out_c = _cross_attention(qc, kc, vc)
    h, hn, info, route_rows, counts = _out_router(
        x2, out_a.reshape(t, A_WIDTH), out_b.reshape(t, B_WIDTH), out_c.reshape(t, C_WIDTH),
        w_out, ffn_norm_g, w_router_group, b_router_group, w_router_expert, b_router_expert)
    dest, blocks, pad_start, pad_len, n_rows = _moe_layout(route_rows, counts, t)
    xs = _dispatch(hn, dest, pad_start, pad_len, blocks[-1], n_rows)
    ys = _experts(xs, blocks, w_gate, w_up, w_down)
    return _combine(h, info, ys, dest).reshape(b, s, d)
```

```python
import functools

import jax
import jax.numpy as jnp
from jax import lax
from jax.experimental import pallas as pl
from jax.experimental.pallas import tpu as pltpu

D_MODEL = 1024
CHUNK = 64
HEAD_DIM = 64
A_HEADS = 8
B_HEADS = 4
C_HEADS = 4
A_WIDTH = A_HEADS * HEAD_DIM
B_WIDTH = B_HEADS * HEAD_DIM
C_WIDTH = C_HEADS * HEAD_DIM
IN_COLS = 3 * A_WIDTH + 4 * B_WIDTH + C_WIDTH
LEFT_CHUNKS = 8
BAND_CHUNKS = LEFT_CHUNKS + 1
MAX_REL_DIST = 128
ROPE_BASE = 10000.0
N_GROUPS = 4
EXPERTS_PER_GROUP = 8
N_EXPERTS = N_GROUPS * EXPERTS_PER_GROUP
D_EXPERT = D_MODEL // 2
EPS = 1e-6
NEG_INF = -1e30
LOG2E = 1.4426950408889634

LANES = 128
SUBLANES = 8
assert D_MODEL == SUBLANES * LANES
PACK_ROWS = SUBLANES // 2
LEFT_ROWS = LEFT_CHUNKS * CHUNK
ATT_Q = 2 * CHUNK
ATT_K = ATT_Q + LEFT_ROWS
ATT_VARIANTS = LEFT_ROWS // ATT_Q + 1
ONES_ROWS = 16
ATT_AHEAD = 3
RET_CHUNK = 256
ROW_BLOCK = 512
EXPERT_SPLIT = 2
EXPERT_AHEAD = 2
ROUTE_LANE0 = N_GROUPS
ROUTE_ROWS = 64
VMEM_LIMIT = 48 * 1024 * 1024

_F32 = jnp.float32
_BF16 = jnp.bfloat16


def _cparams(n_axes):
    return pltpu.CompilerParams(dimension_semantics=("arbitrary",) * n_axes,
                                vmem_limit_bytes=VMEM_LIMIT)


def _dot(a, b):
    return jnp.dot(a, b, preferred_element_type=_F32)


def _dot_nt(a, b):
    return lax.dot_general(a, b, (((1,), (1,)), ((), ())), preferred_element_type=_F32)


def _lane(shape):
    return lax.broadcasted_iota(jnp.int32, shape, len(shape) - 1)


def _pair_rms(t, gain):
    low = _lane(t.shape) < HEAD_DIM
    t2 = t * t
    ms0 = jnp.sum(jnp.where(low, t2, 0.0), axis=-1, keepdims=True) * (1.0 / HEAD_DIM)
    ms1 = jnp.sum(jnp.where(low, 0.0, t2), axis=-1, keepdims=True) * (1.0 / HEAD_DIM)
    r = jnp.where(low, lax.rsqrt(ms0 + EPS), lax.rsqrt(ms1 + EPS))
    return (t * r) * gain


def _rows_to_tiles(ref, val, row0=0):
    n = val.shape[0]
    for s in range(SUBLANES):
        ref[pl.ds(row0 * SUBLANES + s, n, stride=SUBLANES), :] = val[:, s * LANES:(s + 1) * LANES]


def _tile_block(ref, s, n, row0=0):
    return ref[pl.ds(row0 * SUBLANES + s, n, stride=SUBLANES), :]


def _pack_rows(ref, val, row0=0):
    n = val.shape[0]
    for s in range(PACK_ROWS):
        lo = val[:, (2 * s) * LANES:(2 * s + 1) * LANES].astype(_BF16).astype(_F32)
        hi = val[:, (2 * s + 1) * LANES:(2 * s + 2) * LANES].astype(_BF16).astype(_F32)
        word = (lax.bitcast_convert_type(lo, jnp.uint32) >> 16) | (
            lax.bitcast_convert_type(hi, jnp.uint32) & jnp.uint32(0xFFFF0000))
        ref[pl.ds(row0 * PACK_ROWS + s, n, stride=PACK_ROWS), :] = word


def _unpack_rows(ref, n, row0=0):
    parts = []
    for s in range(PACK_ROWS):
        word = ref[pl.ds(row0 * PACK_ROWS + s, n, stride=PACK_ROWS), :]
        parts.append(lax.bitcast_convert_type(word << 16, _F32))
        parts.append(lax.bitcast_convert_type(word & jnp.uint32(0xFFFF0000), _F32))
    return jnp.concatenate(parts, axis=-1).astype(_BF16)


ROPE_HALF = HEAD_DIM // 2
ROPE_PACK = LANES // ROPE_HALF


def _rope_tables(pos_ref, inv_ref, cos_ref, sin_ref):
    ang = pos_ref[...].astype(_F32) * inv_ref[...]
    rows = ang.shape[0]
    lane = _lane(ang.shape)
    sign = jnp.where((lane % HEAD_DIM) < ROPE_HALF, -1.0, 1.0)
    for out_ref, val in ((cos_ref, jnp.cos(ang)), (sin_ref, jnp.sin(ang))):
        for j in range(ROPE_PACK):
            seg = jnp.where(lane // ROPE_HALF == j, val, 0.0)
            full = seg
            for k in range(1, ROPE_PACK):
                full = full + pltpu.roll(seg, k * ROPE_HALF, 1)
            if out_ref is sin_ref:
                full = full * sign
            out_ref[pl.ds(j, rows, stride=ROPE_PACK), :] = full


def _rope_inputs(positions):
    b, s = positions.shape
    inv = ROPE_BASE ** (-jnp.arange(ROPE_HALF, dtype=_F32) / ROPE_HALF)
    inv128 = jnp.tile(inv, ROPE_PACK).reshape(1, LANES)
    pos = jnp.repeat(positions.reshape(b, s // ROPE_PACK, ROPE_PACK), ROPE_HALF, axis=2)
    return pos, inv128


def _mem_kv_kernel(mem_ref, g_ref, w_ref, kn_ref, k_ref, v_ref):
    m = mem_ref[...]
    ms = jnp.mean(m * m, axis=-1, keepdims=True)
    mn = (m * lax.rsqrt(ms + EPS)) * g_ref[...]
    kv = _dot(mn.astype(_BF16), w_ref[...])
    for j in range(C_WIDTH // LANES):
        sl = slice(j * LANES, (j + 1) * LANES)
        k_ref[:, sl] = _pair_rms(kv[:, sl], kn_ref[...]).astype(_BF16)
    v_ref[...] = kv[:, C_WIDTH:].T.astype(_BF16)


def _mem_kv(mem, mem_norm_g, w_mem_kv, kn_c):
    b, m, d = mem.shape
    kn = jnp.tile(kn_c, 2).reshape(1, LANES)
    return pl.pallas_call(
        _mem_kv_kernel,
        grid=(b,),
        in_specs=[pl.BlockSpec((None, m, d), lambda i: (i, 0, 0)),
                  pl.BlockSpec((1, d), lambda i: (0, 0)),
                  pl.BlockSpec((d, 2 * C_WIDTH), lambda i: (0, 0)),
                  pl.BlockSpec((1, LANES), lambda i: (0, 0))],
        out_specs=[pl.BlockSpec((None, m, C_WIDTH), lambda i: (i, 0, 0)),
                   pl.BlockSpec((None, C_WIDTH, m), lambda i: (i, 0, 0))],
        out_shape=[jax.ShapeDtypeStruct((b, m, C_WIDTH), _BF16),
                   jax.ShapeDtypeStruct((b, C_WIDTH, m), _BF16)],
        compiler_params=_cparams(1),
        name="mem_kv",
    )(mem, mem_norm_g.reshape(1, d), w_mem_kv.astype(_BF16), kn)


def _in_proj_kernel(x_ref, pos_ref, inv_ref, g_ref, wq_ref, wk_ref, wvt_ref, wr_ref, wc_ref, qn_ref, kn_ref,
                    cn_ref, zeta_ref, qa_ref, ka_ref, vt_ref, ret_ref, gate_ref, qc_ref, xn_ref, acc_ref,
                    accb_ref, cos_ref, sin_ref):
    x = x_ref[...]
    ms = jnp.mean(x * x, axis=-1, keepdims=True)
    xn_ref[...] = ((x * lax.rsqrt(ms + EPS)) * g_ref[...]).astype(_BF16)

    def normed(slot, out_ref, gain_ref):
        for blk in range(out_ref.shape[1] // LANES):
            sl = slice(blk * LANES, (blk + 1) * LANES)
            out_ref[:, sl] = _pair_rms(acc_ref[slot, :, sl], gain_ref[...]).astype(_BF16)

    acc_ref[0] = _dot(xn_ref[...], wq_ref[...])
    acc_ref[1] = _dot(xn_ref[...], wk_ref[...])
    _rope_tables(pos_ref, inv_ref, cos_ref, sin_ref)
    normed(0, qa_ref, qn_ref)
    acc_ref[0] = _dot_nt(wvt_ref[...], xn_ref[...])
    normed(1, ka_ref, kn_ref)
    accb_ref[...] = _dot(xn_ref[...], wr_ref[...])
    for blk in range(vt_ref.shape[0]):
        vt_ref[blk] = acc_ref[0, :, blk * LANES:(blk + 1) * LANES].astype(_BF16)
    acc_ref[1, :, 0:C_WIDTH] = _dot(xn_ref[...], wc_ref[...])
    cos, sin = cos_ref[...], sin_ref[...]
    chunks = x_ref.shape[0] // RET_CHUNK
    for p in range(B_WIDTH // LANES):
        sl = slice(p * LANES, (p + 1) * LANES)
        q = accb_ref[:, sl]
        k = accb_ref[:, B_WIDTH + p * LANES:B_WIDTH + (p + 1) * LANES]
        kr = (k * cos + _swap_halves(k) * sin) * (HEAD_DIM ** -0.5)
        ret_ref[:, sl] = (q * cos + _swap_halves(q) * sin).astype(_BF16)
        ret_ref[:, B_WIDTH + p * LANES:B_WIDTH + (p + 1) * LANES] = kr.astype(_BF16)
        ret_ref[:, 2 * B_WIDTH + p * LANES:2 * B_WIDTH + (p + 1) * LANES] = (
            kr * jnp.concatenate([zeta_ref[p]] * chunks, axis=0)).astype(_BF16)
        ret_ref[:, 3 * B_WIDTH + p * LANES:3 * B_WIDTH + (p + 1) * LANES] = accb_ref[
            :, 2 * B_WIDTH + p * LANES:2 * B_WIDTH + (p + 1) * LANES].astype(_BF16)
        gate_ref[:, sl] = accb_ref[:, 3 * B_WIDTH + p * LANES:3 * B_WIDTH + (p + 1) * LANES]
    normed(1, qc_ref, cn_ref)


def _in_proj(x3, positions, zeta, g, w_in, qn_a, kn_a, qn_c):
    b, s, d = x3.shape
    tm = min(s, 512)
    assert tm == A_WIDTH
    assert tm % RET_CHUNK == 0
    pos, inv128 = _rope_inputs(positions)
    w = w_in.astype(_BF16)
    cuts = [0, A_WIDTH, 2 * A_WIDTH, 3 * A_WIDTH, 3 * A_WIDTH + 4 * B_WIDTH, IN_COLS]
    wq, wk, wv, wr, wc = (w[:, lo:hi] for lo, hi in zip(cuts[:-1], cuts[1:]))
    scale = HEAD_DIM ** -0.5 * LOG2E
    gains = [(jnp.tile(gn, 2) * sc).reshape(1, LANES) for gn, sc in ((qn_a, scale), (kn_a, 1.0), (qn_c, scale))]

    def whole(arr):
        return pl.BlockSpec(arr.shape, lambda i, j: (0,) * arr.ndim)

    def rows(width):
        return pl.BlockSpec((None, tm, width), lambda i, j: (i, j, 0))

    consts = [inv128, g.reshape(1, d), wq, wk, wv.T, wr, wc] + gains + [zeta]
    return pl.pallas_call(
        _in_proj_kernel,
        grid=(b, s // tm),
        in_specs=[rows(d), pl.BlockSpec((None, tm // ROPE_PACK, LANES), lambda i, j: (i, j, 0))]
        + [whole(c) for c in consts],
        out_specs=[rows(A_WIDTH), rows(A_WIDTH),
                   pl.BlockSpec((None, tm // LANES, A_WIDTH, LANES), lambda i, j: (i, j, 0, 0)),
                   rows(4 * B_WIDTH), rows(B_WIDTH), rows(C_WIDTH)],
        out_shape=[jax.ShapeDtypeStruct((b, s, A_WIDTH), _BF16), jax.ShapeDtypeStruct((b, s, A_WIDTH), _BF16),
                   jax.ShapeDtypeStruct((b, s // LANES, A_WIDTH, LANES), _BF16),
                   jax.ShapeDtypeStruct((b, s, 4 * B_WIDTH), _BF16), jax.ShapeDtypeStruct((b, s, B_WIDTH), _F32),
                   jax.ShapeDtypeStruct((b, s, C_WIDTH), _BF16)],
        scratch_shapes=[pltpu.VMEM((tm, d), _BF16), pltpu.VMEM((2, tm, A_WIDTH), _F32),
                        pltpu.VMEM((tm, 4 * B_WIDTH), _F32), pltpu.VMEM((tm, LANES), _F32),
                        pltpu.VMEM((tm, LANES), _F32)],
        compiler_params=_cparams(2),
        name="in_proj",
    )(x3, pos, *consts)


def _attn_kernel(q_ref, k_ref, vt_ref, bias_ref, o_ref, kp_ref, st_ref, var_ref, *, q_rows):
    qs = pl.program_id(2)
    s = k_ref.shape[0]
    fill_rows = min(s, 1024)
    left_blocks = LEFT_ROWS // LANES

    @pl.when(qs == 0)
    def _():
        kp_ref[0:LEFT_ROWS, :] = jnp.zeros((LEFT_ROWS, LANES), _BF16)

        def fill(i, carry):
            r = pl.multiple_of(i * fill_rows, fill_rows)
            kp_ref[pl.ds(LEFT_ROWS + r, fill_rows), :] = k_ref[pl.ds(r, fill_rows), :]
            return carry

        lax.fori_loop(0, s // fill_rows, fill, 0)
        key = lax.broadcasted_iota(jnp.int32, (ATT_K, 2 * ATT_Q), 0)
        for v in range(ATT_VARIANTS):
            var_ref[v] = jnp.where(key >= LEFT_ROWS - ATT_Q * v, bias_ref[...], NEG_INF)

    low = _lane((ATT_Q, LANES)) < HEAD_DIM
    ones = jnp.ones((ONES_ROWS, ATT_K), _BF16)
    tiles_per_step = q_rows // ATT_Q

    def scores(j):
        cp = qs * tiles_per_step + j
        q = q_ref[j * ATT_Q:(j + 1) * ATT_Q, :]
        q2 = jnp.concatenate([jnp.where(low, q, jnp.zeros_like(q)), jnp.where(low, jnp.zeros_like(q), q)], axis=0)
        kb = kp_ref[pl.ds(pl.multiple_of(cp * ATT_Q, ATT_Q), ATT_K), :]
        st_ref[j % (ATT_AHEAD + 1)] = _dot_nt(kb, q2) + var_ref[jnp.minimum(cp, ATT_VARIANTS - 1)]

    def finish(j):
        cp = qs * tiles_per_step + j
        st = st_ref[j % (ATT_AHEAD + 1)]
        m = jnp.max(st, axis=0, keepdims=True)
        p = jnp.exp2(st - m)
        vt = jnp.concatenate([vt_ref[jnp.maximum(cp + kb_i - left_blocks, 0)] for kb_i in range(ATT_K // LANES)],
                             axis=1)
        ot = _dot(jnp.concatenate([vt, ones], axis=0), p.astype(_BF16))
        inv = 1.0 / ot[LANES:LANES + 1, :]
        out_t = jnp.concatenate([ot[0:HEAD_DIM, 0:ATT_Q] * inv[:, 0:ATT_Q],
                                 ot[HEAD_DIM:LANES, ATT_Q:] * inv[:, ATT_Q:]], axis=0)
        o_ref[j * ATT_Q:(j + 1) * ATT_Q, :] = out_t.T.astype(o_ref.dtype)

    for j in range(min(ATT_AHEAD, tiles_per_step)):
        scores(j)
    for j in range(tiles_per_step):
        if j + ATT_AHEAD < tiles_per_step:
            scores(j + ATT_AHEAD)
        finish(j)


def _toeplitz_bias(rel_bias, q_len, k_len):
    h, table = rel_bias.shape
    n_diag = q_len + k_len - 1
    flat_lo = k_len - 1 - LEFT_ROWS - (CHUNK - 1)
    flat_hi = n_diag - flat_lo - table
    rev = jnp.concatenate([jnp.broadcast_to(rel_bias[:, -1:], (h, flat_hi)), rel_bias[:, ::-1],
                           jnp.broadcast_to(rel_bias[:, :1], (h, flat_lo))], axis=1).astype(_F32)
    flat = jnp.tile(rev, (1, q_len + 1))
    pitch = n_diag - 1
    skew = flat[:, q_len - 1:q_len - 1 + q_len * pitch].reshape(h, q_len, pitch)
    return skew[:, :, :k_len]


def _attn_bias(rel_bias):
    h = rel_bias.shape[0]
    bias = _toeplitz_bias(rel_bias, ATT_Q, ATT_K)
    q = lax.broadcasted_iota(jnp.int32, (ATT_Q, ATT_K), 0)
    k = lax.broadcasted_iota(jnp.int32, (ATT_Q, ATT_K), 1)
    off = k // CHUNK - q // CHUNK
    in_band = (off >= 0) & (off < BAND_CHUNKS)
    full = jnp.where(in_band[None], bias * LOG2E, NEG_INF)
    full = full.reshape(h // 2, 2, ATT_Q, ATT_K)
    return full.transpose(0, 3, 1, 2).reshape(h // 2, ATT_K, 2 * ATT_Q)


def _attention(qa, ka, vta, rel_bias):
    b, s, _ = qa.shape
    q_rows = min(s, 2048)
    pairs = A_HEADS // 2
    return pl.pallas_call(
        functools.partial(_attn_kernel, q_rows=q_rows),
        grid=(b, pairs, s // q_rows),
        in_specs=[pl.BlockSpec((None, q_rows, LANES), lambda i, p, j: (i, j, p)),
                  pl.BlockSpec((None, s, LANES), lambda i, p, j: (i, 0, p)),
                  pl.BlockSpec((None, s // LANES, LANES, LANES), lambda i, p, j: (i, 0, p, 0)),
                  pl.BlockSpec((None, ATT_K, 2 * ATT_Q), lambda i, p, j: (p, 0, 0))],
        out_specs=pl.BlockSpec((None, q_rows, LANES), lambda i, p, j: (i, j, p)),
        out_shape=jax.ShapeDtypeStruct((b, s, A_WIDTH), _BF16),
        scratch_shapes=[pltpu.VMEM((s + LEFT_ROWS, LANES), _BF16),
                        pltpu.VMEM((ATT_AHEAD + 1, ATT_K, 2 * ATT_Q), _F32),
                        pltpu.VMEM((ATT_VARIANTS, ATT_K, 2 * ATT_Q), _F32)],
        compiler_params=_cparams(3),
        name="attn_a",
    )(qa, ka, vta, _attn_bias(rel_bias))


def _swap_halves(t):
    first = (_lane(t.shape) % HEAD_DIM) < (HEAD_DIM // 2)
    return jnp.where(first, pltpu.roll(t, LANES - HEAD_DIM // 2, 1), pltpu.roll(t, HEAD_DIM // 2, 1))


def _retention_kernel(q_ref, k_ref, kz_ref, v_ref, gate_ref, decay_ref, xi_ref, cd_ref, gn_ref, o_ref,
                      state_ref, *, rows):
    @pl.when(pl.program_id(2) == 0)
    def _():
        state_ref[...] = jnp.zeros_like(state_ref)

    c = RET_CHUNK
    low = _lane((c, LANES)) < HEAD_DIM
    eye = jnp.where(lax.broadcasted_iota(jnp.int32, (LANES, LANES), 0) == _lane((LANES, LANES)),
                    1.0, 0.0).astype(_BF16)
    srow = lax.broadcasted_iota(jnp.int32, (LANES, LANES), 0) < HEAD_DIM
    scol = _lane((LANES, LANES)) < HEAD_DIM
    same_head = srow == scol

    for j in range(rows // c):
        sl = slice(j * c, (j + 1) * c)
        qb = q_ref[sl, :]
        kb = k_ref[sl, :]
        vb = v_ref[sl, :]
        inner_out = []
        for h in range(2):
            qh = jnp.where(low if h == 0 else ~low, qb, jnp.zeros_like(qb))
            inner = _dot_nt(qh, kb) * decay_ref[h]
            inner_out.append(_dot(inner.astype(_BF16), vb))
        state = state_ref[...]
        cross = _dot(qb, state.astype(_BF16)) * xi_ref[...]
        o = jnp.where(low, inner_out[0], inner_out[1]) + cross
        kz = _dot_nt(eye, kz_ref[sl, :]).astype(_BF16)
        state_ref[...] = cd_ref[...] * state + jnp.where(same_head, _dot(kz, vb), 0.0)
        mu = jnp.where(low,
                       jnp.sum(jnp.where(low, o, 0.0), axis=-1, keepdims=True),
                       jnp.sum(jnp.where(low, 0.0, o), axis=-1, keepdims=True)) * (1.0 / HEAD_DIM)
        dlt = o - mu
        d2 = dlt * dlt
        var = jnp.where(low,
                        jnp.sum(jnp.where(low, d2, 0.0), axis=-1, keepdims=True),
                        jnp.sum(jnp.where(low, 0.0, d2), axis=-1, keepdims=True)) * (1.0 / HEAD_DIM)
        y = (dlt * lax.rsqrt(var + EPS)) * gn_ref[...]
        g = gate_ref[sl, :]
        o_ref[sl, :] = ((g * jax.nn.sigmoid(g)) * y).astype(o_ref.dtype)


def _retention_tables():
    c = RET_CHUNK
    log_g = jnp.log(1.0 - jnp.exp2(-5.0 - jnp.arange(B_HEADS, dtype=_F32)))
    idx = jnp.arange(c, dtype=_F32)
    diff = idx[:, None] - idx[None, :]
    decay = jnp.where(diff >= 0, jnp.exp(log_g[:, None, None] * jnp.maximum(diff, 0.0)), 0.0)
    zeta = jnp.exp(log_g[:, None] * (c - 1 - idx))
    xi = jnp.exp(log_g[:, None] * (idx + 1.0))
    cd = jnp.exp(log_g * c)

    def lanes(tab):
        return jnp.repeat(tab.reshape(B_HEADS // 2, 2, c), HEAD_DIM, axis=1).transpose(0, 2, 1)

    cdm = jnp.repeat(cd.reshape(B_HEADS // 2, 2), HEAD_DIM, axis=1)
    cdm = jnp.broadcast_to(cdm[:, :, None], (B_HEADS // 2, LANES, LANES))
    return decay, lanes(zeta), lanes(xi), cdm


def _retention(qkzv, gate, tables, ret_gn_g):
    b, s, _ = qkzv.shape
    rows = min(s, 4096)
    pairs = B_HEADS // 2
    decay, _, xi, cdm = tables
    gn = ret_gn_g.reshape(pairs, 1, LANES)

    def col(off):
        return pl.BlockSpec((None, rows, LANES), lambda i, p, j: (i, j, off * pairs + p))

    return pl.pallas_call(
        functools.partial(_retention_kernel, rows=rows),
        grid=(b, pairs, s // rows),
        in_specs=[col(0), col(1), col(2), col(3), col(0),
                  pl.BlockSpec((2, RET_CHUNK, RET_CHUNK), lambda i, p, j: (p, 0, 0)),
                  pl.BlockSpec((None, RET_CHUNK, LANES), lambda i, p, j: (p, 0, 0)),
                  pl.BlockSpec((None, LANES, LANES), lambda i, p, j: (p, 0, 0)),
                  pl.BlockSpec((None, 1, LANES), lambda i, p, j: (p, 0, 0))],
        out_specs=pl.BlockSpec((None, rows, LANES), lambda i, p, j: (i, j, p)),
        out_shape=jax.ShapeDtypeStruct((b, s, B_WIDTH), _BF16),
        scratch_shapes=[pltpu.VMEM((LANES, LANES), _F32)],
        compiler_params=_cparams(3),
        name="retention_b",
    )(qkzv, qkzv, qkzv, qkzv, gate, decay, xi, cdm, gn)


def _cross_kernel(q_ref, k_ref, vt_ref, o_ref, st_ref, *, rows):
    low = _lane((ATT_Q, LANES)) < HEAD_DIM
    lane_blocks = C_WIDTH // LANES
    tiles = [(j, lb) for j in range(rows // ATT_Q) for lb in range(lane_blocks)]
    ones = jnp.ones((ONES_ROWS, vt_ref.shape[1]), _BF16)

    def scores(i):
        j, lb = tiles[i]
        sl = slice(lb * LANES, (lb + 1) * LANES)
        q = q_ref[j * ATT_Q:(j + 1) * ATT_Q, sl]
        q2 = jnp.concatenate([jnp.where(low, q, jnp.zeros_like(q)), jnp.where(low, jnp.zeros_like(q), q)], axis=0)
        st_ref[i % (ATT_AHEAD + 1)] = _dot_nt(k_ref[:, sl], q2)

    def finish(i):
        j, lb = tiles[i]
        sl = slice(lb * LANES, (lb + 1) * LANES)
        st = st_ref[i % (ATT_AHEAD + 1)]
        p = jnp.exp2(st - jnp.max(st, axis=0, keepdims=True))
        ot = _dot(jnp.concatenate([vt_ref[sl, :], ones], axis=0), p.astype(_BF16))
        inv = 1.0 / ot[LANES:LANES + 1, :]
        out_t = jnp.concatenate([ot[0:HEAD_DIM, 0:ATT_Q] * inv[:, 0:ATT_Q],
                                 ot[HEAD_DIM:LANES, ATT_Q:] * inv[:, ATT_Q:]], axis=0)
        o_ref[j * ATT_Q:(j + 1) * ATT_Q, sl] = out_t.T.astype(o_ref.dtype)

    for i in range(min(ATT_AHEAD, len(tiles))):
        scores(i)
    for i in range(len(tiles)):
        if i + ATT_AHEAD < len(tiles):
            scores(i + ATT_AHEAD)
        finish(i)


def _cross_attention(qc, kc, vtc):
    b, s, _ = qc.shape
    m = kc.shape[1]
    rows = min(s, 2048)
    return pl.pallas_call(
        functools.partial(_cross_kernel, rows=rows),
        grid=(b, s // rows),
        in_specs=[pl.BlockSpec((None, rows, C_WIDTH), lambda i, j: (i, j, 0)),
                  pl.BlockSpec((None, m, C_WIDTH), lambda i, j: (i, 0, 0)),
                  pl.BlockSpec((None, C_WIDTH, m), lambda i, j: (i, 0, 0))],
        out_specs=pl.BlockSpec((None, rows, C_WIDTH), lambda i, j: (i, j, 0)),
        out_shape=jax.ShapeDtypeStruct((b, s, C_WIDTH), _BF16),
        scratch_shapes=[pltpu.VMEM((ATT_AHEAD + 1, m, 2 * ATT_Q), _F32)],
        compiler_params=_cparams(2),
        name="cross_c",
    )(qc, kc, vtc)


def _out_router_kernel(x_ref, a_ref, b_ref, c_ref, wo_ref, g_ref, wr_ref, br_ref,
                       h_ref, hn_ref, info_ref, rows_ref, cnt_ref, carry_ref):
    @pl.when(pl.program_id(0) == 0)
    def _():
        carry_ref[...] = jnp.zeros_like(carry_ref)

    tm = x_ref.shape[0]
    h = x_ref[...]
    h = h + _dot(a_ref[...], wo_ref[0:A_WIDTH, :])
    h = h + _dot(b_ref[...], wo_ref[A_WIDTH:A_WIDTH + B_WIDTH, :])
    h = h + _dot(c_ref[...], wo_ref[A_WIDTH + B_WIDTH:, :])
    h_ref[...] = h
    ms = jnp.mean(h * h, axis=-1, keepdims=True)
    hn = (h * lax.rsqrt(ms + EPS)) * g_ref[...]
    _pack_rows(hn_ref, hn)
    logits = _dot_nt(wr_ref[...], hn.astype(_BF16))[0:ROUTE_ROWS, :] + br_ref[:, 0:1]
    row = lax.broadcasted_iota(jnp.int32, (ROUTE_ROWS, tm), 0).astype(_F32)
    big = float(ROUTE_ROWS)

    def first_row(mask):
        return jnp.min(jnp.where(mask, row, big), axis=0, keepdims=True)

    gmask = row < N_GROUPS
    gl = jnp.where(gmask, logits, NEG_INF)
    ge = jnp.exp(gl - jnp.max(gl, axis=0, keepdims=True))
    gp = ge / jnp.sum(ge, axis=0, keepdims=True)
    p_group = jnp.max(gp, axis=0, keepdims=True)
    g_sel = first_row(gmask & (gp == p_group))
    lo = ROUTE_LANE0 + g_sel * EXPERTS_PER_GROUP
    emask = (row >= lo) & (row < lo + EXPERTS_PER_GROUP)
    el = jnp.where(emask, logits, NEG_INF)
    ee = jnp.exp(el - jnp.max(el, axis=0, keepdims=True))
    ep = ee / jnp.sum(ee, axis=0, keepdims=True)
    p1 = jnp.max(ep, axis=0, keepdims=True)
    i1 = first_row(emask & (ep == p1))
    ep2 = jnp.where(emask & (row != i1), ep, -1.0)
    p2 = jnp.max(ep2, axis=0, keepdims=True)
    i2 = first_row(ep2 == p2)
    den = p1 + p2
    w1 = p_group * (p1 / den)
    w2 = p_group * (p2 / den)
    hit1 = row == i1
    hit2 = row == i2
    onehot = jnp.where(hit1 | hit2, 1.0, 0.0)
    r_i = lax.broadcasted_iota(jnp.int32, (tm, tm), 0)
    c_i = lax.broadcasted_iota(jnp.int32, (tm, tm), 1)
    earlier = jnp.where(r_i < c_i, 1.0, 0.0).astype(_BF16)
    before = _dot(onehot.astype(_BF16), earlier) + carry_ref[:, 0:1]
    r1 = jnp.sum(jnp.where(hit1, before, 0.0), axis=0, keepdims=True)
    r2 = jnp.sum(jnp.where(hit2, before, 0.0), axis=0, keepdims=True)
    carry_ref[...] = carry_ref[...] + jnp.sum(onehot, axis=1, keepdims=True)
    cnt_ref[...] = carry_ref[...]
    out_row = lax.broadcasted_iota(jnp.int32, (LANES, tm), 0)
    info = jnp.where(out_row == 0, w1, 0.0)
    info = jnp.where(out_row == 1, w2, info)
    info = jnp.where(out_row == 2, i1 - ROUTE_LANE0, info)
    info = jnp.where(out_row == 3, i2 - ROUTE_LANE0, info)
    info = jnp.where(out_row == 4, r1, info)
    info = jnp.where(out_row == 5, r2, info)
    rows_ref[...] = info[0:SUBLANES, :]
    info_ref[...] = info.T


def _out_router(x2, oa, ob, oc, w_out, ffn_g, w_rg, b_rg, w_re, b_re):
    t, d = x2.shape
    tm = min(t, 512)
    pad = LANES - N_GROUPS - N_EXPERTS
    wr = jnp.concatenate([w_rg, w_re, jnp.zeros((d, pad), _F32)], axis=1).T.astype(_BF16)
    br = jnp.concatenate([b_rg, b_re, jnp.zeros((ROUTE_ROWS - N_GROUPS - N_EXPERTS,), _F32)])
    br = jnp.broadcast_to(br[:, None], (ROUTE_ROWS, LANES))

    def rows(w):
        return pl.BlockSpec((tm, w), lambda i: (i, 0))

    def whole(r, c):
        return pl.BlockSpec((r, c), lambda i: (0, 0))

    return pl.pallas_call(
        _out_router_kernel,
        grid=(t // tm,),
        in_specs=[rows(d), rows(A_WIDTH), rows(B_WIDTH), rows(C_WIDTH), whole(d, d), whole(1, d),
                  whole(LANES, d), whole(ROUTE_ROWS, LANES)],
        out_specs=[rows(d), pl.BlockSpec((tm * PACK_ROWS, LANES), lambda i: (i, 0)), rows(LANES),
                   pl.BlockSpec((SUBLANES, tm), lambda i: (0, i)), whole(ROUTE_ROWS, LANES)],
        out_shape=[jax.ShapeDtypeStruct((t, d), _F32), jax.ShapeDtypeStruct((t * PACK_ROWS, LANES), jnp.uint32),
                   jax.ShapeDtypeStruct((t, LANES), _F32), jax.ShapeDtypeStruct((SUBLANES, t), _F32),
                   jax.ShapeDtypeStruct((ROUTE_ROWS, LANES), _F32)],
        scratch_shapes=[pltpu.VMEM((ROUTE_ROWS, LANES), _F32)],
        compiler_params=_cparams(1),
        name="out_router",
    )(x2, oa, ob, oc, w_out.astype(_BF16), ffn_g.reshape(1, d), wr, br)


DISPATCH_TOKENS = 2048
COMBINE_TOKENS = 256


ROW_UNROLL = 8


def _tile_rows(row, count=1, per=SUBLANES):
    start = row * per
    if not isinstance(start, int):
        start = pl.multiple_of(start, per)
    return pl.ds(start, count * per)


def _row_copy(src, s_row, dst, d_row, sem, per=SUBLANES):
    return pltpu.make_async_copy(src.at[_tile_rows(s_row, 1, per)], dst.at[_tile_rows(d_row, 1, per)], sem)


def _dispatch_kernel(pad_start_ref, pad_len_ref, used_ref, dest_ref, hn_ref, xs_ref, zero_ref, sem,
                     pad_sem):
    per = PACK_ROWS
    n = hn_ref.shape[0] // per

    @pl.when(pl.program_id(0) == 0)
    def _():
        zero_ref[...] = jnp.zeros_like(zero_ref)
        n_blocks = xs_ref.shape[0] // (ROW_BLOCK * per)

        def block_copy(blk):
            return pltpu.make_async_copy(zero_ref, xs_ref.at[_tile_rows(blk * ROW_BLOCK, ROW_BLOCK, per)],
                                         pad_sem)

        def put_block(blk, carry):
            block_copy(blk).start()
            return carry

        def done_block(blk, carry):
            block_copy(blk).wait()
            return carry

        lax.fori_loop(used_ref[0], n_blocks, put_block, 0)
        lax.fori_loop(used_ref[0], n_blocks, done_block, 0)
        bits = [1 << k for k in reversed(range(ROW_BLOCK.bit_length() - 1))]

        def tail(e, wait):
            row = pad_start_ref[e]
            for bit in bits:
                on = (pad_len_ref[e] & bit) != 0
                copy = pltpu.make_async_copy(zero_ref.at[_tile_rows(0, bit, per)],
                                             xs_ref.at[_tile_rows(row, bit, per)], pad_sem)

                @pl.when(on)
                def _():
                    copy.wait() if wait else copy.start()

                row = row + jnp.where(on, bit, 0)

        def put_tail(e, carry):
            tail(e, False)
            return carry

        def done_tail(e, carry):
            tail(e, True)
            return carry

        lax.fori_loop(0, N_EXPERTS, put_tail, 0)
        lax.fori_loop(0, N_EXPERTS, done_tail, 0)

    def issue(i, carry):
        for u in range(ROW_UNROLL):
            t = i * ROW_UNROLL + u
            _row_copy(hn_ref, t, xs_ref, dest_ref[2 * t], sem, per).start(priority=0)
            _row_copy(hn_ref, t, xs_ref, dest_ref[2 * t + 1], sem, per).start(priority=1)
        return carry

    lax.fori_loop(0, n // ROW_UNROLL, issue, 0)
    for _ in range(2):
        pltpu.make_async_copy(hn_ref, xs_ref.at[_tile_rows(0, n, per)], sem).wait()


def _dispatch(hn, dest, pad_start, pad_len, n_used, n_rows):
    t = hn.shape[0] // PACK_ROWS
    n = min(t, DISPATCH_TOKENS)
    return pl.pallas_call(
        _dispatch_kernel,
        grid_spec=pltpu.PrefetchScalarGridSpec(
            num_scalar_prefetch=3,
            grid=(t // n,),
            in_specs=[pl.BlockSpec((2 * n,), lambda i, *_: (i,), memory_space=pltpu.SMEM),
                      pl.BlockSpec((n * PACK_ROWS, LANES), lambda i, *_: (i, 0))],
            out_specs=pl.BlockSpec(memory_space=pl.ANY),
            scratch_shapes=[pltpu.VMEM((ROW_BLOCK * PACK_ROWS, LANES), hn.dtype), pltpu.SemaphoreType.DMA,
                            pltpu.SemaphoreType.DMA]),
        out_shape=jax.ShapeDtypeStruct((n_rows * PACK_ROWS, LANES), hn.dtype),
        compiler_params=_cparams(1),
        name="moe_dispatch",
    )(pad_start, pad_len, n_used, dest, hn)


def _expert_kernel(be_ref, run_ref, next_ref, used_ref, x_ref, wg_hbm, wu_hbm, wd_hbm, y_ref,
                   wg_f32, wu_f32, wd_f32, wg_bf, wu_bf, wd_bf, sem):
    i = pl.program_id(0)
    live = i < used_ref[0]
    new_expert = (i == 0) | (be_ref[i] != be_ref[jnp.maximum(i - 1, 0)])
    slot = run_ref[i] % 2

    def fetch(expert, to_slot):
        return [pltpu.make_async_copy(src.at[expert], dst.at[to_slot], sem.at[to_slot, k])
                for k, (src, dst) in enumerate(((wg_hbm, wg_f32), (wu_hbm, wu_f32), (wd_hbm, wd_f32)))]

    @pl.when(live & (i == 0))
    def _():
        for copy in fetch(be_ref[0], 0):
            copy.start()

    @pl.when(live & new_expert)
    def _():
        for copy in fetch(be_ref[i], slot):
            copy.wait()

        @pl.when(next_ref[i] >= 0)
        def _():
            for copy in fetch(next_ref[i], 1 - slot):
                copy.start()

        wg_bf[...] = wg_f32[slot].astype(_BF16)
        wu_bf[...] = wu_f32[slot].astype(_BF16)
        wd_bf[...] = wd_f32[slot].astype(_BF16)

    @pl.when(live)
    def _():
        sub = ROW_BLOCK // EXPERT_SPLIT
        gate_up = {}

        def first(k):
            x = _unpack_rows(x_ref, sub, k * sub)
            gate_up[k] = (_dot(x, wg_bf[...]), _dot(x, wu_bf[...]))

        def second(k):
            gate, up = gate_up.pop(k)
            act = (gate * jax.nn.sigmoid(gate)) * up
            _rows_to_tiles(y_ref, _dot(act.astype(_BF16), wd_bf[...]), k * sub)

        for k in range(min(EXPERT_AHEAD, EXPERT_SPLIT)):
            first(k)
        for k in range(EXPERT_SPLIT):
            if k + EXPERT_AHEAD < EXPERT_SPLIT:
                first(k + EXPERT_AHEAD)
            second(k)

    @pl.when(i >= used_ref[0])
    def _():
        y_ref[...] = jnp.zeros_like(y_ref)


def _experts(xs, blocks, w_gate, w_up, w_down):
    n_rows, d = xs.shape[0] // PACK_ROWS, D_MODEL
    n_blocks = n_rows // ROW_BLOCK
    tile_block = (ROW_BLOCK * SUBLANES, LANES)
    hbm = pl.BlockSpec(memory_space=pl.ANY)

    return pl.pallas_call(
        _expert_kernel,
        grid_spec=pltpu.PrefetchScalarGridSpec(
            num_scalar_prefetch=4,
            grid=(n_blocks,),
            in_specs=[pl.BlockSpec((ROW_BLOCK * PACK_ROWS, LANES),
                                   lambda i, be, run, nxt, used: (jnp.minimum(i, used[0] - 1), 0)),
                      hbm, hbm, hbm],
            out_specs=pl.BlockSpec(tile_block, lambda i, *_: (i, 0)),
            scratch_shapes=[pltpu.VMEM((2, d, D_EXPERT), _F32), pltpu.VMEM((2, d, D_EXPERT), _F32),
                            pltpu.VMEM((2, D_EXPERT, d), _F32),
                            pltpu.VMEM((d, D_EXPERT), _BF16), pltpu.VMEM((d, D_EXPERT), _BF16),
                            pltpu.VMEM((D_EXPERT, d), _BF16), pltpu.SemaphoreType.DMA((2, 3))]),
        out_shape=jax.ShapeDtypeStruct((n_rows * SUBLANES, LANES), _F32),
        compiler_params=_cparams(1),
        name="moe_experts",
    )(*blocks, xs, w_gate, w_up, w_down)


def _combine_kernel(dest_ref, next_ref, h_ref, info_ref, ys_ref, o_ref, buf_ref, sem):
    n = h_ref.shape[0]
    step = pl.program_id(0)
    slot = step % 2

    def gather(idx_ref, to_slot):
        def issue(i, carry):
            for u in range(ROW_UNROLL):
                t = i * ROW_UNROLL + u
                _row_copy(ys_ref, idx_ref[2 * t], buf_ref.at[to_slot, 0], t,
                          sem.at[to_slot]).start(priority=0)
                _row_copy(ys_ref, idx_ref[2 * t + 1], buf_ref.at[to_slot, 1], t,
                          sem.at[to_slot]).start(priority=1)
            return carry

        lax.fori_loop(0, n // ROW_UNROLL, issue, 0)

    @pl.when(step == 0)
    def _():
        gather(dest_ref, 0)

    @pl.when(step + 1 < pl.num_programs(0))
    def _():
        gather(next_ref, 1 - slot)

    for k in range(2):
        pltpu.make_async_copy(ys_ref.at[_tile_rows(0, n)], buf_ref.at[slot, k], sem.at[slot]).wait()
    info = info_ref[...]
    w0 = info[:, 0:1]
    w1 = info[:, 1:2]
    for s in range(SUBLANES):
        sl = slice(s * LANES, (s + 1) * LANES)
        moe = w0 * _tile_block(buf_ref.at[slot, 0], s, n) + w1 * _tile_block(buf_ref.at[slot, 1], s, n)
        o_ref[:, sl] = h_ref[:, sl] + moe


def _combine(h, info, ys, dest):
    t, d = h.shape
    n = min(t, COMBINE_TOKENS)
    steps = t // n
    return pl.pallas_call(
        _combine_kernel,
        grid=(steps,),
        in_specs=[pl.BlockSpec((2 * n,), lambda i: (i,), memory_space=pltpu.SMEM),
                  pl.BlockSpec((2 * n,), lambda i: (jnp.minimum(i + 1, steps - 1),),
                               memory_space=pltpu.SMEM),
                  pl.BlockSpec((n, d), lambda i: (i, 0)),
                  pl.BlockSpec((n, LANES), lambda i: (i, 0)),
                  pl.BlockSpec(memory_space=pl.ANY)],
        out_specs=pl.BlockSpec((n, d), lambda i: (i, 0)),
        out_shape=jax.ShapeDtypeStruct((t, d), _F32),
        scratch_shapes=[pltpu.VMEM((2, 2, n * SUBLANES, LANES), _F32), pltpu.SemaphoreType.DMA((2,))],
        compiler_params=_cparams(1),
        name="moe_combine",
    )(dest, dest, h, info, ys)


def _moe_layout(route_rows, counts, t):
    counts = counts[ROUTE_LANE0:ROUTE_LANE0 + N_EXPERTS, 0].astype(jnp.int32)
    padded = (counts + ROW_BLOCK - 1) // ROW_BLOCK * ROW_BLOCK
    pends = jnp.cumsum(padded)
    pstarts = pends - padded
    eid = route_rows[2:4].astype(jnp.int32)
    rank = route_rows[4:6].astype(jnp.int32)
    experts = jnp.arange(N_EXPERTS, dtype=jnp.int32)
    start_of = jnp.sum(jnp.where(eid[:, :, None] == experts, pstarts, 0), axis=-1)
    dest = (start_of + rank).T.reshape(-1)
    n_blocks = -(-2 * t // ROW_BLOCK) + N_EXPERTS
    first_row = jnp.arange(n_blocks, dtype=jnp.int32) * ROW_BLOCK
    block_e = jnp.minimum(jnp.sum((pends[None, :] <= first_row[:, None]).astype(jnp.int32), axis=1),
                          N_EXPERTS - 1)
    n_used = (pends[-1:] // ROW_BLOCK).astype(jnp.int32)
    changed = jnp.concatenate([jnp.zeros((1,), jnp.int32), (block_e[1:] != block_e[:-1]).astype(jnp.int32)])
    block_run = jnp.cumsum(changed)
    later = (counts[None, :] > 0) & (experts[None, :] > experts[:, None])
    next_expert = jnp.min(jnp.where(later, experts[None, :], N_EXPERTS), axis=1)
    next_expert = jnp.where(next_expert < N_EXPERTS, next_expert, -1)
    block_next = jnp.sum(jnp.where(block_e[:, None] == experts[None, :], next_expert[None, :], 0), axis=1)
    blocks = (block_e, block_run.astype(jnp.int32), block_next.astype(jnp.int32), n_used)
    return dest, blocks, pstarts + counts, padded - counts, n_blocks * ROW_BLOCK


def kernel(x, mem, positions, mix_norm_g, w_in, qn_a, kn_a, rel_bias, ret_gn_g, mem_norm_g, w_mem_kv,
           qn_c, kn_c, w_out, ffn_norm_g, w_router_group, b_router_group, w_router_expert,
           b_router_expert, w_gate, w_up, w_down):
    b, s, d = x.shape
    t = b * s
    x2 = x.reshape(t, d)
    kc, vc = _mem_kv(mem, mem_norm_g, w_mem_kv, kn_c)
    tables = _retention_tables()
    qa, ka, vta, qkzv, gate, qc = _in_proj(x, positions, tables[1], mix_norm_g, w_in, qn_a, kn_a, qn_c)
    out_a = _attention(qa, ka, vta, rel_bias)
    out_b = _retention(qkzv, gate, tables, ret_gn_g)
    out_c = _cross_attention(qc, kc, vc)
    h, hn, info, route_rows, counts = _out_router(
        x2, out_a.reshape(t, A_WIDTH), out_b.reshape(t, B_WIDTH), out_c.reshape(t, C_WIDTH),
        w_out, ffn_norm_g, w_router_group, b_router_group, w_router_expert, b_router_expert)
    dest, blocks, pad_start, pad_len, n_rows = _moe_layout(route_rows, counts, t)
    xs = _dispatch(hn, dest, pad_start, pad_len, blocks[-1], n_rows)
    ys = _experts(xs, blocks, w_gate, w_up, w_down)
    return _combine(h, info, ys, dest).reshape(b, s, d)
```

```python
import functools

import jax
import jax.numpy as jnp
from jax import lax
from jax.experimental import pallas as pl
from jax.experimental.pallas import tpu as pltpu

D_MODEL = 1024
CHUNK = 64
HEAD_DIM = 64
A_HEADS = 8
B_HEADS = 4
C_HEADS = 4
A_WIDTH = A_HEADS * HEAD_DIM
B_WIDTH = B_HEADS * HEAD_DIM
C_WIDTH = C_HEADS * HEAD_DIM
IN_COLS = 3 * A_WIDTH + 4 * B_WIDTH + C_WIDTH
LEFT_CHUNKS = 8
BAND_CHUNKS = LEFT_CHUNKS + 1
MAX_REL_DIST = 128
ROPE_BASE = 10000.0
N_GROUPS = 4
EXPERTS_PER_GROUP = 8
N_EXPERTS = N_GROUPS * EXPERTS_PER_GROUP
D_EXPERT = D_MODEL // 2
EPS = 1e-6
NEG_INF = -1e30
LOG2E = 1.4426950408889634

LANES = 128
SUBLANES = 8
assert D_MODEL == SUBLANES * LANES
PACK_ROWS = SUBLANES // 2
LEFT_ROWS = LEFT_CHUNKS * CHUNK
ATT_Q = 2 * CHUNK
ATT_K = ATT_Q + LEFT_ROWS
ATT_VARIANTS = LEFT_ROWS // ATT_Q + 1
ONES_ROWS = 16
ATT_AHEAD = 3
RET_CHUNK = 256
ROW_BLOCK = 512
EXPERT_SPLIT = 2
EXPERT_AHEAD = 2
ROUTE_LANE0 = N_GROUPS
ROUTE_ROWS = 64
VMEM_LIMIT = 48 * 1024 * 1024

_F32 = jnp.float32
_BF16 = jnp.bfloat16


def _cparams(n_axes):
    return pltpu.CompilerParams(dimension_semantics=("arbitrary",) * n_axes,
                                vmem_limit_bytes=VMEM_LIMIT)


def _dot(a, b):
    return jnp.dot(a, b, preferred_element_type=_F32)


def _dot_nt(a, b):
    return lax.dot_general(a, b, (((1,), (1,)), ((), ())), preferred_element_type=_F32)


def _lane(shape):
    return lax.broadcasted_iota(jnp.int32, shape, len(shape) - 1)


def _pair_rms(t, gain):
    low = _lane(t.shape) < HEAD_DIM
    t2 = t * t
    ms0 = jnp.sum(jnp.where(low, t2, 0.0), axis=-1, keepdims=True) * (1.0 / HEAD_DIM)
    ms1 = jnp.sum(jnp.where(low, 0.0, t2), axis=-1, keepdims=True) * (1.0 / HEAD_DIM)
    r = jnp.where(low, lax.rsqrt(ms0 + EPS), lax.rsqrt(ms1 + EPS))
    return (t * r) * gain


def _rows_to_tiles(ref, val, row0=0):
    n = val.shape[0]
    for s in range(SUBLANES):
        ref[pl.ds(row0 * SUBLANES + s, n, stride=SUBLANES), :] = val[:, s * LANES:(s + 1) * LANES]


def _tile_block(ref, s, n, row0=0):
    return ref[pl.ds(row0 * SUBLANES + s, n, stride=SUBLANES), :]


def _pack_rows(ref, val, row0=0):
    n = val.shape[0]
    for s in range(PACK_ROWS):
        lo = val[:, (2 * s) * LANES:(2 * s + 1) * LANES].astype(_BF16).astype(_F32)
        hi = val[:, (2 * s + 1) * LANES:(2 * s + 2) * LANES].astype(_BF16).astype(_F32)
        word = (lax.bitcast_convert_type(lo, jnp.uint32) >> 16) | (
            lax.bitcast_convert_type(hi, jnp.uint32) & jnp.uint32(0xFFFF0000))
        ref[pl.ds(row0 * PACK_ROWS + s, n, stride=PACK_ROWS), :] = word


def _unpack_rows(ref, n, row0=0):
    parts = []
    for s in range(PACK_ROWS):
        word = ref[pl.ds(row0 * PACK_ROWS + s, n, stride=PACK_ROWS), :]
        parts.append(lax.bitcast_convert_type(word << 16, _F32))
        parts.append(lax.bitcast_convert_type(word & jnp.uint32(0xFFFF0000), _F32))
    return jnp.concatenate(parts, axis=-1).astype(_BF16)


ROPE_HALF = HEAD_DIM // 2
ROPE_PACK = LANES // ROPE_HALF


def _rope_tables(pos_ref, inv_ref, cos_ref, sin_ref):
    ang = pos_ref[...].astype(_F32) * inv_ref[...]
    rows = ang.shape[0]
    lane = _lane(ang.shape)
    sign = jnp.where((lane % HEAD_DIM) < ROPE_HALF, -1.0, 1.0)
    for out_ref, val in ((cos_ref, jnp.cos(ang)), (sin_ref, jnp.sin(ang))):
        for j in range(ROPE_PACK):
            seg = jnp.where(lane // ROPE_HALF == j, val, 0.0)
            full = seg
            for k in range(1, ROPE_PACK):
                full = full + pltpu.roll(seg, k * ROPE_HALF, 1)
            if out_ref is sin_ref:
                full = full * sign
            out_ref[pl.ds(j, rows, stride=ROPE_PACK), :] = full


def _rope_inputs(positions):
    b, s = positions.shape
    inv = ROPE_BASE ** (-jnp.arange(ROPE_HALF, dtype=_F32) / ROPE_HALF)
    inv128 = jnp.tile(inv, ROPE_PACK).reshape(1, LANES)
    pos = jnp.repeat(positions.reshape(b, s // ROPE_PACK, ROPE_PACK), ROPE_HALF, axis=2)
    return pos, inv128


def _mem_kv_kernel(mem_ref, g_ref, w_ref, kn_ref, k_ref, v_ref):
    m = mem_ref[...]
    ms = jnp.mean(m * m, axis=-1, keepdims=True)
    mn = (m * lax.rsqrt(ms + EPS)) * g_ref[...]
    kv = _dot(mn.astype(_BF16), w_ref[...])
    for j in range(C_WIDTH // LANES):
        sl = slice(j * LANES, (j + 1) * LANES)
        k_ref[:, sl] = _pair_rms(kv[:, sl], kn_ref[...]).astype(_BF16)
    v_ref[...] = kv[:, C_WIDTH:].T.astype(_BF16)


def _mem_kv(mem, mem_norm_g, w_mem_kv, kn_c):
    b, m, d = mem.shape
    kn = jnp.tile(kn_c, 2).reshape(1, LANES)
    return pl.pallas_call(
        _mem_kv_kernel,
        grid=(b,),
        in_specs=[pl.BlockSpec((None, m, d), lambda i: (i, 0, 0)),
                  pl.BlockSpec((1, d), lambda i: (0, 0)),
                  pl.BlockSpec((d, 2 * C_WIDTH), lambda i: (0, 0)),
                  pl.BlockSpec((1, LANES), lambda i: (0, 0))],
        out_specs=[pl.BlockSpec((None, m, C_WIDTH), lambda i: (i, 0, 0)),
                   pl.BlockSpec((None, C_WIDTH, m), lambda i: (i, 0, 0))],
        out_shape=[jax.ShapeDtypeStruct((b, m, C_WIDTH), _BF16),
                   jax.ShapeDtypeStruct((b, C_WIDTH, m), _BF16)],
        compiler_params=_cparams(1),
        name="mem_kv",
    )(mem, mem_norm_g.reshape(1, d), w_mem_kv.astype(_BF16), kn)


def _in_proj_kernel(x_ref, pos_ref, inv_ref, g_ref, wq_ref, wk_ref, wvt_ref, wr_ref, wc_ref, qn_ref, kn_ref,
                    cn_ref, zeta_ref, qa_ref, ka_ref, vt_ref, ret_ref, gate_ref, qc_ref, xn_ref, acc_ref,
                    accb_ref, cos_ref, sin_ref):
    x = x_ref[...]
    ms = jnp.mean(x * x, axis=-1, keepdims=True)
    xn_ref[...] = ((x * lax.rsqrt(ms + EPS)) * g_ref[...]).astype(_BF16)

    def normed(slot, out_ref, gain_ref):
        for blk in range(out_ref.shape[1] // LANES):
            sl = slice(blk * LANES, (blk + 1) * LANES)
            out_ref[:, sl] = _pair_rms(acc_ref[slot, :, sl], gain_ref[...]).astype(_BF16)

    acc_ref[0] = _dot(xn_ref[...], wq_ref[...])
    acc_ref[1] = _dot(xn_ref[...], wk_ref[...])
    _rope_tables(pos_ref, inv_ref, cos_ref, sin_ref)
    normed(0, qa_ref, qn_ref)
    acc_ref[0] = _dot_nt(wvt_ref[...], xn_ref[...])
    normed(1, ka_ref, kn_ref)
    accb_ref[...] = _dot(xn_ref[...], wr_ref[...])
    for blk in range(vt_ref.shape[0]):
        vt_ref[blk] = acc_ref[0, :, blk * LANES:(blk + 1) * LANES].astype(_BF16)
    acc_ref[1, :, 0:C_WIDTH] = _dot(xn_ref[...], wc_ref[...])
    cos, sin = cos_ref[...], sin_ref[...]
    chunks = x_ref.shape[0] // RET_CHUNK
    for p in range(B_WIDTH // LANES):
        sl = slice(p * LANES, (p + 1) * LANES)
        q = accb_ref[:, sl]
        k = accb_ref[:, B_WIDTH + p * LANES:B_WIDTH + (p + 1) * LANES]
        kr = (k * cos + _swap_halves(k) * sin) * (HEAD_DIM ** -0.5)
        ret_ref[:, sl] = (q * cos + _swap_halves(q) * sin).astype(_BF16)
        ret_ref[:, B_WIDTH + p * LANES:B_WIDTH + (p + 1) * LANES] = kr.astype(_BF16)
        ret_ref[:, 2 * B_WIDTH + p * LANES:2 * B_WIDTH + (p + 1) * LANES] = (
            kr * jnp.concatenate([zeta_ref[p]] * chunks, axis=0)).astype(_BF16)
        ret_ref[:, 3 * B_WIDTH + p * LANES:3 * B_WIDTH + (p + 1) * LANES] = accb_ref[
            :, 2 * B_WIDTH + p * LANES:2 * B_WIDTH + (p + 1) * LANES].astype(_BF16)
        gate_ref[:, sl] = accb_ref[:, 3 * B_WIDTH + p * LANES:3 * B_WIDTH + (p + 1) * LANES]
    normed(1, qc_ref, cn_ref)


def _in_proj(x3, positions, zeta, g, w_in, qn_a, kn_a, qn_c):
    b, s, d = x3.shape
    tm = min(s, 512)
    assert tm == A_WIDTH
    assert tm % RET_CHUNK == 0
    pos, inv128 = _rope_inputs(positions)
    w = w_in.astype(_BF16)
    cuts = [0, A_WIDTH, 2 * A_WIDTH, 3 * A_WIDTH, 3 * A_WIDTH + 4 * B_WIDTH, IN_COLS]
    wq, wk, wv, wr, wc = (w[:, lo:hi] for lo, hi in zip(cuts[:-1], cuts[1:]))
    scale = HEAD_DIM ** -0.5 * LOG2E
    gains = [(jnp.tile(gn, 2) * sc).reshape(1, LANES) for gn, sc in ((qn_a, scale), (kn_a, 1.0), (qn_c, scale))]

    def whole(arr):
        return pl.BlockSpec(arr.shape, lambda i, j: (0,) * arr.ndim)

    def rows(width):
        return pl.BlockSpec((None, tm, width), lambda i, j: (i, j, 0))

    consts = [inv128, g.reshape(1, d), wq, wk, wv.T, wr, wc] + gains + [zeta]
    return pl.pallas_call(
        _in_proj_kernel,
        grid=(b, s // tm),
        in_specs=[rows(d), pl.BlockSpec((None, tm // ROPE_PACK, LANES), lambda i, j: (i, j, 0))]
        + [whole(c) for c in consts],
        out_specs=[rows(A_WIDTH), rows(A_WIDTH),
                   pl.BlockSpec((None, tm // LANES, A_WIDTH, LANES), lambda i, j: (i, j, 0, 0)),
                   rows(4 * B_WIDTH), rows(B_WIDTH), rows(C_WIDTH)],
        out_shape=[jax.ShapeDtypeStruct((b, s, A_WIDTH), _BF16), jax.ShapeDtypeStruct((b, s, A_WIDTH), _BF16),
                   jax.ShapeDtypeStruct((b, s // LANES, A_WIDTH, LANES), _BF16),
                   jax.ShapeDtypeStruct((b, s, 4 * B_WIDTH), _BF16), jax.ShapeDtypeStruct((b, s, B_WIDTH), _F32),
                   jax.ShapeDtypeStruct((b, s, C_WIDTH), _BF16)],
        scratch_shapes=[pltpu.VMEM((tm, d), _BF16), pltpu.VMEM((2, tm, A_WIDTH), _F32),
                        pltpu.VMEM((tm, 4 * B_WIDTH), _F32), pltpu.VMEM((tm, LANES), _F32),
                        pltpu.VMEM((tm, LANES), _F32)],
        compiler_params=_cparams(2),
        name="in_proj",
    )(x3, pos, *consts)


def _attn_kernel(q_ref, k_ref, vt_ref, bias_ref, o_ref, kp_ref, st_ref, var_ref, *, q_rows):
    qs = pl.program_id(2)
    s = k_ref.shape[0]
    fill_rows = min(s, 1024)
    left_blocks = LEFT_ROWS // LANES

    @pl.when(qs == 0)
    def _():
        kp_ref[0:LEFT_ROWS, :] = jnp.zeros((LEFT_ROWS, LANES), _BF16)

        def fill(i, carry):
            r = pl.multiple_of(i * fill_rows, fill_rows)
            kp_ref[pl.ds(LEFT_ROWS + r, fill_rows), :] = k_ref[pl.ds(r, fill_rows), :]
            return carry

        lax.fori_loop(0, s // fill_rows, fill, 0)
        key = lax.broadcasted_iota(jnp.int32, (ATT_K, 2 * ATT_Q), 0)
        for v in range(ATT_VARIANTS):
            var_ref[v] = jnp.where(key >= LEFT_ROWS - ATT_Q * v, bias_ref[...], NEG_INF)

    low = _lane((ATT_Q, LANES)) < HEAD_DIM
    ones = jnp.ones((ONES_ROWS, ATT_K), _BF16)
    tiles_per_step = q_rows // ATT_Q

    def scores(j):
        cp = qs * tiles_per_step + j
        q = q_ref[j * ATT_Q:(j + 1) * ATT_Q, :]
        q2 = jnp.concatenate([jnp.where(low, q, jnp.zeros_like(q)), jnp.where(low, jnp.zeros_like(q), q)], axis=0)
        kb = kp_ref[pl.ds(pl.multiple_of(cp * ATT_Q, ATT_Q), ATT_K), :]
        st_ref[j % (ATT_AHEAD + 1)] = _dot_nt(kb, q2) + var_ref[jnp.minimum(cp, ATT_VARIANTS - 1)]

    def finish(j):
        cp = qs * tiles_per_step + j
        st = st_ref[j % (ATT_AHEAD + 1)]
        m = jnp.max(st, axis=0, keepdims=True)
        p = jnp.exp2(st - m)
        vt = jnp.concatenate([vt_ref[jnp.maximum(cp + kb_i - left_blocks, 0)] for kb_i in range(ATT_K // LANES)],
                             axis=1)
        ot = _dot(jnp.concatenate([vt, ones], axis=0), p.astype(_BF16))
        inv = 1.0 / ot[LANES:LANES + 1, :]
        out_t = jnp.concatenate([ot[0:HEAD_DIM, 0:ATT_Q] * inv[:, 0:ATT_Q],
                                 ot[HEAD_DIM:LANES, ATT_Q:] * inv[:, ATT_Q:]], axis=0)
        o_ref[j * ATT_Q:(j + 1) * ATT_Q, :] = out_t.T.astype(o_ref.dtype)

    for j in range(min(ATT_AHEAD, tiles_per_step)):
        scores(j)
    for j in range(tiles_per_step):
        if j + ATT_AHEAD < tiles_per_step:
            scores(j + ATT_AHEAD)
        finish(j)


def _toeplitz_bias(rel_bias, q_len, k_len):
    h, table = rel_bias.shape
    n_diag = q_len + k_len - 1
    flat_lo = k_len - 1 - LEFT_ROWS - (CHUNK - 1)
    flat_hi = n_diag - flat_lo - table
    rev = jnp.concatenate([jnp.broadcast_to(rel_bias[:, -1:], (h, flat_hi)), rel_bias[:, ::-1],
                           jnp.broadcast_to(rel_bias[:, :1], (h, flat_lo))], axis=1).astype(_F32)
    flat = jnp.tile(rev, (1, q_len + 1))
    pitch = n_diag - 1
    skew = flat[:, q_len - 1:q_len - 1 + q_len * pitch].reshape(h, q_len, pitch)
    return skew[:, :, :k_len]


def _attn_bias(rel_bias):
    h = rel_bias.shape[0]
    bias = _toeplitz_bias(rel_bias, ATT_Q, ATT_K)
    q = lax.broadcasted_iota(jnp.int32, (ATT_Q, ATT_K), 0)
    k = lax.broadcasted_iota(jnp.int32, (ATT_Q, ATT_K), 1)
    off = k // CHUNK - q // CHUNK
    in_band = (off >= 0) & (off < BAND_CHUNKS)
    full = jnp.where(in_band[None], bias * LOG2E, NEG_INF)
    full = full.reshape(h // 2, 2, ATT_Q, ATT_K)
    return full.transpose(0, 3, 1, 2).reshape(h // 2, ATT_K, 2 * ATT_Q)


def _attention(qa, ka, vta, rel_bias):
    b, s, _ = qa.shape
    q_rows = min(s, 2048)
    pairs = A_HEADS // 2
    return pl.pallas_call(
        functools.partial(_attn_kernel, q_rows=q_rows),
        grid=(b, pairs, s // q_rows),
        in_specs=[pl.BlockSpec((None, q_rows, LANES), lambda i, p, j: (i, j, p)),
                  pl.BlockSpec((None, s, LANES), lambda i, p, j: (i, 0, p)),
                  pl.BlockSpec((None, s // LANES, LANES, LANES), lambda i, p, j: (i, 0, p, 0)),
                  pl.BlockSpec((None, ATT_K, 2 * ATT_Q), lambda i, p, j: (p, 0, 0))],
        out_specs=pl.BlockSpec((None, q_rows, LANES), lambda i, p, j: (i, j, p)),
        out_shape=jax.ShapeDtypeStruct((b, s, A_WIDTH), _BF16),
        scratch_shapes=[pltpu.VMEM((s + LEFT_ROWS, LANES), _BF16),
                        pltpu.VMEM((ATT_AHEAD + 1, ATT_K, 2 * ATT_Q), _F32),
                        pltpu.VMEM((ATT_VARIANTS, ATT_K, 2 * ATT_Q), _F32)],
        compiler_params=_cparams(3),
        name="attn_a",
    )(qa, ka, vta, _attn_bias(rel_bias))


def _swap_halves(t):
    first = (_lane(t.shape) % HEAD_DIM) < (HEAD_DIM // 2)
    return jnp.where(first, pltpu.roll(t, LANES - HEAD_DIM // 2, 1), pltpu.roll(t, HEAD_DIM // 2, 1))


def _retention_kernel(q_ref, k_ref, kz_ref, v_ref, gate_ref, decay_ref, xi_ref, cd_ref, gn_ref, o_ref,
                      state_ref, *, rows):
    @pl.when(pl.program_id(2) == 0)
    def _():
        state_ref[...] = jnp.zeros_like(state_ref)

    c = RET_CHUNK
    low = _lane((c, LANES)) < HEAD_DIM
    eye = jnp.where(lax.broadcasted_iota(jnp.int32, (LANES, LANES), 0) == _lane((LANES, LANES)),
                    1.0, 0.0).astype(_BF16)
    srow = lax.broadcasted_iota(jnp.int32, (LANES, LANES), 0) < HEAD_DIM
    scol = _lane((LANES, LANES)) < HEAD_DIM
    same_head = srow == scol

    for j in range(rows // c):
        sl = slice(j * c, (j + 1) * c)
        qb = q_ref[sl, :]
        kb = k_ref[sl, :]
        vb = v_ref[sl, :]
        inner_out = []
        for h in range(2):
            qh = jnp.where(low if h == 0 else ~low, qb, jnp.zeros_like(qb))
            inner = _dot_nt(qh, kb) * decay_ref[h]
            inner_out.append(_dot(inner.astype(_BF16), vb))
        state = state_ref[...]
        cross = _dot(qb, state.astype(_BF16)) * xi_ref[...]
        o = jnp.where(low, inner_out[0], inner_out[1]) + cross
        kz = _dot_nt(eye, kz_ref[sl, :]).astype(_BF16)
        state_ref[...] = cd_ref[...] * state + jnp.where(same_head, _dot(kz, vb), 0.0)
        mu = jnp.where(low,
                       jnp.sum(jnp.where(low, o, 0.0), axis=-1, keepdims=True),
                       jnp.sum(jnp.where(low, 0.0, o), axis=-1, keepdims=True)) * (1.0 / HEAD_DIM)
        dlt = o - mu
        d2 = dlt * dlt
        var = jnp.where(low,
                        jnp.sum(jnp.where(low, d2, 0.0), axis=-1, keepdims=True),
                        jnp.sum(jnp.where(low, 0.0, d2), axis=-1, keepdims=True)) * (1.0 / HEAD_DIM)
        y = (dlt * lax.rsqrt(var + EPS)) * gn_ref[...]
        g = gate_ref[sl, :]
        o_ref[sl, :] = ((g * jax.nn.sigmoid(g)) * y).astype(o_ref.dtype)


def _retention_tables():
    c = RET_CHUNK
    log_g = jnp.log(1.0 - jnp.exp2(-5.0 - jnp.arange(B_HEADS, dtype=_F32)))
    idx = jnp.arange(c, dtype=_F32)
    diff = idx[:, None] - idx[None, :]
    decay = jnp.where(diff >= 0, jnp.exp(log_g[:, None, None] * jnp.maximum(diff, 0.0)), 0.0)
    zeta = jnp.exp(log_g[:, None] * (c - 1 - idx))
    xi = jnp.exp(log_g[:, None] * (idx + 1.0))
    cd = jnp.exp(log_g * c)

    def lanes(tab):
        return jnp.repeat(tab.reshape(B_HEADS // 2, 2, c), HEAD_DIM, axis=1).transpose(0, 2, 1)

    cdm = jnp.repeat(cd.reshape(B_HEADS // 2, 2), HEAD_DIM, axis=1)
    cdm = jnp.broadcast_to(cdm[:, :, None], (B_HEADS // 2, LANES, LANES))
    return decay, lanes(zeta), lanes(xi), cdm


def _retention(qkzv, gate, tables, ret_gn_g):
    b, s, _ = qkzv.shape
    rows = min(s, 4096)
    pairs = B_HEADS // 2
    decay, _, xi, cdm = tables
    gn = ret_gn_g.reshape(pairs, 1, LANES)

    def col(off):
        return pl.BlockSpec((None, rows, LANES), lambda i, p, j: (i, j, off * pairs + p))

    return pl.pallas_call(
        functools.partial(_retention_kernel, rows=rows),
        grid=(b, pairs, s // rows),
        in_specs=[col(0), col(1), col(2), col(3), col(0),
                  pl.BlockSpec((2, RET_CHUNK, RET_CHUNK), lambda i, p, j: (p, 0, 0)),
                  pl.BlockSpec((None, RET_CHUNK, LANES), lambda i, p, j: (p, 0, 0)),
                  pl.BlockSpec((None, LANES, LANES), lambda i, p, j: (p, 0, 0)),
                  pl.BlockSpec((None, 1, LANES), lambda i, p, j: (p, 0, 0))],
        out_specs=pl.BlockSpec((None, rows, LANES), lambda i, p, j: (i, j, p)),
        out_shape=jax.ShapeDtypeStruct((b, s, B_WIDTH), _BF16),
        scratch_shapes=[pltpu.VMEM((LANES, LANES), _F32)],
        compiler_params=_cparams(3),
        name="retention_b",
    )(qkzv, qkzv, qkzv, qkzv, gate, decay, xi, cdm, gn)


def _cross_kernel(q_ref, k_ref, vt_ref, o_ref, st_ref, *, rows):
    low = _lane((ATT_Q, LANES)) < HEAD_DIM
    lane_blocks = C_WIDTH // LANES
    tiles = [(j, lb) for j in range(rows // ATT_Q) for lb in range(lane_blocks)]
    ones = jnp.ones((ONES_ROWS, vt_ref.shape[1]), _BF16)

    def scores(i):
        j, lb = tiles[i]
        sl = slice(lb * LANES, (lb + 1) * LANES)
        q = q_ref[j * ATT_Q:(j + 1) * ATT_Q, sl]
        q2 = jnp.concatenate([jnp.where(low, q, jnp.zeros_like(q)), jnp.where(low, jnp.zeros_like(q), q)], axis=0)
        st_ref[i % (ATT_AHEAD + 1)] = _dot_nt(k_ref[:, sl], q2)

    def finish(i):
        j, lb = tiles[i]
        sl = slice(lb * LANES, (lb + 1) * LANES)
        st = st_ref[i % (ATT_AHEAD + 1)]
        p = jnp.exp2(st - jnp.max(st, axis=0, keepdims=True))
        ot = _dot(jnp.concatenate([vt_ref[sl, :], ones], axis=0), p.astype(_BF16))
        inv = 1.0 / ot[LANES:LANES + 1, :]
        out_t = jnp.concatenate([ot[0:HEAD_DIM, 0:ATT_Q] * inv[:, 0:ATT_Q],
                                 ot[HEAD_DIM:LANES, ATT_Q:] * inv[:, ATT_Q:]], axis=0)
        o_ref[j * ATT_Q:(j + 1) * ATT_Q, sl] = out_t.T.astype(o_ref.dtype)

    for i in range(min(ATT_AHEAD, len(tiles))):
        scores(i)
    for i in range(len(tiles)):
        if i + ATT_AHEAD < len(tiles):
            scores(i + ATT_AHEAD)
        finish(i)


def _cross_attention(qc, kc, vtc):
    b, s, _ = qc.shape
    m = kc.shape[1]
    rows = min(s, 2048)
    return pl.pallas_call(
        functools.partial(_cross_kernel, rows=rows),
        grid=(b, s // rows),
        in_specs=[pl.BlockSpec((None, rows, C_WIDTH), lambda i, j: (i, j, 0)),
                  pl.BlockSpec((None, m, C_WIDTH), lambda i, j: (i, 0, 0)),
                  pl.BlockSpec((None, C_WIDTH, m), lambda i, j: (i, 0, 0))],
        out_specs=pl.BlockSpec((None, rows, C_WIDTH), lambda i, j: (i, j, 0)),
        out_shape=jax.ShapeDtypeStruct((b, s, C_WIDTH), _BF16),
        scratch_shapes=[pltpu.VMEM((ATT_AHEAD + 1, m, 2 * ATT_Q), _F32)],
        compiler_params=_cparams(2),
        name="cross_c",
    )(qc, kc, vtc)


def _out_router_kernel(x_ref, a_ref, b_ref, c_ref, wo_ref, g_ref, wr_ref, br_ref,
                       h_ref, hn_ref, rows_ref, cnt_ref, carry_ref):
    @pl.when(pl.program_id(0) == 0)
    def _():
        carry_ref[...] = jnp.zeros_like(carry_ref)

    tm = x_ref.shape[0]
    h = x_ref[...]
    h = h + _dot(a_ref[...], wo_ref[0:A_WIDTH, :])
    h = h + _dot(b_ref[...], wo_ref[A_WIDTH:A_WIDTH + B_WIDTH, :])
    h = h + _dot(c_ref[...], wo_ref[A_WIDTH + B_WIDTH:, :])
    h_ref[...] = h
    ms = jnp.mean(h * h, axis=-1, keepdims=True)
    hn = (h * lax.rsqrt(ms + EPS)) * g_ref[...]
    _pack_rows(hn_ref, hn)
    logits = _dot_nt(wr_ref[...], hn.astype(_BF16))[0:ROUTE_ROWS, :] + br_ref[:, 0:1]
    row = lax.broadcasted_iota(jnp.int32, (ROUTE_ROWS, tm), 0).astype(_F32)
    big = float(ROUTE_ROWS)

    def first_row(mask):
        return jnp.min(jnp.where(mask, row, big), axis=0, keepdims=True)

    gmask = row < N_GROUPS
    gl = jnp.where(gmask, logits, NEG_INF)
    ge = jnp.exp(gl - jnp.max(gl, axis=0, keepdims=True))
    gp = ge / jnp.sum(ge, axis=0, keepdims=True)
    p_group = jnp.max(gp, axis=0, keepdims=True)
    g_sel = first_row(gmask & (gp == p_group))
    lo = ROUTE_LANE0 + g_sel * EXPERTS_PER_GROUP
    emask = (row >= lo) & (row < lo + EXPERTS_PER_GROUP)
    el = jnp.where(emask, logits, NEG_INF)
    ee = jnp.exp(el - jnp.max(el, axis=0, keepdims=True))
    ep = ee / jnp.sum(ee, axis=0, keepdims=True)
    p1 = jnp.max(ep, axis=0, keepdims=True)
    i1 = first_row(emask & (ep == p1))
    ep2 = jnp.where(emask & (row != i1), ep, -1.0)
    p2 = jnp.max(ep2, axis=0, keepdims=True)
    i2 = first_row(ep2 == p2)
    den = p1 + p2
    w1 = p_group * (p1 / den)
    w2 = p_group * (p2 / den)
    hit1 = row == i1
    hit2 = row == i2
    onehot = jnp.where(hit1 | hit2, 1.0, 0.0)
    r_i = lax.broadcasted_iota(jnp.int32, (tm, tm), 0)
    c_i = lax.broadcasted_iota(jnp.int32, (tm, tm), 1)
    earlier = jnp.where(r_i < c_i, 1.0, 0.0).astype(_BF16)
    before = _dot(onehot.astype(_BF16), earlier) + carry_ref[:, 0:1]
    r1 = jnp.sum(jnp.where(hit1, before, 0.0), axis=0, keepdims=True)
    r2 = jnp.sum(jnp.where(hit2, before, 0.0), axis=0, keepdims=True)
    carry_ref[...] = carry_ref[...] + jnp.sum(onehot, axis=1, keepdims=True)
    cnt_ref[...] = carry_ref[...]
    out_row = lax.broadcasted_iota(jnp.int32, (SUBLANES, tm), 0)
    info = jnp.where(out_row == 0, w1, 0.0)
    info = jnp.where(out_row == 1, w2, info)
    info = jnp.where(out_row == 2, i1 - ROUTE_LANE0, info)
    info = jnp.where(out_row == 3, i2 - ROUTE_LANE0, info)
    info = jnp.where(out_row == 4, r1, info)
    info = jnp.where(out_row == 5, r2, info)
    rows_ref[...] = info


def _out_router(x2, oa, ob, oc, w_out, ffn_g, w_rg, b_rg, w_re, b_re):
    t, d = x2.shape
    tm = min(t, 512)
    pad = LANES - N_GROUPS - N_EXPERTS
    wr = jnp.concatenate([w_rg, w_re, jnp.zeros((d, pad), _F32)], axis=1).T.astype(_BF16)
    br = jnp.concatenate([b_rg, b_re, jnp.zeros((ROUTE_ROWS - N_GROUPS - N_EXPERTS,), _F32)])
    br = jnp.broadcast_to(br[:, None], (ROUTE_ROWS, LANES))

    def rows(w):
        return pl.BlockSpec((tm, w), lambda i: (i, 0))

    def whole(r, c):
        return pl.BlockSpec((r, c), lambda i: (0, 0))

    return pl.pallas_call(
        _out_router_kernel,
        grid=(t // tm,),
        in_specs=[rows(d), rows(A_WIDTH), rows(B_WIDTH), rows(C_WIDTH), whole(d, d), whole(1, d),
                  whole(LANES, d), whole(ROUTE_ROWS, LANES)],
        out_specs=[rows(d), pl.BlockSpec((tm * PACK_ROWS, LANES), lambda i: (i, 0)),
                   pl.BlockSpec((SUBLANES, tm), lambda i: (0, i)), whole(ROUTE_ROWS, LANES)],
        out_shape=[jax.ShapeDtypeStruct((t, d), _F32), jax.ShapeDtypeStruct((t * PACK_ROWS, LANES), jnp.uint32),
                   jax.ShapeDtypeStruct((SUBLANES, t), _F32), jax.ShapeDtypeStruct((ROUTE_ROWS, LANES), _F32)],
        scratch_shapes=[pltpu.VMEM((ROUTE_ROWS, LANES), _F32)],
        compiler_params=_cparams(1),
        name="out_router",
    )(x2, oa, ob, oc, w_out.astype(_BF16), ffn_g.reshape(1, d), wr, br)


DISPATCH_TOKENS = 2048
COMBINE_TOKENS = 256


ROW_UNROLL = 8


def _tile_rows(row, count=1, per=SUBLANES):
    start = row * per
    if not isinstance(start, int):
        start = pl.multiple_of(start, per)
    return pl.ds(start, count * per)


def _row_copy(src, s_row, dst, d_row, sem, per=SUBLANES):
    return pltpu.make_async_copy(src.at[_tile_rows(s_row, 1, per)], dst.at[_tile_rows(d_row, 1, per)], sem)


def _dispatch_kernel(pad_start_ref, pad_len_ref, used_ref, dest_ref, hn_ref, xs_ref, zero_ref, sem,
                     pad_sem):
    per = PACK_ROWS
    n = hn_ref.shape[0] // per

    @pl.when(pl.program_id(0) == 0)
    def _():
        zero_ref[...] = jnp.zeros_like(zero_ref)
        n_blocks = xs_ref.shape[0] // (ROW_BLOCK * per)

        def block_copy(blk):
            return pltpu.make_async_copy(zero_ref, xs_ref.at[_tile_rows(blk * ROW_BLOCK, ROW_BLOCK, per)],
                                         pad_sem)

        def put_block(blk, carry):
            block_copy(blk).start()
            return carry

        def done_block(blk, carry):
            block_copy(blk).wait()
            return carry

        lax.fori_loop(used_ref[0], n_blocks, put_block, 0)
        lax.fori_loop(used_ref[0], n_blocks, done_block, 0)
        bits = [1 << k for k in reversed(range(ROW_BLOCK.bit_length() - 1))]

        def tail(e, wait):
            row = pad_start_ref[e]
            for bit in bits:
                on = (pad_len_ref[e] & bit) != 0
                copy = pltpu.make_async_copy(zero_ref.at[_tile_rows(0, bit, per)],
                                             xs_ref.at[_tile_rows(row, bit, per)], pad_sem)

                @pl.when(on)
                def _():
                    copy.wait() if wait else copy.start()

                row = row + jnp.where(on, bit, 0)

        def put_tail(e, carry):
            tail(e, False)
            return carry

        def done_tail(e, carry):
            tail(e, True)
            return carry

        lax.fori_loop(0, N_EXPERTS, put_tail, 0)
        lax.fori_loop(0, N_EXPERTS, done_tail, 0)

    def issue(i, carry):
        for u in range(ROW_UNROLL):
            t = i * ROW_UNROLL + u
            _row_copy(hn_ref, t, xs_ref, dest_ref[2 * t], sem, per).start(priority=0)
            _row_copy(hn_ref, t, xs_ref, dest_ref[2 * t + 1], sem, per).start(priority=1)
        return carry

    lax.fori_loop(0, n // ROW_UNROLL, issue, 0)
    for _ in range(2):
        pltpu.make_async_copy(hn_ref, xs_ref.at[_tile_rows(0, n, per)], sem).wait()


def _dispatch(hn, dest, pad_start, pad_len, n_used, n_rows):
    t = hn.shape[0] // PACK_ROWS
    n = min(t, DISPATCH_TOKENS)
    return pl.pallas_call(
        _dispatch_kernel,
        grid_spec=pltpu.PrefetchScalarGridSpec(
            num_scalar_prefetch=3,
            grid=(t // n,),
            in_specs=[pl.BlockSpec((2 * n,), lambda i, *_: (i,), memory_space=pltpu.SMEM),
                      pl.BlockSpec((n * PACK_ROWS, LANES), lambda i, *_: (i, 0))],
            out_specs=pl.BlockSpec(memory_space=pl.ANY),
            scratch_shapes=[pltpu.VMEM((ROW_BLOCK * PACK_ROWS, LANES), hn.dtype), pltpu.SemaphoreType.DMA,
                            pltpu.SemaphoreType.DMA]),
        out_shape=jax.ShapeDtypeStruct((n_rows * PACK_ROWS, LANES), hn.dtype),
        compiler_params=_cparams(1),
        name="moe_dispatch",
    )(pad_start, pad_len, n_used, dest, hn)


def _expert_kernel(be_ref, run_ref, next_ref, used_ref, x_ref, wg_hbm, wu_hbm, wd_hbm, y_ref,
                   wg_f32, wu_f32, wd_f32, wg_bf, wu_bf, wd_bf, sem):
    i = pl.program_id(0)
    live = i < used_ref[0]
    new_expert = (i == 0) | (be_ref[i] != be_ref[jnp.maximum(i - 1, 0)])
    slot = run_ref[i] % 2

    def fetch(expert, to_slot):
        return [pltpu.make_async_copy(src.at[expert], dst.at[to_slot], sem.at[to_slot, k])
                for k, (src, dst) in enumerate(((wg_hbm, wg_f32), (wu_hbm, wu_f32), (wd_hbm, wd_f32)))]

    @pl.when(live & (i == 0))
    def _():
        for copy in fetch(be_ref[0], 0):
            copy.start()

    @pl.when(live & new_expert)
    def _():
        for copy in fetch(be_ref[i], slot):
            copy.wait()

        @pl.when(next_ref[i] >= 0)
        def _():
            for copy in fetch(next_ref[i], 1 - slot):
                copy.start()

        wg_bf[...] = wg_f32[slot].astype(_BF16)
        wu_bf[...] = wu_f32[slot].astype(_BF16)
        wd_bf[...] = wd_f32[slot].astype(_BF16)

    @pl.when(live)
    def _():
        sub = ROW_BLOCK // EXPERT_SPLIT
        gate_up = {}

        def first(k):
            x = _unpack_rows(x_ref, sub, k * sub)
            gate_up[k] = (_dot(x, wg_bf[...]), _dot(x, wu_bf[...]))

        def second(k):
            gate, up = gate_up.pop(k)
            act = (gate * jax.nn.sigmoid(gate)) * up
            _rows_to_tiles(y_ref, _dot(act.astype(_BF16), wd_bf[...]), k * sub)

        for k in range(min(EXPERT_AHEAD, EXPERT_SPLIT)):
            first(k)
        for k in range(EXPERT_SPLIT):
            if k + EXPERT_AHEAD < EXPERT_SPLIT:
                first(k + EXPERT_AHEAD)
            second(k)

    @pl.when(i >= used_ref[0])
    def _():
        y_ref[...] = jnp.zeros_like(y_ref)


def _experts(xs, blocks, w_gate, w_up, w_down):
    n_rows, d = xs.shape[0] // PACK_ROWS, D_MODEL
    n_blocks = n_rows // ROW_BLOCK
    tile_block = (ROW_BLOCK * SUBLANES, LANES)
    hbm = pl.BlockSpec(memory_space=pl.ANY)

    return pl.pallas_call(
        _expert_kernel,
        grid_spec=pltpu.PrefetchScalarGridSpec(
            num_scalar_prefetch=4,
            grid=(n_blocks,),
            in_specs=[pl.BlockSpec((ROW_BLOCK * PACK_ROWS, LANES),
                                   lambda i, be, run, nxt, used: (jnp.minimum(i, used[0] - 1), 0)),
                      hbm, hbm, hbm],
            out_specs=pl.BlockSpec(tile_block, lambda i, *_: (i, 0)),
            scratch_shapes=[pltpu.VMEM((2, d, D_EXPERT), _F32), pltpu.VMEM((2, d, D_EXPERT), _F32),
                            pltpu.VMEM((2, D_EXPERT, d), _F32),
                            pltpu.VMEM((d, D_EXPERT), _BF16), pltpu.VMEM((d, D_EXPERT), _BF16),
                            pltpu.VMEM((D_EXPERT, d), _BF16), pltpu.SemaphoreType.DMA((2, 3))]),
        out_shape=jax.ShapeDtypeStruct((n_rows * SUBLANES, LANES), _F32),
        compiler_params=_cparams(1),
        name="moe_experts",
    )(*blocks, xs, w_gate, w_up, w_down)


def _combine_kernel(dest_ref, next_ref, h_ref, rows_ref, ys_ref, o_ref, buf_ref, sem):
    n = h_ref.shape[0]
    step = pl.program_id(0)
    slot = step % 2

    def gather(idx_ref, to_slot):
        def issue(i, carry):
            for u in range(ROW_UNROLL):
                t = i * ROW_UNROLL + u
                _row_copy(ys_ref, idx_ref[2 * t], buf_ref.at[to_slot, 0], t,
                          sem.at[to_slot]).start(priority=0)
                _row_copy(ys_ref, idx_ref[2 * t + 1], buf_ref.at[to_slot, 1], t,
                          sem.at[to_slot]).start(priority=1)
            return carry

        lax.fori_loop(0, n // ROW_UNROLL, issue, 0)

    @pl.when(step == 0)
    def _():
        gather(dest_ref, 0)

    @pl.when(step + 1 < pl.num_programs(0))
    def _():
        gather(next_ref, 1 - slot)

    for k in range(2):
        pltpu.make_async_copy(ys_ref.at[_tile_rows(0, n)], buf_ref.at[slot, k], sem.at[slot]).wait()
    info = jnp.concatenate([rows_ref[...], jnp.zeros((LANES - SUBLANES, n), _F32)], axis=0).T
    w0 = info[:, 0:1]
    w1 = info[:, 1:2]
    for s in range(SUBLANES):
        sl = slice(s * LANES, (s + 1) * LANES)
        moe = w0 * _tile_block(buf_ref.at[slot, 0], s, n) + w1 * _tile_block(buf_ref.at[slot, 1], s, n)
        o_ref[:, sl] = h_ref[:, sl] + moe


def _combine(h, route_rows, ys, dest):
    t, d = h.shape
    n = min(t, COMBINE_TOKENS)
    steps = t // n
    return pl.pallas_call(
        _combine_kernel,
        grid=(steps,),
        in_specs=[pl.BlockSpec((2 * n,), lambda i: (i,), memory_space=pltpu.SMEM),
                  pl.BlockSpec((2 * n,), lambda i: (jnp.minimum(i + 1, steps - 1),),
                               memory_space=pltpu.SMEM),
                  pl.BlockSpec((n, d), lambda i: (i, 0)),
                  pl.BlockSpec((SUBLANES, n), lambda i: (0, i)),
                  pl.BlockSpec(memory_space=pl.ANY)],
        out_specs=pl.BlockSpec((n, d), lambda i: (i, 0)),
        out_shape=jax.ShapeDtypeStruct((t, d), _F32),
        scratch_shapes=[pltpu.VMEM((2, 2, n * SUBLANES, LANES), _F32), pltpu.SemaphoreType.DMA((2,))],
        compiler_params=_cparams(1),
        name="moe_combine",
    )(dest, dest, h, route_rows, ys)


def _moe_layout(route_rows, counts, t):
    counts = counts[ROUTE_LANE0:ROUTE_LANE0 + N_EXPERTS, 0].astype(jnp.int32)
    padded = (counts + ROW_BLOCK - 1) // ROW_BLOCK * ROW_BLOCK
    pends = jnp.cumsum(padded)
    pstarts = pends - padded
    eid = route_rows[2:4].astype(jnp.int32)
    rank = route_rows[4:6].astype(jnp.int32)
    experts = jnp.arange(N_EXPERTS, dtype=jnp.int32)
    start_of = jnp.sum(jnp.where(eid[:, :, None] == experts, pstarts, 0), axis=-1)
    dest = (start_of + rank).T.reshape(-1)
    n_blocks = -(-2 * t // ROW_BLOCK) + N_EXPERTS
    first_row = jnp.arange(n_blocks, dtype=jnp.int32) * ROW_BLOCK
    block_e = jnp.minimum(jnp.sum((pends[None, :] <= first_row[:, None]).astype(jnp.int32), axis=1),
                          N_EXPERTS - 1)
    n_used = (pends[-1:] // ROW_BLOCK).astype(jnp.int32)
    changed = jnp.concatenate([jnp.zeros((1,), jnp.int32), (block_e[1:] != block_e[:-1]).astype(jnp.int32)])
    block_run = jnp.cumsum(changed)
    later = (counts[None, :] > 0) & (experts[None, :] > experts[:, None])
    next_expert = jnp.min(jnp.where(later, experts[None, :], N_EXPERTS), axis=1)
    next_expert = jnp.where(next_expert < N_EXPERTS, next_expert, -1)
    block_next = jnp.sum(jnp.where(block_e[:, None] == experts[None, :], next_expert[None, :], 0), axis=1)
    blocks = (block_e, block_run.astype(jnp.int32), block_next.astype(jnp.int32), n_used)
    return dest, blocks, pstarts + counts, padded - counts, n_blocks * ROW_BLOCK


def kernel(x, mem, positions, mix_norm_g, w_in, qn_a, kn_a, rel_bias, ret_gn_g, mem_norm_g, w_mem_kv,
           qn_c, kn_c, w_out, ffn_norm_g, w_router_group, b_router_group, w_router_expert,
           b_router_expert, w_gate, w_up, w_down):
    b, s, d = x.shape
    t = b * s
    x2 = x.reshape(t, d)
    kc, vc = _mem_kv(mem, mem_norm_g, w_mem_kv, kn_c)
    tables = _retention_tables()
    qa, ka, vta, qkzv, gate, qc = _in_proj(x, positions, tables[1], mix_norm_g, w_in, qn_a, kn_a, qn_c)
    out_a = _attention(qa, ka, vta, rel_bias)
    out_b = _retention(qkzv, gate, tables, ret_gn_g)
    out_c = _cross_attention(qc, kc, vc)
    h, hn, route_rows, counts = _out_router(
        x2, out_a.reshape(t, A_WIDTH), out_b.reshape(t, B_WIDTH), out_c.reshape(t, C_WIDTH),
        w_out, ffn_norm_g, w_router_group, b_router_group, w_router_expert, b_router_expert)
    dest, blocks, pad_start, pad_len, n_rows = _moe_layout(route_rows, counts, t)
    xs = _dispatch(hn, dest, pad_start, pad_len, blocks[-1], n_rows)
    ys = _experts(xs, blocks, w_gate, w_up, w_down)
    return _combine(h, route_rows, ys, dest).reshape(b, s, d)
```

```python
import functools

import jax
import jax.numpy as jnp
from jax import lax
from jax.experimental import pallas as pl
from jax.experimental.pallas import tpu as pltpu

D_MODEL = 1024
CHUNK = 64
HEAD_DIM = 64
A_HEADS = 8
B_HEADS = 4
C_HEADS = 4
A_WIDTH = A_HEADS * HEAD_DIM
B_WIDTH = B_HEADS * HEAD_DIM
C_WIDTH = C_HEADS * HEAD_DIM
IN_COLS = 3 * A_WIDTH + 4 * B_WIDTH + C_WIDTH
LEFT_CHUNKS = 8
BAND_CHUNKS = LEFT_CHUNKS + 1
MAX_REL_DIST = 128
ROPE_BASE = 10000.0
N_GROUPS = 4
EXPERTS_PER_GROUP = 8
N_EXPERTS = N_GROUPS * EXPERTS_PER_GROUP
D_EXPERT = D_MODEL // 2
EPS = 1e-6
NEG_INF = -1e30
LOG2E = 1.4426950408889634

LANES = 128
SUBLANES = 8
assert D_MODEL == SUBLANES * LANES
PACK_ROWS = SUBLANES // 2
LEFT_ROWS = LEFT_CHUNKS * CHUNK
ATT_Q = 2 * CHUNK
ATT_K = ATT_Q + LEFT_ROWS
ATT_VARIANTS = LEFT_ROWS // ATT_Q + 1
ONES_ROWS = 16
ATT_AHEAD = 3
RET_CHUNK = 256
ROW_BLOCK = 512
EXPERT_SPLIT = 2
EXPERT_AHEAD = 2
ROUTE_LANE0 = N_GROUPS
ROUTE_ROWS = 64
VMEM_LIMIT = 48 * 1024 * 1024

_F32 = jnp.float32
_BF16 = jnp.bfloat16


def _cparams(n_axes):
    return pltpu.CompilerParams(dimension_semantics=("arbitrary",) * n_axes,
                                vmem_limit_bytes=VMEM_LIMIT)


def _dot(a, b):
    return jnp.dot(a, b, preferred_element_type=_F32)


def _dot_nt(a, b):
    return lax.dot_general(a, b, (((1,), (1,)), ((), ())), preferred_element_type=_F32)


def _lane(shape):
    return lax.broadcasted_iota(jnp.int32, shape, len(shape) - 1)


def _pair_rms(t, gain):
    low = _lane(t.shape) < HEAD_DIM
    t2 = t * t
    ms0 = jnp.sum(jnp.where(low, t2, 0.0), axis=-1, keepdims=True) * (1.0 / HEAD_DIM)
    ms1 = jnp.sum(jnp.where(low, 0.0, t2), axis=-1, keepdims=True) * (1.0 / HEAD_DIM)
    r = jnp.where(low, lax.rsqrt(ms0 + EPS), lax.rsqrt(ms1 + EPS))
    return (t * r) * gain


def _rows_to_tiles(ref, val, row0=0):
    n = val.shape[0]
    for s in range(SUBLANES):
        ref[pl.ds(row0 * SUBLANES + s, n, stride=SUBLANES), :] = val[:, s * LANES:(s + 1) * LANES]


def _tile_block(ref, s, n, row0=0):
    return ref[pl.ds(row0 * SUBLANES + s, n, stride=SUBLANES), :]


def _pack_rows(ref, val, row0=0):
    n = val.shape[0]
    for s in range(PACK_ROWS):
        lo = val[:, (2 * s) * LANES:(2 * s + 1) * LANES].astype(_BF16).astype(_F32)
        hi = val[:, (2 * s + 1) * LANES:(2 * s + 2) * LANES].astype(_BF16).astype(_F32)
        word = (lax.bitcast_convert_type(lo, jnp.uint32) >> 16) | (
            lax.bitcast_convert_type(hi, jnp.uint32) & jnp.uint32(0xFFFF0000))
        ref[pl.ds(row0 * PACK_ROWS + s, n, stride=PACK_ROWS), :] = word


def _unpack_rows(ref, n, row0=0):
    parts = []
    for s in range(PACK_ROWS):
        word = ref[pl.ds(row0 * PACK_ROWS + s, n, stride=PACK_ROWS), :]
        parts.append(lax.bitcast_convert_type(word << 16, _F32))
        parts.append(lax.bitcast_convert_type(word & jnp.uint32(0xFFFF0000), _F32))
    return jnp.concatenate(parts, axis=-1).astype(_BF16)


ROPE_HALF = HEAD_DIM // 2
ROPE_PACK = LANES // ROPE_HALF


def _rope_tables(pos_ref, inv_ref, cos_ref, sin_ref):
    ang = pos_ref[...].astype(_F32) * inv_ref[...]
    rows = ang.shape[0]
    lane = _lane(ang.shape)
    sign = jnp.where((lane % HEAD_DIM) < ROPE_HALF, -1.0, 1.0)
    for out_ref, val in ((cos_ref, jnp.cos(ang)), (sin_ref, jnp.sin(ang))):
        for j in range(ROPE_PACK):
            seg = jnp.where(lane // ROPE_HALF == j, val, 0.0)
            full = seg
            for k in range(1, ROPE_PACK):
                full = full + pltpu.roll(seg, k * ROPE_HALF, 1)
            if out_ref is sin_ref:
                full = full * sign
            out_ref[pl.ds(j, rows, stride=ROPE_PACK), :] = full


def _rope_inputs(positions):
    b, s = positions.shape
    inv = ROPE_BASE ** (-jnp.arange(ROPE_HALF, dtype=_F32) / ROPE_HALF)
    inv128 = jnp.tile(inv, ROPE_PACK).reshape(1, LANES)
    pos = jnp.repeat(positions.reshape(b, s // ROPE_PACK, ROPE_PACK), ROPE_HALF, axis=2)
    return pos, inv128


def _mem_kv_kernel(mem_ref, g_ref, w_ref, kn_ref, k_ref, v_ref):
    m = mem_ref[...]
    ms = jnp.mean(m * m, axis=-1, keepdims=True)
    mn = (m * lax.rsqrt(ms + EPS)) * g_ref[...]
    kv = _dot(mn.astype(_BF16), w_ref[...])
    for j in range(C_WIDTH // LANES):
        sl = slice(j * LANES, (j + 1) * LANES)
        k_ref[:, sl] = _pair_rms(kv[:, sl], kn_ref[...]).astype(_BF16)
    v_ref[...] = kv[:, C_WIDTH:].T.astype(_BF16)


def _mem_kv(mem, mem_norm_g, w_mem_kv, kn_c):
    b, m, d = mem.shape
    kn = jnp.tile(kn_c, 2).reshape(1, LANES)
    return pl.pallas_call(
        _mem_kv_kernel,
        grid=(b,),
        in_specs=[pl.BlockSpec((None, m, d), lambda i: (i, 0, 0)),
                  pl.BlockSpec((1, d), lambda i: (0, 0)),
                  pl.BlockSpec((d, 2 * C_WIDTH), lambda i: (0, 0)),
                  pl.BlockSpec((1, LANES), lambda i: (0, 0))],
        out_specs=[pl.BlockSpec((None, m, C_WIDTH), lambda i: (i, 0, 0)),
                   pl.BlockSpec((None, C_WIDTH, m), lambda i: (i, 0, 0))],
        out_shape=[jax.ShapeDtypeStruct((b, m, C_WIDTH), _BF16),
                   jax.ShapeDtypeStruct((b, C_WIDTH, m), _BF16)],
        compiler_params=_cparams(1),
        name="mem_kv",
    )(mem, mem_norm_g.reshape(1, d), w_mem_kv.astype(_BF16), kn)


def _in_proj_kernel(x_ref, pos_ref, inv_ref, g_ref, wq_ref, wk_ref, wvt_ref, wr_ref, wc_ref, qn_ref, kn_ref,
                    cn_ref, zeta_ref, qk_ref, vt_ref, ret_ref, gate_ref, xn_ref, acc_ref, accb_ref, cos_ref,
                    sin_ref):
    x = x_ref[...]
    ms = jnp.mean(x * x, axis=-1, keepdims=True)
    xn_ref[...] = ((x * lax.rsqrt(ms + EPS)) * g_ref[...]).astype(_BF16)

    def normed(slot, col0, width, gain_ref):
        for blk in range(width // LANES):
            sl = slice(blk * LANES, (blk + 1) * LANES)
            qk_ref[:, col0 + blk * LANES:col0 + (blk + 1) * LANES] = _pair_rms(
                acc_ref[slot, :, sl], gain_ref[...]).astype(_BF16)

    acc_ref[0] = _dot(xn_ref[...], wq_ref[...])
    acc_ref[1] = _dot(xn_ref[...], wk_ref[...])
    _rope_tables(pos_ref, inv_ref, cos_ref, sin_ref)
    normed(0, 0, A_WIDTH, qn_ref)
    acc_ref[0] = _dot_nt(wvt_ref[...], xn_ref[...])
    normed(1, A_WIDTH, A_WIDTH, kn_ref)
    accb_ref[...] = _dot(xn_ref[...], wr_ref[...])
    for blk in range(vt_ref.shape[0]):
        vt_ref[blk] = acc_ref[0, :, blk * LANES:(blk + 1) * LANES].astype(_BF16)
    acc_ref[1, :, 0:C_WIDTH] = _dot(xn_ref[...], wc_ref[...])
    cos, sin = cos_ref[...], sin_ref[...]
    chunks = x_ref.shape[0] // RET_CHUNK
    for p in range(B_WIDTH // LANES):
        sl = slice(p * LANES, (p + 1) * LANES)
        q = accb_ref[:, sl]
        k = accb_ref[:, B_WIDTH + p * LANES:B_WIDTH + (p + 1) * LANES]
        kr = (k * cos + _swap_halves(k) * sin) * (HEAD_DIM ** -0.5)
        ret_ref[:, sl] = (q * cos + _swap_halves(q) * sin).astype(_BF16)
        ret_ref[:, B_WIDTH + p * LANES:B_WIDTH + (p + 1) * LANES] = kr.astype(_BF16)
        ret_ref[:, 2 * B_WIDTH + p * LANES:2 * B_WIDTH + (p + 1) * LANES] = (
            kr * jnp.concatenate([zeta_ref[p]] * chunks, axis=0)).astype(_BF16)
        ret_ref[:, 3 * B_WIDTH + p * LANES:3 * B_WIDTH + (p + 1) * LANES] = accb_ref[
            :, 2 * B_WIDTH + p * LANES:2 * B_WIDTH + (p + 1) * LANES].astype(_BF16)
        gate_ref[:, sl] = accb_ref[:, 3 * B_WIDTH + p * LANES:3 * B_WIDTH + (p + 1) * LANES]
    normed(1, 2 * A_WIDTH, C_WIDTH, cn_ref)


def _in_proj(x3, positions, zeta, g, w_in, qn_a, kn_a, qn_c):
    b, s, d = x3.shape
    tm = min(s, 512)
    assert tm == A_WIDTH
    assert tm % RET_CHUNK == 0
    pos, inv128 = _rope_inputs(positions)
    w = w_in.astype(_BF16)
    cuts = [0, A_WIDTH, 2 * A_WIDTH, 3 * A_WIDTH, 3 * A_WIDTH + 4 * B_WIDTH, IN_COLS]
    wq, wk, wv, wr, wc = (w[:, lo:hi] for lo, hi in zip(cuts[:-1], cuts[1:]))
    scale = HEAD_DIM ** -0.5 * LOG2E
    gains = [(jnp.tile(gn, 2) * sc).reshape(1, LANES) for gn, sc in ((qn_a, scale), (kn_a, 1.0), (qn_c, scale))]

    def whole(arr):
        return pl.BlockSpec(arr.shape, lambda i, j: (0,) * arr.ndim)

    def rows(width):
        return pl.BlockSpec((None, tm, width), lambda i, j: (i, j, 0))

    consts = [inv128, g.reshape(1, d), wq, wk, wv.T, wr, wc] + gains + [zeta]
    return pl.pallas_call(
        _in_proj_kernel,
        grid=(b, s // tm),
        in_specs=[rows(d), pl.BlockSpec((None, tm // ROPE_PACK, LANES), lambda i, j: (i, j, 0))]
        + [whole(c) for c in consts],
        out_specs=[rows(2 * A_WIDTH + C_WIDTH),
                   pl.BlockSpec((None, tm // LANES, A_WIDTH, LANES), lambda i, j: (i, j, 0, 0)),
                   rows(4 * B_WIDTH), rows(B_WIDTH)],
        out_shape=[jax.ShapeDtypeStruct((b, s, 2 * A_WIDTH + C_WIDTH), _BF16),
                   jax.ShapeDtypeStruct((b, s // LANES, A_WIDTH, LANES), _BF16),
                   jax.ShapeDtypeStruct((b, s, 4 * B_WIDTH), _BF16), jax.ShapeDtypeStruct((b, s, B_WIDTH), _F32)],
        scratch_shapes=[pltpu.VMEM((tm, d), _BF16), pltpu.VMEM((2, tm, A_WIDTH), _F32),
                        pltpu.VMEM((tm, 4 * B_WIDTH), _F32), pltpu.VMEM((tm, LANES), _F32),
                        pltpu.VMEM((tm, LANES), _F32)],
        compiler_params=_cparams(2),
        name="in_proj",
    )(x3, pos, *consts)


def _attn_kernel(q_ref, k_ref, vt_ref, bias_ref, o_ref, kp_ref, st_ref, var_ref, *, q_rows):
    qs = pl.program_id(2)
    s = k_ref.shape[0]
    fill_rows = min(s, 1024)
    left_blocks = LEFT_ROWS // LANES

    @pl.when(qs == 0)
    def _():
        kp_ref[0:LEFT_ROWS, :] = jnp.zeros((LEFT_ROWS, LANES), _BF16)

        def fill(i, carry):
            r = pl.multiple_of(i * fill_rows, fill_rows)
            kp_ref[pl.ds(LEFT_ROWS + r, fill_rows), :] = k_ref[pl.ds(r, fill_rows), :]
            return carry

        lax.fori_loop(0, s // fill_rows, fill, 0)
        key = lax.broadcasted_iota(jnp.int32, (ATT_K, 2 * ATT_Q), 0)
        for v in range(ATT_VARIANTS):
            var_ref[v] = jnp.where(key >= LEFT_ROWS - ATT_Q * v, bias_ref[...], NEG_INF)

    low = _lane((ATT_Q, LANES)) < HEAD_DIM
    ones = jnp.ones((ONES_ROWS, ATT_K), _BF16)
    tiles_per_step = q_rows // ATT_Q

    def scores(j):
        cp = qs * tiles_per_step + j
        q = q_ref[j * ATT_Q:(j + 1) * ATT_Q, :]
        q2 = jnp.concatenate([jnp.where(low, q, jnp.zeros_like(q)), jnp.where(low, jnp.zeros_like(q), q)], axis=0)
        kb = kp_ref[pl.ds(pl.multiple_of(cp * ATT_Q, ATT_Q), ATT_K), :]
        st_ref[j % (ATT_AHEAD + 1)] = _dot_nt(kb, q2) + var_ref[jnp.minimum(cp, ATT_VARIANTS - 1)]

    def finish(j):
        cp = qs * tiles_per_step + j
        st = st_ref[j % (ATT_AHEAD + 1)]
        m = jnp.max(st, axis=0, keepdims=True)
        p = jnp.exp2(st - m)
        vt = jnp.concatenate([vt_ref[jnp.maximum(cp + kb_i - left_blocks, 0)] for kb_i in range(ATT_K // LANES)],
                             axis=1)
        ot = _dot(jnp.concatenate([vt, ones], axis=0), p.astype(_BF16))
        inv = 1.0 / ot[LANES:LANES + 1, :]
        out_t = jnp.concatenate([ot[0:HEAD_DIM, 0:ATT_Q] * inv[:, 0:ATT_Q],
                                 ot[HEAD_DIM:LANES, ATT_Q:] * inv[:, ATT_Q:]], axis=0)
        o_ref[j * ATT_Q:(j + 1) * ATT_Q, :] = out_t.T.astype(o_ref.dtype)

    for j in range(min(ATT_AHEAD, tiles_per_step)):
        scores(j)
    for j in range(tiles_per_step):
        if j + ATT_AHEAD < tiles_per_step:
            scores(j + ATT_AHEAD)
        finish(j)


def _toeplitz_bias(rel_bias, q_len, k_len):
    h, table = rel_bias.shape
    n_diag = q_len + k_len - 1
    flat_lo = k_len - 1 - LEFT_ROWS - (CHUNK - 1)
    flat_hi = n_diag - flat_lo - table
    rev = jnp.concatenate([jnp.broadcast_to(rel_bias[:, -1:], (h, flat_hi)), rel_bias[:, ::-1],
                           jnp.broadcast_to(rel_bias[:, :1], (h, flat_lo))], axis=1).astype(_F32)
    flat = jnp.tile(rev, (1, q_len + 1))
    pitch = n_diag - 1
    skew = flat[:, q_len - 1:q_len - 1 + q_len * pitch].reshape(h, q_len, pitch)
    return skew[:, :, :k_len]


def _attn_bias(rel_bias):
    h = rel_bias.shape[0]
    bias = _toeplitz_bias(rel_bias, ATT_Q, ATT_K)
    q = lax.broadcasted_iota(jnp.int32, (ATT_Q, ATT_K), 0)
    k = lax.broadcasted_iota(jnp.int32, (ATT_Q, ATT_K), 1)
    off = k // CHUNK - q // CHUNK
    in_band = (off >= 0) & (off < BAND_CHUNKS)
    full = jnp.where(in_band[None], bias * LOG2E, NEG_INF)
    full = full.reshape(h // 2, 2, ATT_Q, ATT_K)
    return full.transpose(0, 3, 1, 2).reshape(h // 2, ATT_K, 2 * ATT_Q)


def _attention(qk, vta, rel_bias):
    b, s, _ = qk.shape
    q_rows = min(s, 2048)
    pairs = A_HEADS // 2
    return pl.pallas_call(
        functools.partial(_attn_kernel, q_rows=q_rows),
        grid=(b, pairs, s // q_rows),
        in_specs=[pl.BlockSpec((None, q_rows, LANES), lambda i, p, j: (i, j, p)),
                  pl.BlockSpec((None, s, LANES), lambda i, p, j: (i, 0, pairs + p)),
                  pl.BlockSpec((None, s // LANES, LANES, LANES), lambda i, p, j: (i, 0, p, 0)),
                  pl.BlockSpec((None, ATT_K, 2 * ATT_Q), lambda i, p, j: (p, 0, 0))],
        out_specs=pl.BlockSpec((None, q_rows, LANES), lambda i, p, j: (i, j, p)),
        out_shape=jax.ShapeDtypeStruct((b, s, A_WIDTH), _BF16),
        scratch_shapes=[pltpu.VMEM((s + LEFT_ROWS, LANES), _BF16),
                        pltpu.VMEM((ATT_AHEAD + 1, ATT_K, 2 * ATT_Q), _F32),
                        pltpu.VMEM((ATT_VARIANTS, ATT_K, 2 * ATT_Q), _F32)],
        compiler_params=_cparams(3),
        name="attn_a",
    )(qk, qk, vta, _attn_bias(rel_bias))


def _swap_halves(t):
    first = (_lane(t.shape) % HEAD_DIM) < (HEAD_DIM // 2)
    return jnp.where(first, pltpu.roll(t, LANES - HEAD_DIM // 2, 1), pltpu.roll(t, HEAD_DIM // 2, 1))


def _retention_kernel(q_ref, k_ref, kz_ref, v_ref, gate_ref, decay_ref, xi_ref, cd_ref, gn_ref, o_ref,
                      state_ref, *, rows):
    @pl.when(pl.program_id(2) == 0)
    def _():
        state_ref[...] = jnp.zeros_like(state_ref)

    c = RET_CHUNK
    low = _lane((c, LANES)) < HEAD_DIM
    eye = jnp.where(lax.broadcasted_iota(jnp.int32, (LANES, LANES), 0) == _lane((LANES, LANES)),
                    1.0, 0.0).astype(_BF16)
    srow = lax.broadcasted_iota(jnp.int32, (LANES, LANES), 0) < HEAD_DIM
    scol = _lane((LANES, LANES)) < HEAD_DIM
    same_head = srow == scol

    for j in range(rows // c):
        sl = slice(j * c, (j + 1) * c)
        qb = q_ref[sl, :]
        kb = k_ref[sl, :]
        vb = v_ref[sl, :]
        inner_out = []
        for h in range(2):
            qh = jnp.where(low if h == 0 else ~low, qb, jnp.zeros_like(qb))
            inner = _dot_nt(qh, kb) * decay_ref[h]
            inner_out.append(_dot(inner.astype(_BF16), vb))
        state = state_ref[...]
        cross = _dot(qb, state.astype(_BF16)) * xi_ref[...]
        o = jnp.where(low, inner_out[0], inner_out[1]) + cross
        kz = _dot_nt(eye, kz_ref[sl, :]).astype(_BF16)
        state_ref[...] = cd_ref[...] * state + jnp.where(same_head, _dot(kz, vb), 0.0)
        mu = jnp.where(low,
                       jnp.sum(jnp.where(low, o, 0.0), axis=-1, keepdims=True),
                       jnp.sum(jnp.where(low, 0.0, o), axis=-1, keepdims=True)) * (1.0 / HEAD_DIM)
        dlt = o - mu
        d2 = dlt * dlt
        var = jnp.where(low,
                        jnp.sum(jnp.where(low, d2, 0.0), axis=-1, keepdims=True),
                        jnp.sum(jnp.where(low, 0.0, d2), axis=-1, keepdims=True)) * (1.0 / HEAD_DIM)
        y = (dlt * lax.rsqrt(var + EPS)) * gn_ref[...]
        g = gate_ref[sl, :]
        o_ref[sl, :] = ((g * jax.nn.sigmoid(g)) * y).astype(o_ref.dtype)


def _retention_tables():
    c = RET_CHUNK
    log_g = jnp.log(1.0 - jnp.exp2(-5.0 - jnp.arange(B_HEADS, dtype=_F32)))
    idx = jnp.arange(c, dtype=_F32)
    diff = idx[:, None] - idx[None, :]
    decay = jnp.where(diff >= 0, jnp.exp(log_g[:, None, None] * jnp.maximum(diff, 0.0)), 0.0)
    zeta = jnp.exp(log_g[:, None] * (c - 1 - idx))
    xi = jnp.exp(log_g[:, None] * (idx + 1.0))
    cd = jnp.exp(log_g * c)

    def lanes(tab):
        return jnp.repeat(tab.reshape(B_HEADS // 2, 2, c), HEAD_DIM, axis=1).transpose(0, 2, 1)

    cdm = jnp.repeat(cd.reshape(B_HEADS // 2, 2), HEAD_DIM, axis=1)
    cdm = jnp.broadcast_to(cdm[:, :, None], (B_HEADS // 2, LANES, LANES))
    return decay, lanes(zeta), lanes(xi), cdm


def _retention(qkzv, gate, tables, ret_gn_g):
    b, s, _ = qkzv.shape
    rows = min(s, 4096)
    pairs = B_HEADS // 2
    decay, _, xi, cdm = tables
    gn = ret_gn_g.reshape(pairs, 1, LANES)

    def col(off):
        return pl.BlockSpec((None, rows, LANES), lambda i, p, j: (i, j, off * pairs + p))

    return pl.pallas_call(
        functools.partial(_retention_kernel, rows=rows),
        grid=(b, pairs, s // rows),
        in_specs=[col(0), col(1), col(2), col(3), col(0),
                  pl.BlockSpec((2, RET_CHUNK, RET_CHUNK), lambda i, p, j: (p, 0, 0)),
                  pl.BlockSpec((None, RET_CHUNK, LANES), lambda i, p, j: (p, 0, 0)),
                  pl.BlockSpec((None, LANES, LANES), lambda i, p, j: (p, 0, 0)),
                  pl.BlockSpec((None, 1, LANES), lambda i, p, j: (p, 0, 0))],
        out_specs=pl.BlockSpec((None, rows, LANES), lambda i, p, j: (i, j, p)),
        out_shape=jax.ShapeDtypeStruct((b, s, B_WIDTH), _BF16),
        scratch_shapes=[pltpu.VMEM((LANES, LANES), _F32)],
        compiler_params=_cparams(3),
        name="retention_b",
    )(qkzv, qkzv, qkzv, qkzv, gate, decay, xi, cdm, gn)


def _cross_kernel(q_ref, k_ref, vt_ref, o_ref, st_ref, *, rows):
    low = _lane((ATT_Q, LANES)) < HEAD_DIM
    lane_blocks = C_WIDTH // LANES
    tiles = [(j, lb) for j in range(rows // ATT_Q) for lb in range(lane_blocks)]
    ones = jnp.ones((ONES_ROWS, vt_ref.shape[1]), _BF16)

    def scores(i):
        j, lb = tiles[i]
        sl = slice(lb * LANES, (lb + 1) * LANES)
        q = q_ref[j * ATT_Q:(j + 1) * ATT_Q, sl]
        q2 = jnp.concatenate([jnp.where(low, q, jnp.zeros_like(q)), jnp.where(low, jnp.zeros_like(q), q)], axis=0)
        st_ref[i % (ATT_AHEAD + 1)] = _dot_nt(k_ref[:, sl], q2)

    def finish(i):
        j, lb = tiles[i]
        sl = slice(lb * LANES, (lb + 1) * LANES)
        st = st_ref[i % (ATT_AHEAD + 1)]
        p = jnp.exp2(st - jnp.max(st, axis=0, keepdims=True))
        ot = _dot(jnp.concatenate([vt_ref[sl, :], ones], axis=0), p.astype(_BF16))
        inv = 1.0 / ot[LANES:LANES + 1, :]
        out_t = jnp.concatenate([ot[0:HEAD_DIM, 0:ATT_Q] * inv[:, 0:ATT_Q],
                                 ot[HEAD_DIM:LANES, ATT_Q:] * inv[:, ATT_Q:]], axis=0)
        o_ref[j * ATT_Q:(j + 1) * ATT_Q, sl] = out_t.T.astype(o_ref.dtype)

    for i in range(min(ATT_AHEAD, len(tiles))):
        scores(i)
    for i in range(len(tiles)):
        if i + ATT_AHEAD < len(tiles):
            scores(i + ATT_AHEAD)
        finish(i)


def _cross_attention(qk, kc, vtc):
    b, s, width = qk.shape
    m = kc.shape[1]
    rows = min(s, 2048)
    qcol = width // C_WIDTH - 1
    return pl.pallas_call(
        functools.partial(_cross_kernel, rows=rows),
        grid=(b, s // rows),
        in_specs=[pl.BlockSpec((None, rows, C_WIDTH), lambda i, j: (i, j, qcol)),
                  pl.BlockSpec((None, m, C_WIDTH), lambda i, j: (i, 0, 0)),
                  pl.BlockSpec((None, C_WIDTH, m), lambda i, j: (i, 0, 0))],
        out_specs=pl.BlockSpec((None, rows, C_WIDTH), lambda i, j: (i, j, 0)),
        out_shape=jax.ShapeDtypeStruct((b, s, C_WIDTH), _BF16),
        scratch_shapes=[pltpu.VMEM((ATT_AHEAD + 1, m, 2 * ATT_Q), _F32)],
        compiler_params=_cparams(2),
        name="cross_c",
    )(qk, kc, vtc)


def _out_router_kernel(x_ref, a_ref, b_ref, c_ref, wo_ref, g_ref, wr_ref, br_ref,
                       h_ref, hn_ref, rows_ref, cnt_ref, carry_ref):
    @pl.when(pl.program_id(0) == 0)
    def _():
        carry_ref[...] = jnp.zeros_like(carry_ref)

    tm = x_ref.shape[0]
    h = x_ref[...]
    h = h + _dot(a_ref[...], wo_ref[0:A_WIDTH, :])
    h = h + _dot(b_ref[...], wo_ref[A_WIDTH:A_WIDTH + B_WIDTH, :])
    h = h + _dot(c_ref[...], wo_ref[A_WIDTH + B_WIDTH:, :])
    h_ref[...] = h
    ms = jnp.mean(h * h, axis=-1, keepdims=True)
    hn = (h * lax.rsqrt(ms + EPS)) * g_ref[...]
    _pack_rows(hn_ref, hn)
    logits = _dot_nt(wr_ref[...], hn.astype(_BF16))[0:ROUTE_ROWS, :] + br_ref[:, 0:1]
    row = lax.broadcasted_iota(jnp.int32, (ROUTE_ROWS, tm), 0).astype(_F32)
    big = float(ROUTE_ROWS)

    def first_row(mask):
        return jnp.min(jnp.where(mask, row, big), axis=0, keepdims=True)

    gmask = row < N_GROUPS
    gl = jnp.where(gmask, logits, NEG_INF)
    ge = jnp.exp(gl - jnp.max(gl, axis=0, keepdims=True))
    gp = ge / jnp.sum(ge, axis=0, keepdims=True)
    p_group = jnp.max(gp, axis=0, keepdims=True)
    g_sel = first_row(gmask & (gp == p_group))
    lo = ROUTE_LANE0 + g_sel * EXPERTS_PER_GROUP
    emask = (row >= lo) & (row < lo + EXPERTS_PER_GROUP)
    el = jnp.where(emask, logits, NEG_INF)
    ee = jnp.exp(el - jnp.max(el, axis=0, keepdims=True))
    ep = ee / jnp.sum(ee, axis=0, keepdims=True)
    p1 = jnp.max(ep, axis=0, keepdims=True)
    i1 = first_row(emask & (ep == p1))
    ep2 = jnp.where(emask & (row != i1), ep, -1.0)
    p2 = jnp.max(ep2, axis=0, keepdims=True)
    i2 = first_row(ep2 == p2)
    den = p1 + p2
    w1 = p_group * (p1 / den)
    w2 = p_group * (p2 / den)
    hit1 = row == i1
    hit2 = row == i2
    onehot = jnp.where(hit1 | hit2, 1.0, 0.0)
    r_i = lax.broadcasted_iota(jnp.int32, (tm, tm), 0)
    c_i = lax.broadcasted_iota(jnp.int32, (tm, tm), 1)
    earlier = jnp.where(r_i < c_i, 1.0, 0.0).astype(_BF16)
    before = _dot(onehot.astype(_BF16), earlier) + carry_ref[:, 0:1]
    r1 = jnp.sum(jnp.where(hit1, before, 0.0), axis=0, keepdims=True)
    r2 = jnp.sum(jnp.where(hit2, before, 0.0), axis=0, keepdims=True)
    carry_ref[...] = carry_ref[...] + jnp.sum(onehot, axis=1, keepdims=True)
    cnt_ref[...] = carry_ref[...]
    out_row = lax.broadcasted_iota(jnp.int32, (SUBLANES, tm), 0)
    info = jnp.where(out_row == 0, w1, 0.0)
    info = jnp.where(out_row == 1, w2, info)
    info = jnp.where(out_row == 2, i1 - ROUTE_LANE0, info)
    info = jnp.where(out_row == 3, i2 - ROUTE_LANE0, info)
    info = jnp.where(out_row == 4, r1, info)
    info = jnp.where(out_row == 5, r2, info)
    rows_ref[...] = info


def _out_router(x2, oa, ob, oc, w_out, ffn_g, w_rg, b_rg, w_re, b_re):
    t, d = x2.shape
    tm = min(t, 512)
    pad = LANES - N_GROUPS - N_EXPERTS
    wr = jnp.concatenate([w_rg, w_re, jnp.zeros((d, pad), _F32)], axis=1).T.astype(_BF16)
    br = jnp.concatenate([b_rg, b_re, jnp.zeros((ROUTE_ROWS - N_GROUPS - N_EXPERTS,), _F32)])
    br = jnp.broadcast_to(br[:, None], (ROUTE_ROWS, LANES))

    def rows(w):
        return pl.BlockSpec((tm, w), lambda i: (i, 0))

    def whole(r, c):
        return pl.BlockSpec((r, c), lambda i: (0, 0))

    return pl.pallas_call(
        _out_router_kernel,
        grid=(t // tm,),
        in_specs=[rows(d), rows(A_WIDTH), rows(B_WIDTH), rows(C_WIDTH), whole(d, d), whole(1, d),
                  whole(LANES, d), whole(ROUTE_ROWS, LANES)],
        out_specs=[rows(d), pl.BlockSpec((tm * PACK_ROWS, LANES), lambda i: (i, 0)),
                   pl.BlockSpec((SUBLANES, tm), lambda i: (0, i)), whole(ROUTE_ROWS, LANES)],
        out_shape=[jax.ShapeDtypeStruct((t, d), _F32), jax.ShapeDtypeStruct((t * PACK_ROWS, LANES), jnp.uint32),
                   jax.ShapeDtypeStruct((SUBLANES, t), _F32), jax.ShapeDtypeStruct((ROUTE_ROWS, LANES), _F32)],
        scratch_shapes=[pltpu.VMEM((ROUTE_ROWS, LANES), _F32)],
        compiler_params=_cparams(1),
        name="out_router",
    )(x2, oa, ob, oc, w_out.astype(_BF16), ffn_g.reshape(1, d), wr, br)


DISPATCH_TOKENS = 2048
COMBINE_TOKENS = 256


ROW_UNROLL = 8


def _tile_rows(row, count=1, per=SUBLANES):
    start = row * per
    if not isinstance(start, int):
        start = pl.multiple_of(start, per)
    return pl.ds(start, count * per)


def _row_copy(src, s_row, dst, d_row, sem, per=SUBLANES):
    return pltpu.make_async_copy(src.at[_tile_rows(s_row, 1, per)], dst.at[_tile_rows(d_row, 1, per)], sem)


def _dispatch_kernel(pad_start_ref, pad_len_ref, used_ref, dest_ref, hn_ref, xs_ref, zero_ref, sem,
                     pad_sem):
    per = PACK_ROWS
    n = hn_ref.shape[0] // per

    @pl.when(pl.program_id(0) == 0)
    def _():
        zero_ref[...] = jnp.zeros_like(zero_ref)
        n_blocks = xs_ref.shape[0] // (ROW_BLOCK * per)

        def block_copy(blk):
            return pltpu.make_async_copy(zero_ref, xs_ref.at[_tile_rows(blk * ROW_BLOCK, ROW_BLOCK, per)],
                                         pad_sem)

        def put_block(blk, carry):
            block_copy(blk).start()
            return carry

        def done_block(blk, carry):
            block_copy(blk).wait()
            return carry

        lax.fori_loop(used_ref[0], n_blocks, put_block, 0)
        lax.fori_loop(used_ref[0], n_blocks, done_block, 0)
        bits = [1 << k for k in reversed(range(ROW_BLOCK.bit_length() - 1))]

        def tail(e, wait):
            row = pad_start_ref[e]
            for bit in bits:
                on = (pad_len_ref[e] & bit) != 0
                copy = pltpu.make_async_copy(zero_ref.at[_tile_rows(0, bit, per)],
                                             xs_ref.at[_tile_rows(row, bit, per)], pad_sem)

                @pl.when(on)
                def _():
                    copy.wait() if wait else copy.start()

                row = row + jnp.where(on, bit, 0)

        def put_tail(e, carry):
            tail(e, False)
            return carry

        def done_tail(e, carry):
            tail(e, True)
            return carry

        lax.fori_loop(0, N_EXPERTS, put_tail, 0)
        lax.fori_loop(0, N_EXPERTS, done_tail, 0)

    def issue(i, carry):
        for u in range(ROW_UNROLL):
            t = i * ROW_UNROLL + u
            _row_copy(hn_ref, t, xs_ref, dest_ref[2 * t], sem, per).start(priority=0)
            _row_copy(hn_ref, t, xs_ref, dest_ref[2 * t + 1], sem, per).start(priority=1)
        return carry

    lax.fori_loop(0, n // ROW_UNROLL, issue, 0)
    for _ in range(2):
        pltpu.make_async_copy(hn_ref, xs_ref.at[_tile_rows(0, n, per)], sem).wait()


def _dispatch(hn, dest, pad_start, pad_len, n_used, n_rows):
    t = hn.shape[0] // PACK_ROWS
    n = min(t, DISPATCH_TOKENS)
    return pl.pallas_call(
        _dispatch_kernel,
        grid_spec=pltpu.PrefetchScalarGridSpec(
            num_scalar_prefetch=3,
            grid=(t // n,),
            in_specs=[pl.BlockSpec((2 * n,), lambda i, *_: (i,), memory_space=pltpu.SMEM),
                      pl.BlockSpec((n * PACK_ROWS, LANES), lambda i, *_: (i, 0))],
            out_specs=pl.BlockSpec(memory_space=pl.ANY),
            scratch_shapes=[pltpu.VMEM((ROW_BLOCK * PACK_ROWS, LANES), hn.dtype), pltpu.SemaphoreType.DMA,
                            pltpu.SemaphoreType.DMA]),
        out_shape=jax.ShapeDtypeStruct((n_rows * PACK_ROWS, LANES), hn.dtype),
        compiler_params=_cparams(1),
        name="moe_dispatch",
    )(pad_start, pad_len, n_used, dest, hn)


def _expert_kernel(be_ref, run_ref, next_ref, used_ref, x_ref, wg_hbm, wu_hbm, wd_hbm, y_ref,
                   wg_f32, wu_f32, wd_f32, wg_bf, wu_bf, wd_bf, sem):
    i = pl.program_id(0)
    live = i < used_ref[0]
    new_expert = (i == 0) | (be_ref[i] != be_ref[jnp.maximum(i - 1, 0)])
    slot = run_ref[i] % 2

    def fetch(expert, to_slot):
        return [pltpu.make_async_copy(src.at[expert], dst.at[to_slot], sem.at[to_slot, k])
                for k, (src, dst) in enumerate(((wg_hbm, wg_f32), (wu_hbm, wu_f32), (wd_hbm, wd_f32)))]

    @pl.when(live & (i == 0))
    def _():
        for copy in fetch(be_ref[0], 0):
            copy.start()

    @pl.when(live & new_expert)
    def _():
        for copy in fetch(be_ref[i], slot):
            copy.wait()

        @pl.when(next_ref[i] >= 0)
        def _():
            for copy in fetch(next_ref[i], 1 - slot):
                copy.start()

        wg_bf[...] = wg_f32[slot].astype(_BF16)
        wu_bf[...] = wu_f32[slot].astype(_BF16)
        wd_bf[...] = wd_f32[slot].astype(_BF16)

    @pl.when(live)
    def _():
        sub = ROW_BLOCK // EXPERT_SPLIT
        gate_up = {}

        def first(k):
            x = _unpack_rows(x_ref, sub, k * sub)
            gate_up[k] = (_dot(x, wg_bf[...]), _dot(x, wu_bf[...]))

        def second(k):
            gate, up = gate_up.pop(k)
            act = (gate * jax.nn.sigmoid(gate)) * up
            _rows_to_tiles(y_ref, _dot(act.astype(_BF16), wd_bf[...]), k * sub)

        for k in range(min(EXPERT_AHEAD, EXPERT_SPLIT)):
            first(k)
        for k in range(EXPERT_SPLIT):
            if k + EXPERT_AHEAD < EXPERT_SPLIT:
                first(k + EXPERT_AHEAD)
            second(k)

    @pl.when(i >= used_ref[0])
    def _():
        y_ref[...] = jnp.zeros_like(y_ref)


def _experts(xs, blocks, w_gate, w_up, w_down):
    n_rows, d = xs.shape[0] // PACK_ROWS, D_MODEL
    n_blocks = n_rows // ROW_BLOCK
    tile_block = (ROW_BLOCK * SUBLANES, LANES)
    hbm = pl.BlockSpec(memory_space=pl.ANY)

    return pl.pallas_call(
        _expert_kernel,
        grid_spec=pltpu.PrefetchScalarGridSpec(
            num_scalar_prefetch=4,
            grid=(n_blocks,),
            in_specs=[pl.BlockSpec((ROW_BLOCK * PACK_ROWS, LANES),
                                   lambda i, be, run, nxt, used: (jnp.minimum(i, used[0] - 1), 0)),
                      hbm, hbm, hbm],
            out_specs=pl.BlockSpec(tile_block, lambda i, *_: (i, 0)),
            scratch_shapes=[pltpu.VMEM((2, d, D_EXPERT), _F32), pltpu.VMEM((2, d, D_EXPERT), _F32),
                            pltpu.VMEM((2, D_EXPERT, d), _F32),
                            pltpu.VMEM((d, D_EXPERT), _BF16), pltpu.VMEM((d, D_EXPERT), _BF16),
                            pltpu.VMEM((D_EXPERT, d), _BF16), pltpu.SemaphoreType.DMA((2, 3))]),
        out_shape=jax.ShapeDtypeStruct((n_rows * SUBLANES, LANES), _F32),
        compiler_params=_cparams(1),
        name="moe_experts",
    )(*blocks, xs, w_gate, w_up, w_down)


def _combine_kernel(dest_ref, next_ref, h_ref, rows_ref, ys_ref, o_ref, buf_ref, sem):
    n = h_ref.shape[0]
    step = pl.program_id(0)
    slot = step % 2

    def gather(idx_ref, to_slot):
        def issue(i, carry):
            for u in range(ROW_UNROLL):
                t = i * ROW_UNROLL + u
                _row_copy(ys_ref, idx_ref[2 * t], buf_ref.at[to_slot, 0], t,
                          sem.at[to_slot]).start(priority=0)
                _row_copy(ys_ref, idx_ref[2 * t + 1], buf_ref.at[to_slot, 1], t,
                          sem.at[to_slot]).start(priority=1)
            return carry

        lax.fori_loop(0, n // ROW_UNROLL, issue, 0)

    @pl.when(step == 0)
    def _():
        gather(dest_ref, 0)

    @pl.when(step + 1 < pl.num_programs(0))
    def _():
        gather(next_ref, 1 - slot)

    for k in range(2):
        pltpu.make_async_copy(ys_ref.at[_tile_rows(0, n)], buf_ref.at[slot, k], sem.at[slot]).wait()
    info = jnp.concatenate([rows_ref[...], jnp.zeros((LANES - SUBLANES, n), _F32)], axis=0).T
    w0 = info[:, 0:1]
    w1 = info[:, 1:2]
    for s in range(SUBLANES):
        sl = slice(s * LANES, (s + 1) * LANES)
        moe = w0 * _tile_block(buf_ref.at[slot, 0], s, n) + w1 * _tile_block(buf_ref.at[slot, 1], s, n)
        o_ref[:, sl] = h_ref[:, sl] + moe


def _combine(h, route_rows, ys, dest):
    t, d = h.shape
    n = min(t, COMBINE_TOKENS)
    steps = t // n
    return pl.pallas_call(
        _combine_kernel,
        grid=(steps,),
        in_specs=[pl.BlockSpec((2 * n,), lambda i: (i,), memory_space=pltpu.SMEM),
                  pl.BlockSpec((2 * n,), lambda i: (jnp.minimum(i + 1, steps - 1),),
                               memory_space=pltpu.SMEM),
                  pl.BlockSpec((n, d), lambda i: (i, 0)),
                  pl.BlockSpec((SUBLANES, n), lambda i: (0, i)),
                  pl.BlockSpec(memory_space=pl.ANY)],
        out_specs=pl.BlockSpec((n, d), lambda i: (i, 0)),
        out_shape=jax.ShapeDtypeStruct((t, d), _F32),
        scratch_shapes=[pltpu.VMEM((2, 2, n * SUBLANES, LANES), _F32), pltpu.SemaphoreType.DMA((2,))],
        compiler_params=_cparams(1),
        name="moe_combine",
    )(dest, dest, h, route_rows, ys)


def _moe_layout(route_rows, counts, t):
    counts = counts[ROUTE_LANE0:ROUTE_LANE0 + N_EXPERTS, 0].astype(jnp.int32)
    padded = (counts + ROW_BLOCK - 1) // ROW_BLOCK * ROW_BLOCK
    pends = jnp.cumsum(padded)
    pstarts = pends - padded
    eid = route_rows[2:4].astype(jnp.int32)
    rank = route_rows[4:6].astype(jnp.int32)
    experts = jnp.arange(N_EXPERTS, dtype=jnp.int32)
    start_of = jnp.sum(jnp.where(eid[:, :, None] == experts, pstarts, 0), axis=-1)
    dest = (start_of + rank).T.reshape(-1)
    n_blocks = -(-2 * t // ROW_BLOCK) + N_EXPERTS
    first_row = jnp.arange(n_blocks, dtype=jnp.int32) * ROW_BLOCK
    block_e = jnp.minimum(jnp.sum((pends[None, :] <= first_row[:, None]).astype(jnp.int32), axis=1),
                          N_EXPERTS - 1)
    n_used = (pends[-1:] // ROW_BLOCK).astype(jnp.int32)
    changed = jnp.concatenate([jnp.zeros((1,), jnp.int32), (block_e[1:] != block_e[:-1]).astype(jnp.int32)])
    block_run = jnp.cumsum(changed)
    later = (counts[None, :] > 0) & (experts[None, :] > experts[:, None])
    next_expert = jnp.min(jnp.where(later, experts[None, :], N_EXPERTS), axis=1)
    next_expert = jnp.where(next_expert < N_EXPERTS, next_expert, -1)
    block_next = jnp.sum(jnp.where(block_e[:, None] == experts[None, :], next_expert[None, :], 0), axis=1)
    blocks = (block_e, block_run.astype(jnp.int32), block_next.astype(jnp.int32), n_used)
    return dest, blocks, pstarts + counts, padded - counts, n_blocks * ROW_BLOCK


def kernel(x, mem, positions, mix_norm_g, w_in, qn_a, kn_a, rel_bias, ret_gn_g, mem_norm_g, w_mem_kv,
           qn_c, kn_c, w_out, ffn_norm_g, w_router_group, b_router_group, w_router_expert,
           b_router_expert, w_gate, w_up, w_down):
    b, s, d = x.shape
    t = b * s
    x2 = x.reshape(t, d)
    kc, vc = _mem_kv(mem, mem_norm_g, w_mem_kv, kn_c)
    tables = _retention_tables()
    qk, vta, qkzv, gate = _in_proj(x, positions, tables[1], mix_norm_g, w_in, qn_a, kn_a, qn_c)
    out_a = _attention(qk, vta, rel_bias)
    out_b = _retention(qkzv, gate, tables, ret_gn_g)
    out_c = _cross_attention(qk, kc, vc)
    h, hn, route_rows, counts = _out_router(
        x2, out_a.reshape(t, A_WIDTH), out_b.reshape(t, B_WIDTH), out_c.reshape(t, C_WIDTH),
        w_out, ffn_norm_g, w_router_group, b_router_group, w_router_expert, b_router_expert)
    dest, blocks, pad_start, pad_len, n_rows = _moe_layout(route_rows, counts, t)
    xs = _dispatch(hn, dest, pad_start, pad_len, blocks[-1], n_rows)
    ys = _experts(xs, blocks, w_gate, w_up, w_down)
    return _combine(h, route_rows, ys, dest).reshape(b, s, d)
```

```python
import functools

import jax
import jax.numpy as jnp
from jax import lax
from jax.experimental import pallas as pl
from jax.experimental.pallas import tpu as pltpu

D_MODEL = 1024
CHUNK = 64
HEAD_DIM = 64
A_HEADS = 8
B_HEADS = 4
C_HEADS = 4
A_WIDTH = A_HEADS * HEAD_DIM
B_WIDTH = B_HEADS * HEAD_DIM
C_WIDTH = C_HEADS * HEAD_DIM
IN_COLS = 3 * A_WIDTH + 4 * B_WIDTH + C_WIDTH
LEFT_CHUNKS = 8
BAND_CHUNKS = LEFT_CHUNKS + 1
MAX_REL_DIST = 128
ROPE_BASE = 10000.0
N_GROUPS = 4
EXPERTS_PER_GROUP = 8
N_EXPERTS = N_GROUPS * EXPERTS_PER_GROUP
D_EXPERT = D_MODEL // 2
EPS = 1e-6
NEG_INF = -1e30
LOG2E = 1.4426950408889634

LANES = 128
SUBLANES = 8
assert D_MODEL == SUBLANES * LANES
PACK_ROWS = SUBLANES // 2
LEFT_ROWS = LEFT_CHUNKS * CHUNK
ATT_Q = 2 * CHUNK
ATT_K = ATT_Q + LEFT_ROWS
ATT_VARIANTS = LEFT_ROWS // ATT_Q + 1
ONES_ROWS = 16
ATT_AHEAD = 3
RET_CHUNK = 256
ROW_BLOCK = 512
EXPERT_SPLIT = 2
EXPERT_AHEAD = 2
ROUTE_LANE0 = N_GROUPS
ROUTE_ROWS = 64
VMEM_LIMIT = 48 * 1024 * 1024

_F32 = jnp.float32
_BF16 = jnp.bfloat16


def _cparams(n_axes):
    return pltpu.CompilerParams(dimension_semantics=("arbitrary",) * n_axes,
                                vmem_limit_bytes=VMEM_LIMIT)


def _dot(a, b):
    return jnp.dot(a, b, preferred_element_type=_F32)


def _dot_nt(a, b):
    return lax.dot_general(a, b, (((1,), (1,)), ((), ())), preferred_element_type=_F32)


def _lane(shape):
    return lax.broadcasted_iota(jnp.int32, shape, len(shape) - 1)


def _pair_rms(t, gain):
    low = _lane(t.shape) < HEAD_DIM
    t2 = t * t
    ms0 = jnp.sum(jnp.where(low, t2, 0.0), axis=-1, keepdims=True) * (1.0 / HEAD_DIM)
    ms1 = jnp.sum(jnp.where(low, 0.0, t2), axis=-1, keepdims=True) * (1.0 / HEAD_DIM)
    r = jnp.where(low, lax.rsqrt(ms0 + EPS), lax.rsqrt(ms1 + EPS))
    return (t * r) * gain


def _rows_to_tiles(ref, val, row0=0):
    n = val.shape[0]
    for s in range(SUBLANES):
        ref[pl.ds(row0 * SUBLANES + s, n, stride=SUBLANES), :] = val[:, s * LANES:(s + 1) * LANES]


def _tile_block(ref, s, n, row0=0):
    return ref[pl.ds(row0 * SUBLANES + s, n, stride=SUBLANES), :]


def _pack_rows(ref, val, row0=0):
    n = val.shape[0]
    for s in range(PACK_ROWS):
        lo = val[:, (2 * s) * LANES:(2 * s + 1) * LANES].astype(_BF16).astype(_F32)
        hi = val[:, (2 * s + 1) * LANES:(2 * s + 2) * LANES].astype(_BF16).astype(_F32)
        word = (lax.bitcast_convert_type(lo, jnp.uint32) >> 16) | (
            lax.bitcast_convert_type(hi, jnp.uint32) & jnp.uint32(0xFFFF0000))
        ref[pl.ds(row0 * PACK_ROWS + s, n, stride=PACK_ROWS), :] = word


def _unpack_rows(ref, n, row0=0):
    parts = []
    for s in range(PACK_ROWS):
        word = ref[pl.ds(row0 * PACK_ROWS + s, n, stride=PACK_ROWS), :]
        parts.append(lax.bitcast_convert_type(word << 16, _F32))
        parts.append(lax.bitcast_convert_type(word & jnp.uint32(0xFFFF0000), _F32))
    return jnp.concatenate(parts, axis=-1).astype(_BF16)


ROPE_HALF = HEAD_DIM // 2
ROPE_PACK = LANES // ROPE_HALF


def _rope_tables(pos_ref, inv_ref, cos_ref, sin_ref):
    rows = pos_ref.shape[0]
    lane = _lane((rows, LANES))
    pos4 = pos_ref[...].astype(_F32)
    pos = pos4[:, ROPE_PACK - 1:ROPE_PACK]
    for j in reversed(range(ROPE_PACK - 1)):
        pos = jnp.where(lane // ROPE_HALF == j, pos4[:, j:j + 1], pos)
    ang = pos * inv_ref[...]
    sign = jnp.where((lane % HEAD_DIM) < ROPE_HALF, -1.0, 1.0)
    for out_ref, val in ((cos_ref, jnp.cos(ang)), (sin_ref, jnp.sin(ang))):
        for j in range(ROPE_PACK):
            seg = jnp.where(lane // ROPE_HALF == j, val, 0.0)
            full = seg
            for k in range(1, ROPE_PACK):
                full = full + pltpu.roll(seg, k * ROPE_HALF, 1)
            if out_ref is sin_ref:
                full = full * sign
            out_ref[pl.ds(j, rows, stride=ROPE_PACK), :] = full


def _rope_inputs(positions):
    b, s = positions.shape
    inv = ROPE_BASE ** (-jnp.arange(ROPE_HALF, dtype=_F32) / ROPE_HALF)
    inv128 = jnp.tile(inv, ROPE_PACK).reshape(1, LANES)
    return positions.reshape(b, s // ROPE_PACK, ROPE_PACK), inv128


def _mem_kv_kernel(mem_ref, g_ref, w_ref, kn_ref, k_ref, v_ref):
    m = mem_ref[...]
    ms = jnp.mean(m * m, axis=-1, keepdims=True)
    mn = (m * lax.rsqrt(ms + EPS)) * g_ref[...]
    kv = _dot(mn.astype(_BF16), w_ref[...])
    for j in range(C_WIDTH // LANES):
        sl = slice(j * LANES, (j + 1) * LANES)
        k_ref[:, sl] = _pair_rms(kv[:, sl], kn_ref[...]).astype(_BF16)
    v_ref[...] = kv[:, C_WIDTH:].T.astype(_BF16)


def _mem_kv(mem, mem_norm_g, w_mem_kv, kn_c):
    b, m, d = mem.shape
    kn = jnp.tile(kn_c, 2).reshape(1, LANES)
    return pl.pallas_call(
        _mem_kv_kernel,
        grid=(b,),
        in_specs=[pl.BlockSpec((None, m, d), lambda i: (i, 0, 0)),
                  pl.BlockSpec((1, d), lambda i: (0, 0)),
                  pl.BlockSpec((d, 2 * C_WIDTH), lambda i: (0, 0)),
                  pl.BlockSpec((1, LANES), lambda i: (0, 0))],
        out_specs=[pl.BlockSpec((None, m, C_WIDTH), lambda i: (i, 0, 0)),
                   pl.BlockSpec((None, C_WIDTH, m), lambda i: (i, 0, 0))],
        out_shape=[jax.ShapeDtypeStruct((b, m, C_WIDTH), _BF16),
                   jax.ShapeDtypeStruct((b, C_WIDTH, m), _BF16)],
        compiler_params=_cparams(1),
        name="mem_kv",
    )(mem, mem_norm_g.reshape(1, d), w_mem_kv.astype(_BF16), kn)


def _in_proj_kernel(x_ref, pos_ref, inv_ref, g_ref, wq_ref, wk_ref, wvt_ref, wr_ref, wc_ref, qn_ref, kn_ref,
                    cn_ref, zeta_ref, qa_ref, ka_ref, vt_ref, ret_ref, gate_ref, qc_ref, xn_ref, acc_ref,
                    accb_ref, cos_ref, sin_ref):
    x = x_ref[...]
    ms = jnp.mean(x * x, axis=-1, keepdims=True)
    xn_ref[...] = ((x * lax.rsqrt(ms + EPS)) * g_ref[...]).astype(_BF16)

    def normed(slot, out_ref, gain_ref):
        for blk in range(out_ref.shape[1] // LANES):
            sl = slice(blk * LANES, (blk + 1) * LANES)
            out_ref[:, sl] = _pair_rms(acc_ref[slot, :, sl], gain_ref[...]).astype(_BF16)

    acc_ref[0] = _dot(xn_ref[...], wq_ref[...])
    acc_ref[1] = _dot(xn_ref[...], wk_ref[...])
    _rope_tables(pos_ref, inv_ref, cos_ref, sin_ref)
    normed(0, qa_ref, qn_ref)
    acc_ref[0] = _dot_nt(wvt_ref[...], xn_ref[...])
    normed(1, ka_ref, kn_ref)
    accb_ref[...] = _dot(xn_ref[...], wr_ref[...])
    for blk in range(vt_ref.shape[0]):
        vt_ref[blk] = acc_ref[0, :, blk * LANES:(blk + 1) * LANES].astype(_BF16)
    acc_ref[1, :, 0:C_WIDTH] = _dot(xn_ref[...], wc_ref[...])
    cos, sin = cos_ref[...], sin_ref[...]
    chunks = x_ref.shape[0] // RET_CHUNK
    for p in range(B_WIDTH // LANES):
        sl = slice(p * LANES, (p + 1) * LANES)
        q = accb_ref[:, sl]
        k = accb_ref[:, B_WIDTH + p * LANES:B_WIDTH + (p + 1) * LANES]
        kr = (k * cos + _swap_halves(k) * sin) * (HEAD_DIM ** -0.5)
        ret_ref[:, sl] = (q * cos + _swap_halves(q) * sin).astype(_BF16)
        ret_ref[:, B_WIDTH + p * LANES:B_WIDTH + (p + 1) * LANES] = kr.astype(_BF16)
        ret_ref[:, 2 * B_WIDTH + p * LANES:2 * B_WIDTH + (p + 1) * LANES] = (
            kr * jnp.concatenate([zeta_ref[p]] * chunks, axis=0)).astype(_BF16)
        ret_ref[:, 3 * B_WIDTH + p * LANES:3 * B_WIDTH + (p + 1) * LANES] = accb_ref[
            :, 2 * B_WIDTH + p * LANES:2 * B_WIDTH + (p + 1) * LANES].astype(_BF16)
        gate_ref[:, sl] = accb_ref[:, 3 * B_WIDTH + p * LANES:3 * B_WIDTH + (p + 1) * LANES]
    normed(1, qc_ref, cn_ref)


def _in_proj(x3, positions, zeta, g, w_in, qn_a, kn_a, qn_c):
    b, s, d = x3.shape
    tm = min(s, 512)
    assert tm == A_WIDTH
    assert tm % RET_CHUNK == 0
    pos, inv128 = _rope_inputs(positions)
    w = w_in.astype(_BF16)
    cuts = [0, A_WIDTH, 2 * A_WIDTH, 3 * A_WIDTH, 3 * A_WIDTH + 4 * B_WIDTH, IN_COLS]
    wq, wk, wv, wr, wc = (w[:, lo:hi] for lo, hi in zip(cuts[:-1], cuts[1:]))
    scale = HEAD_DIM ** -0.5 * LOG2E
    gains = [(jnp.tile(gn, 2) * sc).reshape(1, LANES) for gn, sc in ((qn_a, scale), (kn_a, 1.0), (qn_c, scale))]

    def whole(arr):
        return pl.BlockSpec(arr.shape, lambda i, j: (0,) * arr.ndim)

    def rows(width):
        return pl.BlockSpec((None, tm, width), lambda i, j: (i, j, 0))

    consts = [inv128, g.reshape(1, d), wq, wk, wv.T, wr, wc] + gains + [zeta]
    return pl.pallas_call(
        _in_proj_kernel,
        grid=(b, s // tm),
        in_specs=[rows(d), pl.BlockSpec((None, tm // ROPE_PACK, ROPE_PACK), lambda i, j: (i, j, 0))]
        + [whole(c) for c in consts],
        out_specs=[rows(A_WIDTH), rows(A_WIDTH),
                   pl.BlockSpec((None, tm // LANES, A_WIDTH, LANES), lambda i, j: (i, j, 0, 0)),
                   rows(4 * B_WIDTH), rows(B_WIDTH), rows(C_WIDTH)],
        out_shape=[jax.ShapeDtypeStruct((b, s, A_WIDTH), _BF16), jax.ShapeDtypeStruct((b, s, A_WIDTH), _BF16),
                   jax.ShapeDtypeStruct((b, s // LANES, A_WIDTH, LANES), _BF16),
                   jax.ShapeDtypeStruct((b, s, 4 * B_WIDTH), _BF16), jax.ShapeDtypeStruct((b, s, B_WIDTH), _F32),
                   jax.ShapeDtypeStruct((b, s, C_WIDTH), _BF16)],
        scratch_shapes=[pltpu.VMEM((tm, d), _BF16), pltpu.VMEM((2, tm, A_WIDTH), _F32),
                        pltpu.VMEM((tm, 4 * B_WIDTH), _F32), pltpu.VMEM((tm, LANES), _F32),
                        pltpu.VMEM((tm, LANES), _F32)],
        compiler_params=_cparams(2),
        name="in_proj",
    )(x3, pos, *consts)


def _attn_kernel(q_ref, k_ref, vt_ref, bias_ref, o_ref, kp_ref, st_ref, var_ref, *, q_rows):
    qs = pl.program_id(2)
    s = k_ref.shape[0]
    fill_rows = min(s, 1024)
    left_blocks = LEFT_ROWS // LANES

    @pl.when(qs == 0)
    def _():
        kp_ref[0:LEFT_ROWS, :] = jnp.zeros((LEFT_ROWS, LANES), _BF16)

        def fill(i, carry):
            r = pl.multiple_of(i * fill_rows, fill_rows)
            kp_ref[pl.ds(LEFT_ROWS + r, fill_rows), :] = k_ref[pl.ds(r, fill_rows), :]
            return carry

        lax.fori_loop(0, s // fill_rows, fill, 0)
        key = lax.broadcasted_iota(jnp.int32, (ATT_K, 2 * ATT_Q), 0)
        for v in range(ATT_VARIANTS):
            var_ref[v] = jnp.where(key >= LEFT_ROWS - ATT_Q * v, bias_ref[...], NEG_INF)

    low = _lane((ATT_Q, LANES)) < HEAD_DIM
    ones = jnp.ones((ONES_ROWS, ATT_K), _BF16)
    tiles_per_step = q_rows // ATT_Q

    def scores(j):
        cp = qs * tiles_per_step + j
        q = q_ref[j * ATT_Q:(j + 1) * ATT_Q, :]
        q2 = jnp.concatenate([jnp.where(low, q, jnp.zeros_like(q)), jnp.where(low, jnp.zeros_like(q), q)], axis=0)
        kb = kp_ref[pl.ds(pl.multiple_of(cp * ATT_Q, ATT_Q), ATT_K), :]
        st_ref[j % (ATT_AHEAD + 1)] = _dot_nt(kb, q2) + var_ref[jnp.minimum(cp, ATT_VARIANTS - 1)]

    def finish(j):
        cp = qs * tiles_per_step + j
        st = st_ref[j % (ATT_AHEAD + 1)]
        m = jnp.max(st, axis=0, keepdims=True)
        p = jnp.exp2(st - m)
        vt = jnp.concatenate([vt_ref[jnp.maximum(cp + kb_i - left_blocks, 0)] for kb_i in range(ATT_K // LANES)],
                             axis=1)
        ot = _dot(jnp.concatenate([vt, ones], axis=0), p.astype(_BF16))
        inv = 1.0 / ot[LANES:LANES + 1, :]
        out_t = jnp.concatenate([ot[0:HEAD_DIM, 0:ATT_Q] * inv[:, 0:ATT_Q],
                                 ot[HEAD_DIM:LANES, ATT_Q:] * inv[:, ATT_Q:]], axis=0)
        o_ref[j * ATT_Q:(j + 1) * ATT_Q, :] = out_t.T.astype(o_ref.dtype)

    for j in range(min(ATT_AHEAD, tiles_per_step)):
        scores(j)
    for j in range(tiles_per_step):
        if j + ATT_AHEAD < tiles_per_step:
            scores(j + ATT_AHEAD)
        finish(j)


def _toeplitz_bias(rel_bias, q_len, k_len):
    h, table = rel_bias.shape
    n_diag = q_len + k_len - 1
    flat_lo = k_len - 1 - LEFT_ROWS - (CHUNK - 1)
    flat_hi = n_diag - flat_lo - table
    rev = jnp.concatenate([jnp.broadcast_to(rel_bias[:, -1:], (h, flat_hi)), rel_bias[:, ::-1],
                           jnp.broadcast_to(rel_bias[:, :1], (h, flat_lo))], axis=1).astype(_F32)
    flat = jnp.tile(rev, (1, q_len + 1))
    pitch = n_diag - 1
    skew = flat[:, q_len - 1:q_len - 1 + q_len * pitch].reshape(h, q_len, pitch)
    return skew[:, :, :k_len]


def _attn_bias(rel_bias):
    h = rel_bias.shape[0]
    bias = _toeplitz_bias(rel_bias, ATT_Q, ATT_K)
    q = lax.broadcasted_iota(jnp.int32, (ATT_Q, ATT_K), 0)
    k = lax.broadcasted_iota(jnp.int32, (ATT_Q, ATT_K), 1)
    off = k // CHUNK - q // CHUNK
    in_band = (off >= 0) & (off < BAND_CHUNKS)
    full = jnp.where(in_band[None], bias * LOG2E, NEG_INF)
    full = full.reshape(h // 2, 2, ATT_Q, ATT_K)
    return full.transpose(0, 3, 1, 2).reshape(h // 2, ATT_K, 2 * ATT_Q)


def _attention(qa, ka, vta, rel_bias):
    b, s, _ = qa.shape
    q_rows = min(s, 2048)
    pairs = A_HEADS // 2
    return pl.pallas_call(
        functools.partial(_attn_kernel, q_rows=q_rows),
        grid=(b, pairs, s // q_rows),
        in_specs=[pl.BlockSpec((None, q_rows, LANES), lambda i, p, j: (i, j, p)),
                  pl.BlockSpec((None, s, LANES), lambda i, p, j: (i, 0, p)),
                  pl.BlockSpec((None, s // LANES, LANES, LANES), lambda i, p, j: (i, 0, p, 0)),
                  pl.BlockSpec((None, ATT_K, 2 * ATT_Q), lambda i, p, j: (p, 0, 0))],
        out_specs=pl.BlockSpec((None, q_rows, LANES), lambda i, p, j: (i, j, p)),
        out_shape=jax.ShapeDtypeStruct((b, s, A_WIDTH), _BF16),
        scratch_shapes=[pltpu.VMEM((s + LEFT_ROWS, LANES), _BF16),
                        pltpu.VMEM((ATT_AHEAD + 1, ATT_K, 2 * ATT_Q), _F32),
                        pltpu.VMEM((ATT_VARIANTS, ATT_K, 2 * ATT_Q), _F32)],
        compiler_params=_cparams(3),
        name="attn_a",
    )(qa, ka, vta, _attn_bias(rel_bias))


def _swap_halves(t):
    first = (_lane(t.shape) % HEAD_DIM) < (HEAD_DIM // 2)
    return jnp.where(first, pltpu.roll(t, LANES - HEAD_DIM // 2, 1), pltpu.roll(t, HEAD_DIM // 2, 1))


def _retention_kernel(q_ref, k_ref, kz_ref, v_ref, gate_ref, decay_ref, xi_ref, cd_ref, gn_ref, o_ref,
                      state_ref, *, rows):
    @pl.when(pl.program_id(2) == 0)
    def _():
        state_ref[...] = jnp.zeros_like(state_ref)

    c = RET_CHUNK
    low = _lane((c, LANES)) < HEAD_DIM
    eye = jnp.where(lax.broadcasted_iota(jnp.int32, (LANES, LANES), 0) == _lane((LANES, LANES)),
                    1.0, 0.0).astype(_BF16)
    srow = lax.broadcasted_iota(jnp.int32, (LANES, LANES), 0) < HEAD_DIM
    scol = _lane((LANES, LANES)) < HEAD_DIM
    same_head = srow == scol

    for j in range(rows // c):
        sl = slice(j * c, (j + 1) * c)
        qb = q_ref[sl, :]
        kb = k_ref[sl, :]
        vb = v_ref[sl, :]
        inner_out = []
        for h in range(2):
            qh = jnp.where(low if h == 0 else ~low, qb, jnp.zeros_like(qb))
            inner = _dot_nt(qh, kb) * decay_ref[h]
            inner_out.append(_dot(inner.astype(_BF16), vb))
        state = state_ref[...]
        cross = _dot(qb, state.astype(_BF16)) * xi_ref[...]
        o = jnp.where(low, inner_out[0], inner_out[1]) + cross
        kz = _dot_nt(eye, kz_ref[sl, :]).astype(_BF16)
        state_ref[...] = cd_ref[...] * state + jnp.where(same_head, _dot(kz, vb), 0.0)
        mu = jnp.where(low,
                       jnp.sum(jnp.where(low, o, 0.0), axis=-1, keepdims=True),
                       jnp.sum(jnp.where(low, 0.0, o), axis=-1, keepdims=True)) * (1.0 / HEAD_DIM)
        dlt = o - mu
        d2 = dlt * dlt
        var = jnp.where(low,
                        jnp.sum(jnp.where(low, d2, 0.0), axis=-1, keepdims=True),
                        jnp.sum(jnp.where(low, 0.0, d2), axis=-1, keepdims=True)) * (1.0 / HEAD_DIM)
        y = (dlt * lax.rsqrt(var + EPS)) * gn_ref[...]
        g = gate_ref[sl, :]
        o_ref[sl, :] = ((g * jax.nn.sigmoid(g)) * y).astype(o_ref.dtype)


def _retention_tables():
    c = RET_CHUNK
    log_g = jnp.log(1.0 - jnp.exp2(-5.0 - jnp.arange(B_HEADS, dtype=_F32)))
    idx = jnp.arange(c, dtype=_F32)
    diff = idx[:, None] - idx[None, :]
    decay = jnp.where(diff >= 0, jnp.exp(log_g[:, None, None] * jnp.maximum(diff, 0.0)), 0.0)
    zeta = jnp.exp(log_g[:, None] * (c - 1 - idx))
    xi = jnp.exp(log_g[:, None] * (idx + 1.0))
    cd = jnp.exp(log_g * c)

    def lanes(tab):
        return jnp.repeat(tab.reshape(B_HEADS // 2, 2, c), HEAD_DIM, axis=1).transpose(0, 2, 1)

    cdm = jnp.repeat(cd.reshape(B_HEADS // 2, 2), HEAD_DIM, axis=1)
    cdm = jnp.broadcast_to(cdm[:, :, None], (B_HEADS // 2, LANES, LANES))
    return decay, lanes(zeta), lanes(xi), cdm


def _retention(qkzv, gate, tables, ret_gn_g):
    b, s, _ = qkzv.shape
    rows = min(s, 4096)
    pairs = B_HEADS // 2
    decay, _, xi, cdm = tables
    gn = ret_gn_g.reshape(pairs, 1, LANES)

    def col(off):
        return pl.BlockSpec((None, rows, LANES), lambda i, p, j: (i, j, off * pairs + p))

    return pl.pallas_call(
        functools.partial(_retention_kernel, rows=rows),
        grid=(b, pairs, s // rows),
        in_specs=[col(0), col(1), col(2), col(3), col(0),
                  pl.BlockSpec((2, RET_CHUNK, RET_CHUNK), lambda i, p, j: (p, 0, 0)),
                  pl.BlockSpec((None, RET_CHUNK, LANES), lambda i, p, j: (p, 0, 0)),
                  pl.BlockSpec((None, LANES, LANES), lambda i, p, j: (p, 0, 0)),
                  pl.BlockSpec((None, 1, LANES), lambda i, p, j: (p, 0, 0))],
        out_specs=pl.BlockSpec((None, rows, LANES), lambda i, p, j: (i, j, p)),
        out_shape=jax.ShapeDtypeStruct((b, s, B_WIDTH), _BF16),
        scratch_shapes=[pltpu.VMEM((LANES, LANES), _F32)],
        compiler_params=_cparams(3),
        name="retention_b",
    )(qkzv, qkzv, qkzv, qkzv, gate, decay, xi, cdm, gn)


def _cross_kernel(q_ref, k_ref, vt_ref, o_ref, st_ref, *, rows):
    low = _lane((ATT_Q, LANES)) < HEAD_DIM
    lane_blocks = C_WIDTH // LANES
    tiles = [(j, lb) for j in range(rows // ATT_Q) for lb in range(lane_blocks)]
    ones = jnp.ones((ONES_ROWS, vt_ref.shape[1]), _BF16)

    def scores(i):
        j, lb = tiles[i]
        sl = slice(lb * LANES, (lb + 1) * LANES)
        q = q_ref[j * ATT_Q:(j + 1) * ATT_Q, sl]
        q2 = jnp.concatenate([jnp.where(low, q, jnp.zeros_like(q)), jnp.where(low, jnp.zeros_like(q), q)], axis=0)
        st_ref[i % (ATT_AHEAD + 1)] = _dot_nt(k_ref[:, sl], q2)

    def finish(i):
        j, lb = tiles[i]
        sl = slice(lb * LANES, (lb + 1) * LANES)
        st = st_ref[i % (ATT_AHEAD + 1)]
        p = jnp.exp2(st - jnp.max(st, axis=0, keepdims=True))
        ot = _dot(jnp.concatenate([vt_ref[sl, :], ones], axis=0), p.astype(_BF16))
        inv = 1.0 / ot[LANES:LANES + 1, :]
        out_t = jnp.concatenate([ot[0:HEAD_DIM, 0:ATT_Q] * inv[:, 0:ATT_Q],
                                 ot[HEAD_DIM:LANES, ATT_Q:] * inv[:, ATT_Q:]], axis=0)
        o_ref[j * ATT_Q:(j + 1) * ATT_Q, sl] = out_t.T.astype(o_ref.dtype)

    for i in range(min(ATT_AHEAD, len(tiles))):
        scores(i)
    for i in range(len(tiles)):
        if i + ATT_AHEAD < len(tiles):
            scores(i + ATT_AHEAD)
        finish(i)


def _cross_attention(qc, kc, vtc):
    b, s, _ = qc.shape
    m = kc.shape[1]
    rows = min(s, 2048)
    return pl.pallas_call(
        functools.partial(_cross_kernel, rows=rows),
        grid=(b, s // rows),
        in_specs=[pl.BlockSpec((None, rows, C_WIDTH), lambda i, j: (i, j, 0)),
                  pl.BlockSpec((None, m, C_WIDTH), lambda i, j: (i, 0, 0)),
                  pl.BlockSpec((None, C_WIDTH, m), lambda i, j: (i, 0, 0))],
        out_specs=pl.BlockSpec((None, rows, C_WIDTH), lambda i, j: (i, j, 0)),
        out_shape=jax.ShapeDtypeStruct((b, s, C_WIDTH), _BF16),
        scratch_shapes=[pltpu.VMEM((ATT_AHEAD + 1, m, 2 * ATT_Q), _F32)],
        compiler_params=_cparams(2),
        name="cross_c",
    )(qc, kc, vtc)


def _out_router_kernel(x_ref, a_ref, b_ref, c_ref, wo_ref, g_ref, wr_ref, br_ref,
                       h_ref, hn_ref, rows_ref, cnt_ref, carry_ref):
    @pl.when(pl.program_id(0) == 0)
    def _():
        carry_ref[...] = jnp.zeros_like(carry_ref)

    tm = x_ref.shape[0]
    h = x_ref[...]
    h = h + _dot(a_ref[...], wo_ref[0:A_WIDTH, :])
    h = h + _dot(b_ref[...], wo_ref[A_WIDTH:A_WIDTH + B_WIDTH, :])
    h = h + _dot(c_ref[...], wo_ref[A_WIDTH + B_WIDTH:, :])
    h_ref[...] = h
    ms = jnp.mean(h * h, axis=-1, keepdims=True)
    hn = (h * lax.rsqrt(ms + EPS)) * g_ref[...]
    _pack_rows(hn_ref, hn)
    logits = _dot_nt(wr_ref[...], hn.astype(_BF16))[0:ROUTE_ROWS, :] + br_ref[:, 0:1]
    row = lax.broadcasted_iota(jnp.int32, (ROUTE_ROWS, tm), 0).astype(_F32)
    big = float(ROUTE_ROWS)

    def first_row(mask):
        return jnp.min(jnp.where(mask, row, big), axis=0, keepdims=True)

    gmask = row < N_GROUPS
    gl = jnp.where(gmask, logits, NEG_INF)
    ge = jnp.exp(gl - jnp.max(gl, axis=0, keepdims=True))
    gp = ge / jnp.sum(ge, axis=0, keepdims=True)
    p_group = jnp.max(gp, axis=0, keepdims=True)
    g_sel = first_row(gmask & (gp == p_group))
    lo = ROUTE_LANE0 + g_sel * EXPERTS_PER_GROUP
    emask = (row >= lo) & (row < lo + EXPERTS_PER_GROUP)
    el = jnp.where(emask, logits, NEG_INF)
    ee = jnp.exp(el - jnp.max(el, axis=0, keepdims=True))
    ep = ee / jnp.sum(ee, axis=0, keepdims=True)
    p1 = jnp.max(ep, axis=0, keepdims=True)
    i1 = first_row(emask & (ep == p1))
    ep2 = jnp.where(emask & (row != i1), ep, -1.0)
    p2 = jnp.max(ep2, axis=0, keepdims=True)
    i2 = first_row(ep2 == p2)
    den = p1 + p2
    w1 = p_group * (p1 / den)
    w2 = p_group * (p2 / den)
    hit1 = row == i1
    hit2 = row == i2
    onehot = jnp.where(hit1 | hit2, 1.0, 0.0)
    r_i = lax.broadcasted_iota(jnp.int32, (tm, tm), 0)
    c_i = lax.broadcasted_iota(jnp.int32, (tm, tm), 1)
    earlier = jnp.where(r_i < c_i, 1.0, 0.0).astype(_BF16)
    before = _dot(onehot.astype(_BF16), earlier) + carry_ref[:, 0:1]
    r1 = jnp.sum(jnp.where(hit1, before, 0.0), axis=0, keepdims=True)
    r2 = jnp.sum(jnp.where(hit2, before, 0.0), axis=0, keepdims=True)
    carry_ref[...] = carry_ref[...] + jnp.sum(onehot, axis=1, keepdims=True)
    cnt_ref[...] = carry_ref[...]
    out_row = lax.broadcasted_iota(jnp.int32, (SUBLANES, tm), 0)
    info = jnp.where(out_row == 0, w1, 0.0)
    info = jnp.where(out_row == 1, w2, info)
    info = jnp.where(out_row == 2, i1 - ROUTE_LANE0, info)
    info = jnp.where(out_row == 3, i2 - ROUTE_LANE0, info)
    info = jnp.where(out_row == 4, r1, info)
    info = jnp.where(out_row == 5, r2, info)
    rows_ref[...] = info


def _out_router(x2, oa, ob, oc, w_out, ffn_g, w_rg, b_rg, w_re, b_re):
    t, d = x2.shape
    tm = min(t, 512)
    pad = LANES - N_GROUPS - N_EXPERTS
    wr = jnp.concatenate([w_rg, w_re, jnp.zeros((d, pad), _F32)], axis=1).T.astype(_BF16)
    br = jnp.concatenate([b_rg, b_re, jnp.zeros((ROUTE_ROWS - N_GROUPS - N_EXPERTS,), _F32)])
    br = jnp.broadcast_to(br[:, None], (ROUTE_ROWS, LANES))

    def rows(w):
        return pl.BlockSpec((tm, w), lambda i: (i, 0))

    def whole(r, c):
        return pl.BlockSpec((r, c), lambda i: (0, 0))

    return pl.pallas_call(
        _out_router_kernel,
        grid=(t // tm,),
        in_specs=[rows(d), rows(A_WIDTH), rows(B_WIDTH), rows(C_WIDTH), whole(d, d), whole(1, d),
                  whole(LANES, d), whole(ROUTE_ROWS, LANES)],
        out_specs=[rows(d), pl.BlockSpec((tm * PACK_ROWS, LANES), lambda i: (i, 0)),
                   pl.BlockSpec((SUBLANES, tm), lambda i: (0, i)), whole(ROUTE_ROWS, LANES)],
        out_shape=[jax.ShapeDtypeStruct((t, d), _F32), jax.ShapeDtypeStruct((t * PACK_ROWS, LANES), jnp.uint32),
                   jax.ShapeDtypeStruct((SUBLANES, t), _F32), jax.ShapeDtypeStruct((ROUTE_ROWS, LANES), _F32)],
        scratch_shapes=[pltpu.VMEM((ROUTE_ROWS, LANES), _F32)],
        compiler_params=_cparams(1),
        name="out_router",
    )(x2, oa, ob, oc, w_out.astype(_BF16), ffn_g.reshape(1, d), wr, br)


DISPATCH_TOKENS = 2048
COMBINE_TOKENS = 256


ROW_UNROLL = 8


def _tile_rows(row, count=1, per=SUBLANES):
    start = row * per
    if not isinstance(start, int):
        start = pl.multiple_of(start, per)
    return pl.ds(start, count * per)


def _row_copy(src, s_row, dst, d_row, sem, per=SUBLANES):
    return pltpu.make_async_copy(src.at[_tile_rows(s_row, 1, per)], dst.at[_tile_rows(d_row, 1, per)], sem)


def _dispatch_kernel(pad_start_ref, pad_len_ref, used_ref, dest_ref, hn_ref, xs_ref, zero_ref, sem,
                     pad_sem):
    per = PACK_ROWS
    n = hn_ref.shape[0] // per

    @pl.when(pl.program_id(0) == 0)
    def _():
        zero_ref[...] = jnp.zeros_like(zero_ref)
        n_blocks = xs_ref.shape[0] // (ROW_BLOCK * per)

        def block_copy(blk):
            return pltpu.make_async_copy(zero_ref, xs_ref.at[_tile_rows(blk * ROW_BLOCK, ROW_BLOCK, per)],
                                         pad_sem)

        def put_block(blk, carry):
            block_copy(blk).start()
            return carry

        def done_block(blk, carry):
            block_copy(blk).wait()
            return carry

        lax.fori_loop(used_ref[0], n_blocks, put_block, 0)
        lax.fori_loop(used_ref[0], n_blocks, done_block, 0)
        bits = [1 << k for k in reversed(range(ROW_BLOCK.bit_length() - 1))]

        def tail(e, wait):
            row = pad_start_ref[e]
            for bit in bits:
                on = (pad_len_ref[e] & bit) != 0
                copy = pltpu.make_async_copy(zero_ref.at[_tile_rows(0, bit, per)],
                                             xs_ref.at[_tile_rows(row, bit, per)], pad_sem)

                @pl.when(on)
                def _():
                    copy.wait() if wait else copy.start()

                row = row + jnp.where(on, bit, 0)

        def put_tail(e, carry):
            tail(e, False)
            return carry

        def done_tail(e, carry):
            tail(e, True)
            return carry

        lax.fori_loop(0, N_EXPERTS, put_tail, 0)
        lax.fori_loop(0, N_EXPERTS, done_tail, 0)

    def issue(i, carry):
        for u in range(ROW_UNROLL):
            t = i * ROW_UNROLL + u
            _row_copy(hn_ref, t, xs_ref, dest_ref[2 * t], sem, per).start(priority=0)
            _row_copy(hn_ref, t, xs_ref, dest_ref[2 * t + 1], sem, per).start(priority=1)
        return carry

    lax.fori_loop(0, n // ROW_UNROLL, issue, 0)
    for _ in range(2):
        pltpu.make_async_copy(hn_ref, xs_ref.at[_tile_rows(0, n, per)], sem).wait()


def _dispatch(hn, dest, pad_start, pad_len, n_used, n_rows):
    t = hn.shape[0] // PACK_ROWS
    n = min(t, DISPATCH_TOKENS)
    return pl.pallas_call(
        _dispatch_kernel,
        grid_spec=pltpu.PrefetchScalarGridSpec(
            num_scalar_prefetch=3,
            grid=(t // n,),
            in_specs=[pl.BlockSpec((2 * n,), lambda i, *_: (i,), memory_space=pltpu.SMEM),
                      pl.BlockSpec((n * PACK_ROWS, LANES), lambda i, *_: (i, 0))],
            out_specs=pl.BlockSpec(memory_space=pl.ANY),
            scratch_shapes=[pltpu.VMEM((ROW_BLOCK * PACK_ROWS, LANES), hn.dtype), pltpu.SemaphoreType.DMA,
                            pltpu.SemaphoreType.DMA]),
        out_shape=jax.ShapeDtypeStruct((n_rows * PACK_ROWS, LANES), hn.dtype),
        compiler_params=_cparams(1),
        name="moe_dispatch",
    )(pad_start, pad_len, n_used, dest, hn)


def _expert_kernel(be_ref, run_ref, next_ref, used_ref, x_ref, wg_hbm, wu_hbm, wd_hbm, y_ref,
                   wg_f32, wu_f32, wd_f32, wg_bf, wu_bf, wd_bf, sem):
    i = pl.program_id(0)
    live = i < used_ref[0]
    new_expert = (i == 0) | (be_ref[i] != be_ref[jnp.maximum(i - 1, 0)])
    slot = run_ref[i] % 2

    def fetch(expert, to_slot):
        return [pltpu.make_async_copy(src.at[expert], dst.at[to_slot], sem.at[to_slot, k])
                for k, (src, dst) in enumerate(((wg_hbm, wg_f32), (wu_hbm, wu_f32), (wd_hbm, wd_f32)))]

    @pl.when(live & (i == 0))
    def _():
        for copy in fetch(be_ref[0], 0):
            copy.start()

    @pl.when(live & new_expert)
    def _():
        for copy in fetch(be_ref[i], slot):
            copy.wait()

        @pl.when(next_ref[i] >= 0)
        def _():
            for copy in fetch(next_ref[i], 1 - slot):
                copy.start()

        wg_bf[...] = wg_f32[slot].astype(_BF16)
        wu_bf[...] = wu_f32[slot].astype(_BF16)
        wd_bf[...] = wd_f32[slot].astype(_BF16)

    @pl.when(live)
    def _():
        sub = ROW_BLOCK // EXPERT_SPLIT
        gate_up = {}

        def first(k):
            x = _unpack_rows(x_ref, sub, k * sub)
            gate_up[k] = (_dot(x, wg_bf[...]), _dot(x, wu_bf[...]))

        def second(k):
            gate, up = gate_up.pop(k)
            act = (gate * jax.nn.sigmoid(gate)) * up
            _rows_to_tiles(y_ref, _dot(act.astype(_BF16), wd_bf[...]), k * sub)

        for k in range(min(EXPERT_AHEAD, EXPERT_SPLIT)):
            first(k)
        for k in range(EXPERT_SPLIT):
            if k + EXPERT_AHEAD < EXPERT_SPLIT:
                first(k + EXPERT_AHEAD)
            second(k)

    @pl.when(i >= used_ref[0])
    def _():
        y_ref[...] = jnp.zeros_like(y_ref)


def _experts(xs, blocks, w_gate, w_up, w_down):
    n_rows, d = xs.shape[0] // PACK_ROWS, D_MODEL
    n_blocks = n_rows // ROW_BLOCK
    tile_block = (ROW_BLOCK * SUBLANES, LANES)
    hbm = pl.BlockSpec(memory_space=pl.ANY)

    return pl.pallas_call(
        _expert_kernel,
        grid_spec=pltpu.PrefetchScalarGridSpec(
            num_scalar_prefetch=4,
            grid=(n_blocks,),
            in_specs=[pl.BlockSpec((ROW_BLOCK * PACK_ROWS, LANES),
                                   lambda i, be, run, nxt, used: (jnp.minimum(i, used[0] - 1), 0)),
                      hbm, hbm, hbm],
            out_specs=pl.BlockSpec(tile_block, lambda i, *_: (i, 0)),
            scratch_shapes=[pltpu.VMEM((2, d, D_EXPERT), _F32), pltpu.VMEM((2, d, D_EXPERT), _F32),
                            pltpu.VMEM((2, D_EXPERT, d), _F32),
                            pltpu.VMEM((d, D_EXPERT), _BF16), pltpu.VMEM((d, D_EXPERT), _BF16),
                            pltpu.VMEM((D_EXPERT, d), _BF16), pltpu.SemaphoreType.DMA((2, 3))]),
        out_shape=jax.ShapeDtypeStruct((n_rows * SUBLANES, LANES), _F32),
        compiler_params=_cparams(1),
        name="moe_experts",
    )(*blocks, xs, w_gate, w_up, w_down)


def _combine_kernel(dest_ref, next_ref, h_ref, rows_ref, ys_ref, o_ref, buf_ref, sem):
    n = h_ref.shape[0]
    step = pl.program_id(0)
    slot = step % 2

    def gather(idx_ref, to_slot):
        def issue(i, carry):
            for u in range(ROW_UNROLL):
                t = i * ROW_UNROLL + u
                _row_copy(ys_ref, idx_ref[2 * t], buf_ref.at[to_slot, 0], t,
                          sem.at[to_slot]).start(priority=0)
                _row_copy(ys_ref, idx_ref[2 * t + 1], buf_ref.at[to_slot, 1], t,
                          sem.at[to_slot]).start(priority=1)
            return carry

        lax.fori_loop(0, n // ROW_UNROLL, issue, 0)

    @pl.when(step == 0)
    def _():
        gather(dest_ref, 0)

    @pl.when(step + 1 < pl.num_programs(0))
    def _():
        gather(next_ref, 1 - slot)

    for k in range(2):
        pltpu.make_async_copy(ys_ref.at[_tile_rows(0, n)], buf_ref.at[slot, k], sem.at[slot]).wait()
    info = jnp.concatenate([rows_ref[...], jnp.zeros((LANES - SUBLANES, n), _F32)], axis=0).T
    w0 = info[:, 0:1]
    w1 = info[:, 1:2]
    for s in range(SUBLANES):
        sl = slice(s * LANES, (s + 1) * LANES)
        moe = w0 * _tile_block(buf_ref.at[slot, 0], s, n) + w1 * _tile_block(buf_ref.at[slot, 1], s, n)
        o_ref[:, sl] = h_ref[:, sl] + moe


def _combine(h, route_rows, ys, dest):
    t, d = h.shape
    n = min(t, COMBINE_TOKENS)
    steps = t // n
    return pl.pallas_call(
        _combine_kernel,
        grid=(steps,),
        in_specs=[pl.BlockSpec((2 * n,), lambda i: (i,), memory_space=pltpu.SMEM),
                  pl.BlockSpec((2 * n,), lambda i: (jnp.minimum(i + 1, steps - 1),),
                               memory_space=pltpu.SMEM),
                  pl.BlockSpec((n, d), lambda i: (i, 0)),
                  pl.BlockSpec((SUBLANES, n), lambda i: (0, i)),
                  pl.BlockSpec(memory_space=pl.ANY)],
        out_specs=pl.BlockSpec((n, d), lambda i: (i, 0)),
        out_shape=jax.ShapeDtypeStruct((t, d), _F32),
        scratch_shapes=[pltpu.VMEM((2, 2, n * SUBLANES, LANES), _F32), pltpu.SemaphoreType.DMA((2,))],
        compiler_params=_cparams(1),
        name="moe_combine",
    )(dest, dest, h, route_rows, ys)


def _moe_layout(route_rows, counts, t):
    counts = counts[ROUTE_LANE0:ROUTE_LANE0 + N_EXPERTS, 0].astype(jnp.int32)
    padded = (counts + ROW_BLOCK - 1) // ROW_BLOCK * ROW_BLOCK
    pends = jnp.cumsum(padded)
    pstarts = pends - padded
    eid = route_rows[2:4].astype(jnp.int32)
    rank = route_rows[4:6].astype(jnp.int32)
    experts = jnp.arange(N_EXPERTS, dtype=jnp.int32)
    start_of = jnp.sum(jnp.where(eid[:, :, None] == experts, pstarts, 0), axis=-1)
    dest = (start_of + rank).T.reshape(-1)
    n_blocks = -(-2 * t // ROW_BLOCK) + N_EXPERTS
    first_row = jnp.arange(n_blocks, dtype=jnp.int32) * ROW_BLOCK
    block_e = jnp.minimum(jnp.sum((pends[None, :] <= first_row[:, None]).astype(jnp.int32), axis=1),
                          N_EXPERTS - 1)
    n_used = (pends[-1:] // ROW_BLOCK).astype(jnp.int32)
    changed = jnp.concatenate([jnp.zeros((1,), jnp.int32), (block_e[1:] != block_e[:-1]).astype(jnp.int32)])
    block_run = jnp.cumsum(changed)
    later = (counts[None, :] > 0) & (experts[None, :] > experts[:, None])
    next_expert = jnp.min(jnp.where(later, experts[None, :], N_EXPERTS), axis=1)
    next_expert = jnp.where(next_expert < N_EXPERTS, next_expert, -1)
    block_next = jnp.sum(jnp.where(block_e[:, None] == experts[None, :], next_expert[None, :], 0), axis=1)
    blocks = (block_e, block_run.astype(jnp.int32), block_next.astype(jnp.int32), n_used)
    return dest, blocks, pstarts + counts, padded - counts, n_blocks * ROW_BLOCK


def kernel(x, mem, positions, mix_norm_g, w_in, qn_a, kn_a, rel_bias, ret_gn_g, mem_norm_g, w_mem_kv,
           qn_c, kn_c, w_out, ffn_norm_g, w_router_group, b_router_group, w_router_expert,
           b_router_expert, w_gate, w_up, w_down):
    b, s, d = x.shape
    t = b * s
    x2 = x.reshape(t, d)
    kc, vc = _mem_kv(mem, mem_norm_g, w_mem_kv, kn_c)
    tables = _retention_tables()
    qa, ka, vta, qkzv, gate, qc = _in_proj(x, positions, tables[1], mix_norm_g, w_in, qn_a, kn_a, qn_c)
    out_a = _attention(qa, ka, vta, rel_bias)
    out_b = _retention(qkzv, gate, tables, ret_gn_g)
    out_c = _cross_attention(qc, kc, vc)
    h, hn, route_rows, counts = _out_router(
        x2, out_a.reshape(t, A_WIDTH), out_b.reshape(t, B_WIDTH), out_c.reshape(t, C_WIDTH),
        w_out, ffn_norm_g, w_router_group, b_router_group, w_router_expert, b_router_expert)
    dest, blocks, pad_start, pad_len, n_rows = _moe_layout(route_rows, counts, t)
    xs = _dispatch(hn, dest, pad_start, pad_len, blocks[-1], n_rows)
    ys = _experts(xs, blocks, w_gate, w_up, w_down)
    return _combine(h, route_rows, ys, dest).reshape(b, s, d)
```

```python
import functools

import jax
import jax.numpy as jnp
from jax import lax
from jax.experimental import pallas as pl
from jax.experimental.pallas import tpu as pltpu

D_MODEL = 1024
CHUNK = 64
HEAD_DIM = 64
A_HEADS = 8
B_HEADS = 4
C_HEADS = 4
A_WIDTH = A_HEADS * HEAD_DIM
B_WIDTH = B_HEADS * HEAD_DIM
C_WIDTH = C_HEADS * HEAD_DIM
IN_COLS = 3 * A_WIDTH + 4 * B_WIDTH + C_WIDTH
LEFT_CHUNKS = 8
BAND_CHUNKS = LEFT_CHUNKS + 1
MAX_REL_DIST = 128
ROPE_BASE = 10000.0
N_GROUPS = 4
EXPERTS_PER_GROUP = 8
N_EXPERTS = N_GROUPS * EXPERTS_PER_GROUP
D_EXPERT = D_MODEL // 2
EPS = 1e-6
NEG_INF = -1e30
LOG2E = 1.4426950408889634

LANES = 128
SUBLANES = 8
assert D_MODEL == SUBLANES * LANES
PACK_ROWS = SUBLANES // 2
LEFT_ROWS = LEFT_CHUNKS * CHUNK
ATT_Q = 2 * CHUNK
ATT_K = ATT_Q + LEFT_ROWS
ATT_VARIANTS = LEFT_ROWS // ATT_Q + 1
ONES_ROWS = 16
ATT_AHEAD = 3
RET_CHUNK = 256
ROW_BLOCK = 512
EXPERT_SPLIT = 2
EXPERT_AHEAD = 2
ROUTE_LANE0 = N_GROUPS
ROUTE_ROWS = 64
VMEM_LIMIT = 48 * 1024 * 1024

_F32 = jnp.float32
_BF16 = jnp.bfloat16


def _cparams(n_axes):
    return pltpu.CompilerParams(dimension_semantics=("arbitrary",) * n_axes,
                                vmem_limit_bytes=VMEM_LIMIT)


def _dot(a, b):
    return jnp.dot(a, b, preferred_element_type=_F32)


def _dot_nt(a, b):
    return lax.dot_general(a, b, (((1,), (1,)), ((), ())), preferred_element_type=_F32)


def _lane(shape):
    return lax.broadcasted_iota(jnp.int32, shape, len(shape) - 1)


def _pair_rms(t, gain):
    low = _lane(t.shape) < HEAD_DIM
    t2 = t * t
    ms0 = jnp.sum(jnp.where(low, t2, 0.0), axis=-1, keepdims=True) * (1.0 / HEAD_DIM)
    ms1 = jnp.sum(jnp.where(low, 0.0, t2), axis=-1, keepdims=True) * (1.0 / HEAD_DIM)
    r = jnp.where(low, lax.rsqrt(ms0 + EPS), lax.rsqrt(ms1 + EPS))
    return (t * r) * gain


def _rows_to_tiles(ref, val, row0=0):
    n = val.shape[0]
    for s in range(SUBLANES):
        ref[pl.ds(row0 * SUBLANES + s, n, stride=SUBLANES), :] = val[:, s * LANES:(s + 1) * LANES]


def _tile_block(ref, s, n, row0=0):
    return ref[pl.ds(row0 * SUBLANES + s, n, stride=SUBLANES), :]


def _pack_rows(ref, val, row0=0):
    n = val.shape[0]
    for s in range(PACK_ROWS):
        lo = val[:, (2 * s) * LANES:(2 * s + 1) * LANES].astype(_BF16).astype(_F32)
        hi = val[:, (2 * s + 1) * LANES:(2 * s + 2) * LANES].astype(_BF16).astype(_F32)
        word = (lax.bitcast_convert_type(lo, jnp.uint32) >> 16) | (
            lax.bitcast_convert_type(hi, jnp.uint32) & jnp.uint32(0xFFFF0000))
        ref[pl.ds(row0 * PACK_ROWS + s, n, stride=PACK_ROWS), :] = word


def _unpack_rows(ref, n, row0=0):
    parts = []
    for s in range(PACK_ROWS):
        word = ref[pl.ds(row0 * PACK_ROWS + s, n, stride=PACK_ROWS), :]
        parts.append(lax.bitcast_convert_type(word << 16, _F32))
        parts.append(lax.bitcast_convert_type(word & jnp.uint32(0xFFFF0000), _F32))
    return jnp.concatenate(parts, axis=-1).astype(_BF16)


ROPE_HALF = HEAD_DIM // 2
ROPE_PACK = LANES // ROPE_HALF


def _rope_tables(pos_ref, inv_ref, cos_ref, sin_ref):
    ang = pos_ref[...].astype(_F32) * inv_ref[...]
    rows = ang.shape[0]
    lane = _lane(ang.shape)
    sign = jnp.where((lane % HEAD_DIM) < ROPE_HALF, -1.0, 1.0)
    for out_ref, val in ((cos_ref, jnp.cos(ang)), (sin_ref, jnp.sin(ang))):
        for j in range(ROPE_PACK):
            seg = jnp.where(lane // ROPE_HALF == j, val, 0.0)
            full = seg
            for k in range(1, ROPE_PACK):
                full = full + pltpu.roll(seg, k * ROPE_HALF, 1)
            if out_ref is sin_ref:
                full = full * sign
            out_ref[pl.ds(j, rows, stride=ROPE_PACK), :] = full


def _rope_inputs(positions):
    b, s = positions.shape
    inv = ROPE_BASE ** (-jnp.arange(ROPE_HALF, dtype=_F32) / ROPE_HALF)
    inv128 = jnp.tile(inv, ROPE_PACK).reshape(1, LANES)
    pos = jnp.repeat(positions.reshape(b, s // ROPE_PACK, ROPE_PACK), ROPE_HALF, axis=2)
    return pos, inv128


def _mem_kv_kernel(mem_ref, g_ref, w_ref, kn_ref, k_ref, v_ref):
    m = mem_ref[...]
    ms = jnp.mean(m * m, axis=-1, keepdims=True)
    mn = (m * lax.rsqrt(ms + EPS)) * g_ref[...]
    kv = _dot(mn.astype(_BF16), w_ref[...])
    for j in range(C_WIDTH // LANES):
        sl = slice(j * LANES, (j + 1) * LANES)
        k_ref[:, sl] = _pair_rms(kv[:, sl], kn_ref[...]).astype(_BF16)
    v_ref[...] = kv[:, C_WIDTH:].T.astype(_BF16)


def _mem_kv(mem, mem_norm_g, w_mem_kv, kn_c):
    b, m, d = mem.shape
    kn = jnp.tile(kn_c, 2).reshape(1, LANES)
    return pl.pallas_call(
        _mem_kv_kernel,
        grid=(b,),
        in_specs=[pl.BlockSpec((None, m, d), lambda i: (i, 0, 0)),
                  pl.BlockSpec((1, d), lambda i: (0, 0)),
                  pl.BlockSpec((d, 2 * C_WIDTH), lambda i: (0, 0)),
                  pl.BlockSpec((1, LANES), lambda i: (0, 0))],
        out_specs=[pl.BlockSpec((None, m, C_WIDTH), lambda i: (i, 0, 0)),
                   pl.BlockSpec((None, C_WIDTH, m), lambda i: (i, 0, 0))],
        out_shape=[jax.ShapeDtypeStruct((b, m, C_WIDTH), _BF16),
                   jax.ShapeDtypeStruct((b, C_WIDTH, m), _BF16)],
        compiler_params=_cparams(1),
        name="mem_kv",
    )(mem, mem_norm_g.reshape(1, d), w_mem_kv.astype(_BF16), kn)


def _in_proj_kernel(x_ref, pos_ref, inv_ref, g_ref, wq_ref, wk_ref, wvt_ref, wr_ref, wc_ref, qn_ref, kn_ref,
                    cn_ref, zeta_ref, qa_ref, ka_ref, vt_ref, ret_ref, gate_ref, qc_ref, xn_ref, acc_ref,
                    accb_ref, cos_ref, sin_ref):
    x = x_ref[...]
    ms = jnp.mean(x * x, axis=-1, keepdims=True)
    xn_ref[...] = ((x * lax.rsqrt(ms + EPS)) * g_ref[...]).astype(_BF16)

    def normed(slot, out_ref, gain_ref):
        for blk in range(out_ref.shape[1] // LANES):
            sl = slice(blk * LANES, (blk + 1) * LANES)
            out_ref[:, sl] = _pair_rms(acc_ref[slot, :, sl], gain_ref[...]).astype(_BF16)

    acc_ref[0] = _dot(xn_ref[...], wq_ref[...])
    acc_ref[1] = _dot(xn_ref[...], wk_ref[...])
    _rope_tables(pos_ref, inv_ref, cos_ref, sin_ref)
    normed(0, qa_ref, qn_ref)
    acc_ref[0] = _dot_nt(wvt_ref[...], xn_ref[...])
    normed(1, ka_ref, kn_ref)
    accb_ref[...] = _dot(xn_ref[...], wr_ref[...])
    for blk in range(vt_ref.shape[0]):
        vt_ref[blk] = acc_ref[0, :, blk * LANES:(blk + 1) * LANES].astype(_BF16)
    acc_ref[1, :, 0:C_WIDTH] = _dot(xn_ref[...], wc_ref[...])
    cos, sin = cos_ref[...], sin_ref[...]
    chunks = x_ref.shape[0] // RET_CHUNK
    for p in range(B_WIDTH // LANES):
        sl = slice(p * LANES, (p + 1) * LANES)
        q = accb_ref[:, sl]
        k = accb_ref[:, B_WIDTH + p * LANES:B_WIDTH + (p + 1) * LANES]
        kr = (k * cos + _swap_halves(k) * sin) * (HEAD_DIM ** -0.5)
        ret_ref[:, sl] = (q * cos + _swap_halves(q) * sin).astype(_BF16)
        ret_ref[:, B_WIDTH + p * LANES:B_WIDTH + (p + 1) * LANES] = kr.astype(_BF16)
        ret_ref[:, 2 * B_WIDTH + p * LANES:2 * B_WIDTH + (p + 1) * LANES] = (
            kr * jnp.concatenate([zeta_ref[p]] * chunks, axis=0)).astype(_BF16)
        ret_ref[:, 3 * B_WIDTH + p * LANES:3 * B_WIDTH + (p + 1) * LANES] = accb_ref[
            :, 2 * B_WIDTH + p * LANES:2 * B_WIDTH + (p + 1) * LANES].astype(_BF16)
        gate_ref[:, sl] = accb_ref[:, 3 * B_WIDTH + p * LANES:3 * B_WIDTH + (p + 1) * LANES]
    normed(1, qc_ref, cn_ref)


def _in_proj(x3, positions, zeta, g, w_in, qn_a, kn_a, qn_c):
    b, s, d = x3.shape
    tm = min(s, 512)
    assert tm == A_WIDTH
    assert tm % RET_CHUNK == 0
    pos, inv128 = _rope_inputs(positions)
    w = w_in.astype(_BF16)
    cuts = [0, A_WIDTH, 2 * A_WIDTH, 3 * A_WIDTH, 3 * A_WIDTH + 4 * B_WIDTH, IN_COLS]
    wq, wk, wv, wr, wc = (w[:, lo:hi] for lo, hi in zip(cuts[:-1], cuts[1:]))
    scale = HEAD_DIM ** -0.5 * LOG2E
    gains = [(jnp.tile(gn, 2) * sc).reshape(1, LANES) for gn, sc in ((qn_a, scale), (kn_a, 1.0), (qn_c, scale))]

    def whole(arr):
        return pl.BlockSpec(arr.shape, lambda i, j: (0,) * arr.ndim)

    def rows(width):
        return pl.BlockSpec((None, tm, width), lambda i, j: (i, j, 0))

    consts = [inv128, g.reshape(1, d), wq, wk, wv.T, wr, wc] + gains + [zeta]
    return pl.pallas_call(
        _in_proj_kernel,
        grid=(b, s // tm),
        in_specs=[rows(d), pl.BlockSpec((None, tm // ROPE_PACK, LANES), lambda i, j: (i, j, 0))]
        + [whole(c) for c in consts],
        out_specs=[rows(A_WIDTH), rows(A_WIDTH),
                   pl.BlockSpec((None, tm // LANES, A_WIDTH, LANES), lambda i, j: (i, j, 0, 0)),
                   rows(4 * B_WIDTH), rows(B_WIDTH), rows(C_WIDTH)],
        out_shape=[jax.ShapeDtypeStruct((b, s, A_WIDTH), _BF16), jax.ShapeDtypeStruct((b, s, A_WIDTH), _BF16),
                   jax.ShapeDtypeStruct((b, s // LANES, A_WIDTH, LANES), _BF16),
                   jax.ShapeDtypeStruct((b, s, 4 * B_WIDTH), _BF16), jax.ShapeDtypeStruct((b, s, B_WIDTH), _F32),
                   jax.ShapeDtypeStruct((b, s, C_WIDTH), _BF16)],
        scratch_shapes=[pltpu.VMEM((tm, d), _BF16), pltpu.VMEM((2, tm, A_WIDTH), _F32),
                        pltpu.VMEM((tm, 4 * B_WIDTH), _F32), pltpu.VMEM((tm, LANES), _F32),
                        pltpu.VMEM((tm, LANES), _F32)],
        compiler_params=_cparams(2),
        name="in_proj",
    )(x3, pos, *consts)


def _attn_kernel(q_ref, k_ref, vt_ref, bias_ref, o_ref, kp_ref, st_ref, var_ref, *, q_rows):
    qs = pl.program_id(2)
    s = k_ref.shape[0]
    fill_rows = min(s, 1024)
    left_blocks = LEFT_ROWS // LANES

    @pl.when(qs == 0)
    def _():
        kp_ref[0:LEFT_ROWS, :] = jnp.zeros((LEFT_ROWS, LANES), _BF16)

        def fill(i, carry):
            r = pl.multiple_of(i * fill_rows, fill_rows)
            kp_ref[pl.ds(LEFT_ROWS + r, fill_rows), :] = k_ref[pl.ds(r, fill_rows), :]
            return carry

        lax.fori_loop(0, s // fill_rows, fill, 0)
        key = lax.broadcasted_iota(jnp.int32, (ATT_K, ATT_Q), 0)
        off = key // CHUNK - _lane((ATT_K, ATT_Q)) // CHUNK
        in_band = (off >= 0) & (off < BAND_CHUNKS)
        n_pad = bias_ref.shape[2]
        for h in range(2):
            skew = pltpu.roll(bias_ref[h], n_pad - (ATT_Q - 1), 1, stride=1, stride_axis=0)[:, 0:ATT_K].T
            for v in range(ATT_VARIANTS):
                var_ref[v, :, h * ATT_Q:(h + 1) * ATT_Q] = jnp.where(
                    in_band & (key >= LEFT_ROWS - ATT_Q * v), skew, NEG_INF)

    low = _lane((ATT_Q, LANES)) < HEAD_DIM
    ones = jnp.ones((ONES_ROWS, ATT_K), _BF16)
    tiles_per_step = q_rows // ATT_Q

    def scores(j):
        cp = qs * tiles_per_step + j
        q = q_ref[j * ATT_Q:(j + 1) * ATT_Q, :]
        q2 = jnp.concatenate([jnp.where(low, q, jnp.zeros_like(q)), jnp.where(low, jnp.zeros_like(q), q)], axis=0)
        kb = kp_ref[pl.ds(pl.multiple_of(cp * ATT_Q, ATT_Q), ATT_K), :]
        st_ref[j % (ATT_AHEAD + 1)] = _dot_nt(kb, q2) + var_ref[jnp.minimum(cp, ATT_VARIANTS - 1)]

    def finish(j):
        cp = qs * tiles_per_step + j
        st = st_ref[j % (ATT_AHEAD + 1)]
        m = jnp.max(st, axis=0, keepdims=True)
        p = jnp.exp2(st - m)
        vt = jnp.concatenate([vt_ref[jnp.maximum(cp + kb_i - left_blocks, 0)] for kb_i in range(ATT_K // LANES)],
                             axis=1)
        ot = _dot(jnp.concatenate([vt, ones], axis=0), p.astype(_BF16))
        inv = 1.0 / ot[LANES:LANES + 1, :]
        out_t = jnp.concatenate([ot[0:HEAD_DIM, 0:ATT_Q] * inv[:, 0:ATT_Q],
                                 ot[HEAD_DIM:LANES, ATT_Q:] * inv[:, ATT_Q:]], axis=0)
        o_ref[j * ATT_Q:(j + 1) * ATT_Q, :] = out_t.T.astype(o_ref.dtype)

    for j in range(min(ATT_AHEAD, tiles_per_step)):
        scores(j)
    for j in range(tiles_per_step):
        if j + ATT_AHEAD < tiles_per_step:
            scores(j + ATT_AHEAD)
        finish(j)


def _attn_bias(rel_bias):
    h, table = rel_bias.shape
    n_diag = ATT_Q + ATT_K - 1
    flat_lo = ATT_K - 1 - LEFT_ROWS - (CHUNK - 1)
    flat_hi = n_diag - flat_lo - table
    pad = -n_diag % LANES
    rev = jnp.concatenate([jnp.broadcast_to(rel_bias[:, -1:], (h, flat_hi)), rel_bias[:, ::-1],
                           jnp.broadcast_to(rel_bias[:, :1], (h, flat_lo + pad))], axis=1).astype(_F32) * LOG2E
    return jnp.broadcast_to(rev[:, None, :], (h, ATT_Q, n_diag + pad)).reshape(h // 2, 2, ATT_Q, n_diag + pad)


def _attention(qa, ka, vta, rel_bias):
    b, s, _ = qa.shape
    q_rows = min(s, 2048)
    pairs = A_HEADS // 2
    bias = _attn_bias(rel_bias)
    return pl.pallas_call(
        functools.partial(_attn_kernel, q_rows=q_rows),
        grid=(b, pairs, s // q_rows),
        in_specs=[pl.BlockSpec((None, q_rows, LANES), lambda i, p, j: (i, j, p)),
                  pl.BlockSpec((None, s, LANES), lambda i, p, j: (i, 0, p)),
                  pl.BlockSpec((None, s // LANES, LANES, LANES), lambda i, p, j: (i, 0, p, 0)),
                  pl.BlockSpec((None,) + bias.shape[1:], lambda i, p, j: (p, 0, 0, 0))],
        out_specs=pl.BlockSpec((None, q_rows, LANES), lambda i, p, j: (i, j, p)),
        out_shape=jax.ShapeDtypeStruct((b, s, A_WIDTH), _BF16),
        scratch_shapes=[pltpu.VMEM((s + LEFT_ROWS, LANES), _BF16),
                        pltpu.VMEM((ATT_AHEAD + 1, ATT_K, 2 * ATT_Q), _F32),
                        pltpu.VMEM((ATT_VARIANTS, ATT_K, 2 * ATT_Q), _F32)],
        compiler_params=_cparams(3),
        name="attn_a",
    )(qa, ka, vta, bias)


def _swap_halves(t):
    first = (_lane(t.shape) % HEAD_DIM) < (HEAD_DIM // 2)
    return jnp.where(first, pltpu.roll(t, LANES - HEAD_DIM // 2, 1), pltpu.roll(t, HEAD_DIM // 2, 1))


def _retention_kernel(q_ref, k_ref, kz_ref, v_ref, gate_ref, decay_ref, xi_ref, cd_ref, gn_ref, o_ref,
                      state_ref, *, rows):
    @pl.when(pl.program_id(2) == 0)
    def _():
        state_ref[...] = jnp.zeros_like(state_ref)

    c = RET_CHUNK
    low = _lane((c, LANES)) < HEAD_DIM
    eye = jnp.where(lax.broadcasted_iota(jnp.int32, (LANES, LANES), 0) == _lane((LANES, LANES)),
                    1.0, 0.0).astype(_BF16)
    srow = lax.broadcasted_iota(jnp.int32, (LANES, LANES), 0) < HEAD_DIM
    scol = _lane((LANES, LANES)) < HEAD_DIM
    same_head = srow == scol

    for j in range(rows // c):
        sl = slice(j * c, (j + 1) * c)
        qb = q_ref[sl, :]
        kb = k_ref[sl, :]
        vb = v_ref[sl, :]
        inner_out = []
        for h in range(2):
            qh = jnp.where(low if h == 0 else ~low, qb, jnp.zeros_like(qb))
            inner = _dot_nt(qh, kb) * decay_ref[h]
            inner_out.append(_dot(inner.astype(_BF16), vb))
        state = state_ref[...]
        cross = _dot(qb, state.astype(_BF16)) * xi_ref[...]
        o = jnp.where(low, inner_out[0], inner_out[1]) + cross
        kz = _dot_nt(eye, kz_ref[sl, :]).astype(_BF16)
        state_ref[...] = cd_ref[...] * state + jnp.where(same_head, _dot(kz, vb), 0.0)
        mu = jnp.where(low,
                       jnp.sum(jnp.where(low, o, 0.0), axis=-1, keepdims=True),
                       jnp.sum(jnp.where(low, 0.0, o), axis=-1, keepdims=True)) * (1.0 / HEAD_DIM)
        dlt = o - mu
        d2 = dlt * dlt
        var = jnp.where(low,
                        jnp.sum(jnp.where(low, d2, 0.0), axis=-1, keepdims=True),
                        jnp.sum(jnp.where(low, 0.0, d2), axis=-1, keepdims=True)) * (1.0 / HEAD_DIM)
        y = (dlt * lax.rsqrt(var + EPS)) * gn_ref[...]
        g = gate_ref[sl, :]
        o_ref[sl, :] = ((g * jax.nn.sigmoid(g)) * y).astype(o_ref.dtype)


def _retention_tables():
    c = RET_CHUNK
    log_g = jnp.log(1.0 - jnp.exp2(-5.0 - jnp.arange(B_HEADS, dtype=_F32)))
    idx = jnp.arange(c, dtype=_F32)
    diff = idx[:, None] - idx[None, :]
    decay = jnp.where(diff >= 0, jnp.exp(log_g[:, None, None] * jnp.maximum(diff, 0.0)), 0.0)
    zeta = jnp.exp(log_g[:, None] * (c - 1 - idx))
    xi = jnp.exp(log_g[:, None] * (idx + 1.0))
    cd = jnp.exp(log_g * c)

    def lanes(tab):
        return jnp.repeat(tab.reshape(B_HEADS // 2, 2, c), HEAD_DIM, axis=1).transpose(0, 2, 1)

    cdm = jnp.repeat(cd.reshape(B_HEADS // 2, 2), HEAD_DIM, axis=1)
    cdm = jnp.broadcast_to(cdm[:, :, None], (B_HEADS // 2, LANES, LANES))
    return decay, lanes(zeta), lanes(xi), cdm


def _retention(qkzv, gate, tables, ret_gn_g):
    b, s, _ = qkzv.shape
    rows = min(s, 4096)
    pairs = B_HEADS // 2
    decay, _, xi, cdm = tables
    gn = ret_gn_g.reshape(pairs, 1, LANES)

    def col(off):
        return pl.BlockSpec((None, rows, LANES), lambda i, p, j: (i, j, off * pairs + p))

    return pl.pallas_call(
        functools.partial(_retention_kernel, rows=rows),
        grid=(b, pairs, s // rows),
        in_specs=[col(0), col(1), col(2), col(3), col(0),
                  pl.BlockSpec((2, RET_CHUNK, RET_CHUNK), lambda i, p, j: (p, 0, 0)),
                  pl.BlockSpec((None, RET_CHUNK, LANES), lambda i, p, j: (p, 0, 0)),
                  pl.BlockSpec((None, LANES, LANES), lambda i, p, j: (p, 0, 0)),
                  pl.BlockSpec((None, 1, LANES), lambda i, p, j: (p, 0, 0))],
        out_specs=pl.BlockSpec((None, rows, LANES), lambda i, p, j: (i, j, p)),
        out_shape=jax.ShapeDtypeStruct((b, s, B_WIDTH), _BF16),
        scratch_shapes=[pltpu.VMEM((LANES, LANES), _F32)],
        compiler_params=_cparams(3),
        name="retention_b",
    )(qkzv, qkzv, qkzv, qkzv, gate, decay, xi, cdm, gn)


def _cross_kernel(q_ref, k_ref, vt_ref, o_ref, st_ref, *, rows):
    low = _lane((ATT_Q, LANES)) < HEAD_DIM
    lane_blocks = C_WIDTH // LANES
    tiles = [(j, lb) for j in range(rows // ATT_Q) for lb in range(lane_blocks)]
    ones = jnp.ones((ONES_ROWS, vt_ref.shape[1]), _BF16)

    def scores(i):
        j, lb = tiles[i]
        sl = slice(lb * LANES, (lb + 1) * LANES)
        q = q_ref[j * ATT_Q:(j + 1) * ATT_Q, sl]
        q2 = jnp.concatenate([jnp.where(low, q, jnp.zeros_like(q)), jnp.where(low, jnp.zeros_like(q), q)], axis=0)
        st_ref[i % (ATT_AHEAD + 1)] = _dot_nt(k_ref[:, sl], q2)

    def finish(i):
        j, lb = tiles[i]
        sl = slice(lb * LANES, (lb + 1) * LANES)
        st = st_ref[i % (ATT_AHEAD + 1)]
        p = jnp.exp2(st - jnp.max(st, axis=0, keepdims=True))
        ot = _dot(jnp.concatenate([vt_ref[sl, :], ones], axis=0), p.astype(_BF16))
        inv = 1.0 / ot[LANES:LANES + 1, :]
        out_t = jnp.concatenate([ot[0:HEAD_DIM, 0:ATT_Q] * inv[:, 0:ATT_Q],
                                 ot[HEAD_DIM:LANES, ATT_Q:] * inv[:, ATT_Q:]], axis=0)
        o_ref[j * ATT_Q:(j + 1) * ATT_Q, sl] = out_t.T.astype(o_ref.dtype)

    for i in range(min(ATT_AHEAD, len(tiles))):
        scores(i)
    for i in range(len(tiles)):
        if i + ATT_AHEAD < len(tiles):
            scores(i + ATT_AHEAD)
        finish(i)


def _cross_attention(qc, kc, vtc):
    b, s, _ = qc.shape
    m = kc.shape[1]
    rows = min(s, 2048)
    return pl.pallas_call(
        functools.partial(_cross_kernel, rows=rows),
        grid=(b, s // rows),
        in_specs=[pl.BlockSpec((None, rows, C_WIDTH), lambda i, j: (i, j, 0)),
                  pl.BlockSpec((None, m, C_WIDTH), lambda i, j: (i, 0, 0)),
                  pl.BlockSpec((None, C_WIDTH, m), lambda i, j: (i, 0, 0))],
        out_specs=pl.BlockSpec((None, rows, C_WIDTH), lambda i, j: (i, j, 0)),
        out_shape=jax.ShapeDtypeStruct((b, s, C_WIDTH), _BF16),
        scratch_shapes=[pltpu.VMEM((ATT_AHEAD + 1, m, 2 * ATT_Q), _F32)],
        compiler_params=_cparams(2),
        name="cross_c",
    )(qc, kc, vtc)


def _out_router_kernel(x_ref, a_ref, b_ref, c_ref, wo_ref, g_ref, wr_ref, br_ref,
                       h_ref, hn_ref, rows_ref, cnt_ref, carry_ref):
    @pl.when(pl.program_id(0) == 0)
    def _():
        carry_ref[...] = jnp.zeros_like(carry_ref)

    tm = x_ref.shape[0]
    h = x_ref[...]
    h = h + _dot(a_ref[...], wo_ref[0:A_WIDTH, :])
    h = h + _dot(b_ref[...], wo_ref[A_WIDTH:A_WIDTH + B_WIDTH, :])
    h = h + _dot(c_ref[...], wo_ref[A_WIDTH + B_WIDTH:, :])
    h_ref[...] = h
    ms = jnp.mean(h * h, axis=-1, keepdims=True)
    hn = (h * lax.rsqrt(ms + EPS)) * g_ref[...]
    _pack_rows(hn_ref, hn)
    logits = _dot_nt(wr_ref[...], hn.astype(_BF16))[0:ROUTE_ROWS, :] + br_ref[:, 0:1]
    row = lax.broadcasted_iota(jnp.int32, (ROUTE_ROWS, tm), 0).astype(_F32)
    big = float(ROUTE_ROWS)

    def first_row(mask):
        return jnp.min(jnp.where(mask, row, big), axis=0, keepdims=True)

    gmask = row < N_GROUPS
    gl = jnp.where(gmask, logits, NEG_INF)
    ge = jnp.exp(gl - jnp.max(gl, axis=0, keepdims=True))
    gp = ge / jnp.sum(ge, axis=0, keepdims=True)
    p_group = jnp.max(gp, axis=0, keepdims=True)
    g_sel = first_row(gmask & (gp == p_group))
    lo = ROUTE_LANE0 + g_sel * EXPERTS_PER_GROUP
    emask = (row >= lo) & (row < lo + EXPERTS_PER_GROUP)
    el = jnp.where(emask, logits, NEG_INF)
    ee = jnp.exp(el - jnp.max(el, axis=0, keepdims=True))
    ep = ee / jnp.sum(ee, axis=0, keepdims=True)
    p1 = jnp.max(ep, axis=0, keepdims=True)
    i1 = first_row(emask & (ep == p1))
    ep2 = jnp.where(emask & (row != i1), ep, -1.0)
    p2 = jnp.max(ep2, axis=0, keepdims=True)
    i2 = first_row(ep2 == p2)
    den = p1 + p2
    w1 = p_group * (p1 / den)
    w2 = p_group * (p2 / den)
    hit1 = row == i1
    hit2 = row == i2
    onehot = jnp.where(hit1 | hit2, 1.0, 0.0)
    r_i = lax.broadcasted_iota(jnp.int32, (tm, tm), 0)
    c_i = lax.broadcasted_iota(jnp.int32, (tm, tm), 1)
    earlier = jnp.where(r_i < c_i, 1.0, 0.0).astype(_BF16)
    before = _dot(onehot.astype(_BF16), earlier) + carry_ref[:, 0:1]
    r1 = jnp.sum(jnp.where(hit1, before, 0.0), axis=0, keepdims=True)
    r2 = jnp.sum(jnp.where(hit2, before, 0.0), axis=0, keepdims=True)
    carry_ref[...] = carry_ref[...] + jnp.sum(onehot, axis=1, keepdims=True)
    cnt_ref[...] = carry_ref[...]
    out_row = lax.broadcasted_iota(jnp.int32, (SUBLANES, tm), 0)
    info = jnp.where(out_row == 0, w1, 0.0)
    info = jnp.where(out_row == 1, w2, info)
    info = jnp.where(out_row == 2, i1 - ROUTE_LANE0, info)
    info = jnp.where(out_row == 3, i2 - ROUTE_LANE0, info)
    info = jnp.where(out_row == 4, r1, info)
    info = jnp.where(out_row == 5, r2, info)
    rows_ref[...] = info


def _out_router(x2, oa, ob, oc, w_out, ffn_g, w_rg, b_rg, w_re, b_re):
    t, d = x2.shape
    tm = min(t, 512)
    pad = LANES - N_GROUPS - N_EXPERTS
    wr = jnp.concatenate([w_rg, w_re, jnp.zeros((d, pad), _F32)], axis=1).T.astype(_BF16)
    br = jnp.concatenate([b_rg, b_re, jnp.zeros((ROUTE_ROWS - N_GROUPS - N_EXPERTS,), _F32)])
    br = jnp.broadcast_to(br[:, None], (ROUTE_ROWS, LANES))

    def rows(w):
        return pl.BlockSpec((tm, w), lambda i: (i, 0))

    def whole(r, c):
        return pl.BlockSpec((r, c), lambda i: (0, 0))

    return pl.pallas_call(
        _out_router_kernel,
        grid=(t // tm,),
        in_specs=[rows(d), rows(A_WIDTH), rows(B_WIDTH), rows(C_WIDTH), whole(d, d), whole(1, d),
                  whole(LANES, d), whole(ROUTE_ROWS, LANES)],
        out_specs=[rows(d), pl.BlockSpec((tm * PACK_ROWS, LANES), lambda i: (i, 0)),
                   pl.BlockSpec((SUBLANES, tm), lambda i: (0, i)), whole(ROUTE_ROWS, LANES)],
        out_shape=[jax.ShapeDtypeStruct((t, d), _F32), jax.ShapeDtypeStruct((t * PACK_ROWS, LANES), jnp.uint32),
                   jax.ShapeDtypeStruct((SUBLANES, t), _F32), jax.ShapeDtypeStruct((ROUTE_ROWS, LANES), _F32)],
        scratch_shapes=[pltpu.VMEM((ROUTE_ROWS, LANES), _F32)],
        compiler_params=_cparams(1),
        name="out_router",
    )(x2, oa, ob, oc, w_out.astype(_BF16), ffn_g.reshape(1, d), wr, br)


DISPATCH_TOKENS = 2048
COMBINE_TOKENS = 256


ROW_UNROLL = 8


def _tile_rows(row, count=1, per=SUBLANES):
    start = row * per
    if not isinstance(start, int):
        start = pl.multiple_of(start, per)
    return pl.ds(start, count * per)


def _row_copy(src, s_row, dst, d_row, sem, per=SUBLANES):
    return pltpu.make_async_copy(src.at[_tile_rows(s_row, 1, per)], dst.at[_tile_rows(d_row, 1, per)], sem)


def _dispatch_kernel(pad_start_ref, pad_len_ref, used_ref, dest_ref, hn_ref, xs_ref, zero_ref, sem,
                     pad_sem):
    per = PACK_ROWS
    n = hn_ref.shape[0] // per

    @pl.when(pl.program_id(0) == 0)
    def _():
        zero_ref[...] = jnp.zeros_like(zero_ref)
        n_blocks = xs_ref.shape[0] // (ROW_BLOCK * per)

        def block_copy(blk):
            return pltpu.make_async_copy(zero_ref, xs_ref.at[_tile_rows(blk * ROW_BLOCK, ROW_BLOCK, per)],
                                         pad_sem)

        def put_block(blk, carry):
            block_copy(blk).start()
            return carry

        def done_block(blk, carry):
            block_copy(blk).wait()
            return carry

        lax.fori_loop(used_ref[0], n_blocks, put_block, 0)
        lax.fori_loop(used_ref[0], n_blocks, done_block, 0)
        bits = [1 << k for k in reversed(range(ROW_BLOCK.bit_length() - 1))]

        def tail(e, wait):
            row = pad_start_ref[e]
            for bit in bits:
                on = (pad_len_ref[e] & bit) != 0
                copy = pltpu.make_async_copy(zero_ref.at[_tile_rows(0, bit, per)],
                                             xs_ref.at[_tile_rows(row, bit, per)], pad_sem)

                @pl.when(on)
                def _():
                    copy.wait() if wait else copy.start()

                row = row + jnp.where(on, bit, 0)

        def put_tail(e, carry):
            tail(e, False)
            return carry

        def done_tail(e, carry):
            tail(e, True)
            return carry

        lax.fori_loop(0, N_EXPERTS, put_tail, 0)
        lax.fori_loop(0, N_EXPERTS, done_tail, 0)

    def issue(i, carry):
        for u in range(ROW_UNROLL):
            t = i * ROW_UNROLL + u
            _row_copy(hn_ref, t, xs_ref, dest_ref[2 * t], sem, per).start(priority=0)
            _row_copy(hn_ref, t, xs_ref, dest_ref[2 * t + 1], sem, per).start(priority=1)
        return carry

    lax.fori_loop(0, n // ROW_UNROLL, issue, 0)
    for _ in range(2):
        pltpu.make_async_copy(hn_ref, xs_ref.at[_tile_rows(0, n, per)], sem).wait()


def _dispatch(hn, dest, pad_start, pad_len, n_used, n_rows):
    t = hn.shape[0] // PACK_ROWS
    n = min(t, DISPATCH_TOKENS)
    return pl.pallas_call(
        _dispatch_kernel,
        grid_spec=pltpu.PrefetchScalarGridSpec(
            num_scalar_prefetch=3,
            grid=(t // n,),
            in_specs=[pl.BlockSpec((2 * n,), lambda i, *_: (i,), memory_space=pltpu.SMEM),
                      pl.BlockSpec((n * PACK_ROWS, LANES), lambda i, *_: (i, 0))],
            out_specs=pl.BlockSpec(memory_space=pl.ANY),
            scratch_shapes=[pltpu.VMEM((ROW_BLOCK * PACK_ROWS, LANES), hn.dtype), pltpu.SemaphoreType.DMA,
                            pltpu.SemaphoreType.DMA]),
        out_shape=jax.ShapeDtypeStruct((n_rows * PACK_ROWS, LANES), hn.dtype),
        compiler_params=_cparams(1),
        name="moe_dispatch",
    )(pad_start, pad_len, n_used, dest, hn)


def _expert_kernel(be_ref, run_ref, next_ref, used_ref, x_ref, wg_hbm, wu_hbm, wd_hbm, y_ref,
                   wg_f32, wu_f32, wd_f32, wg_bf, wu_bf, wd_bf, sem):
    i = pl.program_id(0)
    live = i < used_ref[0]
    new_expert = (i == 0) | (be_ref[i] != be_ref[jnp.maximum(i - 1, 0)])
    slot = run_ref[i] % 2

    def fetch(expert, to_slot):
        return [pltpu.make_async_copy(src.at[expert], dst.at[to_slot], sem.at[to_slot, k])
                for k, (src, dst) in enumerate(((wg_hbm, wg_f32), (wu_hbm, wu_f32), (wd_hbm, wd_f32)))]

    @pl.when(live & (i == 0))
    def _():
        for copy in fetch(be_ref[0], 0):
            copy.start()

    @pl.when(live & new_expert)
    def _():
        for copy in fetch(be_ref[i], slot):
            copy.wait()

        @pl.when(next_ref[i] >= 0)
        def _():
            for copy in fetch(next_ref[i], 1 - slot):
                copy.start()

        wg_bf[...] = wg_f32[slot].astype(_BF16)
        wu_bf[...] = wu_f32[slot].astype(_BF16)
        wd_bf[...] = wd_f32[slot].astype(_BF16)

    @pl.when(live)
    def _():
        sub = ROW_BLOCK // EXPERT_SPLIT
        gate_up = {}

        def first(k):
            x = _unpack_rows(x_ref, sub, k * sub)
            gate_up[k] = (_dot(x, wg_bf[...]), _dot(x, wu_bf[...]))

        def second(k):
            gate, up = gate_up.pop(k)
            act = (gate * jax.nn.sigmoid(gate)) * up
            _rows_to_tiles(y_ref, _dot(act.astype(_BF16), wd_bf[...]), k * sub)

        for k in range(min(EXPERT_AHEAD, EXPERT_SPLIT)):
            first(k)
        for k in range(EXPERT_SPLIT):
            if k + EXPERT_AHEAD < EXPERT_SPLIT:
                first(k + EXPERT_AHEAD)
            second(k)

    @pl.when(i >= used_ref[0])
    def _():
        y_ref[...] = jnp.zeros_like(y_ref)


def _experts(xs, blocks, w_gate, w_up, w_down):
    n_rows, d = xs.shape[0] // PACK_ROWS, D_MODEL
    n_blocks = n_rows // ROW_BLOCK
    tile_block = (ROW_BLOCK * SUBLANES, LANES)
    hbm = pl.BlockSpec(memory_space=pl.ANY)

    return pl.pallas_call(
        _expert_kernel,
        grid_spec=pltpu.PrefetchScalarGridSpec(
            num_scalar_prefetch=4,
            grid=(n_blocks,),
            in_specs=[pl.BlockSpec((ROW_BLOCK * PACK_ROWS, LANES),
                                   lambda i, be, run, nxt, used: (jnp.minimum(i, used[0] - 1), 0)),
                      hbm, hbm, hbm],
            out_specs=pl.BlockSpec(tile_block, lambda i, *_: (i, 0)),
            scratch_shapes=[pltpu.VMEM((2, d, D_EXPERT), _F32), pltpu.VMEM((2, d, D_EXPERT), _F32),
                            pltpu.VMEM((2, D_EXPERT, d), _F32),
                            pltpu.VMEM((d, D_EXPERT), _BF16), pltpu.VMEM((d, D_EXPERT), _BF16),
                            pltpu.VMEM((D_EXPERT, d), _BF16), pltpu.SemaphoreType.DMA((2, 3))]),
        out_shape=jax.ShapeDtypeStruct((n_rows * SUBLANES, LANES), _F32),
        compiler_params=_cparams(1),
        name="moe_experts",
    )(*blocks, xs, w_gate, w_up, w_down)


def _combine_kernel(dest_ref, next_ref, h_ref, rows_ref, ys_ref, o_ref, buf_ref, sem):
    n = h_ref.shape[0]
    step = pl.program_id(0)
    slot = step % 2

    def gather(idx_ref, to_slot):
        def issue(i, carry):
            for u in range(ROW_UNROLL):
                t = i * ROW_UNROLL + u
                _row_copy(ys_ref, idx_ref[2 * t], buf_ref.at[to_slot, 0], t,
                          sem.at[to_slot]).start(priority=0)
                _row_copy(ys_ref, idx_ref[2 * t + 1], buf_ref.at[to_slot, 1], t,
                          sem.at[to_slot]).start(priority=1)
            return carry

        lax.fori_loop(0, n // ROW_UNROLL, issue, 0)

    @pl.when(step == 0)
    def _():
        gather(dest_ref, 0)

    @pl.when(step + 1 < pl.num_programs(0))
    def _():
        gather(next_ref, 1 - slot)

    for k in range(2):
        pltpu.make_async_copy(ys_ref.at[_tile_rows(0, n)], buf_ref.at[slot, k], sem.at[slot]).wait()
    info = jnp.concatenate([rows_ref[...], jnp.zeros((LANES - SUBLANES, n), _F32)], axis=0).T
    w0 = info[:, 0:1]
    w1 = info[:, 1:2]
    for s in range(SUBLANES):
        sl = slice(s * LANES, (s + 1) * LANES)
        moe = w0 * _tile_block(buf_ref.at[slot, 0], s, n) + w1 * _tile_block(buf_ref.at[slot, 1], s, n)
        o_ref[:, sl] = h_ref[:, sl] + moe


def _combine(h, route_rows, ys, dest):
    t, d = h.shape
    n = min(t, COMBINE_TOKENS)
    steps = t // n
    return pl.pallas_call(
        _combine_kernel,
        grid=(steps,),
        in_specs=[pl.BlockSpec((2 * n,), lambda i: (i,), memory_space=pltpu.SMEM),
                  pl.BlockSpec((2 * n,), lambda i: (jnp.minimum(i + 1, steps - 1),),
                               memory_space=pltpu.SMEM),
                  pl.BlockSpec((n, d), lambda i: (i, 0)),
                  pl.BlockSpec((SUBLANES, n), lambda i: (0, i)),
                  pl.BlockSpec(memory_space=pl.ANY)],
        out_specs=pl.BlockSpec((n, d), lambda i: (i, 0)),
        out_shape=jax.ShapeDtypeStruct((t, d), _F32),
        scratch_shapes=[pltpu.VMEM((2, 2, n * SUBLANES, LANES), _F32), pltpu.SemaphoreType.DMA((2,))],
        compiler_params=_cparams(1),
        name="moe_combine",
    )(dest, dest, h, route_rows, ys)


def _moe_layout(route_rows, counts, t):
    counts = counts[ROUTE_LANE0:ROUTE_LANE0 + N_EXPERTS, 0].astype(jnp.int32)
    padded = (counts + ROW_BLOCK - 1) // ROW_BLOCK * ROW_BLOCK
    pends = jnp.cumsum(padded)
    pstarts = pends - padded
    eid = route_rows[2:4].astype(jnp.int32)
    rank = route_rows[4:6].astype(jnp.int32)
    experts = jnp.arange(N_EXPERTS, dtype=jnp.int32)
    start_of = jnp.sum(jnp.where(eid[:, :, None] == experts, pstarts, 0), axis=-1)
    dest = (start_of + rank).T.reshape(-1)
    n_blocks = -(-2 * t // ROW_BLOCK) + N_EXPERTS
    first_row = jnp.arange(n_blocks, dtype=jnp.int32) * ROW_BLOCK
    block_e = jnp.minimum(jnp.sum((pends[None, :] <= first_row[:, None]).astype(jnp.int32), axis=1),
                          N_EXPERTS - 1)
    n_used = (pends[-1:] // ROW_BLOCK).astype(jnp.int32)
    changed = jnp.concatenate([jnp.zeros((1,), jnp.int32), (block_e[1:] != block_e[:-1]).astype(jnp.int32)])
    block_run = jnp.cumsum(changed)
    later = (counts[None, :] > 0) & (experts[None, :] > experts[:, None])
    next_expert = jnp.min(jnp.where(later, experts[None, :], N_EXPERTS), axis=1)
    next_expert = jnp.where(next_expert < N_EXPERTS, next_expert, -1)
    block_next = jnp.sum(jnp.where(block_e[:, None] == experts[None, :], next_expert[None, :], 0), axis=1)
    blocks = (block_e, block_run.astype(jnp.int32), block_next.astype(jnp.int32), n_used)
    return dest, blocks, pstarts + counts, padded - counts, n_blocks * ROW_BLOCK


def kernel(x, mem, positions, mix_norm_g, w_in, qn_a, kn_a, rel_bias, ret_gn_g, mem_norm_g, w_mem_kv,
           qn_c, kn_c, w_out, ffn_norm_g, w_router_group, b_router_group, w_router_expert,
           b_router_expert, w_gate, w_up, w_down):
    b, s, d = x.shape
    t = b * s
    x2 = x.reshape(t, d)
    kc, vc = _mem_kv(mem, mem_norm_g, w_mem_kv, kn_c)
    tables = _retention_tables()
    qa, ka, vta, qkzv, gate, qc = _in_proj(x, positions, tables[1], mix_norm_g, w_in, qn_a, kn_a, qn_c)
    out_a = _attention(qa, ka, vta, rel_bias)
    out_b = _retention(qkzv, gate, tables, ret_gn_g)
    out_c = _cross_attention(qc, kc, vc)
    h, hn, route_rows, counts = _out_router(
        x2, out_a.reshape(t, A_WIDTH), out_b.reshape(t, B_WIDTH), out_c.reshape(t, C_WIDTH),
        w_out, ffn_norm_g, w_router_group, b_router_group, w_router_expert, b_router_expert)
    dest, blocks, pad_start, pad_len, n_rows = _moe_layout(route_rows, counts, t)
    xs = _dispatch(hn, dest, pad_start, pad_len, blocks[-1], n_rows)
    ys = _experts(xs, blocks, w_gate, w_up, w_down)
    return _combine(h, route_rows, ys, dest).reshape(b, s, d)
```

```python
import functools

import jax
import jax.numpy as jnp
from jax import lax
from jax.experimental import pallas as pl
from jax.experimental.pallas import tpu as pltpu

D_MODEL = 1024
CHUNK = 64
HEAD_DIM = 64
A_HEADS = 8
B_HEADS = 4
C_HEADS = 4
A_WIDTH = A_HEADS * HEAD_DIM
B_WIDTH = B_HEADS * HEAD_DIM
C_WIDTH = C_HEADS * HEAD_DIM
IN_COLS = 3 * A_WIDTH + 4 * B_WIDTH + C_WIDTH
LEFT_CHUNKS = 8
BAND_CHUNKS = LEFT_CHUNKS + 1
MAX_REL_DIST = 128
ROPE_BASE = 10000.0
N_GROUPS = 4
EXPERTS_PER_GROUP = 8
N_EXPERTS = N_GROUPS * EXPERTS_PER_GROUP
D_EXPERT = D_MODEL // 2
EPS = 1e-6
NEG_INF = -1e30
LOG2E = 1.4426950408889634

LANES = 128
SUBLANES = 8
assert D_MODEL == SUBLANES * LANES
PACK_ROWS = SUBLANES // 2
LEFT_ROWS = LEFT_CHUNKS * CHUNK
ATT_Q = 2 * CHUNK
ATT_K = ATT_Q + LEFT_ROWS
ATT_VARIANTS = LEFT_ROWS // ATT_Q + 1
ONES_ROWS = 16
ATT_AHEAD = 3
RET_CHUNK = 256
ROW_BLOCK = 512
EXPERT_SPLIT = 2
EXPERT_AHEAD = 2
ROUTE_LANE0 = N_GROUPS
ROUTE_ROWS = 64
VMEM_LIMIT = 48 * 1024 * 1024

_F32 = jnp.float32
_BF16 = jnp.bfloat16


def _cparams(n_axes):
    return pltpu.CompilerParams(dimension_semantics=("arbitrary",) * n_axes,
                                vmem_limit_bytes=VMEM_LIMIT)


def _dot(a, b):
    return jnp.dot(a, b, preferred_element_type=_F32)


def _dot_nt(a, b):
    return lax.dot_general(a, b, (((1,), (1,)), ((), ())), preferred_element_type=_F32)


def _lane(shape):
    return lax.broadcasted_iota(jnp.int32, shape, len(shape) - 1)


def _pair_rms(t, gain):
    low = _lane(t.shape) < HEAD_DIM
    t2 = t * t
    ms0 = jnp.sum(jnp.where(low, t2, 0.0), axis=-1, keepdims=True) * (1.0 / HEAD_DIM)
    ms1 = jnp.sum(jnp.where(low, 0.0, t2), axis=-1, keepdims=True) * (1.0 / HEAD_DIM)
    r = jnp.where(low, lax.rsqrt(ms0 + EPS), lax.rsqrt(ms1 + EPS))
    return (t * r) * gain


def _rows_to_tiles(ref, val, row0=0):
    n = val.shape[0]
    for s in range(SUBLANES):
        ref[pl.ds(row0 * SUBLANES + s, n, stride=SUBLANES), :] = val[:, s * LANES:(s + 1) * LANES]


def _tile_block(ref, s, n, row0=0):
    return ref[pl.ds(row0 * SUBLANES + s, n, stride=SUBLANES), :]


def _pack_rows(ref, val, row0=0):
    n = val.shape[0]
    for s in range(PACK_ROWS):
        lo = val[:, (2 * s) * LANES:(2 * s + 1) * LANES].astype(_BF16).astype(_F32)
        hi = val[:, (2 * s + 1) * LANES:(2 * s + 2) * LANES].astype(_BF16).astype(_F32)
        word = (lax.bitcast_convert_type(lo, jnp.uint32) >> 16) | (
            lax.bitcast_convert_type(hi, jnp.uint32) & jnp.uint32(0xFFFF0000))
        ref[pl.ds(row0 * PACK_ROWS + s, n, stride=PACK_ROWS), :] = word


def _unpack_rows(ref, n, row0=0):
    parts = []
    for s in range(PACK_ROWS):
        word = ref[pl.ds(row0 * PACK_ROWS + s, n, stride=PACK_ROWS), :]
        parts.append(lax.bitcast_convert_type(word << 16, _F32))
        parts.append(lax.bitcast_convert_type(word & jnp.uint32(0xFFFF0000), _F32))
    return jnp.concatenate(parts, axis=-1).astype(_BF16)


ROPE_HALF = HEAD_DIM // 2
ROPE_PACK = LANES // ROPE_HALF


def _rope_tables(pos_ref, inv_ref, cos_ref, sin_ref):
    ang = pos_ref[...].astype(_F32) * inv_ref[...]
    rows = ang.shape[0]
    lane = _lane(ang.shape)
    sign = jnp.where((lane % HEAD_DIM) < ROPE_HALF, -1.0, 1.0)
    for out_ref, val in ((cos_ref, jnp.cos(ang)), (sin_ref, jnp.sin(ang))):
        for j in range(ROPE_PACK):
            seg = jnp.where(lane // ROPE_HALF == j, val, 0.0)
            full = seg
            for k in range(1, ROPE_PACK):
                full = full + pltpu.roll(seg, k * ROPE_HALF, 1)
            if out_ref is sin_ref:
                full = full * sign
            out_ref[pl.ds(j, rows, stride=ROPE_PACK), :] = full


def _rope_inputs(positions):
    b, s = positions.shape
    inv = ROPE_BASE ** (-jnp.arange(ROPE_HALF, dtype=_F32) / ROPE_HALF)
    inv128 = jnp.tile(inv, ROPE_PACK).reshape(1, LANES)
    pos = jnp.repeat(positions.reshape(b, s // ROPE_PACK, ROPE_PACK), ROPE_HALF, axis=2)
    return pos, inv128


def _mem_kv_kernel(mem_ref, g_ref, w_ref, kn_ref, k_ref, v_ref):
    m = mem_ref[...]
    ms = jnp.mean(m * m, axis=-1, keepdims=True)
    mn = (m * lax.rsqrt(ms + EPS)) * g_ref[...]
    kv = _dot(mn.astype(_BF16), w_ref[...])
    for j in range(C_WIDTH // LANES):
        sl = slice(j * LANES, (j + 1) * LANES)
        k_ref[:, sl] = _pair_rms(kv[:, sl], kn_ref[...]).astype(_BF16)
    v_ref[...] = kv[:, C_WIDTH:].T.astype(_BF16)


def _mem_kv(mem, mem_norm_g, w_mem_kv, kn_c):
    b, m, d = mem.shape
    kn = jnp.tile(kn_c, 2).reshape(1, LANES)
    return pl.pallas_call(
        _mem_kv_kernel,
        grid=(b,),
        in_specs=[pl.BlockSpec((None, m, d), lambda i: (i, 0, 0)),
                  pl.BlockSpec((1, d), lambda i: (0, 0)),
                  pl.BlockSpec((d, 2 * C_WIDTH), lambda i: (0, 0)),
                  pl.BlockSpec((1, LANES), lambda i: (0, 0))],
        out_specs=[pl.BlockSpec((None, m, C_WIDTH), lambda i: (i, 0, 0)),
                   pl.BlockSpec((None, C_WIDTH, m), lambda i: (i, 0, 0))],
        out_shape=[jax.ShapeDtypeStruct((b, m, C_WIDTH), _BF16),
                   jax.ShapeDtypeStruct((b, C_WIDTH, m), _BF16)],
        compiler_params=_cparams(1),
        name="mem_kv",
    )(mem, mem_norm_g.reshape(1, d), w_mem_kv.astype(_BF16), kn)


def _in_proj_kernel(x_ref, pos_ref, inv_ref, g_ref, wq_ref, wk_ref, wvt_ref, wr_ref, wc_ref, qn_ref, kn_ref,
                    cn_ref, zeta_ref, qa_ref, ka_ref, vt_ref, ret_ref, gate_ref, qc_ref, xn_ref, acc_ref,
                    accb_ref, cos_ref, sin_ref):
    x = x_ref[...]
    ms = jnp.mean(x * x, axis=-1, keepdims=True)
    xn_ref[...] = ((x * lax.rsqrt(ms + EPS)) * g_ref[...]).astype(_BF16)

    def normed(slot, out_ref, gain_ref):
        for blk in range(out_ref.shape[1] // LANES):
            sl = slice(blk * LANES, (blk + 1) * LANES)
            out_ref[:, sl] = _pair_rms(acc_ref[slot, :, sl], gain_ref[...]).astype(_BF16)

    acc_ref[0] = _dot(xn_ref[...], wq_ref[...])
    acc_ref[1] = _dot(xn_ref[...], wk_ref[...])
    _rope_tables(pos_ref, inv_ref, cos_ref, sin_ref)
    normed(0, qa_ref, qn_ref)
    acc_ref[0] = _dot_nt(wvt_ref[...], xn_ref[...])
    normed(1, ka_ref, kn_ref)
    accb_ref[...] = _dot(xn_ref[...], wr_ref[...])
    for blk in range(vt_ref.shape[0]):
        vt_ref[blk] = acc_ref[0, :, blk * LANES:(blk + 1) * LANES].astype(_BF16)
    acc_ref[1, :, 0:C_WIDTH] = _dot(xn_ref[...], wc_ref[...])
    cos, sin = cos_ref[...], sin_ref[...]
    chunks = x_ref.shape[0] // RET_CHUNK
    for p in range(B_WIDTH // LANES):
        sl = slice(p * LANES, (p + 1) * LANES)
        q = accb_ref[:, sl]
        k = accb_ref[:, B_WIDTH + p * LANES:B_WIDTH + (p + 1) * LANES]
        kr = (k * cos + _swap_halves(k) * sin) * (HEAD_DIM ** -0.5)
        ret_ref[:, sl] = (q * cos + _swap_halves(q) * sin).astype(_BF16)
        ret_ref[:, B_WIDTH + p * LANES:B_WIDTH + (p + 1) * LANES] = kr.astype(_BF16)
        ret_ref[:, 2 * B_WIDTH + p * LANES:2 * B_WIDTH + (p + 1) * LANES] = (
            kr * jnp.concatenate([zeta_ref[p]] * chunks, axis=0)).astype(_BF16)
        ret_ref[:, 3 * B_WIDTH + p * LANES:3 * B_WIDTH + (p + 1) * LANES] = accb_ref[
            :, 2 * B_WIDTH + p * LANES:2 * B_WIDTH + (p + 1) * LANES].astype(_BF16)
        gate_ref[:, sl] = accb_ref[:, 3 * B_WIDTH + p * LANES:3 * B_WIDTH + (p + 1) * LANES]
    normed(1, qc_ref, cn_ref)


def _in_proj(x3, positions, zeta, g, w_in, qn_a, kn_a, qn_c):
    b, s, d = x3.shape
    tm = min(s, 512)
    assert tm == A_WIDTH
    assert tm % RET_CHUNK == 0
    pos, inv128 = _rope_inputs(positions)
    w = w_in.astype(_BF16)
    cuts = [0, A_WIDTH, 2 * A_WIDTH, 3 * A_WIDTH, 3 * A_WIDTH + 4 * B_WIDTH, IN_COLS]
    wq, wk, wv, wr, wc = (w[:, lo:hi] for lo, hi in zip(cuts[:-1], cuts[1:]))
    scale = HEAD_DIM ** -0.5 * LOG2E
    gains = [(jnp.tile(gn, 2) * sc).reshape(1, LANES) for gn, sc in ((qn_a, scale), (kn_a, 1.0), (qn_c, scale))]

    def whole(arr):
        return pl.BlockSpec(arr.shape, lambda i, j: (0,) * arr.ndim)

    def rows(width):
        return pl.BlockSpec((None, tm, width), lambda i, j: (i, j, 0))

    consts = [inv128, g.reshape(1, d), wq, wk, wv.T, wr, wc] + gains + [zeta]
    return pl.pallas_call(
        _in_proj_kernel,
        grid=(b, s // tm),
        in_specs=[rows(d), pl.BlockSpec((None, tm // ROPE_PACK, LANES), lambda i, j: (i, j, 0))]
        + [whole(c) for c in consts],
        out_specs=[rows(A_WIDTH), rows(A_WIDTH),
                   pl.BlockSpec((None, tm // LANES, A_WIDTH, LANES), lambda i, j: (i, j, 0, 0)),
                   rows(4 * B_WIDTH), rows(B_WIDTH), rows(C_WIDTH)],
        out_shape=[jax.ShapeDtypeStruct((b, s, A_WIDTH), _BF16), jax.ShapeDtypeStruct((b, s, A_WIDTH), _BF16),
                   jax.ShapeDtypeStruct((b, s // LANES, A_WIDTH, LANES), _BF16),
                   jax.ShapeDtypeStruct((b, s, 4 * B_WIDTH), _BF16), jax.ShapeDtypeStruct((b, s, B_WIDTH), _F32),
                   jax.ShapeDtypeStruct((b, s, C_WIDTH), _BF16)],
        scratch_shapes=[pltpu.VMEM((tm, d), _BF16), pltpu.VMEM((2, tm, A_WIDTH), _F32),
                        pltpu.VMEM((tm, 4 * B_WIDTH), _F32), pltpu.VMEM((tm, LANES), _F32),
                        pltpu.VMEM((tm, LANES), _F32)],
        compiler_params=_cparams(2),
        name="in_proj",
    )(x3, pos, *consts)


def _attn_kernel(q_ref, k_ref, vt_ref, bias_ref, o_ref, kp_ref, st_ref, var_ref, *, q_rows):
    qs = pl.program_id(2)
    s = k_ref.shape[0]
    fill_rows = min(s, 1024)
    left_blocks = LEFT_ROWS // LANES

    @pl.when(qs == 0)
    def _():
        kp_ref[0:LEFT_ROWS, :] = jnp.zeros((LEFT_ROWS, LANES), _BF16)

        def fill(i, carry):
            r = pl.multiple_of(i * fill_rows, fill_rows)
            kp_ref[pl.ds(LEFT_ROWS + r, fill_rows), :] = k_ref[pl.ds(r, fill_rows), :]
            return carry

        lax.fori_loop(0, s // fill_rows, fill, 0)
        key = lax.broadcasted_iota(jnp.int32, (ATT_K, 2 * ATT_Q), 0)
        for v in range(ATT_VARIANTS):
            var_ref[v] = jnp.where(key >= LEFT_ROWS - ATT_Q * v, bias_ref[...], NEG_INF)

    low = _lane((ATT_Q, LANES)) < HEAD_DIM
    ones = jnp.ones((ONES_ROWS, ATT_K), _BF16)
    tiles_per_step = q_rows // ATT_Q

    def scores(j):
        cp = qs * tiles_per_step + j
        q = q_ref[j * ATT_Q:(j + 1) * ATT_Q, :]
        q2 = jnp.concatenate([jnp.where(low, q, jnp.zeros_like(q)), jnp.where(low, jnp.zeros_like(q), q)], axis=0)
        kb = kp_ref[pl.ds(pl.multiple_of(cp * ATT_Q, ATT_Q), ATT_K), :]
        st_ref[j % (ATT_AHEAD + 1)] = _dot_nt(kb, q2) + var_ref[jnp.minimum(cp, ATT_VARIANTS - 1)]

    def finish(j):
        cp = qs * tiles_per_step + j
        st = st_ref[j % (ATT_AHEAD + 1)]
        m = jnp.max(st, axis=0, keepdims=True)
        p = jnp.exp2(st - m)
        vt = jnp.concatenate([vt_ref[jnp.maximum(cp + kb_i - left_blocks, 0)] for kb_i in range(ATT_K // LANES)],
                             axis=1)
        ot = _dot(jnp.concatenate([vt, ones], axis=0), p.astype(_BF16))
        inv = 1.0 / ot[LANES:LANES + 1, :]
        out_t = jnp.concatenate([ot[0:HEAD_DIM, 0:ATT_Q] * inv[:, 0:ATT_Q],
                                 ot[HEAD_DIM:LANES, ATT_Q:] * inv[:, ATT_Q:]], axis=0)
        o_ref[j * ATT_Q:(j + 1) * ATT_Q, :] = out_t.T.astype(o_ref.dtype)

    for j in range(min(ATT_AHEAD, tiles_per_step)):
        scores(j)
    for j in range(tiles_per_step):
        if j + ATT_AHEAD < tiles_per_step:
            scores(j + ATT_AHEAD)
        finish(j)


def _toeplitz_bias(rel_bias, q_len, k_len):
    h, table = rel_bias.shape
    n_diag = q_len + k_len - 1
    flat_lo = k_len - 1 - LEFT_ROWS - (CHUNK - 1)
    flat_hi = n_diag - flat_lo - table
    rev = jnp.concatenate([jnp.broadcast_to(rel_bias[:, -1:], (h, flat_hi)), rel_bias[:, ::-1],
                           jnp.broadcast_to(rel_bias[:, :1], (h, flat_lo))], axis=1).astype(_F32)
    flat = jnp.tile(rev, (1, q_len + 1))
    pitch = n_diag - 1
    skew = flat[:, q_len - 1:q_len - 1 + q_len * pitch].reshape(h, q_len, pitch)
    return skew[:, :, :k_len]


def _attn_bias(rel_bias):
    h = rel_bias.shape[0]
    bias = _toeplitz_bias(rel_bias, ATT_Q, ATT_K)
    q = lax.broadcasted_iota(jnp.int32, (ATT_Q, ATT_K), 0)
    k = lax.broadcasted_iota(jnp.int32, (ATT_Q, ATT_K), 1)
    off = k // CHUNK - q // CHUNK
    in_band = (off >= 0) & (off < BAND_CHUNKS)
    full = jnp.where(in_band[None], bias * LOG2E, NEG_INF)
    full = full.reshape(h // 2, 2, ATT_Q, ATT_K)
    return full.transpose(0, 3, 1, 2).reshape(h // 2, ATT_K, 2 * ATT_Q)


def _attention(qa, ka, vta, rel_bias):
    b, s, _ = qa.shape
    q_rows = min(s, 4096)
    pairs = A_HEADS // 2
    return pl.pallas_call(
        functools.partial(_attn_kernel, q_rows=q_rows),
        grid=(b, pairs, s // q_rows),
        in_specs=[pl.BlockSpec((None, q_rows, LANES), lambda i, p, j: (i, j, p)),
                  pl.BlockSpec((None, s, LANES), lambda i, p, j: (i, 0, p)),
                  pl.BlockSpec((None, s // LANES, LANES, LANES), lambda i, p, j: (i, 0, p, 0)),
                  pl.BlockSpec((None, ATT_K, 2 * ATT_Q), lambda i, p, j: (p, 0, 0))],
        out_specs=pl.BlockSpec((None, q_rows, LANES), lambda i, p, j: (i, j, p)),
        out_shape=jax.ShapeDtypeStruct((b, s, A_WIDTH), _BF16),
        scratch_shapes=[pltpu.VMEM((s + LEFT_ROWS, LANES), _BF16),
                        pltpu.VMEM((ATT_AHEAD + 1, ATT_K, 2 * ATT_Q), _F32),
                        pltpu.VMEM((ATT_VARIANTS, ATT_K, 2 * ATT_Q), _F32)],
        compiler_params=_cparams(3),
        name="attn_a",
    )(qa, ka, vta, _attn_bias(rel_bias))


def _swap_halves(t):
    first = (_lane(t.shape) % HEAD_DIM) < (HEAD_DIM // 2)
    return jnp.where(first, pltpu.roll(t, LANES - HEAD_DIM // 2, 1), pltpu.roll(t, HEAD_DIM // 2, 1))


def _retention_kernel(q_ref, k_ref, kz_ref, v_ref, gate_ref, decay_ref, xi_ref, cd_ref, gn_ref, o_ref,
                      state_ref, *, rows):
    @pl.when(pl.program_id(2) == 0)
    def _():
        state_ref[...] = jnp.zeros_like(state_ref)

    c = RET_CHUNK
    low = _lane((c, LANES)) < HEAD_DIM
    eye = jnp.where(lax.broadcasted_iota(jnp.int32, (LANES, LANES), 0) == _lane((LANES, LANES)),
                    1.0, 0.0).astype(_BF16)
    srow = lax.broadcasted_iota(jnp.int32, (LANES, LANES), 0) < HEAD_DIM
    scol = _lane((LANES, LANES)) < HEAD_DIM
    same_head = srow == scol

    for j in range(rows // c):
        sl = slice(j * c, (j + 1) * c)
        qb = q_ref[sl, :]
        kb = k_ref[sl, :]
        vb = v_ref[sl, :]
        inner_out = []
        for h in range(2):
            qh = jnp.where(low if h == 0 else ~low, qb, jnp.zeros_like(qb))
            inner = _dot_nt(qh, kb) * decay_ref[h]
            inner_out.append(_dot(inner.astype(_BF16), vb))
        state = state_ref[...]
        cross = _dot(qb, state.astype(_BF16)) * xi_ref[...]
        o = jnp.where(low, inner_out[0], inner_out[1]) + cross
        kz = _dot_nt(eye, kz_ref[sl, :]).astype(_BF16)
        state_ref[...] = cd_ref[...] * state + jnp.where(same_head, _dot(kz, vb), 0.0)
        mu = jnp.where(low,
                       jnp.sum(jnp.where(low, o, 0.0), axis=-1, keepdims=True),
                       jnp.sum(jnp.where(low, 0.0, o), axis=-1, keepdims=True)) * (1.0 / HEAD_DIM)
        dlt = o - mu
        d2 = dlt * dlt
        var = jnp.where(low,
                        jnp.sum(jnp.where(low, d2, 0.0), axis=-1, keepdims=True),
                        jnp.sum(jnp.where(low, 0.0, d2), axis=-1, keepdims=True)) * (1.0 / HEAD_DIM)
        y = (dlt * lax.rsqrt(var + EPS)) * gn_ref[...]
        g = gate_ref[sl, :]
        o_ref[sl, :] = ((g * jax.nn.sigmoid(g)) * y).astype(o_ref.dtype)


def _retention_tables():
    c = RET_CHUNK
    log_g = jnp.log(1.0 - jnp.exp2(-5.0 - jnp.arange(B_HEADS, dtype=_F32)))
    idx = jnp.arange(c, dtype=_F32)
    diff = idx[:, None] - idx[None, :]
    decay = jnp.where(diff >= 0, jnp.exp(log_g[:, None, None] * jnp.maximum(diff, 0.0)), 0.0)
    zeta = jnp.exp(log_g[:, None] * (c - 1 - idx))
    xi = jnp.exp(log_g[:, None] * (idx + 1.0))
    cd = jnp.exp(log_g * c)

    def lanes(tab):
        return jnp.repeat(tab.reshape(B_HEADS // 2, 2, c), HEAD_DIM, axis=1).transpose(0, 2, 1)

    cdm = jnp.repeat(cd.reshape(B_HEADS // 2, 2), HEAD_DIM, axis=1)
    cdm = jnp.broadcast_to(cdm[:, :, None], (B_HEADS // 2, LANES, LANES))
    return decay, lanes(zeta), lanes(xi), cdm


def _retention(qkzv, gate, tables, ret_gn_g):
    b, s, _ = qkzv.shape
    rows = min(s, 4096)
    pairs = B_HEADS // 2
    decay, _, xi, cdm = tables
    gn = ret_gn_g.reshape(pairs, 1, LANES)

    def col(off):
        return pl.BlockSpec((None, rows, LANES), lambda i, p, j: (i, j, off * pairs + p))

    return pl.pallas_call(
        functools.partial(_retention_kernel, rows=rows),
        grid=(b, pairs, s // rows),
        in_specs=[col(0), col(1), col(2), col(3), col(0),
                  pl.BlockSpec((2, RET_CHUNK, RET_CHUNK), lambda i, p, j: (p, 0, 0)),
                  pl.BlockSpec((None, RET_CHUNK, LANES), lambda i, p, j: (p, 0, 0)),
                  pl.BlockSpec((None, LANES, LANES), lambda i, p, j: (p, 0, 0)),
                  pl.BlockSpec((None, 1, LANES), lambda i, p, j: (p, 0, 0))],
        out_specs=pl.BlockSpec((None, rows, LANES), lambda i, p, j: (i, j, p)),
        out_shape=jax.ShapeDtypeStruct((b, s, B_WIDTH), _BF16),
        scratch_shapes=[pltpu.VMEM((LANES, LANES), _F32)],
        compiler_params=_cparams(3),
        name="retention_b",
    )(qkzv, qkzv, qkzv, qkzv, gate, decay, xi, cdm, gn)


def _cross_kernel(q_ref, k_ref, vt_ref, o_ref, st_ref, *, rows):
    low = _lane((ATT_Q, LANES)) < HEAD_DIM
    lane_blocks = C_WIDTH // LANES
    tiles = [(j, lb) for j in range(rows // ATT_Q) for lb in range(lane_blocks)]
    ones = jnp.ones((ONES_ROWS, vt_ref.shape[1]), _BF16)

    def scores(i):
        j, lb = tiles[i]
        sl = slice(lb * LANES, (lb + 1) * LANES)
        q = q_ref[j * ATT_Q:(j + 1) * ATT_Q, sl]
        q2 = jnp.concatenate([jnp.where(low, q, jnp.zeros_like(q)), jnp.where(low, jnp.zeros_like(q), q)], axis=0)
        st_ref[i % (ATT_AHEAD + 1)] = _dot_nt(k_ref[:, sl], q2)

    def finish(i):
        j, lb = tiles[i]
        sl = slice(lb * LANES, (lb + 1) * LANES)
        st = st_ref[i % (ATT_AHEAD + 1)]
        p = jnp.exp2(st - jnp.max(st, axis=0, keepdims=True))
        ot = _dot(jnp.concatenate([vt_ref[sl, :], ones], axis=0), p.astype(_BF16))
        inv = 1.0 / ot[LANES:LANES + 1, :]
        out_t = jnp.concatenate([ot[0:HEAD_DIM, 0:ATT_Q] * inv[:, 0:ATT_Q],
                                 ot[HEAD_DIM:LANES, ATT_Q:] * inv[:, ATT_Q:]], axis=0)
        o_ref[j * ATT_Q:(j + 1) * ATT_Q, sl] = out_t.T.astype(o_ref.dtype)

    for i in range(min(ATT_AHEAD, len(tiles))):
        scores(i)
    for i in range(len(tiles)):
        if i + ATT_AHEAD < len(tiles):
            scores(i + ATT_AHEAD)
        finish(i)


def _cross_attention(qc, kc, vtc):
    b, s, _ = qc.shape
    m = kc.shape[1]
    rows = min(s, 4096)
    return pl.pallas_call(
        functools.partial(_cross_kernel, rows=rows),
        grid=(b, s // rows),
        in_specs=[pl.BlockSpec((None, rows, C_WIDTH), lambda i, j: (i, j, 0)),
                  pl.BlockSpec((None, m, C_WIDTH), lambda i, j: (i, 0, 0)),
                  pl.BlockSpec((None, C_WIDTH, m), lambda i, j: (i, 0, 0))],
        out_specs=pl.BlockSpec((None, rows, C_WIDTH), lambda i, j: (i, j, 0)),
        out_shape=jax.ShapeDtypeStruct((b, s, C_WIDTH), _BF16),
        scratch_shapes=[pltpu.VMEM((ATT_AHEAD + 1, m, 2 * ATT_Q), _F32)],
        compiler_params=_cparams(2),
        name="cross_c",
    )(qc, kc, vtc)


def _out_router_kernel(x_ref, a_ref, b_ref, c_ref, wo_ref, g_ref, wr_ref, br_ref,
                       h_ref, hn_ref, rows_ref, cnt_ref, carry_ref):
    @pl.when(pl.program_id(0) == 0)
    def _():
        carry_ref[...] = jnp.zeros_like(carry_ref)

    tm = x_ref.shape[0]
    h = x_ref[...]
    h = h + _dot(a_ref[...], wo_ref[0:A_WIDTH, :])
    h = h + _dot(b_ref[...], wo_ref[A_WIDTH:A_WIDTH + B_WIDTH, :])
    h = h + _dot(c_ref[...], wo_ref[A_WIDTH + B_WIDTH:, :])
    h_ref[...] = h
    ms = jnp.mean(h * h, axis=-1, keepdims=True)
    hn = (h * lax.rsqrt(ms + EPS)) * g_ref[...]
    _pack_rows(hn_ref, hn)
    logits = _dot_nt(wr_ref[...], hn.astype(_BF16))[0:ROUTE_ROWS, :] + br_ref[:, 0:1]
    row = lax.broadcasted_iota(jnp.int32, (ROUTE_ROWS, tm), 0).astype(_F32)
    big = float(ROUTE_ROWS)

    def first_row(mask):
        return jnp.min(jnp.where(mask, row, big), axis=0, keepdims=True)

    gmask = row < N_GROUPS
    gl = jnp.where(gmask, logits, NEG_INF)
    ge = jnp.exp(gl - jnp.max(gl, axis=0, keepdims=True))
    gp = ge / jnp.sum(ge, axis=0, keepdims=True)
    p_group = jnp.max(gp, axis=0, keepdims=True)
    g_sel = first_row(gmask & (gp == p_group))
    lo = ROUTE_LANE0 + g_sel * EXPERTS_PER_GROUP
    emask = (row >= lo) & (row < lo + EXPERTS_PER_GROUP)
    el = jnp.where(emask, logits, NEG_INF)
    ee = jnp.exp(el - jnp.max(el, axis=0, keepdims=True))
    ep = ee / jnp.sum(ee, axis=0, keepdims=True)
    p1 = jnp.max(ep, axis=0, keepdims=True)
    i1 = first_row(emask & (ep == p1))
    ep2 = jnp.where(emask & (row != i1), ep, -1.0)
    p2 = jnp.max(ep2, axis=0, keepdims=True)
    i2 = first_row(ep2 == p2)
    den = p1 + p2
    w1 = p_group * (p1 / den)
    w2 = p_group * (p2 / den)
    hit1 = row == i1
    hit2 = row == i2
    onehot = jnp.where(hit1 | hit2, 1.0, 0.0)
    r_i = lax.broadcasted_iota(jnp.int32, (tm, tm), 0)
    c_i = lax.broadcasted_iota(jnp.int32, (tm, tm), 1)
    earlier = jnp.where(r_i < c_i, 1.0, 0.0).astype(_BF16)
    before = _dot(onehot.astype(_BF16), earlier) + carry_ref[:, 0:1]
    r1 = jnp.sum(jnp.where(hit1, before, 0.0), axis=0, keepdims=True)
    r2 = jnp.sum(jnp.where(hit2, before, 0.0), axis=0, keepdims=True)
    carry_ref[...] = carry_ref[...] + jnp.sum(onehot, axis=1, keepdims=True)
    cnt_ref[...] = carry_ref[...]
    out_row = lax.broadcasted_iota(jnp.int32, (SUBLANES, tm), 0)
    info = jnp.where(out_row == 0, w1, 0.0)
    info = jnp.where(out_row == 1, w2, info)
    info = jnp.where(out_row == 2, i1 - ROUTE_LANE0, info)
    info = jnp.where(out_row == 3, i2 - ROUTE_LANE0, info)
    info = jnp.where(out_row == 4, r1, info)
    info = jnp.where(out_row == 5, r2, info)
    rows_ref[...] = info


def _out_router(x2, oa, ob, oc, w_out, ffn_g, w_rg, b_rg, w_re, b_re):
    t, d = x2.shape
    tm = min(t, 512)
    pad = LANES - N_GROUPS - N_EXPERTS
    wr = jnp.concatenate([w_rg, w_re, jnp.zeros((d, pad), _F32)], axis=1).T.astype(_BF16)
    br = jnp.concatenate([b_rg, b_re, jnp.zeros((ROUTE_ROWS - N_GROUPS - N_EXPERTS,), _F32)])
    br = jnp.broadcast_to(br[:, None], (ROUTE_ROWS, LANES))

    def rows(w):
        return pl.BlockSpec((tm, w), lambda i: (i, 0))

    def whole(r, c):
        return pl.BlockSpec((r, c), lambda i: (0, 0))

    return pl.pallas_call(
        _out_router_kernel,
        grid=(t // tm,),
        in_specs=[rows(d), rows(A_WIDTH), rows(B_WIDTH), rows(C_WIDTH), whole(d, d), whole(1, d),
                  whole(LANES, d), whole(ROUTE_ROWS, LANES)],
        out_specs=[rows(d), pl.BlockSpec((tm * PACK_ROWS, LANES), lambda i: (i, 0)),
                   pl.BlockSpec((SUBLANES, tm), lambda i: (0, i)), whole(ROUTE_ROWS, LANES)],
        out_shape=[jax.ShapeDtypeStruct((t, d), _F32), jax.ShapeDtypeStruct((t * PACK_ROWS, LANES), jnp.uint32),
                   jax.ShapeDtypeStruct((SUBLANES, t), _F32), jax.ShapeDtypeStruct((ROUTE_ROWS, LANES), _F32)],
        scratch_shapes=[pltpu.VMEM((ROUTE_ROWS, LANES), _F32)],
        compiler_params=_cparams(1),
        name="out_router",
    )(x2, oa, ob, oc, w_out.astype(_BF16), ffn_g.reshape(1, d), wr, br)


DISPATCH_TOKENS = 2048
COMBINE_TOKENS = 256


ROW_UNROLL = 8


def _tile_rows(row, count=1, per=SUBLANES):
    start = row * per
    if not isinstance(start, int):
        start = pl.multiple_of(start, per)
    return pl.ds(start, count * per)


def _row_copy(src, s_row, dst, d_row, sem, per=SUBLANES):
    return pltpu.make_async_copy(src.at[_tile_rows(s_row, 1, per)], dst.at[_tile_rows(d_row, 1, per)], sem)


def _dispatch_kernel(pad_start_ref, pad_len_ref, used_ref, dest_ref, hn_ref, xs_ref, zero_ref, sem,
                     pad_sem):
    per = PACK_ROWS
    n = hn_ref.shape[0] // per

    @pl.when(pl.program_id(0) == 0)
    def _():
        zero_ref[...] = jnp.zeros_like(zero_ref)
        n_blocks = xs_ref.shape[0] // (ROW_BLOCK * per)

        def block_copy(blk):
            return pltpu.make_async_copy(zero_ref, xs_ref.at[_tile_rows(blk * ROW_BLOCK, ROW_BLOCK, per)],
                                         pad_sem)

        def put_block(blk, carry):
            block_copy(blk).start()
            return carry

        def done_block(blk, carry):
            block_copy(blk).wait()
            return carry

        lax.fori_loop(used_ref[0], n_blocks, put_block, 0)
        lax.fori_loop(used_ref[0], n_blocks, done_block, 0)
        bits = [1 << k for k in reversed(range(ROW_BLOCK.bit_length() - 1))]

        def tail(e, wait):
            row = pad_start_ref[e]
            for bit in bits:
                on = (pad_len_ref[e] & bit) != 0
                copy = pltpu.make_async_copy(zero_ref.at[_tile_rows(0, bit, per)],
                                             xs_ref.at[_tile_rows(row, bit, per)], pad_sem)

                @pl.when(on)
                def _():
                    copy.wait() if wait else copy.start()

                row = row + jnp.where(on, bit, 0)

        def put_tail(e, carry):
            tail(e, False)
            return carry

        def done_tail(e, carry):
            tail(e, True)
            return carry

        lax.fori_loop(0, N_EXPERTS, put_tail, 0)
        lax.fori_loop(0, N_EXPERTS, done_tail, 0)

    def issue(i, carry):
        for u in range(ROW_UNROLL):
            t = i * ROW_UNROLL + u
            _row_copy(hn_ref, t, xs_ref, dest_ref[2 * t], sem, per).start(priority=0)
            _row_copy(hn_ref, t, xs_ref, dest_ref[2 * t + 1], sem, per).start(priority=1)
        return carry

    lax.fori_loop(0, n // ROW_UNROLL, issue, 0)
    for _ in range(2):
        pltpu.make_async_copy(hn_ref, xs_ref.at[_tile_rows(0, n, per)], sem).wait()


def _dispatch(hn, dest, pad_start, pad_len, n_used, n_rows):
    t = hn.shape[0] // PACK_ROWS
    n = min(t, DISPATCH_TOKENS)
    return pl.pallas_call(
        _dispatch_kernel,
        grid_spec=pltpu.PrefetchScalarGridSpec(
            num_scalar_prefetch=3,
            grid=(t // n,),
            in_specs=[pl.BlockSpec((2 * n,), lambda i, *_: (i,), memory_space=pltpu.SMEM),
                      pl.BlockSpec((n * PACK_ROWS, LANES), lambda i, *_: (i, 0))],
            out_specs=pl.BlockSpec(memory_space=pl.ANY),
            scratch_shapes=[pltpu.VMEM((ROW_BLOCK * PACK_ROWS, LANES), hn.dtype), pltpu.SemaphoreType.DMA,
                            pltpu.SemaphoreType.DMA]),
        out_shape=jax.ShapeDtypeStruct((n_rows * PACK_ROWS, LANES), hn.dtype),
        compiler_params=_cparams(1),
        name="moe_dispatch",
    )(pad_start, pad_len, n_used, dest, hn)


def _expert_kernel(be_ref, run_ref, next_ref, used_ref, x_ref, wg_hbm, wu_hbm, wd_hbm, y_ref,
                   wg_f32, wu_f32, wd_f32, wg_bf, wu_bf, wd_bf, sem):
    i = pl.program_id(0)
    live = i < used_ref[0]
    new_expert = (i == 0) | (be_ref[i] != be_ref[jnp.maximum(i - 1, 0)])
    slot = run_ref[i] % 2

    def fetch(expert, to_slot):
        return [pltpu.make_async_copy(src.at[expert], dst.at[to_slot], sem.at[to_slot, k])
                for k, (src, dst) in enumerate(((wg_hbm, wg_f32), (wu_hbm, wu_f32), (wd_hbm, wd_f32)))]

    @pl.when(live & (i == 0))
    def _():
        for copy in fetch(be_ref[0], 0):
            copy.start()

    @pl.when(live & new_expert)
    def _():
        for copy in fetch(be_ref[i], slot):
            copy.wait()

        @pl.when(next_ref[i] >= 0)
        def _():
            for copy in fetch(next_ref[i], 1 - slot):
                copy.start()

        wg_bf[...] = wg_f32[slot].astype(_BF16)
        wu_bf[...] = wu_f32[slot].astype(_BF16)
        wd_bf[...] = wd_f32[slot].astype(_BF16)

    @pl.when(live)
    def _():
        sub = ROW_BLOCK // EXPERT_SPLIT
        gate_up = {}

        def first(k):
            x = _unpack_rows(x_ref, sub, k * sub)
            gate_up[k] = (_dot(x, wg_bf[...]), _dot(x, wu_bf[...]))

        def second(k):
            gate, up = gate_up.pop(k)
            act = (gate * jax.nn.sigmoid(gate)) * up
            _rows_to_tiles(y_ref, _dot(act.astype(_BF16), wd_bf[...]), k * sub)

        for k in range(min(EXPERT_AHEAD, EXPERT_SPLIT)):
            first(k)
        for k in range(EXPERT_SPLIT):
            if k + EXPERT_AHEAD < EXPERT_SPLIT:
                first(k + EXPERT_AHEAD)
            second(k)

    @pl.when(i >= used_ref[0])
    def _():
        y_ref[...] = jnp.zeros_like(y_ref)


def _experts(xs, blocks, w_gate, w_up, w_down):
    n_rows, d = xs.shape[0] // PACK_ROWS, D_MODEL
    n_blocks = n_rows // ROW_BLOCK
    tile_block = (ROW_BLOCK * SUBLANES, LANES)
    hbm = pl.BlockSpec(memory_space=pl.ANY)

    return pl.pallas_call(
        _expert_kernel,
        grid_spec=pltpu.PrefetchScalarGridSpec(
            num_scalar_prefetch=4,
            grid=(n_blocks,),
            in_specs=[pl.BlockSpec((ROW_BLOCK * PACK_ROWS, LANES),
                                   lambda i, be, run, nxt, used: (jnp.minimum(i, used[0] - 1), 0)),
                      hbm, hbm, hbm],
            out_specs=pl.BlockSpec(tile_block, lambda i, *_: (i, 0)),
            scratch_shapes=[pltpu.VMEM((2, d, D_EXPERT), _F32), pltpu.VMEM((2, d, D_EXPERT), _F32),
                            pltpu.VMEM((2, D_EXPERT, d), _F32),
                            pltpu.VMEM((d, D_EXPERT), _BF16), pltpu.VMEM((d, D_EXPERT), _BF16),
                            pltpu.VMEM((D_EXPERT, d), _BF16), pltpu.SemaphoreType.DMA((2, 3))]),
        out_shape=jax.ShapeDtypeStruct((n_rows * SUBLANES, LANES), _F32),
        compiler_params=_cparams(1),
        name="moe_experts",
    )(*blocks, xs, w_gate, w_up, w_down)


def _combine_kernel(dest_ref, next_ref, h_ref, rows_ref, ys_ref, o_ref, buf_ref, sem):
    n = h_ref.shape[0]
    step = pl.program_id(0)
    slot = step % 2

    def gather(idx_ref, to_slot):
        def issue(i, carry):
            for u in range(ROW_UNROLL):
                t = i * ROW_UNROLL + u
                _row_copy(ys_ref, idx_ref[2 * t], buf_ref.at[to_slot, 0], t,
                          sem.at[to_slot]).start(priority=0)
                _row_copy(ys_ref, idx_ref[2 * t + 1], buf_ref.at[to_slot, 1], t,
                          sem.at[to_slot]).start(priority=1)
            return carry

        lax.fori_loop(0, n // ROW_UNROLL, issue, 0)

    @pl.when(step == 0)
    def _():
        gather(dest_ref, 0)

    @pl.when(step + 1 < pl.num_programs(0))
    def _():
        gather(next_ref, 1 - slot)

    for k in range(2):
        pltpu.make_async_copy(ys_ref.at[_tile_rows(0, n)], buf_ref.at[slot, k], sem.at[slot]).wait()
    info = jnp.concatenate([rows_ref[...], jnp.zeros((LANES - SUBLANES, n), _F32)], axis=0).T
    w0 = info[:, 0:1]
    w1 = info[:, 1:2]
    for s in range(SUBLANES):
        sl = slice(s * LANES, (s + 1) * LANES)
        moe = w0 * _tile_block(buf_ref.at[slot, 0], s, n) + w1 * _tile_block(buf_ref.at[slot, 1], s, n)
        o_ref[:, sl] = h_ref[:, sl] + moe


def _combine(h, route_rows, ys, dest):
    t, d = h.shape
    n = min(t, COMBINE_TOKENS)
    steps = t // n
    return pl.pallas_call(
        _combine_kernel,
        grid=(steps,),
        in_specs=[pl.BlockSpec((2 * n,), lambda i: (i,), memory_space=pltpu.SMEM),
                  pl.BlockSpec((2 * n,), lambda i: (jnp.minimum(i + 1, steps - 1),),
                               memory_space=pltpu.SMEM),
                  pl.BlockSpec((n, d), lambda i: (i, 0)),
                  pl.BlockSpec((SUBLANES, n), lambda i: (0, i)),
                  pl.BlockSpec(memory_space=pl.ANY)],
        out_specs=pl.BlockSpec((n, d), lambda i: (i, 0)),
        out_shape=jax.ShapeDtypeStruct((t, d), _F32),
        scratch_shapes=[pltpu.VMEM((2, 2, n * SUBLANES, LANES), _F32), pltpu.SemaphoreType.DMA((2,))],
        compiler_params=_cparams(1),
        name="moe_combine",
    )(dest, dest, h, route_rows, ys)


def _moe_layout(route_rows, counts, t):
    counts = counts[ROUTE_LANE0:ROUTE_LANE0 + N_EXPERTS, 0].astype(jnp.int32)
    padded = (counts + ROW_BLOCK - 1) // ROW_BLOCK * ROW_BLOCK
    pends = jnp.cumsum(padded)
    pstarts = pends - padded
    eid = route_rows[2:4].astype(jnp.int32)
    rank = route_rows[4:6].astype(jnp.int32)
    experts = jnp.arange(N_EXPERTS, dtype=jnp.int32)
    start_of = jnp.sum(jnp.where(eid[:, :, None] == experts, pstarts, 0), axis=-1)
    dest = (start_of + rank).T.reshape(-1)
    n_blocks = -(-2 * t // ROW_BLOCK) + N_EXPERTS
    first_row = jnp.arange(n_blocks, dtype=jnp.int32) * ROW_BLOCK
    block_e = jnp.minimum(jnp.sum((pends[None, :] <= first_row[:, None]).astype(jnp.int32), axis=1),
                          N_EXPERTS - 1)
    n_used = (pends[-1:] // ROW_BLOCK).astype(jnp.int32)
    changed = jnp.concatenate([jnp.zeros((1,), jnp.int32), (block_e[1:] != block_e[:-1]).astype(jnp.int32)])
    block_run = jnp.cumsum(changed)
    later = (counts[None, :] > 0) & (experts[None, :] > experts[:, None])
    next_expert = jnp.min(jnp.where(later, experts[None, :], N_EXPERTS), axis=1)
    next_expert = jnp.where(next_expert < N_EXPERTS, next_expert, -1)
    block_next = jnp.sum(jnp.where(block_e[:, None] == experts[None, :], next_expert[None, :], 0), axis=1)
    blocks = (block_e, block_run.astype(jnp.int32), block_next.astype(jnp.int32), n_used)
    return dest, blocks, pstarts + counts, padded - counts, n_blocks * ROW_BLOCK


def kernel(x, mem, positions, mix_norm_g, w_in, qn_a, kn_a, rel_bias, ret_gn_g, mem_norm_g, w_mem_kv,
           qn_c, kn_c, w_out, ffn_norm_g, w_router_group, b_router_group, w_router_expert,
           b_router_expert, w_gate, w_up, w_down):
    b, s, d = x.shape
    t = b * s
    x2 = x.reshape(t, d)
    kc, vc = _mem_kv(mem, mem_norm_g, w_mem_kv, kn_c)
    tables = _retention_tables()
    qa, ka, vta, qkzv, gate, qc = _in_proj(x, positions, tables[1], mix_norm_g, w_in, qn_a, kn_a, qn_c)
    out_a = _attention(qa, ka, vta, rel_bias)
    out_b = _retention(qkzv, gate, tables, ret_gn_g)
    out_c = _cross_attention(qc, kc, vc)
    h, hn, route_rows, counts = _out_router(
        x2, out_a.reshape(t, A_WIDTH), out_b.reshape(t, B_WIDTH), out_c.reshape(t, C_WIDTH),
        w_out, ffn_norm_g, w_router_group, b_router_group, w_router_expert, b_router_expert)
    dest, blocks, pad_start, pad_len, n_rows = _moe_layout(route_rows, counts, t)
    xs = _dispatch(hn, dest, pad_start, pad_len, blocks[-1], n_rows)
    ys = _experts(xs, blocks, w_gate, w_up, w_down)
    return _combine(h, route_rows, ys, dest).reshape(b, s, d)
```

```python
import functools

import jax
import jax.numpy as jnp
from jax import lax
from jax.experimental import pallas as pl
from jax.experimental.pallas import tpu as pltpu

D_MODEL = 1024
CHUNK = 64
HEAD_DIM = 64
A_HEADS = 8
B_HEADS = 4
C_HEADS = 4
A_WIDTH = A_HEADS * HEAD_DIM
B_WIDTH = B_HEADS * HEAD_DIM
C_WIDTH = C_HEADS * HEAD_DIM
IN_COLS = 3 * A_WIDTH + 4 * B_WIDTH + C_WIDTH
LEFT_CHUNKS = 8
BAND_CHUNKS = LEFT_CHUNKS + 1
MAX_REL_DIST = 128
ROPE_BASE = 10000.0
N_GROUPS = 4
EXPERTS_PER_GROUP = 8
N_EXPERTS = N_GROUPS * EXPERTS_PER_GROUP
D_EXPERT = D_MODEL // 2
EPS = 1e-6
NEG_INF = -1e30
LOG2E = 1.4426950408889634

LANES = 128
SUBLANES = 8
assert D_MODEL == SUBLANES * LANES
PACK_ROWS = SUBLANES // 2
LEFT_ROWS = LEFT_CHUNKS * CHUNK
ATT_Q = 2 * CHUNK
ATT_K = ATT_Q + LEFT_ROWS
ATT_VARIANTS = LEFT_ROWS // ATT_Q + 1
ONES_ROWS = 16
ATT_AHEAD = 3
RET_CHUNK = 256
ROW_BLOCK = 512
EXPERT_SPLIT = 2
EXPERT_AHEAD = 2
ROUTE_LANE0 = N_GROUPS
ROUTE_ROWS = 64
VMEM_LIMIT = 48 * 1024 * 1024

_F32 = jnp.float32
_BF16 = jnp.bfloat16


def _cparams(n_axes):
    return pltpu.CompilerParams(dimension_semantics=("arbitrary",) * n_axes,
                                vmem_limit_bytes=VMEM_LIMIT)


def _dot(a, b):
    return jnp.dot(a, b, preferred_element_type=_F32)


def _dot_nt(a, b):
    return lax.dot_general(a, b, (((1,), (1,)), ((), ())), preferred_element_type=_F32)


def _lane(shape):
    return lax.broadcasted_iota(jnp.int32, shape, len(shape) - 1)


def _pair_rms(t, gain):
    low = _lane(t.shape) < HEAD_DIM
    t2 = t * t
    ms0 = jnp.sum(jnp.where(low, t2, 0.0), axis=-1, keepdims=True) * (1.0 / HEAD_DIM)
    ms1 = jnp.sum(jnp.where(low, 0.0, t2), axis=-1, keepdims=True) * (1.0 / HEAD_DIM)
    r = jnp.where(low, lax.rsqrt(ms0 + EPS), lax.rsqrt(ms1 + EPS))
    return (t * r) * gain


def _rows_to_tiles(ref, val, row0=0):
    n = val.shape[0]
    for s in range(SUBLANES):
        ref[pl.ds(row0 * SUBLANES + s, n, stride=SUBLANES), :] = val[:, s * LANES:(s + 1) * LANES]


def _tile_block(ref, s, n, row0=0):
    return ref[pl.ds(row0 * SUBLANES + s, n, stride=SUBLANES), :]


def _pack_rows(ref, val, row0=0):
    n = val.shape[0]
    for s in range(PACK_ROWS):
        lo = val[:, (2 * s) * LANES:(2 * s + 1) * LANES].astype(_BF16).astype(_F32)
        hi = val[:, (2 * s + 1) * LANES:(2 * s + 2) * LANES].astype(_BF16).astype(_F32)
        word = (lax.bitcast_convert_type(lo, jnp.uint32) >> 16) | (
            lax.bitcast_convert_type(hi, jnp.uint32) & jnp.uint32(0xFFFF0000))
        ref[pl.ds(row0 * PACK_ROWS + s, n, stride=PACK_ROWS), :] = word


def _unpack_rows(ref, n, row0=0):
    parts = []
    for s in range(PACK_ROWS):
        word = ref[pl.ds(row0 * PACK_ROWS + s, n, stride=PACK_ROWS), :]
        parts.append(lax.bitcast_convert_type(word << 16, _F32))
        parts.append(lax.bitcast_convert_type(word & jnp.uint32(0xFFFF0000), _F32))
    return jnp.concatenate(parts, axis=-1).astype(_BF16)


ROPE_HALF = HEAD_DIM // 2
ROPE_PACK = LANES // ROPE_HALF


def _rope_tables(pos_ref, inv_ref, cos_ref, sin_ref):
    ang = pos_ref[...].astype(_F32) * inv_ref[...]
    rows = ang.shape[0]
    lane = _lane(ang.shape)
    sign = jnp.where((lane % HEAD_DIM) < ROPE_HALF, -1.0, 1.0)
    for out_ref, val in ((cos_ref, jnp.cos(ang)), (sin_ref, jnp.sin(ang))):
        for j in range(ROPE_PACK):
            seg = jnp.where(lane // ROPE_HALF == j, val, 0.0)
            full = seg
            for k in range(1, ROPE_PACK):
                full = full + pltpu.roll(seg, k * ROPE_HALF, 1)
            if out_ref is sin_ref:
                full = full * sign
            out_ref[pl.ds(j, rows, stride=ROPE_PACK), :] = full


def _rope_inputs(positions):
    b, s = positions.shape
    inv = ROPE_BASE ** (-jnp.arange(ROPE_HALF, dtype=_F32) / ROPE_HALF)
    inv128 = jnp.tile(inv, ROPE_PACK).reshape(1, LANES)
    pos = jnp.repeat(positions.reshape(b, s // ROPE_PACK, ROPE_PACK), ROPE_HALF, axis=2)
    return pos, inv128


def _mem_kv_kernel(mem_ref, g_ref, w_ref, kn_ref, k_ref, v_ref):
    m = mem_ref[...]
    ms = jnp.mean(m * m, axis=-1, keepdims=True)
    mn = (m * lax.rsqrt(ms + EPS)) * g_ref[...]
    kv = _dot(mn.astype(_BF16), w_ref[...])
    for j in range(C_WIDTH // LANES):
        sl = slice(j * LANES, (j + 1) * LANES)
        k_ref[:, sl] = _pair_rms(kv[:, sl], kn_ref[...]).astype(_BF16)
    v_ref[...] = kv[:, C_WIDTH:].T.astype(_BF16)


def _mem_kv(mem, mem_norm_g, w_mem_kv, kn_c):
    b, m, d = mem.shape
    kn = jnp.tile(kn_c, 2).reshape(1, LANES)
    return pl.pallas_call(
        _mem_kv_kernel,
        grid=(b,),
        in_specs=[pl.BlockSpec((None, m, d), lambda i: (i, 0, 0)),
                  pl.BlockSpec((1, d), lambda i: (0, 0)),
                  pl.BlockSpec((d, 2 * C_WIDTH), lambda i: (0, 0)),
                  pl.BlockSpec((1, LANES), lambda i: (0, 0))],
        out_specs=[pl.BlockSpec((None, m, C_WIDTH), lambda i: (i, 0, 0)),
                   pl.BlockSpec((None, C_WIDTH, m), lambda i: (i, 0, 0))],
        out_shape=[jax.ShapeDtypeStruct((b, m, C_WIDTH), _BF16),
                   jax.ShapeDtypeStruct((b, C_WIDTH, m), _BF16)],
        compiler_params=_cparams(1),
        name="mem_kv",
    )(mem, mem_norm_g.reshape(1, d), w_mem_kv.astype(_BF16), kn)


def _in_proj_kernel(x_ref, pos_ref, inv_ref, g_ref, wq_ref, wk_ref, wvt_ref, wr_ref, wc_ref, qn_ref, kn_ref,
                    cn_ref, zeta_ref, qa_ref, ka_ref, vt_ref, ret_ref, gate_ref, qc_ref, xn_ref, acc_ref,
                    accb_ref, cos_ref, sin_ref):
    x = x_ref[...]
    ms = jnp.mean(x * x, axis=-1, keepdims=True)
    xn_ref[...] = ((x * lax.rsqrt(ms + EPS)) * g_ref[...]).astype(_BF16)

    def normed(slot, out_ref, gain_ref):
        for blk in range(out_ref.shape[1] // LANES):
            sl = slice(blk * LANES, (blk + 1) * LANES)
            out_ref[:, sl] = _pair_rms(acc_ref[slot, :, sl], gain_ref[...]).astype(_BF16)

    acc_ref[0] = _dot(xn_ref[...], wq_ref[...])
    acc_ref[1] = _dot(xn_ref[...], wk_ref[...])
    _rope_tables(pos_ref, inv_ref, cos_ref, sin_ref)
    normed(0, qa_ref, qn_ref)
    acc_ref[0] = _dot_nt(wvt_ref[...], xn_ref[...])
    normed(1, ka_ref, kn_ref)
    accb_ref[...] = _dot(xn_ref[...], wr_ref[...])
    for blk in range(vt_ref.shape[0]):
        vt_ref[blk] = acc_ref[0, :, blk * LANES:(blk + 1) * LANES].astype(_BF16)
    acc_ref[1, :, 0:C_WIDTH] = _dot(xn_ref[...], wc_ref[...])
    cos, sin = cos_ref[...], sin_ref[...]
    chunks = x_ref.shape[0] // RET_CHUNK
    for p in range(B_WIDTH // LANES):
        sl = slice(p * LANES, (p + 1) * LANES)
        q = accb_ref[:, sl]
        k = accb_ref[:, B_WIDTH + p * LANES:B_WIDTH + (p + 1) * LANES]
        kr = (k * cos + _swap_halves(k) * sin) * (HEAD_DIM ** -0.5)
        ret_ref[:, sl] = (q * cos + _swap_halves(q) * sin).astype(_BF16)
        ret_ref[:, B_WIDTH + p * LANES:B_WIDTH + (p + 1) * LANES] = kr.astype(_BF16)
        ret_ref[:, 2 * B_WIDTH + p * LANES:2 * B_WIDTH + (p + 1) * LANES] = (
            kr * jnp.concatenate([zeta_ref[p]] * chunks, axis=0)).astype(_BF16)
        ret_ref[:, 3 * B_WIDTH + p * LANES:3 * B_WIDTH + (p + 1) * LANES] = accb_ref[
            :, 2 * B_WIDTH + p * LANES:2 * B_WIDTH + (p + 1) * LANES].astype(_BF16)
        gate_ref[:, sl] = accb_ref[:, 3 * B_WIDTH + p * LANES:3 * B_WIDTH + (p + 1) * LANES]
    normed(1, qc_ref, cn_ref)


def _in_proj(x3, positions, zeta, g, w_in, qn_a, kn_a, qn_c):
    b, s, d = x3.shape
    tm = min(s, 512)
    assert tm == A_WIDTH
    assert tm % RET_CHUNK == 0
    pos, inv128 = _rope_inputs(positions)
    w = w_in.astype(_BF16)
    cuts = [0, A_WIDTH, 2 * A_WIDTH, 3 * A_WIDTH, 3 * A_WIDTH + 4 * B_WIDTH, IN_COLS]
    wq, wk, wv, wr, wc = (w[:, lo:hi] for lo, hi in zip(cuts[:-1], cuts[1:]))
    scale = HEAD_DIM ** -0.5 * LOG2E
    gains = [(jnp.tile(gn, 2) * sc).reshape(1, LANES) for gn, sc in ((qn_a, scale), (kn_a, 1.0), (qn_c, scale))]

    def whole(arr):
        return pl.BlockSpec(arr.shape, lambda i, j: (0,) * arr.ndim)

    def rows(width):
        return pl.BlockSpec((None, tm, width), lambda i, j: (i, j, 0))

    consts = [inv128, g.reshape(1, d), wq, wk, wv.T, wr, wc] + gains + [zeta]
    return pl.pallas_call(
        _in_proj_kernel,
        grid=(b, s // tm),
        in_specs=[rows(d), pl.BlockSpec((None, tm // ROPE_PACK, LANES), lambda i, j: (i, j, 0))]
        + [whole(c) for c in consts],
        out_specs=[rows(A_WIDTH), rows(A_WIDTH),
                   pl.BlockSpec((None, tm // LANES, A_WIDTH, LANES), lambda i, j: (i, j, 0, 0)),
                   rows(4 * B_WIDTH), rows(B_WIDTH), rows(C_WIDTH)],
        out_shape=[jax.ShapeDtypeStruct((b, s, A_WIDTH), _BF16), jax.ShapeDtypeStruct((b, s, A_WIDTH), _BF16),
                   jax.ShapeDtypeStruct((b, s // LANES, A_WIDTH, LANES), _BF16),
                   jax.ShapeDtypeStruct((b, s, 4 * B_WIDTH), _BF16), jax.ShapeDtypeStruct((b, s, B_WIDTH), _F32),
                   jax.ShapeDtypeStruct((b, s, C_WIDTH), _BF16)],
        scratch_shapes=[pltpu.VMEM((tm, d), _BF16), pltpu.VMEM((2, tm, A_WIDTH), _F32),
                        pltpu.VMEM((tm, 4 * B_WIDTH), _F32), pltpu.VMEM((tm, LANES), _F32),
                        pltpu.VMEM((tm, LANES), _F32)],
        compiler_params=_cparams(2),
        name="in_proj",
    )(x3, pos, *consts)


def _attn_kernel(q_ref, k_ref, vt_ref, bias_ref, o_ref, kp_ref, st_ref, var_ref, *, q_rows):
    qs = pl.program_id(2)
    s = k_ref.shape[0]
    fill_rows = min(s, 1024)
    left_blocks = LEFT_ROWS // LANES

    @pl.when(qs == 0)
    def _():
        kp_ref[0:LEFT_ROWS, :] = jnp.zeros((LEFT_ROWS, LANES), _BF16)

        def fill(i, carry):
            r = pl.multiple_of(i * fill_rows, fill_rows)
            kp_ref[pl.ds(LEFT_ROWS + r, fill_rows), :] = k_ref[pl.ds(r, fill_rows), :]
            return carry

        lax.fori_loop(0, s // fill_rows, fill, 0)
        key = lax.broadcasted_iota(jnp.int32, (ATT_K, 2 * ATT_Q), 0)
        for v in range(ATT_VARIANTS):
            var_ref[v] = jnp.where(key >= LEFT_ROWS - ATT_Q * v, bias_ref[...], NEG_INF)

    low = _lane((ATT_Q, LANES)) < HEAD_DIM
    ones = jnp.ones((ONES_ROWS, ATT_K), _BF16)
    tiles_per_step = q_rows // ATT_Q

    def scores(j):
        cp = qs * tiles_per_step + j
        q = q_ref[j * ATT_Q:(j + 1) * ATT_Q, :]
        q2 = jnp.concatenate([jnp.where(low, q, jnp.zeros_like(q)), jnp.where(low, jnp.zeros_like(q), q)], axis=0)
        kb = kp_ref[pl.ds(pl.multiple_of(cp * ATT_Q, ATT_Q), ATT_K), :]
        st_ref[j % (ATT_AHEAD + 1)] = _dot_nt(kb, q2) + var_ref[jnp.minimum(cp, ATT_VARIANTS - 1)]

    def finish(j):
        cp = qs * tiles_per_step + j
        st = st_ref[j % (ATT_AHEAD + 1)]
        m = jnp.max(st, axis=0, keepdims=True)
        p = jnp.exp2(st - m)
        vt = jnp.concatenate([vt_ref[jnp.maximum(cp + kb_i - left_blocks, 0)] for kb_i in range(ATT_K // LANES)],
                             axis=1)
        ot = _dot(jnp.concatenate([vt, ones], axis=0), p.astype(_BF16))
        inv = 1.0 / ot[LANES:LANES + 1, :]
        out_t = jnp.concatenate([ot[0:HEAD_DIM, 0:ATT_Q] * inv[:, 0:ATT_Q],
                                 ot[HEAD_DIM:LANES, ATT_Q:] * inv[:, ATT_Q:]], axis=0)
        o_ref[j * ATT_Q:(j + 1) * ATT_Q, :] = out_t.T.astype(o_ref.dtype)

    for j in range(min(ATT_AHEAD, tiles_per_step)):
        scores(j)
    for j in range(tiles_per_step):
        if j + ATT_AHEAD < tiles_per_step:
            scores(j + ATT_AHEAD)
        finish(j)


def _toeplitz_bias(rel_bias, q_len, k_len):
    h, table = rel_bias.shape
    n_diag = q_len + k_len - 1
    flat_lo = k_len - 1 - LEFT_ROWS - (CHUNK - 1)
    flat_hi = n_diag - flat_lo - table
    rev = jnp.concatenate([jnp.broadcast_to(rel_bias[:, -1:], (h, flat_hi)), rel_bias[:, ::-1],
                           jnp.broadcast_to(rel_bias[:, :1], (h, flat_lo))], axis=1).astype(_F32)
    flat = jnp.tile(rev, (1, q_len + 1))
    pitch = n_diag - 1
    skew = flat[:, q_len - 1:q_len - 1 + q_len * pitch].reshape(h, q_len, pitch)
    return skew[:, :, :k_len]


def _attn_bias(rel_bias):
    h = rel_bias.shape[0]
    bias = _toeplitz_bias(rel_bias, ATT_Q, ATT_K)
    q = lax.broadcasted_iota(jnp.int32, (ATT_Q, ATT_K), 0)
    k = lax.broadcasted_iota(jnp.int32, (ATT_Q, ATT_K), 1)
    off = k // CHUNK - q // CHUNK
    in_band = (off >= 0) & (off < BAND_CHUNKS)
    full = jnp.where(in_band[None], bias * LOG2E, NEG_INF)
    full = full.reshape(h // 2, 2, ATT_Q, ATT_K)
    return full.transpose(0, 3, 1, 2).reshape(h // 2, ATT_K, 2 * ATT_Q)


def _attention(qa, ka, vta, rel_bias):
    b, s, _ = qa.shape
    q_rows = min(s, 8192)
    pairs = A_HEADS // 2
    return pl.pallas_call(
        functools.partial(_attn_kernel, q_rows=q_rows),
        grid=(b, pairs, s // q_rows),
        in_specs=[pl.BlockSpec((None, q_rows, LANES), lambda i, p, j: (i, j, p)),
                  pl.BlockSpec((None, s, LANES), lambda i, p, j: (i, 0, p)),
                  pl.BlockSpec((None, s // LANES, LANES, LANES), lambda i, p, j: (i, 0, p, 0)),
                  pl.BlockSpec((None, ATT_K, 2 * ATT_Q), lambda i, p, j: (p, 0, 0))],
        out_specs=pl.BlockSpec((None, q_rows, LANES), lambda i, p, j: (i, j, p)),
        out_shape=jax.ShapeDtypeStruct((b, s, A_WIDTH), _BF16),
        scratch_shapes=[pltpu.VMEM((s + LEFT_ROWS, LANES), _BF16),
                        pltpu.VMEM((ATT_AHEAD + 1, ATT_K, 2 * ATT_Q), _F32),
                        pltpu.VMEM((ATT_VARIANTS, ATT_K, 2 * ATT_Q), _F32)],
        compiler_params=_cparams(3),
        name="attn_a",
    )(qa, ka, vta, _attn_bias(rel_bias))


def _swap_halves(t):
    first = (_lane(t.shape) % HEAD_DIM) < (HEAD_DIM // 2)
    return jnp.where(first, pltpu.roll(t, LANES - HEAD_DIM // 2, 1), pltpu.roll(t, HEAD_DIM // 2, 1))


def _retention_kernel(q_ref, k_ref, kz_ref, v_ref, gate_ref, decay_ref, xi_ref, cd_ref, gn_ref, o_ref,
                      state_ref, *, rows):
    @pl.when(pl.program_id(2) == 0)
    def _():
        state_ref[...] = jnp.zeros_like(state_ref)

    c = RET_CHUNK
    low = _lane((c, LANES)) < HEAD_DIM
    eye = jnp.where(lax.broadcasted_iota(jnp.int32, (LANES, LANES), 0) == _lane((LANES, LANES)),
                    1.0, 0.0).astype(_BF16)
    srow = lax.broadcasted_iota(jnp.int32, (LANES, LANES), 0) < HEAD_DIM
    scol = _lane((LANES, LANES)) < HEAD_DIM
    same_head = srow == scol

    for j in range(rows // c):
        sl = slice(j * c, (j + 1) * c)
        qb = q_ref[sl, :]
        kb = k_ref[sl, :]
        vb = v_ref[sl, :]
        inner_out = []
        for h in range(2):
            qh = jnp.where(low if h == 0 else ~low, qb, jnp.zeros_like(qb))
            inner = _dot_nt(qh, kb) * decay_ref[h]
            inner_out.append(_dot(inner.astype(_BF16), vb))
        state = state_ref[...]
        cross = _dot(qb, state.astype(_BF16)) * xi_ref[...]
        o = jnp.where(low, inner_out[0], inner_out[1]) + cross
        kz = _dot_nt(eye, kz_ref[sl, :]).astype(_BF16)
        state_ref[...] = cd_ref[...] * state + jnp.where(same_head, _dot(kz, vb), 0.0)
        mu = jnp.where(low,
                       jnp.sum(jnp.where(low, o, 0.0), axis=-1, keepdims=True),
                       jnp.sum(jnp.where(low, 0.0, o), axis=-1, keepdims=True)) * (1.0 / HEAD_DIM)
        dlt = o - mu
        d2 = dlt * dlt
        var = jnp.where(low,
                        jnp.sum(jnp.where(low, d2, 0.0), axis=-1, keepdims=True),
                        jnp.sum(jnp.where(low, 0.0, d2), axis=-1, keepdims=True)) * (1.0 / HEAD_DIM)
        y = (dlt * lax.rsqrt(var + EPS)) * gn_ref[...]
        g = gate_ref[sl, :]
        o_ref[sl, :] = ((g * jax.nn.sigmoid(g)) * y).astype(o_ref.dtype)


def _retention_tables():
    c = RET_CHUNK
    log_g = jnp.log(1.0 - jnp.exp2(-5.0 - jnp.arange(B_HEADS, dtype=_F32)))
    idx = jnp.arange(c, dtype=_F32)
    diff = idx[:, None] - idx[None, :]
    decay = jnp.where(diff >= 0, jnp.exp(log_g[:, None, None] * jnp.maximum(diff, 0.0)), 0.0)
    zeta = jnp.exp(log_g[:, None] * (c - 1 - idx))
    xi = jnp.exp(log_g[:, None] * (idx + 1.0))
    cd = jnp.exp(log_g * c)

    def lanes(tab):
        return jnp.repeat(tab.reshape(B_HEADS // 2, 2, c), HEAD_DIM, axis=1).transpose(0, 2, 1)

    cdm = jnp.repeat(cd.reshape(B_HEADS // 2, 2), HEAD_DIM, axis=1)
    cdm = jnp.broadcast_to(cdm[:, :, None], (B_HEADS // 2, LANES, LANES))
    return decay, lanes(zeta), lanes(xi), cdm


def _retention(qkzv, gate, tables, ret_gn_g):
    b, s, _ = qkzv.shape
    rows = min(s, 4096)
    pairs = B_HEADS // 2
    decay, _, xi, cdm = tables
    gn = ret_gn_g.reshape(pairs, 1, LANES)

    def col(off):
        return pl.BlockSpec((None, rows, LANES), lambda i, p, j: (i, j, off * pairs + p))

    return pl.pallas_call(
        functools.partial(_retention_kernel, rows=rows),
        grid=(b, pairs, s // rows),
        in_specs=[col(0), col(1), col(2), col(3), col(0),
                  pl.BlockSpec((2, RET_CHUNK, RET_CHUNK), lambda i, p, j: (p, 0, 0)),
                  pl.BlockSpec((None, RET_CHUNK, LANES), lambda i, p, j: (p, 0, 0)),
                  pl.BlockSpec((None, LANES, LANES), lambda i, p, j: (p, 0, 0)),
                  pl.BlockSpec((None, 1, LANES), lambda i, p, j: (p, 0, 0))],
        out_specs=pl.BlockSpec((None, rows, LANES), lambda i, p, j: (i, j, p)),
        out_shape=jax.ShapeDtypeStruct((b, s, B_WIDTH), _BF16),
        scratch_shapes=[pltpu.VMEM((LANES, LANES), _F32)],
        compiler_params=_cparams(3),
        name="retention_b",
    )(qkzv, qkzv, qkzv, qkzv, gate, decay, xi, cdm, gn)


def _cross_kernel(q_ref, k_ref, vt_ref, o_ref, st_ref, *, rows):
    low = _lane((ATT_Q, LANES)) < HEAD_DIM
    lane_blocks = C_WIDTH // LANES
    tiles = [(j, lb) for j in range(rows // ATT_Q) for lb in range(lane_blocks)]
    ones = jnp.ones((ONES_ROWS, vt_ref.shape[1]), _BF16)

    def scores(i):
        j, lb = tiles[i]
        sl = slice(lb * LANES, (lb + 1) * LANES)
        q = q_ref[j * ATT_Q:(j + 1) * ATT_Q, sl]
        q2 = jnp.concatenate([jnp.where(low, q, jnp.zeros_like(q)), jnp.where(low, jnp.zeros_like(q), q)], axis=0)
        st_ref[i % (ATT_AHEAD + 1)] = _dot_nt(k_ref[:, sl], q2)

    def finish(i):
        j, lb = tiles[i]
        sl = slice(lb * LANES, (lb + 1) * LANES)
        st = st_ref[i % (ATT_AHEAD + 1)]
        p = jnp.exp2(st - jnp.max(st, axis=0, keepdims=True))
        ot = _dot(jnp.concatenate([vt_ref[sl, :], ones], axis=0), p.astype(_BF16))
        inv = 1.0 / ot[LANES:LANES + 1, :]
        out_t = jnp.concatenate([ot[0:HEAD_DIM, 0:ATT_Q] * inv[:, 0:ATT_Q],
                                 ot[HEAD_DIM:LANES, ATT_Q:] * inv[:, ATT_Q:]], axis=0)
        o_ref[j * ATT_Q:(j + 1) * ATT_Q, sl] = out_t.T.astype(o_ref.dtype)

    for i in range(min(ATT_AHEAD, len(tiles))):
        scores(i)
    for i in range(len(tiles)):
        if i + ATT_AHEAD < len(tiles):
            scores(i + ATT_AHEAD)
        finish(i)


def _cross_attention(qc, kc, vtc):
    b, s, _ = qc.shape
    m = kc.shape[1]
    rows = min(s, 4096)
    return pl.pallas_call(
        functools.partial(_cross_kernel, rows=rows),
        grid=(b, s // rows),
        in_specs=[pl.BlockSpec((None, rows, C_WIDTH), lambda i, j: (i, j, 0)),
                  pl.BlockSpec((None, m, C_WIDTH), lambda i, j: (i, 0, 0)),
                  pl.BlockSpec((None, C_WIDTH, m), lambda i, j: (i, 0, 0))],
        out_specs=pl.BlockSpec((None, rows, C_WIDTH), lambda i, j: (i, j, 0)),
        out_shape=jax.ShapeDtypeStruct((b, s, C_WIDTH), _BF16),
        scratch_shapes=[pltpu.VMEM((ATT_AHEAD + 1, m, 2 * ATT_Q), _F32)],
        compiler_params=_cparams(2),
        name="cross_c",
    )(qc, kc, vtc)


def _out_router_kernel(x_ref, a_ref, b_ref, c_ref, wo_ref, g_ref, wr_ref, br_ref,
                       h_ref, hn_ref, rows_ref, cnt_ref, carry_ref):
    @pl.when(pl.program_id(0) == 0)
    def _():
        carry_ref[...] = jnp.zeros_like(carry_ref)

    tm = x_ref.shape[0]
    h = x_ref[...]
    h = h + _dot(a_ref[...], wo_ref[0:A_WIDTH, :])
    h = h + _dot(b_ref[...], wo_ref[A_WIDTH:A_WIDTH + B_WIDTH, :])
    h = h + _dot(c_ref[...], wo_ref[A_WIDTH + B_WIDTH:, :])
    h_ref[...] = h
    ms = jnp.mean(h * h, axis=-1, keepdims=True)
    hn = (h * lax.rsqrt(ms + EPS)) * g_ref[...]
    _pack_rows(hn_ref, hn)
    logits = _dot_nt(wr_ref[...], hn.astype(_BF16))[0:ROUTE_ROWS, :] + br_ref[:, 0:1]
    row = lax.broadcasted_iota(jnp.int32, (ROUTE_ROWS, tm), 0).astype(_F32)
    big = float(ROUTE_ROWS)

    def first_row(mask):
        return jnp.min(jnp.where(mask, row, big), axis=0, keepdims=True)

    gmask = row < N_GROUPS
    gl = jnp.where(gmask, logits, NEG_INF)
    ge = jnp.exp(gl - jnp.max(gl, axis=0, keepdims=True))
    gp = ge / jnp.sum(ge, axis=0, keepdims=True)
    p_group = jnp.max(gp, axis=0, keepdims=True)
    g_sel = first_row(gmask & (gp == p_group))
    lo = ROUTE_LANE0 + g_sel * EXPERTS_PER_GROUP
    emask = (row >= lo) & (row < lo + EXPERTS_PER_GROUP)
    el = jnp.where(emask, logits, NEG_INF)
    ee = jnp.exp(el - jnp.max(el, axis=0, keepdims=True))
    ep = ee / jnp.sum(ee, axis=0, keepdims=True)
    p1 = jnp.max(ep, axis=0, keepdims=True)
    i1 = first_row(emask & (ep == p1))
    ep2 = jnp.where(emask & (row != i1), ep, -1.0)
    p2 = jnp.max(ep2, axis=0, keepdims=True)
    i2 = first_row(ep2 == p2)
    den = p1 + p2
    w1 = p_group * (p1 / den)
    w2 = p_group * (p2 / den)
    hit1 = row == i1
    hit2 = row == i2
    onehot = jnp.where(hit1 | hit2, 1.0, 0.0)
    r_i = lax.broadcasted_iota(jnp.int32, (tm, tm), 0)
    c_i = lax.broadcasted_iota(jnp.int32, (tm, tm), 1)
    earlier = jnp.where(r_i < c_i, 1.0, 0.0).astype(_BF16)
    before = _dot(onehot.astype(_BF16), earlier) + carry_ref[:, 0:1]
    r1 = jnp.sum(jnp.where(hit1, before, 0.0), axis=0, keepdims=True)
    r2 = jnp.sum(jnp.where(hit2, before, 0.0), axis=0, keepdims=True)
    carry_ref[...] = carry_ref[...] + jnp.sum(onehot, axis=1, keepdims=True)
    cnt_ref[...] = carry_ref[...]
    out_row = lax.broadcasted_iota(jnp.int32, (SUBLANES, tm), 0)
    info = jnp.where(out_row == 0, w1, 0.0)
    info = jnp.where(out_row == 1, w2, info)
    info = jnp.where(out_row == 2, i1 - ROUTE_LANE0, info)
    info = jnp.where(out_row == 3, i2 - ROUTE_LANE0, info)
    info = jnp.where(out_row == 4, r1, info)
    info = jnp.where(out_row == 5, r2, info)
    rows_ref[...] = info


def _out_router(x2, oa, ob, oc, w_out, ffn_g, w_rg, b_rg, w_re, b_re):
    t, d = x2.shape
    tm = min(t, 512)
    pad = LANES - N_GROUPS - N_EXPERTS
    wr = jnp.concatenate([w_rg, w_re, jnp.zeros((d, pad), _F32)], axis=1).T.astype(_BF16)
    br = jnp.concatenate([b_rg, b_re, jnp.zeros((ROUTE_ROWS - N_GROUPS - N_EXPERTS,), _F32)])
    br = jnp.broadcast_to(br[:, None], (ROUTE_ROWS, LANES))

    def rows(w):
        return pl.BlockSpec((tm, w), lambda i: (i, 0))

    def whole(r, c):
        return pl.BlockSpec((r, c), lambda i: (0, 0))

    return pl.pallas_call(
        _out_router_kernel,
        grid=(t // tm,),
        in_specs=[rows(d), rows(A_WIDTH), rows(B_WIDTH), rows(C_WIDTH), whole(d, d), whole(1, d),
                  whole(LANES, d), whole(ROUTE_ROWS, LANES)],
        out_specs=[rows(d), pl.BlockSpec((tm * PACK_ROWS, LANES), lambda i: (i, 0)),
                   pl.BlockSpec((SUBLANES, tm), lambda i: (0, i)), whole(ROUTE_ROWS, LANES)],
        out_shape=[jax.ShapeDtypeStruct((t, d), _F32), jax.ShapeDtypeStruct((t * PACK_ROWS, LANES), jnp.uint32),
                   jax.ShapeDtypeStruct((SUBLANES, t), _F32), jax.ShapeDtypeStruct((ROUTE_ROWS, LANES), _F32)],
        scratch_shapes=[pltpu.VMEM((ROUTE_ROWS, LANES), _F32)],
        compiler_params=_cparams(1),
        name="out_router",
    )(x2, oa, ob, oc, w_out.astype(_BF16), ffn_g.reshape(1, d), wr, br)


DISPATCH_TOKENS = 2048
COMBINE_TOKENS = 256


ROW_UNROLL = 8


def _tile_rows(row, count=1, per=SUBLANES):
    start = row * per
    if not isinstance(start, int):
        start = pl.multiple_of(start, per)
    return pl.ds(start, count * per)


def _row_copy(src, s_row, dst, d_row, sem, per=SUBLANES):
    return pltpu.make_async_copy(src.at[_tile_rows(s_row, 1, per)], dst.at[_tile_rows(d_row, 1, per)], sem)


def _dispatch_kernel(pad_start_ref, pad_len_ref, used_ref, dest_ref, hn_ref, xs_ref, zero_ref, sem,
                     pad_sem):
    per = PACK_ROWS
    n = hn_ref.shape[0] // per

    @pl.when(pl.program_id(0) == 0)
    def _():
        zero_ref[...] = jnp.zeros_like(zero_ref)
        n_blocks = xs_ref.shape[0] // (ROW_BLOCK * per)

        def block_copy(blk):
            return pltpu.make_async_copy(zero_ref, xs_ref.at[_tile_rows(blk * ROW_BLOCK, ROW_BLOCK, per)],
                                         pad_sem)

        def put_block(blk, carry):
            block_copy(blk).start()
            return carry

        def done_block(blk, carry):
            block_copy(blk).wait()
            return carry

        lax.fori_loop(used_ref[0], n_blocks, put_block, 0)
        lax.fori_loop(used_ref[0], n_blocks, done_block, 0)
        bits = [1 << k for k in reversed(range(ROW_BLOCK.bit_length() - 1))]

        def tail(e, wait):
            row = pad_start_ref[e]
            for bit in bits:
                on = (pad_len_ref[e] & bit) != 0
                copy = pltpu.make_async_copy(zero_ref.at[_tile_rows(0, bit, per)],
                                             xs_ref.at[_tile_rows(row, bit, per)], pad_sem)

                @pl.when(on)
                def _():
                    copy.wait() if wait else copy.start()

                row = row + jnp.where(on, bit, 0)

        def put_tail(e, carry):
            tail(e, False)
            return carry

        def done_tail(e, carry):
            tail(e, True)
            return carry

        lax.fori_loop(0, N_EXPERTS, put_tail, 0)
        lax.fori_loop(0, N_EXPERTS, done_tail, 0)

    def issue(i, carry):
        for u in range(ROW_UNROLL):
            t = i * ROW_UNROLL + u
            _row_copy(hn_ref, t, xs_ref, dest_ref[2 * t], sem, per).start(priority=0)
            _row_copy(hn_ref, t, xs_ref, dest_ref[2 * t + 1], sem, per).start(priority=1)
        return carry

    lax.fori_loop(0, n // ROW_UNROLL, issue, 0)
    for _ in range(2):
        pltpu.make_async_copy(hn_ref, xs_ref.at[_tile_rows(0, n, per)], sem).wait()


def _dispatch(hn, dest, pad_start, pad_len, n_used, n_rows):
    t = hn.shape[0] // PACK_ROWS
    n = min(t, DISPATCH_TOKENS)
    return pl.pallas_call(
        _dispatch_kernel,
        grid_spec=pltpu.PrefetchScalarGridSpec(
            num_scalar_prefetch=3,
            grid=(t // n,),
            in_specs=[pl.BlockSpec((2 * n,), lambda i, *_: (i,), memory_space=pltpu.SMEM),
                      pl.BlockSpec((n * PACK_ROWS, LANES), lambda i, *_: (i, 0))],
            out_specs=pl.BlockSpec(memory_space=pl.ANY),
            scratch_shapes=[pltpu.VMEM((ROW_BLOCK * PACK_ROWS, LANES), hn.dtype), pltpu.SemaphoreType.DMA,
                            pltpu.SemaphoreType.DMA]),
        out_shape=jax.ShapeDtypeStruct((n_rows * PACK_ROWS, LANES), hn.dtype),
        compiler_params=_cparams(1),
        name="moe_dispatch",
    )(pad_start, pad_len, n_used, dest, hn)


def _expert_kernel(be_ref, run_ref, next_ref, used_ref, x_ref, wg_hbm, wu_hbm, wd_hbm, y_ref,
                   wg_f32, wu_f32, wd_f32, wg_bf, wu_bf, wd_bf, sem):
    i = pl.program_id(0)
    live = i < used_ref[0]
    new_expert = (i == 0) | (be_ref[i] != be_ref[jnp.maximum(i - 1, 0)])
    slot = run_ref[i] % 2

    def fetch(expert, to_slot):
        return [pltpu.make_async_copy(src.at[expert], dst.at[to_slot], sem.at[to_slot, k])
                for k, (src, dst) in enumerate(((wg_hbm, wg_f32), (wu_hbm, wu_f32), (wd_hbm, wd_f32)))]

    @pl.when(live & (i == 0))
    def _():
        for copy in fetch(be_ref[0], 0):
            copy.start()

    @pl.when(live & new_expert)
    def _():
        for copy in fetch(be_ref[i], slot):
            copy.wait()

        @pl.when(next_ref[i] >= 0)
        def _():
            for copy in fetch(next_ref[i], 1 - slot):
                copy.start()

        wg_bf[...] = wg_f32[slot].astype(_BF16)
        wu_bf[...] = wu_f32[slot].astype(_BF16)
        wd_bf[...] = wd_f32[slot].astype(_BF16)

    @pl.when(live)
    def _():
        sub = ROW_BLOCK // EXPERT_SPLIT
        gate_up = {}

        def first(k):
            x = _unpack_rows(x_ref, sub, k * sub)
            gate_up[k] = (_dot(x, wg_bf[...]), _dot(x, wu_bf[...]))

        def second(k):
            gate, up = gate_up.pop(k)
            act = (gate * jax.nn.sigmoid(gate)) * up
            _rows_to_tiles(y_ref, _dot(act.astype(_BF16), wd_bf[...]), k * sub)

        for k in range(min(EXPERT_AHEAD, EXPERT_SPLIT)):
            first(k)
        for k in range(EXPERT_SPLIT):
            if k + EXPERT_AHEAD < EXPERT_SPLIT:
                first(k + EXPERT_AHEAD)
            second(k)

    @pl.when(i >= used_ref[0])
    def _():
        y_ref[...] = jnp.zeros_like(y_ref)


def _experts(xs, blocks, w_gate, w_up, w_down):
    n_rows, d = xs.shape[0] // PACK_ROWS, D_MODEL
    n_blocks = n_rows // ROW_BLOCK
    tile_block = (ROW_BLOCK * SUBLANES, LANES)
    hbm = pl.BlockSpec(memory_space=pl.ANY)

    return pl.pallas_call(
        _expert_kernel,
        grid_spec=pltpu.PrefetchScalarGridSpec(
            num_scalar_prefetch=4,
            grid=(n_blocks,),
            in_specs=[pl.BlockSpec((ROW_BLOCK * PACK_ROWS, LANES),
                                   lambda i, be, run, nxt, used: (jnp.minimum(i, used[0] - 1), 0)),
                      hbm, hbm, hbm],
            out_specs=pl.BlockSpec(tile_block, lambda i, *_: (i, 0)),
            scratch_shapes=[pltpu.VMEM((2, d, D_EXPERT), _F32), pltpu.VMEM((2, d, D_EXPERT), _F32),
                            pltpu.VMEM((2, D_EXPERT, d), _F32),
                            pltpu.VMEM((d, D_EXPERT), _BF16), pltpu.VMEM((d, D_EXPERT), _BF16),
                            pltpu.VMEM((D_EXPERT, d), _BF16), pltpu.SemaphoreType.DMA((2, 3))]),
        out_shape=jax.ShapeDtypeStruct((n_rows * SUBLANES, LANES), _F32),
        compiler_params=_cparams(1),
        name="moe_experts",
    )(*blocks, xs, w_gate, w_up, w_down)


def _combine_kernel(dest_ref, next_ref, h_ref, rows_ref, ys_ref, o_ref, buf_ref, sem):
    n = h_ref.shape[0]
    step = pl.program_id(0)
    slot = step % 2

    def gather(idx_ref, to_slot):
        def issue(i, carry):
            for u in range(ROW_UNROLL):
                t = i * ROW_UNROLL + u
                _row_copy(ys_ref, idx_ref[2 * t], buf_ref.at[to_slot, 0], t,
                          sem.at[to_slot]).start(priority=0)
                _row_copy(ys_ref, idx_ref[2 * t + 1], buf_ref.at[to_slot, 1], t,
                          sem.at[to_slot]).start(priority=1)
            return carry

        lax.fori_loop(0, n // ROW_UNROLL, issue, 0)

    @pl.when(step == 0)
    def _():
        gather(dest_ref, 0)

    @pl.when(step + 1 < pl.num_programs(0))
    def _():
        gather(next_ref, 1 - slot)

    for k in range(2):
        pltpu.make_async_copy(ys_ref.at[_tile_rows(0, n)], buf_ref.at[slot, k], sem.at[slot]).wait()
    info = jnp.concatenate([rows_ref[...], jnp.zeros((LANES - SUBLANES, n), _F32)], axis=0).T
    w0 = info[:, 0:1]
    w1 = info[:, 1:2]
    for s in range(SUBLANES):
        sl = slice(s * LANES, (s + 1) * LANES)
        moe = w0 * _tile_block(buf_ref.at[slot, 0], s, n) + w1 * _tile_block(buf_ref.at[slot, 1], s, n)
        o_ref[:, sl] = h_ref[:, sl] + moe


def _combine(h, route_rows, ys, dest):
    t, d = h.shape
    n = min(t, COMBINE_TOKENS)
    steps = t // n
    return pl.pallas_call(
        _combine_kernel,
        grid=(steps,),
        in_specs=[pl.BlockSpec((2 * n,), lambda i: (i,), memory_space=pltpu.SMEM),
                  pl.BlockSpec((2 * n,), lambda i: (jnp.minimum(i + 1, steps - 1),),
                               memory_space=pltpu.SMEM),
                  pl.BlockSpec((n, d), lambda i: (i, 0)),
                  pl.BlockSpec((SUBLANES, n), lambda i: (0, i)),
                  pl.BlockSpec(memory_space=pl.ANY)],
        out_specs=pl.BlockSpec((n, d), lambda i: (i, 0)),
        out_shape=jax.ShapeDtypeStruct((t, d), _F32),
        scratch_shapes=[pltpu.VMEM((2, 2, n * SUBLANES, LANES), _F32), pltpu.SemaphoreType.DMA((2,))],
        compiler_params=_cparams(1),
        name="moe_combine",
    )(dest, dest, h, route_rows, ys)


def _moe_layout(route_rows, counts, t):
    counts = counts[ROUTE_LANE0:ROUTE_LANE0 + N_EXPERTS, 0].astype(jnp.int32)
    padded = (counts + ROW_BLOCK - 1) // ROW_BLOCK * ROW_BLOCK
    pends = jnp.cumsum(padded)
    pstarts = pends - padded
    eid = route_rows[2:4].astype(jnp.int32)
    rank = route_rows[4:6].astype(jnp.int32)
    experts = jnp.arange(N_EXPERTS, dtype=jnp.int32)
    start_of = jnp.sum(jnp.where(eid[:, :, None] == experts, pstarts, 0), axis=-1)
    dest = (start_of + rank).T.reshape(-1)
    n_blocks = -(-2 * t // ROW_BLOCK) + N_EXPERTS
    first_row = jnp.arange(n_blocks, dtype=jnp.int32) * ROW_BLOCK
    block_e = jnp.minimum(jnp.sum((pends[None, :] <= first_row[:, None]).astype(jnp.int32), axis=1),
                          N_EXPERTS - 1)
    n_used = (pends[-1:] // ROW_BLOCK).astype(jnp.int32)
    changed = jnp.concatenate([jnp.zeros((1,), jnp.int32), (block_e[1:] != block_e[:-1]).astype(jnp.int32)])
    block_run = jnp.cumsum(changed)
    later = (counts[None, :] > 0) & (experts[None, :] > experts[:, None])
    next_expert = jnp.min(jnp.where(later, experts[None, :], N_EXPERTS), axis=1)
    next_expert = jnp.where(next_expert < N_EXPERTS, next_expert, -1)
    block_next = jnp.sum(jnp.where(block_e[:, None] == experts[None, :], next_expert[None, :], 0), axis=1)
    blocks = (block_e, block_run.astype(jnp.int32), block_next.astype(jnp.int32), n_used)
    return dest, blocks, pstarts + counts, padded - counts, n_blocks * ROW_BLOCK


def kernel(x, mem, positions, mix_norm_g, w_in, qn_a, kn_a, rel_bias, ret_gn_g, mem_norm_g, w_mem_kv,
           qn_c, kn_c, w_out, ffn_norm_g, w_router_group, b_router_group, w_router_expert,
           b_router_expert, w_gate, w_up, w_down):
    b, s, d = x.shape
    t = b * s
    x2 = x.reshape(t, d)
    kc, vc = _mem_kv(mem, mem_norm_g, w_mem_kv, kn_c)
    tables = _retention_tables()
    qa, ka, vta, qkzv, gate, qc = _in_proj(x, positions, tables[1], mix_norm_g, w_in, qn_a, kn_a, qn_c)
    out_a = _attention(qa, ka, vta, rel_bias)
    out_b = _retention(qkzv, gate, tables, ret_gn_g)
    out_c = _cross_attention(qc, kc, vc)
    h, hn, route_rows, counts = _out_router(
        x2, out_a.reshape(t, A_WIDTH), out_b.reshape(t, B_WIDTH), out_c.reshape(t, C_WIDTH),
        w_out, ffn_norm_g, w_router_group, b_router_group, w_router_expert, b_router_expert)
    dest, blocks, pad_start, pad_len, n_rows = _moe_layout(route_rows, counts, t)
    xs = _dispatch(hn, dest, pad_start, pad_len, blocks[-1], n_rows)
    ys = _experts(xs, blocks, w_gate, w_up, w_down)
    return _combine(h, route_rows, ys, dest).reshape(b, s, d)
```
